```python
import jax, jax.numpy as jnp
from jax import lax
import numpy as np

D_MODEL = 1024
BATCH = 8
SEQ = 4096
DEPTH = 2

D_MIX = 512
GMLP_GROUPS = 8
GMLP_CHUNK = 128
GMLP_GROUP_DIM = D_MIX // GMLP_GROUPS
CONV_WIDTH = 31
FOX_HEADS = 8
FOX_HEAD_DIM = D_MIX // FOX_HEADS
QBLK = 128
N_BRANCH = 3
D_FF = 4 * D_MODEL
N_MOD = 6
NORM_EPS = 1e-6

COL_GMLP = 2 * D_MIX
COL_CONV = 2 * D_MIX
COL_FOX = 3 * D_MIX + FOX_HEADS
COL_GATE = N_BRANCH * D_MODEL
D_IN = COL_GMLP + COL_CONV + COL_FOX + COL_GATE
SPLIT_IDX = (COL_GMLP,
             COL_GMLP + COL_CONV,
             COL_GMLP + COL_CONV + D_MIX,
             COL_GMLP + COL_CONV + 2 * D_MIX,
             COL_GMLP + COL_CONV + 3 * D_MIX,
             COL_GMLP + COL_CONV + 3 * D_MIX + FOX_HEADS)

kernel_name = "hybrid_gmlp_conformer_fox_sandwich_adaln"


def rms_norm(x, g):
    xf = x.astype(jnp.float32)
    y = xf * lax.rsqrt(jnp.mean(xf * xf, axis=-1, keepdims=True) + NORM_EPS)
    return (y * g.astype(jnp.float32)).astype(x.dtype)


def layer_norm(x, g, b):
    xf = x.astype(jnp.float32)
    mu = jnp.mean(xf, axis=-1, keepdims=True)
    var = jnp.mean(jnp.square(xf - mu), axis=-1, keepdims=True)
    y = (xf - mu) * lax.rsqrt(var + NORM_EPS)
    return (y * g.astype(jnp.float32) + b.astype(jnp.float32)).astype(x.dtype)


def gmlp_spatial(v, ws, bs):
    b, s, _ = v.shape
    nc = s // GMLP_CHUNK
    vc = v.reshape(b, nc, GMLP_CHUNK, GMLP_GROUPS, GMLP_GROUP_DIM)
    w = ws * jnp.tril(jnp.ones((GMLP_CHUNK, GMLP_CHUNK), ws.dtype))
    sv = jnp.einsum('gts,bnsgc->bntgc', w, vc) + bs.T[None, None, :, :, None]
    return sv.reshape(b, s, D_MIX)


def causal_depthwise_conv(z, w, bias):
    out = lax.conv_general_dilated(
        z, w[:, None, :], window_strides=(1,), padding=[(CONV_WIDTH - 1, 0)],
        dimension_numbers=('NWC', 'WIO', 'NWC'), feature_group_count=D_MIX)
    return out + bias


def fox_attention(q, k, v, log_f):
    b, h, s, dh = q.shape
    nb = s // QBLK
    cum = jnp.cumsum(log_f.astype(jnp.float32), axis=-1)
    qb = jnp.moveaxis(q.reshape(b, h, nb, QBLK, dh), 2, 0)
    cq = jnp.moveaxis(cum.reshape(b, h, nb, QBLK), 2, 0)
    kpos = jnp.arange(s)
    scale = 1.0 / float(np.sqrt(dh))

    def one_block(args):
        qi, ci, i = args
        logits = jnp.einsum('bhqd,bhkd->bhqk', qi, k).astype(jnp.float32) * scale
        logits = logits + ci[..., None] - cum[:, :, None, :]
        qpos = i * QBLK + jnp.arange(QBLK)
        causal = kpos[None, :] <= qpos[:, None]
        logits = jnp.where(causal, logits, -jnp.inf)
        p = jax.nn.softmax(logits, axis=-1)
        return jnp.einsum('bhqk,bhkd->bhqd', p.astype(v.dtype), v)

    out = lax.map(one_block, (qb, cq, jnp.arange(nb)))
    return out.transpose(1, 0, 3, 2, 4).reshape(b, s, h * dh)


def hybrid_mixer(h, w_in, gmlp_ln_g, gmlp_ln_b, gmlp_ws, gmlp_bs, w_a_out,
                 conv_w, conv_b, conv_ln_g, conv_ln_b, w_b_out, fox_bf, w_c_out, w_out):
    b, s, _ = h.shape
    proj = h @ w_in
    uv_a, glu_b, q, k, v, f_raw, gate_raw = jnp.split(proj, SPLIT_IDX, axis=-1)

    u_a, v_a = jnp.split(jax.nn.gelu(uv_a), 2, axis=-1)
    v_a = layer_norm(v_a, gmlp_ln_g, gmlp_ln_b)
    y_a = (u_a * gmlp_spatial(v_a, gmlp_ws, gmlp_bs)) @ w_a_out

    val_b, gat_b = jnp.split(glu_b, 2, axis=-1)
    z = val_b * jax.nn.sigmoid(gat_b)
    z = causal_depthwise_conv(z, conv_w, conv_b)
    z = jax.nn.silu(layer_norm(z, conv_ln_g, conv_ln_b))
    y_b = z @ w_b_out

    def heads(t):
        return t.reshape(b, s, FOX_HEADS, FOX_HEAD_DIM).transpose(0, 2, 1, 3)
    log_f = jax.nn.log_sigmoid((f_raw + fox_bf).astype(jnp.float32)).transpose(0, 2, 1)
    y_c = fox_attention(heads(q), heads(k), heads(v), log_f) @ w_c_out

    g = jax.nn.sigmoid(gate_raw).reshape(b, s, N_BRANCH, D_MODEL)
    merged = g[:, :, 0] * y_a + g[:, :, 1] * y_b + g[:, :, 2] * y_c
    return merged @ w_out


def _fwd_setup_inputs(seed: int = 0) -> dict:
    key = jax.random.key(seed)
    ks = jax.random.split(key, 32)
    f32 = jnp.float32

    def nrm(k, shape, scale):
        return jax.random.normal(k, shape, f32) * scale

    L = DEPTH
    return {
        "x": nrm(ks[0], (BATCH, SEQ, D_MODEL), 1.0),
        "c": nrm(ks[1], (BATCH, D_MODEL), 1.0),
        "ada_w": nrm(ks[2], (L, D_MODEL, N_MOD * D_MODEL), 0.5 * D_MODEL ** -0.5),
        "ada_b": nrm(ks[3], (L, N_MOD * D_MODEL), 0.02),
        "mix_pre_g": 1.0 + nrm(ks[4], (L, D_MODEL), 0.02),
        "mix_post_g": 1.0 + nrm(ks[5], (L, D_MODEL), 0.02),
        "mlp_pre_g": 1.0 + nrm(ks[6], (L, D_MODEL), 0.02),
        "mlp_post_g": 1.0 + nrm(ks[7], (L, D_MODEL), 0.02),
        "w_in": nrm(ks[8], (L, D_MODEL, D_IN), D_MODEL ** -0.5),
        "gmlp_ln_g": 1.0 + nrm(ks[9], (L, D_MIX), 0.02),
        "gmlp_ln_b": nrm(ks[10], (L, D_MIX), 0.02),
        "gmlp_ws": nrm(ks[11], (L, GMLP_GROUPS, GMLP_CHUNK, GMLP_CHUNK), GMLP_CHUNK ** -0.5),
        "gmlp_bs": 1.0 + nrm(ks[12], (L, GMLP_GROUPS, GMLP_CHUNK), 0.02),
        "w_a_out": nrm(ks[13], (L, D_MIX, D_MODEL), D_MIX ** -0.5),
        "conv_w": nrm(ks[14], (L, CONV_WIDTH, D_MIX), CONV_WIDTH ** -0.5),
        "conv_b": nrm(ks[15], (L, D_MIX), 0.02),
        "conv_ln_g": 1.0 + nrm(ks[16], (L, D_MIX), 0.02),
        "conv_ln_b": nrm(ks[17], (L, D_MIX), 0.02),
        "w_b_out": nrm(ks[18], (L, D_MIX, D_MODEL), D_MIX ** -0.5),
        "fox_bf": jnp.linspace(1.0, 6.0, FOX_HEADS, dtype=f32)[None, :] + nrm(ks[19], (L, FOX_HEADS), 0.1),
        "w_c_out": nrm(ks[20], (L, D_MIX, D_MODEL), D_MIX ** -0.5),
        "w_out": nrm(ks[21], (L, D_MODEL, D_MODEL), D_MODEL ** -0.5),
        "mlp_w1": nrm(ks[22], (L, D_MODEL, D_FF), D_MODEL ** -0.5),
        "mlp_w2": nrm(ks[23], (L, D_FF, D_MODEL), D_FF ** -0.5),
    }


def _fwd_reference(x, c, ada_w, ada_b, mix_pre_g, mix_post_g, mlp_pre_g, mlp_post_g, w_in,
              gmlp_ln_g, gmlp_ln_b, gmlp_ws, gmlp_bs, w_a_out, conv_w, conv_b, conv_ln_g,
              conv_ln_b, w_b_out, fox_bf, w_c_out, w_out, mlp_w1, mlp_w2):
    c_act = jax.nn.silu(c)
    for l in range(DEPTH):
        mod = (c_act @ ada_w[l] + ada_b[l])[:, None, :]
        sh1, sc1, gt1, sh2, sc2, gt2 = jnp.split(mod, N_MOD, axis=-1)

        h = rms_norm(x, mix_pre_g[l]) * (1.0 + sc1) + sh1
        y = hybrid_mixer(h, w_in[l], gmlp_ln_g[l], gmlp_ln_b[l], gmlp_ws[l], gmlp_bs[l],
                         w_a_out[l], conv_w[l], conv_b[l], conv_ln_g[l], conv_ln_b[l],
                         w_b_out[l], fox_bf[l], w_c_out[l], w_out[l])
        x = x + gt1 * rms_norm(y, mix_post_g[l])

        h = rms_norm(x, mlp_pre_g[l]) * (1.0 + sc2) + sh2
        y = jnp.square(jax.nn.relu(h @ mlp_w1[l])) @ mlp_w2[l]
        x = x + gt2 * rms_norm(y, mlp_post_g[l])
    return x


import jax as _jax
import jax.numpy as _jnp

TWIN_FORMAT = 'train_step'
FWD_PARAMS = ['x', 'c', 'ada_w', 'ada_b', 'mix_pre_g', 'mix_post_g', 'mlp_pre_g', 'mlp_post_g', 'w_in', 'gmlp_ln_g', 'gmlp_ln_b', 'gmlp_ws', 'gmlp_bs', 'w_a_out', 'conv_w', 'conv_b', 'conv_ln_g', 'conv_ln_b', 'w_b_out', 'fox_bf', 'w_c_out', 'w_out', 'mlp_w1', 'mlp_w2']
TWIN_WEIGHTS = ['ada_w', 'ada_b', 'mix_pre_g', 'mix_post_g', 'mlp_pre_g', 'mlp_post_g', 'w_in', 'gmlp_ln_g', 'gmlp_ln_b', 'gmlp_ws', 'gmlp_bs', 'w_a_out', 'conv_w', 'conv_b', 'conv_ln_g', 'conv_ln_b', 'w_b_out', 'fox_bf', 'w_c_out', 'w_out', 'mlp_w1', 'mlp_w2']
TWIN_DIFF_INPUT = 'x'
TWIN_INPUTS = ['x', 'c', 'ada_w', 'ada_b', 'mix_pre_g', 'mix_post_g', 'mlp_pre_g', 'mlp_post_g', 'w_in', 'gmlp_ln_g', 'gmlp_ln_b', 'gmlp_ws', 'gmlp_bs', 'w_a_out', 'conv_w', 'conv_b', 'conv_ln_g', 'conv_ln_b', 'w_b_out', 'fox_bf', 'w_c_out', 'w_out', 'mlp_w1', 'mlp_w2', 'loss_target', 'm_ada_w', 'm_ada_b', 'm_mix_pre_g', 'm_mix_post_g', 'm_mlp_pre_g', 'm_mlp_post_g', 'm_w_in', 'm_gmlp_ln_g', 'm_gmlp_ln_b', 'm_gmlp_ws', 'm_gmlp_bs', 'm_w_a_out', 'm_conv_w', 'm_conv_b', 'm_conv_ln_g', 'm_conv_ln_b', 'm_w_b_out', 'm_fox_bf', 'm_w_c_out', 'm_w_out', 'm_mlp_w1', 'm_mlp_w2', 'v_ada_w', 'v_ada_b', 'v_mix_pre_g', 'v_mix_post_g', 'v_mlp_pre_g', 'v_mlp_post_g', 'v_w_in', 'v_gmlp_ln_g', 'v_gmlp_ln_b', 'v_gmlp_ws', 'v_gmlp_bs', 'v_w_a_out', 'v_conv_w', 'v_conv_b', 'v_conv_ln_g', 'v_conv_ln_b', 'v_w_b_out', 'v_fox_bf', 'v_w_c_out', 'v_w_out', 'v_mlp_w1', 'v_mlp_w2']
TWIN_OUTPUTS = ['loss', 'grad_x', 'grad_ada_w', 'grad_ada_b', 'grad_mix_pre_g', 'grad_mix_post_g', 'grad_mlp_pre_g', 'grad_mlp_post_g', 'grad_w_in', 'grad_gmlp_ln_g', 'grad_gmlp_ln_b', 'grad_gmlp_ws', 'grad_gmlp_bs', 'grad_w_a_out', 'grad_conv_w', 'grad_conv_b', 'grad_conv_ln_g', 'grad_conv_ln_b', 'grad_w_b_out', 'grad_fox_bf', 'grad_w_c_out', 'grad_w_out', 'grad_mlp_w1', 'grad_mlp_w2', 'delta_ada_w', 'delta_ada_b', 'delta_mix_pre_g', 'delta_mix_post_g', 'delta_mlp_pre_g', 'delta_mlp_post_g', 'delta_w_in', 'delta_gmlp_ln_g', 'delta_gmlp_ln_b', 'delta_gmlp_ws', 'delta_gmlp_bs', 'delta_w_a_out', 'delta_conv_w', 'delta_conv_b', 'delta_conv_ln_g', 'delta_conv_ln_b', 'delta_w_b_out', 'delta_fox_bf', 'delta_w_c_out', 'delta_w_out', 'delta_mlp_w1', 'delta_mlp_w2', 'new_m_ada_w', 'new_m_ada_b', 'new_m_mix_pre_g', 'new_m_mix_post_g', 'new_m_mlp_pre_g', 'new_m_mlp_post_g', 'new_m_w_in', 'new_m_gmlp_ln_g', 'new_m_gmlp_ln_b', 'new_m_gmlp_ws', 'new_m_gmlp_bs', 'new_m_w_a_out', 'new_m_conv_w', 'new_m_conv_b', 'new_m_conv_ln_g', 'new_m_conv_ln_b', 'new_m_w_b_out', 'new_m_fox_bf', 'new_m_w_c_out', 'new_m_w_out', 'new_m_mlp_w1', 'new_m_mlp_w2', 'new_v_ada_w', 'new_v_ada_b', 'new_v_mix_pre_g', 'new_v_mix_post_g', 'new_v_mlp_pre_g', 'new_v_mlp_post_g', 'new_v_w_in', 'new_v_gmlp_ln_g', 'new_v_gmlp_ln_b', 'new_v_gmlp_ws', 'new_v_gmlp_bs', 'new_v_w_a_out', 'new_v_conv_w', 'new_v_conv_b', 'new_v_conv_ln_g', 'new_v_conv_ln_b', 'new_v_w_b_out', 'new_v_fox_bf', 'new_v_w_c_out', 'new_v_w_out', 'new_v_mlp_w1', 'new_v_mlp_w2']
TWIN_LEAF_KINDS = {'loss': 'loss', 'grad_x': 'grad_x', 'grad_ada_w': 'grad_w', 'grad_ada_b': 'grad_w', 'grad_mix_pre_g': 'grad_w', 'grad_mix_post_g': 'grad_w', 'grad_mlp_pre_g': 'grad_w', 'grad_mlp_post_g': 'grad_w', 'grad_w_in': 'grad_w', 'grad_gmlp_ln_g': 'grad_w', 'grad_gmlp_ln_b': 'grad_w', 'grad_gmlp_ws': 'grad_w', 'grad_gmlp_bs': 'grad_w', 'grad_w_a_out': 'grad_w', 'grad_conv_w': 'grad_w', 'grad_conv_b': 'grad_w', 'grad_conv_ln_g': 'grad_w', 'grad_conv_ln_b': 'grad_w', 'grad_w_b_out': 'grad_w', 'grad_fox_bf': 'grad_w', 'grad_w_c_out': 'grad_w', 'grad_w_out': 'grad_w', 'grad_mlp_w1': 'grad_w', 'grad_mlp_w2': 'grad_w', 'delta_ada_w': 'delta_w', 'delta_ada_b': 'delta_w', 'delta_mix_pre_g': 'delta_w', 'delta_mix_post_g': 'delta_w', 'delta_mlp_pre_g': 'delta_w', 'delta_mlp_post_g': 'delta_w', 'delta_w_in': 'delta_w', 'delta_gmlp_ln_g': 'delta_w', 'delta_gmlp_ln_b': 'delta_w', 'delta_gmlp_ws': 'delta_w', 'delta_gmlp_bs': 'delta_w', 'delta_w_a_out': 'delta_w', 'delta_conv_w': 'delta_w', 'delta_conv_b': 'delta_w', 'delta_conv_ln_g': 'delta_w', 'delta_conv_ln_b': 'delta_w', 'delta_w_b_out': 'delta_w', 'delta_fox_bf': 'delta_w', 'delta_w_c_out': 'delta_w', 'delta_w_out': 'delta_w', 'delta_mlp_w1': 'delta_w', 'delta_mlp_w2': 'delta_w', 'new_m_ada_w': 'new_m', 'new_m_ada_b': 'new_m', 'new_m_mix_pre_g': 'new_m', 'new_m_mix_post_g': 'new_m', 'new_m_mlp_pre_g': 'new_m', 'new_m_mlp_post_g': 'new_m', 'new_m_w_in': 'new_m', 'new_m_gmlp_ln_g': 'new_m', 'new_m_gmlp_ln_b': 'new_m', 'new_m_gmlp_ws': 'new_m', 'new_m_gmlp_bs': 'new_m', 'new_m_w_a_out': 'new_m', 'new_m_conv_w': 'new_m', 'new_m_conv_b': 'new_m', 'new_m_conv_ln_g': 'new_m', 'new_m_conv_ln_b': 'new_m', 'new_m_w_b_out': 'new_m', 'new_m_fox_bf': 'new_m', 'new_m_w_c_out': 'new_m', 'new_m_w_out': 'new_m', 'new_m_mlp_w1': 'new_m', 'new_m_mlp_w2': 'new_m', 'new_v_ada_w': 'new_v', 'new_v_ada_b': 'new_v', 'new_v_mix_pre_g': 'new_v', 'new_v_mix_post_g': 'new_v', 'new_v_mlp_pre_g': 'new_v', 'new_v_mlp_post_g': 'new_v', 'new_v_w_in': 'new_v', 'new_v_gmlp_ln_g': 'new_v', 'new_v_gmlp_ln_b': 'new_v', 'new_v_gmlp_ws': 'new_v', 'new_v_gmlp_bs': 'new_v', 'new_v_w_a_out': 'new_v', 'new_v_conv_w': 'new_v', 'new_v_conv_b': 'new_v', 'new_v_conv_ln_g': 'new_v', 'new_v_conv_ln_b': 'new_v', 'new_v_w_b_out': 'new_v', 'new_v_fox_bf': 'new_v', 'new_v_w_c_out': 'new_v', 'new_v_w_out': 'new_v', 'new_v_mlp_w1': 'new_v', 'new_v_mlp_w2': 'new_v'}


def _forward(args):
    return _fwd_reference(*[args[k] for k in FWD_PARAMS])


def _output_shape():
    out = _jax.eval_shape(lambda: _forward(_fwd_setup_inputs(0)))
    return out.shape, out.dtype

N_MICROBATCH = 1
ADAM_LR = 0.001
ADAM_B1 = 0.9
ADAM_B2 = 0.999
ADAM_EPS = 1e-08
ADAM_WD = 0.01
ADAM_STEP = 10
PER_EXAMPLE_BATCH_AXIS = {'x': 0, 'c': 0, 'loss_target': 0}
SHARED_INPUTS = []
_WEIGHT_DTYPES = {'ada_w': _jnp.float32, 'ada_b': _jnp.float32, 'mix_pre_g': _jnp.float32, 'mix_post_g': _jnp.float32, 'mlp_pre_g': _jnp.float32, 'mlp_post_g': _jnp.float32, 'w_in': _jnp.float32, 'gmlp_ln_g': _jnp.float32, 'gmlp_ln_b': _jnp.float32, 'gmlp_ws': _jnp.float32, 'gmlp_bs': _jnp.float32, 'w_a_out': _jnp.float32, 'conv_w': _jnp.float32, 'conv_b': _jnp.float32, 'conv_ln_g': _jnp.float32, 'conv_ln_b': _jnp.float32, 'w_b_out': _jnp.float32, 'fox_bf': _jnp.float32, 'w_c_out': _jnp.float32, 'w_out': _jnp.float32, 'mlp_w1': _jnp.float32, 'mlp_w2': _jnp.float32}
MOMENT_SCALE = {'ada_w': 1.988166e+00, 'ada_b': 3.678917e+00, 'mix_pre_g': 1.713003e-01, 'mix_post_g': 3.889276e+00, 'mlp_pre_g': 1.747265e-01, 'mlp_post_g': 3.922181e+00, 'w_in': 1.650432e-01, 'gmlp_ln_g': 5.817045e-02, 'gmlp_ln_b': 6.738268e-02, 'gmlp_ws': 4.049982e-02, 'gmlp_bs': 6.097754e-02, 'w_a_out': 4.057699e-01, 'conv_w': 2.198960e-01, 'conv_b': 1.363456e+00, 'conv_ln_g': 6.735266e-01, 'conv_ln_b': 9.139035e-01, 'w_b_out': 2.921175e-01, 'fox_bf': 1.651540e-01, 'w_c_out': 3.978482e-01, 'w_out': 6.404344e-01, 'mlp_w1': 1.493703e-01, 'mlp_w2': 6.599383e-01}


def _to_microbatches(a, axis):
    t = _jnp.moveaxis(a, axis, 0)
    t = t.reshape((N_MICROBATCH, t.shape[0] // N_MICROBATCH) + t.shape[1:])
    return _jnp.moveaxis(t, 1, axis + 1)


def setup_inputs(seed: int = 0) -> dict:
    inp = _fwd_setup_inputs(seed)
    key = _jax.random.fold_in(_jax.random.key(seed), 7919)
    shape, _ = _output_shape()
    out = dict(inp)
    out["loss_target"] = _jax.random.normal(_jax.random.fold_in(key, 0), shape, _jnp.float32)
    for i, name in enumerate(TWIN_WEIGHTS):
        w = inp[name].astype(_jnp.float32)
        if MOMENT_SCALE is None:
            s = _jnp.sqrt(_jnp.mean(_jnp.square(w)) + 1e-30)
        else:
            s = MOMENT_SCALE[name]
        km, kv = _jax.random.split(_jax.random.fold_in(key, i + 1))
        out[name] = w
        out["m_" + name] = s * _jax.random.normal(km, w.shape, _jnp.float32)
        out["v_" + name] = (s * s) * _jax.random.uniform(kv, w.shape, _jnp.float32, 0.5, 1.5)
    if N_MICROBATCH > 1:
        for name, axis in PER_EXAMPLE_BATCH_AXIS.items():
            out[name] = _to_microbatches(out[name], axis)
    return {'x': out['x'], 'c': out['c'], 'ada_w': out['ada_w'], 'ada_b': out['ada_b'], 'mix_pre_g': out['mix_pre_g'], 'mix_post_g': out['mix_post_g'], 'mlp_pre_g': out['mlp_pre_g'], 'mlp_post_g': out['mlp_post_g'], 'w_in': out['w_in'], 'gmlp_ln_g': out['gmlp_ln_g'], 'gmlp_ln_b': out['gmlp_ln_b'], 'gmlp_ws': out['gmlp_ws'], 'gmlp_bs': out['gmlp_bs'], 'w_a_out': out['w_a_out'], 'conv_w': out['conv_w'], 'conv_b': out['conv_b'], 'conv_ln_g': out['conv_ln_g'], 'conv_ln_b': out['conv_ln_b'], 'w_b_out': out['w_b_out'], 'fox_bf': out['fox_bf'], 'w_c_out': out['w_c_out'], 'w_out': out['w_out'], 'mlp_w1': out['mlp_w1'], 'mlp_w2': out['mlp_w2'], 'loss_target': out['loss_target'], 'm_ada_w': out['m_ada_w'], 'm_ada_b': out['m_ada_b'], 'm_mix_pre_g': out['m_mix_pre_g'], 'm_mix_post_g': out['m_mix_post_g'], 'm_mlp_pre_g': out['m_mlp_pre_g'], 'm_mlp_post_g': out['m_mlp_post_g'], 'm_w_in': out['m_w_in'], 'm_gmlp_ln_g': out['m_gmlp_ln_g'], 'm_gmlp_ln_b': out['m_gmlp_ln_b'], 'm_gmlp_ws': out['m_gmlp_ws'], 'm_gmlp_bs': out['m_gmlp_bs'], 'm_w_a_out': out['m_w_a_out'], 'm_conv_w': out['m_conv_w'], 'm_conv_b': out['m_conv_b'], 'm_conv_ln_g': out['m_conv_ln_g'], 'm_conv_ln_b': out['m_conv_ln_b'], 'm_w_b_out': out['m_w_b_out'], 'm_fox_bf': out['m_fox_bf'], 'm_w_c_out': out['m_w_c_out'], 'm_w_out': out['m_w_out'], 'm_mlp_w1': out['m_mlp_w1'], 'm_mlp_w2': out['m_mlp_w2'], 'v_ada_w': out['v_ada_w'], 'v_ada_b': out['v_ada_b'], 'v_mix_pre_g': out['v_mix_pre_g'], 'v_mix_post_g': out['v_mix_post_g'], 'v_mlp_pre_g': out['v_mlp_pre_g'], 'v_mlp_post_g': out['v_mlp_post_g'], 'v_w_in': out['v_w_in'], 'v_gmlp_ln_g': out['v_gmlp_ln_g'], 'v_gmlp_ln_b': out['v_gmlp_ln_b'], 'v_gmlp_ws': out['v_gmlp_ws'], 'v_gmlp_bs': out['v_gmlp_bs'], 'v_w_a_out': out['v_w_a_out'], 'v_conv_w': out['v_conv_w'], 'v_conv_b': out['v_conv_b'], 'v_conv_ln_g': out['v_conv_ln_g'], 'v_conv_ln_b': out['v_conv_ln_b'], 'v_w_b_out': out['v_w_b_out'], 'v_fox_bf': out['v_fox_bf'], 'v_w_c_out': out['v_w_c_out'], 'v_w_out': out['v_w_out'], 'v_mlp_w1': out['v_mlp_w1'], 'v_mlp_w2': out['v_mlp_w2']}


def _loss(weights, diff, rest, loss_target):
    with _jax.named_scope("forward"):
        args = {**rest, TWIN_DIFF_INPUT: diff, **{k: w.astype(_WEIGHT_DTYPES[k]) for k, w in weights.items()}}
        y = _forward(args)
    with _jax.named_scope("loss_head"):
        err = _jnp.square(y.astype(_jnp.float32) - loss_target)
        return 0.5 * _jnp.sum(_jnp.mean(err, axis=-1)) if err.ndim else 0.5 * err


def _adamw(w, g, m, v):
    m = ADAM_B1 * m + (1.0 - ADAM_B1) * g
    v = ADAM_B2 * v + (1.0 - ADAM_B2) * _jnp.square(g)
    m_hat = m / (1.0 - ADAM_B1 ** ADAM_STEP)
    v_hat = v / (1.0 - ADAM_B2 ** ADAM_STEP)
    delta = -ADAM_LR * (m_hat / (_jnp.sqrt(v_hat) + ADAM_EPS) + ADAM_WD * w)
    return delta, m, v


def reference(x, c, ada_w, ada_b, mix_pre_g, mix_post_g, mlp_pre_g, mlp_post_g, w_in, gmlp_ln_g, gmlp_ln_b, gmlp_ws, gmlp_bs, w_a_out, conv_w, conv_b, conv_ln_g, conv_ln_b, w_b_out, fox_bf, w_c_out, w_out, mlp_w1, mlp_w2, loss_target, m_ada_w, m_ada_b, m_mix_pre_g, m_mix_post_g, m_mlp_pre_g, m_mlp_post_g, m_w_in, m_gmlp_ln_g, m_gmlp_ln_b, m_gmlp_ws, m_gmlp_bs, m_w_a_out, m_conv_w, m_conv_b, m_conv_ln_g, m_conv_ln_b, m_w_b_out, m_fox_bf, m_w_c_out, m_w_out, m_mlp_w1, m_mlp_w2, v_ada_w, v_ada_b, v_mix_pre_g, v_mix_post_g, v_mlp_pre_g, v_mlp_post_g, v_w_in, v_gmlp_ln_g, v_gmlp_ln_b, v_gmlp_ws, v_gmlp_bs, v_w_a_out, v_conv_w, v_conv_b, v_conv_ln_g, v_conv_ln_b, v_w_b_out, v_fox_bf, v_w_c_out, v_w_out, v_mlp_w1, v_mlp_w2):
    given = dict(x=x, c=c, ada_w=ada_w, ada_b=ada_b, mix_pre_g=mix_pre_g, mix_post_g=mix_post_g, mlp_pre_g=mlp_pre_g, mlp_post_g=mlp_post_g, w_in=w_in, gmlp_ln_g=gmlp_ln_g, gmlp_ln_b=gmlp_ln_b, gmlp_ws=gmlp_ws, gmlp_bs=gmlp_bs, w_a_out=w_a_out, conv_w=conv_w, conv_b=conv_b, conv_ln_g=conv_ln_g, conv_ln_b=conv_ln_b, w_b_out=w_b_out, fox_bf=fox_bf, w_c_out=w_c_out, w_out=w_out, mlp_w1=mlp_w1, mlp_w2=mlp_w2, loss_target=loss_target, m_ada_w=m_ada_w, m_ada_b=m_ada_b, m_mix_pre_g=m_mix_pre_g, m_mix_post_g=m_mix_post_g, m_mlp_pre_g=m_mlp_pre_g, m_mlp_post_g=m_mlp_post_g, m_w_in=m_w_in, m_gmlp_ln_g=m_gmlp_ln_g, m_gmlp_ln_b=m_gmlp_ln_b, m_gmlp_ws=m_gmlp_ws, m_gmlp_bs=m_gmlp_bs, m_w_a_out=m_w_a_out, m_conv_w=m_conv_w, m_conv_b=m_conv_b, m_conv_ln_g=m_conv_ln_g, m_conv_ln_b=m_conv_ln_b, m_w_b_out=m_w_b_out, m_fox_bf=m_fox_bf, m_w_c_out=m_w_c_out, m_w_out=m_w_out, m_mlp_w1=m_mlp_w1, m_mlp_w2=m_mlp_w2, v_ada_w=v_ada_w, v_ada_b=v_ada_b, v_mix_pre_g=v_mix_pre_g, v_mix_post_g=v_mix_post_g, v_mlp_pre_g=v_mlp_pre_g, v_mlp_post_g=v_mlp_post_g, v_w_in=v_w_in, v_gmlp_ln_g=v_gmlp_ln_g, v_gmlp_ln_b=v_gmlp_ln_b, v_gmlp_ws=v_gmlp_ws, v_gmlp_bs=v_gmlp_bs, v_w_a_out=v_w_a_out, v_conv_w=v_conv_w, v_conv_b=v_conv_b, v_conv_ln_g=v_conv_ln_g, v_conv_ln_b=v_conv_ln_b, v_w_b_out=v_w_b_out, v_fox_bf=v_fox_bf, v_w_c_out=v_w_c_out, v_w_out=v_w_out, v_mlp_w1=v_mlp_w1, v_mlp_w2=v_mlp_w2)
    weights = {n: given[n] for n in TWIN_WEIGHTS}
    shared = {n: given[n] for n in SHARED_INPUTS}
    per_example = {n: given[n] for n in ['x', 'c']}
    grad_fn = _jax.value_and_grad(_loss, argnums=(0, 1))

    def one_microbatch(ex, loss_target):
        ex = dict(ex)
        diff = ex.pop(TWIN_DIFF_INPUT)
        return grad_fn(weights, diff, {**shared, **ex}, loss_target)

    if N_MICROBATCH == 1:
        loss, (grad_w, grad_x) = one_microbatch(per_example, given["loss_target"])
    else:
        def body(carry, xs):
            loss_sum, grad_sum = carry
            l_k, (gw_k, gx_k) = one_microbatch(xs[0], xs[1])
            with _jax.named_scope("update"):
                return (loss_sum + l_k, _jax.tree.map(_jnp.add, grad_sum, gw_k)), gx_k

        init = (_jnp.zeros((), _jnp.float32), _jax.tree.map(_jnp.zeros_like, weights))
        (loss, grad_w), grad_x = _jax.lax.scan(body, init, (per_example, given["loss_target"]))
    with _jax.named_scope("update"):
        delta_w, new_m, new_v = {}, {}, {}
        for n in TWIN_WEIGHTS:
            delta_w[n], new_m[n], new_v[n] = _adamw(weights[n], grad_w[n], given["m_" + n], given["v_" + n])
    return (loss, grad_x, *[grad_w[n] for n in TWIN_WEIGHTS], *[delta_w[n] for n in TWIN_WEIGHTS],
            *[new_m[n] for n in TWIN_WEIGHTS], *[new_v[n] for n in TWIN_WEIGHTS])
```

```python
import functools

import jax
import jax.numpy as jnp
from jax import lax
from jax.experimental import pallas as pl
from jax.experimental.pallas import tpu as pltpu

F32 = jnp.float32
BF16 = jnp.bfloat16
I32 = jnp.int32
MESH = pl.DeviceIdType.MESH
ANY = pl.BlockSpec(memory_space=pl.ANY)

D = 1024
DM = 512
NG = 8
CH = 128
KW = 31
HALO = 32
DFF = 4096
NMOD = 6
EPS = 1e-6
LANE = 128
N_CHIPS = 4
N_DEV = 8
C_GATE, C_UV, C_GLU, C_Q, C_K, C_V, C_F, D_INP = 0, 3072, 4096, 5120, 5632, 6144, 6656, 7168
D_IN = 6664
VMEM_LIMIT = 56 * 1024 * 1024

ADAM_LR, ADAM_B1, ADAM_B2, ADAM_EPS, ADAM_WD, ADAM_STEP = 0.001, 0.9, 0.999, 1e-08, 0.01, 10

BIG = (("w_in", 1024, 1666, 1), ("w_a_out", 512, 256, 1), ("w_b_out", 512, 256, 1), ("w_c_out", 512, 256, 1),
       ("w_out", 256, 1024, 0), ("mlp_w1", 1024, 1024, 1), ("mlp_w2", 1024, 1024, 0))
PACK_ROWS = sum(r * c for _, r, c, _ in BIG) // LANE
PACK_TR = 4976


def _cparams(sem):
    return pltpu.CompilerParams(dimension_semantics=sem, vmem_limit_bytes=VMEM_LIMIT)


def _sigmoid(x):
    return jax.nn.sigmoid(x)


_GELU_K = 0.7978845608028654
_GELU_A = 0.044715


def _gelu(x):
    t = jnp.tanh(_GELU_K * (x + _GELU_A * x * x * x))
    return 0.5 * x * (1.0 + t)


def _gelu_grad(x):
    t = jnp.tanh(_GELU_K * (x + _GELU_A * x * x * x))
    return 0.5 * (1.0 + t) + 0.5 * x * (1.0 - t * t) * _GELU_K * (1.0 + 3.0 * _GELU_A * x * x)


def _mean(x):
    return jnp.mean(x, axis=-1, keepdims=True)


def _colsum(x):
    return jnp.sum(x, axis=0, keepdims=True)


def _dot(a, b, dims=((1,), (0,))):
    return lax.dot_general(a, b, (dims, ((), ())), preferred_element_type=F32)


NN = ((1,), (0,))
NT = ((1,), (1,))
TN = ((0,), (0,))


def _matmul(a, b, *, name, ta=False, tb=False, out_dtype=F32, tm=512, tn=512, tk=512, epilogue=None, extra=(),
            extra_out=()):
    m, k = (a.shape[1], a.shape[0]) if ta else a.shape
    n = b.shape[0] if tb else b.shape[1]
    tm, tn, tk = min(tm, m), min(tn, n), min(tk, k)
    assert m % tm == 0 and n % tn == 0 and k % tk == 0, (name, m, n, k, tm, tn, tk)
    nk = k // tk
    dims = ((0 if ta else 1,), (1 if tb else 0,))
    n_extra = len(extra)
    out_dtypes = (out_dtype,) + tuple(extra_out)

    def body(a_ref, b_ref, *rest):
        extra_refs = rest[:n_extra]
        out_refs = rest[n_extra:n_extra + len(out_dtypes)]
        acc_ref = rest[-1]
        kk = pl.program_id(2)

        @pl.when(kk == 0)
        def _():
            acc_ref[...] = jnp.zeros_like(acc_ref)

        acc_ref[...] += _dot(a_ref[...].astype(BF16), b_ref[...].astype(BF16), dims)

        @pl.when(kk == nk - 1)
        def _():
            acc = acc_ref[...]
            if epilogue is None:
                outs = (acc,)
            else:
                outs = epilogue(acc, *[r[...] for r in extra_refs])
            for o_ref, o in zip(out_refs, outs):
                o_ref[...] = o.astype(o_ref.dtype)

    a_spec = pl.BlockSpec((tk, tm), lambda i, j, kk: (kk, i)) if ta else pl.BlockSpec((tm, tk), lambda i, j, kk: (i, kk))
    b_spec = pl.BlockSpec((tn, tk), lambda i, j, kk: (j, kk)) if tb else pl.BlockSpec((tk, tn), lambda i, j, kk: (kk, j))
    o_spec = pl.BlockSpec((tm, tn), lambda i, j, kk: (i, j))
    outs = pl.pallas_call(
        body, name=name, grid=(m // tm, n // tn, nk),
        in_specs=[a_spec, b_spec] + [o_spec] * n_extra,
        out_specs=[o_spec] * len(out_dtypes),
        out_shape=[jax.ShapeDtypeStruct((m, n), dt) for dt in out_dtypes],
        scratch_shapes=[pltpu.VMEM((tm, tn), F32)],
        compiler_params=_cparams(("parallel", "parallel", "arbitrary")),
    )(a, b, *extra)
    return outs[0] if len(outs) == 1 else outs


def _rows(tm, n, col=0):
    return pl.BlockSpec((tm, n), lambda i: (i, col))


def _vec(n):
    return pl.BlockSpec((1, n), lambda i: (0, 0))


def _norm_mod(x, g, sc, sh, *, name, tm=256):
    t = x.shape[0]
    tm = min(tm, t)

    def body(x_ref, g_ref, sc_ref, sh_ref, h_ref):
        xv = x_ref[...]
        inv = lax.rsqrt(_mean(xv * xv) + EPS)
        h_ref[...] = ((xv * inv * g_ref[...]) * (1.0 + sc_ref[...]) + sh_ref[...]).astype(BF16)

    return pl.pallas_call(
        body, name=name, grid=(t // tm,), in_specs=[_rows(tm, D), _vec(D), _vec(D), _vec(D)],
        out_specs=_rows(tm, D), out_shape=jax.ShapeDtypeStruct((t, D), BF16),
        compiler_params=_cparams(("parallel",)))(x, g, sc, sh)


def _resid(x, y, gt, gp, *, name, tm=256):
    t = x.shape[0]
    tm = min(tm, t)

    def body(x_ref, y_ref, gt_ref, gp_ref, o_ref):
        yv = y_ref[...]
        inv = lax.rsqrt(_mean(yv * yv) + EPS)
        o_ref[...] = x_ref[...] + gt_ref[...] * (yv * inv * gp_ref[...])

    return pl.pallas_call(
        body, name=name, grid=(t // tm,), in_specs=[_rows(tm, D), _rows(tm, D), _vec(D), _vec(D)],
        out_specs=_rows(tm, D), out_shape=jax.ShapeDtypeStruct((t, D), F32),
        compiler_params=_cparams(("parallel",)))(x, y, gt, gp)


def _resid_bwd(dx, y, gt, gp, *, name, tm=256):
    t = dx.shape[0]
    tm = min(tm, t)

    def body(dx_ref, y_ref, gt_ref, gp_ref, dy_ref, dgt_ref, dgp_ref):
        @pl.when(pl.program_id(0) == 0)
        def _():
            dgt_ref[...] = jnp.zeros_like(dgt_ref)
            dgp_ref[...] = jnp.zeros_like(dgp_ref)

        dxv, yv, gp_v = dx_ref[...], y_ref[...], gp_ref[...]
        inv = lax.rsqrt(_mean(yv * yv) + EPS)
        yh = yv * inv
        dgt_ref[...] += _colsum(dxv * (yh * gp_v))
        dr = dxv * gt_ref[...]
        dgp_ref[...] += _colsum(dr * yh)
        dyn = dr * gp_v
        dy_ref[...] = (inv * (dyn - yh * _mean(dyn * yh))).astype(BF16)

    return pl.pallas_call(
        body, name=name, grid=(t // tm,), in_specs=[_rows(tm, D), _rows(tm, D), _vec(D), _vec(D)],
        out_specs=[_rows(tm, D), _vec(D), _vec(D)],
        out_shape=[jax.ShapeDtypeStruct((t, D), BF16), jax.ShapeDtypeStruct((1, D), F32),
                   jax.ShapeDtypeStruct((1, D), F32)],
        compiler_params=_cparams(("arbitrary",)))(dx, y, gt, gp)


def _norm_bwd(dh, dx_res, x, g, sc, *, name, tm=256):
    t = dh.shape[0]
    tm = min(tm, t)

    def body(dh_ref, dxr_ref, x_ref, g_ref, sc_ref, dx_ref, dg_ref, dsc_ref, dsh_ref):
        @pl.when(pl.program_id(0) == 0)
        def _():
            dg_ref[...] = jnp.zeros_like(dg_ref)
            dsc_ref[...] = jnp.zeros_like(dsc_ref)
            dsh_ref[...] = jnp.zeros_like(dsh_ref)

        dhv, xv, gv = dh_ref[...], x_ref[...], g_ref[...]
        inv = lax.rsqrt(_mean(xv * xv) + EPS)
        xh = xv * inv
        dsh_ref[...] += _colsum(dhv)
        dsc_ref[...] += _colsum(dhv * (xh * gv))
        dn = dhv * (1.0 + sc_ref[...])
        dg_ref[...] += _colsum(dn * xh)
        dxh = dn * gv
        dx_ref[...] = inv * (dxh - xh * _mean(dxh * xh)) + dxr_ref[...]

    vec_out = jax.ShapeDtypeStruct((1, D), F32)
    return pl.pallas_call(
        body, name=name, grid=(t // tm,), in_specs=[_rows(tm, D), _rows(tm, D), _rows(tm, D), _vec(D), _vec(D)],
        out_specs=[_rows(tm, D), _vec(D), _vec(D), _vec(D)],
        out_shape=[jax.ShapeDtypeStruct((t, D), F32), vec_out, vec_out, vec_out],
        compiler_params=_cparams(("arbitrary",)))(dh, dx_res, x, g, sc)


def _loss_and_grad(x, target, *, tm=256):
    t = x.shape[0]
    tm = min(tm, t)

    def body(x_ref, t_ref, loss_ref, dx_ref):
        @pl.when(pl.program_id(0) == 0)
        def _():
            loss_ref[...] = jnp.zeros_like(loss_ref)

        e = x_ref[...] - t_ref[...]
        dx_ref[...] = e * (1.0 / D)
        s = jnp.sum(jnp.sum(e * e, axis=1, keepdims=True), axis=0, keepdims=True) * (0.5 / D)
        loss_ref[...] += jnp.broadcast_to(s, loss_ref.shape)

    loss, dx = pl.pallas_call(
        body, name="loss", grid=(t // tm,), in_specs=[_rows(tm, D), _rows(tm, D)],
        out_specs=[pl.BlockSpec((8, LANE), lambda i: (0, 0)), _rows(tm, D)],
        out_shape=[jax.ShapeDtypeStruct((8, LANE), F32), jax.ShapeDtypeStruct((t, D), F32)],
        compiler_params=_cparams(("arbitrary",)))(x, target)
    return loss[0, 0], dx


def _gmlp_core(uv, lng, lnb, ws_ref, bsx):
    tm = uv.shape[0]
    gu = _gelu(uv[:, :DM])
    gv = _gelu(uv[:, DM:])
    mu = _mean(gv)
    vc = gv - mu
    rstd = lax.rsqrt(_mean(vc * vc) + EPS)
    vh = vc * rstd
    vln = vh * lng + lnb
    lane = lax.broadcasted_iota(I32, (CH, LANE), 1)
    sv_rows = []
    for nchunk in range(tm // CH):
        vb = vln[nchunk * CH:(nchunk + 1) * CH].astype(BF16)
        cols = []
        for cb in range(DM // LANE):
            vcb = vb[:, cb * LANE:(cb + 1) * LANE]
            lo = _dot(ws_ref[2 * cb], vcb)
            hi = _dot(ws_ref[2 * cb + 1], vcb)
            cols.append(jnp.where(lane < 64, lo, hi))
        sv_rows.append(jnp.concatenate(cols, axis=1) + bsx)
    sv = jnp.concatenate(sv_rows, axis=0) if len(sv_rows) > 1 else sv_rows[0]
    return gu, vh, rstd, vln, sv


def _gmlp_fwd(proj, lng, lnb, wsm, bsx, *, tm=256):
    t = proj.shape[0]
    tm = min(tm, t)

    def body(uv_ref, lng_ref, lnb_ref, ws_ref, bs_ref, ga_ref):
        gu, _, _, _, sv = _gmlp_core(uv_ref[...], lng_ref[...], lnb_ref[...], ws_ref, bs_ref[...])
        ga_ref[...] = (gu * sv).astype(BF16)

    return pl.pallas_call(
        body, name="gmlp_fwd", grid=(t // tm,),
        in_specs=[_rows(tm, 2 * DM, C_UV // (2 * DM)), _vec(DM), _vec(DM),
                  pl.BlockSpec((NG, CH, CH), lambda i: (0, 0, 0)), pl.BlockSpec((CH, DM), lambda i: (0, 0))],
        out_specs=_rows(tm, DM), out_shape=jax.ShapeDtypeStruct((t, DM), BF16),
        compiler_params=_cparams(("parallel",)))(proj, lng, lnb, wsm, bsx)


def _gmlp_bwd(dga, proj, lng, lnb, wsm, wsmt, bsx, *, tm=256):
    t = proj.shape[0]
    tm = min(tm, t)

    def body(dga_ref, uv_ref, lng_ref, lnb_ref, ws_ref, wst_ref, bs_ref, duv_ref, dws_ref, dbs_ref, dlng_ref, dlnb_ref,
             dbsx_ref):
        i = pl.program_id(0)

        @pl.when(i == 0)
        def _():
            dws_ref[...] = jnp.zeros_like(dws_ref)
            dbsx_ref[...] = jnp.zeros_like(dbsx_ref)
            dlng_ref[...] = jnp.zeros_like(dlng_ref)
            dlnb_ref[...] = jnp.zeros_like(dlnb_ref)

        uv = uv_ref[...]
        lng_v = lng_ref[...]
        gu, vh, rstd, vln, sv = _gmlp_core(uv, lng_v, lnb_ref[...], ws_ref, bs_ref[...])
        dga_v = dga_ref[...]
        dgu = dga_v * sv
        dsv = dga_v * gu
        lane = lax.broadcasted_iota(I32, (CH, LANE), 1)
        tril = lax.broadcasted_iota(I32, (CH, CH), 0) >= lax.broadcasted_iota(I32, (CH, CH), 1)
        dvln_rows = []
        for nchunk in range(tm // CH):
            rows = slice(nchunk * CH, (nchunk + 1) * CH)
            dbsx_ref[...] += dsv[rows]
            vb = vln[rows].astype(BF16)
            cols = []
            for cb in range(DM // LANE):
                cs = slice(cb * LANE, (cb + 1) * LANE)
                dsvb = dsv[rows, cs]
                vcb = vb[:, cs]
                dlo = jnp.where(lane < 64, dsvb, 0.0).astype(BF16)
                dhi = jnp.where(lane < 64, 0.0, dsvb).astype(BF16)
                dws_ref[2 * cb] += jnp.where(tril, _dot(dlo, vcb, NT), 0.0)
                dws_ref[2 * cb + 1] += jnp.where(tril, _dot(dhi, vcb, NT), 0.0)
                dsb = dsvb.astype(BF16)
                cols.append(jnp.where(lane < 64, _dot(wst_ref[2 * cb], dsb), _dot(wst_ref[2 * cb + 1], dsb)))
            dvln_rows.append(jnp.concatenate(cols, axis=1))
        dvln = jnp.concatenate(dvln_rows, axis=0) if len(dvln_rows) > 1 else dvln_rows[0]
        dlnb_ref[...] += _colsum(dvln)
        dlng_ref[...] += _colsum(dvln * vh)
        dvh = dvln * lng_v
        dgv = rstd * (dvh - _mean(dvh) - vh * _mean(dvh * vh))
        duv_ref[:, :DM] = (dgu * _gelu_grad(uv[:, :DM])).astype(BF16)
        duv_ref[:, DM:] = (dgv * _gelu_grad(uv[:, DM:])).astype(BF16)

        @pl.when(i == pl.num_programs(0) - 1)
        def _():
            ind = (lax.broadcasted_iota(I32, (DM, LANE), 0) // 64 == lax.broadcasted_iota(I32, (DM, LANE), 1)).astype(F32)
            dbs_ref[...] = jnp.dot(dbsx_ref[...], ind, preferred_element_type=F32, precision=lax.Precision.HIGHEST)

    vec_out = jax.ShapeDtypeStruct((1, DM), F32)
    outs = pl.pallas_call(
        body, name="gmlp_bwd", grid=(t // tm,),
        in_specs=[_rows(tm, DM), _rows(tm, 2 * DM, C_UV // (2 * DM)), _vec(DM), _vec(DM),
                  pl.BlockSpec((NG, CH, CH), lambda i: (0, 0, 0)), pl.BlockSpec((NG, CH, CH), lambda i: (0, 0, 0)),
                  pl.BlockSpec((CH, DM), lambda i: (0, 0))],
        out_specs=[_rows(tm, 2 * DM), pl.BlockSpec((NG, CH, CH), lambda i: (0, 0, 0)),
                   pl.BlockSpec((CH, LANE), lambda i: (0, 0)), _vec(DM), _vec(DM)],
        out_shape=[jax.ShapeDtypeStruct((t, 2 * DM), BF16), jax.ShapeDtypeStruct((NG, CH, CH), F32),
                   jax.ShapeDtypeStruct((CH, LANE), F32), vec_out, vec_out],
        scratch_shapes=[pltpu.VMEM((CH, DM), F32)],
        compiler_params=_cparams(("arbitrary",)))(dga, proj, lng, lnb, wsm, wsmt, bsx)
    return outs


def _glu_into(zs_ref, glu_ref, halo_ref, first):
    hal = halo_ref[...]
    z0h = hal[:, :DM] * _sigmoid(hal[:, DM:])
    zs_ref[0:HALO, :] = jnp.where(first, 0.0, z0h)
    g = glu_ref[...]
    zs_ref[HALO:, :] = g[:, :DM] * _sigmoid(g[:, DM:])


def _conv_fwd(proj, cw, cb, lng, lnb, *, tm=256, rb=64):
    t = proj.shape[0]
    tm = min(tm, t)
    hb = tm // HALO
    gcol = C_GLU // (2 * DM)

    def body(glu_ref, halo_ref, cw_ref, cb_ref, lng_ref, lnb_ref, zc_ref, zb_ref, zs_ref):
        i = pl.program_id(0)
        _glu_into(zs_ref, glu_ref, halo_ref, i == 0)
        for cbk in range(DM // LANE):
            cs = slice(cbk * LANE, (cbk + 1) * LANE)
            for r in range(tm // rb):
                acc = jnp.broadcast_to(cb_ref[:, cs], (rb, LANE))
                for k in range(KW):
                    off = r * rb + HALO - (KW - 1) + k
                    acc = acc + cw_ref[k:k + 1, cs] * zs_ref[off:off + rb, cs]
                zc_ref[r * rb:(r + 1) * rb, cs] = acc
        zc = zc_ref[...]
        mu = _mean(zc)
        zcc = zc - mu
        zh = zcc * lax.rsqrt(_mean(zcc * zcc) + EPS)
        a = zh * lng_ref[...] + lnb_ref[...]
        zb_ref[...] = (a * _sigmoid(a)).astype(BF16)

    return pl.pallas_call(
        body, name="conv_fwd", grid=(t // tm,),
        in_specs=[_rows(tm, 2 * DM, gcol),
                  pl.BlockSpec((HALO, 2 * DM), lambda i: (jnp.maximum(i * hb - 1, 0), gcol)),
                  pl.BlockSpec((KW, DM), lambda i: (0, 0)), _vec(DM), _vec(DM), _vec(DM)],
        out_specs=[_rows(tm, DM), _rows(tm, DM)],
        out_shape=[jax.ShapeDtypeStruct((t, DM), F32), jax.ShapeDtypeStruct((t, DM), BF16)],
        scratch_shapes=[pltpu.VMEM((HALO + tm, DM), F32)],
        compiler_params=_cparams(("parallel",)))(proj, proj, cw, cb, lng, lnb)


def _conv_bwd_ln(dzb, zc, lng, lnb, *, tm=256):
    t = zc.shape[0]
    tm = min(tm, t)

    def body(dzb_ref, zc_ref, lng_ref, lnb_ref, dzc_ref, dlng_ref, dlnb_ref):
        @pl.when(pl.program_id(0) == 0)
        def _():
            dlng_ref[...] = jnp.zeros_like(dlng_ref)
            dlnb_ref[...] = jnp.zeros_like(dlnb_ref)

        zc = zc_ref[...]
        lng_v = lng_ref[...]
        mu = _mean(zc)
        zcc = zc - mu
        rstd = lax.rsqrt(_mean(zcc * zcc) + EPS)
        zh = zcc * rstd
        a = zh * lng_v + lnb_ref[...]
        s = _sigmoid(a)
        da = dzb_ref[...] * (s * (1.0 + a * (1.0 - s)))
        dlnb_ref[...] += _colsum(da)
        dlng_ref[...] += _colsum(da * zh)
        dzh = da * lng_v
        dzc_ref[...] = rstd * (dzh - _mean(dzh) - zh * _mean(dzh * zh))

    vec_out = jax.ShapeDtypeStruct((1, DM), F32)
    return pl.pallas_call(
        body, name="conv_bwd_ln", grid=(t // tm,), in_specs=[_rows(tm, DM), _rows(tm, DM), _vec(DM), _vec(DM)],
        out_specs=[_rows(tm, DM), _vec(DM), _vec(DM)],
        out_shape=[jax.ShapeDtypeStruct((t, DM), F32), vec_out, vec_out],
        compiler_params=_cparams(("arbitrary",)))(dzb, zc, lng, lnb)


def _conv_bwd(dzc, proj, cw, *, tm=256, rb=64):
    t = proj.shape[0]
    tm = min(tm, t)
    hb = tm // HALO
    nblk = t // tm
    gcol = C_GLU // (2 * DM)

    def body(dzc_ref, dnext_ref, glu_ref, halo_ref, cw_ref, dglu_ref, dcw_ref, dcb_ref, zs_ref, ds_ref):
        i = pl.program_id(0)

        @pl.when(i == 0)
        def _():
            dcw_ref[...] = jnp.zeros_like(dcw_ref)
            dcb_ref[...] = jnp.zeros_like(dcb_ref)

        _glu_into(zs_ref, glu_ref, halo_ref, i == 0)
        dzc = dzc_ref[...]
        ds_ref[0:tm, :] = dzc
        ds_ref[tm:, :] = jnp.where(i == nblk - 1, 0.0, dnext_ref[...])
        dcb_ref[...] += _colsum(dzc)
        for k in range(KW):
            off = HALO - (KW - 1) + k
            dcw_ref[k:k + 1, :] += _colsum(dzc * zs_ref[off:off + tm, :])
        g = glu_ref[...]
        val, sg = g[:, :DM], _sigmoid(g[:, DM:])
        for cbk in range(DM // LANE):
            cs = slice(cbk * LANE, (cbk + 1) * LANE)
            for r in range(tm // rb):
                acc = jnp.zeros((rb, LANE), F32)
                for k in range(KW):
                    off = r * rb + (KW - 1) - k
                    acc = acc + cw_ref[k:k + 1, cs] * ds_ref[off:off + rb, cs]
                rs = slice(r * rb, (r + 1) * rb)
                dglu_ref[rs, cs] = (acc * sg[rs, cs]).astype(BF16)
                v, s = val[rs, cs], sg[rs, cs]
                dglu_ref[rs, DM + cbk * LANE:DM + (cbk + 1) * LANE] = (acc * v * s * (1.0 - s)).astype(BF16)

    return pl.pallas_call(
        body, name="conv_bwd", grid=(nblk,),
        in_specs=[_rows(tm, DM),
                  pl.BlockSpec((HALO, DM), lambda i: (jnp.minimum((i + 1) * hb, nblk * hb - 1), 0)),
                  _rows(tm, 2 * DM, gcol),
                  pl.BlockSpec((HALO, 2 * DM), lambda i: (jnp.maximum(i * hb - 1, 0), gcol)),
                  pl.BlockSpec((KW, DM), lambda i: (0, 0))],
        out_specs=[_rows(tm, 2 * DM), pl.BlockSpec((HALO, DM), lambda i: (0, 0)), _vec(DM)],
        out_shape=[jax.ShapeDtypeStruct((t, 2 * DM), BF16), jax.ShapeDtypeStruct((HALO, DM), F32),
                   jax.ShapeDtypeStruct((1, DM), F32)],
        scratch_shapes=[pltpu.VMEM((HALO + tm, DM), F32), pltpu.VMEM((tm + HALO, DM), F32)],
        compiler_params=_cparams(("arbitrary",)))(dzc, dzc, proj, proj, cw)


def _log_sigmoid(x):
    return jnp.minimum(x, 0.0) - jnp.log1p(jnp.exp(-jnp.abs(x)))


def _fox_cum(proj, bfp):
    t = proj.shape[0]
    fcol = C_F // LANE

    def body(f_ref, bf_ref, cum_ref, carry_ref):
        @pl.when(pl.program_id(0) == 0)
        def _():
            carry_ref[...] = jnp.zeros_like(carry_ref)

        lf = _log_sigmoid(f_ref[...] + bf_ref[...])
        tri = (lax.broadcasted_iota(I32, (CH, CH), 0) >= lax.broadcasted_iota(I32, (CH, CH), 1)).astype(F32)
        cum = jnp.dot(tri, lf, preferred_element_type=F32, precision=lax.Precision.HIGHEST) + carry_ref[0:1, :]
        cum_ref[...] = cum
        carry_ref[...] = jnp.broadcast_to(cum[CH - 1:CH, :], carry_ref.shape)

    return pl.pallas_call(
        body, name="fox_cum", grid=(t // CH,), in_specs=[_rows(CH, LANE, fcol), _vec(LANE)],
        out_specs=_rows(CH, LANE), out_shape=jax.ShapeDtypeStruct((t, LANE), F32),
        scratch_shapes=[pltpu.VMEM((8, LANE), F32)],
        compiler_params=_cparams(("arbitrary",)))(proj, bfp)


def _fox_cum_bwd(dcum, proj, bfp):
    t = proj.shape[0]
    nb = t // CH
    fcol = C_F // LANE
    fw = D_INP - C_F

    def body(dc_ref, f_ref, bf_ref, df_ref, dbf_ref, carry_ref):
        @pl.when(pl.program_id(0) == 0)
        def _():
            carry_ref[...] = jnp.zeros_like(carry_ref)
            dbf_ref[...] = jnp.zeros_like(dbf_ref)

        triu = (lax.broadcasted_iota(I32, (CH, CH), 0) <= lax.broadcasted_iota(I32, (CH, CH), 1)).astype(F32)
        dlf = jnp.dot(triu, dc_ref[...], preferred_element_type=F32, precision=lax.Precision.HIGHEST) + carry_ref[0:1, :]
        carry_ref[...] = jnp.broadcast_to(dlf[0:1, :], carry_ref.shape)
        z = f_ref[...] + bf_ref[...]
        lane = lax.broadcasted_iota(I32, (CH, LANE), 1)
        df = jnp.where(lane < NG, dlf * _sigmoid(-z), 0.0)
        dbf_ref[...] += _colsum(df)
        df_ref[:, 0:LANE] = df.astype(BF16)
        df_ref[:, LANE:] = jnp.zeros((CH, fw - LANE), BF16)

    return pl.pallas_call(
        body, name="fox_cum_bwd", grid=(nb,),
        in_specs=[pl.BlockSpec((CH, LANE), lambda i: (nb - 1 - i, 0)),
                  pl.BlockSpec((CH, LANE), lambda i: (nb - 1 - i, fcol)), _vec(LANE)],
        out_specs=[pl.BlockSpec((CH, fw), lambda i: (nb - 1 - i, 0)), _vec(LANE)],
        out_shape=[jax.ShapeDtypeStruct((t, fw), BF16), jax.ShapeDtypeStruct((1, LANE), F32)],
        scratch_shapes=[pltpu.VMEM((8, LANE), F32)],
        compiler_params=_cparams(("arbitrary",)))(dcum, proj, bfp)


HD = 64
ATT_SCALE = 0.125
NEG = -1e30


def _att_scores(q_ref, k_ref, cq_ref, ck_ref, hh, qi, kj, tq, tk):
    hs = slice(hh * HD, (hh + 1) * HD)
    q = q_ref[:, hs].astype(BF16)
    k = k_ref[:, hs].astype(BF16)
    s = _dot(q, k, NT) * ATT_SCALE + cq_ref[0, :, hh * HD:hh * HD + 1] - ck_ref[0, hh:hh + 1, :]
    row = qi * tq + lax.broadcasted_iota(I32, (tq, tk), 0)
    col = kj * tk + lax.broadcasted_iota(I32, (tq, tk), 1)
    return jnp.where(col <= row, s, NEG)


def _attn_fwd(proj, cumq, cumk, *, tq=256):
    t = proj.shape[0]
    tq = min(tq, t)
    tk = tq
    nq = t // tq
    qc, kc, vc = C_Q // LANE, C_K // LANE, C_V // LANE

    def body(q_ref, k_ref, v_ref, cq_ref, ck_ref, o_ref, lse_ref, m_ref, l_ref, acc_ref):
        i, j = pl.program_id(1), pl.program_id(2)

        @pl.when(j == 0)
        def _():
            m_ref[...] = jnp.full_like(m_ref, NEG)
            l_ref[...] = jnp.zeros_like(l_ref)
            acc_ref[...] = jnp.zeros_like(acc_ref)

        @pl.when(j <= i)
        def _():
            for hh in range(2):
                s = _att_scores(q_ref, k_ref, cq_ref, ck_ref, hh, i, j, tq, tk)
                m_prev = m_ref[hh]
                m_new = jnp.maximum(m_prev, jnp.max(s, axis=1, keepdims=True))
                alpha = jnp.exp(m_prev - m_new)
                p = jnp.exp(s - m_new)
                l_ref[hh] = alpha * l_ref[hh] + jnp.sum(p, axis=1, keepdims=True)
                v = v_ref[:, hh * HD:(hh + 1) * HD].astype(BF16)
                acc_ref[hh] = alpha * acc_ref[hh] + _dot(p.astype(BF16), v)
                m_ref[hh] = m_new

        @pl.when(j == i)
        def _():
            for hh in range(2):
                l = l_ref[hh]
                o_ref[:, hh * HD:(hh + 1) * HD] = (acc_ref[hh] / l).astype(BF16)
                lse_ref[0, :, hh * HD:(hh + 1) * HD] = jnp.broadcast_to(m_ref[hh] + jnp.log(l), (tq, HD))

    return pl.pallas_call(
        body, name="attn_fwd", grid=(NG // 2, nq, nq),
        in_specs=[pl.BlockSpec((tq, LANE), lambda h, i, j: (i, qc + h)),
                  pl.BlockSpec((tk, LANE), lambda h, i, j: (jnp.minimum(j, i), kc + h)),
                  pl.BlockSpec((tk, LANE), lambda h, i, j: (jnp.minimum(j, i), vc + h)),
                  pl.BlockSpec((1, tq, LANE), lambda h, i, j: (h, i, 0)),
                  pl.BlockSpec((1, 2, tk), lambda h, i, j: (h, 0, jnp.minimum(j, i)))],
        out_specs=[pl.BlockSpec((tq, LANE), lambda h, i, j: (i, h)),
                   pl.BlockSpec((1, tq, LANE), lambda h, i, j: (h, i, 0))],
        out_shape=[jax.ShapeDtypeStruct((t, DM), BF16), jax.ShapeDtypeStruct((NG // 2, t, LANE), F32)],
        scratch_shapes=[pltpu.VMEM((2, tq, 1), F32), pltpu.VMEM((2, tq, 1), F32), pltpu.VMEM((2, tq, HD), F32)],
        compiler_params=_cparams(("parallel", "parallel", "arbitrary")))(proj, proj, proj, cumq, cumk)


def _attn_bwd_dq(proj, cumq, cumk, o, do, lse, *, tq=256):
    t = proj.shape[0]
    tq = min(tq, t)
    tk = tq
    nq = t // tq
    qc, kc, vc = C_Q // LANE, C_K // LANE, C_V // LANE

    def body(q_ref, k_ref, v_ref, cq_ref, ck_ref, o_ref, do_ref, lse_ref, dq_ref, dcq_ref, delta_ref, rs_ref, acc_ref):
        i, j = pl.program_id(1), pl.program_id(2)

        @pl.when(j == 0)
        def _():
            acc_ref[...] = jnp.zeros_like(acc_ref)
            rs_ref[...] = jnp.zeros_like(rs_ref)
            for hh in range(2):
                hs = slice(hh * HD, (hh + 1) * HD)
                delta_ref[hh] = jnp.sum(do_ref[:, hs] * o_ref[:, hs].astype(F32), axis=1, keepdims=True)

        @pl.when(j <= i)
        def _():
            for hh in range(2):
                hs = slice(hh * HD, (hh + 1) * HD)
                s = _att_scores(q_ref, k_ref, cq_ref, ck_ref, hh, i, j, tq, tk)
                p = jnp.exp(s - lse_ref[0, :, hh * HD:hh * HD + 1])
                dp = _dot(do_ref[:, hs].astype(BF16), v_ref[:, hs].astype(BF16), NT)
                ds = p * (dp - delta_ref[hh])
                rs_ref[hh] += jnp.sum(ds, axis=1, keepdims=True)
                acc_ref[hh] += _dot(ds.astype(BF16), k_ref[:, hs].astype(BF16))

        @pl.when(j == i)
        def _():
            for hh in range(2):
                dq_ref[:, hh * HD:(hh + 1) * HD] = (acc_ref[hh] * ATT_SCALE).astype(BF16)
                dcq_ref[0, :, hh * HD:(hh + 1) * HD] = jnp.broadcast_to(rs_ref[hh], (tq, HD))

    return pl.pallas_call(
        body, name="attn_bwd_dq", grid=(NG // 2, nq, nq),
        in_specs=[pl.BlockSpec((tq, LANE), lambda h, i, j: (i, qc + h)),
                  pl.BlockSpec((tk, LANE), lambda h, i, j: (jnp.minimum(j, i), kc + h)),
                  pl.BlockSpec((tk, LANE), lambda h, i, j: (jnp.minimum(j, i), vc + h)),
                  pl.BlockSpec((1, tq, LANE), lambda h, i, j: (h, i, 0)),
                  pl.BlockSpec((1, 2, tk), lambda h, i, j: (h, 0, jnp.minimum(j, i))),
                  pl.BlockSpec((tq, LANE), lambda h, i, j: (i, h)),
                  pl.BlockSpec((tq, LANE), lambda h, i, j: (i, h)),
                  pl.BlockSpec((1, tq, LANE), lambda h, i, j: (h, i, 0))],
        out_specs=[pl.BlockSpec((tq, LANE), lambda h, i, j: (i, h)),
                   pl.BlockSpec((1, tq, LANE), lambda h, i, j: (h, i, 0))],
        out_shape=[jax.ShapeDtypeStruct((t, DM), BF16), jax.ShapeDtypeStruct((NG // 2, t, LANE), F32)],
        scratch_shapes=[pltpu.VMEM((2, tq, 1), F32), pltpu.VMEM((2, tq, 1), F32), pltpu.VMEM((2, tq, HD), F32)],
        compiler_params=_cparams(("parallel", "parallel", "arbitrary")))(proj, proj, proj, cumq, cumk, o, do, lse)


def _attn_bwd_dkv(proj, cumq, cumk, o, do, lse, *, tq=256):
    t = proj.shape[0]
    tq = min(tq, t)
    tk = tq
    nq = t // tq
    qc, kc, vc = C_Q // LANE, C_K // LANE, C_V // LANE

    def body(q_ref, k_ref, v_ref, cq_ref, ck_ref, o_ref, do_ref, lse_ref, dk_ref, dv_ref, dck_ref, dka_ref, dva_ref,
             dca_ref):
        j, i = pl.program_id(1), pl.program_id(2)

        @pl.when(i == 0)
        def _():
            dka_ref[...] = jnp.zeros_like(dka_ref)
            dva_ref[...] = jnp.zeros_like(dva_ref)
            dca_ref[...] = jnp.zeros_like(dca_ref)

        @pl.when(i >= j)
        def _():
            for hh in range(2):
                hs = slice(hh * HD, (hh + 1) * HD)
                s = _att_scores(q_ref, k_ref, cq_ref, ck_ref, hh, i, j, tq, tk)
                p = jnp.exp(s - lse_ref[0, :, hh * HD:hh * HD + 1])
                dob = do_ref[:, hs].astype(BF16)
                dva_ref[hh] += _dot(p.astype(BF16), dob, TN)
                delta = jnp.sum(do_ref[:, hs] * o_ref[:, hs].astype(F32), axis=1, keepdims=True)
                dp = _dot(dob, v_ref[:, hs].astype(BF16), NT)
                ds = p * (dp - delta)
                dka_ref[hh] += _dot(ds.astype(BF16), q_ref[:, hs].astype(BF16), TN)
                dca_ref[hh:hh + 1, :] -= _colsum(ds)

        @pl.when(i == nq - 1)
        def _():
            for hh in range(2):
                dk_ref[:, hh * HD:(hh + 1) * HD] = (dka_ref[hh] * ATT_SCALE).astype(BF16)
                dv_ref[:, hh * HD:(hh + 1) * HD] = dva_ref[hh].astype(BF16)
            dck_ref[0] = dca_ref[0:2, :]

    return pl.pallas_call(
        body, name="attn_bwd_dkv", grid=(NG // 2, nq, nq),
        in_specs=[pl.BlockSpec((tq, LANE), lambda h, j, i: (jnp.maximum(i, j), qc + h)),
                  pl.BlockSpec((tk, LANE), lambda h, j, i: (j, kc + h)),
                  pl.BlockSpec((tk, LANE), lambda h, j, i: (j, vc + h)),
                  pl.BlockSpec((1, tq, LANE), lambda h, j, i: (h, jnp.maximum(i, j), 0)),
                  pl.BlockSpec((1, 2, tk), lambda h, j, i: (h, 0, j)),
                  pl.BlockSpec((tq, LANE), lambda h, j, i: (jnp.maximum(i, j), h)),
                  pl.BlockSpec((tq, LANE), lambda h, j, i: (jnp.maximum(i, j), h)),
                  pl.BlockSpec((1, tq, LANE), lambda h, j, i: (h, jnp.maximum(i, j), 0))],
        out_specs=[pl.BlockSpec((tk, LANE), lambda h, j, i: (j, h)),
                   pl.BlockSpec((tk, LANE), lambda h, j, i: (j, h)),
                   pl.BlockSpec((1, 2, tk), lambda h, j, i: (h, 0, j))],
        out_shape=[jax.ShapeDtypeStruct((t, DM), BF16), jax.ShapeDtypeStruct((t, DM), BF16),
                   jax.ShapeDtypeStruct((NG // 2, 2, t), F32)],
        scratch_shapes=[pltpu.VMEM((2, tk, HD), F32), pltpu.VMEM((2, tk, HD), F32), pltpu.VMEM((8, tk), F32)],
        compiler_params=_cparams(("parallel", "parallel", "arbitrary")))(proj, proj, proj, cumq, cumk, o, do, lse)


def _merge_fwd(ga, zb, att, proj, wa, wb, wc, *, tm=256):
    t = ga.shape[0]
    tm = min(tm, t)
    wspec = pl.BlockSpec((DM, D), lambda i: (0, 0))

    def body(ga_ref, zb_ref, att_ref, gate_ref, wa_ref, wb_ref, wc_ref, m_ref):
        acc = jnp.zeros((tm, D), F32)
        for b, (x_ref, w_ref) in enumerate(((ga_ref, wa_ref), (zb_ref, wb_ref), (att_ref, wc_ref))):
            acc = acc + _sigmoid(gate_ref[:, b * D:(b + 1) * D]) * _dot(x_ref[...], w_ref[...])
        m_ref[...] = acc.astype(BF16)

    return pl.pallas_call(
        body, name="merge_fwd", grid=(t // tm,),
        in_specs=[_rows(tm, DM), _rows(tm, DM), _rows(tm, DM), _rows(tm, 3 * D, 0), wspec, wspec, wspec],
        out_specs=_rows(tm, D), out_shape=jax.ShapeDtypeStruct((t, D), BF16),
        compiler_params=_cparams(("parallel",)))(ga, zb, att, proj, wa, wb, wc)


def _merge_bwd(dm, ga, zb, att, proj, wa, wb, wc, *, tm=256):
    t = ga.shape[0]
    tm = min(tm, t)
    wspec = pl.BlockSpec((DM, D), lambda i: (0, 0))

    def body(dm_ref, ga_ref, zb_ref, att_ref, gate_ref, wa_ref, wb_ref, wc_ref, dgate_ref, dga_ref, dzb_ref, datt_ref,
             dwa_ref, dwb_ref, dwc_ref):
        @pl.when(pl.program_id(0) == 0)
        def _():
            dwa_ref[...] = jnp.zeros_like(dwa_ref)
            dwb_ref[...] = jnp.zeros_like(dwb_ref)
            dwc_ref[...] = jnp.zeros_like(dwc_ref)

        dmv = dm_ref[...]
        branches = ((ga_ref, wa_ref, dga_ref, dwa_ref), (zb_ref, wb_ref, dzb_ref, dwb_ref),
                    (att_ref, wc_ref, datt_ref, dwc_ref))
        for b, (x_ref, w_ref, dx_ref, dw_ref) in enumerate(branches):
            xv, wv = x_ref[...], w_ref[...]
            y = _dot(xv, wv)
            g = _sigmoid(gate_ref[:, b * D:(b + 1) * D])
            dgate_ref[:, b * D:(b + 1) * D] = (dmv * y * g * (1.0 - g)).astype(BF16)
            dy = (dmv * g).astype(BF16)
            dx_ref[...] = _dot(dy, wv, NT)
            dw_ref[...] += _dot(xv, dy, TN)

    return pl.pallas_call(
        body, name="merge_bwd", grid=(t // tm,),
        in_specs=[_rows(tm, D), _rows(tm, DM), _rows(tm, DM), _rows(tm, DM), _rows(tm, 3 * D, 0), wspec, wspec, wspec],
        out_specs=[_rows(tm, 3 * D), _rows(tm, DM), _rows(tm, DM), _rows(tm, DM), wspec, wspec, wspec],
        out_shape=[jax.ShapeDtypeStruct((t, 3 * D), BF16)] + [jax.ShapeDtypeStruct((t, DM), F32)] * 3
        + [jax.ShapeDtypeStruct((DM, D), F32)] * 3,
        compiler_params=_cparams(("arbitrary",)))(dm, ga, zb, att, proj, wa, wb, wc)


def _heads_layout(cum):
    t = cum.shape[0]
    c8 = cum[:, :NG]
    cumq = jnp.repeat(c8, HD, axis=1).reshape(t, NG // 2, LANE).transpose(1, 0, 2)
    cumk = c8.T.reshape(NG // 2, 2, t)
    return cumq, cumk


def _layer_fwd(x, mod, w):
    sh1, sc1, gt1, sh2, sc2, gt2 = (mod[k:k + 1] for k in range(NMOD))
    h1 = _norm_mod(x, w["mix_pre_g"], sc1, sh1, name="norm_mix")
    proj = _matmul(h1, w["w_in_p"], name="mm_proj", tk=D)
    ga = _gmlp_fwd(proj, w["gmlp_ln_g"], w["gmlp_ln_b"], w["wsm"], w["bsx"])
    zc, zb = _conv_fwd(proj, w["conv_w"], w["conv_b"], w["conv_ln_g"], w["conv_ln_b"])
    cum = _fox_cum(proj, w["bfp"])
    cumq, cumk = _heads_layout(cum)
    att, lse = _attn_fwd(proj, cumq, cumk)
    merged = _merge_fwd(ga, zb, att, proj, w["w_a_out"], w["w_b_out"], w["w_c_out"])
    y1 = _matmul(merged, w["w_out"], name="mm_out", tk=D)
    x2 = _resid(x, y1, gt1, w["mix_post_g"], name="resid_mix")
    h2 = _norm_mod(x2, w["mlp_pre_g"], sc2, sh2, name="norm_mlp")
    a, hid = _matmul(h2, w["mlp_w1"], name="mm_w1", tk=D, extra_out=(BF16,),
                     epilogue=lambda acc: (acc, jnp.square(jnp.maximum(acc, 0.0))))
    y2 = _matmul(hid, w["mlp_w2"], name="mm_w2", tk=1024)
    x3 = _resid(x2, y2, gt2, w["mlp_post_g"], name="resid_mlp")
    saved = dict(x=x, h1=h1, proj=proj, ga=ga, zc=zc, zb=zb, cumq=cumq, cumk=cumk, att=att, lse=lse, merged=merged,
                 y1=y1, x2=x2, h2=h2, a=a, hid=hid, y2=y2)
    return x3, saved


def _layer_bwd(dx3, mod, w, s):
    sh1, sc1, gt1, sh2, sc2, gt2 = (mod[k:k + 1] for k in range(NMOD))
    g = {}
    dy2, dgt2, g["mlp_post_g"] = _resid_bwd(dx3, s["y2"], gt2, w["mlp_post_g"], name="resid_mlp_bwd")
    da = _matmul(dy2, w["mlp_w2"], tb=True, name="mm_dhid", out_dtype=BF16, tk=D, extra=(s["a"],),
                 epilogue=lambda acc, a: (acc * (2.0 * jnp.maximum(a, 0.0)),))
    g["mlp_w2"] = _matmul(s["hid"], dy2, ta=True, name="mm_dw2")
    g["mlp_w1"] = _matmul(s["h2"], da, ta=True, name="mm_dw1")
    dh2 = _matmul(da, w["mlp_w1"], tb=True, name="mm_dh2")
    dx2, g["mlp_pre_g"], dsc2, dsh2 = _norm_bwd(dh2, dx3, s["x2"], w["mlp_pre_g"], sc2, name="norm_mlp_bwd")
    dy1, dgt1, g["mix_post_g"] = _resid_bwd(dx2, s["y1"], gt1, w["mix_post_g"], name="resid_mix_bwd")
    dmerged = _matmul(dy1, w["w_out"], tb=True, name="mm_dmerged", tk=D)
    g["w_out"] = _matmul(s["merged"], dy1, ta=True, name="mm_dwout")
    dgate, dga, dzb, datt, g["w_a_out"], g["w_b_out"], g["w_c_out"] = _merge_bwd(
        dmerged, s["ga"], s["zb"], s["att"], s["proj"], w["w_a_out"], w["w_b_out"], w["w_c_out"])
    duv, g["gmlp_ws"], dbs, g["gmlp_ln_g"], g["gmlp_ln_b"] = _gmlp_bwd(
        dga, s["proj"], w["gmlp_ln_g"], w["gmlp_ln_b"], w["wsm"], w["wsmt"], w["bsx"])
    g["gmlp_bs"] = dbs[:, :NG].T
    dzc, g["conv_ln_g"], g["conv_ln_b"] = _conv_bwd_ln(dzb, s["zc"], w["conv_ln_g"], w["conv_ln_b"])
    dglu, dcw, g["conv_b"] = _conv_bwd(dzc, s["proj"], w["conv_w"])
    g["conv_w"] = dcw[:KW]
    dq, dcq = _attn_bwd_dq(s["proj"], s["cumq"], s["cumk"], s["att"], datt, s["lse"])
    dk, dv, dck = _attn_bwd_dkv(s["proj"], s["cumq"], s["cumk"], s["att"], datt, s["lse"])
    t = dx3.shape[0]
    dcum8 = dck.reshape(NG, t).T + dcq[:, :, ::HD].transpose(1, 0, 2).reshape(t, NG)
    dcum = jnp.pad(dcum8, ((0, 0), (0, LANE - NG)))
    df, dbf = _fox_cum_bwd(dcum, s["proj"], w["bfp"])
    g["fox_bf"] = dbf[0, :NG]
    dproj = jnp.concatenate([dgate, duv, dglu, dq, dk, dv, df], axis=1)
    g["w_in_p"] = _matmul(s["h1"], dproj, ta=True, name="mm_dwin")
    dh1 = _matmul(dproj, w["w_in_p"], tb=True, name="mm_dh1")
    dx, g["mix_pre_g"], dsc1, dsh1 = _norm_bwd(dh1, dx2, s["x"], w["mix_pre_g"], sc1, name="norm_mix_bwd")
    dmod = jnp.concatenate([dsh1, dsc1, dgt1, dsh2, dsc2, dgt2], axis=0)
    return dx, g, dmod


def _position():
    return lax.axis_index("x"), lax.axis_index("y"), lax.axis_index("c")


def _all_gather8(v):
    m_per, n = v.shape

    def body(x_ref, out_ref, send_sems, recv_sems, local_sem):
        x, y, c = _position()
        me, sibling = (x, y, c), (x, y, 1 - c)
        chips = [(1 - x, y), (x, 1 - y), (1 - x, 1 - y)]

        def rows(px, py, pc):
            return out_ref.at[pl.ds((4 * px + 2 * py + pc) * m_per, m_per), :]

        def copy(k, block, to, src=None):
            return pltpu.make_async_remote_copy(
                src_ref=rows(*block) if src is None else src, dst_ref=rows(*block), send_sem=send_sems.at[k],
                recv_sem=recv_sems.at[k], device_id=to, device_id_type=MESH)

        mine = pltpu.make_async_copy(x_ref, rows(*me), local_sem)
        mine.start()
        first = [copy(0, me, sibling, src=x_ref)]
        first += [copy(1 + j, me, (*chip, c), src=x_ref) for j, chip in enumerate(chips)]
        for cp in first:
            cp.start()
        passed = [copy(4 + j, (*chip, c), sibling) for j, chip in enumerate(chips)]
        for j, chip in enumerate(chips):
            copy(1 + j, (*chip, c), me).wait_recv()
            passed[j].start()
        copy(0, sibling, me).wait_recv()
        for j, chip in enumerate(chips):
            copy(4 + j, (*chip, 1 - c), me).wait_recv()
        for cp in first + passed:
            cp.wait_send()
        mine.wait()

    out = pl.pallas_call(
        body, name="all_gather8", out_shape=jax.ShapeDtypeStruct((N_DEV * m_per, n), v.dtype),
        in_specs=[pl.BlockSpec(memory_space=pltpu.VMEM)], out_specs=pl.BlockSpec(memory_space=pltpu.VMEM),
        scratch_shapes=[pltpu.SemaphoreType.DMA((7,)), pltpu.SemaphoreType.DMA((7,)), pltpu.SemaphoreType.DMA],
        compiler_params=pltpu.CompilerParams(vmem_limit_bytes=VMEM_LIMIT),
    )(v)
    return out.reshape(N_DEV, m_per, n)


def _all_gather_chips(wp):
    _, r, n = wp.shape

    def body(w_ref, out_ref, send_sems, recv_sems, local_sem):
        x, y, c = _position()
        sibling = (x, y, 1 - c)
        chips = [(1 - x, y), (x, 1 - y), (1 - x, 1 - y)]

        def slab(px, py, layer):
            return out_ref.at[2 * px + py, layer]

        def copy(k, block, to, src=None):
            return pltpu.make_async_remote_copy(
                src_ref=slab(*block) if src is None else src, dst_ref=slab(*block), send_sem=send_sems.at[k],
                recv_sem=recv_sems.at[k], device_id=to, device_id_type=MESH)

        mine = pltpu.make_async_copy(w_ref, out_ref.at[2 * x + y], local_sem)
        mine.start()
        first = [copy(j, (x, y, c), (*chip, c), src=w_ref.at[c]) for j, chip in enumerate(chips)]
        for cp in first:
            cp.start()
        passed = [copy(3 + j, (*chip, c), sibling) for j, chip in enumerate(chips)]
        for j, chip in enumerate(chips):
            copy(j, (*chip, c), (x, y, c)).wait_recv()
            passed[j].start()
        for j, chip in enumerate(chips):
            copy(3 + j, (*chip, 1 - c), (x, y, c)).wait_recv()
        for cp in first + passed:
            cp.wait_send()
        mine.wait()

    return pl.pallas_call(
        body, name="all_gather_chips", out_shape=jax.ShapeDtypeStruct((N_CHIPS, 2, r, n), wp.dtype),
        in_specs=[ANY], out_specs=ANY,
        scratch_shapes=[pltpu.SemaphoreType.DMA((6,)), pltpu.SemaphoreType.DMA((6,)), pltpu.SemaphoreType.DMA],
    )(wp)


def _swap_layers(gp):
    nj, _, r, n = gp.shape

    def body(g_ref, out_ref, send_sems, recv_sems):
        x, y, c = _position()
        copies = [pltpu.make_async_remote_copy(
            src_ref=g_ref.at[j, 1 - c], dst_ref=out_ref.at[j], send_sem=send_sems.at[j], recv_sem=recv_sems.at[j],
            device_id=(x, y, 1 - c), device_id_type=MESH) for j in range(nj)]
        for cp in copies:
            cp.start()
        for cp in copies:
            cp.wait()

    return pl.pallas_call(
        body, name="rs_swap_layers", out_shape=jax.ShapeDtypeStruct((nj, r, n), gp.dtype), in_specs=[ANY], out_specs=ANY,
        scratch_shapes=[pltpu.SemaphoreType.DMA((nj,)), pltpu.SemaphoreType.DMA((nj,))],
    )(gp)


def _scatter_chips(sp):
    _, r, n = sp.shape

    def body(s_ref, out_ref, send_sems, recv_sems):
        x, y, c = _position()
        chips = [(1 - x, y), (x, 1 - y), (1 - x, 1 - y)]
        copies = [pltpu.make_async_remote_copy(
            src_ref=s_ref.at[2 * px + py], dst_ref=out_ref.at[k], send_sem=send_sems.at[k], recv_sem=recv_sems.at[k],
            device_id=(px, py, c), device_id_type=MESH) for k, (px, py) in enumerate(chips)]
        for cp in copies:
            cp.start()
        for cp in copies:
            cp.wait()

    return pl.pallas_call(
        body, name="rs_scatter_chips", out_shape=jax.ShapeDtypeStruct((3, r, n), sp.dtype), in_specs=[ANY], out_specs=ANY,
        scratch_shapes=[pltpu.SemaphoreType.DMA((3,)), pltpu.SemaphoreType.DMA((3,))],
    )(sp)


def _share_layers(red):
    r, n = red.shape

    def body(r_ref, out_ref, send_sem, recv_sem, local_sem):
        x, y, c = _position()
        mine = pltpu.make_async_copy(r_ref, out_ref.at[c], local_sem)
        mine.start()
        cp = pltpu.make_async_remote_copy(src_ref=r_ref, dst_ref=out_ref.at[c], send_sem=send_sem, recv_sem=recv_sem,
                                          device_id=(x, y, 1 - c), device_id_type=MESH)
        cp.start()
        pltpu.make_async_remote_copy(src_ref=r_ref, dst_ref=out_ref.at[1 - c], send_sem=send_sem, recv_sem=recv_sem,
                                     device_id=(x, y, 1 - c), device_id_type=MESH).wait_recv()
        cp.wait_send()
        mine.wait()

    return pl.pallas_call(
        body, name="rs_share_layers", out_shape=jax.ShapeDtypeStruct((2, r, n), red.dtype), in_specs=[ANY], out_specs=ANY,
        scratch_shapes=[pltpu.SemaphoreType.DMA, pltpu.SemaphoreType.DMA, pltpu.SemaphoreType.DMA],
    )(red)


def _add_own_layer(gp, recv, c):
    nj, _, r, n = gp.shape
    tr = PACK_TR

    def body(c_ref, g_ref, r_ref, o_ref):
        o_ref[0] = g_ref[0, 0] + r_ref[0]

    gs = pltpu.PrefetchScalarGridSpec(
        num_scalar_prefetch=1, grid=(nj, r // tr),
        in_specs=[pl.BlockSpec((1, 1, tr, n), lambda j, i, cc: (j, cc[0], i, 0)),
                  pl.BlockSpec((1, tr, n), lambda j, i, cc: (j, i, 0))],
        out_specs=pl.BlockSpec((1, tr, n), lambda j, i, cc: (j, i, 0)))
    return pl.pallas_call(body, name="rs_add_own_layer", grid_spec=gs, out_shape=jax.ShapeDtypeStruct((nj, r, n), F32),
                          compiler_params=_cparams(("parallel", "parallel")))(jnp.reshape(c, (1,)).astype(I32), gp, recv)


def _add_own_chip(sp, recv, j):
    _, r, n = sp.shape
    tr = PACK_TR

    def body(j_ref, s_ref, r_ref, o_ref):
        o_ref[...] = ((s_ref[0] + r_ref[0]) + r_ref[1]) + r_ref[2]

    gs = pltpu.PrefetchScalarGridSpec(
        num_scalar_prefetch=1, grid=(r // tr,),
        in_specs=[pl.BlockSpec((1, tr, n), lambda i, jj: (jj[0], i, 0)), pl.BlockSpec((3, tr, n), lambda i, jj: (0, i, 0))],
        out_specs=pl.BlockSpec((tr, n), lambda i, jj: (i, 0)))
    return pl.pallas_call(body, name="rs_add_own_chip", grid_spec=gs, out_shape=jax.ShapeDtypeStruct((r, n), F32),
                          compiler_params=_cparams(("parallel",)))(jnp.reshape(j, (1,)).astype(I32), sp, recv)


def _sum8(v):
    _, m, n = v.shape

    def body(v_ref, o_ref):
        acc = v_ref[0]
        for k in range(1, N_DEV):
            acc = acc + v_ref[k]
        o_ref[...] = acc

    return pl.pallas_call(body, name="sum8", grid=(1,), in_specs=[pl.BlockSpec((N_DEV, m, n), lambda i: (0, 0, 0))],
                          out_specs=pl.BlockSpec((m, n), lambda i: (0, 0)), out_shape=jax.ShapeDtypeStruct((m, n), F32),
                          compiler_params=_cparams(("arbitrary",)))(v)


def _ada_mod(c_all, ada_w, ada_b_loc, *, tn=512):
    nl, _, ncol = ada_w.shape

    def body(c_ref, w_ref, b_ref, o_ref):
        cv = c_ref[...]
        ca = (cv * _sigmoid(cv)).astype(BF16)
        o_ref[0] = _dot(ca, w_ref[0].astype(BF16)) + b_ref[0]

    return pl.pallas_call(
        body, name="ada_mod", grid=(nl, ncol // tn),
        in_specs=[pl.BlockSpec((N_DEV, D), lambda l, j: (0, 0)), pl.BlockSpec((1, D, tn), lambda l, j: (l, 0, j)),
                  pl.BlockSpec((1, 1, tn), lambda l, j: (l, 0, j))],
        out_specs=pl.BlockSpec((1, N_DEV, tn), lambda l, j: (l, 0, j)),
        out_shape=jax.ShapeDtypeStruct((nl, N_DEV, ncol), F32),
        compiler_params=_cparams(("parallel", "parallel")))(c_all, ada_w, ada_b_loc)


def _ada_grad(c_pad, dmod_pad, *, tn=512):
    nl, nb, ncol = dmod_pad.shape

    def body(c_ref, d_ref, o_ref):
        cv = c_ref[...]
        ca = (cv * _sigmoid(cv)).astype(BF16)
        o_ref[0] = _dot(ca, d_ref[0].astype(BF16), TN)

    return pl.pallas_call(
        body, name="ada_grad", grid=(nl, ncol // tn),
        in_specs=[pl.BlockSpec((nb, D), lambda l, j: (0, 0)), pl.BlockSpec((1, nb, tn), lambda l, j: (l, 0, j))],
        out_specs=pl.BlockSpec((1, D, tn), lambda l, j: (l, 0, j)),
        out_shape=jax.ShapeDtypeStruct((nl, D, ncol), F32),
        compiler_params=_cparams(("parallel", "parallel")))(c_pad, dmod_pad)


def _adamw(w, g, m, v, *, name):
    shape = w.shape
    if w.ndim == 3:
        lead, rows, cols = shape
    else:
        lead, (rows, cols) = 1, shape
    w3, g3, m3, v3 = (a.reshape(lead, rows, cols) for a in (w, g, m, v))
    tr = rows
    if rows * cols * 4 > (2 << 20):
        tr = next(cand for cand in (256, 128, 64, 8) if rows % cand == 0)
    c1 = 1.0 - ADAM_B1 ** ADAM_STEP
    c2 = 1.0 - ADAM_B2 ** ADAM_STEP

    def body(w_ref, g_ref, m_ref, v_ref, d_ref, nm_ref, nv_ref):
        gv = g_ref[...]
        nm = ADAM_B1 * m_ref[...] + (1.0 - ADAM_B1) * gv
        nv = ADAM_B2 * v_ref[...] + (1.0 - ADAM_B2) * (gv * gv)
        nm_ref[...] = nm
        nv_ref[...] = nv
        d_ref[...] = -ADAM_LR * ((nm / c1) / (jnp.sqrt(nv / c2) + ADAM_EPS) + ADAM_WD * w_ref[...])

    spec = pl.BlockSpec((1, tr, cols), lambda l, i: (l, i, 0))
    outs = pl.pallas_call(
        body, name=name, grid=(lead, rows // tr), in_specs=[spec] * 4, out_specs=[spec] * 3,
        out_shape=[jax.ShapeDtypeStruct((lead, rows, cols), F32)] * 3,
        compiler_params=_cparams(("parallel", "parallel")))(w3, g3, m3, v3)
    return tuple(o.reshape(shape) for o in outs)


SMALL = (("mix_pre_g", (2, D)), ("mix_post_g", (2, D)), ("mlp_pre_g", (2, D)), ("mlp_post_g", (2, D)),
         ("gmlp_ln_g", (2, DM)), ("gmlp_ln_b", (2, DM)), ("gmlp_ws", (2, NG, CH, CH)), ("gmlp_bs", (2, NG, CH)),
         ("conv_b", (2, DM)), ("conv_ln_g", (2, DM)), ("conv_ln_b", (2, DM)), ("fox_bf", (2, NG)))


def _pack_rows(arrays, mult=8):
    flat = []
    for a in arrays:
        f = a.reshape(-1).astype(F32)
        pad = (-f.shape[0]) % LANE
        flat.append(jnp.pad(f, (0, pad)) if pad else f)
    cat = jnp.concatenate(flat)
    rows = cat.shape[0] // LANE
    pad_rows = (-rows) % mult
    if pad_rows:
        cat = jnp.pad(cat, (0, pad_rows * LANE))
    return cat.reshape(-1, LANE)


def _unpack_rows(buf, shapes):
    flat = buf.reshape(-1)
    out, off = [], 0
    for shp in shapes:
        size = 1
        for d in shp:
            size *= d
        out.append(flat[off:off + size].reshape(shp))
        off += size + ((-size) % LANE)
    return out


def _assemble_w_in(w_in_full):
    uv_glu_qkv = w_in_full[:, :3584]
    f = w_in_full[:, 3584:3592]
    gate = w_in_full[:, 3592:]
    fpad = jnp.zeros((D, D_INP - C_F - NG), w_in_full.dtype)
    return jnp.concatenate([gate, uv_glu_qkv, f, fpad], axis=1)


def _disassemble_w_in(g_p):
    return jnp.concatenate([g_p[:, C_UV:C_F], g_p[:, C_F:C_F + NG], g_p[:, :C_UV]], axis=1)


def kernel(x, c, ada_w, ada_b, mix_pre_g, mix_post_g, mlp_pre_g, mlp_post_g, w_in, gmlp_ln_g, gmlp_ln_b, gmlp_ws, gmlp_bs, w_a_out, conv_w, conv_b, conv_ln_g, conv_ln_b, w_b_out, fox_bf, w_c_out, w_out, mlp_w1, mlp_w2, loss_target, m_ada_w, m_ada_b, m_mix_pre_g, m_mix_post_g, m_mlp_pre_g, m_mlp_post_g, m_w_in, m_gmlp_ln_g, m_gmlp_ln_b, m_gmlp_ws, m_gmlp_bs, m_w_a_out, m_conv_w, m_conv_b, m_conv_ln_g, m_conv_ln_b, m_w_b_out, m_fox_bf, m_w_c_out, m_w_out, m_mlp_w1, m_mlp_w2, v_ada_w, v_ada_b, v_mix_pre_g, v_mix_post_g, v_mlp_pre_g, v_mlp_post_g, v_w_in, v_gmlp_ln_g, v_gmlp_ln_b, v_gmlp_ws, v_gmlp_bs, v_w_a_out, v_conv_w, v_conv_b, v_conv_ln_g, v_conv_ln_b, v_w_b_out, v_fox_bf, v_w_c_out, v_w_out, v_mlp_w1, v_mlp_w2):
    weights = dict(ada_w=ada_w, ada_b=ada_b, mix_pre_g=mix_pre_g, mix_post_g=mix_post_g, mlp_pre_g=mlp_pre_g,
                   mlp_post_g=mlp_post_g, w_in=w_in, gmlp_ln_g=gmlp_ln_g, gmlp_ln_b=gmlp_ln_b, gmlp_ws=gmlp_ws,
                   gmlp_bs=gmlp_bs, w_a_out=w_a_out, conv_w=conv_w, conv_b=conv_b, conv_ln_g=conv_ln_g,
                   conv_ln_b=conv_ln_b, w_b_out=w_b_out, fox_bf=fox_bf, w_c_out=w_c_out, w_out=w_out, mlp_w1=mlp_w1,
                   mlp_w2=mlp_w2)
    mom_m = dict(ada_w=m_ada_w, ada_b=m_ada_b, mix_pre_g=m_mix_pre_g, mix_post_g=m_mix_post_g, mlp_pre_g=m_mlp_pre_g,
                 mlp_post_g=m_mlp_post_g, w_in=m_w_in, gmlp_ln_g=m_gmlp_ln_g, gmlp_ln_b=m_gmlp_ln_b, gmlp_ws=m_gmlp_ws,
                 gmlp_bs=m_gmlp_bs, w_a_out=m_w_a_out, conv_w=m_conv_w, conv_b=m_conv_b, conv_ln_g=m_conv_ln_g,
                 conv_ln_b=m_conv_ln_b, w_b_out=m_w_b_out, fox_bf=m_fox_bf, w_c_out=m_w_c_out, w_out=m_w_out,
                 mlp_w1=m_mlp_w1, mlp_w2=m_mlp_w2)
    mom_v = dict(ada_w=v_ada_w, ada_b=v_ada_b, mix_pre_g=v_mix_pre_g, mix_post_g=v_mix_post_g, mlp_pre_g=v_mlp_pre_g,
                 mlp_post_g=v_mlp_post_g, w_in=v_w_in, gmlp_ln_g=v_gmlp_ln_g, gmlp_ln_b=v_gmlp_ln_b, gmlp_ws=v_gmlp_ws,
                 gmlp_bs=v_gmlp_bs, w_a_out=v_w_a_out, conv_w=v_conv_w, conv_b=v_conv_b, conv_ln_g=v_conv_ln_g,
                 conv_ln_b=v_conv_ln_b, w_b_out=v_w_b_out, fox_bf=v_fox_bf, w_c_out=v_w_c_out, w_out=v_w_out,
                 mlp_w1=v_mlp_w1, mlp_w2=v_mlp_w2)
    order = list(weights)
    px, py, pc = _position()
    chip = 2 * px + py
    dev = 2 * chip + pc
    depth = ada_w.shape[0]
    t = x.shape[1]
    xl = x.reshape(t, D)
    tgt = loss_target.reshape(t, D)

    small_in = _pack_rows([c, conv_w])
    gathered = _all_gather8(small_in)
    c_all = gathered[:, :D // LANE, :].reshape(N_DEV, D)
    cw_rows = depth * KW * LANE // LANE
    conv_w_full = jnp.concatenate(
        [gathered[2 * j, D // LANE:D // LANE + cw_rows, :].reshape(depth, KW, LANE) for j in range(N_CHIPS)], axis=2)

    ncol = ada_w.shape[2]
    ada_b_loc = lax.dynamic_slice_in_dim(ada_b, chip * ncol, ncol, axis=1).reshape(depth, 1, ncol)
    mod_sh = _ada_mod(c_all, ada_w, ada_b_loc)
    mod_g = _all_gather8(mod_sh.reshape(-1, LANE)).reshape(N_DEV, depth, N_DEV, ncol)
    mod_all = jnp.concatenate([mod_g[2 * j] for j in range(N_CHIPS)], axis=2)
    mod_mine = lax.dynamic_index_in_dim(mod_all, dev, axis=1, keepdims=False)

    packed = jnp.stack([jnp.concatenate([weights[n][l].astype(BF16).reshape(-1) for n, _, _, _ in BIG]).reshape(-1, LANE)
                        for l in range(depth)])
    gath = _all_gather_chips(packed)
    layers = []
    for l in range(depth):
        w = {}
        off = 0
        for n, r, cdim, ax in BIG:
            rows = r * cdim // LANE
            seg = gath[:, l, off:off + rows, :].reshape(N_CHIPS, r, cdim)
            off += rows
            w[n] = seg.transpose(1, 0, 2).reshape(r, N_CHIPS * cdim) if ax == 1 else seg.reshape(N_CHIPS * r, cdim)
        w["w_in_p"] = _assemble_w_in(w.pop("w_in"))
        for n in ("mix_pre_g", "mix_post_g", "mlp_pre_g", "mlp_post_g", "gmlp_ln_g", "gmlp_ln_b", "conv_b", "conv_ln_g",
                  "conv_ln_b"):
            w[n] = weights[n][l:l + 1]
        tril = jnp.tril(jnp.ones((CH, CH), F32))
        wsm = gmlp_ws[l] * tril
        w["wsm"] = wsm.astype(BF16)
        w["wsmt"] = jnp.swapaxes(wsm, 1, 2).astype(BF16)
        w["bsx"] = jnp.repeat(gmlp_bs[l].T, HD, axis=1)
        w["conv_w"] = conv_w_full[l]
        w["bfp"] = jnp.pad(fox_bf[l], (0, LANE - NG)).reshape(1, LANE)
        layers.append(w)

    xs, saved = xl, []
    for l in range(depth):
        xs, s = _layer_fwd(xs, mod_mine[l].reshape(NMOD, D), layers[l])
        saved.append(s)
    loss_local, dx = _loss_and_grad(xs, tgt)
    loss = lax.psum(loss_local, ("x", "y", "c"))
    grads, dmods = [None] * depth, [None] * depth
    for l in reversed(range(depth)):
        dx, grads[l], dmods[l] = _layer_bwd(dx, mod_mine[l].reshape(NMOD, D), layers[l], saved[l])
    grad_x = dx.reshape(x.shape)

    def shard_of(gfull, j, r, cdim, ax):
        return gfull[:, j * cdim:(j + 1) * cdim] if ax == 1 else gfull[j * r:(j + 1) * r]

    for l in range(depth):
        grads[l]["w_in"] = _disassemble_w_in(grads[l].pop("w_in_p"))
    gp = jnp.stack([jnp.stack([jnp.concatenate([shard_of(grads[l][n], j, r, cdim, ax).reshape(-1)
                                                for n, r, cdim, ax in BIG]).reshape(-1, LANE)
                               for l in range(depth)]) for j in range(N_CHIPS)])
    from_sibling = _swap_layers(gp)
    chip_sum = _add_own_layer(gp, from_sibling, pc)
    from_chips = _scatter_chips(chip_sum)
    reduced = _add_own_chip(chip_sum, from_chips, chip)
    gbig = _share_layers(reduced)
    g_out = {}
    off = 0
    for n, r, cdim, _ in BIG:
        rows = r * cdim // LANE
        g_out[n] = gbig[:, off:off + rows, :].reshape(depth, r, cdim)
        off += rows

    small_names = [n for n, _ in SMALL]
    small_list = [jnp.stack(dmods)] + [jnp.stack([grads[l][n] for l in range(depth)]) for n in small_names]
    small_list.append(jnp.stack([grads[l]["conv_w"] for l in range(depth)]))
    small_shapes = [(depth, NMOD * D)] + [shp for _, shp in SMALL] + [(depth, KW, DM)]
    small_pack = _pack_rows(small_list)
    small_all = _all_gather8(small_pack)
    small_sum = _unpack_rows(_sum8(small_all), small_shapes)
    g_out["ada_b"] = small_sum[0]
    for n, gs in zip(small_names, small_sum[1:-1]):
        g_out[n] = gs
    g_out["conv_w"] = lax.dynamic_slice_in_dim(small_sum[-1], chip * LANE, LANE, axis=2)
    dmod_all = small_all[:, :depth * NMOD * D // LANE, :].reshape(N_DEV, depth, NMOD * D)
    dmod_loc = lax.dynamic_slice_in_dim(dmod_all, chip * ncol, ncol, axis=2).transpose(1, 0, 2)
    g_out["ada_w"] = _ada_grad(jnp.pad(c_all, ((0, 8), (0, 0))), jnp.pad(dmod_loc, ((0, 0), (0, 8), (0, 0))))

    delta, new_m, new_v = {}, {}, {}
    for n in ("ada_w",) + tuple(b[0] for b in BIG):
        delta[n], new_m[n], new_v[n] = _adamw(weights[n], g_out[n], mom_m[n], mom_v[n], name="adamw_" + n)
    small_params = ["ada_b"] + small_names + ["conv_w"]
    packs = [_pack_rows([d[n] for n in small_params]) for d in (weights, g_out, mom_m, mom_v)]
    outs = _adamw(*packs, name="adamw_small")
    shapes = [weights[n].shape for n in small_params]
    for dst, buf in zip((delta, new_m, new_v), outs):
        for n, a in zip(small_params, _unpack_rows(buf, shapes)):
            dst[n] = a

    return (loss, grad_x, *[g_out[n] for n in order], *[delta[n] for n in order], *[new_m[n] for n in order],
            *[new_v[n] for n in order])
```

```python
import functools

import jax
import jax.numpy as jnp
from jax import lax
from jax.experimental import pallas as pl
from jax.experimental.pallas import tpu as pltpu

F32 = jnp.float32
BF16 = jnp.bfloat16
I32 = jnp.int32
MESH = pl.DeviceIdType.MESH
ANY = pl.BlockSpec(memory_space=pl.ANY)

D = 1024
DM = 512
NG = 8
CH = 128
KW = 31
HALO = 32
DFF = 4096
NMOD = 6
EPS = 1e-6
LANE = 128
N_CHIPS = 4
N_DEV = 8
C_GATE, C_UV, C_GLU, C_Q, C_K, C_V, C_F, D_INP = 0, 3072, 4096, 5120, 5632, 6144, 6656, 7168
D_IN = 6664
VMEM_LIMIT = 56 * 1024 * 1024

ADAM_LR, ADAM_B1, ADAM_B2, ADAM_EPS, ADAM_WD, ADAM_STEP = 0.001, 0.9, 0.999, 1e-08, 0.01, 10

BIG = (("w_in", 1024, 1666, 1), ("w_a_out", 512, 256, 1), ("w_b_out", 512, 256, 1), ("w_c_out", 512, 256, 1),
       ("w_out", 256, 1024, 0), ("mlp_w1", 1024, 1024, 1), ("mlp_w2", 1024, 1024, 0))
PACK_ROWS = sum(r * c for _, r, c, _ in BIG) // LANE
PACK_TR = 4976


def _cparams(sem):
    return pltpu.CompilerParams(dimension_semantics=sem, vmem_limit_bytes=VMEM_LIMIT)


def _sigmoid(x):
    return jax.nn.sigmoid(x)


_GELU_K = 0.7978845608028654
_GELU_A = 0.044715


def _gelu(x):
    t = jnp.tanh(_GELU_K * (x + _GELU_A * x * x * x))
    return 0.5 * x * (1.0 + t)


def _gelu_grad(x):
    t = jnp.tanh(_GELU_K * (x + _GELU_A * x * x * x))
    return 0.5 * (1.0 + t) + 0.5 * x * (1.0 - t * t) * _GELU_K * (1.0 + 3.0 * _GELU_A * x * x)


def _mean(x):
    return jnp.mean(x, axis=-1, keepdims=True)


def _colsum(x):
    return jnp.sum(x, axis=0, keepdims=True)


def _dot(a, b, dims=((1,), (0,))):
    return lax.dot_general(a, b, (dims, ((), ())), preferred_element_type=F32)


NN = ((1,), (0,))
NT = ((1,), (1,))
TN = ((0,), (0,))


def _matmul(a, b, *, name, ta=False, tb=False, out_dtype=F32, tm=1024, tn=1024, tk=1024, epilogue=None, extra=(),
            extra_out=()):
    m, k = (a.shape[1], a.shape[0]) if ta else a.shape
    n = b.shape[0] if tb else b.shape[1]
    tm, tn, tk = min(tm, m), min(tn, n), min(tk, k)
    assert m % tm == 0 and n % tn == 0 and k % tk == 0, (name, m, n, k, tm, tn, tk)
    nk = k // tk
    dims = ((0 if ta else 1,), (1 if tb else 0,))
    n_extra = len(extra)
    out_dtypes = (out_dtype,) + tuple(extra_out)

    def body(a_ref, b_ref, *rest):
        extra_refs = rest[:n_extra]
        out_refs = rest[n_extra:n_extra + len(out_dtypes)]
        kk = pl.program_id(2)
        part = _dot(a_ref[...].astype(BF16), b_ref[...].astype(BF16), dims)

        def finish(acc):
            outs = (acc,) if epilogue is None else epilogue(acc, *[r[...] for r in extra_refs])
            for o_ref, o in zip(out_refs, outs):
                o_ref[...] = o.astype(o_ref.dtype)

        if nk == 1:
            finish(part)
        else:
            acc_ref = rest[-1]

            @pl.when(kk == 0)
            def _():
                acc_ref[...] = part

            @pl.when(jnp.logical_and(kk > 0, kk < nk - 1))
            def _():
                acc_ref[...] += part

            @pl.when(kk == nk - 1)
            def _():
                finish(acc_ref[...] + part)

    a_spec = pl.BlockSpec((tk, tm), lambda i, j, kk: (kk, i)) if ta else pl.BlockSpec((tm, tk), lambda i, j, kk: (i, kk))
    b_spec = pl.BlockSpec((tn, tk), lambda i, j, kk: (j, kk)) if tb else pl.BlockSpec((tk, tn), lambda i, j, kk: (kk, j))
    o_spec = pl.BlockSpec((tm, tn), lambda i, j, kk: (i, j))
    outs = pl.pallas_call(
        body, name=name, grid=(m // tm, n // tn, nk),
        in_specs=[a_spec, b_spec] + [o_spec] * n_extra,
        out_specs=[o_spec] * len(out_dtypes),
        out_shape=[jax.ShapeDtypeStruct((m, n), dt) for dt in out_dtypes],
        scratch_shapes=[pltpu.VMEM((tm, tn), F32)] if nk > 1 else [],
        compiler_params=_cparams(("parallel", "parallel", "arbitrary")),
    )(a, b, *extra)
    return outs[0] if len(outs) == 1 else outs


def _rows(tm, n, col=0):
    return pl.BlockSpec((tm, n), lambda i: (i, col))


def _vec(n):
    return pl.BlockSpec((1, n), lambda i: (0, 0))


def _norm_mod(x, g, sc, sh, *, name, tm=256):
    t = x.shape[0]
    tm = min(tm, t)

    def body(x_ref, g_ref, sc_ref, sh_ref, h_ref):
        xv = x_ref[...]
        inv = lax.rsqrt(_mean(xv * xv) + EPS)
        h_ref[...] = ((xv * inv * g_ref[...]) * (1.0 + sc_ref[...]) + sh_ref[...]).astype(BF16)

    return pl.pallas_call(
        body, name=name, grid=(t // tm,), in_specs=[_rows(tm, D), _vec(D), _vec(D), _vec(D)],
        out_specs=_rows(tm, D), out_shape=jax.ShapeDtypeStruct((t, D), BF16),
        compiler_params=_cparams(("parallel",)))(x, g, sc, sh)


def _resid(x, y, gt, gp, *, name, tm=256):
    t = x.shape[0]
    tm = min(tm, t)

    def body(x_ref, y_ref, gt_ref, gp_ref, o_ref):
        yv = y_ref[...]
        inv = lax.rsqrt(_mean(yv * yv) + EPS)
        o_ref[...] = x_ref[...] + gt_ref[...] * (yv * inv * gp_ref[...])

    return pl.pallas_call(
        body, name=name, grid=(t // tm,), in_specs=[_rows(tm, D), _rows(tm, D), _vec(D), _vec(D)],
        out_specs=_rows(tm, D), out_shape=jax.ShapeDtypeStruct((t, D), F32),
        compiler_params=_cparams(("parallel",)))(x, y, gt, gp)


def _resid_bwd(dx, y, gt, gp, *, name, tm=256):
    t = dx.shape[0]
    tm = min(tm, t)

    def body(dx_ref, y_ref, gt_ref, gp_ref, dy_ref, dgt_ref, dgp_ref):
        @pl.when(pl.program_id(0) == 0)
        def _():
            dgt_ref[...] = jnp.zeros_like(dgt_ref)
            dgp_ref[...] = jnp.zeros_like(dgp_ref)

        dxv, yv, gp_v = dx_ref[...], y_ref[...], gp_ref[...]
        inv = lax.rsqrt(_mean(yv * yv) + EPS)
        yh = yv * inv
        dgt_ref[...] += _colsum(dxv * (yh * gp_v))
        dr = dxv * gt_ref[...]
        dgp_ref[...] += _colsum(dr * yh)
        dyn = dr * gp_v
        dy_ref[...] = (inv * (dyn - yh * _mean(dyn * yh))).astype(BF16)

    return pl.pallas_call(
        body, name=name, grid=(t // tm,), in_specs=[_rows(tm, D), _rows(tm, D), _vec(D), _vec(D)],
        out_specs=[_rows(tm, D), _vec(D), _vec(D)],
        out_shape=[jax.ShapeDtypeStruct((t, D), BF16), jax.ShapeDtypeStruct((1, D), F32),
                   jax.ShapeDtypeStruct((1, D), F32)],
        compiler_params=_cparams(("arbitrary",)))(dx, y, gt, gp)


def _norm_bwd(dh, dx_res, x, g, sc, *, name, tm=256):
    t = dh.shape[0]
    tm = min(tm, t)

    def body(dh_ref, dxr_ref, x_ref, g_ref, sc_ref, dx_ref, dg_ref, dsc_ref, dsh_ref):
        @pl.when(pl.program_id(0) == 0)
        def _():
            dg_ref[...] = jnp.zeros_like(dg_ref)
            dsc_ref[...] = jnp.zeros_like(dsc_ref)
            dsh_ref[...] = jnp.zeros_like(dsh_ref)

        dhv, xv, gv = dh_ref[...], x_ref[...], g_ref[...]
        inv = lax.rsqrt(_mean(xv * xv) + EPS)
        xh = xv * inv
        dsh_ref[...] += _colsum(dhv)
        dsc_ref[...] += _colsum(dhv * (xh * gv))
        dn = dhv * (1.0 + sc_ref[...])
        dg_ref[...] += _colsum(dn * xh)
        dxh = dn * gv
        dx_ref[...] = inv * (dxh - xh * _mean(dxh * xh)) + dxr_ref[...]

    vec_out = jax.ShapeDtypeStruct((1, D), F32)
    return pl.pallas_call(
        body, name=name, grid=(t // tm,), in_specs=[_rows(tm, D), _rows(tm, D), _rows(tm, D), _vec(D), _vec(D)],
        out_specs=[_rows(tm, D), _vec(D), _vec(D), _vec(D)],
        out_shape=[jax.ShapeDtypeStruct((t, D), F32), vec_out, vec_out, vec_out],
        compiler_params=_cparams(("arbitrary",)))(dh, dx_res, x, g, sc)


def _loss_and_grad(x, target, *, tm=256):
    t = x.shape[0]
    tm = min(tm, t)

    def body(x_ref, t_ref, loss_ref, dx_ref):
        @pl.when(pl.program_id(0) == 0)
        def _():
            loss_ref[...] = jnp.zeros_like(loss_ref)

        e = x_ref[...] - t_ref[...]
        dx_ref[...] = e * (1.0 / D)
        s = jnp.sum(jnp.sum(e * e, axis=1, keepdims=True), axis=0, keepdims=True) * (0.5 / D)
        loss_ref[...] += jnp.broadcast_to(s, loss_ref.shape)

    loss, dx = pl.pallas_call(
        body, name="loss", grid=(t // tm,), in_specs=[_rows(tm, D), _rows(tm, D)],
        out_specs=[pl.BlockSpec((8, LANE), lambda i: (0, 0)), _rows(tm, D)],
        out_shape=[jax.ShapeDtypeStruct((8, LANE), F32), jax.ShapeDtypeStruct((t, D), F32)],
        compiler_params=_cparams(("arbitrary",)))(x, target)
    return loss[0, 0], dx


def _gmlp_core(uv, lng, lnb, ws_ref, bsx):
    tm = uv.shape[0]
    gu = _gelu(uv[:, :DM])
    gv = _gelu(uv[:, DM:])
    mu = _mean(gv)
    vc = gv - mu
    rstd = lax.rsqrt(_mean(vc * vc) + EPS)
    vh = vc * rstd
    vln = vh * lng + lnb
    lane = lax.broadcasted_iota(I32, (CH, LANE), 1)
    sv_rows = []
    for nchunk in range(tm // CH):
        vb = vln[nchunk * CH:(nchunk + 1) * CH].astype(BF16)
        cols = []
        for cb in range(DM // LANE):
            vcb = vb[:, cb * LANE:(cb + 1) * LANE]
            lo = _dot(ws_ref[2 * cb], vcb)
            hi = _dot(ws_ref[2 * cb + 1], vcb)
            cols.append(jnp.where(lane < 64, lo, hi))
        sv_rows.append(jnp.concatenate(cols, axis=1) + bsx)
    sv = jnp.concatenate(sv_rows, axis=0) if len(sv_rows) > 1 else sv_rows[0]
    return gu, vh, rstd, vln, sv


def _gmlp_fwd(proj, lng, lnb, wsm, bsx, *, tm=256):
    t = proj.shape[0]
    tm = min(tm, t)

    def body(uv_ref, lng_ref, lnb_ref, ws_ref, bs_ref, ga_ref):
        gu, _, _, _, sv = _gmlp_core(uv_ref[...], lng_ref[...], lnb_ref[...], ws_ref, bs_ref[...])
        ga_ref[...] = (gu * sv).astype(BF16)

    return pl.pallas_call(
        body, name="gmlp_fwd", grid=(t // tm,),
        in_specs=[_rows(tm, 2 * DM, C_UV // (2 * DM)), _vec(DM), _vec(DM),
                  pl.BlockSpec((NG, CH, CH), lambda i: (0, 0, 0)), pl.BlockSpec((CH, DM), lambda i: (0, 0))],
        out_specs=_rows(tm, DM), out_shape=jax.ShapeDtypeStruct((t, DM), BF16),
        compiler_params=_cparams(("parallel",)))(proj, lng, lnb, wsm, bsx)


def _gmlp_bwd(dga, proj, lng, lnb, wsm, wsmt, bsx, *, tm=256):
    t = proj.shape[0]
    tm = min(tm, t)

    def body(dga_ref, uv_ref, lng_ref, lnb_ref, ws_ref, wst_ref, bs_ref, duv_ref, dws_ref, dbs_ref, dlng_ref, dlnb_ref,
             dbsx_ref):
        i = pl.program_id(0)

        @pl.when(i == 0)
        def _():
            dws_ref[...] = jnp.zeros_like(dws_ref)
            dbsx_ref[...] = jnp.zeros_like(dbsx_ref)
            dlng_ref[...] = jnp.zeros_like(dlng_ref)
            dlnb_ref[...] = jnp.zeros_like(dlnb_ref)

        uv = uv_ref[...]
        lng_v = lng_ref[...]
        gu, vh, rstd, vln, sv = _gmlp_core(uv, lng_v, lnb_ref[...], ws_ref, bs_ref[...])
        dga_v = dga_ref[...]
        dgu = dga_v * sv
        dsv = dga_v * gu
        lane = lax.broadcasted_iota(I32, (CH, LANE), 1)
        tril = lax.broadcasted_iota(I32, (CH, CH), 0) >= lax.broadcasted_iota(I32, (CH, CH), 1)
        dvln_rows = []
        for nchunk in range(tm // CH):
            rows = slice(nchunk * CH, (nchunk + 1) * CH)
            dbsx_ref[...] += dsv[rows]
            vb = vln[rows].astype(BF16)
            cols = []
            for cb in range(DM // LANE):
                cs = slice(cb * LANE, (cb + 1) * LANE)
                dsvb = dsv[rows, cs]
                vcb = vb[:, cs]
                dlo = jnp.where(lane < 64, dsvb, 0.0).astype(BF16)
                dhi = jnp.where(lane < 64, 0.0, dsvb).astype(BF16)
                dws_ref[2 * cb] += jnp.where(tril, _dot(dlo, vcb, NT), 0.0)
                dws_ref[2 * cb + 1] += jnp.where(tril, _dot(dhi, vcb, NT), 0.0)
                dsb = dsvb.astype(BF16)
                cols.append(jnp.where(lane < 64, _dot(wst_ref[2 * cb], dsb), _dot(wst_ref[2 * cb + 1], dsb)))
            dvln_rows.append(jnp.concatenate(cols, axis=1))
        dvln = jnp.concatenate(dvln_rows, axis=0) if len(dvln_rows) > 1 else dvln_rows[0]
        dlnb_ref[...] += _colsum(dvln)
        dlng_ref[...] += _colsum(dvln * vh)
        dvh = dvln * lng_v
        dgv = rstd * (dvh - _mean(dvh) - vh * _mean(dvh * vh))
        duv_ref[:, :DM] = (dgu * _gelu_grad(uv[:, :DM])).astype(BF16)
        duv_ref[:, DM:] = (dgv * _gelu_grad(uv[:, DM:])).astype(BF16)

        @pl.when(i == pl.num_programs(0) - 1)
        def _():
            ind = (lax.broadcasted_iota(I32, (DM, LANE), 0) // 64 == lax.broadcasted_iota(I32, (DM, LANE), 1)).astype(F32)
            dbs_ref[...] = jnp.dot(dbsx_ref[...], ind, preferred_element_type=F32, precision=lax.Precision.HIGHEST)

    vec_out = jax.ShapeDtypeStruct((1, DM), F32)
    outs = pl.pallas_call(
        body, name="gmlp_bwd", grid=(t // tm,),
        in_specs=[_rows(tm, DM), _rows(tm, 2 * DM, C_UV // (2 * DM)), _vec(DM), _vec(DM),
                  pl.BlockSpec((NG, CH, CH), lambda i: (0, 0, 0)), pl.BlockSpec((NG, CH, CH), lambda i: (0, 0, 0)),
                  pl.BlockSpec((CH, DM), lambda i: (0, 0))],
        out_specs=[_rows(tm, 2 * DM), pl.BlockSpec((NG, CH, CH), lambda i: (0, 0, 0)),
                   pl.BlockSpec((CH, LANE), lambda i: (0, 0)), _vec(DM), _vec(DM)],
        out_shape=[jax.ShapeDtypeStruct((t, 2 * DM), BF16), jax.ShapeDtypeStruct((NG, CH, CH), F32),
                   jax.ShapeDtypeStruct((CH, LANE), F32), vec_out, vec_out],
        scratch_shapes=[pltpu.VMEM((CH, DM), F32)],
        compiler_params=_cparams(("arbitrary",)))(dga, proj, lng, lnb, wsm, wsmt, bsx)
    return outs


def _glu_into(zs_ref, glu_ref, halo_ref, first):
    hal = halo_ref[...]
    z0h = hal[:, :DM] * _sigmoid(hal[:, DM:])
    zs_ref[0:HALO, :] = jnp.where(first, 0.0, z0h)
    g = glu_ref[...]
    zs_ref[HALO:, :] = g[:, :DM] * _sigmoid(g[:, DM:])


def _conv_fwd(proj, cw, cb, lng, lnb, *, tm=256, rb=64):
    t = proj.shape[0]
    tm = min(tm, t)
    hb = tm // HALO
    gcol = C_GLU // (2 * DM)

    def body(glu_ref, halo_ref, cw_ref, cb_ref, lng_ref, lnb_ref, zc_ref, zb_ref, zs_ref):
        i = pl.program_id(0)
        _glu_into(zs_ref, glu_ref, halo_ref, i == 0)
        for cbk in range(DM // LANE):
            cs = slice(cbk * LANE, (cbk + 1) * LANE)
            for r in range(tm // rb):
                acc = jnp.broadcast_to(cb_ref[:, cs], (rb, LANE))
                for k in range(KW):
                    off = r * rb + HALO - (KW - 1) + k
                    acc = acc + cw_ref[k:k + 1, cs] * zs_ref[off:off + rb, cs]
                zc_ref[r * rb:(r + 1) * rb, cs] = acc
        zc = zc_ref[...]
        mu = _mean(zc)
        zcc = zc - mu
        zh = zcc * lax.rsqrt(_mean(zcc * zcc) + EPS)
        a = zh * lng_ref[...] + lnb_ref[...]
        zb_ref[...] = (a * _sigmoid(a)).astype(BF16)

    return pl.pallas_call(
        body, name="conv_fwd", grid=(t // tm,),
        in_specs=[_rows(tm, 2 * DM, gcol),
                  pl.BlockSpec((HALO, 2 * DM), lambda i: (jnp.maximum(i * hb - 1, 0), gcol)),
                  pl.BlockSpec((KW, DM), lambda i: (0, 0)), _vec(DM), _vec(DM), _vec(DM)],
        out_specs=[_rows(tm, DM), _rows(tm, DM)],
        out_shape=[jax.ShapeDtypeStruct((t, DM), F32), jax.ShapeDtypeStruct((t, DM), BF16)],
        scratch_shapes=[pltpu.VMEM((HALO + tm, DM), F32)],
        compiler_params=_cparams(("parallel",)))(proj, proj, cw, cb, lng, lnb)


def _conv_bwd_ln(dzb, zc, lng, lnb, *, tm=256):
    t = zc.shape[0]
    tm = min(tm, t)

    def body(dzb_ref, zc_ref, lng_ref, lnb_ref, dzc_ref, dlng_ref, dlnb_ref):
        @pl.when(pl.program_id(0) == 0)
        def _():
            dlng_ref[...] = jnp.zeros_like(dlng_ref)
            dlnb_ref[...] = jnp.zeros_like(dlnb_ref)

        zc = zc_ref[...]
        lng_v = lng_ref[...]
        mu = _mean(zc)
        zcc = zc - mu
        rstd = lax.rsqrt(_mean(zcc * zcc) + EPS)
        zh = zcc * rstd
        a = zh * lng_v + lnb_ref[...]
        s = _sigmoid(a)
        da = dzb_ref[...] * (s * (1.0 + a * (1.0 - s)))
        dlnb_ref[...] += _colsum(da)
        dlng_ref[...] += _colsum(da * zh)
        dzh = da * lng_v
        dzc_ref[...] = rstd * (dzh - _mean(dzh) - zh * _mean(dzh * zh))

    vec_out = jax.ShapeDtypeStruct((1, DM), F32)
    return pl.pallas_call(
        body, name="conv_bwd_ln", grid=(t // tm,), in_specs=[_rows(tm, DM), _rows(tm, DM), _vec(DM), _vec(DM)],
        out_specs=[_rows(tm, DM), _vec(DM), _vec(DM)],
        out_shape=[jax.ShapeDtypeStruct((t, DM), F32), vec_out, vec_out],
        compiler_params=_cparams(("arbitrary",)))(dzb, zc, lng, lnb)


def _conv_bwd(dzc, proj, cw, *, tm=256, rb=64):
    t = proj.shape[0]
    tm = min(tm, t)
    hb = tm // HALO
    nblk = t // tm
    gcol = C_GLU // (2 * DM)

    def body(dzc_ref, dnext_ref, glu_ref, halo_ref, cw_ref, dglu_ref, dcw_ref, dcb_ref, zs_ref, ds_ref):
        i = pl.program_id(0)

        @pl.when(i == 0)
        def _():
            dcw_ref[...] = jnp.zeros_like(dcw_ref)
            dcb_ref[...] = jnp.zeros_like(dcb_ref)

        _glu_into(zs_ref, glu_ref, halo_ref, i == 0)
        dzc = dzc_ref[...]
        ds_ref[0:tm, :] = dzc
        ds_ref[tm:, :] = jnp.where(i == nblk - 1, 0.0, dnext_ref[...])
        dcb_ref[...] += _colsum(dzc)
        for k in range(KW):
            off = HALO - (KW - 1) + k
            dcw_ref[k:k + 1, :] += _colsum(dzc * zs_ref[off:off + tm, :])
        g = glu_ref[...]
        val, sg = g[:, :DM], _sigmoid(g[:, DM:])
        for cbk in range(DM // LANE):
            cs = slice(cbk * LANE, (cbk + 1) * LANE)
            for r in range(tm // rb):
                acc = jnp.zeros((rb, LANE), F32)
                for k in range(KW):
                    off = r * rb + (KW - 1) - k
                    acc = acc + cw_ref[k:k + 1, cs] * ds_ref[off:off + rb, cs]
                rs = slice(r * rb, (r + 1) * rb)
                dglu_ref[rs, cs] = (acc * sg[rs, cs]).astype(BF16)
                v, s = val[rs, cs], sg[rs, cs]
                dglu_ref[rs, DM + cbk * LANE:DM + (cbk + 1) * LANE] = (acc * v * s * (1.0 - s)).astype(BF16)

    return pl.pallas_call(
        body, name="conv_bwd", grid=(nblk,),
        in_specs=[_rows(tm, DM),
                  pl.BlockSpec((HALO, DM), lambda i: (jnp.minimum((i + 1) * hb, nblk * hb - 1), 0)),
                  _rows(tm, 2 * DM, gcol),
                  pl.BlockSpec((HALO, 2 * DM), lambda i: (jnp.maximum(i * hb - 1, 0), gcol)),
                  pl.BlockSpec((KW, DM), lambda i: (0, 0))],
        out_specs=[_rows(tm, 2 * DM), pl.BlockSpec((HALO, DM), lambda i: (0, 0)), _vec(DM)],
        out_shape=[jax.ShapeDtypeStruct((t, 2 * DM), BF16), jax.ShapeDtypeStruct((HALO, DM), F32),
                   jax.ShapeDtypeStruct((1, DM), F32)],
        scratch_shapes=[pltpu.VMEM((HALO + tm, DM), F32), pltpu.VMEM((tm + HALO, DM), F32)],
        compiler_params=_cparams(("arbitrary",)))(dzc, dzc, proj, proj, cw)


def _log_sigmoid(x):
    return jnp.minimum(x, 0.0) - jnp.log1p(jnp.exp(-jnp.abs(x)))


def _fox_cum(proj, bfp):
    t = proj.shape[0]
    fcol = C_F // LANE

    def body(f_ref, bf_ref, cum_ref, carry_ref):
        @pl.when(pl.program_id(0) == 0)
        def _():
            carry_ref[...] = jnp.zeros_like(carry_ref)

        lf = _log_sigmoid(f_ref[...] + bf_ref[...])
        tri = (lax.broadcasted_iota(I32, (CH, CH), 0) >= lax.broadcasted_iota(I32, (CH, CH), 1)).astype(F32)
        cum = jnp.dot(tri, lf, preferred_element_type=F32, precision=lax.Precision.HIGHEST) + carry_ref[0:1, :]
        cum_ref[...] = cum
        carry_ref[...] = jnp.broadcast_to(cum[CH - 1:CH, :], carry_ref.shape)

    return pl.pallas_call(
        body, name="fox_cum", grid=(t // CH,), in_specs=[_rows(CH, LANE, fcol), _vec(LANE)],
        out_specs=_rows(CH, LANE), out_shape=jax.ShapeDtypeStruct((t, LANE), F32),
        scratch_shapes=[pltpu.VMEM((8, LANE), F32)],
        compiler_params=_cparams(("arbitrary",)))(proj, bfp)


def _fox_cum_bwd(dcum, proj, bfp):
    t = proj.shape[0]
    nb = t // CH
    fcol = C_F // LANE
    fw = D_INP - C_F

    def body(dc_ref, f_ref, bf_ref, df_ref, dbf_ref, carry_ref):
        @pl.when(pl.program_id(0) == 0)
        def _():
            carry_ref[...] = jnp.zeros_like(carry_ref)
            dbf_ref[...] = jnp.zeros_like(dbf_ref)

        triu = (lax.broadcasted_iota(I32, (CH, CH), 0) <= lax.broadcasted_iota(I32, (CH, CH), 1)).astype(F32)
        dlf = jnp.dot(triu, dc_ref[...], preferred_element_type=F32, precision=lax.Precision.HIGHEST) + carry_ref[0:1, :]
        carry_ref[...] = jnp.broadcast_to(dlf[0:1, :], carry_ref.shape)
        z = f_ref[...] + bf_ref[...]
        lane = lax.broadcasted_iota(I32, (CH, LANE), 1)
        df = jnp.where(lane < NG, dlf * _sigmoid(-z), 0.0)
        dbf_ref[...] += _colsum(df)
        df_ref[:, 0:LANE] = df.astype(BF16)
        df_ref[:, LANE:] = jnp.zeros((CH, fw - LANE), BF16)

    return pl.pallas_call(
        body, name="fox_cum_bwd", grid=(nb,),
        in_specs=[pl.BlockSpec((CH, LANE), lambda i: (nb - 1 - i, 0)),
                  pl.BlockSpec((CH, LANE), lambda i: (nb - 1 - i, fcol)), _vec(LANE)],
        out_specs=[pl.BlockSpec((CH, fw), lambda i: (nb - 1 - i, 0)), _vec(LANE)],
        out_shape=[jax.ShapeDtypeStruct((t, fw), BF16), jax.ShapeDtypeStruct((1, LANE), F32)],
        scratch_shapes=[pltpu.VMEM((8, LANE), F32)],
        compiler_params=_cparams(("arbitrary",)))(dcum, proj, bfp)


HD = 64
ATT_SCALE = 0.125
NEG = -1e30


def _qkv_prep(proj, *, tm=512):
    t = proj.shape[0]
    tm = min(tm, t)

    def body(q_ref, k_ref, v_ref, o_ref):
        o_ref[:, 0:DM] = (q_ref[...] * ATT_SCALE).astype(BF16)
        o_ref[:, DM:2 * DM] = k_ref[...].astype(BF16)
        o_ref[:, 2 * DM:] = v_ref[...].astype(BF16)

    return pl.pallas_call(
        body, name="qkv_prep", grid=(t // tm,),
        in_specs=[_rows(tm, DM, C_Q // DM), _rows(tm, DM, C_K // DM), _rows(tm, DM, C_V // DM)],
        out_specs=_rows(tm, 3 * DM), out_shape=jax.ShapeDtypeStruct((t, 3 * DM), BF16),
        compiler_params=_cparams(("parallel",)))(proj, proj, proj)


def _causal_pairs(nq, outer_is_query):
    if outer_is_query:
        pairs = [(i, j) for i in range(nq) for j in range(i + 1)]
    else:
        pairs = [(j, i) for j in range(nq) for i in range(j, nq)]
    return (jnp.asarray([p[0] for p in pairs], I32), jnp.asarray([p[1] for p in pairs], I32))


def _rep(x, tk):
    return x if tk == LANE else jnp.tile(x, (1, tk // LANE))


def _attn_fwd(qkv, ckrow, *, tq=256):
    t = qkv.shape[0]
    tq = min(tq, t)
    tk = tq
    nq = t // tq
    oi, ij = _causal_pairs(nq, True)

    def body(oi_ref, ij_ref, q_ref, k_ref, v_ref, ck_ref, o_ref, lse_ref, m_ref, l_ref, acc_ref):
        n = pl.program_id(0)
        i, j = oi_ref[n], ij_ref[n]

        @pl.when(j == 0)
        def _():
            m_ref[...] = jnp.full_like(m_ref, NEG)
            l_ref[...] = jnp.zeros_like(l_ref)
            acc_ref[...] = jnp.zeros_like(acc_ref)

        def step(masked):
            if masked:
                keep = lax.broadcasted_iota(I32, (tq, tk), 1) <= lax.broadcasted_iota(I32, (tq, tk), 0)
            for h in range(NG):
                hs = slice(h * HD, (h + 1) * HD)
                s = _dot(q_ref[:, hs], k_ref[:, hs], NT) - ck_ref[h:h + 1, :]
                if masked:
                    s = jnp.where(keep, s, NEG)
                m_prev = m_ref[h]
                m_new = jnp.maximum(m_prev, jnp.max(s, axis=1, keepdims=True))
                alpha = jnp.exp(m_prev - m_new)
                p = jnp.exp(s - _rep(m_new, tk))
                l_ref[h] = alpha * l_ref[h] + jnp.sum(p, axis=1, keepdims=True)
                acc_ref[:, hs] = alpha[:, :HD] * acc_ref[:, hs] + _dot(p.astype(BF16), v_ref[:, hs])
                m_ref[h] = m_new

        @pl.when(j < i)
        def _():
            step(False)

        @pl.when(j == i)
        def _():
            step(True)
            for h in range(NG):
                hs = slice(h * HD, (h + 1) * HD)
                l = l_ref[h]
                o_ref[:, hs] = (acc_ref[:, hs] / l[:, :HD]).astype(BF16)
                lse_ref[h] = m_ref[h] + jnp.log(l)

    gs = pltpu.PrefetchScalarGridSpec(
        num_scalar_prefetch=2, grid=(int(oi.shape[0]),),
        in_specs=[pl.BlockSpec((tq, DM), lambda n, a, b: (a[n], 0)),
                  pl.BlockSpec((tk, DM), lambda n, a, b: (b[n], 1)),
                  pl.BlockSpec((tk, DM), lambda n, a, b: (b[n], 2)),
                  pl.BlockSpec((NG, tk), lambda n, a, b: (0, b[n]))],
        out_specs=[pl.BlockSpec((tq, DM), lambda n, a, b: (a[n], 0)),
                   pl.BlockSpec((NG, tq, LANE), lambda n, a, b: (0, a[n], 0))],
        scratch_shapes=[pltpu.VMEM((NG, tq, LANE), F32), pltpu.VMEM((NG, tq, LANE), F32), pltpu.VMEM((tq, DM), F32)])
    return pl.pallas_call(
        body, name="attn_fwd", grid_spec=gs,
        out_shape=[jax.ShapeDtypeStruct((t, DM), BF16), jax.ShapeDtypeStruct((NG, t, LANE), F32)],
        compiler_params=_cparams(("arbitrary",)))(oi, ij, qkv, qkv, qkv, ckrow)


def _attn_bwd_dq(qkv, ckrow, o, do, lse, *, tq=256):
    t = qkv.shape[0]
    tq = min(tq, t)
    tk = tq
    nq = t // tq
    oi, ij = _causal_pairs(nq, True)

    def body(oi_ref, ij_ref, q_ref, k_ref, v_ref, ck_ref, o_ref, do_ref, lse_ref, dq_ref, delta_ref, dcq_ref, acc_ref):
        n = pl.program_id(0)
        i, j = oi_ref[n], ij_ref[n]

        @pl.when(j == 0)
        def _():
            acc_ref[...] = jnp.zeros_like(acc_ref)
            dcq_ref[...] = jnp.zeros_like(dcq_ref)
            for h in range(NG):
                hs = slice(h * HD, (h + 1) * HD)
                d = jnp.sum(do_ref[:, hs] * o_ref[:, hs].astype(F32), axis=1, keepdims=True)
                delta_ref[h] = jnp.broadcast_to(d, (tq, LANE))

        def step(masked):
            if masked:
                keep = lax.broadcasted_iota(I32, (tq, tk), 1) <= lax.broadcasted_iota(I32, (tq, tk), 0)
            for h in range(NG):
                hs = slice(h * HD, (h + 1) * HD)
                s = _dot(q_ref[:, hs], k_ref[:, hs], NT) - ck_ref[h:h + 1, :]
                if masked:
                    s = jnp.where(keep, s, NEG)
                p = jnp.exp(s - _rep(lse_ref[h], tk))
                dp = _dot(do_ref[:, hs].astype(BF16), v_ref[:, hs], NT)
                ds = p * (dp - _rep(delta_ref[h], tk))
                dcq_ref[h] += jnp.sum(ds, axis=1, keepdims=True)
                acc_ref[:, hs] += _dot(ds.astype(BF16), k_ref[:, hs])

        @pl.when(j < i)
        def _():
            step(False)

        @pl.when(j == i)
        def _():
            step(True)
            dq_ref[...] = (acc_ref[...] * ATT_SCALE).astype(BF16)

    gs = pltpu.PrefetchScalarGridSpec(
        num_scalar_prefetch=2, grid=(int(oi.shape[0]),),
        in_specs=[pl.BlockSpec((tq, DM), lambda n, a, b: (a[n], 0)),
                  pl.BlockSpec((tk, DM), lambda n, a, b: (b[n], 1)),
                  pl.BlockSpec((tk, DM), lambda n, a, b: (b[n], 2)),
                  pl.BlockSpec((NG, tk), lambda n, a, b: (0, b[n])),
                  pl.BlockSpec((tq, DM), lambda n, a, b: (a[n], 0)),
                  pl.BlockSpec((tq, DM), lambda n, a, b: (a[n], 0)),
                  pl.BlockSpec((NG, tq, LANE), lambda n, a, b: (0, a[n], 0))],
        out_specs=[pl.BlockSpec((tq, DM), lambda n, a, b: (a[n], 0)),
                   pl.BlockSpec((NG, tq, LANE), lambda n, a, b: (0, a[n], 0)),
                   pl.BlockSpec((NG, tq, LANE), lambda n, a, b: (0, a[n], 0))],
        scratch_shapes=[pltpu.VMEM((tq, DM), F32)])
    return pl.pallas_call(
        body, name="attn_bwd_dq", grid_spec=gs,
        out_shape=[jax.ShapeDtypeStruct((t, DM), BF16), jax.ShapeDtypeStruct((NG, t, LANE), F32), jax.ShapeDtypeStruct((NG, t, LANE), F32)],
        compiler_params=_cparams(("arbitrary",)))(oi, ij, qkv, qkv, qkv, ckrow, o, do, lse)


def _attn_bwd_dkv(qkv, ckcol, do, lserow, deltarow, *, tq=256):
    t = qkv.shape[0]
    tq = min(tq, t)
    tk = tq
    nq = t // tq
    oj, ii = _causal_pairs(nq, False)

    def body(oj_ref, ii_ref, q_ref, k_ref, v_ref, ck_ref, do_ref, lse_ref, delta_ref, dk_ref, dv_ref, dck_ref, dka_ref,
             dva_ref):
        n = pl.program_id(0)
        j, i = oj_ref[n], ii_ref[n]

        @pl.when(i == j)
        def _():
            dka_ref[...] = jnp.zeros_like(dka_ref)
            dva_ref[...] = jnp.zeros_like(dva_ref)
            dck_ref[...] = jnp.zeros_like(dck_ref)

        def step(masked):
            if masked:
                keep = lax.broadcasted_iota(I32, (tk, tq), 0) <= lax.broadcasted_iota(I32, (tk, tq), 1)
            for h in range(NG):
                hs = slice(h * HD, (h + 1) * HD)
                st = _dot(k_ref[:, hs], q_ref[:, hs], NT) - _rep(ck_ref[h], tq)
                if masked:
                    st = jnp.where(keep, st, NEG)
                pt = jnp.exp(st - lse_ref[h:h + 1, :])
                dob = do_ref[:, hs].astype(BF16)
                dva_ref[:, hs] += _dot(pt.astype(BF16), dob)
                dpt = _dot(v_ref[:, hs], dob, NT)
                dst = pt * (dpt - delta_ref[h:h + 1, :])
                dka_ref[:, hs] += _dot(dst.astype(BF16), q_ref[:, hs])
                dck_ref[h] -= jnp.sum(dst, axis=1, keepdims=True)

        @pl.when(i == j)
        def _():
            step(True)

        @pl.when(i > j)
        def _():
            step(False)

        @pl.when(i == nq - 1)
        def _():
            dk_ref[...] = dka_ref[...].astype(BF16)
            dv_ref[...] = dva_ref[...].astype(BF16)

    gs = pltpu.PrefetchScalarGridSpec(
        num_scalar_prefetch=2, grid=(int(oj.shape[0]),),
        in_specs=[pl.BlockSpec((tq, DM), lambda n, a, b: (b[n], 0)),
                  pl.BlockSpec((tk, DM), lambda n, a, b: (a[n], 1)),
                  pl.BlockSpec((tk, DM), lambda n, a, b: (a[n], 2)),
                  pl.BlockSpec((NG, tk, LANE), lambda n, a, b: (0, a[n], 0)),
                  pl.BlockSpec((tq, DM), lambda n, a, b: (b[n], 0)),
                  pl.BlockSpec((NG, tq), lambda n, a, b: (0, b[n])),
                  pl.BlockSpec((NG, tq), lambda n, a, b: (0, b[n]))],
        out_specs=[pl.BlockSpec((tk, DM), lambda n, a, b: (a[n], 0)),
                   pl.BlockSpec((tk, DM), lambda n, a, b: (a[n], 0)),
                   pl.BlockSpec((NG, tk, LANE), lambda n, a, b: (0, a[n], 0))],
        scratch_shapes=[pltpu.VMEM((tk, DM), F32), pltpu.VMEM((tk, DM), F32)])
    return pl.pallas_call(
        body, name="attn_bwd_dkv", grid_spec=gs,
        out_shape=[jax.ShapeDtypeStruct((t, DM), BF16), jax.ShapeDtypeStruct((t, DM), BF16), jax.ShapeDtypeStruct((NG, t, LANE), F32)],
        compiler_params=_cparams(("arbitrary",)))(oj, ii, qkv, qkv, qkv, ckcol, do, lserow, deltarow)


def _merge_fwd(ga, zb, att, proj, wa, wb, wc, *, tm=256):
    t = ga.shape[0]
    tm = min(tm, t)
    wspec = pl.BlockSpec((DM, D), lambda i: (0, 0))

    def body(ga_ref, zb_ref, att_ref, gate_ref, wa_ref, wb_ref, wc_ref, m_ref):
        acc = jnp.zeros((tm, D), F32)
        for b, (x_ref, w_ref) in enumerate(((ga_ref, wa_ref), (zb_ref, wb_ref), (att_ref, wc_ref))):
            acc = acc + _sigmoid(gate_ref[:, b * D:(b + 1) * D]) * _dot(x_ref[...], w_ref[...])
        m_ref[...] = acc.astype(BF16)

    return pl.pallas_call(
        body, name="merge_fwd", grid=(t // tm,),
        in_specs=[_rows(tm, DM), _rows(tm, DM), _rows(tm, DM), _rows(tm, 3 * D, 0), wspec, wspec, wspec],
        out_specs=_rows(tm, D), out_shape=jax.ShapeDtypeStruct((t, D), BF16),
        compiler_params=_cparams(("parallel",)))(ga, zb, att, proj, wa, wb, wc)


def _merge_bwd(dm, ga, zb, att, proj, wa, wb, wc, *, tm=256):
    t = ga.shape[0]
    tm = min(tm, t)
    wspec = pl.BlockSpec((DM, D), lambda i: (0, 0))

    def body(dm_ref, ga_ref, zb_ref, att_ref, gate_ref, wa_ref, wb_ref, wc_ref, dgate_ref, dga_ref, dzb_ref, datt_ref,
             dwa_ref, dwb_ref, dwc_ref):
        @pl.when(pl.program_id(0) == 0)
        def _():
            dwa_ref[...] = jnp.zeros_like(dwa_ref)
            dwb_ref[...] = jnp.zeros_like(dwb_ref)
            dwc_ref[...] = jnp.zeros_like(dwc_ref)

        dmv = dm_ref[...]
        branches = ((ga_ref, wa_ref, dga_ref, dwa_ref), (zb_ref, wb_ref, dzb_ref, dwb_ref),
                    (att_ref, wc_ref, datt_ref, dwc_ref))
        for b, (x_ref, w_ref, dx_ref, dw_ref) in enumerate(branches):
            xv, wv = x_ref[...], w_ref[...]
            y = _dot(xv, wv)
            g = _sigmoid(gate_ref[:, b * D:(b + 1) * D])
            dgate_ref[:, b * D:(b + 1) * D] = (dmv * y * g * (1.0 - g)).astype(BF16)
            dy = (dmv * g).astype(BF16)
            dx_ref[...] = _dot(dy, wv, NT)
            dw_ref[...] += _dot(xv, dy, TN)

    return pl.pallas_call(
        body, name="merge_bwd", grid=(t // tm,),
        in_specs=[_rows(tm, D), _rows(tm, DM), _rows(tm, DM), _rows(tm, DM), _rows(tm, 3 * D, 0), wspec, wspec, wspec],
        out_specs=[_rows(tm, 3 * D), _rows(tm, DM), _rows(tm, DM), _rows(tm, DM), wspec, wspec, wspec],
        out_shape=[jax.ShapeDtypeStruct((t, 3 * D), BF16)] + [jax.ShapeDtypeStruct((t, DM), F32)] * 3
        + [jax.ShapeDtypeStruct((DM, D), F32)] * 3,
        compiler_params=_cparams(("arbitrary",)))(dm, ga, zb, att, proj, wa, wb, wc)


def _heads_layout(cum):
    t = cum.shape[0]
    ckrow = cum[:, :NG].T
    return ckrow, jnp.broadcast_to(ckrow[:, :, None], (NG, t, LANE))


def _layer_fwd(x, mod, w):
    sh1, sc1, gt1, sh2, sc2, gt2 = (mod[k:k + 1] for k in range(NMOD))
    h1 = _norm_mod(x, w["mix_pre_g"], sc1, sh1, name="norm_mix")
    proj = _matmul(h1, w["w_in_p"], name="mm_proj")
    ga = _gmlp_fwd(proj, w["gmlp_ln_g"], w["gmlp_ln_b"], w["wsm"], w["bsx"])
    zc, zb = _conv_fwd(proj, w["conv_w"], w["conv_b"], w["conv_ln_g"], w["conv_ln_b"])
    cum = _fox_cum(proj, w["bfp"])
    ckrow, ckcol = _heads_layout(cum)
    qkv = _qkv_prep(proj)
    att, lse = _attn_fwd(qkv, ckrow)
    merged = _merge_fwd(ga, zb, att, proj, w["w_a_out"], w["w_b_out"], w["w_c_out"])
    y1 = _matmul(merged, w["w_out"], name="mm_out")
    x2 = _resid(x, y1, gt1, w["mix_post_g"], name="resid_mix")
    h2 = _norm_mod(x2, w["mlp_pre_g"], sc2, sh2, name="norm_mlp")
    a, hid = _matmul(h2, w["mlp_w1"], name="mm_w1", extra_out=(BF16,),
                     epilogue=lambda acc: (acc, jnp.square(jnp.maximum(acc, 0.0))))
    y2 = _matmul(hid, w["mlp_w2"], name="mm_w2")
    x3 = _resid(x2, y2, gt2, w["mlp_post_g"], name="resid_mlp")
    saved = dict(x=x, h1=h1, proj=proj, ga=ga, zc=zc, zb=zb, qkv=qkv, ckrow=ckrow, ckcol=ckcol, att=att, lse=lse, merged=merged,
                 y1=y1, x2=x2, h2=h2, a=a, hid=hid, y2=y2)
    return x3, saved


def _layer_bwd(dx3, mod, w, s):
    sh1, sc1, gt1, sh2, sc2, gt2 = (mod[k:k + 1] for k in range(NMOD))
    g = {}
    dy2, dgt2, g["mlp_post_g"] = _resid_bwd(dx3, s["y2"], gt2, w["mlp_post_g"], name="resid_mlp_bwd")
    da = _matmul(dy2, w["mlp_w2"], tb=True, name="mm_dhid", out_dtype=BF16, extra=(s["a"],),
                 epilogue=lambda acc, a: (acc * (2.0 * jnp.maximum(a, 0.0)),))
    g["mlp_w2"] = _matmul(s["hid"], dy2, ta=True, name="mm_dw2")
    g["mlp_w1"] = _matmul(s["h2"], da, ta=True, name="mm_dw1")
    dh2 = _matmul(da, w["mlp_w1"], tb=True, name="mm_dh2")
    dx2, g["mlp_pre_g"], dsc2, dsh2 = _norm_bwd(dh2, dx3, s["x2"], w["mlp_pre_g"], sc2, name="norm_mlp_bwd")
    dy1, dgt1, g["mix_post_g"] = _resid_bwd(dx2, s["y1"], gt1, w["mix_post_g"], name="resid_mix_bwd")
    dmerged = _matmul(dy1, w["w_out"], tb=True, name="mm_dmerged")
    g["w_out"] = _matmul(s["merged"], dy1, ta=True, name="mm_dwout")
    dgate, dga, dzb, datt, g["w_a_out"], g["w_b_out"], g["w_c_out"] = _merge_bwd(
        dmerged, s["ga"], s["zb"], s["att"], s["proj"], w["w_a_out"], w["w_b_out"], w["w_c_out"])
    duv, g["gmlp_ws"], dbs, g["gmlp_ln_g"], g["gmlp_ln_b"] = _gmlp_bwd(
        dga, s["proj"], w["gmlp_ln_g"], w["gmlp_ln_b"], w["wsm"], w["wsmt"], w["bsx"])
    g["gmlp_bs"] = dbs[:, :NG].T
    dzc, g["conv_ln_g"], g["conv_ln_b"] = _conv_bwd_ln(dzb, s["zc"], w["conv_ln_g"], w["conv_ln_b"])
    dglu, dcw, g["conv_b"] = _conv_bwd(dzc, s["proj"], w["conv_w"])
    g["conv_w"] = dcw[:KW]
    dq, delta, dcq = _attn_bwd_dq(s["qkv"], s["ckrow"], s["att"], datt, s["lse"])
    dk, dv, dck = _attn_bwd_dkv(s["qkv"], s["ckcol"], datt, s["lse"][:, :, 0], delta[:, :, 0])
    dcum = jnp.pad((dcq[:, :, 0] + dck[:, :, 0]).T, ((0, 0), (0, LANE - NG)))
    df, dbf = _fox_cum_bwd(dcum, s["proj"], w["bfp"])
    g["fox_bf"] = dbf[0, :NG]
    dproj = jnp.concatenate([dgate, duv, dglu, dq, dk, dv, df], axis=1)
    g["w_in_p"] = _matmul(s["h1"], dproj, ta=True, name="mm_dwin")
    dh1 = _matmul(dproj, w["w_in_p"], tb=True, name="mm_dh1")
    dx, g["mix_pre_g"], dsc1, dsh1 = _norm_bwd(dh1, dx2, s["x"], w["mix_pre_g"], sc1, name="norm_mix_bwd")
    dmod = jnp.concatenate([dsh1, dsc1, dgt1, dsh2, dsc2, dgt2], axis=0)
    return dx, g, dmod


def _position():
    return lax.axis_index("x"), lax.axis_index("y"), lax.axis_index("c")


def _all_gather8(v):
    m_per, n = v.shape

    def body(x_ref, out_ref, send_sems, recv_sems, local_sem):
        x, y, c = _position()
        me, sibling = (x, y, c), (x, y, 1 - c)
        chips = [(1 - x, y), (x, 1 - y), (1 - x, 1 - y)]

        def rows(px, py, pc):
            return out_ref.at[pl.ds((4 * px + 2 * py + pc) * m_per, m_per), :]

        def copy(k, block, to, src=None):
            return pltpu.make_async_remote_copy(
                src_ref=rows(*block) if src is None else src, dst_ref=rows(*block), send_sem=send_sems.at[k],
                recv_sem=recv_sems.at[k], device_id=to, device_id_type=MESH)

        mine = pltpu.make_async_copy(x_ref, rows(*me), local_sem)
        mine.start()
        first = [copy(0, me, sibling, src=x_ref)]
        first += [copy(1 + j, me, (*chip, c), src=x_ref) for j, chip in enumerate(chips)]
        for cp in first:
            cp.start()
        passed = [copy(4 + j, (*chip, c), sibling) for j, chip in enumerate(chips)]
        for j, chip in enumerate(chips):
            copy(1 + j, (*chip, c), me).wait_recv()
            passed[j].start()
        copy(0, sibling, me).wait_recv()
        for j, chip in enumerate(chips):
            copy(4 + j, (*chip, 1 - c), me).wait_recv()
        for cp in first + passed:
            cp.wait_send()
        mine.wait()

    out = pl.pallas_call(
        body, name="all_gather8", out_shape=jax.ShapeDtypeStruct((N_DEV * m_per, n), v.dtype),
        in_specs=[pl.BlockSpec(memory_space=pltpu.VMEM)], out_specs=pl.BlockSpec(memory_space=pltpu.VMEM),
        scratch_shapes=[pltpu.SemaphoreType.DMA((7,)), pltpu.SemaphoreType.DMA((7,)), pltpu.SemaphoreType.DMA],
        compiler_params=pltpu.CompilerParams(vmem_limit_bytes=VMEM_LIMIT),
    )(v)
    return out.reshape(N_DEV, m_per, n)


def _all_gather_chips(wp):
    _, r, n = wp.shape

    def body(w_ref, out_ref, send_sems, recv_sems):
        x, y, c = _position()
        sibling = (x, y, 1 - c)
        chips = [(1 - x, y), (x, 1 - y), (1 - x, 1 - y)]

        def slab(px, py, layer):
            return out_ref.at[2 * px + py, layer]

        def copy(k, block, to, src=None):
            return pltpu.make_async_remote_copy(
                src_ref=slab(*block) if src is None else src, dst_ref=slab(*block), send_sem=send_sems.at[k],
                recv_sem=recv_sems.at[k], device_id=to, device_id_type=MESH)

        first = [copy(j, (x, y, c), (*chip, c), src=w_ref.at[c]) for j, chip in enumerate(chips)]
        for cp in first:
            cp.start()
        passed = [copy(3 + j, (*chip, c), sibling) for j, chip in enumerate(chips)]
        for j, chip in enumerate(chips):
            copy(j, (*chip, c), (x, y, c)).wait_recv()
            passed[j].start()
        for j, chip in enumerate(chips):
            copy(3 + j, (*chip, 1 - c), (x, y, c)).wait_recv()
        for cp in first + passed:
            cp.wait_send()

    return pl.pallas_call(
        body, name="all_gather_chips", out_shape=jax.ShapeDtypeStruct((N_CHIPS, 2, r, n), wp.dtype),
        in_specs=[ANY], out_specs=ANY,
        scratch_shapes=[pltpu.SemaphoreType.DMA((6,)), pltpu.SemaphoreType.DMA((6,))],
    )(wp)


def _swap_layers(gp):
    nj, _, r, n = gp.shape

    def body(g_ref, out_ref, send_sems, recv_sems):
        x, y, c = _position()
        copies = [pltpu.make_async_remote_copy(
            src_ref=g_ref.at[j, 1 - c], dst_ref=out_ref.at[j], send_sem=send_sems.at[j], recv_sem=recv_sems.at[j],
            device_id=(x, y, 1 - c), device_id_type=MESH) for j in range(nj)]
        for cp in copies:
            cp.start()
        for cp in copies:
            cp.wait()

    return pl.pallas_call(
        body, name="rs_swap_layers", out_shape=jax.ShapeDtypeStruct((nj, r, n), gp.dtype), in_specs=[ANY], out_specs=ANY,
        scratch_shapes=[pltpu.SemaphoreType.DMA((nj,)), pltpu.SemaphoreType.DMA((nj,))],
    )(gp)


def _scatter_chips(sp):
    _, r, n = sp.shape

    def body(s_ref, out_ref, send_sems, recv_sems):
        x, y, c = _position()
        chips = [(1 - x, y), (x, 1 - y), (1 - x, 1 - y)]
        copies = [pltpu.make_async_remote_copy(
            src_ref=s_ref.at[2 * px + py], dst_ref=out_ref.at[k], send_sem=send_sems.at[k], recv_sem=recv_sems.at[k],
            device_id=(px, py, c), device_id_type=MESH) for k, (px, py) in enumerate(chips)]
        for cp in copies:
            cp.start()
        for cp in copies:
            cp.wait()

    return pl.pallas_call(
        body, name="rs_scatter_chips", out_shape=jax.ShapeDtypeStruct((3, r, n), sp.dtype), in_specs=[ANY], out_specs=ANY,
        scratch_shapes=[pltpu.SemaphoreType.DMA((3,)), pltpu.SemaphoreType.DMA((3,))],
    )(sp)


def _swap_reduced(red):
    r, n = red.shape

    def body(r_ref, out_ref, send_sem, recv_sem):
        x, y, c = _position()
        cp = pltpu.make_async_remote_copy(src_ref=r_ref, dst_ref=out_ref, send_sem=send_sem, recv_sem=recv_sem,
                                          device_id=(x, y, 1 - c), device_id_type=MESH)
        cp.start()
        cp.wait()

    return pl.pallas_call(
        body, name="rs_swap_reduced", out_shape=jax.ShapeDtypeStruct((r, n), red.dtype), in_specs=[ANY], out_specs=ANY,
        scratch_shapes=[pltpu.SemaphoreType.DMA, pltpu.SemaphoreType.DMA],
    )(red)


def _add_own_layer(gp, recv, c):
    nj, _, r, n = gp.shape
    tr = PACK_TR

    def body(c_ref, g_ref, r_ref, o_ref, ob_ref):
        sm = g_ref[0, 0] + r_ref[0]
        o_ref[0] = sm
        ob_ref[0] = sm.astype(BF16)

    gs = pltpu.PrefetchScalarGridSpec(
        num_scalar_prefetch=1, grid=(nj, r // tr),
        in_specs=[pl.BlockSpec((1, 1, tr, n), lambda j, i, cc: (j, cc[0], i, 0)),
                  pl.BlockSpec((1, tr, n), lambda j, i, cc: (j, i, 0))],
        out_specs=[pl.BlockSpec((1, tr, n), lambda j, i, cc: (j, i, 0))] * 2)
    return pl.pallas_call(body, name="rs_add_own_layer", grid_spec=gs,
                          out_shape=[jax.ShapeDtypeStruct((nj, r, n), F32), jax.ShapeDtypeStruct((nj, r, n), BF16)],
                          compiler_params=_cparams(("parallel", "parallel")))(jnp.reshape(c, (1,)).astype(I32), gp, recv)


def _add_own_chip(sp, recv, j):
    _, r, n = sp.shape
    tr = PACK_TR

    def body(j_ref, s_ref, r_ref, o_ref):
        o_ref[...] = ((s_ref[0] + r_ref[0].astype(F32)) + r_ref[1].astype(F32)) + r_ref[2].astype(F32)

    gs = pltpu.PrefetchScalarGridSpec(
        num_scalar_prefetch=1, grid=(r // tr,),
        in_specs=[pl.BlockSpec((1, tr, n), lambda i, jj: (jj[0], i, 0)), pl.BlockSpec((3, tr, n), lambda i, jj: (0, i, 0))],
        out_specs=pl.BlockSpec((tr, n), lambda i, jj: (i, 0)))
    return pl.pallas_call(body, name="rs_add_own_chip", grid_spec=gs, out_shape=jax.ShapeDtypeStruct((r, n), F32),
                          compiler_params=_cparams(("parallel",)))(jnp.reshape(j, (1,)).astype(I32), sp, recv)


def _sum8(v):
    _, m, n = v.shape

    def body(v_ref, o_ref):
        acc = v_ref[0]
        for k in range(1, N_DEV):
            acc = acc + v_ref[k]
        o_ref[...] = acc

    return pl.pallas_call(body, name="sum8", grid=(1,), in_specs=[pl.BlockSpec((N_DEV, m, n), lambda i: (0, 0, 0))],
                          out_specs=pl.BlockSpec((m, n), lambda i: (0, 0)), out_shape=jax.ShapeDtypeStruct((m, n), F32),
                          compiler_params=_cparams(("arbitrary",)))(v)


def _ada_mod(c_all, ada_w, ada_b_loc, *, tn=512):
    nl, _, ncol = ada_w.shape

    def body(c_ref, w_ref, b_ref, o_ref):
        cv = c_ref[...]
        ca = (cv * _sigmoid(cv)).astype(BF16)
        o_ref[0] = _dot(ca, w_ref[0].astype(BF16)) + b_ref[0]

    return pl.pallas_call(
        body, name="ada_mod", grid=(nl, ncol // tn),
        in_specs=[pl.BlockSpec((N_DEV, D), lambda l, j: (0, 0)), pl.BlockSpec((1, D, tn), lambda l, j: (l, 0, j)),
                  pl.BlockSpec((1, 1, tn), lambda l, j: (l, 0, j))],
        out_specs=pl.BlockSpec((1, N_DEV, tn), lambda l, j: (l, 0, j)),
        out_shape=jax.ShapeDtypeStruct((nl, N_DEV, ncol), F32),
        compiler_params=_cparams(("parallel", "parallel")))(c_all, ada_w, ada_b_loc)


def _ada_grad(c_pad, dmod_pad, *, tn=512):
    nl, nb, ncol = dmod_pad.shape

    def body(c_ref, d_ref, o_ref):
        cv = c_ref[...]
        ca = (cv * _sigmoid(cv)).astype(BF16)
        o_ref[0] = _dot(ca, d_ref[0].astype(BF16), TN)

    return pl.pallas_call(
        body, name="ada_grad", grid=(nl, ncol // tn),
        in_specs=[pl.BlockSpec((nb, D), lambda l, j: (0, 0)), pl.BlockSpec((1, nb, tn), lambda l, j: (l, 0, j))],
        out_specs=pl.BlockSpec((1, D, tn), lambda l, j: (l, 0, j)),
        out_shape=jax.ShapeDtypeStruct((nl, D, ncol), F32),
        compiler_params=_cparams(("parallel", "parallel")))(c_pad, dmod_pad)


def _adamw(w, g, m, v, *, name):
    shape = w.shape
    if w.ndim == 3:
        lead, rows, cols = shape
    else:
        lead, (rows, cols) = 1, shape
    w3, g3, m3, v3 = (a.reshape(lead, rows, cols) for a in (w, g, m, v))
    tr = rows
    if rows * cols * 4 > (2 << 20):
        tr = next(cand for cand in (256, 128, 64, 8) if rows % cand == 0)
    c1 = 1.0 - ADAM_B1 ** ADAM_STEP
    c2 = 1.0 - ADAM_B2 ** ADAM_STEP

    def body(w_ref, g_ref, m_ref, v_ref, d_ref, nm_ref, nv_ref):
        gv = g_ref[...]
        nm = ADAM_B1 * m_ref[...] + (1.0 - ADAM_B1) * gv
        nv = ADAM_B2 * v_ref[...] + (1.0 - ADAM_B2) * (gv * gv)
        nm_ref[...] = nm
        nv_ref[...] = nv
        d_ref[...] = -ADAM_LR * ((nm / c1) / (jnp.sqrt(nv / c2) + ADAM_EPS) + ADAM_WD * w_ref[...])

    spec = pl.BlockSpec((1, tr, cols), lambda l, i: (l, i, 0))
    outs = pl.pallas_call(
        body, name=name, grid=(lead, rows // tr), in_specs=[spec] * 4, out_specs=[spec] * 3,
        out_shape=[jax.ShapeDtypeStruct((lead, rows, cols), F32)] * 3,
        compiler_params=_cparams(("parallel", "parallel")))(w3, g3, m3, v3)
    return tuple(o.reshape(shape) for o in outs)


SMALL = (("mix_pre_g", (2, D)), ("mix_post_g", (2, D)), ("mlp_pre_g", (2, D)), ("mlp_post_g", (2, D)),
         ("gmlp_ln_g", (2, DM)), ("gmlp_ln_b", (2, DM)), ("gmlp_ws", (2, NG, CH, CH)), ("gmlp_bs", (2, NG, CH)),
         ("conv_b", (2, DM)), ("conv_ln_g", (2, DM)), ("conv_ln_b", (2, DM)), ("fox_bf", (2, NG)))


def _pack_rows(arrays, mult=8):
    flat = []
    for a in arrays:
        f = a.reshape(-1).astype(F32)
        pad = (-f.shape[0]) % LANE
        flat.append(jnp.pad(f, (0, pad)) if pad else f)
    cat = jnp.concatenate(flat)
    rows = cat.shape[0] // LANE
    pad_rows = (-rows) % mult
    if pad_rows:
        cat = jnp.pad(cat, (0, pad_rows * LANE))
    return cat.reshape(-1, LANE)


def _unpack_rows(buf, shapes):
    flat = buf.reshape(-1)
    out, off = [], 0
    for shp in shapes:
        size = 1
        for d in shp:
            size *= d
        out.append(flat[off:off + size].reshape(shp))
        off += size + ((-size) % LANE)
    return out


def _assemble_w_in(w_in_full):
    uv_glu_qkv = w_in_full[:, :3584]
    f = w_in_full[:, 3584:3592]
    gate = w_in_full[:, 3592:]
    fpad = jnp.zeros((D, D_INP - C_F - NG), w_in_full.dtype)
    return jnp.concatenate([gate, uv_glu_qkv, f, fpad], axis=1)


def _disassemble_w_in(g_p):
    return jnp.concatenate([g_p[:, C_UV:C_F], g_p[:, C_F:C_F + NG], g_p[:, :C_UV]], axis=1)


def kernel(x, c, ada_w, ada_b, mix_pre_g, mix_post_g, mlp_pre_g, mlp_post_g, w_in, gmlp_ln_g, gmlp_ln_b, gmlp_ws, gmlp_bs, w_a_out, conv_w, conv_b, conv_ln_g, conv_ln_b, w_b_out, fox_bf, w_c_out, w_out, mlp_w1, mlp_w2, loss_target, m_ada_w, m_ada_b, m_mix_pre_g, m_mix_post_g, m_mlp_pre_g, m_mlp_post_g, m_w_in, m_gmlp_ln_g, m_gmlp_ln_b, m_gmlp_ws, m_gmlp_bs, m_w_a_out, m_conv_w, m_conv_b, m_conv_ln_g, m_conv_ln_b, m_w_b_out, m_fox_bf, m_w_c_out, m_w_out, m_mlp_w1, m_mlp_w2, v_ada_w, v_ada_b, v_mix_pre_g, v_mix_post_g, v_mlp_pre_g, v_mlp_post_g, v_w_in, v_gmlp_ln_g, v_gmlp_ln_b, v_gmlp_ws, v_gmlp_bs, v_w_a_out, v_conv_w, v_conv_b, v_conv_ln_g, v_conv_ln_b, v_w_b_out, v_fox_bf, v_w_c_out, v_w_out, v_mlp_w1, v_mlp_w2):
    weights = dict(ada_w=ada_w, ada_b=ada_b, mix_pre_g=mix_pre_g, mix_post_g=mix_post_g, mlp_pre_g=mlp_pre_g,
                   mlp_post_g=mlp_post_g, w_in=w_in, gmlp_ln_g=gmlp_ln_g, gmlp_ln_b=gmlp_ln_b, gmlp_ws=gmlp_ws,
                   gmlp_bs=gmlp_bs, w_a_out=w_a_out, conv_w=conv_w, conv_b=conv_b, conv_ln_g=conv_ln_g,
                   conv_ln_b=conv_ln_b, w_b_out=w_b_out, fox_bf=fox_bf, w_c_out=w_c_out, w_out=w_out, mlp_w1=mlp_w1,
                   mlp_w2=mlp_w2)
    mom_m = dict(ada_w=m_ada_w, ada_b=m_ada_b, mix_pre_g=m_mix_pre_g, mix_post_g=m_mix_post_g, mlp_pre_g=m_mlp_pre_g,
                 mlp_post_g=m_mlp_post_g, w_in=m_w_in, gmlp_ln_g=m_gmlp_ln_g, gmlp_ln_b=m_gmlp_ln_b, gmlp_ws=m_gmlp_ws,
                 gmlp_bs=m_gmlp_bs, w_a_out=m_w_a_out, conv_w=m_conv_w, conv_b=m_conv_b, conv_ln_g=m_conv_ln_g,
                 conv_ln_b=m_conv_ln_b, w_b_out=m_w_b_out, fox_bf=m_fox_bf, w_c_out=m_w_c_out, w_out=m_w_out,
                 mlp_w1=m_mlp_w1, mlp_w2=m_mlp_w2)
    mom_v = dict(ada_w=v_ada_w, ada_b=v_ada_b, mix_pre_g=v_mix_pre_g, mix_post_g=v_mix_post_g, mlp_pre_g=v_mlp_pre_g,
                 mlp_post_g=v_mlp_post_g, w_in=v_w_in, gmlp_ln_g=v_gmlp_ln_g, gmlp_ln_b=v_gmlp_ln_b, gmlp_ws=v_gmlp_ws,
                 gmlp_bs=v_gmlp_bs, w_a_out=v_w_a_out, conv_w=v_conv_w, conv_b=v_conv_b, conv_ln_g=v_conv_ln_g,
                 conv_ln_b=v_conv_ln_b, w_b_out=v_w_b_out, fox_bf=v_fox_bf, w_c_out=v_w_c_out, w_out=v_w_out,
                 mlp_w1=v_mlp_w1, mlp_w2=v_mlp_w2)
    order = list(weights)
    px, py, pc = _position()
    chip = 2 * px + py
    dev = 2 * chip + pc
    depth = ada_w.shape[0]
    t = x.shape[1]
    xl = x.reshape(t, D)
    tgt = loss_target.reshape(t, D)

    small_in = _pack_rows([c, conv_w])
    gathered = _all_gather8(small_in)
    c_all = gathered[:, :D // LANE, :].reshape(N_DEV, D)
    cw_rows = depth * KW * LANE // LANE
    conv_w_full = jnp.concatenate(
        [gathered[2 * j, D // LANE:D // LANE + cw_rows, :].reshape(depth, KW, LANE) for j in range(N_CHIPS)], axis=2)

    ncol = ada_w.shape[2]
    ada_b_loc = lax.dynamic_slice_in_dim(ada_b, chip * ncol, ncol, axis=1).reshape(depth, 1, ncol)
    mod_sh = _ada_mod(c_all, ada_w, ada_b_loc)
    mod_g = _all_gather8(mod_sh.reshape(-1, LANE)).reshape(N_DEV, depth, N_DEV, ncol)
    mod_all = jnp.concatenate([mod_g[2 * j] for j in range(N_CHIPS)], axis=2)
    mod_mine = lax.dynamic_index_in_dim(mod_all, dev, axis=1, keepdims=False)

    packed = jnp.stack([jnp.concatenate([weights[n][l].astype(BF16).reshape(-1) for n, _, _, _ in BIG]).reshape(-1, LANE)
                        for l in range(depth)])
    gath = lax.dynamic_update_slice(_all_gather_chips(packed), packed[None], (chip, 0, 0, 0))
    layers = []
    for l in range(depth):
        w = {}
        off = 0
        for n, r, cdim, ax in BIG:
            rows = r * cdim // LANE
            seg = gath[:, l, off:off + rows, :].reshape(N_CHIPS, r, cdim)
            off += rows
            w[n] = seg.transpose(1, 0, 2).reshape(r, N_CHIPS * cdim) if ax == 1 else seg.reshape(N_CHIPS * r, cdim)
        w["w_in_p"] = _assemble_w_in(w.pop("w_in"))
        for n in ("mix_pre_g", "mix_post_g", "mlp_pre_g", "mlp_post_g", "gmlp_ln_g", "gmlp_ln_b", "conv_b", "conv_ln_g",
                  "conv_ln_b"):
            w[n] = weights[n][l:l + 1]
        tril = jnp.tril(jnp.ones((CH, CH), F32))
        wsm = gmlp_ws[l] * tril
        w["wsm"] = wsm.astype(BF16)
        w["wsmt"] = jnp.swapaxes(wsm, 1, 2).astype(BF16)
        w["bsx"] = jnp.repeat(gmlp_bs[l].T, HD, axis=1)
        w["conv_w"] = conv_w_full[l]
        w["bfp"] = jnp.pad(fox_bf[l], (0, LANE - NG)).reshape(1, LANE)
        layers.append(w)

    xs, saved = xl, []
    for l in range(depth):
        xs, s = _layer_fwd(xs, mod_mine[l].reshape(NMOD, D), layers[l])
        saved.append(s)
    loss_local, dx = _loss_and_grad(xs, tgt)
    loss = lax.psum(loss_local, ("x", "y", "c"))
    grads, dmods = [None] * depth, [None] * depth
    for l in reversed(range(depth)):
        dx, grads[l], dmods[l] = _layer_bwd(dx, mod_mine[l].reshape(NMOD, D), layers[l], saved[l])
    grad_x = dx.reshape(x.shape)

    def shard_of(gfull, j, r, cdim, ax):
        return gfull[:, j * cdim:(j + 1) * cdim] if ax == 1 else gfull[j * r:(j + 1) * r]

    for l in range(depth):
        grads[l]["w_in"] = _disassemble_w_in(grads[l].pop("w_in_p"))
    gp = jnp.stack([jnp.stack([jnp.concatenate([shard_of(grads[l][n], j, r, cdim, ax).reshape(-1)
                                                for n, r, cdim, ax in BIG]).reshape(-1, LANE)
                               for l in range(depth)]) for j in range(N_CHIPS)])
    from_sibling = _swap_layers(gp)
    chip_sum, chip_sum_bf = _add_own_layer(gp, from_sibling, pc)
    from_chips = _scatter_chips(chip_sum_bf)
    reduced = _add_own_chip(chip_sum, from_chips, chip)
    from_sib = _swap_reduced(reduced)
    by_layer = [jnp.where(pc == l, reduced, from_sib) for l in range(depth)]
    g_out = {}
    off = 0
    for n, r, cdim, _ in BIG:
        rows = r * cdim // LANE
        g_out[n] = jnp.stack([by_layer[l][off:off + rows, :].reshape(r, cdim) for l in range(depth)])
        off += rows

    small_names = [n for n, _ in SMALL]
    small_list = [jnp.stack(dmods)] + [jnp.stack([grads[l][n] for l in range(depth)]) for n in small_names]
    small_list.append(jnp.stack([grads[l]["conv_w"] for l in range(depth)]))
    small_shapes = [(depth, NMOD * D)] + [shp for _, shp in SMALL] + [(depth, KW, DM)]
    small_pack = _pack_rows(small_list)
    small_all = _all_gather8(small_pack)
    small_sum = _unpack_rows(_sum8(small_all), small_shapes)
    g_out["ada_b"] = small_sum[0]
    for n, gs in zip(small_names, small_sum[1:-1]):
        g_out[n] = gs
    g_out["conv_w"] = lax.dynamic_slice_in_dim(small_sum[-1], chip * LANE, LANE, axis=2)
    dmod_all = small_all[:, :depth * NMOD * D // LANE, :].reshape(N_DEV, depth, NMOD * D)
    dmod_loc = lax.dynamic_slice_in_dim(dmod_all, chip * ncol, ncol, axis=2).transpose(1, 0, 2)
    g_out["ada_w"] = _ada_grad(jnp.pad(c_all, ((0, 8), (0, 0))), jnp.pad(dmod_loc, ((0, 0), (0, 8), (0, 0))))

    delta, new_m, new_v = {}, {}, {}
    for n in ("ada_w",) + tuple(b[0] for b in BIG):
        delta[n], new_m[n], new_v[n] = _adamw(weights[n], g_out[n], mom_m[n], mom_v[n], name="adamw_" + n)
    small_params = ["ada_b"] + small_names + ["conv_w"]
    packs = [_pack_rows([d[n] for n in small_params]) for d in (weights, g_out, mom_m, mom_v)]
    outs = _adamw(*packs, name="adamw_small")
    shapes = [weights[n].shape for n in small_params]
    for dst, buf in zip((delta, new_m, new_v), outs):
        for n, a in zip(small_params, _unpack_rows(buf, shapes)):
            dst[n] = a

    return (loss, grad_x, *[g_out[n] for n in order], *[delta[n] for n in order], *[new_m[n] for n in order],
            *[new_v[n] for n in order])
```

```python
import functools

import jax
import jax.numpy as jnp
from jax import lax
from jax.experimental import pallas as pl
from jax.experimental.pallas import tpu as pltpu

F32 = jnp.float32
BF16 = jnp.bfloat16
I32 = jnp.int32
MESH = pl.DeviceIdType.MESH
ANY = pl.BlockSpec(memory_space=pl.ANY)

D = 1024
DM = 512
NG = 8
CH = 128
KW = 31
HALO = 32
DFF = 4096
NMOD = 6
EPS = 1e-6
LANE = 128
N_CHIPS = 4
N_DEV = 8
C_GATE, C_UV, C_GLU, C_Q, C_K, C_V, C_F, D_INP = 0, 3072, 4096, 5120, 5632, 6144, 6656, 7168
D_IN = 6664
VMEM_LIMIT = 56 * 1024 * 1024

ADAM_LR, ADAM_B1, ADAM_B2, ADAM_EPS, ADAM_WD, ADAM_STEP = 0.001, 0.9, 0.999, 1e-08, 0.01, 10

BIG = (("w_in", 1024, 1666, 1), ("w_a_out", 512, 256, 1), ("w_b_out", 512, 256, 1), ("w_c_out", 512, 256, 1),
       ("w_out", 256, 1024, 0), ("mlp_w1", 1024, 1024, 1), ("mlp_w2", 1024, 1024, 0))


def _cparams(sem):
    return pltpu.CompilerParams(dimension_semantics=sem, vmem_limit_bytes=VMEM_LIMIT)


def _sigmoid(x):
    return jax.nn.sigmoid(x)


_GELU_K = 0.7978845608028654
_GELU_A = 0.044715


def _gelu(x):
    t = jnp.tanh(_GELU_K * (x + _GELU_A * x * x * x))
    return 0.5 * x * (1.0 + t)


def _gelu_grad(x):
    t = jnp.tanh(_GELU_K * (x + _GELU_A * x * x * x))
    return 0.5 * (1.0 + t) + 0.5 * x * (1.0 - t * t) * _GELU_K * (1.0 + 3.0 * _GELU_A * x * x)


def _mean(x):
    return jnp.mean(x, axis=-1, keepdims=True)


def _colsum(x):
    return jnp.sum(x, axis=0, keepdims=True)


def _dot(a, b, dims=((1,), (0,))):
    return lax.dot_general(a, b, (dims, ((), ())), preferred_element_type=F32)


NN = ((1,), (0,))
NT = ((1,), (1,))
TN = ((0,), (0,))


def _matmul(a, b, *, name, ta=False, tb=False, out_dtype=F32, tm=1024, tn=1024, tk=1024, epilogue=None, extra=(),
            extra_out=(), b_slabs=False, out_slabs=0):
    m, k = (a.shape[1], a.shape[0]) if ta else a.shape
    if b_slabs:
        ns, brows, bw = b.shape
        n = brows if tb else ns * bw
        assert (ns * bw if tb else brows) == k, (name, b.shape, k)
        tn, tk = (tn, bw) if tb else (bw, tk)
    else:
        n = b.shape[0] if tb else b.shape[1]
    tm, tn, tk = min(tm, m), min(tn, n), min(tk, k)
    assert m % tm == 0 and n % tn == 0 and k % tk == 0, (name, m, n, k, tm, tn, tk)
    assert not out_slabs or (n // out_slabs == tn and epilogue is None), name
    nk = k // tk
    dims = ((0 if ta else 1,), (1 if tb else 0,))
    n_extra = len(extra)
    out_dtypes = (out_dtype,) + tuple(extra_out)

    def body(a_ref, b_ref, *rest):
        extra_refs = rest[:n_extra]
        out_refs = rest[n_extra:n_extra + len(out_dtypes)]
        kk = pl.program_id(2)
        part = _dot(a_ref[...].astype(BF16), b_ref[...].astype(BF16), dims)

        def finish(acc):
            outs = (acc,) if epilogue is None else epilogue(acc, *[r[...] for r in extra_refs])
            for o_ref, o in zip(out_refs, outs):
                o_ref[...] = o.astype(o_ref.dtype)

        if nk == 1:
            finish(part)
        else:
            acc_ref = rest[-1]

            @pl.when(kk == 0)
            def _():
                acc_ref[...] = part

            @pl.when(jnp.logical_and(kk > 0, kk < nk - 1))
            def _():
                acc_ref[...] += part

            @pl.when(kk == nk - 1)
            def _():
                finish(acc_ref[...] + part)

    a_spec = pl.BlockSpec((tk, tm), lambda i, j, kk: (kk, i)) if ta else pl.BlockSpec((tm, tk), lambda i, j, kk: (i, kk))
    if b_slabs and tb:
        b_spec = pl.BlockSpec((None, tn, tk), lambda i, j, kk: (kk, j, 0))
    elif b_slabs:
        b_spec = pl.BlockSpec((None, tk, tn), lambda i, j, kk: (j, kk, 0))
    else:
        b_spec = pl.BlockSpec((tn, tk), lambda i, j, kk: (j, kk)) if tb else pl.BlockSpec((tk, tn), lambda i, j, kk: (kk, j))
    if out_slabs:
        o_spec = pl.BlockSpec((None, tm, tn), lambda i, j, kk: (j, i, 0))
        o_shape = (out_slabs, m, tn)
    else:
        o_spec = pl.BlockSpec((tm, tn), lambda i, j, kk: (i, j))
        o_shape = (m, n)
    outs = pl.pallas_call(
        body, name=name, grid=(m // tm, n // tn, nk),
        in_specs=[a_spec, b_spec] + [o_spec] * n_extra,
        out_specs=[o_spec] * len(out_dtypes),
        out_shape=[jax.ShapeDtypeStruct(o_shape, dt) for dt in out_dtypes],
        scratch_shapes=[pltpu.VMEM((tm, tn), F32)] if nk > 1 else [],
        compiler_params=_cparams(("parallel", "parallel", "arbitrary")),
    )(a, b, *extra)
    return outs[0] if len(outs) == 1 else outs


def _rows(tm, n, col=0):
    return pl.BlockSpec((tm, n), lambda i: (i, col))


def _vec(n):
    return pl.BlockSpec((1, n), lambda i: (0, 0))


def _norm_mod(x, g, sc, sh, *, name, tm=256):
    t = x.shape[0]
    tm = min(tm, t)

    def body(x_ref, g_ref, sc_ref, sh_ref, h_ref):
        xv = x_ref[...]
        inv = lax.rsqrt(_mean(xv * xv) + EPS)
        h_ref[...] = ((xv * inv * g_ref[...]) * (1.0 + sc_ref[...]) + sh_ref[...]).astype(BF16)

    return pl.pallas_call(
        body, name=name, grid=(t // tm,), in_specs=[_rows(tm, D), _vec(D), _vec(D), _vec(D)],
        out_specs=_rows(tm, D), out_shape=jax.ShapeDtypeStruct((t, D), BF16),
        compiler_params=_cparams(("parallel",)))(x, g, sc, sh)


def _resid(x, y, gt, gp, *, name, tm=256):
    t = x.shape[0]
    tm = min(tm, t)

    def body(x_ref, y_ref, gt_ref, gp_ref, o_ref):
        yv = y_ref[...]
        inv = lax.rsqrt(_mean(yv * yv) + EPS)
        o_ref[...] = x_ref[...] + gt_ref[...] * (yv * inv * gp_ref[...])

    return pl.pallas_call(
        body, name=name, grid=(t // tm,), in_specs=[_rows(tm, D), _rows(tm, D), _vec(D), _vec(D)],
        out_specs=_rows(tm, D), out_shape=jax.ShapeDtypeStruct((t, D), F32),
        compiler_params=_cparams(("parallel",)))(x, y, gt, gp)


def _resid_bwd(dx, y, gt, gp, *, name, tm=256):
    t = dx.shape[0]
    tm = min(tm, t)

    def body(dx_ref, y_ref, gt_ref, gp_ref, dy_ref, dgt_ref, dgp_ref):
        @pl.when(pl.program_id(0) == 0)
        def _():
            dgt_ref[...] = jnp.zeros_like(dgt_ref)
            dgp_ref[...] = jnp.zeros_like(dgp_ref)

        dxv, yv, gp_v = dx_ref[...], y_ref[...], gp_ref[...]
        inv = lax.rsqrt(_mean(yv * yv) + EPS)
        yh = yv * inv
        dgt_ref[...] += _colsum(dxv * (yh * gp_v))
        dr = dxv * gt_ref[...]
        dgp_ref[...] += _colsum(dr * yh)
        dyn = dr * gp_v
        dy_ref[...] = (inv * (dyn - yh * _mean(dyn * yh))).astype(BF16)

    return pl.pallas_call(
        body, name=name, grid=(t // tm,), in_specs=[_rows(tm, D), _rows(tm, D), _vec(D), _vec(D)],
        out_specs=[_rows(tm, D), _vec(D), _vec(D)],
        out_shape=[jax.ShapeDtypeStruct((t, D), BF16), jax.ShapeDtypeStruct((1, D), F32),
                   jax.ShapeDtypeStruct((1, D), F32)],
        compiler_params=_cparams(("arbitrary",)))(dx, y, gt, gp)


def _norm_bwd(dh, dx_res, x, g, sc, *, name, tm=256):
    t = dh.shape[0]
    tm = min(tm, t)

    def body(dh_ref, dxr_ref, x_ref, g_ref, sc_ref, dx_ref, dg_ref, dsc_ref, dsh_ref):
        @pl.when(pl.program_id(0) == 0)
        def _():
            dg_ref[...] = jnp.zeros_like(dg_ref)
            dsc_ref[...] = jnp.zeros_like(dsc_ref)
            dsh_ref[...] = jnp.zeros_like(dsh_ref)

        dhv, xv, gv = dh_ref[...], x_ref[...], g_ref[...]
        inv = lax.rsqrt(_mean(xv * xv) + EPS)
        xh = xv * inv
        dsh_ref[...] += _colsum(dhv)
        dsc_ref[...] += _colsum(dhv * (xh * gv))
        dn = dhv * (1.0 + sc_ref[...])
        dg_ref[...] += _colsum(dn * xh)
        dxh = dn * gv
        dx_ref[...] = inv * (dxh - xh * _mean(dxh * xh)) + dxr_ref[...]

    vec_out = jax.ShapeDtypeStruct((1, D), F32)
    return pl.pallas_call(
        body, name=name, grid=(t // tm,), in_specs=[_rows(tm, D), _rows(tm, D), _rows(tm, D), _vec(D), _vec(D)],
        out_specs=[_rows(tm, D), _vec(D), _vec(D), _vec(D)],
        out_shape=[jax.ShapeDtypeStruct((t, D), F32), vec_out, vec_out, vec_out],
        compiler_params=_cparams(("arbitrary",)))(dh, dx_res, x, g, sc)


def _loss_and_grad(x, target, *, tm=256):
    t = x.shape[0]
    tm = min(tm, t)

    def body(x_ref, t_ref, loss_ref, dx_ref):
        @pl.when(pl.program_id(0) == 0)
        def _():
            loss_ref[...] = jnp.zeros_like(loss_ref)

        e = x_ref[...] - t_ref[...]
        dx_ref[...] = e * (1.0 / D)
        s = jnp.sum(jnp.sum(e * e, axis=1, keepdims=True), axis=0, keepdims=True) * (0.5 / D)
        loss_ref[...] += jnp.broadcast_to(s, loss_ref.shape)

    loss, dx = pl.pallas_call(
        body, name="loss", grid=(t // tm,), in_specs=[_rows(tm, D), _rows(tm, D)],
        out_specs=[pl.BlockSpec((8, LANE), lambda i: (0, 0)), _rows(tm, D)],
        out_shape=[jax.ShapeDtypeStruct((8, LANE), F32), jax.ShapeDtypeStruct((t, D), F32)],
        compiler_params=_cparams(("arbitrary",)))(x, target)
    return loss[0, 0], dx


def _gmlp_core(uv, lng, lnb, ws_ref, bsx):
    tm = uv.shape[0]
    gu = _gelu(uv[:, :DM])
    gv = _gelu(uv[:, DM:])
    mu = _mean(gv)
    vc = gv - mu
    rstd = lax.rsqrt(_mean(vc * vc) + EPS)
    vh = vc * rstd
    vln = vh * lng + lnb
    lane = lax.broadcasted_iota(I32, (CH, LANE), 1)
    sv_rows = []
    for nchunk in range(tm // CH):
        vb = vln[nchunk * CH:(nchunk + 1) * CH].astype(BF16)
        cols = []
        for cb in range(DM // LANE):
            vcb = vb[:, cb * LANE:(cb + 1) * LANE]
            lo = _dot(ws_ref[2 * cb], vcb)
            hi = _dot(ws_ref[2 * cb + 1], vcb)
            cols.append(jnp.where(lane < 64, lo, hi))
        sv_rows.append(jnp.concatenate(cols, axis=1) + bsx)
    sv = jnp.concatenate(sv_rows, axis=0) if len(sv_rows) > 1 else sv_rows[0]
    return gu, vh, rstd, vln, sv


def _gmlp_fwd(proj, lng, lnb, wsm, bsx, *, tm=256):
    t = proj.shape[0]
    tm = min(tm, t)

    def body(uv_ref, lng_ref, lnb_ref, ws_ref, bs_ref, ga_ref):
        gu, _, _, _, sv = _gmlp_core(uv_ref[...], lng_ref[...], lnb_ref[...], ws_ref, bs_ref[...])
        ga_ref[...] = (gu * sv).astype(BF16)

    return pl.pallas_call(
        body, name="gmlp_fwd", grid=(t // tm,),
        in_specs=[_rows(tm, 2 * DM, C_UV // (2 * DM)), _vec(DM), _vec(DM),
                  pl.BlockSpec((NG, CH, CH), lambda i: (0, 0, 0)), pl.BlockSpec((CH, DM), lambda i: (0, 0))],
        out_specs=_rows(tm, DM), out_shape=jax.ShapeDtypeStruct((t, DM), BF16),
        compiler_params=_cparams(("parallel",)))(proj, lng, lnb, wsm, bsx)


def _gmlp_bwd(dga, proj, lng, lnb, wsm, wsmt, bsx, *, tm=256):
    t = proj.shape[0]
    tm = min(tm, t)

    def body(dga_ref, uv_ref, lng_ref, lnb_ref, ws_ref, wst_ref, bs_ref, duv_ref, dws_ref, dbs_ref, dlng_ref, dlnb_ref,
             dbsx_ref):
        i = pl.program_id(0)

        @pl.when(i == 0)
        def _():
            dws_ref[...] = jnp.zeros_like(dws_ref)
            dbsx_ref[...] = jnp.zeros_like(dbsx_ref)
            dlng_ref[...] = jnp.zeros_like(dlng_ref)
            dlnb_ref[...] = jnp.zeros_like(dlnb_ref)

        uv = uv_ref[...]
        lng_v = lng_ref[...]
        gu, vh, rstd, vln, sv = _gmlp_core(uv, lng_v, lnb_ref[...], ws_ref, bs_ref[...])
        dga_v = dga_ref[...]
        dgu = dga_v * sv
        dsv = dga_v * gu
        lane = lax.broadcasted_iota(I32, (CH, LANE), 1)
        tril = lax.broadcasted_iota(I32, (CH, CH), 0) >= lax.broadcasted_iota(I32, (CH, CH), 1)
        dvln_rows = []
        for nchunk in range(tm // CH):
            rows = slice(nchunk * CH, (nchunk + 1) * CH)
            dbsx_ref[...] += dsv[rows]
            vb = vln[rows].astype(BF16)
            cols = []
            for cb in range(DM // LANE):
                cs = slice(cb * LANE, (cb + 1) * LANE)
                dsvb = dsv[rows, cs]
                vcb = vb[:, cs]
                dlo = jnp.where(lane < 64, dsvb, 0.0).astype(BF16)
                dhi = jnp.where(lane < 64, 0.0, dsvb).astype(BF16)
                dws_ref[2 * cb] += jnp.where(tril, _dot(dlo, vcb, NT), 0.0)
                dws_ref[2 * cb + 1] += jnp.where(tril, _dot(dhi, vcb, NT), 0.0)
                dsb = dsvb.astype(BF16)
                cols.append(jnp.where(lane < 64, _dot(wst_ref[2 * cb], dsb), _dot(wst_ref[2 * cb + 1], dsb)))
            dvln_rows.append(jnp.concatenate(cols, axis=1))
        dvln = jnp.concatenate(dvln_rows, axis=0) if len(dvln_rows) > 1 else dvln_rows[0]
        dlnb_ref[...] += _colsum(dvln)
        dlng_ref[...] += _colsum(dvln * vh)
        dvh = dvln * lng_v
        dgv = rstd * (dvh - _mean(dvh) - vh * _mean(dvh * vh))
        duv_ref[:, :DM] = (dgu * _gelu_grad(uv[:, :DM])).astype(BF16)
        duv_ref[:, DM:] = (dgv * _gelu_grad(uv[:, DM:])).astype(BF16)

        @pl.when(i == pl.num_programs(0) - 1)
        def _():
            ind = (lax.broadcasted_iota(I32, (DM, LANE), 0) // 64 == lax.broadcasted_iota(I32, (DM, LANE), 1)).astype(F32)
            dbs_ref[...] = jnp.dot(dbsx_ref[...], ind, preferred_element_type=F32, precision=lax.Precision.HIGHEST)

    vec_out = jax.ShapeDtypeStruct((1, DM), F32)
    outs = pl.pallas_call(
        body, name="gmlp_bwd", grid=(t // tm,),
        in_specs=[_rows(tm, DM), _rows(tm, 2 * DM, C_UV // (2 * DM)), _vec(DM), _vec(DM),
                  pl.BlockSpec((NG, CH, CH), lambda i: (0, 0, 0)), pl.BlockSpec((NG, CH, CH), lambda i: (0, 0, 0)),
                  pl.BlockSpec((CH, DM), lambda i: (0, 0))],
        out_specs=[_rows(tm, 2 * DM), pl.BlockSpec((NG, CH, CH), lambda i: (0, 0, 0)),
                   pl.BlockSpec((CH, LANE), lambda i: (0, 0)), _vec(DM), _vec(DM)],
        out_shape=[jax.ShapeDtypeStruct((t, 2 * DM), BF16), jax.ShapeDtypeStruct((NG, CH, CH), F32),
                   jax.ShapeDtypeStruct((CH, LANE), F32), vec_out, vec_out],
        scratch_shapes=[pltpu.VMEM((CH, DM), F32)],
        compiler_params=_cparams(("arbitrary",)))(dga, proj, lng, lnb, wsm, wsmt, bsx)
    return outs


def _glu_into(zs_ref, glu_ref, halo_ref, first):
    hal = halo_ref[...]
    z0h = hal[:, :DM] * _sigmoid(hal[:, DM:])
    zs_ref[0:HALO, :] = jnp.where(first, 0.0, z0h)
    g = glu_ref[...]
    zs_ref[HALO:, :] = g[:, :DM] * _sigmoid(g[:, DM:])


def _conv_fwd(proj, cw, cb, lng, lnb, *, tm=256, rb=64):
    t = proj.shape[0]
    tm = min(tm, t)
    hb = tm // HALO
    gcol = C_GLU // (2 * DM)

    def body(glu_ref, halo_ref, cw_ref, cb_ref, lng_ref, lnb_ref, zc_ref, zb_ref, zs_ref):
        i = pl.program_id(0)
        _glu_into(zs_ref, glu_ref, halo_ref, i == 0)
        for cbk in range(DM // LANE):
            cs = slice(cbk * LANE, (cbk + 1) * LANE)
            for r in range(tm // rb):
                acc = jnp.broadcast_to(cb_ref[:, cs], (rb, LANE))
                for k in range(KW):
                    off = r * rb + HALO - (KW - 1) + k
                    acc = acc + cw_ref[k:k + 1, cs] * zs_ref[off:off + rb, cs]
                zc_ref[r * rb:(r + 1) * rb, cs] = acc
        zc = zc_ref[...]
        mu = _mean(zc)
        zcc = zc - mu
        zh = zcc * lax.rsqrt(_mean(zcc * zcc) + EPS)
        a = zh * lng_ref[...] + lnb_ref[...]
        zb_ref[...] = (a * _sigmoid(a)).astype(BF16)

    return pl.pallas_call(
        body, name="conv_fwd", grid=(t // tm,),
        in_specs=[_rows(tm, 2 * DM, gcol),
                  pl.BlockSpec((HALO, 2 * DM), lambda i: (jnp.maximum(i * hb - 1, 0), gcol)),
                  pl.BlockSpec((KW, DM), lambda i: (0, 0)), _vec(DM), _vec(DM), _vec(DM)],
        out_specs=[_rows(tm, DM), _rows(tm, DM)],
        out_shape=[jax.ShapeDtypeStruct((t, DM), F32), jax.ShapeDtypeStruct((t, DM), BF16)],
        scratch_shapes=[pltpu.VMEM((HALO + tm, DM), F32)],
        compiler_params=_cparams(("parallel",)))(proj, proj, cw, cb, lng, lnb)


def _conv_bwd_ln(dzb, zc, lng, lnb, *, tm=256):
    t = zc.shape[0]
    tm = min(tm, t)

    def body(dzb_ref, zc_ref, lng_ref, lnb_ref, dzc_ref, dlng_ref, dlnb_ref):
        @pl.when(pl.program_id(0) == 0)
        def _():
            dlng_ref[...] = jnp.zeros_like(dlng_ref)
            dlnb_ref[...] = jnp.zeros_like(dlnb_ref)

        zc = zc_ref[...]
        lng_v = lng_ref[...]
        mu = _mean(zc)
        zcc = zc - mu
        rstd = lax.rsqrt(_mean(zcc * zcc) + EPS)
        zh = zcc * rstd
        a = zh * lng_v + lnb_ref[...]
        s = _sigmoid(a)
        da = dzb_ref[...] * (s * (1.0 + a * (1.0 - s)))
        dlnb_ref[...] += _colsum(da)
        dlng_ref[...] += _colsum(da * zh)
        dzh = da * lng_v
        dzc_ref[...] = rstd * (dzh - _mean(dzh) - zh * _mean(dzh * zh))

    vec_out = jax.ShapeDtypeStruct((1, DM), F32)
    return pl.pallas_call(
        body, name="conv_bwd_ln", grid=(t // tm,), in_specs=[_rows(tm, DM), _rows(tm, DM), _vec(DM), _vec(DM)],
        out_specs=[_rows(tm, DM), _vec(DM), _vec(DM)],
        out_shape=[jax.ShapeDtypeStruct((t, DM), F32), vec_out, vec_out],
        compiler_params=_cparams(("arbitrary",)))(dzb, zc, lng, lnb)


def _conv_bwd(dzc, proj, cw, *, tm=256, rb=64):
    t = proj.shape[0]
    tm = min(tm, t)
    hb = tm // HALO
    nblk = t // tm
    gcol = C_GLU // (2 * DM)

    def body(dzc_ref, dnext_ref, glu_ref, halo_ref, cw_ref, dglu_ref, dcw_ref, dcb_ref, zs_ref, ds_ref):
        i = pl.program_id(0)

        @pl.when(i == 0)
        def _():
            dcw_ref[...] = jnp.zeros_like(dcw_ref)
            dcb_ref[...] = jnp.zeros_like(dcb_ref)

        _glu_into(zs_ref, glu_ref, halo_ref, i == 0)
        dzc = dzc_ref[...]
        ds_ref[0:tm, :] = dzc
        ds_ref[tm:, :] = jnp.where(i == nblk - 1, 0.0, dnext_ref[...])
        dcb_ref[...] += _colsum(dzc)
        for k in range(KW):
            off = HALO - (KW - 1) + k
            dcw_ref[k:k + 1, :] += _colsum(dzc * zs_ref[off:off + tm, :])
        g = glu_ref[...]
        val, sg = g[:, :DM], _sigmoid(g[:, DM:])
        for cbk in range(DM // LANE):
            cs = slice(cbk * LANE, (cbk + 1) * LANE)
            for r in range(tm // rb):
                acc = jnp.zeros((rb, LANE), F32)
                for k in range(KW):
                    off = r * rb + (KW - 1) - k
                    acc = acc + cw_ref[k:k + 1, cs] * ds_ref[off:off + rb, cs]
                rs = slice(r * rb, (r + 1) * rb)
                dglu_ref[rs, cs] = (acc * sg[rs, cs]).astype(BF16)
                v, s = val[rs, cs], sg[rs, cs]
                dglu_ref[rs, DM + cbk * LANE:DM + (cbk + 1) * LANE] = (acc * v * s * (1.0 - s)).astype(BF16)

    return pl.pallas_call(
        body, name="conv_bwd", grid=(nblk,),
        in_specs=[_rows(tm, DM),
                  pl.BlockSpec((HALO, DM), lambda i: (jnp.minimum((i + 1) * hb, nblk * hb - 1), 0)),
                  _rows(tm, 2 * DM, gcol),
                  pl.BlockSpec((HALO, 2 * DM), lambda i: (jnp.maximum(i * hb - 1, 0), gcol)),
                  pl.BlockSpec((KW, DM), lambda i: (0, 0))],
        out_specs=[_rows(tm, 2 * DM), pl.BlockSpec((HALO, DM), lambda i: (0, 0)), _vec(DM)],
        out_shape=[jax.ShapeDtypeStruct((t, 2 * DM), BF16), jax.ShapeDtypeStruct((HALO, DM), F32),
                   jax.ShapeDtypeStruct((1, DM), F32)],
        scratch_shapes=[pltpu.VMEM((HALO + tm, DM), F32), pltpu.VMEM((tm + HALO, DM), F32)],
        compiler_params=_cparams(("arbitrary",)))(dzc, dzc, proj, proj, cw)


def _log_sigmoid(x):
    return jnp.minimum(x, 0.0) - jnp.log1p(jnp.exp(-jnp.abs(x)))


def _fox_cum(proj, bfp):
    t = proj.shape[0]
    fcol = C_F // LANE

    def body(f_ref, bf_ref, cum_ref, carry_ref):
        @pl.when(pl.program_id(0) == 0)
        def _():
            carry_ref[...] = jnp.zeros_like(carry_ref)

        lf = _log_sigmoid(f_ref[...] + bf_ref[...])
        tri = (lax.broadcasted_iota(I32, (CH, CH), 0) >= lax.broadcasted_iota(I32, (CH, CH), 1)).astype(F32)
        cum = jnp.dot(tri, lf, preferred_element_type=F32, precision=lax.Precision.HIGHEST) + carry_ref[0:1, :]
        cum_ref[...] = cum
        carry_ref[...] = jnp.broadcast_to(cum[CH - 1:CH, :], carry_ref.shape)

    return pl.pallas_call(
        body, name="fox_cum", grid=(t // CH,), in_specs=[_rows(CH, LANE, fcol), _vec(LANE)],
        out_specs=_rows(CH, LANE), out_shape=jax.ShapeDtypeStruct((t, LANE), F32),
        scratch_shapes=[pltpu.VMEM((8, LANE), F32)],
        compiler_params=_cparams(("arbitrary",)))(proj, bfp)


def _fox_cum_bwd(dcum, proj, bfp):
    t = proj.shape[0]
    nb = t // CH
    fcol = C_F // LANE
    fw = D_INP - C_F

    def body(dc_ref, f_ref, bf_ref, df_ref, dbf_ref, carry_ref):
        @pl.when(pl.program_id(0) == 0)
        def _():
            carry_ref[...] = jnp.zeros_like(carry_ref)
            dbf_ref[...] = jnp.zeros_like(dbf_ref)

        triu = (lax.broadcasted_iota(I32, (CH, CH), 0) <= lax.broadcasted_iota(I32, (CH, CH), 1)).astype(F32)
        dlf = jnp.dot(triu, dc_ref[...], preferred_element_type=F32, precision=lax.Precision.HIGHEST) + carry_ref[0:1, :]
        carry_ref[...] = jnp.broadcast_to(dlf[0:1, :], carry_ref.shape)
        z = f_ref[...] + bf_ref[...]
        lane = lax.broadcasted_iota(I32, (CH, LANE), 1)
        df = jnp.where(lane < NG, dlf * _sigmoid(-z), 0.0)
        dbf_ref[...] += _colsum(df)
        df_ref[:, 0:LANE] = df.astype(BF16)
        df_ref[:, LANE:] = jnp.zeros((CH, fw - LANE), BF16)

    return pl.pallas_call(
        body, name="fox_cum_bwd", grid=(nb,),
        in_specs=[pl.BlockSpec((CH, LANE), lambda i: (nb - 1 - i, 0)),
                  pl.BlockSpec((CH, LANE), lambda i: (nb - 1 - i, fcol)), _vec(LANE)],
        out_specs=[pl.BlockSpec((CH, fw), lambda i: (nb - 1 - i, 0)), _vec(LANE)],
        out_shape=[jax.ShapeDtypeStruct((t, fw), BF16), jax.ShapeDtypeStruct((1, LANE), F32)],
        scratch_shapes=[pltpu.VMEM((8, LANE), F32)],
        compiler_params=_cparams(("arbitrary",)))(dcum, proj, bfp)


HD = 64
ATT_SCALE = 0.125
NEG = -1e30


def _qkv_prep(proj, *, tm=512):
    t = proj.shape[0]
    tm = min(tm, t)

    def body(q_ref, k_ref, v_ref, o_ref):
        o_ref[:, 0:DM] = (q_ref[...] * ATT_SCALE).astype(BF16)
        o_ref[:, DM:2 * DM] = k_ref[...].astype(BF16)
        o_ref[:, 2 * DM:] = v_ref[...].astype(BF16)

    return pl.pallas_call(
        body, name="qkv_prep", grid=(t // tm,),
        in_specs=[_rows(tm, DM, C_Q // DM), _rows(tm, DM, C_K // DM), _rows(tm, DM, C_V // DM)],
        out_specs=_rows(tm, 3 * DM), out_shape=jax.ShapeDtypeStruct((t, 3 * DM), BF16),
        compiler_params=_cparams(("parallel",)))(proj, proj, proj)


def _causal_pairs(nq, outer_is_query):
    if outer_is_query:
        pairs = [(i, j) for i in range(nq) for j in range(i + 1)]
    else:
        pairs = [(j, i) for j in range(nq) for i in range(j, nq)]
    return (jnp.asarray([p[0] for p in pairs], I32), jnp.asarray([p[1] for p in pairs], I32))


def _to_row(col):
    return jnp.transpose(col)[0:1, :]


def _rep(x, tk):
    return x if tk == LANE else jnp.tile(x, (1, tk // LANE))


def _attn_fwd(qkv, ckrow, *, tq=256):
    t = qkv.shape[0]
    tq = min(tq, t)
    tk = tq
    nq = t // tq
    oi, ij = _causal_pairs(nq, True)

    def body(oi_ref, ij_ref, q_ref, k_ref, v_ref, ck_ref, o_ref, lse_ref, lser_ref, m_ref, l_ref, acc_ref):
        n = pl.program_id(0)
        i, j = oi_ref[n], ij_ref[n]

        @pl.when(j == 0)
        def _():
            m_ref[...] = jnp.full_like(m_ref, NEG)
            l_ref[...] = jnp.zeros_like(l_ref)
            acc_ref[...] = jnp.zeros_like(acc_ref)

        def step(masked):
            if masked:
                keep = lax.broadcasted_iota(I32, (tq, tk), 1) <= lax.broadcasted_iota(I32, (tq, tk), 0)
            for h in range(NG):
                hs = slice(h * HD, (h + 1) * HD)
                s = _dot(q_ref[:, hs], k_ref[:, hs], NT) - ck_ref[h:h + 1, :]
                if masked:
                    s = jnp.where(keep, s, NEG)
                m_prev = m_ref[h]
                m_new = jnp.maximum(m_prev, jnp.max(s, axis=1, keepdims=True))
                alpha = jnp.exp(m_prev - m_new)
                p = jnp.exp(s - _rep(m_new, tk))
                l_ref[h] = alpha * l_ref[h] + jnp.sum(p, axis=1, keepdims=True)
                acc_ref[:, hs] = alpha[:, :HD] * acc_ref[:, hs] + _dot(p.astype(BF16), v_ref[:, hs])
                m_ref[h] = m_new

        @pl.when(j < i)
        def _():
            step(False)

        @pl.when(j == i)
        def _():
            step(True)
            for h in range(NG):
                hs = slice(h * HD, (h + 1) * HD)
                l = l_ref[h]
                o_ref[:, hs] = (acc_ref[:, hs] / l[:, :HD]).astype(BF16)
                lse = m_ref[h] + jnp.log(l)
                lse_ref[h] = lse
                lser_ref[h:h + 1, :] = _to_row(lse)

    gs = pltpu.PrefetchScalarGridSpec(
        num_scalar_prefetch=2, grid=(int(oi.shape[0]),),
        in_specs=[pl.BlockSpec((tq, DM), lambda n, a, b: (a[n], 0)),
                  pl.BlockSpec((tk, DM), lambda n, a, b: (b[n], 1)),
                  pl.BlockSpec((tk, DM), lambda n, a, b: (b[n], 2)),
                  pl.BlockSpec((NG, tk), lambda n, a, b: (0, b[n]))],
        out_specs=[pl.BlockSpec((tq, DM), lambda n, a, b: (a[n], 0)),
                   pl.BlockSpec((NG, tq, LANE), lambda n, a, b: (0, a[n], 0)),
                   pl.BlockSpec((NG, tq), lambda n, a, b: (0, a[n]))],
        scratch_shapes=[pltpu.VMEM((NG, tq, LANE), F32), pltpu.VMEM((NG, tq, LANE), F32), pltpu.VMEM((tq, DM), F32)])
    return pl.pallas_call(
        body, name="attn_fwd", grid_spec=gs,
        out_shape=[jax.ShapeDtypeStruct((t, DM), BF16), jax.ShapeDtypeStruct((NG, t, LANE), F32),
                   jax.ShapeDtypeStruct((NG, t), F32)],
        compiler_params=_cparams(("arbitrary",)))(oi, ij, qkv, qkv, qkv, ckrow)


def _attn_bwd_dq(qkv, ckrow, o, do, lse, *, tq=256):
    t = qkv.shape[0]
    tq = min(tq, t)
    tk = tq
    nq = t // tq
    oi, ij = _causal_pairs(nq, True)

    def body(oi_ref, ij_ref, q_ref, k_ref, v_ref, ck_ref, o_ref, do_ref, lse_ref, dq_ref, deltar_ref, dcqr_ref, delta_ref,
             dcq_ref, acc_ref):
        n = pl.program_id(0)
        i, j = oi_ref[n], ij_ref[n]

        @pl.when(j == 0)
        def _():
            acc_ref[...] = jnp.zeros_like(acc_ref)
            dcq_ref[...] = jnp.zeros_like(dcq_ref)
            for h in range(NG):
                hs = slice(h * HD, (h + 1) * HD)
                d = jnp.sum(do_ref[:, hs] * o_ref[:, hs].astype(F32), axis=1, keepdims=True)
                dcol = jnp.broadcast_to(d, (tq, LANE))
                delta_ref[h] = dcol
                deltar_ref[h:h + 1, :] = _to_row(dcol)

        def step(masked):
            if masked:
                keep = lax.broadcasted_iota(I32, (tq, tk), 1) <= lax.broadcasted_iota(I32, (tq, tk), 0)
            for h in range(NG):
                hs = slice(h * HD, (h + 1) * HD)
                s = _dot(q_ref[:, hs], k_ref[:, hs], NT) - ck_ref[h:h + 1, :]
                if masked:
                    s = jnp.where(keep, s, NEG)
                p = jnp.exp(s - _rep(lse_ref[h], tk))
                dp = _dot(do_ref[:, hs].astype(BF16), v_ref[:, hs], NT)
                ds = p * (dp - _rep(delta_ref[h], tk))
                dcq_ref[h] += jnp.sum(ds, axis=1, keepdims=True)
                acc_ref[:, hs] += _dot(ds.astype(BF16), k_ref[:, hs])

        @pl.when(j < i)
        def _():
            step(False)

        @pl.when(j == i)
        def _():
            step(True)
            dq_ref[...] = (acc_ref[...] * ATT_SCALE).astype(BF16)
            for h in range(NG):
                dcqr_ref[h:h + 1, :] = _to_row(dcq_ref[h])

    gs = pltpu.PrefetchScalarGridSpec(
        num_scalar_prefetch=2, grid=(int(oi.shape[0]),),
        in_specs=[pl.BlockSpec((tq, DM), lambda n, a, b: (a[n], 0)),
                  pl.BlockSpec((tk, DM), lambda n, a, b: (b[n], 1)),
                  pl.BlockSpec((tk, DM), lambda n, a, b: (b[n], 2)),
                  pl.BlockSpec((NG, tk), lambda n, a, b: (0, b[n])),
                  pl.BlockSpec((tq, DM), lambda n, a, b: (a[n], 0)),
                  pl.BlockSpec((tq, DM), lambda n, a, b: (a[n], 0)),
                  pl.BlockSpec((NG, tq, LANE), lambda n, a, b: (0, a[n], 0))],
        out_specs=[pl.BlockSpec((tq, DM), lambda n, a, b: (a[n], 0)),
                   pl.BlockSpec((NG, tq), lambda n, a, b: (0, a[n])),
                   pl.BlockSpec((NG, tq), lambda n, a, b: (0, a[n]))],
        scratch_shapes=[pltpu.VMEM((NG, tq, LANE), F32), pltpu.VMEM((NG, tq, LANE), F32), pltpu.VMEM((tq, DM), F32)])
    return pl.pallas_call(
        body, name="attn_bwd_dq", grid_spec=gs,
        out_shape=[jax.ShapeDtypeStruct((t, DM), BF16), jax.ShapeDtypeStruct((NG, t), F32), jax.ShapeDtypeStruct((NG, t), F32)],
        compiler_params=_cparams(("arbitrary",)))(oi, ij, qkv, qkv, qkv, ckrow, o, do, lse)


def _attn_bwd_dkv(qkv, ckcol, do, lserow, deltarow, *, tq=256):
    t = qkv.shape[0]
    tq = min(tq, t)
    tk = tq
    nq = t // tq
    oj, ii = _causal_pairs(nq, False)

    def body(oj_ref, ii_ref, q_ref, k_ref, v_ref, ck_ref, do_ref, lse_ref, delta_ref, dk_ref, dv_ref, dckr_ref, dka_ref,
             dva_ref, dck_ref):
        n = pl.program_id(0)
        j, i = oj_ref[n], ii_ref[n]

        @pl.when(i == j)
        def _():
            dka_ref[...] = jnp.zeros_like(dka_ref)
            dva_ref[...] = jnp.zeros_like(dva_ref)
            dck_ref[...] = jnp.zeros_like(dck_ref)

        def step(masked):
            if masked:
                keep = lax.broadcasted_iota(I32, (tk, tq), 0) <= lax.broadcasted_iota(I32, (tk, tq), 1)
            for h in range(NG):
                hs = slice(h * HD, (h + 1) * HD)
                st = _dot(k_ref[:, hs], q_ref[:, hs], NT) - _rep(ck_ref[h], tq)
                if masked:
                    st = jnp.where(keep, st, NEG)
                pt = jnp.exp(st - lse_ref[h:h + 1, :])
                dob = do_ref[:, hs].astype(BF16)
                dva_ref[:, hs] += _dot(pt.astype(BF16), dob)
                dpt = _dot(v_ref[:, hs], dob, NT)
                dst = pt * (dpt - delta_ref[h:h + 1, :])
                dka_ref[:, hs] += _dot(dst.astype(BF16), q_ref[:, hs])
                dck_ref[h] -= jnp.sum(dst, axis=1, keepdims=True)

        @pl.when(i == j)
        def _():
            step(True)

        @pl.when(i > j)
        def _():
            step(False)

        @pl.when(i == nq - 1)
        def _():
            dk_ref[...] = dka_ref[...].astype(BF16)
            dv_ref[...] = dva_ref[...].astype(BF16)
            for h in range(NG):
                dckr_ref[h:h + 1, :] = _to_row(dck_ref[h])

    gs = pltpu.PrefetchScalarGridSpec(
        num_scalar_prefetch=2, grid=(int(oj.shape[0]),),
        in_specs=[pl.BlockSpec((tq, DM), lambda n, a, b: (b[n], 0)),
                  pl.BlockSpec((tk, DM), lambda n, a, b: (a[n], 1)),
                  pl.BlockSpec((tk, DM), lambda n, a, b: (a[n], 2)),
                  pl.BlockSpec((NG, tk, LANE), lambda n, a, b: (0, a[n], 0)),
                  pl.BlockSpec((tq, DM), lambda n, a, b: (b[n], 0)),
                  pl.BlockSpec((NG, tq), lambda n, a, b: (0, b[n])),
                  pl.BlockSpec((NG, tq), lambda n, a, b: (0, b[n]))],
        out_specs=[pl.BlockSpec((tk, DM), lambda n, a, b: (a[n], 0)),
                   pl.BlockSpec((tk, DM), lambda n, a, b: (a[n], 0)),
                   pl.BlockSpec((NG, tk), lambda n, a, b: (0, a[n]))],
        scratch_shapes=[pltpu.VMEM((tk, DM), F32), pltpu.VMEM((tk, DM), F32), pltpu.VMEM((NG, tk, LANE), F32)])
    return pl.pallas_call(
        body, name="attn_bwd_dkv", grid_spec=gs,
        out_shape=[jax.ShapeDtypeStruct((t, DM), BF16), jax.ShapeDtypeStruct((t, DM), BF16), jax.ShapeDtypeStruct((NG, t), F32)],
        compiler_params=_cparams(("arbitrary",)))(oj, ii, qkv, qkv, qkv, ckcol, do, lserow, deltarow)


def _merge_fwd(ga, zb, att, proj, wa, wb, wc, *, tm=256):
    t = ga.shape[0]
    tm = min(tm, t)
    wspec = pl.BlockSpec((DM, D), lambda i: (0, 0))

    def body(ga_ref, zb_ref, att_ref, gate_ref, wa_ref, wb_ref, wc_ref, m_ref):
        acc = jnp.zeros((tm, D), F32)
        for b, (x_ref, w_ref) in enumerate(((ga_ref, wa_ref), (zb_ref, wb_ref), (att_ref, wc_ref))):
            acc = acc + _sigmoid(gate_ref[:, b * D:(b + 1) * D]) * _dot(x_ref[...], w_ref[...])
        m_ref[...] = acc.astype(BF16)

    return pl.pallas_call(
        body, name="merge_fwd", grid=(t // tm,),
        in_specs=[_rows(tm, DM), _rows(tm, DM), _rows(tm, DM), _rows(tm, 3 * D, 0), wspec, wspec, wspec],
        out_specs=_rows(tm, D), out_shape=jax.ShapeDtypeStruct((t, D), BF16),
        compiler_params=_cparams(("parallel",)))(ga, zb, att, proj, wa, wb, wc)


def _merge_bwd(dm, ga, zb, att, proj, wa, wb, wc, *, tm=256):
    t = ga.shape[0]
    tm = min(tm, t)
    wspec = pl.BlockSpec((DM, D), lambda i: (0, 0))

    def body(dm_ref, ga_ref, zb_ref, att_ref, gate_ref, wa_ref, wb_ref, wc_ref, dgate_ref, dga_ref, dzb_ref, datt_ref,
             dwa_ref, dwb_ref, dwc_ref):
        @pl.when(pl.program_id(0) == 0)
        def _():
            dwa_ref[...] = jnp.zeros_like(dwa_ref)
            dwb_ref[...] = jnp.zeros_like(dwb_ref)
            dwc_ref[...] = jnp.zeros_like(dwc_ref)

        dmv = dm_ref[...]
        branches = ((ga_ref, wa_ref, dga_ref, dwa_ref), (zb_ref, wb_ref, dzb_ref, dwb_ref),
                    (att_ref, wc_ref, datt_ref, dwc_ref))
        for b, (x_ref, w_ref, dx_ref, dw_ref) in enumerate(branches):
            xv, wv = x_ref[...], w_ref[...]
            y = _dot(xv, wv)
            g = _sigmoid(gate_ref[:, b * D:(b + 1) * D])
            dgate_ref[:, b * D:(b + 1) * D] = (dmv * y * g * (1.0 - g)).astype(BF16)
            dy = (dmv * g).astype(BF16)
            dx_ref[...] = _dot(dy, wv, NT)
            dw_ref[...] += _dot(xv, dy, TN)

    return pl.pallas_call(
        body, name="merge_bwd", grid=(t // tm,),
        in_specs=[_rows(tm, D), _rows(tm, DM), _rows(tm, DM), _rows(tm, DM), _rows(tm, 3 * D, 0), wspec, wspec, wspec],
        out_specs=[_rows(tm, 3 * D), _rows(tm, DM), _rows(tm, DM), _rows(tm, DM), wspec, wspec, wspec],
        out_shape=[jax.ShapeDtypeStruct((t, 3 * D), BF16)] + [jax.ShapeDtypeStruct((t, DM), F32)] * 3
        + [jax.ShapeDtypeStruct((DM, D), F32)] * 3,
        compiler_params=_cparams(("arbitrary",)))(dm, ga, zb, att, proj, wa, wb, wc)


def _heads_layout(cum):
    t = cum.shape[0]
    ckrow = cum[:, :NG].T
    return ckrow, jnp.broadcast_to(ckrow[:, :, None], (NG, t, LANE))


def _layer_fwd(x, mod, w):
    sh1, sc1, gt1, sh2, sc2, gt2 = (mod[k:k + 1] for k in range(NMOD))
    h1 = _norm_mod(x, w["mix_pre_g"], sc1, sh1, name="norm_mix")
    proj = _matmul(h1, w["w_in_p"], name="mm_proj")
    ga = _gmlp_fwd(proj, w["gmlp_ln_g"], w["gmlp_ln_b"], w["wsm"], w["bsx"])
    zc, zb = _conv_fwd(proj, w["conv_w"], w["conv_b"], w["conv_ln_g"], w["conv_ln_b"])
    cum = _fox_cum(proj, w["bfp"])
    ckrow, ckcol = _heads_layout(cum)
    qkv = _qkv_prep(proj)
    att, lse, lser = _attn_fwd(qkv, ckrow)
    merged = _merge_fwd(ga, zb, att, proj, w["w_a_out"], w["w_b_out"], w["w_c_out"])
    y1 = _matmul(merged, w["w_out"], name="mm_out")
    x2 = _resid(x, y1, gt1, w["mix_post_g"], name="resid_mix")
    h2 = _norm_mod(x2, w["mlp_pre_g"], sc2, sh2, name="norm_mlp")
    a, hid = _matmul(h2, w["mlp_w1"], name="mm_w1", b_slabs=True, extra_out=(BF16,),
                     epilogue=lambda acc: (acc, jnp.square(jnp.maximum(acc, 0.0))))
    y2 = _matmul(hid, w["mlp_w2"], name="mm_w2")
    x3 = _resid(x2, y2, gt2, w["mlp_post_g"], name="resid_mlp")
    saved = dict(x=x, h1=h1, proj=proj, ga=ga, zc=zc, zb=zb, qkv=qkv, ckrow=ckrow, ckcol=ckcol, att=att, lse=lse, lser=lser, merged=merged,
                 y1=y1, x2=x2, h2=h2, a=a, hid=hid, y2=y2)
    return x3, saved


def _layer_bwd(dx3, mod, w, s):
    sh1, sc1, gt1, sh2, sc2, gt2 = (mod[k:k + 1] for k in range(NMOD))
    g = {}
    dy2, dgt2, g["mlp_post_g"] = _resid_bwd(dx3, s["y2"], gt2, w["mlp_post_g"], name="resid_mlp_bwd")
    da = _matmul(dy2, w["mlp_w2"], tb=True, name="mm_dhid", out_dtype=BF16, extra=(s["a"],),
                 epilogue=lambda acc, a: (acc * (2.0 * jnp.maximum(a, 0.0)),))
    g["mlp_w2"] = _matmul(s["hid"], dy2, ta=True, name="mm_dw2")
    g["mlp_w1"] = _matmul(s["h2"], da, ta=True, name="mm_dw1", out_slabs=N_CHIPS)
    dh2 = _matmul(da, w["mlp_w1"], tb=True, name="mm_dh2", b_slabs=True)
    dx2, g["mlp_pre_g"], dsc2, dsh2 = _norm_bwd(dh2, dx3, s["x2"], w["mlp_pre_g"], sc2, name="norm_mlp_bwd")
    dy1, dgt1, g["mix_post_g"] = _resid_bwd(dx2, s["y1"], gt1, w["mix_post_g"], name="resid_mix_bwd")
    dmerged = _matmul(dy1, w["w_out"], tb=True, name="mm_dmerged")
    g["w_out"] = _matmul(s["merged"], dy1, ta=True, name="mm_dwout")
    dgate, dga, dzb, datt, g["w_a_out"], g["w_b_out"], g["w_c_out"] = _merge_bwd(
        dmerged, s["ga"], s["zb"], s["att"], s["proj"], w["w_a_out"], w["w_b_out"], w["w_c_out"])
    duv, g["gmlp_ws"], dbs, g["gmlp_ln_g"], g["gmlp_ln_b"] = _gmlp_bwd(
        dga, s["proj"], w["gmlp_ln_g"], w["gmlp_ln_b"], w["wsm"], w["wsmt"], w["bsx"])
    g["gmlp_bs"] = dbs[:, :NG].T
    dzc, g["conv_ln_g"], g["conv_ln_b"] = _conv_bwd_ln(dzb, s["zc"], w["conv_ln_g"], w["conv_ln_b"])
    dglu, dcw, g["conv_b"] = _conv_bwd(dzc, s["proj"], w["conv_w"])
    g["conv_w"] = dcw[:KW]
    dq, delta, dcq = _attn_bwd_dq(s["qkv"], s["ckrow"], s["att"], datt, s["lse"])
    dk, dv, dck = _attn_bwd_dkv(s["qkv"], s["ckcol"], datt, s["lser"], delta)
    dcum = jnp.pad((dcq + dck).T, ((0, 0), (0, LANE - NG)))
    df, dbf = _fox_cum_bwd(dcum, s["proj"], w["bfp"])
    g["fox_bf"] = dbf[0, :NG]
    dproj = jnp.concatenate([dgate, duv, dglu, dq, dk, dv, df], axis=1)
    g["w_in_p"] = _matmul(s["h1"], dproj, ta=True, name="mm_dwin")
    dh1 = _matmul(dproj, w["w_in_p"], tb=True, name="mm_dh1")
    dx, g["mix_pre_g"], dsc1, dsh1 = _norm_bwd(dh1, dx2, s["x"], w["mix_pre_g"], sc1, name="norm_mix_bwd")
    dmod = jnp.concatenate([dsh1, dsc1, dgt1, dsh2, dsc2, dgt2], axis=0)
    return dx, g, dmod


def _position():
    return lax.axis_index("x"), lax.axis_index("y"), lax.axis_index("c")


def _all_gather8(v):
    m_per, n = v.shape

    def body(x_ref, out_ref, send_sems, recv_sems, local_sem):
        x, y, c = _position()
        me, sibling = (x, y, c), (x, y, 1 - c)
        chips = [(1 - x, y), (x, 1 - y), (1 - x, 1 - y)]

        def rows(px, py, pc):
            return out_ref.at[pl.ds((4 * px + 2 * py + pc) * m_per, m_per), :]

        def copy(k, block, to, src=None):
            return pltpu.make_async_remote_copy(
                src_ref=rows(*block) if src is None else src, dst_ref=rows(*block), send_sem=send_sems.at[k],
                recv_sem=recv_sems.at[k], device_id=to, device_id_type=MESH)

        mine = pltpu.make_async_copy(x_ref, rows(*me), local_sem)
        mine.start()
        first = [copy(0, me, sibling, src=x_ref)]
        first += [copy(1 + j, me, (*chip, c), src=x_ref) for j, chip in enumerate(chips)]
        for cp in first:
            cp.start()
        passed = [copy(4 + j, (*chip, c), sibling) for j, chip in enumerate(chips)]
        for j, chip in enumerate(chips):
            copy(1 + j, (*chip, c), me).wait_recv()
            passed[j].start()
        copy(0, sibling, me).wait_recv()
        for j, chip in enumerate(chips):
            copy(4 + j, (*chip, 1 - c), me).wait_recv()
        for cp in first + passed:
            cp.wait_send()
        mine.wait()

    out = pl.pallas_call(
        body, name="all_gather8", out_shape=jax.ShapeDtypeStruct((N_DEV * m_per, n), v.dtype),
        in_specs=[pl.BlockSpec(memory_space=pltpu.VMEM)], out_specs=pl.BlockSpec(memory_space=pltpu.VMEM),
        scratch_shapes=[pltpu.SemaphoreType.DMA((7,)), pltpu.SemaphoreType.DMA((7,)), pltpu.SemaphoreType.DMA],
        compiler_params=pltpu.CompilerParams(vmem_limit_bytes=VMEM_LIMIT),
    )(v)
    return out.reshape(N_DEV, m_per, n)


def _all_gather_chips(shards):
    nw = len(shards)

    def body(*refs):
        w_refs, out_refs = refs[:nw], refs[nw:2 * nw]
        send_sems, recv_sems = refs[2 * nw:]
        x, y, c = _position()
        sibling = (x, y, 1 - c)
        chips = [(1 - x, y), (x, 1 - y), (1 - x, 1 - y)]

        def copy(wi, k, block, to, src=None):
            px, py, layer = block
            dst = out_refs[wi].at[layer, 2 * px + py]
            return pltpu.make_async_remote_copy(
                src_ref=dst if src is None else src, dst_ref=dst, send_sem=send_sems.at[wi * 6 + k],
                recv_sem=recv_sems.at[wi * 6 + k], device_id=to, device_id_type=MESH)

        first = [copy(wi, j, (x, y, c), (*chip, c), src=w_refs[wi].at[c]) for wi in range(nw) for j, chip in enumerate(chips)]
        for cp in first:
            cp.start()
        passed = []
        for wi in range(nw):
            for j, chip in enumerate(chips):
                copy(wi, j, (*chip, c), (x, y, c)).wait_recv()
                cp = copy(wi, 3 + j, (*chip, c), sibling)
                cp.start()
                passed.append(cp)
        for wi in range(nw):
            for j, chip in enumerate(chips):
                copy(wi, 3 + j, (*chip, 1 - c), (x, y, c)).wait_recv()
        for cp in first + passed:
            cp.wait_send()

    return pl.pallas_call(
        body, name="all_gather_chips",
        out_shape=[jax.ShapeDtypeStruct((2, N_CHIPS) + sh.shape[1:], sh.dtype) for sh in shards],
        in_specs=[ANY] * nw, out_specs=[ANY] * nw,
        scratch_shapes=[pltpu.SemaphoreType.DMA((6 * nw,)), pltpu.SemaphoreType.DMA((6 * nw,))],
    )(*shards)


def _swap_layers(g0, g1):
    nw = len(g0)

    def body(*refs):
        g_refs = (refs[:nw], refs[nw:2 * nw])
        out_refs = refs[2 * nw:3 * nw]
        send_sems, recv_sems = refs[3 * nw:]
        x, y, c = _position()
        for layer in range(2):
            @pl.when(c == 1 - layer)
            def _():
                copies = [pltpu.make_async_remote_copy(
                    src_ref=g_refs[layer][wi], dst_ref=out_refs[wi], send_sem=send_sems.at[wi], recv_sem=recv_sems.at[wi],
                    device_id=(x, y, 1 - c), device_id_type=MESH) for wi in range(nw)]
                for cp in copies:
                    cp.start()
                for cp in copies:
                    cp.wait()

    return pl.pallas_call(
        body, name="rs_swap_layers", out_shape=[jax.ShapeDtypeStruct(g.shape, g.dtype) for g in g0],
        in_specs=[ANY] * (2 * nw), out_specs=[ANY] * nw,
        scratch_shapes=[pltpu.SemaphoreType.DMA((nw,)), pltpu.SemaphoreType.DMA((nw,))],
    )(*g0, *g1)


def _scatter_chips(sps):
    nw = len(sps)

    def body(*refs):
        s_refs, out_refs = refs[:nw], refs[nw:2 * nw]
        send_sems, recv_sems = refs[2 * nw:]
        x, y, c = _position()
        chips = [(1 - x, y), (x, 1 - y), (1 - x, 1 - y)]
        copies = [pltpu.make_async_remote_copy(
            src_ref=s_refs[wi].at[2 * px + py], dst_ref=out_refs[wi].at[k], send_sem=send_sems.at[wi * 3 + k],
            recv_sem=recv_sems.at[wi * 3 + k], device_id=(px, py, c), device_id_type=MESH)
            for wi in range(nw) for k, (px, py) in enumerate(chips)]
        for cp in copies:
            cp.start()
        for cp in copies:
            cp.wait()

    return pl.pallas_call(
        body, name="rs_scatter_chips", out_shape=[jax.ShapeDtypeStruct((3,) + sp.shape[1:], sp.dtype) for sp in sps],
        in_specs=[ANY] * nw, out_specs=[ANY] * nw,
        scratch_shapes=[pltpu.SemaphoreType.DMA((3 * nw,)), pltpu.SemaphoreType.DMA((3 * nw,))],
    )(*sps)


def _swap_reduced(reds):
    nw = len(reds)

    def body(*refs):
        r_refs, out_refs = refs[:nw], refs[nw:2 * nw]
        send_sems, recv_sems = refs[2 * nw:]
        x, y, c = _position()
        copies = [pltpu.make_async_remote_copy(
            src_ref=r_refs[wi], dst_ref=out_refs[wi], send_sem=send_sems.at[wi], recv_sem=recv_sems.at[wi],
            device_id=(x, y, 1 - c), device_id_type=MESH) for wi in range(nw)]
        for cp in copies:
            cp.start()
        for cp in copies:
            cp.wait()

    return pl.pallas_call(
        body, name="rs_swap_reduced", out_shape=[jax.ShapeDtypeStruct(r.shape, r.dtype) for r in reds],
        in_specs=[ANY] * nw, out_specs=[ANY] * nw,
        scratch_shapes=[pltpu.SemaphoreType.DMA((nw,)), pltpu.SemaphoreType.DMA((nw,))],
    )(*reds)


def _row_tile(rows, cols):
    tr = rows
    while tr * cols * 4 > (2 << 20) and tr % 32 == 0:
        tr //= 2
    return tr


def _add_own_layer(g0, g1, recv, c, *, name):
    nj, r, w = recv.shape
    tr = _row_tile(r, w)

    def body(c_ref, g0_ref, g1_ref, r_ref, o_ref, ob_ref):
        sm = jnp.where(c_ref[0] == 0, g0_ref[0], g1_ref[0]) + r_ref[0]
        o_ref[0] = sm
        ob_ref[0] = sm.astype(BF16)

    spec = pl.BlockSpec((1, tr, w), lambda j, i, cc: (j, i, 0))
    gs = pltpu.PrefetchScalarGridSpec(
        num_scalar_prefetch=1, grid=(nj, r // tr),
        in_specs=[pl.BlockSpec((1, tr, w), lambda j, i, cc: (j, jnp.where(cc[0] == 0, i, 0), 0)),
                  pl.BlockSpec((1, tr, w), lambda j, i, cc: (j, jnp.where(cc[0] == 1, i, 0), 0)), spec],
        out_specs=[spec, spec])
    return pl.pallas_call(body, name=name, grid_spec=gs,
                          out_shape=[jax.ShapeDtypeStruct((nj, r, w), F32), jax.ShapeDtypeStruct((nj, r, w), BF16)],
                          compiler_params=_cparams(("parallel", "parallel")))(jnp.reshape(c, (1,)).astype(I32), g0, g1, recv)


def _add_own_chip(sp, recv, j, *, name):
    _, r, w = sp.shape
    tr = _row_tile(r, w)

    def body(j_ref, s_ref, r_ref, o_ref):
        o_ref[...] = ((s_ref[0] + r_ref[0].astype(F32)) + r_ref[1].astype(F32)) + r_ref[2].astype(F32)

    gs = pltpu.PrefetchScalarGridSpec(
        num_scalar_prefetch=1, grid=(r // tr,),
        in_specs=[pl.BlockSpec((1, tr, w), lambda i, jj: (jj[0], i, 0)), pl.BlockSpec((3, tr, w), lambda i, jj: (0, i, 0))],
        out_specs=pl.BlockSpec((tr, w), lambda i, jj: (i, 0)))
    return pl.pallas_call(body, name=name, grid_spec=gs, out_shape=jax.ShapeDtypeStruct((r, w), F32),
                          compiler_params=_cparams(("parallel",)))(jnp.reshape(j, (1,)).astype(I32), sp, recv)


def _sum8(v):
    _, m, n = v.shape

    def body(v_ref, o_ref):
        acc = v_ref[0]
        for k in range(1, N_DEV):
            acc = acc + v_ref[k]
        o_ref[...] = acc

    return pl.pallas_call(body, name="sum8", grid=(1,), in_specs=[pl.BlockSpec((N_DEV, m, n), lambda i: (0, 0, 0))],
                          out_specs=pl.BlockSpec((m, n), lambda i: (0, 0)), out_shape=jax.ShapeDtypeStruct((m, n), F32),
                          compiler_params=_cparams(("arbitrary",)))(v)


def _ada_mod(c_all, ada_w, ada_b_loc, *, tn=512):
    nl, _, ncol = ada_w.shape

    def body(c_ref, w_ref, b_ref, o_ref):
        cv = c_ref[...]
        ca = (cv * _sigmoid(cv)).astype(BF16)
        o_ref[0] = _dot(ca, w_ref[0].astype(BF16)) + b_ref[0]

    return pl.pallas_call(
        body, name="ada_mod", grid=(nl, ncol // tn),
        in_specs=[pl.BlockSpec((N_DEV, D), lambda l, j: (0, 0)), pl.BlockSpec((1, D, tn), lambda l, j: (l, 0, j)),
                  pl.BlockSpec((1, 1, tn), lambda l, j: (l, 0, j))],
        out_specs=pl.BlockSpec((1, N_DEV, tn), lambda l, j: (l, 0, j)),
        out_shape=jax.ShapeDtypeStruct((nl, N_DEV, ncol), F32),
        compiler_params=_cparams(("parallel", "parallel")))(c_all, ada_w, ada_b_loc)


def _ada_grad(c_pad, dmod_pad, *, tn=512):
    nl, nb, ncol = dmod_pad.shape

    def body(c_ref, d_ref, o_ref):
        cv = c_ref[...]
        ca = (cv * _sigmoid(cv)).astype(BF16)
        o_ref[0] = _dot(ca, d_ref[0].astype(BF16), TN)

    return pl.pallas_call(
        body, name="ada_grad", grid=(nl, ncol // tn),
        in_specs=[pl.BlockSpec((nb, D), lambda l, j: (0, 0)), pl.BlockSpec((1, nb, tn), lambda l, j: (l, 0, j))],
        out_specs=pl.BlockSpec((1, D, tn), lambda l, j: (l, 0, j)),
        out_shape=jax.ShapeDtypeStruct((nl, D, ncol), F32),
        compiler_params=_cparams(("parallel", "parallel")))(c_pad, dmod_pad)


def _adamw(w, g, m, v, *, name):
    shape = w.shape
    if w.ndim == 3:
        lead, rows, cols = shape
    else:
        lead, (rows, cols) = 1, shape
    w3, g3, m3, v3 = (a.reshape(lead, rows, cols) for a in (w, g, m, v))
    tr = rows
    if rows * cols * 4 > (2 << 20):
        tr = next(cand for cand in (256, 128, 64, 8) if rows % cand == 0)
    c1 = 1.0 - ADAM_B1 ** ADAM_STEP
    c2 = 1.0 - ADAM_B2 ** ADAM_STEP

    def body(w_ref, g_ref, m_ref, v_ref, d_ref, nm_ref, nv_ref):
        gv = g_ref[...]
        nm = ADAM_B1 * m_ref[...] + (1.0 - ADAM_B1) * gv
        nv = ADAM_B2 * v_ref[...] + (1.0 - ADAM_B2) * (gv * gv)
        nm_ref[...] = nm
        nv_ref[...] = nv
        d_ref[...] = -ADAM_LR * ((nm / c1) / (jnp.sqrt(nv / c2) + ADAM_EPS) + ADAM_WD * w_ref[...])

    spec = pl.BlockSpec((1, tr, cols), lambda l, i: (l, i, 0))
    outs = pl.pallas_call(
        body, name=name, grid=(lead, rows // tr), in_specs=[spec] * 4, out_specs=[spec] * 3,
        out_shape=[jax.ShapeDtypeStruct((lead, rows, cols), F32)] * 3,
        compiler_params=_cparams(("parallel", "parallel")))(w3, g3, m3, v3)
    return tuple(o.reshape(shape) for o in outs)


def _adamw_big(w, g_own, g_other, m, v, c, *, name):
    _, rows, cols = w.shape
    tr = _row_tile(rows, cols)
    c1 = 1.0 - ADAM_B1 ** ADAM_STEP
    c2 = 1.0 - ADAM_B2 ** ADAM_STEP

    def body(c_ref, w_ref, go_ref, gx_ref, m_ref, v_ref, g_ref, d_ref, nm_ref, nv_ref):
        gv = jnp.where(pl.program_id(0) == c_ref[0], go_ref[...], gx_ref[...])
        nm = ADAM_B1 * m_ref[0] + (1.0 - ADAM_B1) * gv
        nv = ADAM_B2 * v_ref[0] + (1.0 - ADAM_B2) * (gv * gv)
        g_ref[0] = gv
        nm_ref[0] = nm
        nv_ref[0] = nv
        d_ref[0] = -ADAM_LR * ((nm / c1) / (jnp.sqrt(nv / c2) + ADAM_EPS) + ADAM_WD * w_ref[0])

    spec = pl.BlockSpec((1, tr, cols), lambda l, i, cc: (l, i, 0))
    gs = pltpu.PrefetchScalarGridSpec(
        num_scalar_prefetch=1, grid=(2, rows // tr),
        in_specs=[spec, pl.BlockSpec((tr, cols), lambda l, i, cc: (jnp.where(l == cc[0], i, 0), 0)),
                  pl.BlockSpec((tr, cols), lambda l, i, cc: (jnp.where(l == cc[0], 0, i), 0)), spec, spec],
        out_specs=[spec] * 4)
    return pl.pallas_call(
        body, name=name, grid_spec=gs, out_shape=[jax.ShapeDtypeStruct(w.shape, F32)] * 4,
        compiler_params=_cparams(("parallel", "parallel")))(jnp.reshape(c, (1,)).astype(I32), w, g_own, g_other, m, v)


SMALL = (("mix_pre_g", (2, D)), ("mix_post_g", (2, D)), ("mlp_pre_g", (2, D)), ("mlp_post_g", (2, D)),
         ("gmlp_ln_g", (2, DM)), ("gmlp_ln_b", (2, DM)), ("gmlp_ws", (2, NG, CH, CH)), ("gmlp_bs", (2, NG, CH)),
         ("conv_b", (2, DM)), ("conv_ln_g", (2, DM)), ("conv_ln_b", (2, DM)), ("fox_bf", (2, NG)))


def _pack_rows(arrays, mult=8):
    flat = []
    for a in arrays:
        f = a.reshape(-1).astype(F32)
        pad = (-f.shape[0]) % LANE
        flat.append(jnp.pad(f, (0, pad)) if pad else f)
    cat = jnp.concatenate(flat)
    rows = cat.shape[0] // LANE
    pad_rows = (-rows) % mult
    if pad_rows:
        cat = jnp.pad(cat, (0, pad_rows * LANE))
    return cat.reshape(-1, LANE)


def _unpack_rows(buf, shapes):
    flat = buf.reshape(-1)
    out, off = [], 0
    for shp in shapes:
        size = 1
        for d in shp:
            size *= d
        out.append(flat[off:off + size].reshape(shp))
        off += size + ((-size) % LANE)
    return out


def _assemble_w_in(w_in_full):
    uv_glu_qkv = w_in_full[:, :3584]
    f = w_in_full[:, 3584:3592]
    gate = w_in_full[:, 3592:]
    fpad = jnp.zeros((D, D_INP - C_F - NG), w_in_full.dtype)
    return jnp.concatenate([gate, uv_glu_qkv, f, fpad], axis=1)


def _disassemble_w_in(g_p):
    return jnp.concatenate([g_p[:, C_UV:C_F], g_p[:, C_F:C_F + NG], g_p[:, :C_UV]], axis=1)


def kernel(x, c, ada_w, ada_b, mix_pre_g, mix_post_g, mlp_pre_g, mlp_post_g, w_in, gmlp_ln_g, gmlp_ln_b, gmlp_ws, gmlp_bs, w_a_out, conv_w, conv_b, conv_ln_g, conv_ln_b, w_b_out, fox_bf, w_c_out, w_out, mlp_w1, mlp_w2, loss_target, m_ada_w, m_ada_b, m_mix_pre_g, m_mix_post_g, m_mlp_pre_g, m_mlp_post_g, m_w_in, m_gmlp_ln_g, m_gmlp_ln_b, m_gmlp_ws, m_gmlp_bs, m_w_a_out, m_conv_w, m_conv_b, m_conv_ln_g, m_conv_ln_b, m_w_b_out, m_fox_bf, m_w_c_out, m_w_out, m_mlp_w1, m_mlp_w2, v_ada_w, v_ada_b, v_mix_pre_g, v_mix_post_g, v_mlp_pre_g, v_mlp_post_g, v_w_in, v_gmlp_ln_g, v_gmlp_ln_b, v_gmlp_ws, v_gmlp_bs, v_w_a_out, v_conv_w, v_conv_b, v_conv_ln_g, v_conv_ln_b, v_w_b_out, v_fox_bf, v_w_c_out, v_w_out, v_mlp_w1, v_mlp_w2):
    weights = dict(ada_w=ada_w, ada_b=ada_b, mix_pre_g=mix_pre_g, mix_post_g=mix_post_g, mlp_pre_g=mlp_pre_g,
                   mlp_post_g=mlp_post_g, w_in=w_in, gmlp_ln_g=gmlp_ln_g, gmlp_ln_b=gmlp_ln_b, gmlp_ws=gmlp_ws,
                   gmlp_bs=gmlp_bs, w_a_out=w_a_out, conv_w=conv_w, conv_b=conv_b, conv_ln_g=conv_ln_g,
                   conv_ln_b=conv_ln_b, w_b_out=w_b_out, fox_bf=fox_bf, w_c_out=w_c_out, w_out=w_out, mlp_w1=mlp_w1,
                   mlp_w2=mlp_w2)
    mom_m = dict(ada_w=m_ada_w, ada_b=m_ada_b, mix_pre_g=m_mix_pre_g, mix_post_g=m_mix_post_g, mlp_pre_g=m_mlp_pre_g,
                 mlp_post_g=m_mlp_post_g, w_in=m_w_in, gmlp_ln_g=m_gmlp_ln_g, gmlp_ln_b=m_gmlp_ln_b, gmlp_ws=m_gmlp_ws,
                 gmlp_bs=m_gmlp_bs, w_a_out=m_w_a_out, conv_w=m_conv_w, conv_b=m_conv_b, conv_ln_g=m_conv_ln_g,
                 conv_ln_b=m_conv_ln_b, w_b_out=m_w_b_out, fox_bf=m_fox_bf, w_c_out=m_w_c_out, w_out=m_w_out,
                 mlp_w1=m_mlp_w1, mlp_w2=m_mlp_w2)
    mom_v = dict(ada_w=v_ada_w, ada_b=v_ada_b, mix_pre_g=v_mix_pre_g, mix_post_g=v_mix_post_g, mlp_pre_g=v_mlp_pre_g,
                 mlp_post_g=v_mlp_post_g, w_in=v_w_in, gmlp_ln_g=v_gmlp_ln_g, gmlp_ln_b=v_gmlp_ln_b, gmlp_ws=v_gmlp_ws,
                 gmlp_bs=v_gmlp_bs, w_a_out=v_w_a_out, conv_w=v_conv_w, conv_b=v_conv_b, conv_ln_g=v_conv_ln_g,
                 conv_ln_b=v_conv_ln_b, w_b_out=v_w_b_out, fox_bf=v_fox_bf, w_c_out=v_w_c_out, w_out=v_w_out,
                 mlp_w1=v_mlp_w1, mlp_w2=v_mlp_w2)
    order = list(weights)
    px, py, pc = _position()
    chip = 2 * px + py
    dev = 2 * chip + pc
    depth = ada_w.shape[0]
    t = x.shape[1]
    xl = x.reshape(t, D)
    tgt = loss_target.reshape(t, D)

    small_in = _pack_rows([c, conv_w])
    gathered = _all_gather8(small_in)
    c_all = gathered[:, :D // LANE, :].reshape(N_DEV, D)
    cw_rows = depth * KW * LANE // LANE
    conv_w_full = jnp.concatenate(
        [gathered[2 * j, D // LANE:D // LANE + cw_rows, :].reshape(depth, KW, LANE) for j in range(N_CHIPS)], axis=2)

    ncol = ada_w.shape[2]
    ada_b_loc = lax.dynamic_slice_in_dim(ada_b, chip * ncol, ncol, axis=1).reshape(depth, 1, ncol)
    mod_sh = _ada_mod(c_all, ada_w, ada_b_loc)
    mod_g = _all_gather8(mod_sh.reshape(-1, LANE)).reshape(N_DEV, depth, N_DEV, ncol)
    mod_all = jnp.concatenate([mod_g[2 * j] for j in range(N_CHIPS)], axis=2)
    mod_mine = lax.dynamic_index_in_dim(mod_all, dev, axis=1, keepdims=False)

    big_names = [b[0] for b in BIG]
    shards = [weights[n].astype(BF16) for n in big_names]
    gath = [lax.dynamic_update_slice(g, sh[:, None], (0, chip, 0, 0))
            for g, sh in zip(_all_gather_chips(shards), shards)]
    layers = []
    for l in range(depth):
        w = {}
        for (n, r, cdim, ax), g in zip(BIG, gath):
            if n == "mlp_w1":
                w[n] = g[l]
            elif ax == 1:
                w[n] = g[l].transpose(1, 0, 2).reshape(r, N_CHIPS * cdim)
            else:
                w[n] = g[l].reshape(N_CHIPS * r, cdim)
        w["w_in_p"] = _assemble_w_in(w.pop("w_in"))
        for n in ("mix_pre_g", "mix_post_g", "mlp_pre_g", "mlp_post_g", "gmlp_ln_g", "gmlp_ln_b", "conv_b", "conv_ln_g",
                  "conv_ln_b"):
            w[n] = weights[n][l:l + 1]
        tril = jnp.tril(jnp.ones((CH, CH), F32))
        wsm = gmlp_ws[l] * tril
        w["wsm"] = wsm.astype(BF16)
        w["wsmt"] = jnp.swapaxes(wsm, 1, 2).astype(BF16)
        w["bsx"] = jnp.repeat(gmlp_bs[l].T, HD, axis=1)
        w["conv_w"] = conv_w_full[l]
        w["bfp"] = jnp.pad(fox_bf[l], (0, LANE - NG)).reshape(1, LANE)
        layers.append(w)

    xs, saved = xl, []
    for l in range(depth):
        xs, s = _layer_fwd(xs, mod_mine[l].reshape(NMOD, D), layers[l])
        saved.append(s)
    loss_local, dx = _loss_and_grad(xs, tgt)
    loss = lax.psum(loss_local, ("x", "y", "c"))
    grads, dmods = [None] * depth, [None] * depth
    for l in reversed(range(depth)):
        dx, grads[l], dmods[l] = _layer_bwd(dx, mod_mine[l].reshape(NMOD, D), layers[l], saved[l])
    grad_x = dx.reshape(x.shape)

    def slabs(gfull, n, r, cdim, ax):
        if n == "mlp_w1":
            return gfull
        if ax == 1:
            return gfull.reshape(gfull.shape[0], N_CHIPS, cdim).transpose(1, 0, 2)
        return gfull.reshape(N_CHIPS, r, cdim)

    for l in range(depth):
        grads[l]["w_in"] = _disassemble_w_in(grads[l].pop("w_in_p"))
    gsl = [[slabs(grads[l][n], n, r, cdim, ax) for n, r, cdim, ax in BIG] for l in range(depth)]
    from_sibling = _swap_layers(gsl[0], gsl[1])
    sums = [_add_own_layer(g0, g1, rv, pc, name="rs_add_layer_" + n)
            for g0, g1, rv, n in zip(gsl[0], gsl[1], from_sibling, big_names)]
    from_chips = _scatter_chips([sb for _, sb in sums])
    reduced = [_add_own_chip(sf, rv, chip, name="rs_add_chip_" + n)
               for (sf, _), rv, n in zip(sums, from_chips, big_names)]
    from_sib = _swap_reduced(reduced)
    g_out = {}

    small_names = [n for n, _ in SMALL]
    small_list = [jnp.stack(dmods)] + [jnp.stack([grads[l][n] for l in range(depth)]) for n in small_names]
    small_list.append(jnp.stack([grads[l]["conv_w"] for l in range(depth)]))
    small_shapes = [(depth, NMOD * D)] + [shp for _, shp in SMALL] + [(depth, KW, DM)]
    small_pack = _pack_rows(small_list)
    small_all = _all_gather8(small_pack)
    small_sum = _unpack_rows(_sum8(small_all), small_shapes)
    g_out["ada_b"] = small_sum[0]
    for n, gs in zip(small_names, small_sum[1:-1]):
        g_out[n] = gs
    g_out["conv_w"] = lax.dynamic_slice_in_dim(small_sum[-1], chip * LANE, LANE, axis=2)
    dmod_all = small_all[:, :depth * NMOD * D // LANE, :].reshape(N_DEV, depth, NMOD * D)
    dmod_loc = lax.dynamic_slice_in_dim(dmod_all, chip * ncol, ncol, axis=2).transpose(1, 0, 2)
    g_out["ada_w"] = _ada_grad(jnp.pad(c_all, ((0, 8), (0, 0))), jnp.pad(dmod_loc, ((0, 0), (0, 8), (0, 0))))

    delta, new_m, new_v = {}, {}, {}
    delta["ada_w"], new_m["ada_w"], new_v["ada_w"] = _adamw(ada_w, g_out["ada_w"], m_ada_w, v_ada_w, name="adamw_ada_w")
    for n, own, oth in zip(big_names, reduced, from_sib):
        g_out[n], delta[n], new_m[n], new_v[n] = _adamw_big(weights[n], own, oth, mom_m[n], mom_v[n], pc, name="adamw_" + n)
    small_params = ["ada_b"] + small_names + ["conv_w"]
    packs = [_pack_rows([d[n] for n in small_params]) for d in (weights, g_out, mom_m, mom_v)]
    outs = _adamw(*packs, name="adamw_small")
    shapes = [weights[n].shape for n in small_params]
    for dst, buf in zip((delta, new_m, new_v), outs):
        for n, a in zip(small_params, _unpack_rows(buf, shapes)):
            dst[n] = a

    return (loss, grad_x, *[g_out[n] for n in order], *[delta[n] for n in order], *[new_m[n] for n in order],
            *[new_v[n] for n in order])
```

```python
import functools

import jax
import jax.numpy as jnp
from jax import lax
from jax.experimental import pallas as pl
from jax.experimental.pallas import tpu as pltpu

F32 = jnp.float32
BF16 = jnp.bfloat16
I32 = jnp.int32
MESH = pl.DeviceIdType.MESH
ANY = pl.BlockSpec(memory_space=pl.ANY)

D = 1024
DM = 512
NG = 8
CH = 128
KW = 31
HALO = 32
DFF = 4096
NMOD = 6
EPS = 1e-6
LANE = 128
N_CHIPS = 4
N_DEV = 8
C_GATE, C_UV, C_GLU, C_Q, C_K, C_V, C_F, D_INP = 0, 3072, 4096, 5120, 5632, 6144, 6656, 7168
D_IN = 6664
VMEM_LIMIT = 56 * 1024 * 1024

ADAM_LR, ADAM_B1, ADAM_B2, ADAM_EPS, ADAM_WD, ADAM_STEP = 0.001, 0.9, 0.999, 1e-08, 0.01, 10

BIG = (("w_in", 1024, 1666, 1), ("w_a_out", 512, 256, 1), ("w_b_out", 512, 256, 1), ("w_c_out", 512, 256, 1),
       ("w_out", 256, 1024, 0), ("mlp_w1", 1024, 1024, 1), ("mlp_w2", 1024, 1024, 0))


def _cparams(sem):
    return pltpu.CompilerParams(dimension_semantics=sem, vmem_limit_bytes=VMEM_LIMIT)


def _sigmoid(x):
    return jax.nn.sigmoid(x)


_GELU_K = 0.7978845608028654
_GELU_A = 0.044715


def _gelu(x):
    t = jnp.tanh(_GELU_K * (x + _GELU_A * x * x * x))
    return 0.5 * x * (1.0 + t)


def _gelu_grad(x):
    t = jnp.tanh(_GELU_K * (x + _GELU_A * x * x * x))
    return 0.5 * (1.0 + t) + 0.5 * x * (1.0 - t * t) * _GELU_K * (1.0 + 3.0 * _GELU_A * x * x)


def _mean(x):
    return jnp.mean(x, axis=-1, keepdims=True)


def _colsum(x):
    return jnp.sum(x, axis=0, keepdims=True)


def _dot(a, b, dims=((1,), (0,))):
    return lax.dot_general(a, b, (dims, ((), ())), preferred_element_type=F32)


NN = ((1,), (0,))
NT = ((1,), (1,))
TN = ((0,), (0,))


def _matmul(a, b, *, name, ta=False, tb=False, out_dtype=F32, tm=1024, tn=1024, tk=1024, epilogue=None, extra=(),
            extra_out=(), b_slabs=False, out_slabs=0):
    m, k = (a.shape[1], a.shape[0]) if ta else a.shape
    if b_slabs:
        ns, brows, bw = b.shape
        n = brows if tb else ns * bw
        assert (ns * bw if tb else brows) == k, (name, b.shape, k)
        tn, tk = (tn, bw) if tb else (bw, tk)
    else:
        n = b.shape[0] if tb else b.shape[1]
    tm, tn, tk = min(tm, m), min(tn, n), min(tk, k)
    assert m % tm == 0 and n % tn == 0 and k % tk == 0, (name, m, n, k, tm, tn, tk)
    assert not out_slabs or (n // out_slabs == tn and epilogue is None), name
    nk = k // tk
    dims = ((0 if ta else 1,), (1 if tb else 0,))
    n_extra = len(extra)
    out_dtypes = (out_dtype,) + tuple(extra_out)

    def body(a_ref, b_ref, *rest):
        extra_refs = rest[:n_extra]
        out_refs = rest[n_extra:n_extra + len(out_dtypes)]
        kk = pl.program_id(2)
        part = _dot(a_ref[...].astype(BF16), b_ref[...].astype(BF16), dims)

        def finish(acc):
            outs = (acc,) if epilogue is None else epilogue(acc, *[r[...] for r in extra_refs])
            for o_ref, o in zip(out_refs, outs):
                o_ref[...] = o.astype(o_ref.dtype)

        if nk == 1:
            finish(part)
        else:
            acc_ref = rest[-1]

            @pl.when(kk == 0)
            def _():
                acc_ref[...] = part

            @pl.when(jnp.logical_and(kk > 0, kk < nk - 1))
            def _():
                acc_ref[...] += part

            @pl.when(kk == nk - 1)
            def _():
                finish(acc_ref[...] + part)

    a_spec = pl.BlockSpec((tk, tm), lambda i, j, kk: (kk, i)) if ta else pl.BlockSpec((tm, tk), lambda i, j, kk: (i, kk))
    if b_slabs and tb:
        b_spec = pl.BlockSpec((None, tn, tk), lambda i, j, kk: (kk, j, 0))
    elif b_slabs:
        b_spec = pl.BlockSpec((None, tk, tn), lambda i, j, kk: (j, kk, 0))
    else:
        b_spec = pl.BlockSpec((tn, tk), lambda i, j, kk: (j, kk)) if tb else pl.BlockSpec((tk, tn), lambda i, j, kk: (kk, j))
    if out_slabs:
        o_spec = pl.BlockSpec((None, tm, tn), lambda i, j, kk: (j, i, 0))
        o_shape = (out_slabs, m, tn)
    else:
        o_spec = pl.BlockSpec((tm, tn), lambda i, j, kk: (i, j))
        o_shape = (m, n)
    outs = pl.pallas_call(
        body, name=name, grid=(m // tm, n // tn, nk),
        in_specs=[a_spec, b_spec] + [o_spec] * n_extra,
        out_specs=[o_spec] * len(out_dtypes),
        out_shape=[jax.ShapeDtypeStruct(o_shape, dt) for dt in out_dtypes],
        scratch_shapes=[pltpu.VMEM((tm, tn), F32)] if nk > 1 else [],
        compiler_params=_cparams(("parallel", "parallel", "arbitrary")),
    )(a, b, *extra)
    return outs[0] if len(outs) == 1 else outs


def _rows(tm, n, col=0):
    return pl.BlockSpec((tm, n), lambda i: (i, col))


def _vec(n):
    return pl.BlockSpec((1, n), lambda i: (0, 0))


def _norm_mod(x, g, sc, sh, *, name, tm=256):
    t = x.shape[0]
    tm = min(tm, t)

    def body(x_ref, g_ref, sc_ref, sh_ref, h_ref):
        xv = x_ref[...]
        inv = lax.rsqrt(_mean(xv * xv) + EPS)
        h_ref[...] = ((xv * inv * g_ref[...]) * (1.0 + sc_ref[...]) + sh_ref[...]).astype(BF16)

    return pl.pallas_call(
        body, name=name, grid=(t // tm,), in_specs=[_rows(tm, D), _vec(D), _vec(D), _vec(D)],
        out_specs=_rows(tm, D), out_shape=jax.ShapeDtypeStruct((t, D), BF16),
        compiler_params=_cparams(("parallel",)))(x, g, sc, sh)


def _resid(x, y, gt, gp, *, name, tm=256):
    t = x.shape[0]
    tm = min(tm, t)

    def body(x_ref, y_ref, gt_ref, gp_ref, o_ref):
        yv = y_ref[...]
        inv = lax.rsqrt(_mean(yv * yv) + EPS)
        o_ref[...] = x_ref[...] + gt_ref[...] * (yv * inv * gp_ref[...])

    return pl.pallas_call(
        body, name=name, grid=(t // tm,), in_specs=[_rows(tm, D), _rows(tm, D), _vec(D), _vec(D)],
        out_specs=_rows(tm, D), out_shape=jax.ShapeDtypeStruct((t, D), F32),
        compiler_params=_cparams(("parallel",)))(x, y, gt, gp)


def _resid_bwd(dx, y, gt, gp, *, name, tm=256):
    t = dx.shape[0]
    tm = min(tm, t)

    def body(dx_ref, y_ref, gt_ref, gp_ref, dy_ref, dgt_ref, dgp_ref):
        @pl.when(pl.program_id(0) == 0)
        def _():
            dgt_ref[...] = jnp.zeros_like(dgt_ref)
            dgp_ref[...] = jnp.zeros_like(dgp_ref)

        dxv, yv, gp_v = dx_ref[...], y_ref[...], gp_ref[...]
        inv = lax.rsqrt(_mean(yv * yv) + EPS)
        yh = yv * inv
        dgt_ref[...] += _colsum(dxv * (yh * gp_v))
        dr = dxv * gt_ref[...]
        dgp_ref[...] += _colsum(dr * yh)
        dyn = dr * gp_v
        dy_ref[...] = (inv * (dyn - yh * _mean(dyn * yh))).astype(BF16)

    return pl.pallas_call(
        body, name=name, grid=(t // tm,), in_specs=[_rows(tm, D), _rows(tm, D), _vec(D), _vec(D)],
        out_specs=[_rows(tm, D), _vec(D), _vec(D)],
        out_shape=[jax.ShapeDtypeStruct((t, D), BF16), jax.ShapeDtypeStruct((1, D), F32),
                   jax.ShapeDtypeStruct((1, D), F32)],
        compiler_params=_cparams(("arbitrary",)))(dx, y, gt, gp)


def _norm_bwd(dh, dx_res, x, g, sc, *, name, tm=256):
    t = dh.shape[0]
    tm = min(tm, t)

    def body(dh_ref, dxr_ref, x_ref, g_ref, sc_ref, dx_ref, dg_ref, dsc_ref, dsh_ref):
        @pl.when(pl.program_id(0) == 0)
        def _():
            dg_ref[...] = jnp.zeros_like(dg_ref)
            dsc_ref[...] = jnp.zeros_like(dsc_ref)
            dsh_ref[...] = jnp.zeros_like(dsh_ref)

        dhv, xv, gv = dh_ref[...], x_ref[...], g_ref[...]
        inv = lax.rsqrt(_mean(xv * xv) + EPS)
        xh = xv * inv
        dsh_ref[...] += _colsum(dhv)
        dsc_ref[...] += _colsum(dhv * (xh * gv))
        dn = dhv * (1.0 + sc_ref[...])
        dg_ref[...] += _colsum(dn * xh)
        dxh = dn * gv
        dx_ref[...] = inv * (dxh - xh * _mean(dxh * xh)) + dxr_ref[...]

    vec_out = jax.ShapeDtypeStruct((1, D), F32)
    return pl.pallas_call(
        body, name=name, grid=(t // tm,), in_specs=[_rows(tm, D), _rows(tm, D), _rows(tm, D), _vec(D), _vec(D)],
        out_specs=[_rows(tm, D), _vec(D), _vec(D), _vec(D)],
        out_shape=[jax.ShapeDtypeStruct((t, D), F32), vec_out, vec_out, vec_out],
        compiler_params=_cparams(("arbitrary",)))(dh, dx_res, x, g, sc)


def _loss_and_grad(x, target, *, tm=256):
    t = x.shape[0]
    tm = min(tm, t)

    def body(x_ref, t_ref, loss_ref, dx_ref):
        @pl.when(pl.program_id(0) == 0)
        def _():
            loss_ref[...] = jnp.zeros_like(loss_ref)

        e = x_ref[...] - t_ref[...]
        dx_ref[...] = e * (1.0 / D)
        s = jnp.sum(jnp.sum(e * e, axis=1, keepdims=True), axis=0, keepdims=True) * (0.5 / D)
        loss_ref[...] += jnp.broadcast_to(s, loss_ref.shape)

    loss, dx = pl.pallas_call(
        body, name="loss", grid=(t // tm,), in_specs=[_rows(tm, D), _rows(tm, D)],
        out_specs=[pl.BlockSpec((8, LANE), lambda i: (0, 0)), _rows(tm, D)],
        out_shape=[jax.ShapeDtypeStruct((8, LANE), F32), jax.ShapeDtypeStruct((t, D), F32)],
        compiler_params=_cparams(("arbitrary",)))(x, target)
    return loss[0, 0], dx


def _gmlp_core(uv, lng, lnb, ws_ref, bsx):
    tm = uv.shape[0]
    gu = _gelu(uv[:, :DM])
    gv = _gelu(uv[:, DM:])
    mu = _mean(gv)
    vc = gv - mu
    rstd = lax.rsqrt(_mean(vc * vc) + EPS)
    vh = vc * rstd
    vln = vh * lng + lnb
    lane = lax.broadcasted_iota(I32, (CH, LANE), 1)
    sv_rows = []
    for nchunk in range(tm // CH):
        vb = vln[nchunk * CH:(nchunk + 1) * CH].astype(BF16)
        cols = []
        for cb in range(DM // LANE):
            vcb = vb[:, cb * LANE:(cb + 1) * LANE]
            lo = _dot(ws_ref[2 * cb], vcb)
            hi = _dot(ws_ref[2 * cb + 1], vcb)
            cols.append(jnp.where(lane < 64, lo, hi))
        sv_rows.append(jnp.concatenate(cols, axis=1) + bsx)
    sv = jnp.concatenate(sv_rows, axis=0) if len(sv_rows) > 1 else sv_rows[0]
    return gu, vh, rstd, vln, sv


def _gmlp_fwd(proj, lng, lnb, wsm, bsx, *, tm=256):
    t = proj.shape[0]
    tm = min(tm, t)

    def body(uv_ref, lng_ref, lnb_ref, ws_ref, bs_ref, ga_ref):
        gu, _, _, _, sv = _gmlp_core(uv_ref[...], lng_ref[...], lnb_ref[...], ws_ref, bs_ref[...])
        ga_ref[...] = (gu * sv).astype(BF16)

    return pl.pallas_call(
        body, name="gmlp_fwd", grid=(t // tm,),
        in_specs=[_rows(tm, 2 * DM, C_UV // (2 * DM)), _vec(DM), _vec(DM),
                  pl.BlockSpec((NG, CH, CH), lambda i: (0, 0, 0)), pl.BlockSpec((CH, DM), lambda i: (0, 0))],
        out_specs=_rows(tm, DM), out_shape=jax.ShapeDtypeStruct((t, DM), BF16),
        compiler_params=_cparams(("parallel",)))(proj, lng, lnb, wsm, bsx)


def _gmlp_bwd(dga, proj, lng, lnb, wsm, wsmt, bsx, *, tm=256):
    t = proj.shape[0]
    tm = min(tm, t)

    def body(dga_ref, uv_ref, lng_ref, lnb_ref, ws_ref, wst_ref, bs_ref, duv_ref, dws_ref, dbs_ref, dlng_ref, dlnb_ref,
             dbsx_ref):
        i = pl.program_id(0)

        @pl.when(i == 0)
        def _():
            dws_ref[...] = jnp.zeros_like(dws_ref)
            dbsx_ref[...] = jnp.zeros_like(dbsx_ref)
            dlng_ref[...] = jnp.zeros_like(dlng_ref)
            dlnb_ref[...] = jnp.zeros_like(dlnb_ref)

        uv = uv_ref[...]
        lng_v = lng_ref[...]
        gu, vh, rstd, vln, sv = _gmlp_core(uv, lng_v, lnb_ref[...], ws_ref, bs_ref[...])
        dga_v = dga_ref[...]
        dgu = dga_v * sv
        dsv = dga_v * gu
        lane = lax.broadcasted_iota(I32, (CH, LANE), 1)
        tril = lax.broadcasted_iota(I32, (CH, CH), 0) >= lax.broadcasted_iota(I32, (CH, CH), 1)
        dvln_rows = []
        for nchunk in range(tm // CH):
            rows = slice(nchunk * CH, (nchunk + 1) * CH)
            dbsx_ref[...] += dsv[rows]
            vb = vln[rows].astype(BF16)
            cols = []
            for cb in range(DM // LANE):
                cs = slice(cb * LANE, (cb + 1) * LANE)
                dsvb = dsv[rows, cs]
                vcb = vb[:, cs]
                dlo = jnp.where(lane < 64, dsvb, 0.0).astype(BF16)
                dhi = jnp.where(lane < 64, 0.0, dsvb).astype(BF16)
                dws_ref[2 * cb] += jnp.where(tril, _dot(dlo, vcb, NT), 0.0)
                dws_ref[2 * cb + 1] += jnp.where(tril, _dot(dhi, vcb, NT), 0.0)
                dsb = dsvb.astype(BF16)
                cols.append(jnp.where(lane < 64, _dot(wst_ref[2 * cb], dsb), _dot(wst_ref[2 * cb + 1], dsb)))
            dvln_rows.append(jnp.concatenate(cols, axis=1))
        dvln = jnp.concatenate(dvln_rows, axis=0) if len(dvln_rows) > 1 else dvln_rows[0]
        dlnb_ref[...] += _colsum(dvln)
        dlng_ref[...] += _colsum(dvln * vh)
        dvh = dvln * lng_v
        dgv = rstd * (dvh - _mean(dvh) - vh * _mean(dvh * vh))
        duv_ref[:, :DM] = (dgu * _gelu_grad(uv[:, :DM])).astype(BF16)
        duv_ref[:, DM:] = (dgv * _gelu_grad(uv[:, DM:])).astype(BF16)

        @pl.when(i == pl.num_programs(0) - 1)
        def _():
            ind = (lax.broadcasted_iota(I32, (DM, LANE), 0) // 64 == lax.broadcasted_iota(I32, (DM, LANE), 1)).astype(F32)
            dbs_ref[...] = jnp.dot(dbsx_ref[...], ind, preferred_element_type=F32, precision=lax.Precision.HIGHEST)

    vec_out = jax.ShapeDtypeStruct((1, DM), F32)
    outs = pl.pallas_call(
        body, name="gmlp_bwd", grid=(t // tm,),
        in_specs=[_rows(tm, DM), _rows(tm, 2 * DM, C_UV // (2 * DM)), _vec(DM), _vec(DM),
                  pl.BlockSpec((NG, CH, CH), lambda i: (0, 0, 0)), pl.BlockSpec((NG, CH, CH), lambda i: (0, 0, 0)),
                  pl.BlockSpec((CH, DM), lambda i: (0, 0))],
        out_specs=[_rows(tm, 2 * DM), pl.BlockSpec((NG, CH, CH), lambda i: (0, 0, 0)),
                   pl.BlockSpec((CH, LANE), lambda i: (0, 0)), _vec(DM), _vec(DM)],
        out_shape=[jax.ShapeDtypeStruct((t, 2 * DM), BF16), jax.ShapeDtypeStruct((NG, CH, CH), F32),
                   jax.ShapeDtypeStruct((CH, LANE), F32), vec_out, vec_out],
        scratch_shapes=[pltpu.VMEM((CH, DM), F32)],
        compiler_params=_cparams(("arbitrary",)))(dga, proj, lng, lnb, wsm, wsmt, bsx)
    return outs


def _glu_into(zs_ref, glu_ref, halo_ref, first):
    hal = halo_ref[...]
    z0h = hal[:, :DM] * _sigmoid(hal[:, DM:])
    zs_ref[0:HALO, :] = jnp.where(first, 0.0, z0h)
    g = glu_ref[...]
    zs_ref[HALO:, :] = g[:, :DM] * _sigmoid(g[:, DM:])


def _conv_fwd(proj, cw, cb, lng, lnb, *, tm=256, rb=64):
    t = proj.shape[0]
    tm = min(tm, t)
    hb = tm // HALO
    gcol = C_GLU // (2 * DM)

    def body(glu_ref, halo_ref, cw_ref, cb_ref, lng_ref, lnb_ref, zc_ref, zb_ref, zs_ref):
        i = pl.program_id(0)
        _glu_into(zs_ref, glu_ref, halo_ref, i == 0)
        for cbk in range(DM // LANE):
            cs = slice(cbk * LANE, (cbk + 1) * LANE)
            for r in range(tm // rb):
                acc = jnp.broadcast_to(cb_ref[:, cs], (rb, LANE))
                for k in range(KW):
                    off = r * rb + HALO - (KW - 1) + k
                    acc = acc + cw_ref[k:k + 1, cs] * zs_ref[off:off + rb, cs]
                zc_ref[r * rb:(r + 1) * rb, cs] = acc
        zc = zc_ref[...]
        mu = _mean(zc)
        zcc = zc - mu
        zh = zcc * lax.rsqrt(_mean(zcc * zcc) + EPS)
        a = zh * lng_ref[...] + lnb_ref[...]
        zb_ref[...] = (a * _sigmoid(a)).astype(BF16)

    return pl.pallas_call(
        body, name="conv_fwd", grid=(t // tm,),
        in_specs=[_rows(tm, 2 * DM, gcol),
                  pl.BlockSpec((HALO, 2 * DM), lambda i: (jnp.maximum(i * hb - 1, 0), gcol)),
                  pl.BlockSpec((KW, DM), lambda i: (0, 0)), _vec(DM), _vec(DM), _vec(DM)],
        out_specs=[_rows(tm, DM), _rows(tm, DM)],
        out_shape=[jax.ShapeDtypeStruct((t, DM), F32), jax.ShapeDtypeStruct((t, DM), BF16)],
        scratch_shapes=[pltpu.VMEM((HALO + tm, DM), F32)],
        compiler_params=_cparams(("parallel",)))(proj, proj, cw, cb, lng, lnb)


def _conv_bwd_ln(dzb, zc, lng, lnb, *, tm=256):
    t = zc.shape[0]
    tm = min(tm, t)

    def body(dzb_ref, zc_ref, lng_ref, lnb_ref, dzc_ref, dlng_ref, dlnb_ref):
        @pl.when(pl.program_id(0) == 0)
        def _():
            dlng_ref[...] = jnp.zeros_like(dlng_ref)
            dlnb_ref[...] = jnp.zeros_like(dlnb_ref)

        zc = zc_ref[...]
        lng_v = lng_ref[...]
        mu = _mean(zc)
        zcc = zc - mu
        rstd = lax.rsqrt(_mean(zcc * zcc) + EPS)
        zh = zcc * rstd
        a = zh * lng_v + lnb_ref[...]
        s = _sigmoid(a)
        da = dzb_ref[...] * (s * (1.0 + a * (1.0 - s)))
        dlnb_ref[...] += _colsum(da)
        dlng_ref[...] += _colsum(da * zh)
        dzh = da * lng_v
        dzc_ref[...] = rstd * (dzh - _mean(dzh) - zh * _mean(dzh * zh))

    vec_out = jax.ShapeDtypeStruct((1, DM), F32)
    return pl.pallas_call(
        body, name="conv_bwd_ln", grid=(t // tm,), in_specs=[_rows(tm, DM), _rows(tm, DM), _vec(DM), _vec(DM)],
        out_specs=[_rows(tm, DM), _vec(DM), _vec(DM)],
        out_shape=[jax.ShapeDtypeStruct((t, DM), F32), vec_out, vec_out],
        compiler_params=_cparams(("arbitrary",)))(dzb, zc, lng, lnb)


def _conv_bwd(dzc, proj, cw, *, tm=256, rb=64):
    t = proj.shape[0]
    tm = min(tm, t)
    hb = tm // HALO
    nblk = t // tm
    gcol = C_GLU // (2 * DM)

    def body(dzc_ref, dnext_ref, glu_ref, halo_ref, cw_ref, dglu_ref, dcw_ref, dcb_ref, zs_ref, ds_ref):
        i = pl.program_id(0)

        @pl.when(i == 0)
        def _():
            dcw_ref[...] = jnp.zeros_like(dcw_ref)
            dcb_ref[...] = jnp.zeros_like(dcb_ref)

        _glu_into(zs_ref, glu_ref, halo_ref, i == 0)
        dzc = dzc_ref[...]
        ds_ref[0:tm, :] = dzc
        ds_ref[tm:, :] = jnp.where(i == nblk - 1, 0.0, dnext_ref[...])
        dcb_ref[...] += _colsum(dzc)
        for k in range(KW):
            off = HALO - (KW - 1) + k
            dcw_ref[k:k + 1, :] += _colsum(dzc * zs_ref[off:off + tm, :])
        g = glu_ref[...]
        val, sg = g[:, :DM], _sigmoid(g[:, DM:])
        for cbk in range(DM // LANE):
            cs = slice(cbk * LANE, (cbk + 1) * LANE)
            for r in range(tm // rb):
                acc = jnp.zeros((rb, LANE), F32)
                for k in range(KW):
                    off = r * rb + (KW - 1) - k
                    acc = acc + cw_ref[k:k + 1, cs] * ds_ref[off:off + rb, cs]
                rs = slice(r * rb, (r + 1) * rb)
                dglu_ref[rs, cs] = (acc * sg[rs, cs]).astype(BF16)
                v, s = val[rs, cs], sg[rs, cs]
                dglu_ref[rs, DM + cbk * LANE:DM + (cbk + 1) * LANE] = (acc * v * s * (1.0 - s)).astype(BF16)

    return pl.pallas_call(
        body, name="conv_bwd", grid=(nblk,),
        in_specs=[_rows(tm, DM),
                  pl.BlockSpec((HALO, DM), lambda i: (jnp.minimum((i + 1) * hb, nblk * hb - 1), 0)),
                  _rows(tm, 2 * DM, gcol),
                  pl.BlockSpec((HALO, 2 * DM), lambda i: (jnp.maximum(i * hb - 1, 0), gcol)),
                  pl.BlockSpec((KW, DM), lambda i: (0, 0))],
        out_specs=[_rows(tm, 2 * DM), pl.BlockSpec((HALO, DM), lambda i: (0, 0)), _vec(DM)],
        out_shape=[jax.ShapeDtypeStruct((t, 2 * DM), BF16), jax.ShapeDtypeStruct((HALO, DM), F32),
                   jax.ShapeDtypeStruct((1, DM), F32)],
        scratch_shapes=[pltpu.VMEM((HALO + tm, DM), F32), pltpu.VMEM((tm + HALO, DM), F32)],
        compiler_params=_cparams(("arbitrary",)))(dzc, dzc, proj, proj, cw)


def _log_sigmoid(x):
    return jnp.minimum(x, 0.0) - jnp.log1p(jnp.exp(-jnp.abs(x)))


def _fox_cum(proj, bfp):
    t = proj.shape[0]
    fcol = C_F // LANE

    def body(f_ref, bf_ref, cum_ref, carry_ref):
        @pl.when(pl.program_id(0) == 0)
        def _():
            carry_ref[...] = jnp.zeros_like(carry_ref)

        lf = _log_sigmoid(f_ref[...] + bf_ref[...])
        tri = (lax.broadcasted_iota(I32, (CH, CH), 0) >= lax.broadcasted_iota(I32, (CH, CH), 1)).astype(F32)
        cum = jnp.dot(tri, lf, preferred_element_type=F32, precision=lax.Precision.HIGHEST) + carry_ref[0:1, :]
        cum_ref[...] = cum
        carry_ref[...] = jnp.broadcast_to(cum[CH - 1:CH, :], carry_ref.shape)

    return pl.pallas_call(
        body, name="fox_cum", grid=(t // CH,), in_specs=[_rows(CH, LANE, fcol), _vec(LANE)],
        out_specs=_rows(CH, LANE), out_shape=jax.ShapeDtypeStruct((t, LANE), F32),
        scratch_shapes=[pltpu.VMEM((8, LANE), F32)],
        compiler_params=_cparams(("arbitrary",)))(proj, bfp)


def _fox_cum_bwd(dcum, proj, bfp):
    t = proj.shape[0]
    nb = t // CH
    fcol = C_F // LANE
    fw = D_INP - C_F

    def body(dc_ref, f_ref, bf_ref, df_ref, dbf_ref, carry_ref):
        @pl.when(pl.program_id(0) == 0)
        def _():
            carry_ref[...] = jnp.zeros_like(carry_ref)
            dbf_ref[...] = jnp.zeros_like(dbf_ref)

        triu = (lax.broadcasted_iota(I32, (CH, CH), 0) <= lax.broadcasted_iota(I32, (CH, CH), 1)).astype(F32)
        dlf = jnp.dot(triu, dc_ref[...], preferred_element_type=F32, precision=lax.Precision.HIGHEST) + carry_ref[0:1, :]
        carry_ref[...] = jnp.broadcast_to(dlf[0:1, :], carry_ref.shape)
        z = f_ref[...] + bf_ref[...]
        lane = lax.broadcasted_iota(I32, (CH, LANE), 1)
        df = jnp.where(lane < NG, dlf * _sigmoid(-z), 0.0)
        dbf_ref[...] += _colsum(df)
        df_ref[:, 0:LANE] = df.astype(BF16)
        df_ref[:, LANE:] = jnp.zeros((CH, fw - LANE), BF16)

    return pl.pallas_call(
        body, name="fox_cum_bwd", grid=(nb,),
        in_specs=[pl.BlockSpec((CH, LANE), lambda i: (nb - 1 - i, 0)),
                  pl.BlockSpec((CH, LANE), lambda i: (nb - 1 - i, fcol)), _vec(LANE)],
        out_specs=[pl.BlockSpec((CH, fw), lambda i: (nb - 1 - i, 0)), _vec(LANE)],
        out_shape=[jax.ShapeDtypeStruct((t, fw), BF16), jax.ShapeDtypeStruct((1, LANE), F32)],
        scratch_shapes=[pltpu.VMEM((8, LANE), F32)],
        compiler_params=_cparams(("arbitrary",)))(dcum, proj, bfp)


HD = 64
ATT_SCALE = 0.125
NEG = -1e30


def _qkv_prep(proj, *, tm=512):
    t = proj.shape[0]
    tm = min(tm, t)

    def body(q_ref, k_ref, v_ref, o_ref):
        o_ref[:, 0:DM] = (q_ref[...] * ATT_SCALE).astype(BF16)
        o_ref[:, DM:2 * DM] = k_ref[...].astype(BF16)
        o_ref[:, 2 * DM:] = v_ref[...].astype(BF16)

    return pl.pallas_call(
        body, name="qkv_prep", grid=(t // tm,),
        in_specs=[_rows(tm, DM, C_Q // DM), _rows(tm, DM, C_K // DM), _rows(tm, DM, C_V // DM)],
        out_specs=_rows(tm, 3 * DM), out_shape=jax.ShapeDtypeStruct((t, 3 * DM), BF16),
        compiler_params=_cparams(("parallel",)))(proj, proj, proj)


def _causal_pairs(nq, outer_is_query):
    if outer_is_query:
        pairs = [(i, j) for i in range(nq) for j in range(i + 1)]
    else:
        pairs = [(j, i) for j in range(nq) for i in range(j, nq)]
    return (jnp.asarray([p[0] for p in pairs], I32), jnp.asarray([p[1] for p in pairs], I32))


def _to_row(col):
    return jnp.transpose(col)[0:1, :]


def _rep(x, tk):
    return x if tk == LANE else jnp.tile(x, (1, tk // LANE))


def _attn_fwd(qkv, ckrow, *, tq=512):
    t = qkv.shape[0]
    tq = min(tq, t)
    tk = tq
    nq = t // tq
    oi, ij = _causal_pairs(nq, True)

    def body(oi_ref, ij_ref, q_ref, k_ref, v_ref, ck_ref, o_ref, lse_ref, lser_ref, m_ref, l_ref, acc_ref):
        n = pl.program_id(0)
        i, j = oi_ref[n], ij_ref[n]

        @pl.when(j == 0)
        def _():
            m_ref[...] = jnp.full_like(m_ref, NEG)
            l_ref[...] = jnp.zeros_like(l_ref)
            acc_ref[...] = jnp.zeros_like(acc_ref)

        def step(masked):
            if masked:
                keep = lax.broadcasted_iota(I32, (tq, tk), 1) <= lax.broadcasted_iota(I32, (tq, tk), 0)
            lo = lax.broadcasted_iota(I32, (tq, LANE), 1) < HD
            for hp in range(NG // 2):
                cs = slice(hp * LANE, (hp + 1) * LANE)
                qp, kp, vp = q_ref[:, cs], k_ref[:, cs], v_ref[:, cs]
                alphas, pvs = [], []
                for hh in range(2):
                    h = 2 * hp + hh
                    qm = jnp.where(lo if hh == 0 else jnp.logical_not(lo), qp, jnp.zeros_like(qp))
                    s = _dot(qm, kp, NT) - ck_ref[h:h + 1, :]
                    if masked:
                        s = jnp.where(keep, s, NEG)
                    m_prev = m_ref[h]
                    m_new = jnp.maximum(m_prev, jnp.max(s, axis=1, keepdims=True))
                    alpha = jnp.exp(m_prev - m_new)
                    p = jnp.exp(s - _rep(m_new, tk))
                    l_ref[h] = alpha * l_ref[h] + jnp.sum(p, axis=1, keepdims=True)
                    m_ref[h] = m_new
                    alphas.append(alpha)
                    pvs.append(_dot(p.astype(BF16), vp))
                acc_ref[:, cs] = jnp.where(lo, alphas[0], alphas[1]) * acc_ref[:, cs] + jnp.where(lo, pvs[0], pvs[1])

        @pl.when(j < i)
        def _():
            step(False)

        @pl.when(j == i)
        def _():
            step(True)
            lo = lax.broadcasted_iota(I32, (tq, LANE), 1) < HD
            for hp in range(NG // 2):
                cs = slice(hp * LANE, (hp + 1) * LANE)
                o_ref[:, cs] = (acc_ref[:, cs] / jnp.where(lo, l_ref[2 * hp], l_ref[2 * hp + 1])).astype(BF16)
            for h in range(NG):
                lse = m_ref[h] + jnp.log(l_ref[h])
                lse_ref[h] = lse
                lser_ref[h:h + 1, :] = _to_row(lse)

    gs = pltpu.PrefetchScalarGridSpec(
        num_scalar_prefetch=2, grid=(int(oi.shape[0]),),
        in_specs=[pl.BlockSpec((tq, DM), lambda n, a, b: (a[n], 0)),
                  pl.BlockSpec((tk, DM), lambda n, a, b: (b[n], 1)),
                  pl.BlockSpec((tk, DM), lambda n, a, b: (b[n], 2)),
                  pl.BlockSpec((NG, tk), lambda n, a, b: (0, b[n]))],
        out_specs=[pl.BlockSpec((tq, DM), lambda n, a, b: (a[n], 0)),
                   pl.BlockSpec((NG, tq, LANE), lambda n, a, b: (0, a[n], 0)),
                   pl.BlockSpec((NG, tq), lambda n, a, b: (0, a[n]))],
        scratch_shapes=[pltpu.VMEM((NG, tq, LANE), F32), pltpu.VMEM((NG, tq, LANE), F32), pltpu.VMEM((tq, DM), F32)])
    return pl.pallas_call(
        body, name="attn_fwd", grid_spec=gs,
        out_shape=[jax.ShapeDtypeStruct((t, DM), BF16), jax.ShapeDtypeStruct((NG, t, LANE), F32),
                   jax.ShapeDtypeStruct((NG, t), F32)],
        compiler_params=_cparams(("arbitrary",)))(oi, ij, qkv, qkv, qkv, ckrow)


def _attn_bwd_dq(qkv, ckrow, o, do, lse, *, tq=512):
    t = qkv.shape[0]
    tq = min(tq, t)
    tk = tq
    nq = t // tq
    oi, ij = _causal_pairs(nq, True)

    def body(oi_ref, ij_ref, q_ref, k_ref, v_ref, ck_ref, o_ref, do_ref, lse_ref, dq_ref, deltar_ref, dcqr_ref, delta_ref,
             dcq_ref, acc_ref):
        n = pl.program_id(0)
        i, j = oi_ref[n], ij_ref[n]

        @pl.when(j == 0)
        def _():
            acc_ref[...] = jnp.zeros_like(acc_ref)
            dcq_ref[...] = jnp.zeros_like(dcq_ref)
            lo = lax.broadcasted_iota(I32, (tq, LANE), 1) < HD
            for hp in range(NG // 2):
                cs = slice(hp * LANE, (hp + 1) * LANE)
                prod = do_ref[:, cs] * o_ref[:, cs].astype(F32)
                for hh in range(2):
                    d = jnp.sum(jnp.where(lo if hh == 0 else jnp.logical_not(lo), prod, 0.0), axis=1, keepdims=True)
                    dcol = jnp.broadcast_to(d, (tq, LANE))
                    delta_ref[2 * hp + hh] = dcol
                    deltar_ref[2 * hp + hh:2 * hp + hh + 1, :] = _to_row(dcol)

        def step(masked):
            if masked:
                keep = lax.broadcasted_iota(I32, (tq, tk), 1) <= lax.broadcasted_iota(I32, (tq, tk), 0)
            lo = lax.broadcasted_iota(I32, (tq, LANE), 1) < HD
            for hp in range(NG // 2):
                cs = slice(hp * LANE, (hp + 1) * LANE)
                qp, kp, vp, dop = q_ref[:, cs], k_ref[:, cs], v_ref[:, cs], do_ref[:, cs].astype(BF16)
                parts = []
                for hh in range(2):
                    h = 2 * hp + hh
                    sel = lo if hh == 0 else jnp.logical_not(lo)
                    s = _dot(jnp.where(sel, qp, jnp.zeros_like(qp)), kp, NT) - ck_ref[h:h + 1, :]
                    if masked:
                        s = jnp.where(keep, s, NEG)
                    p = jnp.exp(s - _rep(lse_ref[h], tk))
                    dp = _dot(jnp.where(sel, dop, jnp.zeros_like(dop)), vp, NT)
                    ds = p * (dp - _rep(delta_ref[h], tk))
                    dcq_ref[h] += jnp.sum(ds, axis=1, keepdims=True)
                    parts.append(_dot(ds.astype(BF16), kp))
                acc_ref[:, cs] += jnp.where(lo, parts[0], parts[1])

        @pl.when(j < i)
        def _():
            step(False)

        @pl.when(j == i)
        def _():
            step(True)
            dq_ref[...] = (acc_ref[...] * ATT_SCALE).astype(BF16)
            for h in range(NG):
                dcqr_ref[h:h + 1, :] = _to_row(dcq_ref[h])

    gs = pltpu.PrefetchScalarGridSpec(
        num_scalar_prefetch=2, grid=(int(oi.shape[0]),),
        in_specs=[pl.BlockSpec((tq, DM), lambda n, a, b: (a[n], 0)),
                  pl.BlockSpec((tk, DM), lambda n, a, b: (b[n], 1)),
                  pl.BlockSpec((tk, DM), lambda n, a, b: (b[n], 2)),
                  pl.BlockSpec((NG, tk), lambda n, a, b: (0, b[n])),
                  pl.BlockSpec((tq, DM), lambda n, a, b: (a[n], 0)),
                  pl.BlockSpec((tq, DM), lambda n, a, b: (a[n], 0)),
                  pl.BlockSpec((NG, tq, LANE), lambda n, a, b: (0, a[n], 0))],
        out_specs=[pl.BlockSpec((tq, DM), lambda n, a, b: (a[n], 0)),
                   pl.BlockSpec((NG, tq), lambda n, a, b: (0, a[n])),
                   pl.BlockSpec((NG, tq), lambda n, a, b: (0, a[n]))],
        scratch_shapes=[pltpu.VMEM((NG, tq, LANE), F32), pltpu.VMEM((NG, tq, LANE), F32), pltpu.VMEM((tq, DM), F32)])
    return pl.pallas_call(
        body, name="attn_bwd_dq", grid_spec=gs,
        out_shape=[jax.ShapeDtypeStruct((t, DM), BF16), jax.ShapeDtypeStruct((NG, t), F32), jax.ShapeDtypeStruct((NG, t), F32)],
        compiler_params=_cparams(("arbitrary",)))(oi, ij, qkv, qkv, qkv, ckrow, o, do, lse)


def _attn_bwd_dkv(qkv, ckcol, do, lserow, deltarow, *, tq=512):
    t = qkv.shape[0]
    tq = min(tq, t)
    tk = tq
    nq = t // tq
    oj, ii = _causal_pairs(nq, False)

    def body(oj_ref, ii_ref, q_ref, k_ref, v_ref, ck_ref, do_ref, lse_ref, delta_ref, dk_ref, dv_ref, dckr_ref, dka_ref,
             dva_ref, dck_ref):
        n = pl.program_id(0)
        j, i = oj_ref[n], ii_ref[n]

        @pl.when(i == j)
        def _():
            dka_ref[...] = jnp.zeros_like(dka_ref)
            dva_ref[...] = jnp.zeros_like(dva_ref)
            dck_ref[...] = jnp.zeros_like(dck_ref)

        def step(masked):
            if masked:
                keep = lax.broadcasted_iota(I32, (tk, tq), 0) <= lax.broadcasted_iota(I32, (tk, tq), 1)
            lo = lax.broadcasted_iota(I32, (tk, LANE), 1) < HD
            for hp in range(NG // 2):
                cs = slice(hp * LANE, (hp + 1) * LANE)
                qp, kp, vp, dop = q_ref[:, cs], k_ref[:, cs], v_ref[:, cs], do_ref[:, cs].astype(BF16)
                dvs, dks = [], []
                for hh in range(2):
                    h = 2 * hp + hh
                    sel = lo if hh == 0 else jnp.logical_not(lo)
                    st = _dot(jnp.where(sel, kp, jnp.zeros_like(kp)), qp, NT) - _rep(ck_ref[h], tq)
                    if masked:
                        st = jnp.where(keep, st, NEG)
                    pt = jnp.exp(st - lse_ref[h:h + 1, :])
                    dvs.append(_dot(pt.astype(BF16), dop))
                    dpt = _dot(jnp.where(sel, vp, jnp.zeros_like(vp)), dop, NT)
                    dst = pt * (dpt - delta_ref[h:h + 1, :])
                    dks.append(_dot(dst.astype(BF16), qp))
                    dck_ref[h] -= jnp.sum(dst, axis=1, keepdims=True)
                dva_ref[:, cs] += jnp.where(lo, dvs[0], dvs[1])
                dka_ref[:, cs] += jnp.where(lo, dks[0], dks[1])

        @pl.when(i == j)
        def _():
            step(True)

        @pl.when(i > j)
        def _():
            step(False)

        @pl.when(i == nq - 1)
        def _():
            dk_ref[...] = dka_ref[...].astype(BF16)
            dv_ref[...] = dva_ref[...].astype(BF16)
            for h in range(NG):
                dckr_ref[h:h + 1, :] = _to_row(dck_ref[h])

    gs = pltpu.PrefetchScalarGridSpec(
        num_scalar_prefetch=2, grid=(int(oj.shape[0]),),
        in_specs=[pl.BlockSpec((tq, DM), lambda n, a, b: (b[n], 0)),
                  pl.BlockSpec((tk, DM), lambda n, a, b: (a[n], 1)),
                  pl.BlockSpec((tk, DM), lambda n, a, b: (a[n], 2)),
                  pl.BlockSpec((NG, tk, LANE), lambda n, a, b: (0, a[n], 0)),
                  pl.BlockSpec((tq, DM), lambda n, a, b: (b[n], 0)),
                  pl.BlockSpec((NG, tq), lambda n, a, b: (0, b[n])),
                  pl.BlockSpec((NG, tq), lambda n, a, b: (0, b[n]))],
        out_specs=[pl.BlockSpec((tk, DM), lambda n, a, b: (a[n], 0)),
                   pl.BlockSpec((tk, DM), lambda n, a, b: (a[n], 0)),
                   pl.BlockSpec((NG, tk), lambda n, a, b: (0, a[n]))],
        scratch_shapes=[pltpu.VMEM((tk, DM), F32), pltpu.VMEM((tk, DM), F32), pltpu.VMEM((NG, tk, LANE), F32)])
    return pl.pallas_call(
        body, name="attn_bwd_dkv", grid_spec=gs,
        out_shape=[jax.ShapeDtypeStruct((t, DM), BF16), jax.ShapeDtypeStruct((t, DM), BF16), jax.ShapeDtypeStruct((NG, t), F32)],
        compiler_params=_cparams(("arbitrary",)))(oj, ii, qkv, qkv, qkv, ckcol, do, lserow, deltarow)


def _merge_fwd(ga, zb, att, proj, wa, wb, wc, *, tm=256):
    t = ga.shape[0]
    tm = min(tm, t)
    wspec = pl.BlockSpec((DM, D), lambda i: (0, 0))

    def body(ga_ref, zb_ref, att_ref, gate_ref, wa_ref, wb_ref, wc_ref, m_ref):
        acc = jnp.zeros((tm, D), F32)
        for b, (x_ref, w_ref) in enumerate(((ga_ref, wa_ref), (zb_ref, wb_ref), (att_ref, wc_ref))):
            acc = acc + _sigmoid(gate_ref[:, b * D:(b + 1) * D]) * _dot(x_ref[...], w_ref[...])
        m_ref[...] = acc.astype(BF16)

    return pl.pallas_call(
        body, name="merge_fwd", grid=(t // tm,),
        in_specs=[_rows(tm, DM), _rows(tm, DM), _rows(tm, DM), _rows(tm, 3 * D, 0), wspec, wspec, wspec],
        out_specs=_rows(tm, D), out_shape=jax.ShapeDtypeStruct((t, D), BF16),
        compiler_params=_cparams(("parallel",)))(ga, zb, att, proj, wa, wb, wc)


def _merge_bwd(dm, ga, zb, att, proj, wa, wb, wc, *, tm=256):
    t = ga.shape[0]
    tm = min(tm, t)
    wspec = pl.BlockSpec((DM, D), lambda i: (0, 0))

    def body(dm_ref, ga_ref, zb_ref, att_ref, gate_ref, wa_ref, wb_ref, wc_ref, dgate_ref, dga_ref, dzb_ref, datt_ref,
             dwa_ref, dwb_ref, dwc_ref):
        @pl.when(pl.program_id(0) == 0)
        def _():
            dwa_ref[...] = jnp.zeros_like(dwa_ref)
            dwb_ref[...] = jnp.zeros_like(dwb_ref)
            dwc_ref[...] = jnp.zeros_like(dwc_ref)

        dmv = dm_ref[...]
        branches = ((ga_ref, wa_ref, dga_ref, dwa_ref), (zb_ref, wb_ref, dzb_ref, dwb_ref),
                    (att_ref, wc_ref, datt_ref, dwc_ref))
        for b, (x_ref, w_ref, dx_ref, dw_ref) in enumerate(branches):
            xv, wv = x_ref[...], w_ref[...]
            y = _dot(xv, wv)
            g = _sigmoid(gate_ref[:, b * D:(b + 1) * D])
            dgate_ref[:, b * D:(b + 1) * D] = (dmv * y * g * (1.0 - g)).astype(BF16)
            dy = (dmv * g).astype(BF16)
            dx_ref[...] = _dot(dy, wv, NT)
            dw_ref[...] += _dot(xv, dy, TN)

    return pl.pallas_call(
        body, name="merge_bwd", grid=(t // tm,),
        in_specs=[_rows(tm, D), _rows(tm, DM), _rows(tm, DM), _rows(tm, DM), _rows(tm, 3 * D, 0), wspec, wspec, wspec],
        out_specs=[_rows(tm, 3 * D), _rows(tm, DM), _rows(tm, DM), _rows(tm, DM), wspec, wspec, wspec],
        out_shape=[jax.ShapeDtypeStruct((t, 3 * D), BF16)] + [jax.ShapeDtypeStruct((t, DM), F32)] * 3
        + [jax.ShapeDtypeStruct((DM, D), F32)] * 3,
        compiler_params=_cparams(("arbitrary",)))(dm, ga, zb, att, proj, wa, wb, wc)


def _heads_layout(cum):
    t = cum.shape[0]
    ckrow = cum[:, :NG].T
    return ckrow, jnp.broadcast_to(ckrow[:, :, None], (NG, t, LANE))


def _layer_fwd(x, mod, w):
    sh1, sc1, gt1, sh2, sc2, gt2 = (mod[k:k + 1] for k in range(NMOD))
    h1 = _norm_mod(x, w["mix_pre_g"], sc1, sh1, name="norm_mix")
    proj = _matmul(h1, w["w_in_p"], name="mm_proj")
    ga = _gmlp_fwd(proj, w["gmlp_ln_g"], w["gmlp_ln_b"], w["wsm"], w["bsx"])
    zc, zb = _conv_fwd(proj, w["conv_w"], w["conv_b"], w["conv_ln_g"], w["conv_ln_b"])
    cum = _fox_cum(proj, w["bfp"])
    ckrow, ckcol = _heads_layout(cum)
    qkv = _qkv_prep(proj)
    att, lse, lser = _attn_fwd(qkv, ckrow)
    merged = _merge_fwd(ga, zb, att, proj, w["w_a_out"], w["w_b_out"], w["w_c_out"])
    y1 = _matmul(merged, w["w_out"], name="mm_out")
    x2 = _resid(x, y1, gt1, w["mix_post_g"], name="resid_mix")
    h2 = _norm_mod(x2, w["mlp_pre_g"], sc2, sh2, name="norm_mlp")
    a, hid = _matmul(h2, w["mlp_w1"], name="mm_w1", b_slabs=True, extra_out=(BF16,),
                     epilogue=lambda acc: (acc, jnp.square(jnp.maximum(acc, 0.0))))
    y2 = _matmul(hid, w["mlp_w2"], name="mm_w2")
    x3 = _resid(x2, y2, gt2, w["mlp_post_g"], name="resid_mlp")
    saved = dict(x=x, h1=h1, proj=proj, ga=ga, zc=zc, zb=zb, qkv=qkv, ckrow=ckrow, ckcol=ckcol, att=att, lse=lse, lser=lser, merged=merged,
                 y1=y1, x2=x2, h2=h2, a=a, hid=hid, y2=y2)
    return x3, saved


def _layer_bwd(dx3, mod, w, s):
    sh1, sc1, gt1, sh2, sc2, gt2 = (mod[k:k + 1] for k in range(NMOD))
    g = {}
    dy2, dgt2, g["mlp_post_g"] = _resid_bwd(dx3, s["y2"], gt2, w["mlp_post_g"], name="resid_mlp_bwd")
    da = _matmul(dy2, w["mlp_w2"], tb=True, name="mm_dhid", out_dtype=BF16, extra=(s["a"],),
                 epilogue=lambda acc, a: (acc * (2.0 * jnp.maximum(a, 0.0)),))
    g["mlp_w2"] = _matmul(s["hid"], dy2, ta=True, name="mm_dw2")
    g["mlp_w1"] = _matmul(s["h2"], da, ta=True, name="mm_dw1", out_slabs=N_CHIPS)
    dh2 = _matmul(da, w["mlp_w1"], tb=True, name="mm_dh2", b_slabs=True)
    dx2, g["mlp_pre_g"], dsc2, dsh2 = _norm_bwd(dh2, dx3, s["x2"], w["mlp_pre_g"], sc2, name="norm_mlp_bwd")
    dy1, dgt1, g["mix_post_g"] = _resid_bwd(dx2, s["y1"], gt1, w["mix_post_g"], name="resid_mix_bwd")
    dmerged = _matmul(dy1, w["w_out"], tb=True, name="mm_dmerged")
    g["w_out"] = _matmul(s["merged"], dy1, ta=True, name="mm_dwout")
    dgate, dga, dzb, datt, g["w_a_out"], g["w_b_out"], g["w_c_out"] = _merge_bwd(
        dmerged, s["ga"], s["zb"], s["att"], s["proj"], w["w_a_out"], w["w_b_out"], w["w_c_out"])
    duv, g["gmlp_ws"], dbs, g["gmlp_ln_g"], g["gmlp_ln_b"] = _gmlp_bwd(
        dga, s["proj"], w["gmlp_ln_g"], w["gmlp_ln_b"], w["wsm"], w["wsmt"], w["bsx"])
    g["gmlp_bs"] = dbs[:, :NG].T
    dzc, g["conv_ln_g"], g["conv_ln_b"] = _conv_bwd_ln(dzb, s["zc"], w["conv_ln_g"], w["conv_ln_b"])
    dglu, dcw, g["conv_b"] = _conv_bwd(dzc, s["proj"], w["conv_w"])
    g["conv_w"] = dcw[:KW]
    dq, delta, dcq = _attn_bwd_dq(s["qkv"], s["ckrow"], s["att"], datt, s["lse"])
    dk, dv, dck = _attn_bwd_dkv(s["qkv"], s["ckcol"], datt, s["lser"], delta)
    dcum = jnp.pad((dcq + dck).T, ((0, 0), (0, LANE - NG)))
    df, dbf = _fox_cum_bwd(dcum, s["proj"], w["bfp"])
    g["fox_bf"] = dbf[0, :NG]
    dproj = jnp.concatenate([dgate, duv, dglu, dq, dk, dv, df], axis=1)
    g["w_in_p"] = _matmul(s["h1"], dproj, ta=True, name="mm_dwin")
    dh1 = _matmul(dproj, w["w_in_p"], tb=True, name="mm_dh1")
    dx, g["mix_pre_g"], dsc1, dsh1 = _norm_bwd(dh1, dx2, s["x"], w["mix_pre_g"], sc1, name="norm_mix_bwd")
    dmod = jnp.concatenate([dsh1, dsc1, dgt1, dsh2, dsc2, dgt2], axis=0)
    return dx, g, dmod


def _position():
    return lax.axis_index("x"), lax.axis_index("y"), lax.axis_index("c")


def _all_gather8(v):
    m_per, n = v.shape

    def body(x_ref, out_ref, send_sems, recv_sems, local_sem):
        x, y, c = _position()
        me, sibling = (x, y, c), (x, y, 1 - c)
        chips = [(1 - x, y), (x, 1 - y), (1 - x, 1 - y)]

        def rows(px, py, pc):
            return out_ref.at[pl.ds((4 * px + 2 * py + pc) * m_per, m_per), :]

        def copy(k, block, to, src=None):
            return pltpu.make_async_remote_copy(
                src_ref=rows(*block) if src is None else src, dst_ref=rows(*block), send_sem=send_sems.at[k],
                recv_sem=recv_sems.at[k], device_id=to, device_id_type=MESH)

        mine = pltpu.make_async_copy(x_ref, rows(*me), local_sem)
        mine.start()
        first = [copy(0, me, sibling, src=x_ref)]
        first += [copy(1 + j, me, (*chip, c), src=x_ref) for j, chip in enumerate(chips)]
        for cp in first:
            cp.start()
        passed = [copy(4 + j, (*chip, c), sibling) for j, chip in enumerate(chips)]
        for j, chip in enumerate(chips):
            copy(1 + j, (*chip, c), me).wait_recv()
            passed[j].start()
        copy(0, sibling, me).wait_recv()
        for j, chip in enumerate(chips):
            copy(4 + j, (*chip, 1 - c), me).wait_recv()
        for cp in first + passed:
            cp.wait_send()
        mine.wait()

    out = pl.pallas_call(
        body, name="all_gather8", out_shape=jax.ShapeDtypeStruct((N_DEV * m_per, n), v.dtype),
        in_specs=[pl.BlockSpec(memory_space=pltpu.VMEM)], out_specs=pl.BlockSpec(memory_space=pltpu.VMEM),
        scratch_shapes=[pltpu.SemaphoreType.DMA((7,)), pltpu.SemaphoreType.DMA((7,)), pltpu.SemaphoreType.DMA],
        compiler_params=pltpu.CompilerParams(vmem_limit_bytes=VMEM_LIMIT),
    )(v)
    return out.reshape(N_DEV, m_per, n)


def _all_gather_chips(shards):
    nw = len(shards)

    def body(*refs):
        w_refs, out_refs = refs[:nw], refs[nw:2 * nw]
        send_sems, recv_sems = refs[2 * nw:]
        x, y, c = _position()
        sibling = (x, y, 1 - c)
        chips = [(1 - x, y), (x, 1 - y), (1 - x, 1 - y)]

        def copy(wi, k, block, to, src=None):
            px, py, layer = block
            dst = out_refs[wi].at[layer, 2 * px + py]
            return pltpu.make_async_remote_copy(
                src_ref=dst if src is None else src, dst_ref=dst, send_sem=send_sems.at[wi * 6 + k],
                recv_sem=recv_sems.at[wi * 6 + k], device_id=to, device_id_type=MESH)

        first = [copy(wi, j, (x, y, c), (*chip, c), src=w_refs[wi].at[c]) for wi in range(nw) for j, chip in enumerate(chips)]
        for cp in first:
            cp.start()
        passed = []
        for wi in range(nw):
            for j, chip in enumerate(chips):
                copy(wi, j, (*chip, c), (x, y, c)).wait_recv()
                cp = copy(wi, 3 + j, (*chip, c), sibling)
                cp.start()
                passed.append(cp)
        for wi in range(nw):
            for j, chip in enumerate(chips):
                copy(wi, 3 + j, (*chip, 1 - c), (x, y, c)).wait_recv()
        for cp in first + passed:
            cp.wait_send()

    return pl.pallas_call(
        body, name="all_gather_chips",
        out_shape=[jax.ShapeDtypeStruct((2, N_CHIPS) + sh.shape[1:], sh.dtype) for sh in shards],
        in_specs=[ANY] * nw, out_specs=[ANY] * nw,
        scratch_shapes=[pltpu.SemaphoreType.DMA((6 * nw,)), pltpu.SemaphoreType.DMA((6 * nw,))],
    )(*shards)


def _swap_layers(g0, g1):
    nw = len(g0)

    def body(*refs):
        g_refs = (refs[:nw], refs[nw:2 * nw])
        out_refs = refs[2 * nw:3 * nw]
        send_sems, recv_sems = refs[3 * nw:]
        x, y, c = _position()
        for layer in range(2):
            @pl.when(c == 1 - layer)
            def _():
                copies = [pltpu.make_async_remote_copy(
                    src_ref=g_refs[layer][wi], dst_ref=out_refs[wi], send_sem=send_sems.at[wi], recv_sem=recv_sems.at[wi],
                    device_id=(x, y, 1 - c), device_id_type=MESH) for wi in range(nw)]
                for cp in copies:
                    cp.start()
                for cp in copies:
                    cp.wait()

    return pl.pallas_call(
        body, name="rs_swap_layers", out_shape=[jax.ShapeDtypeStruct(g.shape, g.dtype) for g in g0],
        in_specs=[ANY] * (2 * nw), out_specs=[ANY] * nw,
        scratch_shapes=[pltpu.SemaphoreType.DMA((nw,)), pltpu.SemaphoreType.DMA((nw,))],
    )(*g0, *g1)


def _scatter_chips(sps):
    nw = len(sps)

    def body(*refs):
        s_refs, out_refs = refs[:nw], refs[nw:2 * nw]
        send_sems, recv_sems = refs[2 * nw:]
        x, y, c = _position()
        chips = [(1 - x, y), (x, 1 - y), (1 - x, 1 - y)]
        copies = [pltpu.make_async_remote_copy(
            src_ref=s_refs[wi].at[2 * px + py], dst_ref=out_refs[wi].at[k], send_sem=send_sems.at[wi * 3 + k],
            recv_sem=recv_sems.at[wi * 3 + k], device_id=(px, py, c), device_id_type=MESH)
            for wi in range(nw) for k, (px, py) in enumerate(chips)]
        for cp in copies:
            cp.start()
        for cp in copies:
            cp.wait()

    return pl.pallas_call(
        body, name="rs_scatter_chips", out_shape=[jax.ShapeDtypeStruct((3,) + sp.shape[1:], sp.dtype) for sp in sps],
        in_specs=[ANY] * nw, out_specs=[ANY] * nw,
        scratch_shapes=[pltpu.SemaphoreType.DMA((3 * nw,)), pltpu.SemaphoreType.DMA((3 * nw,))],
    )(*sps)


def _swap_reduced(reds):
    nw = len(reds)

    def body(*refs):
        r_refs, out_refs = refs[:nw], refs[nw:2 * nw]
        send_sems, recv_sems = refs[2 * nw:]
        x, y, c = _position()
        copies = [pltpu.make_async_remote_copy(
            src_ref=r_refs[wi], dst_ref=out_refs[wi], send_sem=send_sems.at[wi], recv_sem=recv_sems.at[wi],
            device_id=(x, y, 1 - c), device_id_type=MESH) for wi in range(nw)]
        for cp in copies:
            cp.start()
        for cp in copies:
            cp.wait()

    return pl.pallas_call(
        body, name="rs_swap_reduced", out_shape=[jax.ShapeDtypeStruct(r.shape, r.dtype) for r in reds],
        in_specs=[ANY] * nw, out_specs=[ANY] * nw,
        scratch_shapes=[pltpu.SemaphoreType.DMA((nw,)), pltpu.SemaphoreType.DMA((nw,))],
    )(*reds)


def _row_tile(rows, cols):
    tr = rows
    while tr * cols * 4 > (2 << 20) and tr % 32 == 0:
        tr //= 2
    return tr


def _add_own_layer(g0, g1, recv, c, *, name):
    nj, r, w = recv.shape
    tr = _row_tile(r, w)

    def body(c_ref, g0_ref, g1_ref, r_ref, o_ref, ob_ref):
        sm = jnp.where(c_ref[0] == 0, g0_ref[0], g1_ref[0]) + r_ref[0]
        o_ref[0] = sm
        ob_ref[0] = sm.astype(BF16)

    spec = pl.BlockSpec((1, tr, w), lambda j, i, cc: (j, i, 0))
    gs = pltpu.PrefetchScalarGridSpec(
        num_scalar_prefetch=1, grid=(nj, r // tr),
        in_specs=[pl.BlockSpec((1, tr, w), lambda j, i, cc: (j, jnp.where(cc[0] == 0, i, 0), 0)),
                  pl.BlockSpec((1, tr, w), lambda j, i, cc: (j, jnp.where(cc[0] == 1, i, 0), 0)), spec],
        out_specs=[spec, spec])
    return pl.pallas_call(body, name=name, grid_spec=gs,
                          out_shape=[jax.ShapeDtypeStruct((nj, r, w), F32), jax.ShapeDtypeStruct((nj, r, w), BF16)],
                          compiler_params=_cparams(("parallel", "parallel")))(jnp.reshape(c, (1,)).astype(I32), g0, g1, recv)


def _add_own_chip(sp, recv, j, *, name):
    _, r, w = sp.shape
    tr = _row_tile(r, w)

    def body(j_ref, s_ref, r_ref, o_ref):
        o_ref[...] = ((s_ref[0] + r_ref[0].astype(F32)) + r_ref[1].astype(F32)) + r_ref[2].astype(F32)

    gs = pltpu.PrefetchScalarGridSpec(
        num_scalar_prefetch=1, grid=(r // tr,),
        in_specs=[pl.BlockSpec((1, tr, w), lambda i, jj: (jj[0], i, 0)), pl.BlockSpec((3, tr, w), lambda i, jj: (0, i, 0))],
        out_specs=pl.BlockSpec((tr, w), lambda i, jj: (i, 0)))
    return pl.pallas_call(body, name=name, grid_spec=gs, out_shape=jax.ShapeDtypeStruct((r, w), F32),
                          compiler_params=_cparams(("parallel",)))(jnp.reshape(j, (1,)).astype(I32), sp, recv)


def _sum8(v):
    _, m, n = v.shape

    def body(v_ref, o_ref):
        acc = v_ref[0]
        for k in range(1, N_DEV):
            acc = acc + v_ref[k]
        o_ref[...] = acc

    return pl.pallas_call(body, name="sum8", grid=(1,), in_specs=[pl.BlockSpec((N_DEV, m, n), lambda i: (0, 0, 0))],
                          out_specs=pl.BlockSpec((m, n), lambda i: (0, 0)), out_shape=jax.ShapeDtypeStruct((m, n), F32),
                          compiler_params=_cparams(("arbitrary",)))(v)


def _ada_mod(c_all, ada_w, ada_b_loc, *, tn=512):
    nl, _, ncol = ada_w.shape

    def body(c_ref, w_ref, b_ref, o_ref):
        cv = c_ref[...]
        ca = (cv * _sigmoid(cv)).astype(BF16)
        o_ref[0] = _dot(ca, w_ref[0].astype(BF16)) + b_ref[0]

    return pl.pallas_call(
        body, name="ada_mod", grid=(nl, ncol // tn),
        in_specs=[pl.BlockSpec((N_DEV, D), lambda l, j: (0, 0)), pl.BlockSpec((1, D, tn), lambda l, j: (l, 0, j)),
                  pl.BlockSpec((1, 1, tn), lambda l, j: (l, 0, j))],
        out_specs=pl.BlockSpec((1, N_DEV, tn), lambda l, j: (l, 0, j)),
        out_shape=jax.ShapeDtypeStruct((nl, N_DEV, ncol), F32),
        compiler_params=_cparams(("parallel", "parallel")))(c_all, ada_w, ada_b_loc)


def _ada_grad(c_pad, dmod_pad, *, tn=512):
    nl, nb, ncol = dmod_pad.shape

    def body(c_ref, d_ref, o_ref):
        cv = c_ref[...]
        ca = (cv * _sigmoid(cv)).astype(BF16)
        o_ref[0] = _dot(ca, d_ref[0].astype(BF16), TN)

    return pl.pallas_call(
        body, name="ada_grad", grid=(nl, ncol // tn),
        in_specs=[pl.BlockSpec((nb, D), lambda l, j: (0, 0)), pl.BlockSpec((1, nb, tn), lambda l, j: (l, 0, j))],
        out_specs=pl.BlockSpec((1, D, tn), lambda l, j: (l, 0, j)),
        out_shape=jax.ShapeDtypeStruct((nl, D, ncol), F32),
        compiler_params=_cparams(("parallel", "parallel")))(c_pad, dmod_pad)


def _adamw(w, g, m, v, *, name):
    shape = w.shape
    if w.ndim == 3:
        lead, rows, cols = shape
    else:
        lead, (rows, cols) = 1, shape
    w3, g3, m3, v3 = (a.reshape(lead, rows, cols) for a in (w, g, m, v))
    tr = rows
    if rows * cols * 4 > (2 << 20):
        tr = next(cand for cand in (256, 128, 64, 8) if rows % cand == 0)
    c1 = 1.0 - ADAM_B1 ** ADAM_STEP
    c2 = 1.0 - ADAM_B2 ** ADAM_STEP

    def body(w_ref, g_ref, m_ref, v_ref, d_ref, nm_ref, nv_ref):
        gv = g_ref[...]
        nm = ADAM_B1 * m_ref[...] + (1.0 - ADAM_B1) * gv
        nv = ADAM_B2 * v_ref[...] + (1.0 - ADAM_B2) * (gv * gv)
        nm_ref[...] = nm
        nv_ref[...] = nv
        d_ref[...] = -ADAM_LR * ((nm / c1) / (jnp.sqrt(nv / c2) + ADAM_EPS) + ADAM_WD * w_ref[...])

    spec = pl.BlockSpec((1, tr, cols), lambda l, i: (l, i, 0))
    outs = pl.pallas_call(
        body, name=name, grid=(lead, rows // tr), in_specs=[spec] * 4, out_specs=[spec] * 3,
        out_shape=[jax.ShapeDtypeStruct((lead, rows, cols), F32)] * 3,
        compiler_params=_cparams(("parallel", "parallel")))(w3, g3, m3, v3)
    return tuple(o.reshape(shape) for o in outs)


def _adamw_big(w, g_own, g_other, m, v, c, *, name):
    _, rows, cols = w.shape
    tr = _row_tile(rows, cols)
    c1 = 1.0 - ADAM_B1 ** ADAM_STEP
    c2 = 1.0 - ADAM_B2 ** ADAM_STEP

    def body(c_ref, w_ref, go_ref, gx_ref, m_ref, v_ref, g_ref, d_ref, nm_ref, nv_ref):
        gv = jnp.where(pl.program_id(0) == c_ref[0], go_ref[...], gx_ref[...])
        nm = ADAM_B1 * m_ref[0] + (1.0 - ADAM_B1) * gv
        nv = ADAM_B2 * v_ref[0] + (1.0 - ADAM_B2) * (gv * gv)
        g_ref[0] = gv
        nm_ref[0] = nm
        nv_ref[0] = nv
        d_ref[0] = -ADAM_LR * ((nm / c1) / (jnp.sqrt(nv / c2) + ADAM_EPS) + ADAM_WD * w_ref[0])

    spec = pl.BlockSpec((1, tr, cols), lambda l, i, cc: (l, i, 0))
    gs = pltpu.PrefetchScalarGridSpec(
        num_scalar_prefetch=1, grid=(2, rows // tr),
        in_specs=[spec, pl.BlockSpec((tr, cols), lambda l, i, cc: (jnp.where(l == cc[0], i, 0), 0)),
                  pl.BlockSpec((tr, cols), lambda l, i, cc: (jnp.where(l == cc[0], 0, i), 0)), spec, spec],
        out_specs=[spec] * 4)
    return pl.pallas_call(
        body, name=name, grid_spec=gs, out_shape=[jax.ShapeDtypeStruct(w.shape, F32)] * 4,
        compiler_params=_cparams(("parallel", "parallel")))(jnp.reshape(c, (1,)).astype(I32), w, g_own, g_other, m, v)


SMALL = (("mix_pre_g", (2, D)), ("mix_post_g", (2, D)), ("mlp_pre_g", (2, D)), ("mlp_post_g", (2, D)),
         ("gmlp_ln_g", (2, DM)), ("gmlp_ln_b", (2, DM)), ("gmlp_ws", (2, NG, CH, CH)), ("gmlp_bs", (2, NG, CH)),
         ("conv_b", (2, DM)), ("conv_ln_g", (2, DM)), ("conv_ln_b", (2, DM)), ("fox_bf", (2, NG)))


def _pack_rows(arrays, mult=8):
    flat = []
    for a in arrays:
        f = a.reshape(-1).astype(F32)
        pad = (-f.shape[0]) % LANE
        flat.append(jnp.pad(f, (0, pad)) if pad else f)
    cat = jnp.concatenate(flat)
    rows = cat.shape[0] // LANE
    pad_rows = (-rows) % mult
    if pad_rows:
        cat = jnp.pad(cat, (0, pad_rows * LANE))
    return cat.reshape(-1, LANE)


def _unpack_rows(buf, shapes):
    flat = buf.reshape(-1)
    out, off = [], 0
    for shp in shapes:
        size = 1
        for d in shp:
            size *= d
        out.append(flat[off:off + size].reshape(shp))
        off += size + ((-size) % LANE)
    return out


def _assemble_w_in(w_in_full):
    uv_glu_qkv = w_in_full[:, :3584]
    f = w_in_full[:, 3584:3592]
    gate = w_in_full[:, 3592:]
    fpad = jnp.zeros((D, D_INP - C_F - NG), w_in_full.dtype)
    return jnp.concatenate([gate, uv_glu_qkv, f, fpad], axis=1)


def _disassemble_w_in(g_p):
    return jnp.concatenate([g_p[:, C_UV:C_F], g_p[:, C_F:C_F + NG], g_p[:, :C_UV]], axis=1)


def kernel(x, c, ada_w, ada_b, mix_pre_g, mix_post_g, mlp_pre_g, mlp_post_g, w_in, gmlp_ln_g, gmlp_ln_b, gmlp_ws, gmlp_bs, w_a_out, conv_w, conv_b, conv_ln_g, conv_ln_b, w_b_out, fox_bf, w_c_out, w_out, mlp_w1, mlp_w2, loss_target, m_ada_w, m_ada_b, m_mix_pre_g, m_mix_post_g, m_mlp_pre_g, m_mlp_post_g, m_w_in, m_gmlp_ln_g, m_gmlp_ln_b, m_gmlp_ws, m_gmlp_bs, m_w_a_out, m_conv_w, m_conv_b, m_conv_ln_g, m_conv_ln_b, m_w_b_out, m_fox_bf, m_w_c_out, m_w_out, m_mlp_w1, m_mlp_w2, v_ada_w, v_ada_b, v_mix_pre_g, v_mix_post_g, v_mlp_pre_g, v_mlp_post_g, v_w_in, v_gmlp_ln_g, v_gmlp_ln_b, v_gmlp_ws, v_gmlp_bs, v_w_a_out, v_conv_w, v_conv_b, v_conv_ln_g, v_conv_ln_b, v_w_b_out, v_fox_bf, v_w_c_out, v_w_out, v_mlp_w1, v_mlp_w2):
    weights = dict(ada_w=ada_w, ada_b=ada_b, mix_pre_g=mix_pre_g, mix_post_g=mix_post_g, mlp_pre_g=mlp_pre_g,
                   mlp_post_g=mlp_post_g, w_in=w_in, gmlp_ln_g=gmlp_ln_g, gmlp_ln_b=gmlp_ln_b, gmlp_ws=gmlp_ws,
                   gmlp_bs=gmlp_bs, w_a_out=w_a_out, conv_w=conv_w, conv_b=conv_b, conv_ln_g=conv_ln_g,
                   conv_ln_b=conv_ln_b, w_b_out=w_b_out, fox_bf=fox_bf, w_c_out=w_c_out, w_out=w_out, mlp_w1=mlp_w1,
                   mlp_w2=mlp_w2)
    mom_m = dict(ada_w=m_ada_w, ada_b=m_ada_b, mix_pre_g=m_mix_pre_g, mix_post_g=m_mix_post_g, mlp_pre_g=m_mlp_pre_g,
                 mlp_post_g=m_mlp_post_g, w_in=m_w_in, gmlp_ln_g=m_gmlp_ln_g, gmlp_ln_b=m_gmlp_ln_b, gmlp_ws=m_gmlp_ws,
                 gmlp_bs=m_gmlp_bs, w_a_out=m_w_a_out, conv_w=m_conv_w, conv_b=m_conv_b, conv_ln_g=m_conv_ln_g,
                 conv_ln_b=m_conv_ln_b, w_b_out=m_w_b_out, fox_bf=m_fox_bf, w_c_out=m_w_c_out, w_out=m_w_out,
                 mlp_w1=m_mlp_w1, mlp_w2=m_mlp_w2)
    mom_v = dict(ada_w=v_ada_w, ada_b=v_ada_b, mix_pre_g=v_mix_pre_g, mix_post_g=v_mix_post_g, mlp_pre_g=v_mlp_pre_g,
                 mlp_post_g=v_mlp_post_g, w_in=v_w_in, gmlp_ln_g=v_gmlp_ln_g, gmlp_ln_b=v_gmlp_ln_b, gmlp_ws=v_gmlp_ws,
                 gmlp_bs=v_gmlp_bs, w_a_out=v_w_a_out, conv_w=v_conv_w, conv_b=v_conv_b, conv_ln_g=v_conv_ln_g,
                 conv_ln_b=v_conv_ln_b, w_b_out=v_w_b_out, fox_bf=v_fox_bf, w_c_out=v_w_c_out, w_out=v_w_out,
                 mlp_w1=v_mlp_w1, mlp_w2=v_mlp_w2)
    order = list(weights)
    px, py, pc = _position()
    chip = 2 * px + py
    dev = 2 * chip + pc
    depth = ada_w.shape[0]
    t = x.shape[1]
    xl = x.reshape(t, D)
    tgt = loss_target.reshape(t, D)

    small_in = _pack_rows([c, conv_w])
    gathered = _all_gather8(small_in)
    c_all = gathered[:, :D // LANE, :].reshape(N_DEV, D)
    cw_rows = depth * KW * LANE // LANE
    conv_w_full = jnp.concatenate(
        [gathered[2 * j, D // LANE:D // LANE + cw_rows, :].reshape(depth, KW, LANE) for j in range(N_CHIPS)], axis=2)

    ncol = ada_w.shape[2]
    ada_b_loc = lax.dynamic_slice_in_dim(ada_b, chip * ncol, ncol, axis=1).reshape(depth, 1, ncol)
    mod_sh = _ada_mod(c_all, ada_w, ada_b_loc)
    mod_g = _all_gather8(mod_sh.reshape(-1, LANE)).reshape(N_DEV, depth, N_DEV, ncol)
    mod_all = jnp.concatenate([mod_g[2 * j] for j in range(N_CHIPS)], axis=2)
    mod_mine = lax.dynamic_index_in_dim(mod_all, dev, axis=1, keepdims=False)

    big_names = [b[0] for b in BIG]
    shards = [weights[n].astype(BF16) for n in big_names]
    gath = [lax.dynamic_update_slice(g, sh[:, None], (0, chip, 0, 0))
            for g, sh in zip(_all_gather_chips(shards), shards)]
    layers = []
    for l in range(depth):
        w = {}
        for (n, r, cdim, ax), g in zip(BIG, gath):
            if n == "mlp_w1":
                w[n] = g[l]
            elif ax == 1:
                w[n] = g[l].transpose(1, 0, 2).reshape(r, N_CHIPS * cdim)
            else:
                w[n] = g[l].reshape(N_CHIPS * r, cdim)
        w["w_in_p"] = _assemble_w_in(w.pop("w_in"))
        for n in ("mix_pre_g", "mix_post_g", "mlp_pre_g", "mlp_post_g", "gmlp_ln_g", "gmlp_ln_b", "conv_b", "conv_ln_g",
                  "conv_ln_b"):
            w[n] = weights[n][l:l + 1]
        tril = jnp.tril(jnp.ones((CH, CH), F32))
        wsm = gmlp_ws[l] * tril
        w["wsm"] = wsm.astype(BF16)
        w["wsmt"] = jnp.swapaxes(wsm, 1, 2).astype(BF16)
        w["bsx"] = jnp.repeat(gmlp_bs[l].T, HD, axis=1)
        w["conv_w"] = conv_w_full[l]
        w["bfp"] = jnp.pad(fox_bf[l], (0, LANE - NG)).reshape(1, LANE)
        layers.append(w)

    xs, saved = xl, []
    for l in range(depth):
        xs, s = _layer_fwd(xs, mod_mine[l].reshape(NMOD, D), layers[l])
        saved.append(s)
    loss_local, dx = _loss_and_grad(xs, tgt)
    loss = lax.psum(loss_local, ("x", "y", "c"))
    grads, dmods = [None] * depth, [None] * depth
    for l in reversed(range(depth)):
        dx, grads[l], dmods[l] = _layer_bwd(dx, mod_mine[l].reshape(NMOD, D), layers[l], saved[l])
    grad_x = dx.reshape(x.shape)

    def slabs(gfull, n, r, cdim, ax):
        if n == "mlp_w1":
            return gfull
        if ax == 1:
            return gfull.reshape(gfull.shape[0], N_CHIPS, cdim).transpose(1, 0, 2)
        return gfull.reshape(N_CHIPS, r, cdim)

    for l in range(depth):
        grads[l]["w_in"] = _disassemble_w_in(grads[l].pop("w_in_p"))
    gsl = [[slabs(grads[l][n], n, r, cdim, ax) for n, r, cdim, ax in BIG] for l in range(depth)]
    from_sibling = _swap_layers(gsl[0], gsl[1])
    sums = [_add_own_layer(g0, g1, rv, pc, name="rs_add_layer_" + n)
            for g0, g1, rv, n in zip(gsl[0], gsl[1], from_sibling, big_names)]
    from_chips = _scatter_chips([sb for _, sb in sums])
    reduced = [_add_own_chip(sf, rv, chip, name="rs_add_chip_" + n)
               for (sf, _), rv, n in zip(sums, from_chips, big_names)]
    from_sib = _swap_reduced(reduced)
    g_out = {}

    small_names = [n for n, _ in SMALL]
    small_list = [jnp.stack(dmods)] + [jnp.stack([grads[l][n] for l in range(depth)]) for n in small_names]
    small_list.append(jnp.stack([grads[l]["conv_w"] for l in range(depth)]))
    small_shapes = [(depth, NMOD * D)] + [shp for _, shp in SMALL] + [(depth, KW, DM)]
    small_pack = _pack_rows(small_list)
    small_all = _all_gather8(small_pack)
    small_sum = _unpack_rows(_sum8(small_all), small_shapes)
    g_out["ada_b"] = small_sum[0]
    for n, gs in zip(small_names, small_sum[1:-1]):
        g_out[n] = gs
    g_out["conv_w"] = lax.dynamic_slice_in_dim(small_sum[-1], chip * LANE, LANE, axis=2)
    dmod_all = small_all[:, :depth * NMOD * D // LANE, :].reshape(N_DEV, depth, NMOD * D)
    dmod_loc = lax.dynamic_slice_in_dim(dmod_all, chip * ncol, ncol, axis=2).transpose(1, 0, 2)
    g_out["ada_w"] = _ada_grad(jnp.pad(c_all, ((0, 8), (0, 0))), jnp.pad(dmod_loc, ((0, 0), (0, 8), (0, 0))))

    delta, new_m, new_v = {}, {}, {}
    delta["ada_w"], new_m["ada_w"], new_v["ada_w"] = _adamw(ada_w, g_out["ada_w"], m_ada_w, v_ada_w, name="adamw_ada_w")
    for n, own, oth in zip(big_names, reduced, from_sib):
        g_out[n], delta[n], new_m[n], new_v[n] = _adamw_big(weights[n], own, oth, mom_m[n], mom_v[n], pc, name="adamw_" + n)
    small_params = ["ada_b"] + small_names + ["conv_w"]
    packs = [_pack_rows([d[n] for n in small_params]) for d in (weights, g_out, mom_m, mom_v)]
    outs = _adamw(*packs, name="adamw_small")
    shapes = [weights[n].shape for n in small_params]
    for dst, buf in zip((delta, new_m, new_v), outs):
        for n, a in zip(small_params, _unpack_rows(buf, shapes)):
            dst[n] = a

    return (loss, grad_x, *[g_out[n] for n in order], *[delta[n] for n in order], *[new_m[n] for n in order],
            *[new_v[n] for n in order])
```

```python
import functools

import jax
import jax.numpy as jnp
from jax import lax
from jax.experimental import pallas as pl
from jax.experimental.pallas import tpu as pltpu

F32 = jnp.float32
BF16 = jnp.bfloat16
I32 = jnp.int32
MESH = pl.DeviceIdType.MESH
ANY = pl.BlockSpec(memory_space=pl.ANY)

D = 1024
DM = 512
NG = 8
CH = 128
KW = 31
HALO = 32
DFF = 4096
NMOD = 6
EPS = 1e-6
LANE = 128
N_CHIPS = 4
N_DEV = 8
C_GATE, C_UV, C_GLU, C_Q, C_K, C_V, C_F, D_INP = 0, 3072, 4096, 5120, 5632, 6144, 6656, 7168
D_IN = 6664
VMEM_LIMIT = 56 * 1024 * 1024

ADAM_LR, ADAM_B1, ADAM_B2, ADAM_EPS, ADAM_WD, ADAM_STEP = 0.001, 0.9, 0.999, 1e-08, 0.01, 10

BIG = (("w_in", 1024, 1666, 1), ("w_a_out", 512, 256, 1), ("w_b_out", 512, 256, 1), ("w_c_out", 512, 256, 1),
       ("w_out", 256, 1024, 0), ("mlp_w1", 1024, 1024, 1), ("mlp_w2", 1024, 1024, 0))


def _cparams(sem):
    return pltpu.CompilerParams(dimension_semantics=sem, vmem_limit_bytes=VMEM_LIMIT)


def _sigmoid(x):
    return jax.nn.sigmoid(x)


_GELU_K = 0.7978845608028654
_GELU_A = 0.044715


def _gelu(x):
    t = jnp.tanh(_GELU_K * (x + _GELU_A * x * x * x))
    return 0.5 * x * (1.0 + t)


def _gelu_grad(x):
    t = jnp.tanh(_GELU_K * (x + _GELU_A * x * x * x))
    return 0.5 * (1.0 + t) + 0.5 * x * (1.0 - t * t) * _GELU_K * (1.0 + 3.0 * _GELU_A * x * x)


def _mean(x):
    return jnp.mean(x, axis=-1, keepdims=True)


def _colsum(x):
    return jnp.sum(x, axis=0, keepdims=True)


def _dot(a, b, dims=((1,), (0,))):
    return lax.dot_general(a, b, (dims, ((), ())), preferred_element_type=F32)


NN = ((1,), (0,))
NT = ((1,), (1,))
TN = ((0,), (0,))


def _matmul(a, b, *, name, ta=False, tb=False, out_dtype=F32, tm=1024, tn=1024, tk=1024, epilogue=None, extra=(),
            extra_out=(), b_slabs=False, out_slabs=0):
    m, k = (a.shape[1], a.shape[0]) if ta else a.shape
    if b_slabs:
        ns, brows, bw = b.shape
        n = brows if tb else ns * bw
        assert (ns * bw if tb else brows) == k, (name, b.shape, k)
        tn, tk = (tn, bw) if tb else (bw, tk)
    else:
        n = b.shape[0] if tb else b.shape[1]
    tm, tn, tk = min(tm, m), min(tn, n), min(tk, k)
    assert m % tm == 0 and n % tn == 0 and k % tk == 0, (name, m, n, k, tm, tn, tk)
    assert not out_slabs or (n // out_slabs == tn and epilogue is None), name
    nk = k // tk
    dims = ((0 if ta else 1,), (1 if tb else 0,))
    n_extra = len(extra)
    out_dtypes = (out_dtype,) + tuple(extra_out)

    def body(a_ref, b_ref, *rest):
        extra_refs = rest[:n_extra]
        out_refs = rest[n_extra:n_extra + len(out_dtypes)]
        kk = pl.program_id(2)
        part = _dot(a_ref[...].astype(BF16), b_ref[...].astype(BF16), dims)

        def finish(acc):
            outs = (acc,) if epilogue is None else epilogue(acc, *[r[...] for r in extra_refs])
            for o_ref, o in zip(out_refs, outs):
                o_ref[...] = o.astype(o_ref.dtype)

        if nk == 1:
            finish(part)
        else:
            acc_ref = rest[-1]

            @pl.when(kk == 0)
            def _():
                acc_ref[...] = part

            @pl.when(jnp.logical_and(kk > 0, kk < nk - 1))
            def _():
                acc_ref[...] += part

            @pl.when(kk == nk - 1)
            def _():
                finish(acc_ref[...] + part)

    a_spec = pl.BlockSpec((tk, tm), lambda i, j, kk: (kk, i)) if ta else pl.BlockSpec((tm, tk), lambda i, j, kk: (i, kk))
    if b_slabs and tb:
        b_spec = pl.BlockSpec((None, tn, tk), lambda i, j, kk: (kk, j, 0))
    elif b_slabs:
        b_spec = pl.BlockSpec((None, tk, tn), lambda i, j, kk: (j, kk, 0))
    else:
        b_spec = pl.BlockSpec((tn, tk), lambda i, j, kk: (j, kk)) if tb else pl.BlockSpec((tk, tn), lambda i, j, kk: (kk, j))
    if out_slabs:
        o_spec = pl.BlockSpec((None, tm, tn), lambda i, j, kk: (j, i, 0))
        o_shape = (out_slabs, m, tn)
    else:
        o_spec = pl.BlockSpec((tm, tn), lambda i, j, kk: (i, j))
        o_shape = (m, n)
    outs = pl.pallas_call(
        body, name=name, grid=(m // tm, n // tn, nk),
        in_specs=[a_spec, b_spec] + [o_spec] * n_extra,
        out_specs=[o_spec] * len(out_dtypes),
        out_shape=[jax.ShapeDtypeStruct(o_shape, dt) for dt in out_dtypes],
        scratch_shapes=[pltpu.VMEM((tm, tn), F32)] if nk > 1 else [],
        compiler_params=_cparams(("parallel", "parallel", "arbitrary")),
    )(a, b, *extra)
    return outs[0] if len(outs) == 1 else outs


def _rows(tm, n, col=0):
    return pl.BlockSpec((tm, n), lambda i: (i, col))


def _vec(n):
    return pl.BlockSpec((1, n), lambda i: (0, 0))


def _norm_mod(x, g, sc, sh, *, name, tm=256):
    t = x.shape[0]
    tm = min(tm, t)

    def body(x_ref, g_ref, sc_ref, sh_ref, h_ref):
        xv = x_ref[...]
        inv = lax.rsqrt(_mean(xv * xv) + EPS)
        h_ref[...] = ((xv * inv * g_ref[...]) * (1.0 + sc_ref[...]) + sh_ref[...]).astype(BF16)

    return pl.pallas_call(
        body, name=name, grid=(t // tm,), in_specs=[_rows(tm, D), _vec(D), _vec(D), _vec(D)],
        out_specs=_rows(tm, D), out_shape=jax.ShapeDtypeStruct((t, D), BF16),
        compiler_params=_cparams(("parallel",)))(x, g, sc, sh)


def _resid(x, y, gt, gp, *, name, tm=256):
    t = x.shape[0]
    tm = min(tm, t)

    def body(x_ref, y_ref, gt_ref, gp_ref, o_ref):
        yv = y_ref[...]
        inv = lax.rsqrt(_mean(yv * yv) + EPS)
        o_ref[...] = x_ref[...] + gt_ref[...] * (yv * inv * gp_ref[...])

    return pl.pallas_call(
        body, name=name, grid=(t // tm,), in_specs=[_rows(tm, D), _rows(tm, D), _vec(D), _vec(D)],
        out_specs=_rows(tm, D), out_shape=jax.ShapeDtypeStruct((t, D), F32),
        compiler_params=_cparams(("parallel",)))(x, y, gt, gp)


def _resid_bwd(dx, y, gt, gp, *, name, tm=256):
    t = dx.shape[0]
    tm = min(tm, t)

    def body(dx_ref, y_ref, gt_ref, gp_ref, dy_ref, dgt_ref, dgp_ref):
        @pl.when(pl.program_id(0) == 0)
        def _():
            dgt_ref[...] = jnp.zeros_like(dgt_ref)
            dgp_ref[...] = jnp.zeros_like(dgp_ref)

        dxv, yv, gp_v = dx_ref[...], y_ref[...], gp_ref[...]
        inv = lax.rsqrt(_mean(yv * yv) + EPS)
        yh = yv * inv
        dgt_ref[...] += _colsum(dxv * (yh * gp_v))
        dr = dxv * gt_ref[...]
        dgp_ref[...] += _colsum(dr * yh)
        dyn = dr * gp_v
        dy_ref[...] = (inv * (dyn - yh * _mean(dyn * yh))).astype(BF16)

    return pl.pallas_call(
        body, name=name, grid=(t // tm,), in_specs=[_rows(tm, D), _rows(tm, D), _vec(D), _vec(D)],
        out_specs=[_rows(tm, D), _vec(D), _vec(D)],
        out_shape=[jax.ShapeDtypeStruct((t, D), BF16), jax.ShapeDtypeStruct((1, D), F32),
                   jax.ShapeDtypeStruct((1, D), F32)],
        compiler_params=_cparams(("arbitrary",)))(dx, y, gt, gp)


def _norm_bwd(dh, dx_res, x, g, sc, *, name, tm=256):
    t = dh.shape[0]
    tm = min(tm, t)

    def body(dh_ref, dxr_ref, x_ref, g_ref, sc_ref, dx_ref, dg_ref, dsc_ref, dsh_ref):
        @pl.when(pl.program_id(0) == 0)
        def _():
            dg_ref[...] = jnp.zeros_like(dg_ref)
            dsc_ref[...] = jnp.zeros_like(dsc_ref)
            dsh_ref[...] = jnp.zeros_like(dsh_ref)

        dhv, xv, gv = dh_ref[...], x_ref[...], g_ref[...]
        inv = lax.rsqrt(_mean(xv * xv) + EPS)
        xh = xv * inv
        dsh_ref[...] += _colsum(dhv)
        dsc_ref[...] += _colsum(dhv * (xh * gv))
        dn = dhv * (1.0 + sc_ref[...])
        dg_ref[...] += _colsum(dn * xh)
        dxh = dn * gv
        dx_ref[...] = inv * (dxh - xh * _mean(dxh * xh)) + dxr_ref[...]

    vec_out = jax.ShapeDtypeStruct((1, D), F32)
    return pl.pallas_call(
        body, name=name, grid=(t // tm,), in_specs=[_rows(tm, D), _rows(tm, D), _rows(tm, D), _vec(D), _vec(D)],
        out_specs=[_rows(tm, D), _vec(D), _vec(D), _vec(D)],
        out_shape=[jax.ShapeDtypeStruct((t, D), F32), vec_out, vec_out, vec_out],
        compiler_params=_cparams(("arbitrary",)))(dh, dx_res, x, g, sc)


def _loss_and_grad(x, target, *, tm=256):
    t = x.shape[0]
    tm = min(tm, t)

    def body(x_ref, t_ref, loss_ref, dx_ref):
        @pl.when(pl.program_id(0) == 0)
        def _():
            loss_ref[...] = jnp.zeros_like(loss_ref)

        e = x_ref[...] - t_ref[...]
        dx_ref[...] = e * (1.0 / D)
        s = jnp.sum(jnp.sum(e * e, axis=1, keepdims=True), axis=0, keepdims=True) * (0.5 / D)
        loss_ref[...] += jnp.broadcast_to(s, loss_ref.shape)

    loss, dx = pl.pallas_call(
        body, name="loss", grid=(t // tm,), in_specs=[_rows(tm, D), _rows(tm, D)],
        out_specs=[pl.BlockSpec((8, LANE), lambda i: (0, 0)), _rows(tm, D)],
        out_shape=[jax.ShapeDtypeStruct((8, LANE), F32), jax.ShapeDtypeStruct((t, D), F32)],
        compiler_params=_cparams(("arbitrary",)))(x, target)
    return loss[0, 0], dx


def _gmlp_core(uv, lng, lnb, ws_ref, bsx):
    tm = uv.shape[0]
    gu = _gelu(uv[:, :DM])
    gv = _gelu(uv[:, DM:])
    mu = _mean(gv)
    vc = gv - mu
    rstd = lax.rsqrt(_mean(vc * vc) + EPS)
    vh = vc * rstd
    vln = vh * lng + lnb
    lane = lax.broadcasted_iota(I32, (CH, LANE), 1)
    sv_rows = []
    for nchunk in range(tm // CH):
        vb = vln[nchunk * CH:(nchunk + 1) * CH].astype(BF16)
        cols = []
        for cb in range(DM // LANE):
            vcb = vb[:, cb * LANE:(cb + 1) * LANE]
            lo = _dot(ws_ref[2 * cb], vcb)
            hi = _dot(ws_ref[2 * cb + 1], vcb)
            cols.append(jnp.where(lane < 64, lo, hi))
        sv_rows.append(jnp.concatenate(cols, axis=1) + bsx)
    sv = jnp.concatenate(sv_rows, axis=0) if len(sv_rows) > 1 else sv_rows[0]
    return gu, vh, rstd, vln, sv


def _gmlp_fwd(proj, lng, lnb, wsm, bsx, *, tm=256):
    t = proj.shape[0]
    tm = min(tm, t)

    def body(uv_ref, lng_ref, lnb_ref, ws_ref, bs_ref, ga_ref):
        gu, _, _, _, sv = _gmlp_core(uv_ref[...], lng_ref[...], lnb_ref[...], ws_ref, bs_ref[...])
        ga_ref[...] = (gu * sv).astype(BF16)

    return pl.pallas_call(
        body, name="gmlp_fwd", grid=(t // tm,),
        in_specs=[_rows(tm, 2 * DM, C_UV // (2 * DM)), _vec(DM), _vec(DM),
                  pl.BlockSpec((NG, CH, CH), lambda i: (0, 0, 0)), pl.BlockSpec((CH, DM), lambda i: (0, 0))],
        out_specs=_rows(tm, DM), out_shape=jax.ShapeDtypeStruct((t, DM), BF16),
        compiler_params=_cparams(("parallel",)))(proj, lng, lnb, wsm, bsx)


def _gmlp_bwd(dga, proj, lng, lnb, wsm, wsmt, bsx, *, tm=256):
    t = proj.shape[0]
    tm = min(tm, t)

    def body(dga_ref, uv_ref, lng_ref, lnb_ref, ws_ref, wst_ref, bs_ref, duv_ref, dws_ref, dbs_ref, dlng_ref, dlnb_ref,
             dbsx_ref):
        i = pl.program_id(0)

        @pl.when(i == 0)
        def _():
            dws_ref[...] = jnp.zeros_like(dws_ref)
            dbsx_ref[...] = jnp.zeros_like(dbsx_ref)
            dlng_ref[...] = jnp.zeros_like(dlng_ref)
            dlnb_ref[...] = jnp.zeros_like(dlnb_ref)

        uv = uv_ref[...]
        lng_v = lng_ref[...]
        gu, vh, rstd, vln, sv = _gmlp_core(uv, lng_v, lnb_ref[...], ws_ref, bs_ref[...])
        dga_v = dga_ref[...]
        dgu = dga_v * sv
        dsv = dga_v * gu
        lane = lax.broadcasted_iota(I32, (CH, LANE), 1)
        tril = lax.broadcasted_iota(I32, (CH, CH), 0) >= lax.broadcasted_iota(I32, (CH, CH), 1)
        dvln_rows = []
        for nchunk in range(tm // CH):
            rows = slice(nchunk * CH, (nchunk + 1) * CH)
            dbsx_ref[...] += dsv[rows]
            vb = vln[rows].astype(BF16)
            cols = []
            for cb in range(DM // LANE):
                cs = slice(cb * LANE, (cb + 1) * LANE)
                dsvb = dsv[rows, cs]
                vcb = vb[:, cs]
                dlo = jnp.where(lane < 64, dsvb, 0.0).astype(BF16)
                dhi = jnp.where(lane < 64, 0.0, dsvb).astype(BF16)
                dws_ref[2 * cb] += jnp.where(tril, _dot(dlo, vcb, NT), 0.0)
                dws_ref[2 * cb + 1] += jnp.where(tril, _dot(dhi, vcb, NT), 0.0)
                dsb = dsvb.astype(BF16)
                cols.append(jnp.where(lane < 64, _dot(wst_ref[2 * cb], dsb), _dot(wst_ref[2 * cb + 1], dsb)))
            dvln_rows.append(jnp.concatenate(cols, axis=1))
        dvln = jnp.concatenate(dvln_rows, axis=0) if len(dvln_rows) > 1 else dvln_rows[0]
        dlnb_ref[...] += _colsum(dvln)
        dlng_ref[...] += _colsum(dvln * vh)
        dvh = dvln * lng_v
        dgv = rstd * (dvh - _mean(dvh) - vh * _mean(dvh * vh))
        duv_ref[:, :DM] = (dgu * _gelu_grad(uv[:, :DM])).astype(BF16)
        duv_ref[:, DM:] = (dgv * _gelu_grad(uv[:, DM:])).astype(BF16)

        @pl.when(i == pl.num_programs(0) - 1)
        def _():
            ind = (lax.broadcasted_iota(I32, (DM, LANE), 0) // 64 == lax.broadcasted_iota(I32, (DM, LANE), 1)).astype(F32)
            dbs_ref[...] = jnp.dot(dbsx_ref[...], ind, preferred_element_type=F32, precision=lax.Precision.HIGHEST)

    vec_out = jax.ShapeDtypeStruct((1, DM), F32)
    outs = pl.pallas_call(
        body, name="gmlp_bwd", grid=(t // tm,),
        in_specs=[_rows(tm, DM), _rows(tm, 2 * DM, C_UV // (2 * DM)), _vec(DM), _vec(DM),
                  pl.BlockSpec((NG, CH, CH), lambda i: (0, 0, 0)), pl.BlockSpec((NG, CH, CH), lambda i: (0, 0, 0)),
                  pl.BlockSpec((CH, DM), lambda i: (0, 0))],
        out_specs=[_rows(tm, 2 * DM), pl.BlockSpec((NG, CH, CH), lambda i: (0, 0, 0)),
                   pl.BlockSpec((CH, LANE), lambda i: (0, 0)), _vec(DM), _vec(DM)],
        out_shape=[jax.ShapeDtypeStruct((t, 2 * DM), BF16), jax.ShapeDtypeStruct((NG, CH, CH), F32),
                   jax.ShapeDtypeStruct((CH, LANE), F32), vec_out, vec_out],
        scratch_shapes=[pltpu.VMEM((CH, DM), F32)],
        compiler_params=_cparams(("arbitrary",)))(dga, proj, lng, lnb, wsm, wsmt, bsx)
    return outs


def _glu_into(zs_ref, glu_ref, halo_ref, first):
    hal = halo_ref[...]
    z0h = hal[:, :DM] * _sigmoid(hal[:, DM:])
    zs_ref[0:HALO, :] = jnp.where(first, 0.0, z0h)
    g = glu_ref[...]
    zs_ref[HALO:, :] = g[:, :DM] * _sigmoid(g[:, DM:])


def _conv_fwd(proj, cw, cb, lng, lnb, *, tm=256, rb=64):
    t = proj.shape[0]
    tm = min(tm, t)
    hb = tm // HALO
    gcol = C_GLU // (2 * DM)

    def body(glu_ref, halo_ref, cw_ref, cb_ref, lng_ref, lnb_ref, zc_ref, zb_ref, zs_ref):
        i = pl.program_id(0)
        _glu_into(zs_ref, glu_ref, halo_ref, i == 0)
        for cbk in range(DM // LANE):
            cs = slice(cbk * LANE, (cbk + 1) * LANE)
            for r in range(tm // rb):
                acc = jnp.broadcast_to(cb_ref[:, cs], (rb, LANE))
                for k in range(KW):
                    off = r * rb + HALO - (KW - 1) + k
                    acc = acc + cw_ref[k:k + 1, cs] * zs_ref[off:off + rb, cs]
                zc_ref[r * rb:(r + 1) * rb, cs] = acc
        zc = zc_ref[...]
        mu = _mean(zc)
        zcc = zc - mu
        zh = zcc * lax.rsqrt(_mean(zcc * zcc) + EPS)
        a = zh * lng_ref[...] + lnb_ref[...]
        zb_ref[...] = (a * _sigmoid(a)).astype(BF16)

    return pl.pallas_call(
        body, name="conv_fwd", grid=(t // tm,),
        in_specs=[_rows(tm, 2 * DM, gcol),
                  pl.BlockSpec((HALO, 2 * DM), lambda i: (jnp.maximum(i * hb - 1, 0), gcol)),
                  pl.BlockSpec((KW, DM), lambda i: (0, 0)), _vec(DM), _vec(DM), _vec(DM)],
        out_specs=[_rows(tm, DM), _rows(tm, DM)],
        out_shape=[jax.ShapeDtypeStruct((t, DM), F32), jax.ShapeDtypeStruct((t, DM), BF16)],
        scratch_shapes=[pltpu.VMEM((HALO + tm, DM), F32)],
        compiler_params=_cparams(("parallel",)))(proj, proj, cw, cb, lng, lnb)


def _conv_bwd_ln(dzb, zc, lng, lnb, *, tm=256):
    t = zc.shape[0]
    tm = min(tm, t)

    def body(dzb_ref, zc_ref, lng_ref, lnb_ref, dzc_ref, dlng_ref, dlnb_ref):
        @pl.when(pl.program_id(0) == 0)
        def _():
            dlng_ref[...] = jnp.zeros_like(dlng_ref)
            dlnb_ref[...] = jnp.zeros_like(dlnb_ref)

        zc = zc_ref[...]
        lng_v = lng_ref[...]
        mu = _mean(zc)
        zcc = zc - mu
        rstd = lax.rsqrt(_mean(zcc * zcc) + EPS)
        zh = zcc * rstd
        a = zh * lng_v + lnb_ref[...]
        s = _sigmoid(a)
        da = dzb_ref[...] * (s * (1.0 + a * (1.0 - s)))
        dlnb_ref[...] += _colsum(da)
        dlng_ref[...] += _colsum(da * zh)
        dzh = da * lng_v
        dzc_ref[...] = rstd * (dzh - _mean(dzh) - zh * _mean(dzh * zh))

    vec_out = jax.ShapeDtypeStruct((1, DM), F32)
    return pl.pallas_call(
        body, name="conv_bwd_ln", grid=(t // tm,), in_specs=[_rows(tm, DM), _rows(tm, DM), _vec(DM), _vec(DM)],
        out_specs=[_rows(tm, DM), _vec(DM), _vec(DM)],
        out_shape=[jax.ShapeDtypeStruct((t, DM), F32), vec_out, vec_out],
        compiler_params=_cparams(("arbitrary",)))(dzb, zc, lng, lnb)


def _conv_bwd(dzc, proj, cw, *, tm=256, rb=64):
    t = proj.shape[0]
    tm = min(tm, t)
    hb = tm // HALO
    nblk = t // tm
    gcol = C_GLU // (2 * DM)

    def body(dzc_ref, dnext_ref, glu_ref, halo_ref, cw_ref, dglu_ref, dcw_ref, dcb_ref, zs_ref, ds_ref):
        i = pl.program_id(0)

        @pl.when(i == 0)
        def _():
            dcw_ref[...] = jnp.zeros_like(dcw_ref)
            dcb_ref[...] = jnp.zeros_like(dcb_ref)

        _glu_into(zs_ref, glu_ref, halo_ref, i == 0)
        dzc = dzc_ref[...]
        ds_ref[0:tm, :] = dzc
        ds_ref[tm:, :] = jnp.where(i == nblk - 1, 0.0, dnext_ref[...])
        dcb_ref[...] += _colsum(dzc)
        for k in range(KW):
            off = HALO - (KW - 1) + k
            dcw_ref[k:k + 1, :] += _colsum(dzc * zs_ref[off:off + tm, :])
        g = glu_ref[...]
        val, sg = g[:, :DM], _sigmoid(g[:, DM:])
        for cbk in range(DM // LANE):
            cs = slice(cbk * LANE, (cbk + 1) * LANE)
            for r in range(tm // rb):
                acc = jnp.zeros((rb, LANE), F32)
                for k in range(KW):
                    off = r * rb + (KW - 1) - k
                    acc = acc + cw_ref[k:k + 1, cs] * ds_ref[off:off + rb, cs]
                rs = slice(r * rb, (r + 1) * rb)
                dglu_ref[rs, cs] = (acc * sg[rs, cs]).astype(BF16)
                v, s = val[rs, cs], sg[rs, cs]
                dglu_ref[rs, DM + cbk * LANE:DM + (cbk + 1) * LANE] = (acc * v * s * (1.0 - s)).astype(BF16)

    return pl.pallas_call(
        body, name="conv_bwd", grid=(nblk,),
        in_specs=[_rows(tm, DM),
                  pl.BlockSpec((HALO, DM), lambda i: (jnp.minimum((i + 1) * hb, nblk * hb - 1), 0)),
                  _rows(tm, 2 * DM, gcol),
                  pl.BlockSpec((HALO, 2 * DM), lambda i: (jnp.maximum(i * hb - 1, 0), gcol)),
                  pl.BlockSpec((KW, DM), lambda i: (0, 0))],
        out_specs=[_rows(tm, 2 * DM), pl.BlockSpec((HALO, DM), lambda i: (0, 0)), _vec(DM)],
        out_shape=[jax.ShapeDtypeStruct((t, 2 * DM), BF16), jax.ShapeDtypeStruct((HALO, DM), F32),
                   jax.ShapeDtypeStruct((1, DM), F32)],
        scratch_shapes=[pltpu.VMEM((HALO + tm, DM), F32), pltpu.VMEM((tm + HALO, DM), F32)],
        compiler_params=_cparams(("arbitrary",)))(dzc, dzc, proj, proj, cw)


def _log_sigmoid(x):
    return jnp.minimum(x, 0.0) - jnp.log1p(jnp.exp(-jnp.abs(x)))


def _fox_cum(proj, bfp):
    t = proj.shape[0]
    fcol = C_F // LANE

    def body(f_ref, bf_ref, cum_ref, carry_ref):
        @pl.when(pl.program_id(0) == 0)
        def _():
            carry_ref[...] = jnp.zeros_like(carry_ref)

        lf = _log_sigmoid(f_ref[...] + bf_ref[...])
        tri = (lax.broadcasted_iota(I32, (CH, CH), 0) >= lax.broadcasted_iota(I32, (CH, CH), 1)).astype(F32)
        cum = jnp.dot(tri, lf, preferred_element_type=F32, precision=lax.Precision.HIGHEST) + carry_ref[0:1, :]
        cum_ref[...] = cum
        carry_ref[...] = jnp.broadcast_to(cum[CH - 1:CH, :], carry_ref.shape)

    return pl.pallas_call(
        body, name="fox_cum", grid=(t // CH,), in_specs=[_rows(CH, LANE, fcol), _vec(LANE)],
        out_specs=_rows(CH, LANE), out_shape=jax.ShapeDtypeStruct((t, LANE), F32),
        scratch_shapes=[pltpu.VMEM((8, LANE), F32)],
        compiler_params=_cparams(("arbitrary",)))(proj, bfp)


def _fox_cum_bwd(dcum, proj, bfp):
    t = proj.shape[0]
    nb = t // CH
    fcol = C_F // LANE
    fw = D_INP - C_F

    def body(dc_ref, f_ref, bf_ref, df_ref, dbf_ref, carry_ref):
        @pl.when(pl.program_id(0) == 0)
        def _():
            carry_ref[...] = jnp.zeros_like(carry_ref)
            dbf_ref[...] = jnp.zeros_like(dbf_ref)

        triu = (lax.broadcasted_iota(I32, (CH, CH), 0) <= lax.broadcasted_iota(I32, (CH, CH), 1)).astype(F32)
        dlf = jnp.dot(triu, dc_ref[...], preferred_element_type=F32, precision=lax.Precision.HIGHEST) + carry_ref[0:1, :]
        carry_ref[...] = jnp.broadcast_to(dlf[0:1, :], carry_ref.shape)
        z = f_ref[...] + bf_ref[...]
        lane = lax.broadcasted_iota(I32, (CH, LANE), 1)
        df = jnp.where(lane < NG, dlf * _sigmoid(-z), 0.0)
        dbf_ref[...] += _colsum(df)
        df_ref[:, 0:LANE] = df.astype(BF16)
        df_ref[:, LANE:] = jnp.zeros((CH, fw - LANE), BF16)

    return pl.pallas_call(
        body, name="fox_cum_bwd", grid=(nb,),
        in_specs=[pl.BlockSpec((CH, LANE), lambda i: (nb - 1 - i, 0)),
                  pl.BlockSpec((CH, LANE), lambda i: (nb - 1 - i, fcol)), _vec(LANE)],
        out_specs=[pl.BlockSpec((CH, fw), lambda i: (nb - 1 - i, 0)), _vec(LANE)],
        out_shape=[jax.ShapeDtypeStruct((t, fw), BF16), jax.ShapeDtypeStruct((1, LANE), F32)],
        scratch_shapes=[pltpu.VMEM((8, LANE), F32)],
        compiler_params=_cparams(("arbitrary",)))(dcum, proj, bfp)


HD = 64
ATT_SCALE = 0.125
NEG = -1e30


def _qkv_prep(proj, *, tm=512):
    t = proj.shape[0]
    tm = min(tm, t)

    def body(q_ref, k_ref, v_ref, o_ref):
        o_ref[:, 0:DM] = (q_ref[...] * ATT_SCALE).astype(BF16)
        o_ref[:, DM:2 * DM] = k_ref[...].astype(BF16)
        o_ref[:, 2 * DM:] = v_ref[...].astype(BF16)

    return pl.pallas_call(
        body, name="qkv_prep", grid=(t // tm,),
        in_specs=[_rows(tm, DM, C_Q // DM), _rows(tm, DM, C_K // DM), _rows(tm, DM, C_V // DM)],
        out_specs=_rows(tm, 3 * DM), out_shape=jax.ShapeDtypeStruct((t, 3 * DM), BF16),
        compiler_params=_cparams(("parallel",)))(proj, proj, proj)


def _causal_pairs(nq, outer_is_query):
    if outer_is_query:
        pairs = [(i, j) for i in range(nq) for j in range(i + 1)]
    else:
        pairs = [(j, i) for j in range(nq) for i in range(j, nq)]
    return (jnp.asarray([p[0] for p in pairs], I32), jnp.asarray([p[1] for p in pairs], I32))


def _to_row(col):
    return jnp.transpose(col)[0:1, :]


def _rep(x, tk):
    return x if tk == LANE else jnp.tile(x, (1, tk // LANE))


def _attn_fwd(qkv, ckrow, *, tq=512):
    t = qkv.shape[0]
    tq = min(tq, t)
    tk = tq
    nq = t // tq
    oi, ij = _causal_pairs(nq, True)

    def body(oi_ref, ij_ref, q_ref, k_ref, v_ref, ck_ref, o_ref, lse_ref, lser_ref, m_ref, l_ref, acc_ref):
        n = pl.program_id(0)
        i, j = oi_ref[n], ij_ref[n]

        @pl.when(j == 0)
        def _():
            m_ref[...] = jnp.full_like(m_ref, NEG)
            l_ref[...] = jnp.zeros_like(l_ref)
            acc_ref[...] = jnp.zeros_like(acc_ref)

        def step(masked):
            if masked:
                keep = lax.broadcasted_iota(I32, (tq, tk), 1) <= lax.broadcasted_iota(I32, (tq, tk), 0)
            lo = lax.broadcasted_iota(I32, (tq, LANE), 1) < HD
            for hp in range(NG // 2):
                cs = slice(hp * LANE, (hp + 1) * LANE)
                qp, kp, vp = q_ref[:, cs], k_ref[:, cs], v_ref[:, cs]
                alphas, pvs = [], []
                for hh in range(2):
                    h = 2 * hp + hh
                    qm = jnp.where(lo if hh == 0 else jnp.logical_not(lo), qp, jnp.zeros_like(qp))
                    s = _dot(qm, kp, NT) - ck_ref[h:h + 1, :]
                    if masked:
                        s = jnp.where(keep, s, NEG)
                    m_prev = m_ref[h]
                    m_new = jnp.maximum(m_prev, jnp.max(s, axis=1, keepdims=True))
                    alpha = jnp.exp(m_prev - m_new)
                    p = jnp.exp(s - _rep(m_new, tk))
                    l_ref[h] = alpha * l_ref[h] + jnp.sum(p, axis=1, keepdims=True)
                    m_ref[h] = m_new
                    alphas.append(alpha)
                    pvs.append(_dot(p.astype(BF16), vp))
                acc_ref[:, cs] = jnp.where(lo, alphas[0], alphas[1]) * acc_ref[:, cs] + jnp.where(lo, pvs[0], pvs[1])

        @pl.when(j < i)
        def _():
            step(False)

        @pl.when(j == i)
        def _():
            step(True)
            lo = lax.broadcasted_iota(I32, (tq, LANE), 1) < HD
            for hp in range(NG // 2):
                cs = slice(hp * LANE, (hp + 1) * LANE)
                o_ref[:, cs] = (acc_ref[:, cs] / jnp.where(lo, l_ref[2 * hp], l_ref[2 * hp + 1])).astype(BF16)
            for h in range(NG):
                lse = m_ref[h] + jnp.log(l_ref[h])
                lse_ref[h] = lse
                lser_ref[h:h + 1, :] = _to_row(lse)

    gs = pltpu.PrefetchScalarGridSpec(
        num_scalar_prefetch=2, grid=(int(oi.shape[0]),),
        in_specs=[pl.BlockSpec((tq, DM), lambda n, a, b: (a[n], 0)),
                  pl.BlockSpec((tk, DM), lambda n, a, b: (b[n], 1)),
                  pl.BlockSpec((tk, DM), lambda n, a, b: (b[n], 2)),
                  pl.BlockSpec((NG, tk), lambda n, a, b: (0, b[n]))],
        out_specs=[pl.BlockSpec((tq, DM), lambda n, a, b: (a[n], 0)),
                   pl.BlockSpec((NG, tq, LANE), lambda n, a, b: (0, a[n], 0)),
                   pl.BlockSpec((NG, tq), lambda n, a, b: (0, a[n]))],
        scratch_shapes=[pltpu.VMEM((NG, tq, LANE), F32), pltpu.VMEM((NG, tq, LANE), F32), pltpu.VMEM((tq, DM), F32)])
    return pl.pallas_call(
        body, name="attn_fwd", grid_spec=gs,
        out_shape=[jax.ShapeDtypeStruct((t, DM), BF16), jax.ShapeDtypeStruct((NG, t, LANE), F32),
                   jax.ShapeDtypeStruct((NG, t), F32)],
        compiler_params=_cparams(("arbitrary",)))(oi, ij, qkv, qkv, qkv, ckrow)


def _attn_bwd_dq(qkv, ckrow, o, do, lse, *, tq=512):
    t = qkv.shape[0]
    tq = min(tq, t)
    tk = tq
    nq = t // tq
    oi, ij = _causal_pairs(nq, True)

    def body(oi_ref, ij_ref, q_ref, k_ref, v_ref, ck_ref, o_ref, do_ref, lse_ref, dq_ref, deltar_ref, dcqr_ref, delta_ref,
             dcq_ref, acc_ref):
        n = pl.program_id(0)
        i, j = oi_ref[n], ij_ref[n]

        @pl.when(j == 0)
        def _():
            acc_ref[...] = jnp.zeros_like(acc_ref)
            dcq_ref[...] = jnp.zeros_like(dcq_ref)
            lo = lax.broadcasted_iota(I32, (tq, LANE), 1) < HD
            for hp in range(NG // 2):
                cs = slice(hp * LANE, (hp + 1) * LANE)
                prod = do_ref[:, cs] * o_ref[:, cs].astype(F32)
                for hh in range(2):
                    d = jnp.sum(jnp.where(lo if hh == 0 else jnp.logical_not(lo), prod, 0.0), axis=1, keepdims=True)
                    dcol = jnp.broadcast_to(d, (tq, LANE))
                    delta_ref[2 * hp + hh] = dcol
                    deltar_ref[2 * hp + hh:2 * hp + hh + 1, :] = _to_row(dcol)

        def step(masked):
            if masked:
                keep = lax.broadcasted_iota(I32, (tq, tk), 1) <= lax.broadcasted_iota(I32, (tq, tk), 0)
            lo = lax.broadcasted_iota(I32, (tq, LANE), 1) < HD
            for hp in range(NG // 2):
                cs = slice(hp * LANE, (hp + 1) * LANE)
                qp, kp, vp, dop = q_ref[:, cs], k_ref[:, cs], v_ref[:, cs], do_ref[:, cs].astype(BF16)
                parts = []
                for hh in range(2):
                    h = 2 * hp + hh
                    sel = lo if hh == 0 else jnp.logical_not(lo)
                    s = _dot(jnp.where(sel, qp, jnp.zeros_like(qp)), kp, NT) - ck_ref[h:h + 1, :]
                    if masked:
                        s = jnp.where(keep, s, NEG)
                    p = jnp.exp(s - _rep(lse_ref[h], tk))
                    dp = _dot(jnp.where(sel, dop, jnp.zeros_like(dop)), vp, NT)
                    ds = p * (dp - _rep(delta_ref[h], tk))
                    dcq_ref[h] += jnp.sum(ds, axis=1, keepdims=True)
                    parts.append(_dot(ds.astype(BF16), kp))
                acc_ref[:, cs] += jnp.where(lo, parts[0], parts[1])

        @pl.when(j < i)
        def _():
            step(False)

        @pl.when(j == i)
        def _():
            step(True)
            dq_ref[...] = (acc_ref[...] * ATT_SCALE).astype(BF16)
            for h in range(NG):
                dcqr_ref[h:h + 1, :] = _to_row(dcq_ref[h])

    gs = pltpu.PrefetchScalarGridSpec(
        num_scalar_prefetch=2, grid=(int(oi.shape[0]),),
        in_specs=[pl.BlockSpec((tq, DM), lambda n, a, b: (a[n], 0)),
                  pl.BlockSpec((tk, DM), lambda n, a, b: (b[n], 1)),
                  pl.BlockSpec((tk, DM), lambda n, a, b: (b[n], 2)),
                  pl.BlockSpec((NG, tk), lambda n, a, b: (0, b[n])),
                  pl.BlockSpec((tq, DM), lambda n, a, b: (a[n], 0)),
                  pl.BlockSpec((tq, DM), lambda n, a, b: (a[n], 0)),
                  pl.BlockSpec((NG, tq, LANE), lambda n, a, b: (0, a[n], 0))],
        out_specs=[pl.BlockSpec((tq, DM), lambda n, a, b: (a[n], 0)),
                   pl.BlockSpec((NG, tq), lambda n, a, b: (0, a[n])),
                   pl.BlockSpec((NG, tq), lambda n, a, b: (0, a[n]))],
        scratch_shapes=[pltpu.VMEM((NG, tq, LANE), F32), pltpu.VMEM((NG, tq, LANE), F32), pltpu.VMEM((tq, DM), F32)])
    return pl.pallas_call(
        body, name="attn_bwd_dq", grid_spec=gs,
        out_shape=[jax.ShapeDtypeStruct((t, DM), BF16), jax.ShapeDtypeStruct((NG, t), F32), jax.ShapeDtypeStruct((NG, t), F32)],
        compiler_params=_cparams(("arbitrary",)))(oi, ij, qkv, qkv, qkv, ckrow, o, do, lse)


def _attn_bwd_dkv(qkv, ckcol, do, lserow, deltarow, *, tq=512):
    t = qkv.shape[0]
    tq = min(tq, t)
    tk = tq
    nq = t // tq
    oj, ii = _causal_pairs(nq, False)

    def body(oj_ref, ii_ref, q_ref, k_ref, v_ref, ck_ref, do_ref, lse_ref, delta_ref, dk_ref, dv_ref, dckr_ref, dka_ref,
             dva_ref, dck_ref):
        n = pl.program_id(0)
        j, i = oj_ref[n], ii_ref[n]

        @pl.when(i == j)
        def _():
            dka_ref[...] = jnp.zeros_like(dka_ref)
            dva_ref[...] = jnp.zeros_like(dva_ref)
            dck_ref[...] = jnp.zeros_like(dck_ref)

        def step(masked):
            if masked:
                keep = lax.broadcasted_iota(I32, (tk, tq), 0) <= lax.broadcasted_iota(I32, (tk, tq), 1)
            lo = lax.broadcasted_iota(I32, (tk, LANE), 1) < HD
            for hp in range(NG // 2):
                cs = slice(hp * LANE, (hp + 1) * LANE)
                qp, kp, vp, dop = q_ref[:, cs], k_ref[:, cs], v_ref[:, cs], do_ref[:, cs].astype(BF16)
                dvs, dks = [], []
                for hh in range(2):
                    h = 2 * hp + hh
                    sel = lo if hh == 0 else jnp.logical_not(lo)
                    st = _dot(jnp.where(sel, kp, jnp.zeros_like(kp)), qp, NT) - _rep(ck_ref[h], tq)
                    if masked:
                        st = jnp.where(keep, st, NEG)
                    pt = jnp.exp(st - lse_ref[h:h + 1, :])
                    dvs.append(_dot(pt.astype(BF16), dop))
                    dpt = _dot(jnp.where(sel, vp, jnp.zeros_like(vp)), dop, NT)
                    dst = pt * (dpt - delta_ref[h:h + 1, :])
                    dks.append(_dot(dst.astype(BF16), qp))
                    dck_ref[h] -= jnp.sum(dst, axis=1, keepdims=True)
                dva_ref[:, cs] += jnp.where(lo, dvs[0], dvs[1])
                dka_ref[:, cs] += jnp.where(lo, dks[0], dks[1])

        @pl.when(i == j)
        def _():
            step(True)

        @pl.when(i > j)
        def _():
            step(False)

        @pl.when(i == nq - 1)
        def _():
            dk_ref[...] = dka_ref[...].astype(BF16)
            dv_ref[...] = dva_ref[...].astype(BF16)
            for h in range(NG):
                dckr_ref[h:h + 1, :] = _to_row(dck_ref[h])

    gs = pltpu.PrefetchScalarGridSpec(
        num_scalar_prefetch=2, grid=(int(oj.shape[0]),),
        in_specs=[pl.BlockSpec((tq, DM), lambda n, a, b: (b[n], 0)),
                  pl.BlockSpec((tk, DM), lambda n, a, b: (a[n], 1)),
                  pl.BlockSpec((tk, DM), lambda n, a, b: (a[n], 2)),
                  pl.BlockSpec((NG, tk, LANE), lambda n, a, b: (0, a[n], 0)),
                  pl.BlockSpec((tq, DM), lambda n, a, b: (b[n], 0)),
                  pl.BlockSpec((NG, tq), lambda n, a, b: (0, b[n])),
                  pl.BlockSpec((NG, tq), lambda n, a, b: (0, b[n]))],
        out_specs=[pl.BlockSpec((tk, DM), lambda n, a, b: (a[n], 0)),
                   pl.BlockSpec((tk, DM), lambda n, a, b: (a[n], 0)),
                   pl.BlockSpec((NG, tk), lambda n, a, b: (0, a[n]))],
        scratch_shapes=[pltpu.VMEM((tk, DM), F32), pltpu.VMEM((tk, DM), F32), pltpu.VMEM((NG, tk, LANE), F32)])
    return pl.pallas_call(
        body, name="attn_bwd_dkv", grid_spec=gs,
        out_shape=[jax.ShapeDtypeStruct((t, DM), BF16), jax.ShapeDtypeStruct((t, DM), BF16), jax.ShapeDtypeStruct((NG, t), F32)],
        compiler_params=_cparams(("arbitrary",)))(oj, ii, qkv, qkv, qkv, ckcol, do, lserow, deltarow)


def _merge_fwd(ga, zb, att, proj, wa, wb, wc, *, tm=256):
    t = ga.shape[0]
    tm = min(tm, t)
    wspec = pl.BlockSpec((DM, D), lambda i: (0, 0))

    def body(ga_ref, zb_ref, att_ref, gate_ref, wa_ref, wb_ref, wc_ref, m_ref):
        acc = jnp.zeros((tm, D), F32)
        for b, (x_ref, w_ref) in enumerate(((ga_ref, wa_ref), (zb_ref, wb_ref), (att_ref, wc_ref))):
            acc = acc + _sigmoid(gate_ref[:, b * D:(b + 1) * D]) * _dot(x_ref[...], w_ref[...])
        m_ref[...] = acc.astype(BF16)

    return pl.pallas_call(
        body, name="merge_fwd", grid=(t // tm,),
        in_specs=[_rows(tm, DM), _rows(tm, DM), _rows(tm, DM), _rows(tm, 3 * D, 0), wspec, wspec, wspec],
        out_specs=_rows(tm, D), out_shape=jax.ShapeDtypeStruct((t, D), BF16),
        compiler_params=_cparams(("parallel",)))(ga, zb, att, proj, wa, wb, wc)


def _merge_bwd(dm, ga, zb, att, proj, wa, wb, wc, *, tm=256):
    t = ga.shape[0]
    tm = min(tm, t)
    wspec = pl.BlockSpec((DM, D), lambda i: (0, 0))

    def body(dm_ref, ga_ref, zb_ref, att_ref, gate_ref, wa_ref, wb_ref, wc_ref, dgate_ref, dga_ref, dzb_ref, datt_ref,
             dwa_ref, dwb_ref, dwc_ref):
        @pl.when(pl.program_id(0) == 0)
        def _():
            dwa_ref[...] = jnp.zeros_like(dwa_ref)
            dwb_ref[...] = jnp.zeros_like(dwb_ref)
            dwc_ref[...] = jnp.zeros_like(dwc_ref)

        dmv = dm_ref[...]
        branches = ((ga_ref, wa_ref, dga_ref, dwa_ref), (zb_ref, wb_ref, dzb_ref, dwb_ref),
                    (att_ref, wc_ref, datt_ref, dwc_ref))
        for b, (x_ref, w_ref, dx_ref, dw_ref) in enumerate(branches):
            xv, wv = x_ref[...], w_ref[...]
            y = _dot(xv, wv)
            g = _sigmoid(gate_ref[:, b * D:(b + 1) * D])
            dgate_ref[:, b * D:(b + 1) * D] = (dmv * y * g * (1.0 - g)).astype(BF16)
            dy = (dmv * g).astype(BF16)
            dx_ref[...] = _dot(dy, wv, NT)
            dw_ref[...] += _dot(xv, dy, TN)

    return pl.pallas_call(
        body, name="merge_bwd", grid=(t // tm,),
        in_specs=[_rows(tm, D), _rows(tm, DM), _rows(tm, DM), _rows(tm, DM), _rows(tm, 3 * D, 0), wspec, wspec, wspec],
        out_specs=[_rows(tm, 3 * D), _rows(tm, DM), _rows(tm, DM), _rows(tm, DM), wspec, wspec, wspec],
        out_shape=[jax.ShapeDtypeStruct((t, 3 * D), BF16)] + [jax.ShapeDtypeStruct((t, DM), F32)] * 3
        + [jax.ShapeDtypeStruct((DM, D), F32)] * 3,
        compiler_params=_cparams(("arbitrary",)))(dm, ga, zb, att, proj, wa, wb, wc)


def _heads_layout(cum):
    t = cum.shape[0]
    ckrow = cum[:, :NG].T
    return ckrow, jnp.broadcast_to(ckrow[:, :, None], (NG, t, LANE))


def _layer_fwd(x, mod, w):
    sh1, sc1, gt1, sh2, sc2, gt2 = (mod[k:k + 1] for k in range(NMOD))
    h1 = _norm_mod(x, w["mix_pre_g"], sc1, sh1, name="norm_mix")
    proj = _matmul(h1, w["w_in_p"], name="mm_proj")
    ga = _gmlp_fwd(proj, w["gmlp_ln_g"], w["gmlp_ln_b"], w["wsm"], w["bsx"])
    zc, zb = _conv_fwd(proj, w["conv_w"], w["conv_b"], w["conv_ln_g"], w["conv_ln_b"])
    cum = _fox_cum(proj, w["bfp"])
    ckrow, ckcol = _heads_layout(cum)
    qkv = _qkv_prep(proj)
    att, lse, lser = _attn_fwd(qkv, ckrow)
    merged = _merge_fwd(ga, zb, att, proj, w["w_a_out"], w["w_b_out"], w["w_c_out"])
    y1 = _matmul(merged, w["w_out"], name="mm_out")
    x2 = _resid(x, y1, gt1, w["mix_post_g"], name="resid_mix")
    h2 = _norm_mod(x2, w["mlp_pre_g"], sc2, sh2, name="norm_mlp")
    a, hid = _matmul(h2, w["mlp_w1"], name="mm_w1", b_slabs=True, extra_out=(BF16,),
                     epilogue=lambda acc: (acc, jnp.square(jnp.maximum(acc, 0.0))))
    y2 = _matmul(hid, w["mlp_w2"], name="mm_w2")
    x3 = _resid(x2, y2, gt2, w["mlp_post_g"], name="resid_mlp")
    saved = dict(x=x, h1=h1, proj=proj, ga=ga, zc=zc, zb=zb, qkv=qkv, ckrow=ckrow, ckcol=ckcol, att=att, lse=lse, lser=lser, merged=merged,
                 y1=y1, x2=x2, h2=h2, a=a, hid=hid, y2=y2)
    return x3, saved


def _layer_bwd(dx3, mod, w, s):
    sh1, sc1, gt1, sh2, sc2, gt2 = (mod[k:k + 1] for k in range(NMOD))
    g = {}
    dy2, dgt2, g["mlp_post_g"] = _resid_bwd(dx3, s["y2"], gt2, w["mlp_post_g"], name="resid_mlp_bwd")
    da = _matmul(dy2, w["mlp_w2"], tb=True, name="mm_dhid", out_dtype=BF16, extra=(s["a"],),
                 epilogue=lambda acc, a: (acc * (2.0 * jnp.maximum(a, 0.0)),))
    g["mlp_w2"] = _matmul(s["hid"], dy2, ta=True, name="mm_dw2")
    g["mlp_w1"] = _matmul(s["h2"], da, ta=True, name="mm_dw1", out_slabs=N_CHIPS)
    dh2 = _matmul(da, w["mlp_w1"], tb=True, name="mm_dh2", b_slabs=True)
    dx2, g["mlp_pre_g"], dsc2, dsh2 = _norm_bwd(dh2, dx3, s["x2"], w["mlp_pre_g"], sc2, name="norm_mlp_bwd")
    dy1, dgt1, g["mix_post_g"] = _resid_bwd(dx2, s["y1"], gt1, w["mix_post_g"], name="resid_mix_bwd")
    dmerged = _matmul(dy1, w["w_out"], tb=True, name="mm_dmerged")
    g["w_out"] = _matmul(s["merged"], dy1, ta=True, name="mm_dwout")
    dgate, dga, dzb, datt, g["w_a_out"], g["w_b_out"], g["w_c_out"] = _merge_bwd(
        dmerged, s["ga"], s["zb"], s["att"], s["proj"], w["w_a_out"], w["w_b_out"], w["w_c_out"])
    duv, g["gmlp_ws"], dbs, g["gmlp_ln_g"], g["gmlp_ln_b"] = _gmlp_bwd(
        dga, s["proj"], w["gmlp_ln_g"], w["gmlp_ln_b"], w["wsm"], w["wsmt"], w["bsx"])
    g["gmlp_bs"] = dbs[:, :NG].T
    dzc, g["conv_ln_g"], g["conv_ln_b"] = _conv_bwd_ln(dzb, s["zc"], w["conv_ln_g"], w["conv_ln_b"])
    dglu, dcw, g["conv_b"] = _conv_bwd(dzc, s["proj"], w["conv_w"])
    g["conv_w"] = dcw[:KW]
    dq, delta, dcq = _attn_bwd_dq(s["qkv"], s["ckrow"], s["att"], datt, s["lse"])
    dk, dv, dck = _attn_bwd_dkv(s["qkv"], s["ckcol"], datt, s["lser"], delta)
    dcum = jnp.pad((dcq + dck).T, ((0, 0), (0, LANE - NG)))
    df, dbf = _fox_cum_bwd(dcum, s["proj"], w["bfp"])
    g["fox_bf"] = dbf[0, :NG]
    dproj = jnp.concatenate([dgate, duv, dglu, dq, dk, dv, df], axis=1)
    g["w_in_p"] = _matmul(s["h1"], dproj, ta=True, name="mm_dwin")
    dh1 = _matmul(dproj, w["w_in_p"], tb=True, name="mm_dh1")
    dx, g["mix_pre_g"], dsc1, dsh1 = _norm_bwd(dh1, dx2, s["x"], w["mix_pre_g"], sc1, name="norm_mix_bwd")
    dmod = jnp.concatenate([dsh1, dsc1, dgt1, dsh2, dsc2, dgt2], axis=0)
    return dx, g, dmod


def _position():
    return lax.axis_index("x"), lax.axis_index("y"), lax.axis_index("c")


def _all_gather8(v):
    m_per, n = v.shape

    def body(x_ref, out_ref, send_sems, recv_sems, local_sem):
        x, y, c = _position()
        me, sibling = (x, y, c), (x, y, 1 - c)
        chips = [(1 - x, y), (x, 1 - y), (1 - x, 1 - y)]

        def rows(px, py, pc):
            return out_ref.at[pl.ds((4 * px + 2 * py + pc) * m_per, m_per), :]

        def copy(k, block, to, src=None):
            return pltpu.make_async_remote_copy(
                src_ref=rows(*block) if src is None else src, dst_ref=rows(*block), send_sem=send_sems.at[k],
                recv_sem=recv_sems.at[k], device_id=to, device_id_type=MESH)

        mine = pltpu.make_async_copy(x_ref, rows(*me), local_sem)
        mine.start()
        first = [copy(0, me, sibling, src=x_ref)]
        first += [copy(1 + j, me, (*chip, c), src=x_ref) for j, chip in enumerate(chips)]
        for cp in first:
            cp.start()
        passed = [copy(4 + j, (*chip, c), sibling) for j, chip in enumerate(chips)]
        for j, chip in enumerate(chips):
            copy(1 + j, (*chip, c), me).wait_recv()
            passed[j].start()
        copy(0, sibling, me).wait_recv()
        for j, chip in enumerate(chips):
            copy(4 + j, (*chip, 1 - c), me).wait_recv()
        for cp in first + passed:
            cp.wait_send()
        mine.wait()

    out = pl.pallas_call(
        body, name="all_gather8", out_shape=jax.ShapeDtypeStruct((N_DEV * m_per, n), v.dtype),
        in_specs=[pl.BlockSpec(memory_space=pltpu.VMEM)], out_specs=pl.BlockSpec(memory_space=pltpu.VMEM),
        scratch_shapes=[pltpu.SemaphoreType.DMA((7,)), pltpu.SemaphoreType.DMA((7,)), pltpu.SemaphoreType.DMA],
        compiler_params=pltpu.CompilerParams(vmem_limit_bytes=VMEM_LIMIT),
    )(v)
    return out.reshape(N_DEV, m_per, n)


def _half(c, rows):
    return pl.ds(c * (rows // 2), rows // 2)


def _ag_ici(shards, *, name):
    nw = len(shards)

    def body(*refs):
        w_refs, out_refs = refs[:nw], refs[nw:2 * nw]
        send_sems, recv_sems = refs[2 * nw:]
        x, y, c = _position()
        chips = [(1 - x, y), (x, 1 - y), (1 - x, 1 - y)]
        def copy(wi, k, slab):
            rows = w_refs[wi].shape[0]
            px, py = chips[k]
            return pltpu.make_async_remote_copy(
                src_ref=w_refs[wi].at[_half(c, rows)], dst_ref=out_refs[wi].at[slab, _half(c, rows)],
                send_sem=send_sems.at[wi * 3 + k], recv_sem=recv_sems.at[wi * 3 + k], device_id=(px, py, c),
                device_id_type=MESH)

        sends = [copy(wi, k, 2 * x + y) for wi in range(nw) for k in range(3)]
        for cp in sends:
            cp.start()
        for wi in range(nw):
            for k, (px, py) in enumerate(chips):
                copy(wi, k, 2 * px + py).wait_recv()
        for cp in sends:
            cp.wait_send()

    return pl.pallas_call(
        body, name=name, out_shape=[jax.ShapeDtypeStruct((N_CHIPS,) + sh.shape, sh.dtype) for sh in shards],
        in_specs=[ANY] * nw, out_specs=[ANY] * nw,
        scratch_shapes=[pltpu.SemaphoreType.DMA((3 * nw,)), pltpu.SemaphoreType.DMA((3 * nw,))],
    )(*shards)


HBM = pl.BlockSpec(memory_space=pltpu.HBM)
SEM = pl.BlockSpec(memory_space=pltpu.SEMAPHORE)
EFFECT = pltpu.SideEffectType.DATAFLOW_SIDE_EFFECTING


def _ici_copies(kind, src_refs, land_refs, send_sems, recv_sems):
    x, y, c = _position()
    chips = [(1 - x, y), (x, 1 - y), (1 - x, 1 - y)]
    sends, recvs = [], []
    for wi, (src, land) in enumerate(zip(src_refs, land_refs)):
        for k, (px, py) in enumerate(chips):
            if kind == "gather":
                rows = src.shape[0]
                s_win = src.at[_half(c, rows)]
                there, here = land.at[2 * x + y, _half(c, rows)], land.at[2 * px + py, _half(c, rows)]
            else:
                s_win = src.at[2 * px + py]
                there = here = land.at[k]
            for dst, out in ((there, sends), (here, recvs)):
                out.append(pltpu.make_async_remote_copy(
                    src_ref=s_win, dst_ref=dst, send_sem=send_sems.at[wi * 3 + k], recv_sem=recv_sems.at[wi * 3 + k],
                    device_id=(px, py, c), device_id_type=MESH))
    return sends, recvs


def _ici_start(kind, srcs, land_shapes, *, name):
    nw = len(srcs)

    def body(*refs):
        src_refs, land_refs = refs[:nw], refs[nw:2 * nw]
        send_sems, recv_sems = refs[2 * nw:2 * nw + 2]
        token = refs[-1]
        sends, _ = _ici_copies(kind, src_refs, land_refs, send_sems, recv_sems)
        for cp in sends:
            cp.start()
        token[...] = jnp.zeros_like(token)

    lands = [pltpu.with_memory_space_constraint(lax.empty(shp, s.dtype), pltpu.HBM) for shp, s in zip(land_shapes, srcs)]
    outs = pl.pallas_call(
        body, name=name,
        out_shape=(pltpu.SemaphoreType.DMA((3 * nw,)), pltpu.SemaphoreType.DMA((3 * nw,)),
                   *[pltpu.HBM(s.shape, s.dtype) for s in srcs], *[pltpu.HBM(shp, s.dtype) for shp, s in zip(land_shapes, srcs)],
                   jax.ShapeDtypeStruct((8, LANE), F32)),
        in_specs=[HBM] * (2 * nw), out_specs=(SEM, SEM, *[HBM] * (2 * nw), pl.BlockSpec(memory_space=pltpu.VMEM)),
        input_output_aliases={i: 2 + i for i in range(2 * nw)},
        compiler_params=pltpu.CompilerParams(has_side_effects=EFFECT),
    )(*[pltpu.with_memory_space_constraint(s, pltpu.HBM) for s in srcs], *lands)
    return outs[0], outs[1], outs[2:2 + nw], outs[2 + nw:2 + 2 * nw], outs[-1]


def _ici_wait(kind, send_sems, recv_sems, srcs, lands, after, *, name):
    nw = len(srcs)

    def body(*refs):
        src_refs, land_refs = refs[:nw], refs[nw:2 * nw]
        s_sems, r_sems = refs[2 * nw:2 * nw + 2]
        sends, recvs = _ici_copies(kind, src_refs, land_refs, s_sems, r_sems)
        for cp in sends:
            cp.wait_send()
        for cp in recvs:
            cp.wait_recv()

    outs = pl.pallas_call(
        body, name=name,
        out_shape=(*[pltpu.HBM(s.shape, s.dtype) for s in srcs], *[pltpu.HBM(a.shape, a.dtype) for a in lands]),
        in_specs=[HBM] * (2 * nw) + [SEM, SEM, ANY], out_specs=tuple([HBM] * (2 * nw)),
        input_output_aliases={i: i for i in range(2 * nw)},
        compiler_params=pltpu.CompilerParams(has_side_effects=EFFECT),
    )(*srcs, *lands, send_sems, recv_sems, after)
    return outs[nw:]


def _ag_d2d(lands, *, name):
    nw = len(lands)

    def body(*refs):
        out_refs = refs[nw:2 * nw]
        send_sems, recv_sems = refs[2 * nw:]
        x, y, c = _position()
        chips = [(1 - x, y), (x, 1 - y), (1 - x, 1 - y)]
        copies = []
        for wi in range(nw):
            rows = out_refs[wi].shape[1]
            for k, (px, py) in enumerate(chips):
                win = out_refs[wi].at[2 * px + py, _half(c, rows)]
                copies.append(pltpu.make_async_remote_copy(
                    src_ref=win, dst_ref=win, send_sem=send_sems.at[wi * 3 + k], recv_sem=recv_sems.at[wi * 3 + k],
                    device_id=(x, y, 1 - c), device_id_type=MESH))
        for cp in copies:
            cp.start()
        for wi in range(nw):
            rows = out_refs[wi].shape[1]
            for k, (px, py) in enumerate(chips):
                win = out_refs[wi].at[2 * px + py, _half(1 - c, rows)]
                pltpu.make_async_remote_copy(
                    src_ref=win, dst_ref=win, send_sem=send_sems.at[wi * 3 + k], recv_sem=recv_sems.at[wi * 3 + k],
                    device_id=(x, y, 1 - c), device_id_type=MESH).wait_recv()
        for cp in copies:
            cp.wait_send()

    return pl.pallas_call(
        body, name=name, out_shape=[jax.ShapeDtypeStruct(a.shape, a.dtype) for a in lands],
        in_specs=[ANY] * nw, out_specs=[ANY] * nw, input_output_aliases={i: i for i in range(nw)},
        scratch_shapes=[pltpu.SemaphoreType.DMA((3 * nw,)), pltpu.SemaphoreType.DMA((3 * nw,))],
    )(*lands)


def _rs_swap(gs, *, name):
    nw = len(gs)

    def body(*refs):
        g_refs, out_refs = refs[:nw], refs[nw:2 * nw]
        send_sems, recv_sems = refs[2 * nw:]
        x, y, c = _position()
        copies = []
        for wi in range(nw):
            rows = g_refs[wi].shape[1]
            for j in range(N_CHIPS):
                copies.append(pltpu.make_async_remote_copy(
                    src_ref=g_refs[wi].at[j, _half(1 - c, rows)], dst_ref=out_refs[wi].at[j],
                    send_sem=send_sems.at[wi * N_CHIPS + j], recv_sem=recv_sems.at[wi * N_CHIPS + j],
                    device_id=(x, y, 1 - c), device_id_type=MESH))
        for cp in copies:
            cp.start()
        for cp in copies:
            cp.wait()

    return pl.pallas_call(
        body, name=name,
        out_shape=[jax.ShapeDtypeStruct((N_CHIPS, g.shape[1] // 2, g.shape[2]), g.dtype) for g in gs],
        in_specs=[ANY] * nw, out_specs=[ANY] * nw,
        scratch_shapes=[pltpu.SemaphoreType.DMA((N_CHIPS * nw,)), pltpu.SemaphoreType.DMA((N_CHIPS * nw,))],
    )(*gs)


def _scatter_chips(sps):
    nw = len(sps)

    def body(*refs):
        s_refs, out_refs = refs[:nw], refs[nw:2 * nw]
        send_sems, recv_sems = refs[2 * nw:]
        x, y, c = _position()
        chips = [(1 - x, y), (x, 1 - y), (1 - x, 1 - y)]
        copies = [pltpu.make_async_remote_copy(
            src_ref=s_refs[wi].at[2 * px + py], dst_ref=out_refs[wi].at[k], send_sem=send_sems.at[wi * 3 + k],
            recv_sem=recv_sems.at[wi * 3 + k], device_id=(px, py, c), device_id_type=MESH)
            for wi in range(nw) for k, (px, py) in enumerate(chips)]
        for cp in copies:
            cp.start()
        for cp in copies:
            cp.wait()

    return pl.pallas_call(
        body, name="rs_scatter_chips", out_shape=[jax.ShapeDtypeStruct((3,) + sp.shape[1:], sp.dtype) for sp in sps],
        in_specs=[ANY] * nw, out_specs=[ANY] * nw,
        scratch_shapes=[pltpu.SemaphoreType.DMA((3 * nw,)), pltpu.SemaphoreType.DMA((3 * nw,))],
    )(*sps)


def _swap_reduced(reds):
    nw = len(reds)

    def body(*refs):
        r_refs, out_refs = refs[:nw], refs[nw:2 * nw]
        send_sems, recv_sems = refs[2 * nw:]
        x, y, c = _position()
        copies = [pltpu.make_async_remote_copy(
            src_ref=r_refs[wi], dst_ref=out_refs[wi], send_sem=send_sems.at[wi], recv_sem=recv_sems.at[wi],
            device_id=(x, y, 1 - c), device_id_type=MESH) for wi in range(nw)]
        for cp in copies:
            cp.start()
        for cp in copies:
            cp.wait()

    return pl.pallas_call(
        body, name="rs_swap_reduced", out_shape=[jax.ShapeDtypeStruct(r.shape, r.dtype) for r in reds],
        in_specs=[ANY] * nw, out_specs=[ANY] * nw,
        scratch_shapes=[pltpu.SemaphoreType.DMA((nw,)), pltpu.SemaphoreType.DMA((nw,))],
    )(*reds)


def _row_tile(rows, cols):
    tr = rows
    while tr * cols * 4 > (2 << 20) and tr % 32 == 0:
        tr //= 2
    return tr


def _add_own_half(g, recv, c, *, name):
    nj, h, w = recv.shape
    tr = _row_tile(h, w)
    nb = h // tr

    def body(c_ref, g_ref, r_ref, o_ref, ob_ref):
        sm = g_ref[0] + r_ref[0]
        o_ref[0] = sm
        ob_ref[0] = sm.astype(BF16)

    spec = pl.BlockSpec((1, tr, w), lambda j, i, cc: (j, i, 0))
    gs = pltpu.PrefetchScalarGridSpec(
        num_scalar_prefetch=1, grid=(nj, nb),
        in_specs=[pl.BlockSpec((1, tr, w), lambda j, i, cc: (j, cc[0] * nb + i, 0)), spec],
        out_specs=[spec, spec])
    return pl.pallas_call(body, name=name, grid_spec=gs,
                          out_shape=[jax.ShapeDtypeStruct((nj, h, w), F32), jax.ShapeDtypeStruct((nj, h, w), BF16)],
                          compiler_params=_cparams(("parallel", "parallel")))(jnp.reshape(c, (1,)).astype(I32), g, recv)


def _add_own_chip(sp, recv, j, *, name):
    _, r, w = sp.shape
    tr = _row_tile(r, w)

    def body(j_ref, s_ref, r_ref, o_ref):
        o_ref[...] = ((s_ref[0] + r_ref[0].astype(F32)) + r_ref[1].astype(F32)) + r_ref[2].astype(F32)

    gs = pltpu.PrefetchScalarGridSpec(
        num_scalar_prefetch=1, grid=(r // tr,),
        in_specs=[pl.BlockSpec((1, tr, w), lambda i, jj: (jj[0], i, 0)), pl.BlockSpec((3, tr, w), lambda i, jj: (0, i, 0))],
        out_specs=pl.BlockSpec((tr, w), lambda i, jj: (i, 0)))
    return pl.pallas_call(body, name=name, grid_spec=gs, out_shape=jax.ShapeDtypeStruct((r, w), F32),
                          compiler_params=_cparams(("parallel",)))(jnp.reshape(j, (1,)).astype(I32), sp, recv)


def _sum8(v):
    _, m, n = v.shape

    def body(v_ref, o_ref):
        acc = v_ref[0]
        for k in range(1, N_DEV):
            acc = acc + v_ref[k]
        o_ref[...] = acc

    return pl.pallas_call(body, name="sum8", grid=(1,), in_specs=[pl.BlockSpec((N_DEV, m, n), lambda i: (0, 0, 0))],
                          out_specs=pl.BlockSpec((m, n), lambda i: (0, 0)), out_shape=jax.ShapeDtypeStruct((m, n), F32),
                          compiler_params=_cparams(("arbitrary",)))(v)


def _ada_mod(c_all, ada_w, ada_b_loc, *, tn=512):
    nl, _, ncol = ada_w.shape

    def body(c_ref, w_ref, b_ref, o_ref):
        cv = c_ref[...]
        ca = (cv * _sigmoid(cv)).astype(BF16)
        o_ref[0] = _dot(ca, w_ref[0].astype(BF16)) + b_ref[0]

    return pl.pallas_call(
        body, name="ada_mod", grid=(nl, ncol // tn),
        in_specs=[pl.BlockSpec((N_DEV, D), lambda l, j: (0, 0)), pl.BlockSpec((1, D, tn), lambda l, j: (l, 0, j)),
                  pl.BlockSpec((1, 1, tn), lambda l, j: (l, 0, j))],
        out_specs=pl.BlockSpec((1, N_DEV, tn), lambda l, j: (l, 0, j)),
        out_shape=jax.ShapeDtypeStruct((nl, N_DEV, ncol), F32),
        compiler_params=_cparams(("parallel", "parallel")))(c_all, ada_w, ada_b_loc)


def _ada_grad(c_pad, dmod_pad, *, tn=512):
    nl, nb, ncol = dmod_pad.shape

    def body(c_ref, d_ref, o_ref):
        cv = c_ref[...]
        ca = (cv * _sigmoid(cv)).astype(BF16)
        o_ref[0] = _dot(ca, d_ref[0].astype(BF16), TN)

    return pl.pallas_call(
        body, name="ada_grad", grid=(nl, ncol // tn),
        in_specs=[pl.BlockSpec((nb, D), lambda l, j: (0, 0)), pl.BlockSpec((1, nb, tn), lambda l, j: (l, 0, j))],
        out_specs=pl.BlockSpec((1, D, tn), lambda l, j: (l, 0, j)),
        out_shape=jax.ShapeDtypeStruct((nl, D, ncol), F32),
        compiler_params=_cparams(("parallel", "parallel")))(c_pad, dmod_pad)


def _adamw(w, g, m, v, *, name):
    shape = w.shape
    if w.ndim == 3:
        lead, rows, cols = shape
    else:
        lead, (rows, cols) = 1, shape
    w3, g3, m3, v3 = (a.reshape(lead, rows, cols) for a in (w, g, m, v))
    tr = rows
    if rows * cols * 4 > (2 << 20):
        tr = next(cand for cand in (256, 128, 64, 8) if rows % cand == 0)
    c1 = 1.0 - ADAM_B1 ** ADAM_STEP
    c2 = 1.0 - ADAM_B2 ** ADAM_STEP

    def body(w_ref, g_ref, m_ref, v_ref, d_ref, nm_ref, nv_ref):
        gv = g_ref[...]
        nm = ADAM_B1 * m_ref[...] + (1.0 - ADAM_B1) * gv
        nv = ADAM_B2 * v_ref[...] + (1.0 - ADAM_B2) * (gv * gv)
        nm_ref[...] = nm
        nv_ref[...] = nv
        d_ref[...] = -ADAM_LR * ((nm / c1) / (jnp.sqrt(nv / c2) + ADAM_EPS) + ADAM_WD * w_ref[...])

    spec = pl.BlockSpec((1, tr, cols), lambda l, i: (l, i, 0))
    outs = pl.pallas_call(
        body, name=name, grid=(lead, rows // tr), in_specs=[spec] * 4, out_specs=[spec] * 3,
        out_shape=[jax.ShapeDtypeStruct((lead, rows, cols), F32)] * 3,
        compiler_params=_cparams(("parallel", "parallel")))(w3, g3, m3, v3)
    return tuple(o.reshape(shape) for o in outs)


def _adamw_big(w, g_own, g_sib, m, v, c, *, name):
    _, rows, cols = w.shape
    tr = _row_tile(rows // 2, cols)
    nbh = rows // 2 // tr
    c1 = 1.0 - ADAM_B1 ** ADAM_STEP
    c2 = 1.0 - ADAM_B2 ** ADAM_STEP

    def body(c_ref, w_ref, o0_ref, s0_ref, o1_ref, s1_ref, m_ref, v_ref, g_ref, d_ref, nm_ref, nv_ref):
        mine = pl.program_id(1) // nbh == c_ref[0]
        gv = jnp.where(pl.program_id(0) == 0, jnp.where(mine, o0_ref[...], s0_ref[...]),
                       jnp.where(mine, o1_ref[...], s1_ref[...]))
        nm = ADAM_B1 * m_ref[0] + (1.0 - ADAM_B1) * gv
        nv = ADAM_B2 * v_ref[0] + (1.0 - ADAM_B2) * (gv * gv)
        g_ref[0] = gv
        nm_ref[0] = nm
        nv_ref[0] = nv
        d_ref[0] = -ADAM_LR * ((nm / c1) / (jnp.sqrt(nv / c2) + ADAM_EPS) + ADAM_WD * w_ref[0])

    def source(layer, own):
        def index(l, i, cc):
            active = jnp.logical_and(l == layer, (i // nbh == cc[0]) == own)
            return (jnp.where(active, i % nbh, 0), 0)
        return pl.BlockSpec((tr, cols), index)

    spec = pl.BlockSpec((1, tr, cols), lambda l, i, cc: (l, i, 0))
    gs = pltpu.PrefetchScalarGridSpec(
        num_scalar_prefetch=1, grid=(2, 2 * nbh),
        in_specs=[spec, source(0, True), source(0, False), source(1, True), source(1, False), spec, spec],
        out_specs=[spec] * 4)
    return pl.pallas_call(
        body, name=name, grid_spec=gs, out_shape=[jax.ShapeDtypeStruct(w.shape, F32)] * 4,
        compiler_params=_cparams(("parallel", "parallel")))(
            jnp.reshape(c, (1,)).astype(I32), w, g_own[0], g_sib[0], g_own[1], g_sib[1], m, v)


SMALL = (("mix_pre_g", (2, D)), ("mix_post_g", (2, D)), ("mlp_pre_g", (2, D)), ("mlp_post_g", (2, D)),
         ("gmlp_ln_g", (2, DM)), ("gmlp_ln_b", (2, DM)), ("gmlp_ws", (2, NG, CH, CH)), ("gmlp_bs", (2, NG, CH)),
         ("conv_b", (2, DM)), ("conv_ln_g", (2, DM)), ("conv_ln_b", (2, DM)), ("fox_bf", (2, NG)))


def _pack_rows(arrays, mult=8):
    flat = []
    for a in arrays:
        f = a.reshape(-1).astype(F32)
        pad = (-f.shape[0]) % LANE
        flat.append(jnp.pad(f, (0, pad)) if pad else f)
    cat = jnp.concatenate(flat)
    rows = cat.shape[0] // LANE
    pad_rows = (-rows) % mult
    if pad_rows:
        cat = jnp.pad(cat, (0, pad_rows * LANE))
    return cat.reshape(-1, LANE)


def _unpack_rows(buf, shapes):
    flat = buf.reshape(-1)
    out, off = [], 0
    for shp in shapes:
        size = 1
        for d in shp:
            size *= d
        out.append(flat[off:off + size].reshape(shp))
        off += size + ((-size) % LANE)
    return out


def _assemble_w_in(w_in_full):
    uv_glu_qkv = w_in_full[:, :3584]
    f = w_in_full[:, 3584:3592]
    gate = w_in_full[:, 3592:]
    fpad = jnp.zeros((D, D_INP - C_F - NG), w_in_full.dtype)
    return jnp.concatenate([gate, uv_glu_qkv, f, fpad], axis=1)


def _disassemble_w_in(g_p):
    return jnp.concatenate([g_p[:, C_UV:C_F], g_p[:, C_F:C_F + NG], g_p[:, :C_UV]], axis=1)


def kernel(x, c, ada_w, ada_b, mix_pre_g, mix_post_g, mlp_pre_g, mlp_post_g, w_in, gmlp_ln_g, gmlp_ln_b, gmlp_ws, gmlp_bs, w_a_out, conv_w, conv_b, conv_ln_g, conv_ln_b, w_b_out, fox_bf, w_c_out, w_out, mlp_w1, mlp_w2, loss_target, m_ada_w, m_ada_b, m_mix_pre_g, m_mix_post_g, m_mlp_pre_g, m_mlp_post_g, m_w_in, m_gmlp_ln_g, m_gmlp_ln_b, m_gmlp_ws, m_gmlp_bs, m_w_a_out, m_conv_w, m_conv_b, m_conv_ln_g, m_conv_ln_b, m_w_b_out, m_fox_bf, m_w_c_out, m_w_out, m_mlp_w1, m_mlp_w2, v_ada_w, v_ada_b, v_mix_pre_g, v_mix_post_g, v_mlp_pre_g, v_mlp_post_g, v_w_in, v_gmlp_ln_g, v_gmlp_ln_b, v_gmlp_ws, v_gmlp_bs, v_w_a_out, v_conv_w, v_conv_b, v_conv_ln_g, v_conv_ln_b, v_w_b_out, v_fox_bf, v_w_c_out, v_w_out, v_mlp_w1, v_mlp_w2):
    weights = dict(ada_w=ada_w, ada_b=ada_b, mix_pre_g=mix_pre_g, mix_post_g=mix_post_g, mlp_pre_g=mlp_pre_g,
                   mlp_post_g=mlp_post_g, w_in=w_in, gmlp_ln_g=gmlp_ln_g, gmlp_ln_b=gmlp_ln_b, gmlp_ws=gmlp_ws,
                   gmlp_bs=gmlp_bs, w_a_out=w_a_out, conv_w=conv_w, conv_b=conv_b, conv_ln_g=conv_ln_g,
                   conv_ln_b=conv_ln_b, w_b_out=w_b_out, fox_bf=fox_bf, w_c_out=w_c_out, w_out=w_out, mlp_w1=mlp_w1,
                   mlp_w2=mlp_w2)
    mom_m = dict(ada_w=m_ada_w, ada_b=m_ada_b, mix_pre_g=m_mix_pre_g, mix_post_g=m_mix_post_g, mlp_pre_g=m_mlp_pre_g,
                 mlp_post_g=m_mlp_post_g, w_in=m_w_in, gmlp_ln_g=m_gmlp_ln_g, gmlp_ln_b=m_gmlp_ln_b, gmlp_ws=m_gmlp_ws,
                 gmlp_bs=m_gmlp_bs, w_a_out=m_w_a_out, conv_w=m_conv_w, conv_b=m_conv_b, conv_ln_g=m_conv_ln_g,
                 conv_ln_b=m_conv_ln_b, w_b_out=m_w_b_out, fox_bf=m_fox_bf, w_c_out=m_w_c_out, w_out=m_w_out,
                 mlp_w1=m_mlp_w1, mlp_w2=m_mlp_w2)
    mom_v = dict(ada_w=v_ada_w, ada_b=v_ada_b, mix_pre_g=v_mix_pre_g, mix_post_g=v_mix_post_g, mlp_pre_g=v_mlp_pre_g,
                 mlp_post_g=v_mlp_post_g, w_in=v_w_in, gmlp_ln_g=v_gmlp_ln_g, gmlp_ln_b=v_gmlp_ln_b, gmlp_ws=v_gmlp_ws,
                 gmlp_bs=v_gmlp_bs, w_a_out=v_w_a_out, conv_w=v_conv_w, conv_b=v_conv_b, conv_ln_g=v_conv_ln_g,
                 conv_ln_b=v_conv_ln_b, w_b_out=v_w_b_out, fox_bf=v_fox_bf, w_c_out=v_w_c_out, w_out=v_w_out,
                 mlp_w1=v_mlp_w1, mlp_w2=v_mlp_w2)
    order = list(weights)
    px, py, pc = _position()
    chip = 2 * px + py
    dev = 2 * chip + pc
    depth = ada_w.shape[0]
    t = x.shape[1]
    xl = x.reshape(t, D)
    tgt = loss_target.reshape(t, D)

    small_in = _pack_rows([c, conv_w])
    gathered = _all_gather8(small_in)
    c_all = gathered[:, :D // LANE, :].reshape(N_DEV, D)
    cw_rows = depth * KW * LANE // LANE
    conv_w_full = jnp.concatenate(
        [gathered[2 * j, D // LANE:D // LANE + cw_rows, :].reshape(depth, KW, LANE) for j in range(N_CHIPS)], axis=2)

    ncol = ada_w.shape[2]
    ada_b_loc = lax.dynamic_slice_in_dim(ada_b, chip * ncol, ncol, axis=1).reshape(depth, 1, ncol)
    mod_sh = _ada_mod(c_all, ada_w, ada_b_loc)
    mod_g = _all_gather8(mod_sh.reshape(-1, LANE)).reshape(N_DEV, depth, N_DEV, ncol)
    mod_all = jnp.concatenate([mod_g[2 * j] for j in range(N_CHIPS)], axis=2)
    mod_mine = lax.dynamic_index_in_dim(mod_all, dev, axis=1, keepdims=False)

    big_names = [b[0] for b in BIG]

    shards = [[weights[n][l].astype(BF16) for n in big_names] for l in range(depth)]
    mods = [mod_mine[l].reshape(NMOD, D) for l in range(depth)]

    def layer_weights(l, lands):
        lands = _ag_d2d(lands, name="ag_d2d")
        gath = [lax.dynamic_update_slice(g, sh[None], (chip, 0, 0)) for g, sh in zip(lands, shards[l])]
        w = {}
        for (n, r, cdim, ax), g in zip(BIG, gath):
            if n == "mlp_w1":
                w[n] = g
            elif ax == 1:
                w[n] = g.transpose(1, 0, 2).reshape(r, N_CHIPS * cdim)
            else:
                w[n] = g.reshape(N_CHIPS * r, cdim)
        w["w_in_p"] = _assemble_w_in(w.pop("w_in"))
        for n in ("mix_pre_g", "mix_post_g", "mlp_pre_g", "mlp_post_g", "gmlp_ln_g", "gmlp_ln_b", "conv_b", "conv_ln_g",
                  "conv_ln_b"):
            w[n] = weights[n][l:l + 1]
        tril = jnp.tril(jnp.ones((CH, CH), F32))
        wsm = gmlp_ws[l] * tril
        w["wsm"] = wsm.astype(BF16)
        w["wsmt"] = jnp.swapaxes(wsm, 1, 2).astype(BF16)
        w["bsx"] = jnp.repeat(gmlp_bs[l].T, HD, axis=1)
        w["conv_w"] = conv_w_full[l]
        w["bfp"] = jnp.pad(fox_bf[l], (0, LANE - NG)).reshape(1, LANE)
        return w

    def slabs(gfull, n, r, cdim, ax):
        if n == "mlp_w1":
            return gfull
        if ax == 1:
            return gfull.reshape(gfull.shape[0], N_CHIPS, cdim).transpose(1, 0, 2)
        return gfull.reshape(N_CHIPS, r, cdim)

    def chip_sums(g):
        g["w_in"] = _disassemble_w_in(g.pop("w_in_p"))
        gs = [slabs(g[n], n, r, cdim, ax) for n, r, cdim, ax in BIG]
        from_sibling = _rs_swap(gs, name="rs_swap")
        return [_add_own_half(a, rv, pc, name="rs_add_half_" + n) for a, rv, n in zip(gs, from_sibling, big_names)]

    def reduce_rest(sums, from_chips):
        red = [_add_own_chip(sf, rv, chip, name="rs_add_chip_" + n) for (sf, _), rv, n in zip(sums, from_chips, big_names)]
        return red, _swap_reduced(red)

    assert depth == 2
    lands0 = _ag_ici(shards[0], name="ag_ici")
    lands0, shards1 = lax.optimization_barrier((lands0, shards[1]))
    ag = _ici_start("gather", shards1, [(N_CHIPS,) + sh.shape for sh in shards1], name="ag_ici_start")
    token, mod0 = lax.optimization_barrier((ag[4], mods[0]))
    layers = [layer_weights(0, lands0), None]
    saved = [None] * depth
    xs, saved[0] = _layer_fwd(xl, mod0, layers[0])
    layers[1] = layer_weights(1, _ici_wait("gather", ag[0], ag[1], ag[2], ag[3], xs, name="ag_ici_wait"))
    xs, saved[1] = _layer_fwd(xs, mods[1], layers[1])
    loss_local, dx = _loss_and_grad(xs, tgt)
    loss = lax.psum(loss_local, ("x", "y", "c"))
    grads, dmods = [None] * depth, [None] * depth
    dx, grads[1], dmods[1] = _layer_bwd(dx, mods[1], layers[1], saved[1])
    sums1 = chip_sums(grads[1])
    sbf1 = [sb for _, sb in sums1]
    rs = _ici_start("scatter", sbf1, [(3,) + sb.shape[1:] for sb in sbf1], name="rs_scatter_start")
    token, mod0 = lax.optimization_barrier((rs[4], mod0))
    dx, grads[0], dmods[0] = _layer_bwd(dx, mod0, layers[0], saved[0])
    grad_x = dx.reshape(x.shape)
    red1 = reduce_rest(sums1, _ici_wait("scatter", rs[0], rs[1], rs[2], rs[3], dx, name="rs_scatter_wait"))
    sums0 = chip_sums(grads[0])
    red0 = reduce_rest(sums0, _scatter_chips([sb for _, sb in sums0]))
    reduced, from_sib = zip(red0, red1)
    g_out = {}

    small_names = [n for n, _ in SMALL]
    small_list = [jnp.stack(dmods)] + [jnp.stack([grads[l][n] for l in range(depth)]) for n in small_names]
    small_list.append(jnp.stack([grads[l]["conv_w"] for l in range(depth)]))
    small_shapes = [(depth, NMOD * D)] + [shp for _, shp in SMALL] + [(depth, KW, DM)]
    small_pack = _pack_rows(small_list)
    small_all = _all_gather8(small_pack)
    small_sum = _unpack_rows(_sum8(small_all), small_shapes)
    g_out["ada_b"] = small_sum[0]
    for n, gs in zip(small_names, small_sum[1:-1]):
        g_out[n] = gs
    g_out["conv_w"] = lax.dynamic_slice_in_dim(small_sum[-1], chip * LANE, LANE, axis=2)
    dmod_all = small_all[:, :depth * NMOD * D // LANE, :].reshape(N_DEV, depth, NMOD * D)
    dmod_loc = lax.dynamic_slice_in_dim(dmod_all, chip * ncol, ncol, axis=2).transpose(1, 0, 2)
    g_out["ada_w"] = _ada_grad(jnp.pad(c_all, ((0, 8), (0, 0))), jnp.pad(dmod_loc, ((0, 0), (0, 8), (0, 0))))

    delta, new_m, new_v = {}, {}, {}
    delta["ada_w"], new_m["ada_w"], new_v["ada_w"] = _adamw(ada_w, g_out["ada_w"], m_ada_w, v_ada_w, name="adamw_ada_w")
    for wi, n in enumerate(big_names):
        g_out[n], delta[n], new_m[n], new_v[n] = _adamw_big(
            weights[n], [reduced[l][wi] for l in range(depth)], [from_sib[l][wi] for l in range(depth)], mom_m[n], mom_v[n],
            pc, name="adamw_" + n)
    small_params = ["ada_b"] + small_names + ["conv_w"]
    packs = [_pack_rows([d[n] for n in small_params]) for d in (weights, g_out, mom_m, mom_v)]
    outs = _adamw(*packs, name="adamw_small")
    shapes = [weights[n].shape for n in small_params]
    for dst, buf in zip((delta, new_m, new_v), outs):
        for n, a in zip(small_params, _unpack_rows(buf, shapes)):
            dst[n] = a

    return (loss, grad_x, *[g_out[n] for n in order], *[delta[n] for n in order], *[new_m[n] for n in order],
            *[new_v[n] for n in order])
```

```python
import functools

import jax
import jax.numpy as jnp
from jax import lax
from jax.experimental import pallas as pl
from jax.experimental.pallas import tpu as pltpu

F32 = jnp.float32
BF16 = jnp.bfloat16
I32 = jnp.int32
MESH = pl.DeviceIdType.MESH
ANY = pl.BlockSpec(memory_space=pl.ANY)

D = 1024
DM = 512
NG = 8
CH = 128
KW = 31
HALO = 32
DFF = 4096
NMOD = 6
EPS = 1e-6
LANE = 128
N_CHIPS = 4
N_DEV = 8
C_GATE, C_UV, C_GLU, C_Q, C_K, C_V, C_F, D_INP = 0, 3072, 4096, 5120, 5632, 6144, 6656, 7168
D_IN = 6664
VMEM_LIMIT = 56 * 1024 * 1024

ADAM_LR, ADAM_B1, ADAM_B2, ADAM_EPS, ADAM_WD, ADAM_STEP = 0.001, 0.9, 0.999, 1e-08, 0.01, 10

BIG = (("w_in", 1024, 1666, 1), ("w_a_out", 512, 256, 1), ("w_b_out", 512, 256, 1), ("w_c_out", 512, 256, 1),
       ("w_out", 256, 1024, 0), ("mlp_w1", 1024, 1024, 1), ("mlp_w2", 1024, 1024, 0))


def _cparams(sem):
    return pltpu.CompilerParams(dimension_semantics=sem, vmem_limit_bytes=VMEM_LIMIT)


def _sigmoid(x):
    return jax.nn.sigmoid(x)


_GELU_K = 0.7978845608028654
_GELU_A = 0.044715


def _gelu(x):
    t = jnp.tanh(_GELU_K * (x + _GELU_A * x * x * x))
    return 0.5 * x * (1.0 + t)


def _gelu_grad(x):
    t = jnp.tanh(_GELU_K * (x + _GELU_A * x * x * x))
    return 0.5 * (1.0 + t) + 0.5 * x * (1.0 - t * t) * _GELU_K * (1.0 + 3.0 * _GELU_A * x * x)


def _mean(x):
    return jnp.mean(x, axis=-1, keepdims=True)


def _colsum(x):
    return jnp.sum(x, axis=0, keepdims=True)


def _dot(a, b, dims=((1,), (0,))):
    return lax.dot_general(a, b, (dims, ((), ())), preferred_element_type=F32)


NN = ((1,), (0,))
NT = ((1,), (1,))
TN = ((0,), (0,))


def _matmul(a, b, *, name, ta=False, tb=False, out_dtype=F32, tm=1024, tn=1024, tk=1024, epilogue=None, extra=(),
            extra_out=(), b_slabs=False, out_slabs=0):
    m, k = (a.shape[1], a.shape[0]) if ta else a.shape
    if b_slabs:
        ns, brows, bw = b.shape
        n = brows if tb else ns * bw
        assert (ns * bw if tb else brows) == k, (name, b.shape, k)
        tn, tk = (tn, bw) if tb else (bw, tk)
    else:
        n = b.shape[0] if tb else b.shape[1]
    tm, tn, tk = min(tm, m), min(tn, n), min(tk, k)
    assert m % tm == 0 and n % tn == 0 and k % tk == 0, (name, m, n, k, tm, tn, tk)
    assert not out_slabs or (n // out_slabs == tn and epilogue is None), name
    nk = k // tk
    dims = ((0 if ta else 1,), (1 if tb else 0,))
    n_extra = len(extra)
    out_dtypes = (out_dtype,) + tuple(extra_out)

    def body(a_ref, b_ref, *rest):
        extra_refs = rest[:n_extra]
        out_refs = rest[n_extra:n_extra + len(out_dtypes)]
        kk = pl.program_id(2)
        part = _dot(a_ref[...].astype(BF16), b_ref[...].astype(BF16), dims)

        def finish(acc):
            outs = (acc,) if epilogue is None else epilogue(acc, *[r[...] for r in extra_refs])
            for o_ref, o in zip(out_refs, outs):
                o_ref[...] = o.astype(o_ref.dtype)

        if nk == 1:
            finish(part)
        else:
            acc_ref = rest[-1]

            @pl.when(kk == 0)
            def _():
                acc_ref[...] = part

            @pl.when(jnp.logical_and(kk > 0, kk < nk - 1))
            def _():
                acc_ref[...] += part

            @pl.when(kk == nk - 1)
            def _():
                finish(acc_ref[...] + part)

    a_spec = pl.BlockSpec((tk, tm), lambda i, j, kk: (kk, i)) if ta else pl.BlockSpec((tm, tk), lambda i, j, kk: (i, kk))
    if b_slabs and tb:
        b_spec = pl.BlockSpec((None, tn, tk), lambda i, j, kk: (kk, j, 0))
    elif b_slabs:
        b_spec = pl.BlockSpec((None, tk, tn), lambda i, j, kk: (j, kk, 0))
    else:
        b_spec = pl.BlockSpec((tn, tk), lambda i, j, kk: (j, kk)) if tb else pl.BlockSpec((tk, tn), lambda i, j, kk: (kk, j))
    if out_slabs:
        o_spec = pl.BlockSpec((None, tm, tn), lambda i, j, kk: (j, i, 0))
        o_shape = (out_slabs, m, tn)
    else:
        o_spec = pl.BlockSpec((tm, tn), lambda i, j, kk: (i, j))
        o_shape = (m, n)
    outs = pl.pallas_call(
        body, name=name, grid=(m // tm, n // tn, nk),
        in_specs=[a_spec, b_spec] + [o_spec] * n_extra,
        out_specs=[o_spec] * len(out_dtypes),
        out_shape=[jax.ShapeDtypeStruct(o_shape, dt) for dt in out_dtypes],
        scratch_shapes=[pltpu.VMEM((tm, tn), F32)] if nk > 1 else [],
        compiler_params=_cparams(("parallel", "parallel", "arbitrary")),
    )(a, b, *extra)
    return outs[0] if len(outs) == 1 else outs


def _rows(tm, n, col=0):
    return pl.BlockSpec((tm, n), lambda i: (i, col))


def _vec(n):
    return pl.BlockSpec((1, n), lambda i: (0, 0))


def _norm_mod(x, g, sc, sh, *, name, tm=256):
    t = x.shape[0]
    tm = min(tm, t)

    def body(x_ref, g_ref, sc_ref, sh_ref, h_ref):
        xv = x_ref[...]
        inv = lax.rsqrt(_mean(xv * xv) + EPS)
        h_ref[...] = ((xv * inv * g_ref[...]) * (1.0 + sc_ref[...]) + sh_ref[...]).astype(BF16)

    return pl.pallas_call(
        body, name=name, grid=(t // tm,), in_specs=[_rows(tm, D), _vec(D), _vec(D), _vec(D)],
        out_specs=_rows(tm, D), out_shape=jax.ShapeDtypeStruct((t, D), BF16),
        compiler_params=_cparams(("parallel",)))(x, g, sc, sh)


def _resid(x, y, gt, gp, *, name, tm=256):
    t = x.shape[0]
    tm = min(tm, t)

    def body(x_ref, y_ref, gt_ref, gp_ref, o_ref):
        yv = y_ref[...]
        inv = lax.rsqrt(_mean(yv * yv) + EPS)
        o_ref[...] = x_ref[...] + gt_ref[...] * (yv * inv * gp_ref[...])

    return pl.pallas_call(
        body, name=name, grid=(t // tm,), in_specs=[_rows(tm, D), _rows(tm, D), _vec(D), _vec(D)],
        out_specs=_rows(tm, D), out_shape=jax.ShapeDtypeStruct((t, D), F32),
        compiler_params=_cparams(("parallel",)))(x, y, gt, gp)


def _resid_bwd(dx, y, gt, gp, *, name, tm=256):
    t = dx.shape[0]
    tm = min(tm, t)

    def body(dx_ref, y_ref, gt_ref, gp_ref, dy_ref, dgt_ref, dgp_ref):
        @pl.when(pl.program_id(0) == 0)
        def _():
            dgt_ref[...] = jnp.zeros_like(dgt_ref)
            dgp_ref[...] = jnp.zeros_like(dgp_ref)

        dxv, yv, gp_v = dx_ref[...], y_ref[...], gp_ref[...]
        inv = lax.rsqrt(_mean(yv * yv) + EPS)
        yh = yv * inv
        dgt_ref[...] += _colsum(dxv * (yh * gp_v))
        dr = dxv * gt_ref[...]
        dgp_ref[...] += _colsum(dr * yh)
        dyn = dr * gp_v
        dy_ref[...] = (inv * (dyn - yh * _mean(dyn * yh))).astype(BF16)

    return pl.pallas_call(
        body, name=name, grid=(t // tm,), in_specs=[_rows(tm, D), _rows(tm, D), _vec(D), _vec(D)],
        out_specs=[_rows(tm, D), _vec(D), _vec(D)],
        out_shape=[jax.ShapeDtypeStruct((t, D), BF16), jax.ShapeDtypeStruct((1, D), F32),
                   jax.ShapeDtypeStruct((1, D), F32)],
        compiler_params=_cparams(("arbitrary",)))(dx, y, gt, gp)


def _norm_bwd(dh, dx_res, x, g, sc, *, name, tm=256):
    t = dh.shape[0]
    tm = min(tm, t)

    def body(dh_ref, dxr_ref, x_ref, g_ref, sc_ref, dx_ref, dg_ref, dsc_ref, dsh_ref):
        @pl.when(pl.program_id(0) == 0)
        def _():
            dg_ref[...] = jnp.zeros_like(dg_ref)
            dsc_ref[...] = jnp.zeros_like(dsc_ref)
            dsh_ref[...] = jnp.zeros_like(dsh_ref)

        dhv, xv, gv = dh_ref[...], x_ref[...], g_ref[...]
        inv = lax.rsqrt(_mean(xv * xv) + EPS)
        xh = xv * inv
        dsh_ref[...] += _colsum(dhv)
        dsc_ref[...] += _colsum(dhv * (xh * gv))
        dn = dhv * (1.0 + sc_ref[...])
        dg_ref[...] += _colsum(dn * xh)
        dxh = dn * gv
        dx_ref[...] = inv * (dxh - xh * _mean(dxh * xh)) + dxr_ref[...]

    vec_out = jax.ShapeDtypeStruct((1, D), F32)
    return pl.pallas_call(
        body, name=name, grid=(t // tm,), in_specs=[_rows(tm, D), _rows(tm, D), _rows(tm, D), _vec(D), _vec(D)],
        out_specs=[_rows(tm, D), _vec(D), _vec(D), _vec(D)],
        out_shape=[jax.ShapeDtypeStruct((t, D), F32), vec_out, vec_out, vec_out],
        compiler_params=_cparams(("arbitrary",)))(dh, dx_res, x, g, sc)


def _loss_and_grad(x, target, *, tm=256):
    t = x.shape[0]
    tm = min(tm, t)

    def body(x_ref, t_ref, loss_ref, dx_ref):
        @pl.when(pl.program_id(0) == 0)
        def _():
            loss_ref[...] = jnp.zeros_like(loss_ref)

        e = x_ref[...] - t_ref[...]
        dx_ref[...] = e * (1.0 / D)
        s = jnp.sum(jnp.sum(e * e, axis=1, keepdims=True), axis=0, keepdims=True) * (0.5 / D)
        loss_ref[...] += jnp.broadcast_to(s, loss_ref.shape)

    loss, dx = pl.pallas_call(
        body, name="loss", grid=(t // tm,), in_specs=[_rows(tm, D), _rows(tm, D)],
        out_specs=[pl.BlockSpec((8, LANE), lambda i: (0, 0)), _rows(tm, D)],
        out_shape=[jax.ShapeDtypeStruct((8, LANE), F32), jax.ShapeDtypeStruct((t, D), F32)],
        compiler_params=_cparams(("arbitrary",)))(x, target)
    return loss[0, 0], dx


def _gmlp_core(uv, lng, lnb, ws_ref, bsx):
    tm = uv.shape[0]
    gu = _gelu(uv[:, :DM])
    gv = _gelu(uv[:, DM:])
    mu = _mean(gv)
    vc = gv - mu
    rstd = lax.rsqrt(_mean(vc * vc) + EPS)
    vh = vc * rstd
    vln = vh * lng + lnb
    lane = lax.broadcasted_iota(I32, (CH, LANE), 1)
    sv_rows = []
    for nchunk in range(tm // CH):
        vb = vln[nchunk * CH:(nchunk + 1) * CH].astype(BF16)
        cols = []
        for cb in range(DM // LANE):
            vcb = vb[:, cb * LANE:(cb + 1) * LANE]
            lo = _dot(ws_ref[2 * cb], vcb)
            hi = _dot(ws_ref[2 * cb + 1], vcb)
            cols.append(jnp.where(lane < 64, lo, hi))
        sv_rows.append(jnp.concatenate(cols, axis=1) + bsx)
    sv = jnp.concatenate(sv_rows, axis=0) if len(sv_rows) > 1 else sv_rows[0]
    return gu, vh, rstd, vln, sv


def _gmlp_fwd(proj, lng, lnb, wsm, bsx, *, tm=256):
    t = proj.shape[0]
    tm = min(tm, t)

    def body(uv_ref, lng_ref, lnb_ref, ws_ref, bs_ref, ga_ref):
        gu, _, _, _, sv = _gmlp_core(uv_ref[...], lng_ref[...], lnb_ref[...], ws_ref, bs_ref[...])
        ga_ref[...] = (gu * sv).astype(BF16)

    return pl.pallas_call(
        body, name="gmlp_fwd", grid=(t // tm,),
        in_specs=[_rows(tm, 2 * DM, C_UV // (2 * DM)), _vec(DM), _vec(DM),
                  pl.BlockSpec((NG, CH, CH), lambda i: (0, 0, 0)), pl.BlockSpec((CH, DM), lambda i: (0, 0))],
        out_specs=_rows(tm, DM), out_shape=jax.ShapeDtypeStruct((t, DM), BF16),
        compiler_params=_cparams(("parallel",)))(proj, lng, lnb, wsm, bsx)


def _gmlp_bwd(dga, proj, lng, lnb, wsm, wsmt, bsx, *, tm=256):
    t = proj.shape[0]
    tm = min(tm, t)

    def body(dga_ref, uv_ref, lng_ref, lnb_ref, ws_ref, wst_ref, bs_ref, duv_ref, dws_ref, dbs_ref, dlng_ref, dlnb_ref,
             dbsx_ref):
        i = pl.program_id(0)

        @pl.when(i == 0)
        def _():
            dws_ref[...] = jnp.zeros_like(dws_ref)
            dbsx_ref[...] = jnp.zeros_like(dbsx_ref)
            dlng_ref[...] = jnp.zeros_like(dlng_ref)
            dlnb_ref[...] = jnp.zeros_like(dlnb_ref)

        uv = uv_ref[...]
        lng_v = lng_ref[...]
        gu, vh, rstd, vln, sv = _gmlp_core(uv, lng_v, lnb_ref[...], ws_ref, bs_ref[...])
        dga_v = dga_ref[...]
        dgu = dga_v * sv
        dsv = dga_v * gu
        lane = lax.broadcasted_iota(I32, (CH, LANE), 1)
        tril = lax.broadcasted_iota(I32, (CH, CH), 0) >= lax.broadcasted_iota(I32, (CH, CH), 1)
        dvln_rows = []
        for nchunk in range(tm // CH):
            rows = slice(nchunk * CH, (nchunk + 1) * CH)
            dbsx_ref[...] += dsv[rows]
            vb = vln[rows].astype(BF16)
            cols = []
            for cb in range(DM // LANE):
                cs = slice(cb * LANE, (cb + 1) * LANE)
                dsvb = dsv[rows, cs]
                vcb = vb[:, cs]
                dlo = jnp.where(lane < 64, dsvb, 0.0).astype(BF16)
                dhi = jnp.where(lane < 64, 0.0, dsvb).astype(BF16)
                dws_ref[2 * cb] += jnp.where(tril, _dot(dlo, vcb, NT), 0.0)
                dws_ref[2 * cb + 1] += jnp.where(tril, _dot(dhi, vcb, NT), 0.0)
                dsb = dsvb.astype(BF16)
                cols.append(jnp.where(lane < 64, _dot(wst_ref[2 * cb], dsb), _dot(wst_ref[2 * cb + 1], dsb)))
            dvln_rows.append(jnp.concatenate(cols, axis=1))
        dvln = jnp.concatenate(dvln_rows, axis=0) if len(dvln_rows) > 1 else dvln_rows[0]
        dlnb_ref[...] += _colsum(dvln)
        dlng_ref[...] += _colsum(dvln * vh)
        dvh = dvln * lng_v
        dgv = rstd * (dvh - _mean(dvh) - vh * _mean(dvh * vh))
        duv_ref[:, :DM] = (dgu * _gelu_grad(uv[:, :DM])).astype(BF16)
        duv_ref[:, DM:] = (dgv * _gelu_grad(uv[:, DM:])).astype(BF16)

        @pl.when(i == pl.num_programs(0) - 1)
        def _():
            ind = (lax.broadcasted_iota(I32, (DM, LANE), 0) // 64 == lax.broadcasted_iota(I32, (DM, LANE), 1)).astype(F32)
            dbs_ref[...] = jnp.dot(dbsx_ref[...], ind, preferred_element_type=F32, precision=lax.Precision.HIGHEST)

    vec_out = jax.ShapeDtypeStruct((1, DM), F32)
    outs = pl.pallas_call(
        body, name="gmlp_bwd", grid=(t // tm,),
        in_specs=[_rows(tm, DM), _rows(tm, 2 * DM, C_UV // (2 * DM)), _vec(DM), _vec(DM),
                  pl.BlockSpec((NG, CH, CH), lambda i: (0, 0, 0)), pl.BlockSpec((NG, CH, CH), lambda i: (0, 0, 0)),
                  pl.BlockSpec((CH, DM), lambda i: (0, 0))],
        out_specs=[_rows(tm, 2 * DM), pl.BlockSpec((NG, CH, CH), lambda i: (0, 0, 0)),
                   pl.BlockSpec((CH, LANE), lambda i: (0, 0)), _vec(DM), _vec(DM)],
        out_shape=[jax.ShapeDtypeStruct((t, 2 * DM), BF16), jax.ShapeDtypeStruct((NG, CH, CH), F32),
                   jax.ShapeDtypeStruct((CH, LANE), F32), vec_out, vec_out],
        scratch_shapes=[pltpu.VMEM((CH, DM), F32)],
        compiler_params=_cparams(("arbitrary",)))(dga, proj, lng, lnb, wsm, wsmt, bsx)
    return outs


def _glu_into(zs_ref, glu_ref, halo_ref, first):
    hal = halo_ref[...]
    z0h = hal[:, :DM] * _sigmoid(hal[:, DM:])
    zs_ref[0:HALO, :] = jnp.where(first, 0.0, z0h)
    g = glu_ref[...]
    zs_ref[HALO:, :] = g[:, :DM] * _sigmoid(g[:, DM:])


def _conv_fwd(proj, cw, cb, lng, lnb, *, tm=256, rb=64):
    t = proj.shape[0]
    tm = min(tm, t)
    hb = tm // HALO
    gcol = C_GLU // (2 * DM)

    def body(glu_ref, halo_ref, cw_ref, cb_ref, lng_ref, lnb_ref, zc_ref, zb_ref, zs_ref):
        i = pl.program_id(0)
        _glu_into(zs_ref, glu_ref, halo_ref, i == 0)
        for cbk in range(DM // LANE):
            cs = slice(cbk * LANE, (cbk + 1) * LANE)
            for r in range(tm // rb):
                acc = jnp.broadcast_to(cb_ref[:, cs], (rb, LANE))
                for k in range(KW):
                    off = r * rb + HALO - (KW - 1) + k
                    acc = acc + cw_ref[k:k + 1, cs] * zs_ref[off:off + rb, cs]
                zc_ref[r * rb:(r + 1) * rb, cs] = acc
        zc = zc_ref[...]
        mu = _mean(zc)
        zcc = zc - mu
        zh = zcc * lax.rsqrt(_mean(zcc * zcc) + EPS)
        a = zh * lng_ref[...] + lnb_ref[...]
        zb_ref[...] = (a * _sigmoid(a)).astype(BF16)

    return pl.pallas_call(
        body, name="conv_fwd", grid=(t // tm,),
        in_specs=[_rows(tm, 2 * DM, gcol),
                  pl.BlockSpec((HALO, 2 * DM), lambda i: (jnp.maximum(i * hb - 1, 0), gcol)),
                  pl.BlockSpec((KW, DM), lambda i: (0, 0)), _vec(DM), _vec(DM), _vec(DM)],
        out_specs=[_rows(tm, DM), _rows(tm, DM)],
        out_shape=[jax.ShapeDtypeStruct((t, DM), F32), jax.ShapeDtypeStruct((t, DM), BF16)],
        scratch_shapes=[pltpu.VMEM((HALO + tm, DM), F32)],
        compiler_params=_cparams(("parallel",)))(proj, proj, cw, cb, lng, lnb)


def _conv_bwd_ln(dzb, zc, lng, lnb, *, tm=256):
    t = zc.shape[0]
    tm = min(tm, t)

    def body(dzb_ref, zc_ref, lng_ref, lnb_ref, dzc_ref, dlng_ref, dlnb_ref):
        @pl.when(pl.program_id(0) == 0)
        def _():
            dlng_ref[...] = jnp.zeros_like(dlng_ref)
            dlnb_ref[...] = jnp.zeros_like(dlnb_ref)

        zc = zc_ref[...]
        lng_v = lng_ref[...]
        mu = _mean(zc)
        zcc = zc - mu
        rstd = lax.rsqrt(_mean(zcc * zcc) + EPS)
        zh = zcc * rstd
        a = zh * lng_v + lnb_ref[...]
        s = _sigmoid(a)
        da = dzb_ref[...] * (s * (1.0 + a * (1.0 - s)))
        dlnb_ref[...] += _colsum(da)
        dlng_ref[...] += _colsum(da * zh)
        dzh = da * lng_v
        dzc_ref[...] = rstd * (dzh - _mean(dzh) - zh * _mean(dzh * zh))

    vec_out = jax.ShapeDtypeStruct((1, DM), F32)
    return pl.pallas_call(
        body, name="conv_bwd_ln", grid=(t // tm,), in_specs=[_rows(tm, DM), _rows(tm, DM), _vec(DM), _vec(DM)],
        out_specs=[_rows(tm, DM), _vec(DM), _vec(DM)],
        out_shape=[jax.ShapeDtypeStruct((t, DM), F32), vec_out, vec_out],
        compiler_params=_cparams(("arbitrary",)))(dzb, zc, lng, lnb)


def _conv_bwd(dzc, proj, cw, *, tm=256, rb=64):
    t = proj.shape[0]
    tm = min(tm, t)
    hb = tm // HALO
    nblk = t // tm
    gcol = C_GLU // (2 * DM)

    def body(dzc_ref, dnext_ref, glu_ref, halo_ref, cw_ref, dglu_ref, dcw_ref, dcb_ref, zs_ref, ds_ref):
        i = pl.program_id(0)

        @pl.when(i == 0)
        def _():
            dcw_ref[...] = jnp.zeros_like(dcw_ref)
            dcb_ref[...] = jnp.zeros_like(dcb_ref)

        _glu_into(zs_ref, glu_ref, halo_ref, i == 0)
        dzc = dzc_ref[...]
        ds_ref[0:tm, :] = dzc
        ds_ref[tm:, :] = jnp.where(i == nblk - 1, 0.0, dnext_ref[...])
        dcb_ref[...] += _colsum(dzc)
        for k in range(KW):
            off = HALO - (KW - 1) + k
            dcw_ref[k:k + 1, :] += _colsum(dzc * zs_ref[off:off + tm, :])
        g = glu_ref[...]
        val, sg = g[:, :DM], _sigmoid(g[:, DM:])
        for cbk in range(DM // LANE):
            cs = slice(cbk * LANE, (cbk + 1) * LANE)
            for r in range(tm // rb):
                acc = jnp.zeros((rb, LANE), F32)
                for k in range(KW):
                    off = r * rb + (KW - 1) - k
                    acc = acc + cw_ref[k:k + 1, cs] * ds_ref[off:off + rb, cs]
                rs = slice(r * rb, (r + 1) * rb)
                dglu_ref[rs, cs] = (acc * sg[rs, cs]).astype(BF16)
                v, s = val[rs, cs], sg[rs, cs]
                dglu_ref[rs, DM + cbk * LANE:DM + (cbk + 1) * LANE] = (acc * v * s * (1.0 - s)).astype(BF16)

    return pl.pallas_call(
        body, name="conv_bwd", grid=(nblk,),
        in_specs=[_rows(tm, DM),
                  pl.BlockSpec((HALO, DM), lambda i: (jnp.minimum((i + 1) * hb, nblk * hb - 1), 0)),
                  _rows(tm, 2 * DM, gcol),
                  pl.BlockSpec((HALO, 2 * DM), lambda i: (jnp.maximum(i * hb - 1, 0), gcol)),
                  pl.BlockSpec((KW, DM), lambda i: (0, 0))],
        out_specs=[_rows(tm, 2 * DM), pl.BlockSpec((HALO, DM), lambda i: (0, 0)), _vec(DM)],
        out_shape=[jax.ShapeDtypeStruct((t, 2 * DM), BF16), jax.ShapeDtypeStruct((HALO, DM), F32),
                   jax.ShapeDtypeStruct((1, DM), F32)],
        scratch_shapes=[pltpu.VMEM((HALO + tm, DM), F32), pltpu.VMEM((tm + HALO, DM), F32)],
        compiler_params=_cparams(("arbitrary",)))(dzc, dzc, proj, proj, cw)


def _log_sigmoid(x):
    return jnp.minimum(x, 0.0) - jnp.log1p(jnp.exp(-jnp.abs(x)))


def _fox_cum(proj, bfp):
    t = proj.shape[0]
    fcol = C_F // LANE

    def body(f_ref, bf_ref, cum_ref, carry_ref):
        @pl.when(pl.program_id(0) == 0)
        def _():
            carry_ref[...] = jnp.zeros_like(carry_ref)

        lf = _log_sigmoid(f_ref[...] + bf_ref[...])
        tri = (lax.broadcasted_iota(I32, (CH, CH), 0) >= lax.broadcasted_iota(I32, (CH, CH), 1)).astype(F32)
        cum = jnp.dot(tri, lf, preferred_element_type=F32, precision=lax.Precision.HIGHEST) + carry_ref[0:1, :]
        cum_ref[...] = cum
        carry_ref[...] = jnp.broadcast_to(cum[CH - 1:CH, :], carry_ref.shape)

    return pl.pallas_call(
        body, name="fox_cum", grid=(t // CH,), in_specs=[_rows(CH, LANE, fcol), _vec(LANE)],
        out_specs=_rows(CH, LANE), out_shape=jax.ShapeDtypeStruct((t, LANE), F32),
        scratch_shapes=[pltpu.VMEM((8, LANE), F32)],
        compiler_params=_cparams(("arbitrary",)))(proj, bfp)


def _fox_cum_bwd(dcum, proj, bfp):
    t = proj.shape[0]
    nb = t // CH
    fcol = C_F // LANE
    fw = D_INP - C_F

    def body(dc_ref, f_ref, bf_ref, df_ref, dbf_ref, carry_ref):
        @pl.when(pl.program_id(0) == 0)
        def _():
            carry_ref[...] = jnp.zeros_like(carry_ref)
            dbf_ref[...] = jnp.zeros_like(dbf_ref)

        triu = (lax.broadcasted_iota(I32, (CH, CH), 0) <= lax.broadcasted_iota(I32, (CH, CH), 1)).astype(F32)
        dlf = jnp.dot(triu, dc_ref[...], preferred_element_type=F32, precision=lax.Precision.HIGHEST) + carry_ref[0:1, :]
        carry_ref[...] = jnp.broadcast_to(dlf[0:1, :], carry_ref.shape)
        z = f_ref[...] + bf_ref[...]
        lane = lax.broadcasted_iota(I32, (CH, LANE), 1)
        df = jnp.where(lane < NG, dlf * _sigmoid(-z), 0.0)
        dbf_ref[...] += _colsum(df)
        df_ref[:, 0:LANE] = df.astype(BF16)
        df_ref[:, LANE:] = jnp.zeros((CH, fw - LANE), BF16)

    return pl.pallas_call(
        body, name="fox_cum_bwd", grid=(nb,),
        in_specs=[pl.BlockSpec((CH, LANE), lambda i: (nb - 1 - i, 0)),
                  pl.BlockSpec((CH, LANE), lambda i: (nb - 1 - i, fcol)), _vec(LANE)],
        out_specs=[pl.BlockSpec((CH, fw), lambda i: (nb - 1 - i, 0)), _vec(LANE)],
        out_shape=[jax.ShapeDtypeStruct((t, fw), BF16), jax.ShapeDtypeStruct((1, LANE), F32)],
        scratch_shapes=[pltpu.VMEM((8, LANE), F32)],
        compiler_params=_cparams(("arbitrary",)))(dcum, proj, bfp)


HD = 64
ATT_SCALE = 0.125
NEG = -1e30


def _qkv_prep(proj, *, tm=512):
    t = proj.shape[0]
    tm = min(tm, t)

    def body(q_ref, k_ref, v_ref, o_ref):
        o_ref[:, 0:DM] = (q_ref[...] * ATT_SCALE).astype(BF16)
        o_ref[:, DM:2 * DM] = k_ref[...].astype(BF16)
        o_ref[:, 2 * DM:] = v_ref[...].astype(BF16)

    return pl.pallas_call(
        body, name="qkv_prep", grid=(t // tm,),
        in_specs=[_rows(tm, DM, C_Q // DM), _rows(tm, DM, C_K // DM), _rows(tm, DM, C_V // DM)],
        out_specs=_rows(tm, 3 * DM), out_shape=jax.ShapeDtypeStruct((t, 3 * DM), BF16),
        compiler_params=_cparams(("parallel",)))(proj, proj, proj)


def _causal_pairs(nq, outer_is_query):
    if outer_is_query:
        pairs = [(i, j) for i in range(nq) for j in range(i + 1)]
    else:
        pairs = [(j, i) for j in range(nq) for i in range(j, nq)]
    return (jnp.asarray([p[0] for p in pairs], I32), jnp.asarray([p[1] for p in pairs], I32))


def _to_row(col):
    return jnp.transpose(col)[0:1, :]


def _rep(x, tk):
    return x if tk == LANE else jnp.tile(x, (1, tk // LANE))


def _attn_fwd(qkv, ckrow, *, tq=512):
    t = qkv.shape[0]
    tq = min(tq, t)
    tk = tq
    nq = t // tq
    oi, ij = _causal_pairs(nq, True)

    def body(oi_ref, ij_ref, q_ref, k_ref, v_ref, ck_ref, o_ref, lse_ref, lser_ref, m_ref, l_ref, acc_ref):
        n = pl.program_id(0)
        i, j = oi_ref[n], ij_ref[n]

        @pl.when(j == 0)
        def _():
            m_ref[...] = jnp.full_like(m_ref, NEG)
            l_ref[...] = jnp.zeros_like(l_ref)
            acc_ref[...] = jnp.zeros_like(acc_ref)

        def step(masked):
            if masked:
                keep = lax.broadcasted_iota(I32, (tq, tk), 1) <= lax.broadcasted_iota(I32, (tq, tk), 0)
            lo = lax.broadcasted_iota(I32, (tq, LANE), 1) < HD
            for hp in range(NG // 2):
                cs = slice(hp * LANE, (hp + 1) * LANE)
                qp, kp, vp = q_ref[:, cs], k_ref[:, cs], v_ref[:, cs]
                alphas, pvs = [], []
                for hh in range(2):
                    h = 2 * hp + hh
                    qm = jnp.where(lo if hh == 0 else jnp.logical_not(lo), qp, jnp.zeros_like(qp))
                    s = _dot(qm, kp, NT) - ck_ref[h:h + 1, :]
                    if masked:
                        s = jnp.where(keep, s, NEG)
                    m_prev = m_ref[h]
                    m_new = jnp.maximum(m_prev, jnp.max(s, axis=1, keepdims=True))
                    alpha = jnp.exp(m_prev - m_new)
                    p = jnp.exp(s - _rep(m_new, tk))
                    l_ref[h] = alpha * l_ref[h] + jnp.sum(p, axis=1, keepdims=True)
                    m_ref[h] = m_new
                    alphas.append(alpha)
                    pvs.append(_dot(p.astype(BF16), vp))
                acc_ref[:, cs] = jnp.where(lo, alphas[0], alphas[1]) * acc_ref[:, cs] + jnp.where(lo, pvs[0], pvs[1])

        @pl.when(j < i)
        def _():
            step(False)

        @pl.when(j == i)
        def _():
            step(True)
            lo = lax.broadcasted_iota(I32, (tq, LANE), 1) < HD
            for hp in range(NG // 2):
                cs = slice(hp * LANE, (hp + 1) * LANE)
                o_ref[:, cs] = (acc_ref[:, cs] / jnp.where(lo, l_ref[2 * hp], l_ref[2 * hp + 1])).astype(BF16)
            for h in range(NG):
                lse = m_ref[h] + jnp.log(l_ref[h])
                lse_ref[h] = lse
                lser_ref[h:h + 1, :] = _to_row(lse)

    gs = pltpu.PrefetchScalarGridSpec(
        num_scalar_prefetch=2, grid=(int(oi.shape[0]),),
        in_specs=[pl.BlockSpec((tq, DM), lambda n, a, b: (a[n], 0)),
                  pl.BlockSpec((tk, DM), lambda n, a, b: (b[n], 1)),
                  pl.BlockSpec((tk, DM), lambda n, a, b: (b[n], 2)),
                  pl.BlockSpec((NG, tk), lambda n, a, b: (0, b[n]))],
        out_specs=[pl.BlockSpec((tq, DM), lambda n, a, b: (a[n], 0)),
                   pl.BlockSpec((NG, tq, LANE), lambda n, a, b: (0, a[n], 0)),
                   pl.BlockSpec((NG, tq), lambda n, a, b: (0, a[n]))],
        scratch_shapes=[pltpu.VMEM((NG, tq, LANE), F32), pltpu.VMEM((NG, tq, LANE), F32), pltpu.VMEM((tq, DM), F32)])
    return pl.pallas_call(
        body, name="attn_fwd", grid_spec=gs,
        out_shape=[jax.ShapeDtypeStruct((t, DM), BF16), jax.ShapeDtypeStruct((NG, t, LANE), F32),
                   jax.ShapeDtypeStruct((NG, t), F32)],
        compiler_params=_cparams(("arbitrary",)))(oi, ij, qkv, qkv, qkv, ckrow)


def _attn_bwd_dq(qkv, ckrow, o, do, lse, *, tq=512):
    t = qkv.shape[0]
    tq = min(tq, t)
    tk = tq
    nq = t // tq
    oi, ij = _causal_pairs(nq, True)

    def body(oi_ref, ij_ref, q_ref, k_ref, v_ref, ck_ref, o_ref, do_ref, lse_ref, dq_ref, deltar_ref, dcqr_ref, delta_ref,
             dcq_ref, acc_ref):
        n = pl.program_id(0)
        i, j = oi_ref[n], ij_ref[n]

        @pl.when(j == 0)
        def _():
            acc_ref[...] = jnp.zeros_like(acc_ref)
            dcq_ref[...] = jnp.zeros_like(dcq_ref)
            lo = lax.broadcasted_iota(I32, (tq, LANE), 1) < HD
            for hp in range(NG // 2):
                cs = slice(hp * LANE, (hp + 1) * LANE)
                prod = do_ref[:, cs] * o_ref[:, cs].astype(F32)
                for hh in range(2):
                    d = jnp.sum(jnp.where(lo if hh == 0 else jnp.logical_not(lo), prod, 0.0), axis=1, keepdims=True)
                    dcol = jnp.broadcast_to(d, (tq, LANE))
                    delta_ref[2 * hp + hh] = dcol
                    deltar_ref[2 * hp + hh:2 * hp + hh + 1, :] = _to_row(dcol)

        def step(masked):
            if masked:
                keep = lax.broadcasted_iota(I32, (tq, tk), 1) <= lax.broadcasted_iota(I32, (tq, tk), 0)
            lo = lax.broadcasted_iota(I32, (tq, LANE), 1) < HD
            for hp in range(NG // 2):
                cs = slice(hp * LANE, (hp + 1) * LANE)
                qp, kp, vp, dop = q_ref[:, cs], k_ref[:, cs], v_ref[:, cs], do_ref[:, cs].astype(BF16)
                parts = []
                for hh in range(2):
                    h = 2 * hp + hh
                    sel = lo if hh == 0 else jnp.logical_not(lo)
                    s = _dot(jnp.where(sel, qp, jnp.zeros_like(qp)), kp, NT) - ck_ref[h:h + 1, :]
                    if masked:
                        s = jnp.where(keep, s, NEG)
                    p = jnp.exp(s - _rep(lse_ref[h], tk))
                    dp = _dot(jnp.where(sel, dop, jnp.zeros_like(dop)), vp, NT)
                    ds = p * (dp - _rep(delta_ref[h], tk))
                    dcq_ref[h] += jnp.sum(ds, axis=1, keepdims=True)
                    parts.append(_dot(ds.astype(BF16), kp))
                acc_ref[:, cs] += jnp.where(lo, parts[0], parts[1])

        @pl.when(j < i)
        def _():
            step(False)

        @pl.when(j == i)
        def _():
            step(True)
            dq_ref[...] = (acc_ref[...] * ATT_SCALE).astype(BF16)
            for h in range(NG):
                dcqr_ref[h:h + 1, :] = _to_row(dcq_ref[h])

    gs = pltpu.PrefetchScalarGridSpec(
        num_scalar_prefetch=2, grid=(int(oi.shape[0]),),
        in_specs=[pl.BlockSpec((tq, DM), lambda n, a, b: (a[n], 0)),
                  pl.BlockSpec((tk, DM), lambda n, a, b: (b[n], 1)),
                  pl.BlockSpec((tk, DM), lambda n, a, b: (b[n], 2)),
                  pl.BlockSpec((NG, tk), lambda n, a, b: (0, b[n])),
                  pl.BlockSpec((tq, DM), lambda n, a, b: (a[n], 0)),
                  pl.BlockSpec((tq, DM), lambda n, a, b: (a[n], 0)),
                  pl.BlockSpec((NG, tq, LANE), lambda n, a, b: (0, a[n], 0))],
        out_specs=[pl.BlockSpec((tq, DM), lambda n, a, b: (a[n], 0)),
                   pl.BlockSpec((NG, tq), lambda n, a, b: (0, a[n])),
                   pl.BlockSpec((NG, tq), lambda n, a, b: (0, a[n]))],
        scratch_shapes=[pltpu.VMEM((NG, tq, LANE), F32), pltpu.VMEM((NG, tq, LANE), F32), pltpu.VMEM((tq, DM), F32)])
    return pl.pallas_call(
        body, name="attn_bwd_dq", grid_spec=gs,
        out_shape=[jax.ShapeDtypeStruct((t, DM), BF16), jax.ShapeDtypeStruct((NG, t), F32), jax.ShapeDtypeStruct((NG, t), F32)],
        compiler_params=_cparams(("arbitrary",)))(oi, ij, qkv, qkv, qkv, ckrow, o, do, lse)


def _attn_bwd_dkv(qkv, ckcol, do, lserow, deltarow, *, tq=512):
    t = qkv.shape[0]
    tq = min(tq, t)
    tk = tq
    nq = t // tq
    oj, ii = _causal_pairs(nq, False)

    def body(oj_ref, ii_ref, q_ref, k_ref, v_ref, ck_ref, do_ref, lse_ref, delta_ref, dk_ref, dv_ref, dckr_ref, dka_ref,
             dva_ref, dck_ref):
        n = pl.program_id(0)
        j, i = oj_ref[n], ii_ref[n]

        @pl.when(i == j)
        def _():
            dka_ref[...] = jnp.zeros_like(dka_ref)
            dva_ref[...] = jnp.zeros_like(dva_ref)
            dck_ref[...] = jnp.zeros_like(dck_ref)

        def step(masked):
            if masked:
                keep = lax.broadcasted_iota(I32, (tk, tq), 0) <= lax.broadcasted_iota(I32, (tk, tq), 1)
            lo = lax.broadcasted_iota(I32, (tk, LANE), 1) < HD
            for hp in range(NG // 2):
                cs = slice(hp * LANE, (hp + 1) * LANE)
                qp, kp, vp, dop = q_ref[:, cs], k_ref[:, cs], v_ref[:, cs], do_ref[:, cs].astype(BF16)
                dvs, dks = [], []
                for hh in range(2):
                    h = 2 * hp + hh
                    sel = lo if hh == 0 else jnp.logical_not(lo)
                    st = _dot(jnp.where(sel, kp, jnp.zeros_like(kp)), qp, NT) - _rep(ck_ref[h], tq)
                    if masked:
                        st = jnp.where(keep, st, NEG)
                    pt = jnp.exp(st - lse_ref[h:h + 1, :])
                    dvs.append(_dot(pt.astype(BF16), dop))
                    dpt = _dot(jnp.where(sel, vp, jnp.zeros_like(vp)), dop, NT)
                    dst = pt * (dpt - delta_ref[h:h + 1, :])
                    dks.append(_dot(dst.astype(BF16), qp))
                    dck_ref[h] -= jnp.sum(dst, axis=1, keepdims=True)
                dva_ref[:, cs] += jnp.where(lo, dvs[0], dvs[1])
                dka_ref[:, cs] += jnp.where(lo, dks[0], dks[1])

        @pl.when(i == j)
        def _():
            step(True)

        @pl.when(i > j)
        def _():
            step(False)

        @pl.when(i == nq - 1)
        def _():
            dk_ref[...] = dka_ref[...].astype(BF16)
            dv_ref[...] = dva_ref[...].astype(BF16)
            for h in range(NG):
                dckr_ref[h:h + 1, :] = _to_row(dck_ref[h])

    gs = pltpu.PrefetchScalarGridSpec(
        num_scalar_prefetch=2, grid=(int(oj.shape[0]),),
        in_specs=[pl.BlockSpec((tq, DM), lambda n, a, b: (b[n], 0)),
                  pl.BlockSpec((tk, DM), lambda n, a, b: (a[n], 1)),
                  pl.BlockSpec((tk, DM), lambda n, a, b: (a[n], 2)),
                  pl.BlockSpec((NG, tk, LANE), lambda n, a, b: (0, a[n], 0)),
                  pl.BlockSpec((tq, DM), lambda n, a, b: (b[n], 0)),
                  pl.BlockSpec((NG, tq), lambda n, a, b: (0, b[n])),
                  pl.BlockSpec((NG, tq), lambda n, a, b: (0, b[n]))],
        out_specs=[pl.BlockSpec((tk, DM), lambda n, a, b: (a[n], 0)),
                   pl.BlockSpec((tk, DM), lambda n, a, b: (a[n], 0)),
                   pl.BlockSpec((NG, tk), lambda n, a, b: (0, a[n]))],
        scratch_shapes=[pltpu.VMEM((tk, DM), F32), pltpu.VMEM((tk, DM), F32), pltpu.VMEM((NG, tk, LANE), F32)])
    return pl.pallas_call(
        body, name="attn_bwd_dkv", grid_spec=gs,
        out_shape=[jax.ShapeDtypeStruct((t, DM), BF16), jax.ShapeDtypeStruct((t, DM), BF16), jax.ShapeDtypeStruct((NG, t), F32)],
        compiler_params=_cparams(("arbitrary",)))(oj, ii, qkv, qkv, qkv, ckcol, do, lserow, deltarow)


def _merge_fwd(ga, zb, att, proj, wa, wb, wc, *, tm=256):
    t = ga.shape[0]
    tm = min(tm, t)
    wspec = pl.BlockSpec((DM, D), lambda i: (0, 0))

    def body(ga_ref, zb_ref, att_ref, gate_ref, wa_ref, wb_ref, wc_ref, m_ref):
        acc = jnp.zeros((tm, D), F32)
        for b, (x_ref, w_ref) in enumerate(((ga_ref, wa_ref), (zb_ref, wb_ref), (att_ref, wc_ref))):
            acc = acc + _sigmoid(gate_ref[:, b * D:(b + 1) * D]) * _dot(x_ref[...], w_ref[...])
        m_ref[...] = acc.astype(BF16)

    return pl.pallas_call(
        body, name="merge_fwd", grid=(t // tm,),
        in_specs=[_rows(tm, DM), _rows(tm, DM), _rows(tm, DM), _rows(tm, 3 * D, 0), wspec, wspec, wspec],
        out_specs=_rows(tm, D), out_shape=jax.ShapeDtypeStruct((t, D), BF16),
        compiler_params=_cparams(("parallel",)))(ga, zb, att, proj, wa, wb, wc)


def _merge_bwd(dm, ga, zb, att, proj, wa, wb, wc, *, tm=256):
    t = ga.shape[0]
    tm = min(tm, t)
    wspec = pl.BlockSpec((DM, D), lambda i: (0, 0))

    def body(dm_ref, ga_ref, zb_ref, att_ref, gate_ref, wa_ref, wb_ref, wc_ref, dgate_ref, dga_ref, dzb_ref, datt_ref,
             dwa_ref, dwb_ref, dwc_ref):
        @pl.when(pl.program_id(0) == 0)
        def _():
            dwa_ref[...] = jnp.zeros_like(dwa_ref)
            dwb_ref[...] = jnp.zeros_like(dwb_ref)
            dwc_ref[...] = jnp.zeros_like(dwc_ref)

        dmv = dm_ref[...]
        branches = ((ga_ref, wa_ref, dga_ref, dwa_ref), (zb_ref, wb_ref, dzb_ref, dwb_ref),
                    (att_ref, wc_ref, datt_ref, dwc_ref))
        for b, (x_ref, w_ref, dx_ref, dw_ref) in enumerate(branches):
            xv, wv = x_ref[...], w_ref[...]
            y = _dot(xv, wv)
            g = _sigmoid(gate_ref[:, b * D:(b + 1) * D])
            dgate_ref[:, b * D:(b + 1) * D] = (dmv * y * g * (1.0 - g)).astype(BF16)
            dy = (dmv * g).astype(BF16)
            dx_ref[...] = _dot(dy, wv, NT)
            dw_ref[...] += _dot(xv, dy, TN)

    return pl.pallas_call(
        body, name="merge_bwd", grid=(t // tm,),
        in_specs=[_rows(tm, D), _rows(tm, DM), _rows(tm, DM), _rows(tm, DM), _rows(tm, 3 * D, 0), wspec, wspec, wspec],
        out_specs=[_rows(tm, 3 * D), _rows(tm, DM), _rows(tm, DM), _rows(tm, DM), wspec, wspec, wspec],
        out_shape=[jax.ShapeDtypeStruct((t, 3 * D), BF16)] + [jax.ShapeDtypeStruct((t, DM), F32)] * 3
        + [jax.ShapeDtypeStruct((DM, D), F32)] * 3,
        compiler_params=_cparams(("arbitrary",)))(dm, ga, zb, att, proj, wa, wb, wc)


def _heads_layout(cum):
    t = cum.shape[0]
    ckrow = cum[:, :NG].T
    return ckrow, jnp.broadcast_to(ckrow[:, :, None], (NG, t, LANE))


def _layer_fwd(x, mod, w):
    sh1, sc1, gt1, sh2, sc2, gt2 = (mod[k:k + 1] for k in range(NMOD))
    h1 = _norm_mod(x, w["mix_pre_g"], sc1, sh1, name="norm_mix")
    proj = _matmul(h1, w["w_in_p"], name="mm_proj")
    ga = _gmlp_fwd(proj, w["gmlp_ln_g"], w["gmlp_ln_b"], w["wsm"], w["bsx"])
    zc, zb = _conv_fwd(proj, w["conv_w"], w["conv_b"], w["conv_ln_g"], w["conv_ln_b"])
    cum = _fox_cum(proj, w["bfp"])
    ckrow, ckcol = _heads_layout(cum)
    qkv = _qkv_prep(proj)
    att, lse, lser = _attn_fwd(qkv, ckrow)
    merged = _merge_fwd(ga, zb, att, proj, w["w_a_out"], w["w_b_out"], w["w_c_out"])
    y1 = _matmul(merged, w["w_out"], name="mm_out")
    x2 = _resid(x, y1, gt1, w["mix_post_g"], name="resid_mix")
    h2 = _norm_mod(x2, w["mlp_pre_g"], sc2, sh2, name="norm_mlp")
    a, hid = _matmul(h2, w["mlp_w1"], name="mm_w1", b_slabs=True, extra_out=(BF16,),
                     epilogue=lambda acc: (acc, jnp.square(jnp.maximum(acc, 0.0))))
    y2 = _matmul(hid, w["mlp_w2"], name="mm_w2")
    x3 = _resid(x2, y2, gt2, w["mlp_post_g"], name="resid_mlp")
    saved = dict(x=x, h1=h1, proj=proj, ga=ga, zc=zc, zb=zb, qkv=qkv, ckrow=ckrow, ckcol=ckcol, att=att, lse=lse, lser=lser, merged=merged,
                 y1=y1, x2=x2, h2=h2, a=a, hid=hid, y2=y2)
    return x3, saved


def _layer_bwd(dx3, mod, w, s):
    sh1, sc1, gt1, sh2, sc2, gt2 = (mod[k:k + 1] for k in range(NMOD))
    g = {}
    dy2, dgt2, g["mlp_post_g"] = _resid_bwd(dx3, s["y2"], gt2, w["mlp_post_g"], name="resid_mlp_bwd")
    da = _matmul(dy2, w["mlp_w2"], tb=True, name="mm_dhid", out_dtype=BF16, extra=(s["a"],),
                 epilogue=lambda acc, a: (acc * (2.0 * jnp.maximum(a, 0.0)),))
    g["mlp_w2"] = _matmul(s["hid"], dy2, ta=True, name="mm_dw2")
    g["mlp_w1"] = _matmul(s["h2"], da, ta=True, name="mm_dw1", out_slabs=N_CHIPS)
    dh2 = _matmul(da, w["mlp_w1"], tb=True, name="mm_dh2", b_slabs=True)
    dx2, g["mlp_pre_g"], dsc2, dsh2 = _norm_bwd(dh2, dx3, s["x2"], w["mlp_pre_g"], sc2, name="norm_mlp_bwd")
    dy1, dgt1, g["mix_post_g"] = _resid_bwd(dx2, s["y1"], gt1, w["mix_post_g"], name="resid_mix_bwd")
    dmerged = _matmul(dy1, w["w_out"], tb=True, name="mm_dmerged")
    g["w_out"] = _matmul(s["merged"], dy1, ta=True, name="mm_dwout")
    dgate, dga, dzb, datt, g["w_a_out"], g["w_b_out"], g["w_c_out"] = _merge_bwd(
        dmerged, s["ga"], s["zb"], s["att"], s["proj"], w["w_a_out"], w["w_b_out"], w["w_c_out"])
    duv, g["gmlp_ws"], dbs, g["gmlp_ln_g"], g["gmlp_ln_b"] = _gmlp_bwd(
        dga, s["proj"], w["gmlp_ln_g"], w["gmlp_ln_b"], w["wsm"], w["wsmt"], w["bsx"])
    g["gmlp_bs"] = dbs[:, :NG].T
    dzc, g["conv_ln_g"], g["conv_ln_b"] = _conv_bwd_ln(dzb, s["zc"], w["conv_ln_g"], w["conv_ln_b"])
    dglu, dcw, g["conv_b"] = _conv_bwd(dzc, s["proj"], w["conv_w"])
    g["conv_w"] = dcw[:KW]
    dq, delta, dcq = _attn_bwd_dq(s["qkv"], s["ckrow"], s["att"], datt, s["lse"])
    dk, dv, dck = _attn_bwd_dkv(s["qkv"], s["ckcol"], datt, s["lser"], delta)
    dcum = jnp.pad((dcq + dck).T, ((0, 0), (0, LANE - NG)))
    df, dbf = _fox_cum_bwd(dcum, s["proj"], w["bfp"])
    g["fox_bf"] = dbf[0, :NG]
    dproj = jnp.concatenate([dgate, duv, dglu, dq, dk, dv, df], axis=1)
    g["w_in_p"] = _matmul(s["h1"], dproj, ta=True, name="mm_dwin")
    dh1 = _matmul(dproj, w["w_in_p"], tb=True, name="mm_dh1")
    dx, g["mix_pre_g"], dsc1, dsh1 = _norm_bwd(dh1, dx2, s["x"], w["mix_pre_g"], sc1, name="norm_mix_bwd")
    dmod = jnp.concatenate([dsh1, dsc1, dgt1, dsh2, dsc2, dgt2], axis=0)
    return dx, g, dmod


def _position():
    return lax.axis_index("x"), lax.axis_index("y"), lax.axis_index("c")


def _all_gather8(v):
    m_per, n = v.shape

    def body(x_ref, out_ref, send_sems, recv_sems, local_sem):
        x, y, c = _position()
        me, sibling = (x, y, c), (x, y, 1 - c)
        chips = [(1 - x, y), (x, 1 - y), (1 - x, 1 - y)]

        def rows(px, py, pc):
            return out_ref.at[pl.ds((4 * px + 2 * py + pc) * m_per, m_per), :]

        def copy(k, block, to, src=None):
            return pltpu.make_async_remote_copy(
                src_ref=rows(*block) if src is None else src, dst_ref=rows(*block), send_sem=send_sems.at[k],
                recv_sem=recv_sems.at[k], device_id=to, device_id_type=MESH)

        mine = pltpu.make_async_copy(x_ref, rows(*me), local_sem)
        mine.start()
        first = [copy(0, me, sibling, src=x_ref)]
        first += [copy(1 + j, me, (*chip, c), src=x_ref) for j, chip in enumerate(chips)]
        for cp in first:
            cp.start()
        passed = [copy(4 + j, (*chip, c), sibling) for j, chip in enumerate(chips)]
        for j, chip in enumerate(chips):
            copy(1 + j, (*chip, c), me).wait_recv()
            passed[j].start()
        copy(0, sibling, me).wait_recv()
        for j, chip in enumerate(chips):
            copy(4 + j, (*chip, 1 - c), me).wait_recv()
        for cp in first + passed:
            cp.wait_send()
        mine.wait()

    out = pl.pallas_call(
        body, name="all_gather8", out_shape=jax.ShapeDtypeStruct((N_DEV * m_per, n), v.dtype),
        in_specs=[pl.BlockSpec(memory_space=pltpu.VMEM)], out_specs=pl.BlockSpec(memory_space=pltpu.VMEM),
        scratch_shapes=[pltpu.SemaphoreType.DMA((7,)), pltpu.SemaphoreType.DMA((7,)), pltpu.SemaphoreType.DMA],
        compiler_params=pltpu.CompilerParams(vmem_limit_bytes=VMEM_LIMIT),
    )(v)
    return out.reshape(N_DEV, m_per, n)


def _half(c, rows):
    return pl.ds(c * (rows // 2), rows // 2)


def _ag_ici(shards, *, name):
    nw = len(shards)

    def body(*refs):
        w_refs, out_refs = refs[:nw], refs[nw:2 * nw]
        send_sems, recv_sems = refs[2 * nw:]
        x, y, c = _position()
        chips = [(1 - x, y), (x, 1 - y), (1 - x, 1 - y)]
        def copy(wi, k, slab):
            rows = w_refs[wi].shape[0]
            px, py = chips[k]
            return pltpu.make_async_remote_copy(
                src_ref=w_refs[wi].at[_half(c, rows)], dst_ref=out_refs[wi].at[slab, _half(c, rows)],
                send_sem=send_sems.at[wi * 3 + k], recv_sem=recv_sems.at[wi * 3 + k], device_id=(px, py, c),
                device_id_type=MESH)

        sends = [copy(wi, k, 2 * x + y) for wi in range(nw) for k in range(3)]
        for cp in sends:
            cp.start()
        for wi in range(nw):
            for k, (px, py) in enumerate(chips):
                copy(wi, k, 2 * px + py).wait_recv()
        for cp in sends:
            cp.wait_send()

    return pl.pallas_call(
        body, name=name, out_shape=[jax.ShapeDtypeStruct((N_CHIPS,) + sh.shape, sh.dtype) for sh in shards],
        in_specs=[ANY] * nw, out_specs=[ANY] * nw,
        scratch_shapes=[pltpu.SemaphoreType.DMA((3 * nw,)), pltpu.SemaphoreType.DMA((3 * nw,))],
    )(*shards)


HBM = pl.BlockSpec(memory_space=pltpu.HBM)
SEM = pl.BlockSpec(memory_space=pltpu.SEMAPHORE)
EFFECT = pltpu.SideEffectType.DATAFLOW_SIDE_EFFECTING


def _ici_copies(kind, src_refs, land_refs, send_sems, recv_sems):
    x, y, c = _position()
    chips = [(1 - x, y), (x, 1 - y), (1 - x, 1 - y)]
    sends, recvs = [], []
    for wi, (src, land) in enumerate(zip(src_refs, land_refs)):
        for k, (px, py) in enumerate(chips):
            if kind == "gather":
                rows = src.shape[0]
                s_win = src.at[_half(c, rows)]
                there, here = land.at[2 * x + y, _half(c, rows)], land.at[2 * px + py, _half(c, rows)]
            else:
                s_win = src.at[2 * px + py]
                there = here = land.at[k]
            for dst, out in ((there, sends), (here, recvs)):
                out.append(pltpu.make_async_remote_copy(
                    src_ref=s_win, dst_ref=dst, send_sem=send_sems.at[wi * 3 + k], recv_sem=recv_sems.at[wi * 3 + k],
                    device_id=(px, py, c), device_id_type=MESH))
    return sends, recvs


def _ici_start(kind, srcs, land_shapes, after, *, name):
    nw = len(srcs)

    def body(*refs):
        src_refs, land_refs = refs[:nw], refs[nw:2 * nw]
        send_sems, recv_sems = refs[2 * nw + 1:2 * nw + 3]
        token = refs[-1]
        sends, _ = _ici_copies(kind, src_refs, land_refs, send_sems, recv_sems)
        for cp in sends:
            cp.start()
        token[...] = jnp.zeros_like(token)

    lands = [pltpu.with_memory_space_constraint(lax.empty(shp, s.dtype), pltpu.HBM) for shp, s in zip(land_shapes, srcs)]
    outs = pl.pallas_call(
        body, name=name,
        out_shape=(pltpu.SemaphoreType.DMA((3 * nw,)), pltpu.SemaphoreType.DMA((3 * nw,)),
                   *[pltpu.HBM(s.shape, s.dtype) for s in srcs], *[pltpu.HBM(shp, s.dtype) for shp, s in zip(land_shapes, srcs)],
                   jax.ShapeDtypeStruct((8, LANE), F32)),
        in_specs=[HBM] * (2 * nw) + [ANY], out_specs=(SEM, SEM, *[HBM] * (2 * nw), pl.BlockSpec(memory_space=pltpu.VMEM)),
        input_output_aliases={i: 2 + i for i in range(2 * nw)},
        compiler_params=pltpu.CompilerParams(has_side_effects=EFFECT),
    )(*[pltpu.with_memory_space_constraint(s, pltpu.HBM) for s in srcs], *lands, after)
    return outs[0], outs[1], outs[2:2 + nw], outs[2 + nw:2 + 2 * nw], outs[-1]


def _ici_wait(kind, send_sems, recv_sems, srcs, lands, after, *, name):
    nw = len(srcs)

    def body(*refs):
        src_refs, land_refs = refs[:nw], refs[nw:2 * nw]
        s_sems, r_sems = refs[2 * nw:2 * nw + 2]
        sends, recvs = _ici_copies(kind, src_refs, land_refs, s_sems, r_sems)
        for cp in sends:
            cp.wait_send()
        for cp in recvs:
            cp.wait_recv()

    outs = pl.pallas_call(
        body, name=name,
        out_shape=(*[pltpu.HBM(s.shape, s.dtype) for s in srcs], *[pltpu.HBM(a.shape, a.dtype) for a in lands]),
        in_specs=[HBM] * (2 * nw) + [SEM, SEM, ANY], out_specs=tuple([HBM] * (2 * nw)),
        input_output_aliases={i: i for i in range(2 * nw)},
        compiler_params=pltpu.CompilerParams(has_side_effects=EFFECT),
    )(*srcs, *lands, send_sems, recv_sems, after)
    return outs[nw:]


def _ag_d2d(lands, *, name):
    nw = len(lands)

    def body(*refs):
        out_refs = refs[nw:2 * nw]
        send_sems, recv_sems = refs[2 * nw:]
        x, y, c = _position()
        chips = [(1 - x, y), (x, 1 - y), (1 - x, 1 - y)]
        copies = []
        for wi in range(nw):
            rows = out_refs[wi].shape[1]
            for k, (px, py) in enumerate(chips):
                win = out_refs[wi].at[2 * px + py, _half(c, rows)]
                copies.append(pltpu.make_async_remote_copy(
                    src_ref=win, dst_ref=win, send_sem=send_sems.at[wi * 3 + k], recv_sem=recv_sems.at[wi * 3 + k],
                    device_id=(x, y, 1 - c), device_id_type=MESH))
        for cp in copies:
            cp.start()
        for wi in range(nw):
            rows = out_refs[wi].shape[1]
            for k, (px, py) in enumerate(chips):
                win = out_refs[wi].at[2 * px + py, _half(1 - c, rows)]
                pltpu.make_async_remote_copy(
                    src_ref=win, dst_ref=win, send_sem=send_sems.at[wi * 3 + k], recv_sem=recv_sems.at[wi * 3 + k],
                    device_id=(x, y, 1 - c), device_id_type=MESH).wait_recv()
        for cp in copies:
            cp.wait_send()

    return pl.pallas_call(
        body, name=name, out_shape=[jax.ShapeDtypeStruct(a.shape, a.dtype) for a in lands],
        in_specs=[ANY] * nw, out_specs=[ANY] * nw, input_output_aliases={i: i for i in range(nw)},
        scratch_shapes=[pltpu.SemaphoreType.DMA((3 * nw,)), pltpu.SemaphoreType.DMA((3 * nw,))],
    )(*lands)


def _rs_swap(gs, *, name):
    nw = len(gs)

    def body(*refs):
        g_refs, out_refs = refs[:nw], refs[nw:2 * nw]
        send_sems, recv_sems = refs[2 * nw:]
        x, y, c = _position()
        copies = []
        for wi in range(nw):
            rows = g_refs[wi].shape[1]
            for j in range(N_CHIPS):
                copies.append(pltpu.make_async_remote_copy(
                    src_ref=g_refs[wi].at[j, _half(1 - c, rows)], dst_ref=out_refs[wi].at[j],
                    send_sem=send_sems.at[wi * N_CHIPS + j], recv_sem=recv_sems.at[wi * N_CHIPS + j],
                    device_id=(x, y, 1 - c), device_id_type=MESH))
        for cp in copies:
            cp.start()
        for cp in copies:
            cp.wait()

    return pl.pallas_call(
        body, name=name,
        out_shape=[jax.ShapeDtypeStruct((N_CHIPS, g.shape[1] // 2, g.shape[2]), g.dtype) for g in gs],
        in_specs=[ANY] * nw, out_specs=[ANY] * nw,
        scratch_shapes=[pltpu.SemaphoreType.DMA((N_CHIPS * nw,)), pltpu.SemaphoreType.DMA((N_CHIPS * nw,))],
    )(*gs)


def _scatter_chips(sps):
    nw = len(sps)

    def body(*refs):
        s_refs, out_refs = refs[:nw], refs[nw:2 * nw]
        send_sems, recv_sems = refs[2 * nw:]
        x, y, c = _position()
        chips = [(1 - x, y), (x, 1 - y), (1 - x, 1 - y)]
        copies = [pltpu.make_async_remote_copy(
            src_ref=s_refs[wi].at[2 * px + py], dst_ref=out_refs[wi].at[k], send_sem=send_sems.at[wi * 3 + k],
            recv_sem=recv_sems.at[wi * 3 + k], device_id=(px, py, c), device_id_type=MESH)
            for wi in range(nw) for k, (px, py) in enumerate(chips)]
        for cp in copies:
            cp.start()
        for cp in copies:
            cp.wait()

    return pl.pallas_call(
        body, name="rs_scatter_chips", out_shape=[jax.ShapeDtypeStruct((3,) + sp.shape[1:], sp.dtype) for sp in sps],
        in_specs=[ANY] * nw, out_specs=[ANY] * nw,
        scratch_shapes=[pltpu.SemaphoreType.DMA((3 * nw,)), pltpu.SemaphoreType.DMA((3 * nw,))],
    )(*sps)


def _swap_reduced(reds):
    nw = len(reds)

    def body(*refs):
        r_refs, out_refs = refs[:nw], refs[nw:2 * nw]
        send_sems, recv_sems = refs[2 * nw:]
        x, y, c = _position()
        copies = [pltpu.make_async_remote_copy(
            src_ref=r_refs[wi], dst_ref=out_refs[wi], send_sem=send_sems.at[wi], recv_sem=recv_sems.at[wi],
            device_id=(x, y, 1 - c), device_id_type=MESH) for wi in range(nw)]
        for cp in copies:
            cp.start()
        for cp in copies:
            cp.wait()

    return pl.pallas_call(
        body, name="rs_swap_reduced", out_shape=[jax.ShapeDtypeStruct(r.shape, r.dtype) for r in reds],
        in_specs=[ANY] * nw, out_specs=[ANY] * nw,
        scratch_shapes=[pltpu.SemaphoreType.DMA((nw,)), pltpu.SemaphoreType.DMA((nw,))],
    )(*reds)


def _row_tile(rows, cols):
    tr = rows
    while tr * cols * 4 > (2 << 20) and tr % 32 == 0:
        tr //= 2
    return tr


def _add_own_half(g, recv, c, *, name):
    nj, h, w = recv.shape
    tr = _row_tile(h, w)
    nb = h // tr

    def body(c_ref, g_ref, r_ref, o_ref, ob_ref):
        sm = g_ref[0] + r_ref[0]
        o_ref[0] = sm
        ob_ref[0] = sm.astype(BF16)

    spec = pl.BlockSpec((1, tr, w), lambda j, i, cc: (j, i, 0))
    gs = pltpu.PrefetchScalarGridSpec(
        num_scalar_prefetch=1, grid=(nj, nb),
        in_specs=[pl.BlockSpec((1, tr, w), lambda j, i, cc: (j, cc[0] * nb + i, 0)), spec],
        out_specs=[spec, spec])
    return pl.pallas_call(body, name=name, grid_spec=gs,
                          out_shape=[jax.ShapeDtypeStruct((nj, h, w), F32), jax.ShapeDtypeStruct((nj, h, w), BF16)],
                          compiler_params=_cparams(("parallel", "parallel")))(jnp.reshape(c, (1,)).astype(I32), g, recv)


def _add_own_chip(sp, recv, j, *, name):
    _, r, w = sp.shape
    tr = _row_tile(r, w)

    def body(j_ref, s_ref, r_ref, o_ref):
        o_ref[...] = ((s_ref[0] + r_ref[0].astype(F32)) + r_ref[1].astype(F32)) + r_ref[2].astype(F32)

    gs = pltpu.PrefetchScalarGridSpec(
        num_scalar_prefetch=1, grid=(r // tr,),
        in_specs=[pl.BlockSpec((1, tr, w), lambda i, jj: (jj[0], i, 0)), pl.BlockSpec((3, tr, w), lambda i, jj: (0, i, 0))],
        out_specs=pl.BlockSpec((tr, w), lambda i, jj: (i, 0)))
    return pl.pallas_call(body, name=name, grid_spec=gs, out_shape=jax.ShapeDtypeStruct((r, w), F32),
                          compiler_params=_cparams(("parallel",)))(jnp.reshape(j, (1,)).astype(I32), sp, recv)


def _sum8(v):
    _, m, n = v.shape

    def body(v_ref, o_ref):
        acc = v_ref[0]
        for k in range(1, N_DEV):
            acc = acc + v_ref[k]
        o_ref[...] = acc

    return pl.pallas_call(body, name="sum8", grid=(1,), in_specs=[pl.BlockSpec((N_DEV, m, n), lambda i: (0, 0, 0))],
                          out_specs=pl.BlockSpec((m, n), lambda i: (0, 0)), out_shape=jax.ShapeDtypeStruct((m, n), F32),
                          compiler_params=_cparams(("arbitrary",)))(v)


def _ada_mod(c_all, ada_w, ada_b_loc, *, tn=512):
    nl, _, ncol = ada_w.shape

    def body(c_ref, w_ref, b_ref, o_ref):
        cv = c_ref[...]
        ca = (cv * _sigmoid(cv)).astype(BF16)
        o_ref[0] = _dot(ca, w_ref[0].astype(BF16)) + b_ref[0]

    return pl.pallas_call(
        body, name="ada_mod", grid=(nl, ncol // tn),
        in_specs=[pl.BlockSpec((N_DEV, D), lambda l, j: (0, 0)), pl.BlockSpec((1, D, tn), lambda l, j: (l, 0, j)),
                  pl.BlockSpec((1, 1, tn), lambda l, j: (l, 0, j))],
        out_specs=pl.BlockSpec((1, N_DEV, tn), lambda l, j: (l, 0, j)),
        out_shape=jax.ShapeDtypeStruct((nl, N_DEV, ncol), F32),
        compiler_params=_cparams(("parallel", "parallel")))(c_all, ada_w, ada_b_loc)


def _ada_grad(c_pad, dmod_pad, *, tn=512):
    nl, nb, ncol = dmod_pad.shape

    def body(c_ref, d_ref, o_ref):
        cv = c_ref[...]
        ca = (cv * _sigmoid(cv)).astype(BF16)
        o_ref[0] = _dot(ca, d_ref[0].astype(BF16), TN)

    return pl.pallas_call(
        body, name="ada_grad", grid=(nl, ncol // tn),
        in_specs=[pl.BlockSpec((nb, D), lambda l, j: (0, 0)), pl.BlockSpec((1, nb, tn), lambda l, j: (l, 0, j))],
        out_specs=pl.BlockSpec((1, D, tn), lambda l, j: (l, 0, j)),
        out_shape=jax.ShapeDtypeStruct((nl, D, ncol), F32),
        compiler_params=_cparams(("parallel", "parallel")))(c_pad, dmod_pad)


def _adamw(w, g, m, v, *, name):
    shape = w.shape
    if w.ndim == 3:
        lead, rows, cols = shape
    else:
        lead, (rows, cols) = 1, shape
    w3, g3, m3, v3 = (a.reshape(lead, rows, cols) for a in (w, g, m, v))
    tr = rows
    if rows * cols * 4 > (2 << 20):
        tr = next(cand for cand in (256, 128, 64, 8) if rows % cand == 0)
    c1 = 1.0 - ADAM_B1 ** ADAM_STEP
    c2 = 1.0 - ADAM_B2 ** ADAM_STEP

    def body(w_ref, g_ref, m_ref, v_ref, d_ref, nm_ref, nv_ref):
        gv = g_ref[...]
        nm = ADAM_B1 * m_ref[...] + (1.0 - ADAM_B1) * gv
        nv = ADAM_B2 * v_ref[...] + (1.0 - ADAM_B2) * (gv * gv)
        nm_ref[...] = nm
        nv_ref[...] = nv
        d_ref[...] = -ADAM_LR * ((nm / c1) / (jnp.sqrt(nv / c2) + ADAM_EPS) + ADAM_WD * w_ref[...])

    spec = pl.BlockSpec((1, tr, cols), lambda l, i: (l, i, 0))
    outs = pl.pallas_call(
        body, name=name, grid=(lead, rows // tr), in_specs=[spec] * 4, out_specs=[spec] * 3,
        out_shape=[jax.ShapeDtypeStruct((lead, rows, cols), F32)] * 3,
        compiler_params=_cparams(("parallel", "parallel")))(w3, g3, m3, v3)
    return tuple(o.reshape(shape) for o in outs)


def _adamw_big(w, g_own, g_sib, m, v, c, *, name):
    _, rows, cols = w.shape
    tr = _row_tile(rows // 2, cols)
    nbh = rows // 2 // tr
    c1 = 1.0 - ADAM_B1 ** ADAM_STEP
    c2 = 1.0 - ADAM_B2 ** ADAM_STEP

    def body(c_ref, w_ref, o0_ref, s0_ref, o1_ref, s1_ref, m_ref, v_ref, g_ref, d_ref, nm_ref, nv_ref):
        mine = pl.program_id(1) // nbh == c_ref[0]
        gv = jnp.where(pl.program_id(0) == 0, jnp.where(mine, o0_ref[...], s0_ref[...]),
                       jnp.where(mine, o1_ref[...], s1_ref[...]))
        nm = ADAM_B1 * m_ref[0] + (1.0 - ADAM_B1) * gv
        nv = ADAM_B2 * v_ref[0] + (1.0 - ADAM_B2) * (gv * gv)
        g_ref[0] = gv
        nm_ref[0] = nm
        nv_ref[0] = nv
        d_ref[0] = -ADAM_LR * ((nm / c1) / (jnp.sqrt(nv / c2) + ADAM_EPS) + ADAM_WD * w_ref[0])

    def source(layer, own):
        def index(l, i, cc):
            active = jnp.logical_and(l == layer, (i // nbh == cc[0]) == own)
            return (jnp.where(active, i % nbh, 0), 0)
        return pl.BlockSpec((tr, cols), index)

    spec = pl.BlockSpec((1, tr, cols), lambda l, i, cc: (l, i, 0))
    gs = pltpu.PrefetchScalarGridSpec(
        num_scalar_prefetch=1, grid=(2, 2 * nbh),
        in_specs=[spec, source(0, True), source(0, False), source(1, True), source(1, False), spec, spec],
        out_specs=[spec] * 4)
    return pl.pallas_call(
        body, name=name, grid_spec=gs, out_shape=[jax.ShapeDtypeStruct(w.shape, F32)] * 4,
        compiler_params=_cparams(("parallel", "parallel")))(
            jnp.reshape(c, (1,)).astype(I32), w, g_own[0], g_sib[0], g_own[1], g_sib[1], m, v)


SMALL = (("mix_pre_g", (2, D)), ("mix_post_g", (2, D)), ("mlp_pre_g", (2, D)), ("mlp_post_g", (2, D)),
         ("gmlp_ln_g", (2, DM)), ("gmlp_ln_b", (2, DM)), ("gmlp_ws", (2, NG, CH, CH)), ("gmlp_bs", (2, NG, CH)),
         ("conv_b", (2, DM)), ("conv_ln_g", (2, DM)), ("conv_ln_b", (2, DM)), ("fox_bf", (2, NG)))


def _pack_rows(arrays, mult=8):
    flat = []
    for a in arrays:
        f = a.reshape(-1).astype(F32)
        pad = (-f.shape[0]) % LANE
        flat.append(jnp.pad(f, (0, pad)) if pad else f)
    cat = jnp.concatenate(flat)
    rows = cat.shape[0] // LANE
    pad_rows = (-rows) % mult
    if pad_rows:
        cat = jnp.pad(cat, (0, pad_rows * LANE))
    return cat.reshape(-1, LANE)


def _unpack_rows(buf, shapes):
    flat = buf.reshape(-1)
    out, off = [], 0
    for shp in shapes:
        size = 1
        for d in shp:
            size *= d
        out.append(flat[off:off + size].reshape(shp))
        off += size + ((-size) % LANE)
    return out


def _assemble_w_in(w_in_full):
    uv_glu_qkv = w_in_full[:, :3584]
    f = w_in_full[:, 3584:3592]
    gate = w_in_full[:, 3592:]
    fpad = jnp.zeros((D, D_INP - C_F - NG), w_in_full.dtype)
    return jnp.concatenate([gate, uv_glu_qkv, f, fpad], axis=1)


def _disassemble_w_in(g_p):
    return jnp.concatenate([g_p[:, C_UV:C_F], g_p[:, C_F:C_F + NG], g_p[:, :C_UV]], axis=1)


def kernel(x, c, ada_w, ada_b, mix_pre_g, mix_post_g, mlp_pre_g, mlp_post_g, w_in, gmlp_ln_g, gmlp_ln_b, gmlp_ws, gmlp_bs, w_a_out, conv_w, conv_b, conv_ln_g, conv_ln_b, w_b_out, fox_bf, w_c_out, w_out, mlp_w1, mlp_w2, loss_target, m_ada_w, m_ada_b, m_mix_pre_g, m_mix_post_g, m_mlp_pre_g, m_mlp_post_g, m_w_in, m_gmlp_ln_g, m_gmlp_ln_b, m_gmlp_ws, m_gmlp_bs, m_w_a_out, m_conv_w, m_conv_b, m_conv_ln_g, m_conv_ln_b, m_w_b_out, m_fox_bf, m_w_c_out, m_w_out, m_mlp_w1, m_mlp_w2, v_ada_w, v_ada_b, v_mix_pre_g, v_mix_post_g, v_mlp_pre_g, v_mlp_post_g, v_w_in, v_gmlp_ln_g, v_gmlp_ln_b, v_gmlp_ws, v_gmlp_bs, v_w_a_out, v_conv_w, v_conv_b, v_conv_ln_g, v_conv_ln_b, v_w_b_out, v_fox_bf, v_w_c_out, v_w_out, v_mlp_w1, v_mlp_w2):
    weights = dict(ada_w=ada_w, ada_b=ada_b, mix_pre_g=mix_pre_g, mix_post_g=mix_post_g, mlp_pre_g=mlp_pre_g,
                   mlp_post_g=mlp_post_g, w_in=w_in, gmlp_ln_g=gmlp_ln_g, gmlp_ln_b=gmlp_ln_b, gmlp_ws=gmlp_ws,
                   gmlp_bs=gmlp_bs, w_a_out=w_a_out, conv_w=conv_w, conv_b=conv_b, conv_ln_g=conv_ln_g,
                   conv_ln_b=conv_ln_b, w_b_out=w_b_out, fox_bf=fox_bf, w_c_out=w_c_out, w_out=w_out, mlp_w1=mlp_w1,
                   mlp_w2=mlp_w2)
    mom_m = dict(ada_w=m_ada_w, ada_b=m_ada_b, mix_pre_g=m_mix_pre_g, mix_post_g=m_mix_post_g, mlp_pre_g=m_mlp_pre_g,
                 mlp_post_g=m_mlp_post_g, w_in=m_w_in, gmlp_ln_g=m_gmlp_ln_g, gmlp_ln_b=m_gmlp_ln_b, gmlp_ws=m_gmlp_ws,
                 gmlp_bs=m_gmlp_bs, w_a_out=m_w_a_out, conv_w=m_conv_w, conv_b=m_conv_b, conv_ln_g=m_conv_ln_g,
                 conv_ln_b=m_conv_ln_b, w_b_out=m_w_b_out, fox_bf=m_fox_bf, w_c_out=m_w_c_out, w_out=m_w_out,
                 mlp_w1=m_mlp_w1, mlp_w2=m_mlp_w2)
    mom_v = dict(ada_w=v_ada_w, ada_b=v_ada_b, mix_pre_g=v_mix_pre_g, mix_post_g=v_mix_post_g, mlp_pre_g=v_mlp_pre_g,
                 mlp_post_g=v_mlp_post_g, w_in=v_w_in, gmlp_ln_g=v_gmlp_ln_g, gmlp_ln_b=v_gmlp_ln_b, gmlp_ws=v_gmlp_ws,
                 gmlp_bs=v_gmlp_bs, w_a_out=v_w_a_out, conv_w=v_conv_w, conv_b=v_conv_b, conv_ln_g=v_conv_ln_g,
                 conv_ln_b=v_conv_ln_b, w_b_out=v_w_b_out, fox_bf=v_fox_bf, w_c_out=v_w_c_out, w_out=v_w_out,
                 mlp_w1=v_mlp_w1, mlp_w2=v_mlp_w2)
    order = list(weights)
    px, py, pc = _position()
    chip = 2 * px + py
    dev = 2 * chip + pc
    depth = ada_w.shape[0]
    t = x.shape[1]
    xl = x.reshape(t, D)
    tgt = loss_target.reshape(t, D)

    big_names = [b[0] for b in BIG]
    shards = [[weights[n][l].astype(BF16) for n in big_names] for l in range(depth)]
    land_shapes = [(N_CHIPS,) + sh.shape for sh in shards[0]]
    ag0 = _ici_start("gather", shards[0], land_shapes, c, name="ag0_start")

    small_in = _pack_rows([c, conv_w]) + ag0[4][0, 0]
    gathered = _all_gather8(small_in)
    c_all = gathered[:, :D // LANE, :].reshape(N_DEV, D)
    cw_rows = depth * KW * LANE // LANE
    conv_w_full = jnp.concatenate(
        [gathered[2 * j, D // LANE:D // LANE + cw_rows, :].reshape(depth, KW, LANE) for j in range(N_CHIPS)], axis=2)

    ncol = ada_w.shape[2]
    ada_b_loc = lax.dynamic_slice_in_dim(ada_b, chip * ncol, ncol, axis=1).reshape(depth, 1, ncol)
    mod_sh = _ada_mod(c_all, ada_w, ada_b_loc)
    mod_g = _all_gather8(mod_sh.reshape(-1, LANE)).reshape(N_DEV, depth, N_DEV, ncol)
    mod_all = jnp.concatenate([mod_g[2 * j] for j in range(N_CHIPS)], axis=2)
    mod_mine = lax.dynamic_index_in_dim(mod_all, dev, axis=1, keepdims=False)

    mods = [mod_mine[l].reshape(NMOD, D) for l in range(depth)]

    def layer_weights(l, lands):
        lands = _ag_d2d(lands, name="ag_d2d")
        gath = [lax.dynamic_update_slice(g, sh[None], (chip, 0, 0)) for g, sh in zip(lands, shards[l])]
        w = {}
        for (n, r, cdim, ax), g in zip(BIG, gath):
            if n == "mlp_w1":
                w[n] = g
            elif ax == 1:
                w[n] = g.transpose(1, 0, 2).reshape(r, N_CHIPS * cdim)
            else:
                w[n] = g.reshape(N_CHIPS * r, cdim)
        w["w_in_p"] = _assemble_w_in(w.pop("w_in"))
        for n in ("mix_pre_g", "mix_post_g", "mlp_pre_g", "mlp_post_g", "gmlp_ln_g", "gmlp_ln_b", "conv_b", "conv_ln_g",
                  "conv_ln_b"):
            w[n] = weights[n][l:l + 1]
        tril = jnp.tril(jnp.ones((CH, CH), F32))
        wsm = gmlp_ws[l] * tril
        w["wsm"] = wsm.astype(BF16)
        w["wsmt"] = jnp.swapaxes(wsm, 1, 2).astype(BF16)
        w["bsx"] = jnp.repeat(gmlp_bs[l].T, HD, axis=1)
        w["conv_w"] = conv_w_full[l]
        w["bfp"] = jnp.pad(fox_bf[l], (0, LANE - NG)).reshape(1, LANE)
        return w

    def slabs(gfull, n, r, cdim, ax):
        if n == "mlp_w1":
            return gfull
        if ax == 1:
            return gfull.reshape(gfull.shape[0], N_CHIPS, cdim).transpose(1, 0, 2)
        return gfull.reshape(N_CHIPS, r, cdim)

    def chip_sums(g):
        g["w_in"] = _disassemble_w_in(g.pop("w_in_p"))
        gs = [slabs(g[n], n, r, cdim, ax) for n, r, cdim, ax in BIG]
        from_sibling = _rs_swap(gs, name="rs_swap")
        return [_add_own_half(a, rv, pc, name="rs_add_half_" + n) for a, rv, n in zip(gs, from_sibling, big_names)]

    def reduce_rest(sums, from_chips):
        red = [_add_own_chip(sf, rv, chip, name="rs_add_chip_" + n) for (sf, _), rv, n in zip(sums, from_chips, big_names)]
        return red, _swap_reduced(red)

    assert depth == 2
    lands0 = _ici_wait("gather", ag0[0], ag0[1], ag0[2], ag0[3], mod_mine, name="ag0_wait")
    ag1 = _ici_start("gather", shards[1], land_shapes, lands0[0], name="ag1_start")
    mod0 = mods[0] + ag1[4][0, 0]
    layers = [layer_weights(0, lands0), None]
    saved = [None] * depth
    xs, saved[0] = _layer_fwd(xl, mod0, layers[0])
    layers[1] = layer_weights(1, _ici_wait("gather", ag1[0], ag1[1], ag1[2], ag1[3], xs, name="ag1_wait"))
    xs, saved[1] = _layer_fwd(xs, mods[1], layers[1])
    loss_local, dx = _loss_and_grad(xs, tgt)
    loss = lax.psum(loss_local, ("x", "y", "c"))
    grads, dmods = [None] * depth, [None] * depth
    dx, grads[1], dmods[1] = _layer_bwd(dx, mods[1], layers[1], saved[1])
    sums1 = chip_sums(grads[1])
    sbf1 = [sb for _, sb in sums1]
    slot_shapes = [(3,) + sb.shape[1:] for sb in sbf1]
    rs1 = _ici_start("scatter", sbf1, slot_shapes, dx, name="rs1_start")
    dx, grads[0], dmods[0] = _layer_bwd(dx, mod0 + rs1[4][0, 0], layers[0], saved[0])
    grad_x = dx.reshape(x.shape)
    red1 = reduce_rest(sums1, _ici_wait("scatter", rs1[0], rs1[1], rs1[2], rs1[3], dx, name="rs1_wait"))
    sums0 = chip_sums(grads[0])
    rs0 = _ici_start("scatter", [sb for _, sb in sums0], slot_shapes, red1[1][0], name="rs0_start")
    g_out = {}

    small_names = [n for n, _ in SMALL]
    small_list = [jnp.stack(dmods)] + [jnp.stack([grads[l][n] for l in range(depth)]) for n in small_names]
    small_list.append(jnp.stack([grads[l]["conv_w"] for l in range(depth)]))
    small_shapes = [(depth, NMOD * D)] + [shp for _, shp in SMALL] + [(depth, KW, DM)]
    small_pack = _pack_rows(small_list) + rs0[4][0, 0]
    small_all = _all_gather8(small_pack)
    small_sum = _unpack_rows(_sum8(small_all), small_shapes)
    g_out["ada_b"] = small_sum[0]
    for n, gs in zip(small_names, small_sum[1:-1]):
        g_out[n] = gs
    g_out["conv_w"] = lax.dynamic_slice_in_dim(small_sum[-1], chip * LANE, LANE, axis=2)
    dmod_all = small_all[:, :depth * NMOD * D // LANE, :].reshape(N_DEV, depth, NMOD * D)
    dmod_loc = lax.dynamic_slice_in_dim(dmod_all, chip * ncol, ncol, axis=2).transpose(1, 0, 2)
    g_out["ada_w"] = _ada_grad(jnp.pad(c_all, ((0, 8), (0, 0))), jnp.pad(dmod_loc, ((0, 0), (0, 8), (0, 0))))

    delta, new_m, new_v = {}, {}, {}
    delta["ada_w"], new_m["ada_w"], new_v["ada_w"] = _adamw(ada_w, g_out["ada_w"], m_ada_w, v_ada_w, name="adamw_ada_w")
    red0 = reduce_rest(sums0, _ici_wait("scatter", rs0[0], rs0[1], rs0[2], rs0[3], delta["ada_w"], name="rs0_wait"))
    reduced, from_sib = zip(red0, red1)
    for wi, n in enumerate(big_names):
        g_out[n], delta[n], new_m[n], new_v[n] = _adamw_big(
            weights[n], [reduced[l][wi] for l in range(depth)], [from_sib[l][wi] for l in range(depth)], mom_m[n], mom_v[n],
            pc, name="adamw_" + n)
    small_params = ["ada_b"] + small_names + ["conv_w"]
    packs = [_pack_rows([d[n] for n in small_params]) for d in (weights, g_out, mom_m, mom_v)]
    outs = _adamw(*packs, name="adamw_small")
    shapes = [weights[n].shape for n in small_params]
    for dst, buf in zip((delta, new_m, new_v), outs):
        for n, a in zip(small_params, _unpack_rows(buf, shapes)):
            dst[n] = a

    return (loss, grad_x, *[g_out[n] for n in order], *[delta[n] for n in order], *[new_m[n] for n in order],
            *[new_v[n] for n in order])
```

```python
import functools

import jax
import jax.numpy as jnp
from jax import lax
from jax.experimental import pallas as pl
from jax.experimental.pallas import tpu as pltpu

F32 = jnp.float32
BF16 = jnp.bfloat16
I32 = jnp.int32
MESH = pl.DeviceIdType.MESH
ANY = pl.BlockSpec(memory_space=pl.ANY)

D = 1024
DM = 512
NG = 8
CH = 128
KW = 31
HALO = 32
DFF = 4096
NMOD = 6
EPS = 1e-6
LANE = 128
N_CHIPS = 4
N_DEV = 8
C_GATE, C_UV, C_GLU, C_Q, C_K, C_V, C_F, D_INP = 0, 3072, 4096, 5120, 5632, 6144, 6656, 7168
D_IN = 6664
VMEM_LIMIT = 56 * 1024 * 1024

ADAM_LR, ADAM_B1, ADAM_B2, ADAM_EPS, ADAM_WD, ADAM_STEP = 0.001, 0.9, 0.999, 1e-08, 0.01, 10

BIG = (("w_in", 1024, 1666, 1), ("w_a_out", 512, 256, 1), ("w_b_out", 512, 256, 1), ("w_c_out", 512, 256, 1),
       ("w_out", 256, 1024, 0), ("mlp_w1", 1024, 1024, 1), ("mlp_w2", 1024, 1024, 0))


def _cparams(sem):
    return pltpu.CompilerParams(dimension_semantics=sem, vmem_limit_bytes=VMEM_LIMIT)


def _sigmoid(x):
    return jax.nn.sigmoid(x)


_GELU_K = 0.7978845608028654
_GELU_A = 0.044715


def _gelu(x):
    t = jnp.tanh(_GELU_K * (x + _GELU_A * x * x * x))
    return 0.5 * x * (1.0 + t)


def _gelu_grad(x):
    t = jnp.tanh(_GELU_K * (x + _GELU_A * x * x * x))
    return 0.5 * (1.0 + t) + 0.5 * x * (1.0 - t * t) * _GELU_K * (1.0 + 3.0 * _GELU_A * x * x)


def _mean(x):
    return jnp.mean(x, axis=-1, keepdims=True)


def _colsum(x):
    return jnp.sum(x, axis=0, keepdims=True)


def _dot(a, b, dims=((1,), (0,))):
    return lax.dot_general(a, b, (dims, ((), ())), preferred_element_type=F32)


NN = ((1,), (0,))
NT = ((1,), (1,))
TN = ((0,), (0,))


def _matmul(a, b, *, name, ta=False, tb=False, out_dtype=F32, tm=1024, tn=1024, tk=1024, epilogue=None, extra=(),
            extra_out=(), b_slabs=False, out_slabs=0):
    m, k = (a.shape[1], a.shape[0]) if ta else a.shape
    if b_slabs:
        ns, brows, bw = b.shape
        n = brows if tb else ns * bw
        assert (ns * bw if tb else brows) == k, (name, b.shape, k)
        tn, tk = (tn, bw) if tb else (bw, tk)
    else:
        n = b.shape[0] if tb else b.shape[1]
    tm, tn, tk = min(tm, m), min(tn, n), min(tk, k)
    assert m % tm == 0 and n % tn == 0 and k % tk == 0, (name, m, n, k, tm, tn, tk)
    assert not out_slabs or (n // out_slabs == tn and epilogue is None), name
    nk = k // tk
    dims = ((0 if ta else 1,), (1 if tb else 0,))
    n_extra = len(extra)
    out_dtypes = (out_dtype,) + tuple(extra_out)

    def body(a_ref, b_ref, *rest):
        extra_refs = rest[:n_extra]
        out_refs = rest[n_extra:n_extra + len(out_dtypes)]
        kk = pl.program_id(2)
        part = _dot(a_ref[...].astype(BF16), b_ref[...].astype(BF16), dims)

        def finish(acc):
            outs = (acc,) if epilogue is None else epilogue(acc, *[r[...] for r in extra_refs])
            for o_ref, o in zip(out_refs, outs):
                o_ref[...] = o.astype(o_ref.dtype)

        if nk == 1:
            finish(part)
        else:
            acc_ref = rest[-1]

            @pl.when(kk == 0)
            def _():
                acc_ref[...] = part

            @pl.when(jnp.logical_and(kk > 0, kk < nk - 1))
            def _():
                acc_ref[...] += part

            @pl.when(kk == nk - 1)
            def _():
                finish(acc_ref[...] + part)

    a_spec = pl.BlockSpec((tk, tm), lambda i, j, kk: (kk, i)) if ta else pl.BlockSpec((tm, tk), lambda i, j, kk: (i, kk))
    if b_slabs and tb:
        b_spec = pl.BlockSpec((None, tn, tk), lambda i, j, kk: (kk, j, 0))
    elif b_slabs:
        b_spec = pl.BlockSpec((None, tk, tn), lambda i, j, kk: (j, kk, 0))
    else:
        b_spec = pl.BlockSpec((tn, tk), lambda i, j, kk: (j, kk)) if tb else pl.BlockSpec((tk, tn), lambda i, j, kk: (kk, j))
    if out_slabs:
        o_spec = pl.BlockSpec((None, tm, tn), lambda i, j, kk: (j, i, 0))
        o_shape = (out_slabs, m, tn)
    else:
        o_spec = pl.BlockSpec((tm, tn), lambda i, j, kk: (i, j))
        o_shape = (m, n)
    outs = pl.pallas_call(
        body, name=name, grid=(m // tm, n // tn, nk),
        in_specs=[a_spec, b_spec] + [o_spec] * n_extra,
        out_specs=[o_spec] * len(out_dtypes),
        out_shape=[jax.ShapeDtypeStruct(o_shape, dt) for dt in out_dtypes],
        scratch_shapes=[pltpu.VMEM((tm, tn), F32)] if nk > 1 else [],
        compiler_params=_cparams(("parallel", "parallel", "arbitrary")),
    )(a, b, *extra)
    return outs[0] if len(outs) == 1 else outs


def _rows(tm, n, col=0):
    return pl.BlockSpec((tm, n), lambda i: (i, col))


def _vec(n):
    return pl.BlockSpec((1, n), lambda i: (0, 0))


def _norm_mod(x, g, sc, sh, *, name, tm=256):
    t = x.shape[0]
    tm = min(tm, t)

    def body(x_ref, g_ref, sc_ref, sh_ref, h_ref):
        xv = x_ref[...]
        inv = lax.rsqrt(_mean(xv * xv) + EPS)
        h_ref[...] = ((xv * inv * g_ref[...]) * (1.0 + sc_ref[...]) + sh_ref[...]).astype(BF16)

    return pl.pallas_call(
        body, name=name, grid=(t // tm,), in_specs=[_rows(tm, D), _vec(D), _vec(D), _vec(D)],
        out_specs=_rows(tm, D), out_shape=jax.ShapeDtypeStruct((t, D), BF16),
        compiler_params=_cparams(("parallel",)))(x, g, sc, sh)


def _resid(x, y, gt, gp, *, name, tm=256):
    t = x.shape[0]
    tm = min(tm, t)

    def body(x_ref, y_ref, gt_ref, gp_ref, o_ref):
        yv = y_ref[...]
        inv = lax.rsqrt(_mean(yv * yv) + EPS)
        o_ref[...] = x_ref[...] + gt_ref[...] * (yv * inv * gp_ref[...])

    return pl.pallas_call(
        body, name=name, grid=(t // tm,), in_specs=[_rows(tm, D), _rows(tm, D), _vec(D), _vec(D)],
        out_specs=_rows(tm, D), out_shape=jax.ShapeDtypeStruct((t, D), F32),
        compiler_params=_cparams(("parallel",)))(x, y, gt, gp)


def _resid_bwd(dx, y, gt, gp, *, name, tm=256):
    t = dx.shape[0]
    tm = min(tm, t)

    def body(dx_ref, y_ref, gt_ref, gp_ref, dy_ref, dgt_ref, dgp_ref):
        @pl.when(pl.program_id(0) == 0)
        def _():
            dgt_ref[...] = jnp.zeros_like(dgt_ref)
            dgp_ref[...] = jnp.zeros_like(dgp_ref)

        dxv, yv, gp_v = dx_ref[...], y_ref[...], gp_ref[...]
        inv = lax.rsqrt(_mean(yv * yv) + EPS)
        yh = yv * inv
        dgt_ref[...] += _colsum(dxv * (yh * gp_v))
        dr = dxv * gt_ref[...]
        dgp_ref[...] += _colsum(dr * yh)
        dyn = dr * gp_v
        dy_ref[...] = (inv * (dyn - yh * _mean(dyn * yh))).astype(BF16)

    return pl.pallas_call(
        body, name=name, grid=(t // tm,), in_specs=[_rows(tm, D), _rows(tm, D), _vec(D), _vec(D)],
        out_specs=[_rows(tm, D), _vec(D), _vec(D)],
        out_shape=[jax.ShapeDtypeStruct((t, D), BF16), jax.ShapeDtypeStruct((1, D), F32),
                   jax.ShapeDtypeStruct((1, D), F32)],
        compiler_params=_cparams(("arbitrary",)))(dx, y, gt, gp)


def _norm_bwd(dh, dx_res, x, g, sc, *, name, tm=256):
    t = dh.shape[0]
    tm = min(tm, t)

    def body(dh_ref, dxr_ref, x_ref, g_ref, sc_ref, dx_ref, dg_ref, dsc_ref, dsh_ref):
        @pl.when(pl.program_id(0) == 0)
        def _():
            dg_ref[...] = jnp.zeros_like(dg_ref)
            dsc_ref[...] = jnp.zeros_like(dsc_ref)
            dsh_ref[...] = jnp.zeros_like(dsh_ref)

        dhv, xv, gv = dh_ref[...], x_ref[...], g_ref[...]
        inv = lax.rsqrt(_mean(xv * xv) + EPS)
        xh = xv * inv
        dsh_ref[...] += _colsum(dhv)
        dsc_ref[...] += _colsum(dhv * (xh * gv))
        dn = dhv * (1.0 + sc_ref[...])
        dg_ref[...] += _colsum(dn * xh)
        dxh = dn * gv
        dx_ref[...] = inv * (dxh - xh * _mean(dxh * xh)) + dxr_ref[...]

    vec_out = jax.ShapeDtypeStruct((1, D), F32)
    return pl.pallas_call(
        body, name=name, grid=(t // tm,), in_specs=[_rows(tm, D), _rows(tm, D), _rows(tm, D), _vec(D), _vec(D)],
        out_specs=[_rows(tm, D), _vec(D), _vec(D), _vec(D)],
        out_shape=[jax.ShapeDtypeStruct((t, D), F32), vec_out, vec_out, vec_out],
        compiler_params=_cparams(("arbitrary",)))(dh, dx_res, x, g, sc)


def _loss_and_grad(x, target, *, tm=256):
    t = x.shape[0]
    tm = min(tm, t)

    def body(x_ref, t_ref, loss_ref, dx_ref):
        @pl.when(pl.program_id(0) == 0)
        def _():
            loss_ref[...] = jnp.zeros_like(loss_ref)

        e = x_ref[...] - t_ref[...]
        dx_ref[...] = e * (1.0 / D)
        s = jnp.sum(jnp.sum(e * e, axis=1, keepdims=True), axis=0, keepdims=True) * (0.5 / D)
        loss_ref[...] += jnp.broadcast_to(s, loss_ref.shape)

    loss, dx = pl.pallas_call(
        body, name="loss", grid=(t // tm,), in_specs=[_rows(tm, D), _rows(tm, D)],
        out_specs=[pl.BlockSpec((8, LANE), lambda i: (0, 0)), _rows(tm, D)],
        out_shape=[jax.ShapeDtypeStruct((8, LANE), F32), jax.ShapeDtypeStruct((t, D), F32)],
        compiler_params=_cparams(("arbitrary",)))(x, target)
    return loss[0, 0], dx


def _gmlp_core(uv, lng, lnb, ws_ref, bsx):
    tm = uv.shape[0]
    gu = _gelu(uv[:, :DM])
    gv = _gelu(uv[:, DM:])
    mu = _mean(gv)
    vc = gv - mu
    rstd = lax.rsqrt(_mean(vc * vc) + EPS)
    vh = vc * rstd
    vln = vh * lng + lnb
    lane = lax.broadcasted_iota(I32, (CH, LANE), 1)
    sv_rows = []
    for nchunk in range(tm // CH):
        vb = vln[nchunk * CH:(nchunk + 1) * CH].astype(BF16)
        cols = []
        for cb in range(DM // LANE):
            vcb = vb[:, cb * LANE:(cb + 1) * LANE]
            lo = _dot(ws_ref[2 * cb], vcb)
            hi = _dot(ws_ref[2 * cb + 1], vcb)
            cols.append(jnp.where(lane < 64, lo, hi))
        sv_rows.append(jnp.concatenate(cols, axis=1) + bsx)
    sv = jnp.concatenate(sv_rows, axis=0) if len(sv_rows) > 1 else sv_rows[0]
    return gu, vh, rstd, vln, sv


def _gmlp_fwd(proj, lng, lnb, wsm, bsx, *, tm=256):
    t = proj.shape[0]
    tm = min(tm, t)

    def body(uv_ref, lng_ref, lnb_ref, ws_ref, bs_ref, ga_ref):
        gu, _, _, _, sv = _gmlp_core(uv_ref[...], lng_ref[...], lnb_ref[...], ws_ref, bs_ref[...])
        ga_ref[...] = (gu * sv).astype(BF16)

    return pl.pallas_call(
        body, name="gmlp_fwd", grid=(t // tm,),
        in_specs=[_rows(tm, 2 * DM, C_UV // (2 * DM)), _vec(DM), _vec(DM),
                  pl.BlockSpec((NG, CH, CH), lambda i: (0, 0, 0)), pl.BlockSpec((CH, DM), lambda i: (0, 0))],
        out_specs=_rows(tm, DM), out_shape=jax.ShapeDtypeStruct((t, DM), BF16),
        compiler_params=_cparams(("parallel",)))(proj, lng, lnb, wsm, bsx)


def _gmlp_bwd(dga, proj, lng, lnb, wsm, wsmt, bsx, *, tm=256):
    t = proj.shape[0]
    tm = min(tm, t)

    def body(dga_ref, uv_ref, lng_ref, lnb_ref, ws_ref, wst_ref, bs_ref, duv_ref, dws_ref, dbs_ref, dlng_ref, dlnb_ref,
             dbsx_ref):
        i = pl.program_id(0)

        @pl.when(i == 0)
        def _():
            dws_ref[...] = jnp.zeros_like(dws_ref)
            dbsx_ref[...] = jnp.zeros_like(dbsx_ref)
            dlng_ref[...] = jnp.zeros_like(dlng_ref)
            dlnb_ref[...] = jnp.zeros_like(dlnb_ref)

        uv = uv_ref[...]
        lng_v = lng_ref[...]
        gu, vh, rstd, vln, sv = _gmlp_core(uv, lng_v, lnb_ref[...], ws_ref, bs_ref[...])
        dga_v = dga_ref[...]
        dgu = dga_v * sv
        dsv = dga_v * gu
        lane = lax.broadcasted_iota(I32, (CH, LANE), 1)
        tril = lax.broadcasted_iota(I32, (CH, CH), 0) >= lax.broadcasted_iota(I32, (CH, CH), 1)
        dvln_rows = []
        for nchunk in range(tm // CH):
            rows = slice(nchunk * CH, (nchunk + 1) * CH)
            dbsx_ref[...] += dsv[rows]
            vb = vln[rows].astype(BF16)
            cols = []
            for cb in range(DM // LANE):
                cs = slice(cb * LANE, (cb + 1) * LANE)
                dsvb = dsv[rows, cs]
                vcb = vb[:, cs]
                dlo = jnp.where(lane < 64, dsvb, 0.0).astype(BF16)
                dhi = jnp.where(lane < 64, 0.0, dsvb).astype(BF16)
                dws_ref[2 * cb] += jnp.where(tril, _dot(dlo, vcb, NT), 0.0)
                dws_ref[2 * cb + 1] += jnp.where(tril, _dot(dhi, vcb, NT), 0.0)
                dsb = dsvb.astype(BF16)
                cols.append(jnp.where(lane < 64, _dot(wst_ref[2 * cb], dsb), _dot(wst_ref[2 * cb + 1], dsb)))
            dvln_rows.append(jnp.concatenate(cols, axis=1))
        dvln = jnp.concatenate(dvln_rows, axis=0) if len(dvln_rows) > 1 else dvln_rows[0]
        dlnb_ref[...] += _colsum(dvln)
        dlng_ref[...] += _colsum(dvln * vh)
        dvh = dvln * lng_v
        dgv = rstd * (dvh - _mean(dvh) - vh * _mean(dvh * vh))
        duv_ref[:, :DM] = (dgu * _gelu_grad(uv[:, :DM])).astype(BF16)
        duv_ref[:, DM:] = (dgv * _gelu_grad(uv[:, DM:])).astype(BF16)

        @pl.when(i == pl.num_programs(0) - 1)
        def _():
            ind = (lax.broadcasted_iota(I32, (DM, LANE), 0) // 64 == lax.broadcasted_iota(I32, (DM, LANE), 1)).astype(F32)
            dbs_ref[...] = jnp.dot(dbsx_ref[...], ind, preferred_element_type=F32, precision=lax.Precision.HIGHEST)

    vec_out = jax.ShapeDtypeStruct((1, DM), F32)
    outs = pl.pallas_call(
        body, name="gmlp_bwd", grid=(t // tm,),
        in_specs=[_rows(tm, DM), _rows(tm, 2 * DM, C_UV // (2 * DM)), _vec(DM), _vec(DM),
                  pl.BlockSpec((NG, CH, CH), lambda i: (0, 0, 0)), pl.BlockSpec((NG, CH, CH), lambda i: (0, 0, 0)),
                  pl.BlockSpec((CH, DM), lambda i: (0, 0))],
        out_specs=[_rows(tm, 2 * DM), pl.BlockSpec((NG, CH, CH), lambda i: (0, 0, 0)),
                   pl.BlockSpec((CH, LANE), lambda i: (0, 0)), _vec(DM), _vec(DM)],
        out_shape=[jax.ShapeDtypeStruct((t, 2 * DM), BF16), jax.ShapeDtypeStruct((NG, CH, CH), F32),
                   jax.ShapeDtypeStruct((CH, LANE), F32), vec_out, vec_out],
        scratch_shapes=[pltpu.VMEM((CH, DM), F32)],
        compiler_params=_cparams(("arbitrary",)))(dga, proj, lng, lnb, wsm, wsmt, bsx)
    return outs


def _glu_into(zs_ref, glu_ref, halo_ref, first):
    hal = halo_ref[...]
    z0h = hal[:, :DM] * _sigmoid(hal[:, DM:])
    zs_ref[0:HALO, :] = jnp.where(first, 0.0, z0h)
    g = glu_ref[...]
    zs_ref[HALO:, :] = g[:, :DM] * _sigmoid(g[:, DM:])


def _conv_fwd(proj, cw, cb, lng, lnb, *, tm=256, rb=64):
    t = proj.shape[0]
    tm = min(tm, t)
    hb = tm // HALO
    gcol = C_GLU // (2 * DM)

    def body(glu_ref, halo_ref, cw_ref, cb_ref, lng_ref, lnb_ref, zc_ref, zb_ref, zs_ref):
        i = pl.program_id(0)
        _glu_into(zs_ref, glu_ref, halo_ref, i == 0)
        for cbk in range(DM // LANE):
            cs = slice(cbk * LANE, (cbk + 1) * LANE)
            for r in range(tm // rb):
                acc = jnp.broadcast_to(cb_ref[:, cs], (rb, LANE))
                for k in range(KW):
                    off = r * rb + HALO - (KW - 1) + k
                    acc = acc + cw_ref[k:k + 1, cs] * zs_ref[off:off + rb, cs]
                zc_ref[r * rb:(r + 1) * rb, cs] = acc
        zc = zc_ref[...]
        mu = _mean(zc)
        zcc = zc - mu
        zh = zcc * lax.rsqrt(_mean(zcc * zcc) + EPS)
        a = zh * lng_ref[...] + lnb_ref[...]
        zb_ref[...] = (a * _sigmoid(a)).astype(BF16)

    return pl.pallas_call(
        body, name="conv_fwd", grid=(t // tm,),
        in_specs=[_rows(tm, 2 * DM, gcol),
                  pl.BlockSpec((HALO, 2 * DM), lambda i: (jnp.maximum(i * hb - 1, 0), gcol)),
                  pl.BlockSpec((KW, DM), lambda i: (0, 0)), _vec(DM), _vec(DM), _vec(DM)],
        out_specs=[_rows(tm, DM), _rows(tm, DM)],
        out_shape=[jax.ShapeDtypeStruct((t, DM), F32), jax.ShapeDtypeStruct((t, DM), BF16)],
        scratch_shapes=[pltpu.VMEM((HALO + tm, DM), F32)],
        compiler_params=_cparams(("parallel",)))(proj, proj, cw, cb, lng, lnb)


def _conv_bwd_ln(dzb, zc, lng, lnb, *, tm=256):
    t = zc.shape[0]
    tm = min(tm, t)

    def body(dzb_ref, zc_ref, lng_ref, lnb_ref, dzc_ref, dlng_ref, dlnb_ref):
        @pl.when(pl.program_id(0) == 0)
        def _():
            dlng_ref[...] = jnp.zeros_like(dlng_ref)
            dlnb_ref[...] = jnp.zeros_like(dlnb_ref)

        zc = zc_ref[...]
        lng_v = lng_ref[...]
        mu = _mean(zc)
        zcc = zc - mu
        rstd = lax.rsqrt(_mean(zcc * zcc) + EPS)
        zh = zcc * rstd
        a = zh * lng_v + lnb_ref[...]
        s = _sigmoid(a)
        da = dzb_ref[...] * (s * (1.0 + a * (1.0 - s)))
        dlnb_ref[...] += _colsum(da)
        dlng_ref[...] += _colsum(da * zh)
        dzh = da * lng_v
        dzc_ref[...] = rstd * (dzh - _mean(dzh) - zh * _mean(dzh * zh))

    vec_out = jax.ShapeDtypeStruct((1, DM), F32)
    return pl.pallas_call(
        body, name="conv_bwd_ln", grid=(t // tm,), in_specs=[_rows(tm, DM), _rows(tm, DM), _vec(DM), _vec(DM)],
        out_specs=[_rows(tm, DM), _vec(DM), _vec(DM)],
        out_shape=[jax.ShapeDtypeStruct((t, DM), F32), vec_out, vec_out],
        compiler_params=_cparams(("arbitrary",)))(dzb, zc, lng, lnb)


def _conv_bwd(dzc, proj, cw, *, tm=256, rb=64):
    t = proj.shape[0]
    tm = min(tm, t)
    hb = tm // HALO
    nblk = t // tm
    gcol = C_GLU // (2 * DM)

    def body(dzc_ref, dnext_ref, glu_ref, halo_ref, cw_ref, dglu_ref, dcw_ref, dcb_ref, zs_ref, ds_ref):
        i = pl.program_id(0)

        @pl.when(i == 0)
        def _():
            dcw_ref[...] = jnp.zeros_like(dcw_ref)
            dcb_ref[...] = jnp.zeros_like(dcb_ref)

        _glu_into(zs_ref, glu_ref, halo_ref, i == 0)
        dzc = dzc_ref[...]
        ds_ref[0:tm, :] = dzc
        ds_ref[tm:, :] = jnp.where(i == nblk - 1, 0.0, dnext_ref[...])
        dcb_ref[...] += _colsum(dzc)
        for k in range(KW):
            off = HALO - (KW - 1) + k
            dcw_ref[k:k + 1, :] += _colsum(dzc * zs_ref[off:off + tm, :])
        g = glu_ref[...]
        val, sg = g[:, :DM], _sigmoid(g[:, DM:])
        for cbk in range(DM // LANE):
            cs = slice(cbk * LANE, (cbk + 1) * LANE)
            for r in range(tm // rb):
                acc = jnp.zeros((rb, LANE), F32)
                for k in range(KW):
                    off = r * rb + (KW - 1) - k
                    acc = acc + cw_ref[k:k + 1, cs] * ds_ref[off:off + rb, cs]
                rs = slice(r * rb, (r + 1) * rb)
                dglu_ref[rs, cs] = (acc * sg[rs, cs]).astype(BF16)
                v, s = val[rs, cs], sg[rs, cs]
                dglu_ref[rs, DM + cbk * LANE:DM + (cbk + 1) * LANE] = (acc * v * s * (1.0 - s)).astype(BF16)

    return pl.pallas_call(
        body, name="conv_bwd", grid=(nblk,),
        in_specs=[_rows(tm, DM),
                  pl.BlockSpec((HALO, DM), lambda i: (jnp.minimum((i + 1) * hb, nblk * hb - 1), 0)),
                  _rows(tm, 2 * DM, gcol),
                  pl.BlockSpec((HALO, 2 * DM), lambda i: (jnp.maximum(i * hb - 1, 0), gcol)),
                  pl.BlockSpec((KW, DM), lambda i: (0, 0))],
        out_specs=[_rows(tm, 2 * DM), pl.BlockSpec((HALO, DM), lambda i: (0, 0)), _vec(DM)],
        out_shape=[jax.ShapeDtypeStruct((t, 2 * DM), BF16), jax.ShapeDtypeStruct((HALO, DM), F32),
                   jax.ShapeDtypeStruct((1, DM), F32)],
        scratch_shapes=[pltpu.VMEM((HALO + tm, DM), F32), pltpu.VMEM((tm + HALO, DM), F32)],
        compiler_params=_cparams(("arbitrary",)))(dzc, dzc, proj, proj, cw)


def _log_sigmoid(x):
    return jnp.minimum(x, 0.0) - jnp.log1p(jnp.exp(-jnp.abs(x)))


def _fox_cum(proj, bfp):
    t = proj.shape[0]
    fcol = C_F // LANE

    def body(f_ref, bf_ref, cum_ref, carry_ref):
        @pl.when(pl.program_id(0) == 0)
        def _():
            carry_ref[...] = jnp.zeros_like(carry_ref)

        lf = _log_sigmoid(f_ref[...] + bf_ref[...])
        tri = (lax.broadcasted_iota(I32, (CH, CH), 0) >= lax.broadcasted_iota(I32, (CH, CH), 1)).astype(F32)
        cum = jnp.dot(tri, lf, preferred_element_type=F32, precision=lax.Precision.HIGHEST) + carry_ref[0:1, :]
        cum_ref[...] = cum
        carry_ref[...] = jnp.broadcast_to(cum[CH - 1:CH, :], carry_ref.shape)

    return pl.pallas_call(
        body, name="fox_cum", grid=(t // CH,), in_specs=[_rows(CH, LANE, fcol), _vec(LANE)],
        out_specs=_rows(CH, LANE), out_shape=jax.ShapeDtypeStruct((t, LANE), F32),
        scratch_shapes=[pltpu.VMEM((8, LANE), F32)],
        compiler_params=_cparams(("arbitrary",)))(proj, bfp)


def _fox_cum_bwd(dcum, proj, bfp):
    t = proj.shape[0]
    nb = t // CH
    fcol = C_F // LANE
    fw = D_INP - C_F

    def body(dc_ref, f_ref, bf_ref, df_ref, dbf_ref, carry_ref):
        @pl.when(pl.program_id(0) == 0)
        def _():
            carry_ref[...] = jnp.zeros_like(carry_ref)
            dbf_ref[...] = jnp.zeros_like(dbf_ref)

        triu = (lax.broadcasted_iota(I32, (CH, CH), 0) <= lax.broadcasted_iota(I32, (CH, CH), 1)).astype(F32)
        dlf = jnp.dot(triu, dc_ref[...], preferred_element_type=F32, precision=lax.Precision.HIGHEST) + carry_ref[0:1, :]
        carry_ref[...] = jnp.broadcast_to(dlf[0:1, :], carry_ref.shape)
        z = f_ref[...] + bf_ref[...]
        lane = lax.broadcasted_iota(I32, (CH, LANE), 1)
        df = jnp.where(lane < NG, dlf * _sigmoid(-z), 0.0)
        dbf_ref[...] += _colsum(df)
        df_ref[:, 0:LANE] = df.astype(BF16)
        df_ref[:, LANE:] = jnp.zeros((CH, fw - LANE), BF16)

    return pl.pallas_call(
        body, name="fox_cum_bwd", grid=(nb,),
        in_specs=[pl.BlockSpec((CH, LANE), lambda i: (nb - 1 - i, 0)),
                  pl.BlockSpec((CH, LANE), lambda i: (nb - 1 - i, fcol)), _vec(LANE)],
        out_specs=[pl.BlockSpec((CH, fw), lambda i: (nb - 1 - i, 0)), _vec(LANE)],
        out_shape=[jax.ShapeDtypeStruct((t, fw), BF16), jax.ShapeDtypeStruct((1, LANE), F32)],
        scratch_shapes=[pltpu.VMEM((8, LANE), F32)],
        compiler_params=_cparams(("arbitrary",)))(dcum, proj, bfp)


HD = 64
ATT_SCALE = 0.125
NEG = -1e30


def _qkv_prep(proj, *, tm=512):
    t = proj.shape[0]
    tm = min(tm, t)

    def body(q_ref, k_ref, v_ref, o_ref):
        o_ref[:, 0:DM] = (q_ref[...] * ATT_SCALE).astype(BF16)
        o_ref[:, DM:2 * DM] = k_ref[...].astype(BF16)
        o_ref[:, 2 * DM:] = v_ref[...].astype(BF16)

    return pl.pallas_call(
        body, name="qkv_prep", grid=(t // tm,),
        in_specs=[_rows(tm, DM, C_Q // DM), _rows(tm, DM, C_K // DM), _rows(tm, DM, C_V // DM)],
        out_specs=_rows(tm, 3 * DM), out_shape=jax.ShapeDtypeStruct((t, 3 * DM), BF16),
        compiler_params=_cparams(("parallel",)))(proj, proj, proj)


def _causal_pairs(nq, outer_is_query):
    if outer_is_query:
        pairs = [(i, j) for i in range(nq) for j in range(i + 1)]
    else:
        pairs = [(j, i) for j in range(nq) for i in range(j, nq)]
    return (jnp.asarray([p[0] for p in pairs], I32), jnp.asarray([p[1] for p in pairs], I32))


def _to_row(col):
    return jnp.transpose(col)[0:1, :]


def _rep(x, tk):
    return x if tk == LANE else jnp.tile(x, (1, tk // LANE))


def _attn_fwd(qkv, ckrow, *, tq=512):
    t = qkv.shape[0]
    tq = min(tq, t)
    tk = tq
    nq = t // tq
    oi, ij = _causal_pairs(nq, True)

    def body(oi_ref, ij_ref, q_ref, k_ref, v_ref, ck_ref, o_ref, lse_ref, lser_ref, m_ref, l_ref, acc_ref):
        n = pl.program_id(0)
        i, j = oi_ref[n], ij_ref[n]

        @pl.when(j == 0)
        def _():
            m_ref[...] = jnp.full_like(m_ref, NEG)
            l_ref[...] = jnp.zeros_like(l_ref)
            acc_ref[...] = jnp.zeros_like(acc_ref)

        def step(masked):
            if masked:
                keep = lax.broadcasted_iota(I32, (tq, tk), 1) <= lax.broadcasted_iota(I32, (tq, tk), 0)
            lo = lax.broadcasted_iota(I32, (tq, LANE), 1) < HD
            for hp in range(NG // 2):
                cs = slice(hp * LANE, (hp + 1) * LANE)
                qp, kp, vp = q_ref[:, cs], k_ref[:, cs], v_ref[:, cs]
                alphas, pvs = [], []
                for hh in range(2):
                    h = 2 * hp + hh
                    qm = jnp.where(lo if hh == 0 else jnp.logical_not(lo), qp, jnp.zeros_like(qp))
                    s = _dot(qm, kp, NT) - ck_ref[h:h + 1, :]
                    if masked:
                        s = jnp.where(keep, s, NEG)
                    m_prev = m_ref[h]
                    m_new = jnp.maximum(m_prev, jnp.max(s, axis=1, keepdims=True))
                    alpha = jnp.exp(m_prev - m_new)
                    p = jnp.exp(s - _rep(m_new, tk))
                    l_ref[h] = alpha * l_ref[h] + jnp.sum(p, axis=1, keepdims=True)
                    m_ref[h] = m_new
                    alphas.append(alpha)
                    pvs.append(_dot(p.astype(BF16), vp))
                acc_ref[:, cs] = jnp.where(lo, alphas[0], alphas[1]) * acc_ref[:, cs] + jnp.where(lo, pvs[0], pvs[1])

        @pl.when(j < i)
        def _():
            step(False)

        @pl.when(j == i)
        def _():
            step(True)
            lo = lax.broadcasted_iota(I32, (tq, LANE), 1) < HD
            for hp in range(NG // 2):
                cs = slice(hp * LANE, (hp + 1) * LANE)
                o_ref[:, cs] = (acc_ref[:, cs] / jnp.where(lo, l_ref[2 * hp], l_ref[2 * hp + 1])).astype(BF16)
            for h in range(NG):
                lse = m_ref[h] + jnp.log(l_ref[h])
                lse_ref[h] = lse
                lser_ref[h:h + 1, :] = _to_row(lse)

    gs = pltpu.PrefetchScalarGridSpec(
        num_scalar_prefetch=2, grid=(int(oi.shape[0]),),
        in_specs=[pl.BlockSpec((tq, DM), lambda n, a, b: (a[n], 0)),
                  pl.BlockSpec((tk, DM), lambda n, a, b: (b[n], 1)),
                  pl.BlockSpec((tk, DM), lambda n, a, b: (b[n], 2)),
                  pl.BlockSpec((NG, tk), lambda n, a, b: (0, b[n]))],
        out_specs=[pl.BlockSpec((tq, DM), lambda n, a, b: (a[n], 0)),
                   pl.BlockSpec((NG, tq, LANE), lambda n, a, b: (0, a[n], 0)),
                   pl.BlockSpec((NG, tq), lambda n, a, b: (0, a[n]))],
        scratch_shapes=[pltpu.VMEM((NG, tq, LANE), F32), pltpu.VMEM((NG, tq, LANE), F32), pltpu.VMEM((tq, DM), F32)])
    return pl.pallas_call(
        body, name="attn_fwd", grid_spec=gs,
        out_shape=[jax.ShapeDtypeStruct((t, DM), BF16), jax.ShapeDtypeStruct((NG, t, LANE), F32),
                   jax.ShapeDtypeStruct((NG, t), F32)],
        compiler_params=_cparams(("arbitrary",)))(oi, ij, qkv, qkv, qkv, ckrow)


def _attn_bwd_dq(qkv, ckrow, o, do, lse, *, tq=512):
    t = qkv.shape[0]
    tq = min(tq, t)
    tk = tq
    nq = t // tq
    oi, ij = _causal_pairs(nq, True)

    def body(oi_ref, ij_ref, q_ref, k_ref, v_ref, ck_ref, o_ref, do_ref, lse_ref, dq_ref, deltar_ref, dcqr_ref, delta_ref,
             dcq_ref, acc_ref):
        n = pl.program_id(0)
        i, j = oi_ref[n], ij_ref[n]

        @pl.when(j == 0)
        def _():
            acc_ref[...] = jnp.zeros_like(acc_ref)
            dcq_ref[...] = jnp.zeros_like(dcq_ref)
            lo = lax.broadcasted_iota(I32, (tq, LANE), 1) < HD
            for hp in range(NG // 2):
                cs = slice(hp * LANE, (hp + 1) * LANE)
                prod = do_ref[:, cs] * o_ref[:, cs].astype(F32)
                for hh in range(2):
                    d = jnp.sum(jnp.where(lo if hh == 0 else jnp.logical_not(lo), prod, 0.0), axis=1, keepdims=True)
                    dcol = jnp.broadcast_to(d, (tq, LANE))
                    delta_ref[2 * hp + hh] = dcol
                    deltar_ref[2 * hp + hh:2 * hp + hh + 1, :] = _to_row(dcol)

        def step(masked):
            if masked:
                keep = lax.broadcasted_iota(I32, (tq, tk), 1) <= lax.broadcasted_iota(I32, (tq, tk), 0)
            lo = lax.broadcasted_iota(I32, (tq, LANE), 1) < HD
            for hp in range(NG // 2):
                cs = slice(hp * LANE, (hp + 1) * LANE)
                qp, kp, vp, dop = q_ref[:, cs], k_ref[:, cs], v_ref[:, cs], do_ref[:, cs].astype(BF16)
                parts = []
                for hh in range(2):
                    h = 2 * hp + hh
                    sel = lo if hh == 0 else jnp.logical_not(lo)
                    s = _dot(jnp.where(sel, qp, jnp.zeros_like(qp)), kp, NT) - ck_ref[h:h + 1, :]
                    if masked:
                        s = jnp.where(keep, s, NEG)
                    p = jnp.exp(s - _rep(lse_ref[h], tk))
                    dp = _dot(jnp.where(sel, dop, jnp.zeros_like(dop)), vp, NT)
                    ds = p * (dp - _rep(delta_ref[h], tk))
                    dcq_ref[h] += jnp.sum(ds, axis=1, keepdims=True)
                    parts.append(_dot(ds.astype(BF16), kp))
                acc_ref[:, cs] += jnp.where(lo, parts[0], parts[1])

        @pl.when(j < i)
        def _():
            step(False)

        @pl.when(j == i)
        def _():
            step(True)
            dq_ref[...] = (acc_ref[...] * ATT_SCALE).astype(BF16)
            for h in range(NG):
                dcqr_ref[h:h + 1, :] = _to_row(dcq_ref[h])

    gs = pltpu.PrefetchScalarGridSpec(
        num_scalar_prefetch=2, grid=(int(oi.shape[0]),),
        in_specs=[pl.BlockSpec((tq, DM), lambda n, a, b: (a[n], 0)),
                  pl.BlockSpec((tk, DM), lambda n, a, b: (b[n], 1)),
                  pl.BlockSpec((tk, DM), lambda n, a, b: (b[n], 2)),
                  pl.BlockSpec((NG, tk), lambda n, a, b: (0, b[n])),
                  pl.BlockSpec((tq, DM), lambda n, a, b: (a[n], 0)),
                  pl.BlockSpec((tq, DM), lambda n, a, b: (a[n], 0)),
                  pl.BlockSpec((NG, tq, LANE), lambda n, a, b: (0, a[n], 0))],
        out_specs=[pl.BlockSpec((tq, DM), lambda n, a, b: (a[n], 0)),
                   pl.BlockSpec((NG, tq), lambda n, a, b: (0, a[n])),
                   pl.BlockSpec((NG, tq), lambda n, a, b: (0, a[n]))],
        scratch_shapes=[pltpu.VMEM((NG, tq, LANE), F32), pltpu.VMEM((NG, tq, LANE), F32), pltpu.VMEM((tq, DM), F32)])
    return pl.pallas_call(
        body, name="attn_bwd_dq", grid_spec=gs,
        out_shape=[jax.ShapeDtypeStruct((t, DM), BF16), jax.ShapeDtypeStruct((NG, t), F32), jax.ShapeDtypeStruct((NG, t), F32)],
        compiler_params=_cparams(("arbitrary",)))(oi, ij, qkv, qkv, qkv, ckrow, o, do, lse)


def _attn_bwd_dkv(qkv, ckcol, do, lserow, deltarow, *, tq=512):
    t = qkv.shape[0]
    tq = min(tq, t)
    tk = tq
    nq = t // tq
    oj, ii = _causal_pairs(nq, False)

    def body(oj_ref, ii_ref, q_ref, k_ref, v_ref, ck_ref, do_ref, lse_ref, delta_ref, dk_ref, dv_ref, dckr_ref, dka_ref,
             dva_ref, dck_ref):
        n = pl.program_id(0)
        j, i = oj_ref[n], ii_ref[n]

        @pl.when(i == j)
        def _():
            dka_ref[...] = jnp.zeros_like(dka_ref)
            dva_ref[...] = jnp.zeros_like(dva_ref)
            dck_ref[...] = jnp.zeros_like(dck_ref)

        def step(masked):
            if masked:
                keep = lax.broadcasted_iota(I32, (tk, tq), 0) <= lax.broadcasted_iota(I32, (tk, tq), 1)
            lo = lax.broadcasted_iota(I32, (tk, LANE), 1) < HD
            for hp in range(NG // 2):
                cs = slice(hp * LANE, (hp + 1) * LANE)
                qp, kp, vp, dop = q_ref[:, cs], k_ref[:, cs], v_ref[:, cs], do_ref[:, cs].astype(BF16)
                dvs, dks = [], []
                for hh in range(2):
                    h = 2 * hp + hh
                    sel = lo if hh == 0 else jnp.logical_not(lo)
                    st = _dot(jnp.where(sel, kp, jnp.zeros_like(kp)), qp, NT) - _rep(ck_ref[h], tq)
                    if masked:
                        st = jnp.where(keep, st, NEG)
                    pt = jnp.exp(st - lse_ref[h:h + 1, :])
                    dvs.append(_dot(pt.astype(BF16), dop))
                    dpt = _dot(jnp.where(sel, vp, jnp.zeros_like(vp)), dop, NT)
                    dst = pt * (dpt - delta_ref[h:h + 1, :])
                    dks.append(_dot(dst.astype(BF16), qp))
                    dck_ref[h] -= jnp.sum(dst, axis=1, keepdims=True)
                dva_ref[:, cs] += jnp.where(lo, dvs[0], dvs[1])
                dka_ref[:, cs] += jnp.where(lo, dks[0], dks[1])

        @pl.when(i == j)
        def _():
            step(True)

        @pl.when(i > j)
        def _():
            step(False)

        @pl.when(i == nq - 1)
        def _():
            dk_ref[...] = dka_ref[...].astype(BF16)
            dv_ref[...] = dva_ref[...].astype(BF16)
            for h in range(NG):
                dckr_ref[h:h + 1, :] = _to_row(dck_ref[h])

    gs = pltpu.PrefetchScalarGridSpec(
        num_scalar_prefetch=2, grid=(int(oj.shape[0]),),
        in_specs=[pl.BlockSpec((tq, DM), lambda n, a, b: (b[n], 0)),
                  pl.BlockSpec((tk, DM), lambda n, a, b: (a[n], 1)),
                  pl.BlockSpec((tk, DM), lambda n, a, b: (a[n], 2)),
                  pl.BlockSpec((NG, tk, LANE), lambda n, a, b: (0, a[n], 0)),
                  pl.BlockSpec((tq, DM), lambda n, a, b: (b[n], 0)),
                  pl.BlockSpec((NG, tq), lambda n, a, b: (0, b[n])),
                  pl.BlockSpec((NG, tq), lambda n, a, b: (0, b[n]))],
        out_specs=[pl.BlockSpec((tk, DM), lambda n, a, b: (a[n], 0)),
                   pl.BlockSpec((tk, DM), lambda n, a, b: (a[n], 0)),
                   pl.BlockSpec((NG, tk), lambda n, a, b: (0, a[n]))],
        scratch_shapes=[pltpu.VMEM((tk, DM), F32), pltpu.VMEM((tk, DM), F32), pltpu.VMEM((NG, tk, LANE), F32)])
    return pl.pallas_call(
        body, name="attn_bwd_dkv", grid_spec=gs,
        out_shape=[jax.ShapeDtypeStruct((t, DM), BF16), jax.ShapeDtypeStruct((t, DM), BF16), jax.ShapeDtypeStruct((NG, t), F32)],
        compiler_params=_cparams(("arbitrary",)))(oj, ii, qkv, qkv, qkv, ckcol, do, lserow, deltarow)


def _merge_fwd(ga, zb, att, proj, wa, wb, wc, *, tm=256):
    t = ga.shape[0]
    tm = min(tm, t)
    wspec = pl.BlockSpec((DM, D), lambda i: (0, 0))

    def body(ga_ref, zb_ref, att_ref, gate_ref, wa_ref, wb_ref, wc_ref, m_ref):
        acc = jnp.zeros((tm, D), F32)
        for b, (x_ref, w_ref) in enumerate(((ga_ref, wa_ref), (zb_ref, wb_ref), (att_ref, wc_ref))):
            acc = acc + _sigmoid(gate_ref[:, b * D:(b + 1) * D]) * _dot(x_ref[...], w_ref[...])
        m_ref[...] = acc.astype(BF16)

    return pl.pallas_call(
        body, name="merge_fwd", grid=(t // tm,),
        in_specs=[_rows(tm, DM), _rows(tm, DM), _rows(tm, DM), _rows(tm, 3 * D, 0), wspec, wspec, wspec],
        out_specs=_rows(tm, D), out_shape=jax.ShapeDtypeStruct((t, D), BF16),
        compiler_params=_cparams(("parallel",)))(ga, zb, att, proj, wa, wb, wc)


def _merge_bwd(dm, ga, zb, att, proj, wa, wb, wc, *, tm=256):
    t = ga.shape[0]
    tm = min(tm, t)
    wspec = pl.BlockSpec((DM, D), lambda i: (0, 0))

    def body(dm_ref, ga_ref, zb_ref, att_ref, gate_ref, wa_ref, wb_ref, wc_ref, dgate_ref, dga_ref, dzb_ref, datt_ref,
             dwa_ref, dwb_ref, dwc_ref):
        @pl.when(pl.program_id(0) == 0)
        def _():
            dwa_ref[...] = jnp.zeros_like(dwa_ref)
            dwb_ref[...] = jnp.zeros_like(dwb_ref)
            dwc_ref[...] = jnp.zeros_like(dwc_ref)

        dmv = dm_ref[...]
        branches = ((ga_ref, wa_ref, dga_ref, dwa_ref), (zb_ref, wb_ref, dzb_ref, dwb_ref),
                    (att_ref, wc_ref, datt_ref, dwc_ref))
        for b, (x_ref, w_ref, dx_ref, dw_ref) in enumerate(branches):
            xv, wv = x_ref[...], w_ref[...]
            y = _dot(xv, wv)
            g = _sigmoid(gate_ref[:, b * D:(b + 1) * D])
            dgate_ref[:, b * D:(b + 1) * D] = (dmv * y * g * (1.0 - g)).astype(BF16)
            dy = (dmv * g).astype(BF16)
            dx_ref[...] = _dot(dy, wv, NT)
            dw_ref[...] += _dot(xv, dy, TN)

    return pl.pallas_call(
        body, name="merge_bwd", grid=(t // tm,),
        in_specs=[_rows(tm, D), _rows(tm, DM), _rows(tm, DM), _rows(tm, DM), _rows(tm, 3 * D, 0), wspec, wspec, wspec],
        out_specs=[_rows(tm, 3 * D), _rows(tm, DM), _rows(tm, DM), _rows(tm, DM), wspec, wspec, wspec],
        out_shape=[jax.ShapeDtypeStruct((t, 3 * D), BF16)] + [jax.ShapeDtypeStruct((t, DM), F32)] * 3
        + [jax.ShapeDtypeStruct((DM, D), F32)] * 3,
        compiler_params=_cparams(("arbitrary",)))(dm, ga, zb, att, proj, wa, wb, wc)


def _heads_layout(cum):
    t = cum.shape[0]
    ckrow = cum[:, :NG].T
    return ckrow, jnp.broadcast_to(ckrow[:, :, None], (NG, t, LANE))


def _layer_fwd(x, mod, w, late=None):
    sh1, sc1, gt1, sh2, sc2, gt2 = (mod[k:k + 1] for k in range(NMOD))
    h1 = _norm_mod(x, w["mix_pre_g"], sc1, sh1, name="norm_mix")
    proj = _matmul(h1, w["w_in_p"], name="mm_proj")
    ga = _gmlp_fwd(proj, w["gmlp_ln_g"], w["gmlp_ln_b"], w["wsm"], w["bsx"])
    zc, zb = _conv_fwd(proj, w["conv_w"], w["conv_b"], w["conv_ln_g"], w["conv_ln_b"])
    cum = _fox_cum(proj, w["bfp"])
    ckrow, ckcol = _heads_layout(cum)
    qkv = _qkv_prep(proj)
    att, lse, lser = _attn_fwd(qkv, ckrow)
    if late is not None:
        w = {**w, **late(att)}
    merged = _merge_fwd(ga, zb, att, proj, w["w_a_out"], w["w_b_out"], w["w_c_out"])
    y1 = _matmul(merged, w["w_out"], name="mm_out")
    x2 = _resid(x, y1, gt1, w["mix_post_g"], name="resid_mix")
    h2 = _norm_mod(x2, w["mlp_pre_g"], sc2, sh2, name="norm_mlp")
    a, hid = _matmul(h2, w["mlp_w1"], name="mm_w1", b_slabs=True, extra_out=(BF16,),
                     epilogue=lambda acc: (acc, jnp.square(jnp.maximum(acc, 0.0))))
    y2 = _matmul(hid, w["mlp_w2"], name="mm_w2")
    x3 = _resid(x2, y2, gt2, w["mlp_post_g"], name="resid_mlp")
    saved = dict(x=x, h1=h1, proj=proj, ga=ga, zc=zc, zb=zb, qkv=qkv, ckrow=ckrow, ckcol=ckcol, att=att, lse=lse, lser=lser, merged=merged,
                 y1=y1, x2=x2, h2=h2, a=a, hid=hid, y2=y2)
    return x3, saved, w


def _layer_bwd(dx3, mod, w, s):
    sh1, sc1, gt1, sh2, sc2, gt2 = (mod[k:k + 1] for k in range(NMOD))
    g = {}
    dy2, dgt2, g["mlp_post_g"] = _resid_bwd(dx3, s["y2"], gt2, w["mlp_post_g"], name="resid_mlp_bwd")
    da = _matmul(dy2, w["mlp_w2"], tb=True, name="mm_dhid", out_dtype=BF16, extra=(s["a"],),
                 epilogue=lambda acc, a: (acc * (2.0 * jnp.maximum(a, 0.0)),))
    g["mlp_w2"] = _matmul(s["hid"], dy2, ta=True, name="mm_dw2")
    g["mlp_w1"] = _matmul(s["h2"], da, ta=True, name="mm_dw1", out_slabs=N_CHIPS)
    dh2 = _matmul(da, w["mlp_w1"], tb=True, name="mm_dh2", b_slabs=True)
    dx2, g["mlp_pre_g"], dsc2, dsh2 = _norm_bwd(dh2, dx3, s["x2"], w["mlp_pre_g"], sc2, name="norm_mlp_bwd")
    dy1, dgt1, g["mix_post_g"] = _resid_bwd(dx2, s["y1"], gt1, w["mix_post_g"], name="resid_mix_bwd")
    dmerged = _matmul(dy1, w["w_out"], tb=True, name="mm_dmerged")
    g["w_out"] = _matmul(s["merged"], dy1, ta=True, name="mm_dwout")
    dgate, dga, dzb, datt, g["w_a_out"], g["w_b_out"], g["w_c_out"] = _merge_bwd(
        dmerged, s["ga"], s["zb"], s["att"], s["proj"], w["w_a_out"], w["w_b_out"], w["w_c_out"])
    duv, g["gmlp_ws"], dbs, g["gmlp_ln_g"], g["gmlp_ln_b"] = _gmlp_bwd(
        dga, s["proj"], w["gmlp_ln_g"], w["gmlp_ln_b"], w["wsm"], w["wsmt"], w["bsx"])
    g["gmlp_bs"] = dbs[:, :NG].T
    dzc, g["conv_ln_g"], g["conv_ln_b"] = _conv_bwd_ln(dzb, s["zc"], w["conv_ln_g"], w["conv_ln_b"])
    dglu, dcw, g["conv_b"] = _conv_bwd(dzc, s["proj"], w["conv_w"])
    g["conv_w"] = dcw[:KW]
    dq, delta, dcq = _attn_bwd_dq(s["qkv"], s["ckrow"], s["att"], datt, s["lse"])
    dk, dv, dck = _attn_bwd_dkv(s["qkv"], s["ckcol"], datt, s["lser"], delta)
    dcum = jnp.pad((dcq + dck).T, ((0, 0), (0, LANE - NG)))
    df, dbf = _fox_cum_bwd(dcum, s["proj"], w["bfp"])
    g["fox_bf"] = dbf[0, :NG]
    dproj = jnp.concatenate([dgate, duv, dglu, dq, dk, dv, df], axis=1)
    g["w_in_p"] = _matmul(s["h1"], dproj, ta=True, name="mm_dwin")
    dh1 = _matmul(dproj, w["w_in_p"], tb=True, name="mm_dh1")
    dx, g["mix_pre_g"], dsc1, dsh1 = _norm_bwd(dh1, dx2, s["x"], w["mix_pre_g"], sc1, name="norm_mix_bwd")
    dmod = jnp.concatenate([dsh1, dsc1, dgt1, dsh2, dsc2, dgt2], axis=0)
    return dx, g, dmod


def _position():
    return lax.axis_index("x"), lax.axis_index("y"), lax.axis_index("c")


def _all_gather8(v):
    m_per, n = v.shape

    def body(x_ref, out_ref, send_sems, recv_sems, local_sem):
        x, y, c = _position()
        me, sibling = (x, y, c), (x, y, 1 - c)
        chips = [(1 - x, y), (x, 1 - y), (1 - x, 1 - y)]

        def rows(px, py, pc):
            return out_ref.at[pl.ds((4 * px + 2 * py + pc) * m_per, m_per), :]

        def copy(k, block, to, src=None):
            return pltpu.make_async_remote_copy(
                src_ref=rows(*block) if src is None else src, dst_ref=rows(*block), send_sem=send_sems.at[k],
                recv_sem=recv_sems.at[k], device_id=to, device_id_type=MESH)

        mine = pltpu.make_async_copy(x_ref, rows(*me), local_sem)
        mine.start()
        first = [copy(0, me, sibling, src=x_ref)]
        first += [copy(1 + j, me, (*chip, c), src=x_ref) for j, chip in enumerate(chips)]
        for cp in first:
            cp.start()
        passed = [copy(4 + j, (*chip, c), sibling) for j, chip in enumerate(chips)]
        for j, chip in enumerate(chips):
            copy(1 + j, (*chip, c), me).wait_recv()
            passed[j].start()
        copy(0, sibling, me).wait_recv()
        for j, chip in enumerate(chips):
            copy(4 + j, (*chip, 1 - c), me).wait_recv()
        for cp in first + passed:
            cp.wait_send()
        mine.wait()

    out = pl.pallas_call(
        body, name="all_gather8", out_shape=jax.ShapeDtypeStruct((N_DEV * m_per, n), v.dtype),
        in_specs=[pl.BlockSpec(memory_space=pltpu.VMEM)], out_specs=pl.BlockSpec(memory_space=pltpu.VMEM),
        scratch_shapes=[pltpu.SemaphoreType.DMA((7,)), pltpu.SemaphoreType.DMA((7,)), pltpu.SemaphoreType.DMA],
        compiler_params=pltpu.CompilerParams(vmem_limit_bytes=VMEM_LIMIT),
    )(v)
    return out.reshape(N_DEV, m_per, n)


def _half(c, rows):
    return pl.ds(c * (rows // 2), rows // 2)


HBM = pl.BlockSpec(memory_space=pltpu.HBM)
SEM = pl.BlockSpec(memory_space=pltpu.SEMAPHORE)
EFFECT = pltpu.SideEffectType.DATAFLOW_SIDE_EFFECTING


def _ici_copies(kind, src_refs, land_refs, send_sems, recv_sems):
    x, y, c = _position()
    chips = [(1 - x, y), (x, 1 - y), (1 - x, 1 - y)]
    sends, recvs = [], []
    for wi, (src, land) in enumerate(zip(src_refs, land_refs)):
        for k, (px, py) in enumerate(chips):
            if kind == "gather":
                rows = src.shape[0]
                s_win = src.at[_half(c, rows)]
                there, here = land.at[2 * x + y, _half(c, rows)], land.at[2 * px + py, _half(c, rows)]
            else:
                s_win = src.at[2 * px + py]
                there = here = land.at[k]
            for dst, out in ((there, sends), (here, recvs)):
                out.append(pltpu.make_async_remote_copy(
                    src_ref=s_win, dst_ref=dst, send_sem=send_sems.at[wi * 3 + k], recv_sem=recv_sems.at[wi * 3 + k],
                    device_id=(px, py, c), device_id_type=MESH))
    return sends, recvs


def _ici_start(kind, srcs, land_shapes, after, *, name):
    nw = len(srcs)

    def body(*refs):
        src_refs, land_refs = refs[:nw], refs[nw:2 * nw]
        send_sems, recv_sems = refs[2 * nw + 1:2 * nw + 3]
        token = refs[-1]
        sends, _ = _ici_copies(kind, src_refs, land_refs, send_sems, recv_sems)
        for cp in sends:
            cp.start()
        token[...] = jnp.zeros_like(token)

    lands = [pltpu.with_memory_space_constraint(lax.empty(shp, s.dtype), pltpu.HBM) for shp, s in zip(land_shapes, srcs)]
    outs = pl.pallas_call(
        body, name=name,
        out_shape=(pltpu.SemaphoreType.DMA((3 * nw,)), pltpu.SemaphoreType.DMA((3 * nw,)),
                   *[pltpu.HBM(s.shape, s.dtype) for s in srcs], *[pltpu.HBM(shp, s.dtype) for shp, s in zip(land_shapes, srcs)],
                   jax.ShapeDtypeStruct((8, LANE), F32)),
        in_specs=[HBM] * (2 * nw) + [ANY], out_specs=(SEM, SEM, *[HBM] * (2 * nw), pl.BlockSpec(memory_space=pltpu.VMEM)),
        input_output_aliases={i: 2 + i for i in range(2 * nw)},
        compiler_params=pltpu.CompilerParams(has_side_effects=EFFECT),
    )(*[pltpu.with_memory_space_constraint(s, pltpu.HBM) for s in srcs], *lands, after)
    return outs[0], outs[1], outs[2:2 + nw], outs[2 + nw:2 + 2 * nw], outs[-1]


def _ici_wait(kind, send_sems, recv_sems, srcs, lands, after, *, name):
    nw = len(srcs)

    def body(*refs):
        src_refs, land_refs = refs[:nw], refs[nw:2 * nw]
        s_sems, r_sems = refs[2 * nw:2 * nw + 2]
        sends, recvs = _ici_copies(kind, src_refs, land_refs, s_sems, r_sems)
        for cp in sends:
            cp.wait_send()
        for cp in recvs:
            cp.wait_recv()

    outs = pl.pallas_call(
        body, name=name,
        out_shape=(*[pltpu.HBM(s.shape, s.dtype) for s in srcs], *[pltpu.HBM(a.shape, a.dtype) for a in lands]),
        in_specs=[HBM] * (2 * nw) + [SEM, SEM, ANY], out_specs=tuple([HBM] * (2 * nw)),
        input_output_aliases={i: i for i in range(2 * nw)},
        compiler_params=pltpu.CompilerParams(has_side_effects=EFFECT),
    )(*srcs, *lands, send_sems, recv_sems, after)
    return outs[nw:]


def _ag_d2d(lands, *, name):
    nw = len(lands)

    def body(*refs):
        out_refs = refs[nw:2 * nw]
        send_sems, recv_sems = refs[2 * nw:]
        x, y, c = _position()
        chips = [(1 - x, y), (x, 1 - y), (1 - x, 1 - y)]
        copies = []
        for wi in range(nw):
            rows = out_refs[wi].shape[1]
            for k, (px, py) in enumerate(chips):
                win = out_refs[wi].at[2 * px + py, _half(c, rows)]
                copies.append(pltpu.make_async_remote_copy(
                    src_ref=win, dst_ref=win, send_sem=send_sems.at[wi * 3 + k], recv_sem=recv_sems.at[wi * 3 + k],
                    device_id=(x, y, 1 - c), device_id_type=MESH))
        for cp in copies:
            cp.start()
        for wi in range(nw):
            rows = out_refs[wi].shape[1]
            for k, (px, py) in enumerate(chips):
                win = out_refs[wi].at[2 * px + py, _half(1 - c, rows)]
                pltpu.make_async_remote_copy(
                    src_ref=win, dst_ref=win, send_sem=send_sems.at[wi * 3 + k], recv_sem=recv_sems.at[wi * 3 + k],
                    device_id=(x, y, 1 - c), device_id_type=MESH).wait_recv()
        for cp in copies:
            cp.wait_send()

    return pl.pallas_call(
        body, name=name, out_shape=[jax.ShapeDtypeStruct(a.shape, a.dtype) for a in lands],
        in_specs=[ANY] * nw, out_specs=[ANY] * nw, input_output_aliases={i: i for i in range(nw)},
        scratch_shapes=[pltpu.SemaphoreType.DMA((3 * nw,)), pltpu.SemaphoreType.DMA((3 * nw,))],
    )(*lands)


def _rs_swap(gs, *, name):
    nw = len(gs)

    def body(*refs):
        g_refs, out_refs = refs[:nw], refs[nw:2 * nw]
        send_sems, recv_sems = refs[2 * nw:]
        x, y, c = _position()
        copies = []
        for wi in range(nw):
            rows = g_refs[wi].shape[1]
            for j in range(N_CHIPS):
                copies.append(pltpu.make_async_remote_copy(
                    src_ref=g_refs[wi].at[j, _half(1 - c, rows)], dst_ref=out_refs[wi].at[j],
                    send_sem=send_sems.at[wi * N_CHIPS + j], recv_sem=recv_sems.at[wi * N_CHIPS + j],
                    device_id=(x, y, 1 - c), device_id_type=MESH))
        for cp in copies:
            cp.start()
        for cp in copies:
            cp.wait()

    return pl.pallas_call(
        body, name=name,
        out_shape=[jax.ShapeDtypeStruct((N_CHIPS, g.shape[1] // 2, g.shape[2]), g.dtype) for g in gs],
        in_specs=[ANY] * nw, out_specs=[ANY] * nw,
        scratch_shapes=[pltpu.SemaphoreType.DMA((N_CHIPS * nw,)), pltpu.SemaphoreType.DMA((N_CHIPS * nw,))],
    )(*gs)


def _swap_reduced(reds):
    nw = len(reds)

    def body(*refs):
        r_refs, out_refs = refs[:nw], refs[nw:2 * nw]
        send_sems, recv_sems = refs[2 * nw:]
        x, y, c = _position()
        copies = [pltpu.make_async_remote_copy(
            src_ref=r_refs[wi], dst_ref=out_refs[wi], send_sem=send_sems.at[wi], recv_sem=recv_sems.at[wi],
            device_id=(x, y, 1 - c), device_id_type=MESH) for wi in range(nw)]
        for cp in copies:
            cp.start()
        for cp in copies:
            cp.wait()

    return pl.pallas_call(
        body, name="rs_swap_reduced", out_shape=[jax.ShapeDtypeStruct(r.shape, r.dtype) for r in reds],
        in_specs=[ANY] * nw, out_specs=[ANY] * nw,
        scratch_shapes=[pltpu.SemaphoreType.DMA((nw,)), pltpu.SemaphoreType.DMA((nw,))],
    )(*reds)


def _row_tile(rows, cols):
    tr = rows
    while tr * cols * 4 > (2 << 20) and tr % 32 == 0:
        tr //= 2
    return tr


def _add_own_half(g, recv, c, *, name):
    nj, h, w = recv.shape
    tr = _row_tile(h, w)
    nb = h // tr

    def body(c_ref, g_ref, r_ref, o_ref, ob_ref):
        sm = g_ref[0] + r_ref[0]
        o_ref[0] = sm
        ob_ref[0] = sm.astype(BF16)

    spec = pl.BlockSpec((1, tr, w), lambda j, i, cc: (j, i, 0))
    gs = pltpu.PrefetchScalarGridSpec(
        num_scalar_prefetch=1, grid=(nj, nb),
        in_specs=[pl.BlockSpec((1, tr, w), lambda j, i, cc: (j, cc[0] * nb + i, 0)), spec],
        out_specs=[spec, spec])
    return pl.pallas_call(body, name=name, grid_spec=gs,
                          out_shape=[jax.ShapeDtypeStruct((nj, h, w), F32), jax.ShapeDtypeStruct((nj, h, w), BF16)],
                          compiler_params=_cparams(("parallel", "parallel")))(jnp.reshape(c, (1,)).astype(I32), g, recv)


def _add_own_chip(sp, recv, j, *, name):
    _, r, w = sp.shape
    tr = _row_tile(r, w)

    def body(j_ref, s_ref, r_ref, o_ref):
        o_ref[...] = ((s_ref[0] + r_ref[0].astype(F32)) + r_ref[1].astype(F32)) + r_ref[2].astype(F32)

    gs = pltpu.PrefetchScalarGridSpec(
        num_scalar_prefetch=1, grid=(r // tr,),
        in_specs=[pl.BlockSpec((1, tr, w), lambda i, jj: (jj[0], i, 0)), pl.BlockSpec((3, tr, w), lambda i, jj: (0, i, 0))],
        out_specs=pl.BlockSpec((tr, w), lambda i, jj: (i, 0)))
    return pl.pallas_call(body, name=name, grid_spec=gs, out_shape=jax.ShapeDtypeStruct((r, w), F32),
                          compiler_params=_cparams(("parallel",)))(jnp.reshape(j, (1,)).astype(I32), sp, recv)


def _sum8(v):
    _, m, n = v.shape

    def body(v_ref, o_ref):
        acc = v_ref[0]
        for k in range(1, N_DEV):
            acc = acc + v_ref[k]
        o_ref[...] = acc

    return pl.pallas_call(body, name="sum8", grid=(1,), in_specs=[pl.BlockSpec((N_DEV, m, n), lambda i: (0, 0, 0))],
                          out_specs=pl.BlockSpec((m, n), lambda i: (0, 0)), out_shape=jax.ShapeDtypeStruct((m, n), F32),
                          compiler_params=_cparams(("arbitrary",)))(v)


def _ada_mod(c_all, ada_w, ada_b_loc, *, tn=512):
    nl, _, ncol = ada_w.shape

    def body(c_ref, w_ref, b_ref, o_ref):
        cv = c_ref[...]
        ca = (cv * _sigmoid(cv)).astype(BF16)
        o_ref[0] = _dot(ca, w_ref[0].astype(BF16)) + b_ref[0]

    return pl.pallas_call(
        body, name="ada_mod", grid=(nl, ncol // tn),
        in_specs=[pl.BlockSpec((N_DEV, D), lambda l, j: (0, 0)), pl.BlockSpec((1, D, tn), lambda l, j: (l, 0, j)),
                  pl.BlockSpec((1, 1, tn), lambda l, j: (l, 0, j))],
        out_specs=pl.BlockSpec((1, N_DEV, tn), lambda l, j: (l, 0, j)),
        out_shape=jax.ShapeDtypeStruct((nl, N_DEV, ncol), F32),
        compiler_params=_cparams(("parallel", "parallel")))(c_all, ada_w, ada_b_loc)


def _ada_grad(c_pad, dmod_pad, *, tn=512):
    nl, nb, ncol = dmod_pad.shape

    def body(c_ref, d_ref, o_ref):
        cv = c_ref[...]
        ca = (cv * _sigmoid(cv)).astype(BF16)
        o_ref[0] = _dot(ca, d_ref[0].astype(BF16), TN)

    return pl.pallas_call(
        body, name="ada_grad", grid=(nl, ncol // tn),
        in_specs=[pl.BlockSpec((nb, D), lambda l, j: (0, 0)), pl.BlockSpec((1, nb, tn), lambda l, j: (l, 0, j))],
        out_specs=pl.BlockSpec((1, D, tn), lambda l, j: (l, 0, j)),
        out_shape=jax.ShapeDtypeStruct((nl, D, ncol), F32),
        compiler_params=_cparams(("parallel", "parallel")))(c_pad, dmod_pad)


def _adamw(w, g, m, v, *, name):
    shape = w.shape
    if w.ndim == 3:
        lead, rows, cols = shape
    else:
        lead, (rows, cols) = 1, shape
    w3, g3, m3, v3 = (a.reshape(lead, rows, cols) for a in (w, g, m, v))
    tr = rows
    if rows * cols * 4 > (2 << 20):
        tr = next(cand for cand in (256, 128, 64, 8) if rows % cand == 0)
    c1 = 1.0 - ADAM_B1 ** ADAM_STEP
    c2 = 1.0 - ADAM_B2 ** ADAM_STEP

    def body(w_ref, g_ref, m_ref, v_ref, d_ref, nm_ref, nv_ref):
        gv = g_ref[...]
        nm = ADAM_B1 * m_ref[...] + (1.0 - ADAM_B1) * gv
        nv = ADAM_B2 * v_ref[...] + (1.0 - ADAM_B2) * (gv * gv)
        nm_ref[...] = nm
        nv_ref[...] = nv
        d_ref[...] = -ADAM_LR * ((nm / c1) / (jnp.sqrt(nv / c2) + ADAM_EPS) + ADAM_WD * w_ref[...])

    spec = pl.BlockSpec((1, tr, cols), lambda l, i: (l, i, 0))
    outs = pl.pallas_call(
        body, name=name, grid=(lead, rows // tr), in_specs=[spec] * 4, out_specs=[spec] * 3,
        out_shape=[jax.ShapeDtypeStruct((lead, rows, cols), F32)] * 3,
        compiler_params=_cparams(("parallel", "parallel")))(w3, g3, m3, v3)
    return tuple(o.reshape(shape) for o in outs)


def _adamw_layer(w, g_own, g_sib, m, v, c, layer, prev, after, *, name):
    _, rows, cols = w.shape
    tr = _row_tile(rows // 2, cols)
    nbh = rows // 2 // tr
    c1 = 1.0 - ADAM_B1 ** ADAM_STEP
    c2 = 1.0 - ADAM_B2 ** ADAM_STEP
    n_prev = 0 if prev is None else 4

    def body(c_ref, w_ref, o_ref, s_ref, m_ref, v_ref, *rest):
        g_ref, d_ref, nm_ref, nv_ref = rest[n_prev + 1:]
        gv = jnp.where(pl.program_id(0) // nbh == c_ref[0], o_ref[...], s_ref[...])
        nm = ADAM_B1 * m_ref[0] + (1.0 - ADAM_B1) * gv
        nv = ADAM_B2 * v_ref[0] + (1.0 - ADAM_B2) * (gv * gv)
        g_ref[0] = gv
        nm_ref[0] = nm
        nv_ref[0] = nv
        d_ref[0] = -ADAM_LR * ((nm / c1) / (jnp.sqrt(nv / c2) + ADAM_EPS) + ADAM_WD * w_ref[0])

    def source(own):
        return pl.BlockSpec((tr, cols), lambda i, cc: (jnp.where((i // nbh == cc[0]) == own, i % nbh, 0), 0))

    spec = pl.BlockSpec((1, tr, cols), lambda i, cc: (layer, i, 0))
    gs = pltpu.PrefetchScalarGridSpec(
        num_scalar_prefetch=1, grid=(2 * nbh,),
        in_specs=[spec, source(True), source(False), spec, spec] + [ANY] * (n_prev + 1), out_specs=[spec] * 4)
    return pl.pallas_call(
        body, name=name, grid_spec=gs, out_shape=[jax.ShapeDtypeStruct(w.shape, F32)] * 4,
        input_output_aliases={6 + k: k for k in range(n_prev)},
        compiler_params=_cparams(("parallel",)))(
            jnp.reshape(c, (1,)).astype(I32), w, g_own, g_sib, m, v, *(prev or ()), after)


SMALL = (("mix_pre_g", (2, D)), ("mix_post_g", (2, D)), ("mlp_pre_g", (2, D)), ("mlp_post_g", (2, D)),
         ("gmlp_ln_g", (2, DM)), ("gmlp_ln_b", (2, DM)), ("gmlp_ws", (2, NG, CH, CH)), ("gmlp_bs", (2, NG, CH)),
         ("conv_b", (2, DM)), ("conv_ln_g", (2, DM)), ("conv_ln_b", (2, DM)), ("fox_bf", (2, NG)))


def _pack_rows(arrays, mult=8):
    flat = []
    for a in arrays:
        f = a.reshape(-1).astype(F32)
        pad = (-f.shape[0]) % LANE
        flat.append(jnp.pad(f, (0, pad)) if pad else f)
    cat = jnp.concatenate(flat)
    rows = cat.shape[0] // LANE
    pad_rows = (-rows) % mult
    if pad_rows:
        cat = jnp.pad(cat, (0, pad_rows * LANE))
    return cat.reshape(-1, LANE)


def _unpack_rows(buf, shapes):
    flat = buf.reshape(-1)
    out, off = [], 0
    for shp in shapes:
        size = 1
        for d in shp:
            size *= d
        out.append(flat[off:off + size].reshape(shp))
        off += size + ((-size) % LANE)
    return out


def _assemble_w_in(w_in_full):
    uv_glu_qkv = w_in_full[:, :3584]
    f = w_in_full[:, 3584:3592]
    gate = w_in_full[:, 3592:]
    fpad = jnp.zeros((D, D_INP - C_F - NG), w_in_full.dtype)
    return jnp.concatenate([gate, uv_glu_qkv, f, fpad], axis=1)


def _disassemble_w_in(g_p):
    return jnp.concatenate([g_p[:, C_UV:C_F], g_p[:, C_F:C_F + NG], g_p[:, :C_UV]], axis=1)


def kernel(x, c, ada_w, ada_b, mix_pre_g, mix_post_g, mlp_pre_g, mlp_post_g, w_in, gmlp_ln_g, gmlp_ln_b, gmlp_ws, gmlp_bs, w_a_out, conv_w, conv_b, conv_ln_g, conv_ln_b, w_b_out, fox_bf, w_c_out, w_out, mlp_w1, mlp_w2, loss_target, m_ada_w, m_ada_b, m_mix_pre_g, m_mix_post_g, m_mlp_pre_g, m_mlp_post_g, m_w_in, m_gmlp_ln_g, m_gmlp_ln_b, m_gmlp_ws, m_gmlp_bs, m_w_a_out, m_conv_w, m_conv_b, m_conv_ln_g, m_conv_ln_b, m_w_b_out, m_fox_bf, m_w_c_out, m_w_out, m_mlp_w1, m_mlp_w2, v_ada_w, v_ada_b, v_mix_pre_g, v_mix_post_g, v_mlp_pre_g, v_mlp_post_g, v_w_in, v_gmlp_ln_g, v_gmlp_ln_b, v_gmlp_ws, v_gmlp_bs, v_w_a_out, v_conv_w, v_conv_b, v_conv_ln_g, v_conv_ln_b, v_w_b_out, v_fox_bf, v_w_c_out, v_w_out, v_mlp_w1, v_mlp_w2):
    weights = dict(ada_w=ada_w, ada_b=ada_b, mix_pre_g=mix_pre_g, mix_post_g=mix_post_g, mlp_pre_g=mlp_pre_g,
                   mlp_post_g=mlp_post_g, w_in=w_in, gmlp_ln_g=gmlp_ln_g, gmlp_ln_b=gmlp_ln_b, gmlp_ws=gmlp_ws,
                   gmlp_bs=gmlp_bs, w_a_out=w_a_out, conv_w=conv_w, conv_b=conv_b, conv_ln_g=conv_ln_g,
                   conv_ln_b=conv_ln_b, w_b_out=w_b_out, fox_bf=fox_bf, w_c_out=w_c_out, w_out=w_out, mlp_w1=mlp_w1,
                   mlp_w2=mlp_w2)
    mom_m = dict(ada_w=m_ada_w, ada_b=m_ada_b, mix_pre_g=m_mix_pre_g, mix_post_g=m_mix_post_g, mlp_pre_g=m_mlp_pre_g,
                 mlp_post_g=m_mlp_post_g, w_in=m_w_in, gmlp_ln_g=m_gmlp_ln_g, gmlp_ln_b=m_gmlp_ln_b, gmlp_ws=m_gmlp_ws,
                 gmlp_bs=m_gmlp_bs, w_a_out=m_w_a_out, conv_w=m_conv_w, conv_b=m_conv_b, conv_ln_g=m_conv_ln_g,
                 conv_ln_b=m_conv_ln_b, w_b_out=m_w_b_out, fox_bf=m_fox_bf, w_c_out=m_w_c_out, w_out=m_w_out,
                 mlp_w1=m_mlp_w1, mlp_w2=m_mlp_w2)
    mom_v = dict(ada_w=v_ada_w, ada_b=v_ada_b, mix_pre_g=v_mix_pre_g, mix_post_g=v_mix_post_g, mlp_pre_g=v_mlp_pre_g,
                 mlp_post_g=v_mlp_post_g, w_in=v_w_in, gmlp_ln_g=v_gmlp_ln_g, gmlp_ln_b=v_gmlp_ln_b, gmlp_ws=v_gmlp_ws,
                 gmlp_bs=v_gmlp_bs, w_a_out=v_w_a_out, conv_w=v_conv_w, conv_b=v_conv_b, conv_ln_g=v_conv_ln_g,
                 conv_ln_b=v_conv_ln_b, w_b_out=v_w_b_out, fox_bf=v_fox_bf, w_c_out=v_w_c_out, w_out=v_w_out,
                 mlp_w1=v_mlp_w1, mlp_w2=v_mlp_w2)
    order = list(weights)
    px, py, pc = _position()
    chip = 2 * px + py
    dev = 2 * chip + pc
    depth = ada_w.shape[0]
    t = x.shape[1]
    xl = x.reshape(t, D)
    tgt = loss_target.reshape(t, D)

    big_names = [b[0] for b in BIG]
    shards = [[weights[n][l].astype(BF16) for n in big_names] for l in range(depth)]
    land_shapes = [(N_CHIPS,) + sh.shape for sh in shards[0]]

    small_in = _pack_rows([c, conv_w])
    gathered = _all_gather8(small_in)
    c_all = gathered[:, :D // LANE, :].reshape(N_DEV, D)
    cw_rows = depth * KW * LANE // LANE
    conv_w_full = jnp.concatenate(
        [gathered[2 * j, D // LANE:D // LANE + cw_rows, :].reshape(depth, KW, LANE) for j in range(N_CHIPS)], axis=2)

    ncol = ada_w.shape[2]
    ada_b_loc = lax.dynamic_slice_in_dim(ada_b, chip * ncol, ncol, axis=1).reshape(depth, 1, ncol)
    mod_sh = _ada_mod(c_all, ada_w, ada_b_loc)
    mod_g = _all_gather8(mod_sh.reshape(-1, LANE)).reshape(N_DEV, depth, N_DEV, ncol)
    mod_all = jnp.concatenate([mod_g[2 * j] for j in range(N_CHIPS)], axis=2)
    mod_mine = lax.dynamic_index_in_dim(mod_all, dev, axis=1, keepdims=False)

    mods = [mod_mine[l].reshape(NMOD, D) for l in range(depth)]

    def gathered_weights(l, idx, lands):
        lands = _ag_d2d(lands, name="ag_d2d")
        w = {}
        for i, g in zip(idx, lands):
            n, r, cdim, ax = BIG[i]
            g = lax.dynamic_update_slice(g, shards[l][i][None], (chip, 0, 0))
            if n == "w_in":
                w["w_in_p"] = _assemble_w_in(g.transpose(1, 0, 2).reshape(r, N_CHIPS * cdim))
            elif n == "mlp_w1":
                w[n] = g
            elif ax == 1:
                w[n] = g.transpose(1, 0, 2).reshape(r, N_CHIPS * cdim)
            else:
                w[n] = g.reshape(N_CHIPS * r, cdim)
        return w

    def local_weights(l):
        w = {}
        for n in ("mix_pre_g", "mix_post_g", "mlp_pre_g", "mlp_post_g", "gmlp_ln_g", "gmlp_ln_b", "conv_b", "conv_ln_g",
                  "conv_ln_b"):
            w[n] = weights[n][l:l + 1]
        tril = jnp.tril(jnp.ones((CH, CH), F32))
        wsm = gmlp_ws[l] * tril
        w["wsm"] = wsm.astype(BF16)
        w["wsmt"] = jnp.swapaxes(wsm, 1, 2).astype(BF16)
        w["bsx"] = jnp.repeat(gmlp_bs[l].T, HD, axis=1)
        w["conv_w"] = conv_w_full[l]
        w["bfp"] = jnp.pad(fox_bf[l], (0, LANE - NG)).reshape(1, LANE)
        return w

    def slabs(gfull, n, r, cdim, ax):
        if n == "mlp_w1":
            return gfull
        if ax == 1:
            return gfull.reshape(gfull.shape[0], N_CHIPS, cdim).transpose(1, 0, 2)
        return gfull.reshape(N_CHIPS, r, cdim)

    def chip_sums(g):
        g["w_in"] = _disassemble_w_in(g.pop("w_in_p"))
        gs = [slabs(g[n], n, r, cdim, ax) for n, r, cdim, ax in BIG]
        from_sibling = _rs_swap(gs, name="rs_swap")
        return [_add_own_half(a, rv, pc, name="rs_add_half_" + n) for a, rv, n in zip(gs, from_sibling, big_names)]

    def reduce_rest(sums, from_chips):
        red = [_add_own_chip(sf, rv, chip, name="rs_add_chip_" + n) for (sf, _), rv, n in zip(sums, from_chips, big_names)]
        return red, _swap_reduced(red)

    assert depth == 2
    ga, gb = [0], list(range(1, len(BIG)))

    def pick(seq, idx):
        return [seq[i] for i in idx]

    ag0a = _ici_start("gather", pick(shards[0], ga), pick(land_shapes, ga), mod_mine, name="ag0a_start")
    lands0a = _ici_wait("gather", ag0a[0], ag0a[1], ag0a[2], ag0a[3], ag0a[4], name="ag0a_wait")
    ag0b = _ici_start("gather", pick(shards[0], gb), pick(land_shapes, gb), lands0a[0], name="ag0b_start")
    ag1 = _ici_start("gather", shards[1], land_shapes, ag0b[4], name="ag1_start")
    mod0 = mods[0] + ag1[4][0, 0]
    layers, saved = [None] * depth, [None] * depth

    def late0(att):
        return gathered_weights(0, gb, _ici_wait("gather", ag0b[0], ag0b[1], ag0b[2], ag0b[3], att, name="ag0b_wait"))

    xs, saved[0], layers[0] = _layer_fwd(xl, mod0, {**local_weights(0), **gathered_weights(0, ga, lands0a)}, late0)
    lands1 = _ici_wait("gather", ag1[0], ag1[1], ag1[2], ag1[3], xs, name="ag1_wait")
    xs, saved[1], layers[1] = _layer_fwd(xs, mods[1], {**local_weights(1), **gathered_weights(1, ga + gb, lands1)})
    loss_local, dx = _loss_and_grad(xs, tgt)
    loss = lax.psum(loss_local, ("x", "y", "c"))
    grads, dmods = [None] * depth, [None] * depth
    dx, grads[1], dmods[1] = _layer_bwd(dx, mods[1], layers[1], saved[1])
    sums1 = chip_sums(grads[1])
    sbf1 = [sb for _, sb in sums1]
    slot_shapes = [(3,) + sb.shape[1:] for sb in sbf1]
    rs1 = _ici_start("scatter", sbf1, slot_shapes, dx, name="rs1_start")
    dx, grads[0], dmods[0] = _layer_bwd(dx, mod0 + rs1[4][0, 0], layers[0], saved[0])
    grad_x = dx.reshape(x.shape)
    red1 = reduce_rest(sums1, _ici_wait("scatter", rs1[0], rs1[1], rs1[2], rs1[3], dx, name="rs1_wait"))
    sums0 = chip_sums(grads[0])
    g_out = {}

    small_names = [n for n, _ in SMALL]
    small_list = [jnp.stack(dmods)] + [jnp.stack([grads[l][n] for l in range(depth)]) for n in small_names]
    small_list.append(jnp.stack([grads[l]["conv_w"] for l in range(depth)]))
    small_shapes = [(depth, NMOD * D)] + [shp for _, shp in SMALL] + [(depth, KW, DM)]
    small_all = _all_gather8(_pack_rows(small_list))
    rs0 = _ici_start("scatter", [sb for _, sb in sums0], slot_shapes, small_all, name="rs0_start")
    tok0 = rs0[4][0, 0]
    c_all = c_all + tok0
    small_sum = _unpack_rows(_sum8(small_all) + tok0, small_shapes)
    g_out["ada_b"] = small_sum[0]
    for n, gs in zip(small_names, small_sum[1:-1]):
        g_out[n] = gs
    g_out["conv_w"] = lax.dynamic_slice_in_dim(small_sum[-1], chip * LANE, LANE, axis=2)
    dmod_all = small_all[:, :depth * NMOD * D // LANE, :].reshape(N_DEV, depth, NMOD * D)
    dmod_loc = lax.dynamic_slice_in_dim(dmod_all, chip * ncol, ncol, axis=2).transpose(1, 0, 2)
    g_out["ada_w"] = _ada_grad(jnp.pad(c_all, ((0, 8), (0, 0))), jnp.pad(dmod_loc, ((0, 0), (0, 8), (0, 0))))

    delta, new_m, new_v = {}, {}, {}
    delta["ada_w"], new_m["ada_w"], new_v["ada_w"] = _adamw(ada_w, g_out["ada_w"], m_ada_w, v_ada_w, name="adamw_ada_w")
    small_params = ["ada_b"] + small_names + ["conv_w"]
    packs = [_pack_rows([d[n] for n in small_params]) for d in (weights, g_out, mom_m, mom_v)]
    outs = _adamw(*packs, name="adamw_small")
    shapes = [weights[n].shape for n in small_params]
    for dst, buf in zip((delta, new_m, new_v), outs):
        for n, a in zip(small_params, _unpack_rows(buf, shapes)):
            dst[n] = a
    half = {n: _adamw_layer(weights[n], red1[0][wi], red1[1][wi], mom_m[n], mom_v[n], pc, 1, None, rs0[4],
                            name="adamw1_" + n) for wi, n in enumerate(big_names)}
    done = jnp.stack([delta["ada_w"][0, 0, 0], outs[0][0, 0]] + [half[n][1][1, 0, 0] for n in big_names])
    red0 = reduce_rest(sums0, _ici_wait("scatter", rs0[0], rs0[1], rs0[2], rs0[3], done, name="rs0_wait"))
    for wi, n in enumerate(big_names):
        g_out[n], delta[n], new_m[n], new_v[n] = _adamw_layer(
            weights[n], red0[0][wi], red0[1][wi], mom_m[n], mom_v[n], pc, 0, half[n], rs0[4], name="adamw0_" + n)

    return (loss, grad_x, *[g_out[n] for n in order], *[delta[n] for n in order], *[new_m[n] for n in order],
            *[new_v[n] for n in order])
```

```python
import functools

import jax
import jax.numpy as jnp
from jax import lax
from jax.experimental import pallas as pl
from jax.experimental.pallas import tpu as pltpu

F32 = jnp.float32
BF16 = jnp.bfloat16
I32 = jnp.int32
MESH = pl.DeviceIdType.MESH
ANY = pl.BlockSpec(memory_space=pl.ANY)

D = 1024
DM = 512
NG = 8
CH = 128
KW = 31
HALO = 32
DFF = 4096
NMOD = 6
EPS = 1e-6
LANE = 128
N_CHIPS = 4
N_DEV = 8
C_GATE, C_UV, C_GLU, C_Q, C_K, C_V, C_F, D_INP = 0, 3072, 4096, 5120, 5632, 6144, 6656, 7168
D_IN = 6664
VMEM_LIMIT = 56 * 1024 * 1024
TK_DEEP = 4096

ADAM_LR, ADAM_B1, ADAM_B2, ADAM_EPS, ADAM_WD, ADAM_STEP = 0.001, 0.9, 0.999, 1e-08, 0.01, 10

BIG = (("w_in", 1024, 1666, 1), ("w_a_out", 512, 256, 1), ("w_b_out", 512, 256, 1), ("w_c_out", 512, 256, 1),
       ("w_out", 256, 1024, 0), ("mlp_w1", 1024, 1024, 1), ("mlp_w2", 1024, 1024, 0))


def _cparams(sem):
    return pltpu.CompilerParams(dimension_semantics=sem, vmem_limit_bytes=VMEM_LIMIT)


def _sigmoid(x):
    return jax.nn.sigmoid(x)


_GELU_K = 0.7978845608028654
_GELU_A = 0.044715


def _gelu(x):
    t = jnp.tanh(_GELU_K * (x + _GELU_A * x * x * x))
    return 0.5 * x * (1.0 + t)


def _gelu_grad(x):
    t = jnp.tanh(_GELU_K * (x + _GELU_A * x * x * x))
    return 0.5 * (1.0 + t) + 0.5 * x * (1.0 - t * t) * _GELU_K * (1.0 + 3.0 * _GELU_A * x * x)


def _mean(x):
    return jnp.mean(x, axis=-1, keepdims=True)


def _colsum(x):
    return jnp.sum(x, axis=0, keepdims=True)


def _dot(a, b, dims=((1,), (0,))):
    return lax.dot_general(a, b, (dims, ((), ())), preferred_element_type=F32)


NN = ((1,), (0,))
NT = ((1,), (1,))
TN = ((0,), (0,))


def _matmul(a, b, *, name, ta=False, tb=False, out_dtype=F32, tm=1024, tn=1024, tk=1024, epilogue=None, extra=(),
            extra_out=(), b_slabs=False, out_slabs=0):
    m, k = (a.shape[1], a.shape[0]) if ta else a.shape
    if b_slabs:
        ns, brows, bw = b.shape
        n = brows if tb else ns * bw
        assert (ns * bw if tb else brows) == k, (name, b.shape, k)
        tn, tk = (tn, bw) if tb else (bw, tk)
    else:
        n = b.shape[0] if tb else b.shape[1]
    tm, tn, tk = min(tm, m), min(tn, n), min(tk, k)
    assert m % tm == 0 and n % tn == 0 and k % tk == 0, (name, m, n, k, tm, tn, tk)
    assert not out_slabs or (n // out_slabs == tn and epilogue is None), name
    nk = k // tk
    dims = ((0 if ta else 1,), (1 if tb else 0,))
    n_extra = len(extra)
    out_dtypes = (out_dtype,) + tuple(extra_out)

    def body(a_ref, b_ref, *rest):
        extra_refs = rest[:n_extra]
        out_refs = rest[n_extra:n_extra + len(out_dtypes)]
        kk = pl.program_id(2)
        part = _dot(a_ref[...].astype(BF16), b_ref[...].astype(BF16), dims)

        def finish(acc):
            outs = (acc,) if epilogue is None else epilogue(acc, *[r[...] for r in extra_refs])
            for o_ref, o in zip(out_refs, outs):
                o_ref[...] = o.astype(o_ref.dtype)

        if nk == 1:
            finish(part)
        else:
            acc_ref = rest[-1]

            @pl.when(kk == 0)
            def _():
                acc_ref[...] = part

            @pl.when(jnp.logical_and(kk > 0, kk < nk - 1))
            def _():
                acc_ref[...] += part

            @pl.when(kk == nk - 1)
            def _():
                finish(acc_ref[...] + part)

    a_spec = pl.BlockSpec((tk, tm), lambda i, j, kk: (kk, i)) if ta else pl.BlockSpec((tm, tk), lambda i, j, kk: (i, kk))
    if b_slabs and tb:
        b_spec = pl.BlockSpec((None, tn, tk), lambda i, j, kk: (kk, j, 0))
    elif b_slabs:
        b_spec = pl.BlockSpec((None, tk, tn), lambda i, j, kk: (j, kk, 0))
    else:
        b_spec = pl.BlockSpec((tn, tk), lambda i, j, kk: (j, kk)) if tb else pl.BlockSpec((tk, tn), lambda i, j, kk: (kk, j))
    if out_slabs:
        o_spec = pl.BlockSpec((None, tm, tn), lambda i, j, kk: (j, i, 0))
        o_shape = (out_slabs, m, tn)
    else:
        o_spec = pl.BlockSpec((tm, tn), lambda i, j, kk: (i, j))
        o_shape = (m, n)
    outs = pl.pallas_call(
        body, name=name, grid=(m // tm, n // tn, nk),
        in_specs=[a_spec, b_spec] + [o_spec] * n_extra,
        out_specs=[o_spec] * len(out_dtypes),
        out_shape=[jax.ShapeDtypeStruct(o_shape, dt) for dt in out_dtypes],
        scratch_shapes=[pltpu.VMEM((tm, tn), F32)] if nk > 1 else [],
        compiler_params=_cparams(("parallel", "parallel", "arbitrary")),
    )(a, b, *extra)
    return outs[0] if len(outs) == 1 else outs


def _rows(tm, n, col=0):
    return pl.BlockSpec((tm, n), lambda i: (i, col))


def _vec(n):
    return pl.BlockSpec((1, n), lambda i: (0, 0))


def _norm_mod(x, g, sc, sh, *, name, tm=256):
    t = x.shape[0]
    tm = min(tm, t)

    def body(x_ref, g_ref, sc_ref, sh_ref, h_ref):
        xv = x_ref[...]
        inv = lax.rsqrt(_mean(xv * xv) + EPS)
        h_ref[...] = ((xv * inv * g_ref[...]) * (1.0 + sc_ref[...]) + sh_ref[...]).astype(BF16)

    return pl.pallas_call(
        body, name=name, grid=(t // tm,), in_specs=[_rows(tm, D), _vec(D), _vec(D), _vec(D)],
        out_specs=_rows(tm, D), out_shape=jax.ShapeDtypeStruct((t, D), BF16),
        compiler_params=_cparams(("parallel",)))(x, g, sc, sh)


def _resid(x, y, gt, gp, *, name, tm=256):
    t = x.shape[0]
    tm = min(tm, t)

    def body(x_ref, y_ref, gt_ref, gp_ref, o_ref):
        yv = y_ref[...]
        inv = lax.rsqrt(_mean(yv * yv) + EPS)
        o_ref[...] = x_ref[...] + gt_ref[...] * (yv * inv * gp_ref[...])

    return pl.pallas_call(
        body, name=name, grid=(t // tm,), in_specs=[_rows(tm, D), _rows(tm, D), _vec(D), _vec(D)],
        out_specs=_rows(tm, D), out_shape=jax.ShapeDtypeStruct((t, D), F32),
        compiler_params=_cparams(("parallel",)))(x, y, gt, gp)


def _resid_bwd(dx, y, gt, gp, *, name, tm=256):
    t = dx.shape[0]
    tm = min(tm, t)

    def body(dx_ref, y_ref, gt_ref, gp_ref, dy_ref, dgt_ref, dgp_ref):
        @pl.when(pl.program_id(0) == 0)
        def _():
            dgt_ref[...] = jnp.zeros_like(dgt_ref)
            dgp_ref[...] = jnp.zeros_like(dgp_ref)

        dxv, yv, gp_v = dx_ref[...], y_ref[...], gp_ref[...]
        inv = lax.rsqrt(_mean(yv * yv) + EPS)
        yh = yv * inv
        dgt_ref[...] += _colsum(dxv * (yh * gp_v))
        dr = dxv * gt_ref[...]
        dgp_ref[...] += _colsum(dr * yh)
        dyn = dr * gp_v
        dy_ref[...] = (inv * (dyn - yh * _mean(dyn * yh))).astype(BF16)

    return pl.pallas_call(
        body, name=name, grid=(t // tm,), in_specs=[_rows(tm, D), _rows(tm, D), _vec(D), _vec(D)],
        out_specs=[_rows(tm, D), _vec(D), _vec(D)],
        out_shape=[jax.ShapeDtypeStruct((t, D), BF16), jax.ShapeDtypeStruct((1, D), F32),
                   jax.ShapeDtypeStruct((1, D), F32)],
        compiler_params=_cparams(("arbitrary",)))(dx, y, gt, gp)


def _norm_bwd(dh, dx_res, x, g, sc, *, name, tm=256):
    t = dh.shape[0]
    tm = min(tm, t)

    def body(dh_ref, dxr_ref, x_ref, g_ref, sc_ref, dx_ref, dg_ref, dsc_ref, dsh_ref):
        @pl.when(pl.program_id(0) == 0)
        def _():
            dg_ref[...] = jnp.zeros_like(dg_ref)
            dsc_ref[...] = jnp.zeros_like(dsc_ref)
            dsh_ref[...] = jnp.zeros_like(dsh_ref)

        dhv, xv, gv = dh_ref[...], x_ref[...], g_ref[...]
        inv = lax.rsqrt(_mean(xv * xv) + EPS)
        xh = xv * inv
        dsh_ref[...] += _colsum(dhv)
        dsc_ref[...] += _colsum(dhv * (xh * gv))
        dn = dhv * (1.0 + sc_ref[...])
        dg_ref[...] += _colsum(dn * xh)
        dxh = dn * gv
        dx_ref[...] = inv * (dxh - xh * _mean(dxh * xh)) + dxr_ref[...]

    vec_out = jax.ShapeDtypeStruct((1, D), F32)
    return pl.pallas_call(
        body, name=name, grid=(t // tm,), in_specs=[_rows(tm, D), _rows(tm, D), _rows(tm, D), _vec(D), _vec(D)],
        out_specs=[_rows(tm, D), _vec(D), _vec(D), _vec(D)],
        out_shape=[jax.ShapeDtypeStruct((t, D), F32), vec_out, vec_out, vec_out],
        compiler_params=_cparams(("arbitrary",)))(dh, dx_res, x, g, sc)


def _loss_and_grad(x, target, *, tm=256):
    t = x.shape[0]
    tm = min(tm, t)

    def body(x_ref, t_ref, loss_ref, dx_ref):
        @pl.when(pl.program_id(0) == 0)
        def _():
            loss_ref[...] = jnp.zeros_like(loss_ref)

        e = x_ref[...] - t_ref[...]
        dx_ref[...] = e * (1.0 / D)
        s = jnp.sum(jnp.sum(e * e, axis=1, keepdims=True), axis=0, keepdims=True) * (0.5 / D)
        loss_ref[...] += jnp.broadcast_to(s, loss_ref.shape)

    loss, dx = pl.pallas_call(
        body, name="loss", grid=(t // tm,), in_specs=[_rows(tm, D), _rows(tm, D)],
        out_specs=[pl.BlockSpec((8, LANE), lambda i: (0, 0)), _rows(tm, D)],
        out_shape=[jax.ShapeDtypeStruct((8, LANE), F32), jax.ShapeDtypeStruct((t, D), F32)],
        compiler_params=_cparams(("arbitrary",)))(x, target)
    return loss[0, 0], dx


def _gmlp_core(uv, lng, lnb, ws_ref, bsx):
    tm = uv.shape[0]
    gu = _gelu(uv[:, :DM])
    gv = _gelu(uv[:, DM:])
    mu = _mean(gv)
    vc = gv - mu
    rstd = lax.rsqrt(_mean(vc * vc) + EPS)
    vh = vc * rstd
    vln = vh * lng + lnb
    lane = lax.broadcasted_iota(I32, (CH, LANE), 1)
    sv_rows = []
    for nchunk in range(tm // CH):
        vb = vln[nchunk * CH:(nchunk + 1) * CH].astype(BF16)
        cols = []
        for cb in range(DM // LANE):
            vcb = vb[:, cb * LANE:(cb + 1) * LANE]
            lo = _dot(ws_ref[2 * cb], vcb)
            hi = _dot(ws_ref[2 * cb + 1], vcb)
            cols.append(jnp.where(lane < 64, lo, hi))
        sv_rows.append(jnp.concatenate(cols, axis=1) + bsx)
    sv = jnp.concatenate(sv_rows, axis=0) if len(sv_rows) > 1 else sv_rows[0]
    return gu, vh, rstd, vln, sv


def _gmlp_fwd(proj, lng, lnb, wsm, bsx, *, tm=256):
    t = proj.shape[0]
    tm = min(tm, t)

    def body(uv_ref, lng_ref, lnb_ref, ws_ref, bs_ref, ga_ref):
        gu, _, _, _, sv = _gmlp_core(uv_ref[...], lng_ref[...], lnb_ref[...], ws_ref, bs_ref[...])
        ga_ref[...] = (gu * sv).astype(BF16)

    return pl.pallas_call(
        body, name="gmlp_fwd", grid=(t // tm,),
        in_specs=[_rows(tm, 2 * DM, C_UV // (2 * DM)), _vec(DM), _vec(DM),
                  pl.BlockSpec((NG, CH, CH), lambda i: (0, 0, 0)), pl.BlockSpec((CH, DM), lambda i: (0, 0))],
        out_specs=_rows(tm, DM), out_shape=jax.ShapeDtypeStruct((t, DM), BF16),
        compiler_params=_cparams(("parallel",)))(proj, lng, lnb, wsm, bsx)


def _gmlp_bwd(dga, proj, lng, lnb, wsm, wsmt, bsx, *, tm=256):
    t = proj.shape[0]
    tm = min(tm, t)

    def body(dga_ref, uv_ref, lng_ref, lnb_ref, ws_ref, wst_ref, bs_ref, duv_ref, dws_ref, dbs_ref, dlng_ref, dlnb_ref,
             dbsx_ref):
        i = pl.program_id(0)

        @pl.when(i == 0)
        def _():
            dws_ref[...] = jnp.zeros_like(dws_ref)
            dbsx_ref[...] = jnp.zeros_like(dbsx_ref)
            dlng_ref[...] = jnp.zeros_like(dlng_ref)
            dlnb_ref[...] = jnp.zeros_like(dlnb_ref)

        uv = uv_ref[...]
        lng_v = lng_ref[...]
        gu, vh, rstd, vln, sv = _gmlp_core(uv, lng_v, lnb_ref[...], ws_ref, bs_ref[...])
        dga_v = dga_ref[...]
        dgu = dga_v * sv
        dsv = dga_v * gu
        lane = lax.broadcasted_iota(I32, (CH, LANE), 1)
        tril = lax.broadcasted_iota(I32, (CH, CH), 0) >= lax.broadcasted_iota(I32, (CH, CH), 1)
        dvln_rows = []
        for nchunk in range(tm // CH):
            rows = slice(nchunk * CH, (nchunk + 1) * CH)
            dbsx_ref[...] += dsv[rows]
            vb = vln[rows].astype(BF16)
            cols = []
            for cb in range(DM // LANE):
                cs = slice(cb * LANE, (cb + 1) * LANE)
                dsvb = dsv[rows, cs]
                vcb = vb[:, cs]
                dlo = jnp.where(lane < 64, dsvb, 0.0).astype(BF16)
                dhi = jnp.where(lane < 64, 0.0, dsvb).astype(BF16)
                dws_ref[2 * cb] += jnp.where(tril, _dot(dlo, vcb, NT), 0.0)
                dws_ref[2 * cb + 1] += jnp.where(tril, _dot(dhi, vcb, NT), 0.0)
                dsb = dsvb.astype(BF16)
                cols.append(jnp.where(lane < 64, _dot(wst_ref[2 * cb], dsb), _dot(wst_ref[2 * cb + 1], dsb)))
            dvln_rows.append(jnp.concatenate(cols, axis=1))
        dvln = jnp.concatenate(dvln_rows, axis=0) if len(dvln_rows) > 1 else dvln_rows[0]
        dlnb_ref[...] += _colsum(dvln)
        dlng_ref[...] += _colsum(dvln * vh)
        dvh = dvln * lng_v
        dgv = rstd * (dvh - _mean(dvh) - vh * _mean(dvh * vh))
        duv_ref[:, :DM] = (dgu * _gelu_grad(uv[:, :DM])).astype(BF16)
        duv_ref[:, DM:] = (dgv * _gelu_grad(uv[:, DM:])).astype(BF16)

        @pl.when(i == pl.num_programs(0) - 1)
        def _():
            ind = (lax.broadcasted_iota(I32, (DM, LANE), 0) // 64 == lax.broadcasted_iota(I32, (DM, LANE), 1)).astype(F32)
            dbs_ref[...] = jnp.dot(dbsx_ref[...], ind, preferred_element_type=F32, precision=lax.Precision.HIGHEST)

    vec_out = jax.ShapeDtypeStruct((1, DM), F32)
    outs = pl.pallas_call(
        body, name="gmlp_bwd", grid=(t // tm,),
        in_specs=[_rows(tm, DM), _rows(tm, 2 * DM, C_UV // (2 * DM)), _vec(DM), _vec(DM),
                  pl.BlockSpec((NG, CH, CH), lambda i: (0, 0, 0)), pl.BlockSpec((NG, CH, CH), lambda i: (0, 0, 0)),
                  pl.BlockSpec((CH, DM), lambda i: (0, 0))],
        out_specs=[_rows(tm, 2 * DM), pl.BlockSpec((NG, CH, CH), lambda i: (0, 0, 0)),
                   pl.BlockSpec((CH, LANE), lambda i: (0, 0)), _vec(DM), _vec(DM)],
        out_shape=[jax.ShapeDtypeStruct((t, 2 * DM), BF16), jax.ShapeDtypeStruct((NG, CH, CH), F32),
                   jax.ShapeDtypeStruct((CH, LANE), F32), vec_out, vec_out],
        scratch_shapes=[pltpu.VMEM((CH, DM), F32)],
        compiler_params=_cparams(("arbitrary",)))(dga, proj, lng, lnb, wsm, wsmt, bsx)
    return outs


def _glu_into(zs_ref, glu_ref, halo_ref, first):
    hal = halo_ref[...]
    z0h = hal[:, :DM] * _sigmoid(hal[:, DM:])
    zs_ref[0:HALO, :] = jnp.where(first, 0.0, z0h)
    g = glu_ref[...]
    zs_ref[HALO:, :] = g[:, :DM] * _sigmoid(g[:, DM:])


SUB = 8


def _shifted_copies(dst_ref, src_ref):
    n = src_ref.shape[0]
    for s in range(SUB):
        dst_ref[s, 0:n - s, :] = src_ref[s:n, :]


def _window(shifted_ref, off, rows, cs):
    s = off % SUB
    return shifted_ref[s, off - s:off - s + rows, cs]


def _conv_fwd(proj, cw, cb, lng, lnb, *, tm=256, rb=64):
    t = proj.shape[0]
    tm = min(tm, t)
    hb = tm // HALO
    gcol = C_GLU // (2 * DM)

    def body(glu_ref, halo_ref, cw_ref, cb_ref, lng_ref, lnb_ref, zc_ref, zb_ref, zs_ref, zsh_ref):
        i = pl.program_id(0)
        _glu_into(zs_ref, glu_ref, halo_ref, i == 0)
        _shifted_copies(zsh_ref, zs_ref)
        for cbk in range(DM // LANE):
            cs = slice(cbk * LANE, (cbk + 1) * LANE)
            for r in range(tm // rb):
                acc = jnp.broadcast_to(cb_ref[:, cs], (rb, LANE))
                for k in range(KW):
                    acc = acc + cw_ref[k:k + 1, cs] * _window(zsh_ref, r * rb + HALO - (KW - 1) + k, rb, cs)
                zc_ref[r * rb:(r + 1) * rb, cs] = acc
        zc = zc_ref[...]
        mu = _mean(zc)
        zcc = zc - mu
        zh = zcc * lax.rsqrt(_mean(zcc * zcc) + EPS)
        a = zh * lng_ref[...] + lnb_ref[...]
        zb_ref[...] = (a * _sigmoid(a)).astype(BF16)

    return pl.pallas_call(
        body, name="conv_fwd", grid=(t // tm,),
        in_specs=[_rows(tm, 2 * DM, gcol),
                  pl.BlockSpec((HALO, 2 * DM), lambda i: (jnp.maximum(i * hb - 1, 0), gcol)),
                  pl.BlockSpec((KW, DM), lambda i: (0, 0)), _vec(DM), _vec(DM), _vec(DM)],
        out_specs=[_rows(tm, DM), _rows(tm, DM)],
        out_shape=[jax.ShapeDtypeStruct((t, DM), F32), jax.ShapeDtypeStruct((t, DM), BF16)],
        scratch_shapes=[pltpu.VMEM((HALO + tm, DM), F32), pltpu.VMEM((SUB, HALO + tm, DM), F32)],
        compiler_params=_cparams(("parallel",)))(proj, proj, cw, cb, lng, lnb)


def _conv_bwd_ln(dzb, zc, lng, lnb, *, tm=256):
    t = zc.shape[0]
    tm = min(tm, t)

    def body(dzb_ref, zc_ref, lng_ref, lnb_ref, dzc_ref, dlng_ref, dlnb_ref):
        @pl.when(pl.program_id(0) == 0)
        def _():
            dlng_ref[...] = jnp.zeros_like(dlng_ref)
            dlnb_ref[...] = jnp.zeros_like(dlnb_ref)

        zc = zc_ref[...]
        lng_v = lng_ref[...]
        mu = _mean(zc)
        zcc = zc - mu
        rstd = lax.rsqrt(_mean(zcc * zcc) + EPS)
        zh = zcc * rstd
        a = zh * lng_v + lnb_ref[...]
        s = _sigmoid(a)
        da = dzb_ref[...] * (s * (1.0 + a * (1.0 - s)))
        dlnb_ref[...] += _colsum(da)
        dlng_ref[...] += _colsum(da * zh)
        dzh = da * lng_v
        dzc_ref[...] = rstd * (dzh - _mean(dzh) - zh * _mean(dzh * zh))

    vec_out = jax.ShapeDtypeStruct((1, DM), F32)
    return pl.pallas_call(
        body, name="conv_bwd_ln", grid=(t // tm,), in_specs=[_rows(tm, DM), _rows(tm, DM), _vec(DM), _vec(DM)],
        out_specs=[_rows(tm, DM), _vec(DM), _vec(DM)],
        out_shape=[jax.ShapeDtypeStruct((t, DM), F32), vec_out, vec_out],
        compiler_params=_cparams(("arbitrary",)))(dzb, zc, lng, lnb)


def _conv_bwd(dzc, proj, cw, *, tm=256, rb=64):
    t = proj.shape[0]
    tm = min(tm, t)
    hb = tm // HALO
    nblk = t // tm
    gcol = C_GLU // (2 * DM)

    def body(dzc_ref, dnext_ref, glu_ref, halo_ref, cw_ref, dglu_ref, dcw_ref, dcb_ref, zs_ref, ds_ref, zsh_ref, dsh_ref):
        i = pl.program_id(0)

        @pl.when(i == 0)
        def _():
            dcw_ref[...] = jnp.zeros_like(dcw_ref)
            dcb_ref[...] = jnp.zeros_like(dcb_ref)

        _glu_into(zs_ref, glu_ref, halo_ref, i == 0)
        _shifted_copies(zsh_ref, zs_ref)
        dzc = dzc_ref[...]
        ds_ref[0:tm, :] = dzc
        ds_ref[tm:, :] = jnp.where(i == nblk - 1, 0.0, dnext_ref[...])
        _shifted_copies(dsh_ref, ds_ref)
        dcb_ref[...] += _colsum(dzc)
        for k in range(KW):
            dcw_ref[k:k + 1, :] += _colsum(dzc * _window(zsh_ref, HALO - (KW - 1) + k, tm, slice(None)))
        g = glu_ref[...]
        val, sg = g[:, :DM], _sigmoid(g[:, DM:])
        for cbk in range(DM // LANE):
            cs = slice(cbk * LANE, (cbk + 1) * LANE)
            for r in range(tm // rb):
                acc = jnp.zeros((rb, LANE), F32)
                for k in range(KW):
                    acc = acc + cw_ref[k:k + 1, cs] * _window(dsh_ref, r * rb + (KW - 1) - k, rb, cs)
                rs = slice(r * rb, (r + 1) * rb)
                dglu_ref[rs, cs] = (acc * sg[rs, cs]).astype(BF16)
                v, s = val[rs, cs], sg[rs, cs]
                dglu_ref[rs, DM + cbk * LANE:DM + (cbk + 1) * LANE] = (acc * v * s * (1.0 - s)).astype(BF16)

    return pl.pallas_call(
        body, name="conv_bwd", grid=(nblk,),
        in_specs=[_rows(tm, DM),
                  pl.BlockSpec((HALO, DM), lambda i: (jnp.minimum((i + 1) * hb, nblk * hb - 1), 0)),
                  _rows(tm, 2 * DM, gcol),
                  pl.BlockSpec((HALO, 2 * DM), lambda i: (jnp.maximum(i * hb - 1, 0), gcol)),
                  pl.BlockSpec((KW, DM), lambda i: (0, 0))],
        out_specs=[_rows(tm, 2 * DM), pl.BlockSpec((HALO, DM), lambda i: (0, 0)), _vec(DM)],
        out_shape=[jax.ShapeDtypeStruct((t, 2 * DM), BF16), jax.ShapeDtypeStruct((HALO, DM), F32),
                   jax.ShapeDtypeStruct((1, DM), F32)],
        scratch_shapes=[pltpu.VMEM((HALO + tm, DM), F32), pltpu.VMEM((tm + HALO, DM), F32),
                        pltpu.VMEM((SUB, HALO + tm, DM), F32), pltpu.VMEM((SUB, tm + HALO, DM), F32)],
        compiler_params=_cparams(("arbitrary",)))(dzc, dzc, proj, proj, cw)


def _log_sigmoid(x):
    return jnp.minimum(x, 0.0) - jnp.log1p(jnp.exp(-jnp.abs(x)))


def _fox_cum(proj, bfp):
    t = proj.shape[0]
    fcol = C_F // LANE

    def body(f_ref, bf_ref, cum_ref, carry_ref):
        @pl.when(pl.program_id(0) == 0)
        def _():
            carry_ref[...] = jnp.zeros_like(carry_ref)

        lf = _log_sigmoid(f_ref[...] + bf_ref[...])
        tri = (lax.broadcasted_iota(I32, (CH, CH), 0) >= lax.broadcasted_iota(I32, (CH, CH), 1)).astype(F32)
        cum = jnp.dot(tri, lf, preferred_element_type=F32, precision=lax.Precision.HIGHEST) + carry_ref[0:1, :]
        cum_ref[...] = cum
        carry_ref[...] = jnp.broadcast_to(cum[CH - 1:CH, :], carry_ref.shape)

    return pl.pallas_call(
        body, name="fox_cum", grid=(t // CH,), in_specs=[_rows(CH, LANE, fcol), _vec(LANE)],
        out_specs=_rows(CH, LANE), out_shape=jax.ShapeDtypeStruct((t, LANE), F32),
        scratch_shapes=[pltpu.VMEM((8, LANE), F32)],
        compiler_params=_cparams(("arbitrary",)))(proj, bfp)


def _fox_cum_bwd(dcum, proj, bfp):
    t = proj.shape[0]
    nb = t // CH
    fcol = C_F // LANE
    fw = D_INP - C_F

    def body(dc_ref, f_ref, bf_ref, df_ref, dbf_ref, carry_ref):
        @pl.when(pl.program_id(0) == 0)
        def _():
            carry_ref[...] = jnp.zeros_like(carry_ref)
            dbf_ref[...] = jnp.zeros_like(dbf_ref)

        triu = (lax.broadcasted_iota(I32, (CH, CH), 0) <= lax.broadcasted_iota(I32, (CH, CH), 1)).astype(F32)
        dlf = jnp.dot(triu, dc_ref[...], preferred_element_type=F32, precision=lax.Precision.HIGHEST) + carry_ref[0:1, :]
        carry_ref[...] = jnp.broadcast_to(dlf[0:1, :], carry_ref.shape)
        z = f_ref[...] + bf_ref[...]
        lane = lax.broadcasted_iota(I32, (CH, LANE), 1)
        df = jnp.where(lane < NG, dlf * _sigmoid(-z), 0.0)
        dbf_ref[...] += _colsum(df)
        df_ref[:, 0:LANE] = df.astype(BF16)
        df_ref[:, LANE:] = jnp.zeros((CH, fw - LANE), BF16)

    return pl.pallas_call(
        body, name="fox_cum_bwd", grid=(nb,),
        in_specs=[pl.BlockSpec((CH, LANE), lambda i: (nb - 1 - i, 0)),
                  pl.BlockSpec((CH, LANE), lambda i: (nb - 1 - i, fcol)), _vec(LANE)],
        out_specs=[pl.BlockSpec((CH, fw), lambda i: (nb - 1 - i, 0)), _vec(LANE)],
        out_shape=[jax.ShapeDtypeStruct((t, fw), BF16), jax.ShapeDtypeStruct((1, LANE), F32)],
        scratch_shapes=[pltpu.VMEM((8, LANE), F32)],
        compiler_params=_cparams(("arbitrary",)))(dcum, proj, bfp)


HD = 64
ATT_SCALE = 0.125
NEG = -1e30


def _qkv_prep(proj, *, tm=512):
    t = proj.shape[0]
    tm = min(tm, t)

    def body(q_ref, k_ref, v_ref, o_ref):
        o_ref[:, 0:DM] = (q_ref[...] * ATT_SCALE).astype(BF16)
        o_ref[:, DM:2 * DM] = k_ref[...].astype(BF16)
        o_ref[:, 2 * DM:] = v_ref[...].astype(BF16)

    return pl.pallas_call(
        body, name="qkv_prep", grid=(t // tm,),
        in_specs=[_rows(tm, DM, C_Q // DM), _rows(tm, DM, C_K // DM), _rows(tm, DM, C_V // DM)],
        out_specs=_rows(tm, 3 * DM), out_shape=jax.ShapeDtypeStruct((t, 3 * DM), BF16),
        compiler_params=_cparams(("parallel",)))(proj, proj, proj)


def _causal_pairs(nq, outer_is_query):
    if outer_is_query:
        pairs = [(i, j) for i in range(nq) for j in range(i + 1)]
    else:
        pairs = [(j, i) for j in range(nq) for i in range(j, nq)]
    return (jnp.asarray([p[0] for p in pairs], I32), jnp.asarray([p[1] for p in pairs], I32))


def _to_row(col):
    return jnp.transpose(col)[0:1, :]


def _rep(x, tk):
    return x if tk == LANE else jnp.tile(x, (1, tk // LANE))


def _attn_fwd(qkv, ckrow, *, tq=512):
    t = qkv.shape[0]
    tq = min(tq, t)
    tk = tq
    nq = t // tq
    oi, ij = _causal_pairs(nq, True)

    def body(oi_ref, ij_ref, q_ref, k_ref, v_ref, ck_ref, o_ref, lse_ref, lser_ref, m_ref, l_ref, acc_ref):
        n = pl.program_id(0)
        i, j = oi_ref[n], ij_ref[n]

        @pl.when(j == 0)
        def _():
            m_ref[...] = jnp.full_like(m_ref, NEG)
            l_ref[...] = jnp.zeros_like(l_ref)
            acc_ref[...] = jnp.zeros_like(acc_ref)

        def step(masked):
            if masked:
                keep = lax.broadcasted_iota(I32, (tq, tk), 1) <= lax.broadcasted_iota(I32, (tq, tk), 0)
            lo = lax.broadcasted_iota(I32, (tq, LANE), 1) < HD
            for hp in range(NG // 2):
                cs = slice(hp * LANE, (hp + 1) * LANE)
                qp, kp, vp = q_ref[:, cs], k_ref[:, cs], v_ref[:, cs]
                alphas, pvs = [], []
                for hh in range(2):
                    h = 2 * hp + hh
                    qm = jnp.where(lo if hh == 0 else jnp.logical_not(lo), qp, jnp.zeros_like(qp))
                    s = _dot(qm, kp, NT) - ck_ref[h:h + 1, :]
                    if masked:
                        s = jnp.where(keep, s, NEG)
                    m_prev = m_ref[h]
                    m_new = jnp.maximum(m_prev, jnp.max(s, axis=1, keepdims=True))
                    alpha = jnp.exp(m_prev - m_new)
                    p = jnp.exp(s - _rep(m_new, tk))
                    l_ref[h] = alpha * l_ref[h] + jnp.sum(p, axis=1, keepdims=True)
                    m_ref[h] = m_new
                    alphas.append(alpha)
                    pvs.append(_dot(p.astype(BF16), vp))
                acc_ref[:, cs] = jnp.where(lo, alphas[0], alphas[1]) * acc_ref[:, cs] + jnp.where(lo, pvs[0], pvs[1])

        @pl.when(j < i)
        def _():
            step(False)

        @pl.when(j == i)
        def _():
            step(True)
            lo = lax.broadcasted_iota(I32, (tq, LANE), 1) < HD
            for hp in range(NG // 2):
                cs = slice(hp * LANE, (hp + 1) * LANE)
                o_ref[:, cs] = (acc_ref[:, cs] / jnp.where(lo, l_ref[2 * hp], l_ref[2 * hp + 1])).astype(BF16)
            for h in range(NG):
                lse = m_ref[h] + jnp.log(l_ref[h])
                lse_ref[h] = lse
                lser_ref[h:h + 1, :] = _to_row(lse)

    gs = pltpu.PrefetchScalarGridSpec(
        num_scalar_prefetch=2, grid=(int(oi.shape[0]),),
        in_specs=[pl.BlockSpec((tq, DM), lambda n, a, b: (a[n], 0)),
                  pl.BlockSpec((tk, DM), lambda n, a, b: (b[n], 1)),
                  pl.BlockSpec((tk, DM), lambda n, a, b: (b[n], 2)),
                  pl.BlockSpec((NG, tk), lambda n, a, b: (0, b[n]))],
        out_specs=[pl.BlockSpec((tq, DM), lambda n, a, b: (a[n], 0)),
                   pl.BlockSpec((NG, tq, LANE), lambda n, a, b: (0, a[n], 0)),
                   pl.BlockSpec((NG, tq), lambda n, a, b: (0, a[n]))],
        scratch_shapes=[pltpu.VMEM((NG, tq, LANE), F32), pltpu.VMEM((NG, tq, LANE), F32), pltpu.VMEM((tq, DM), F32)])
    return pl.pallas_call(
        body, name="attn_fwd", grid_spec=gs,
        out_shape=[jax.ShapeDtypeStruct((t, DM), BF16), jax.ShapeDtypeStruct((NG, t, LANE), F32),
                   jax.ShapeDtypeStruct((NG, t), F32)],
        compiler_params=_cparams(("arbitrary",)))(oi, ij, qkv, qkv, qkv, ckrow)


def _attn_bwd_dq(qkv, ckrow, o, do, lse, *, tq=512):
    t = qkv.shape[0]
    tq = min(tq, t)
    tk = tq
    nq = t // tq
    oi, ij = _causal_pairs(nq, True)

    def body(oi_ref, ij_ref, q_ref, k_ref, v_ref, ck_ref, o_ref, do_ref, lse_ref, dq_ref, deltar_ref, dcqr_ref, delta_ref,
             dcq_ref, acc_ref):
        n = pl.program_id(0)
        i, j = oi_ref[n], ij_ref[n]

        @pl.when(j == 0)
        def _():
            acc_ref[...] = jnp.zeros_like(acc_ref)
            dcq_ref[...] = jnp.zeros_like(dcq_ref)
            lo = lax.broadcasted_iota(I32, (tq, LANE), 1) < HD
            for hp in range(NG // 2):
                cs = slice(hp * LANE, (hp + 1) * LANE)
                prod = do_ref[:, cs] * o_ref[:, cs].astype(F32)
                for hh in range(2):
                    d = jnp.sum(jnp.where(lo if hh == 0 else jnp.logical_not(lo), prod, 0.0), axis=1, keepdims=True)
                    dcol = jnp.broadcast_to(d, (tq, LANE))
                    delta_ref[2 * hp + hh] = dcol
                    deltar_ref[2 * hp + hh:2 * hp + hh + 1, :] = _to_row(dcol)

        def step(masked):
            if masked:
                keep = lax.broadcasted_iota(I32, (tq, tk), 1) <= lax.broadcasted_iota(I32, (tq, tk), 0)
            lo = lax.broadcasted_iota(I32, (tq, LANE), 1) < HD
            for hp in range(NG // 2):
                cs = slice(hp * LANE, (hp + 1) * LANE)
                qp, kp, vp, dop = q_ref[:, cs], k_ref[:, cs], v_ref[:, cs], do_ref[:, cs].astype(BF16)
                parts = []
                for hh in range(2):
                    h = 2 * hp + hh
                    sel = lo if hh == 0 else jnp.logical_not(lo)
                    s = _dot(jnp.where(sel, qp, jnp.zeros_like(qp)), kp, NT) - ck_ref[h:h + 1, :]
                    if masked:
                        s = jnp.where(keep, s, NEG)
                    p = jnp.exp(s - _rep(lse_ref[h], tk))
                    dp = _dot(jnp.where(sel, dop, jnp.zeros_like(dop)), vp, NT)
                    ds = p * (dp - _rep(delta_ref[h], tk))
                    dcq_ref[h] += jnp.sum(ds, axis=1, keepdims=True)
                    parts.append(_dot(ds.astype(BF16), kp))
                acc_ref[:, cs] += jnp.where(lo, parts[0], parts[1])

        @pl.when(j < i)
        def _():
            step(False)

        @pl.when(j == i)
        def _():
            step(True)
            dq_ref[...] = (acc_ref[...] * ATT_SCALE).astype(BF16)
            for h in range(NG):
                dcqr_ref[h:h + 1, :] = _to_row(dcq_ref[h])

    gs = pltpu.PrefetchScalarGridSpec(
        num_scalar_prefetch=2, grid=(int(oi.shape[0]),),
        in_specs=[pl.BlockSpec((tq, DM), lambda n, a, b: (a[n], 0)),
                  pl.BlockSpec((tk, DM), lambda n, a, b: (b[n], 1)),
                  pl.BlockSpec((tk, DM), lambda n, a, b: (b[n], 2)),
                  pl.BlockSpec((NG, tk), lambda n, a, b: (0, b[n])),
                  pl.BlockSpec((tq, DM), lambda n, a, b: (a[n], 0)),
                  pl.BlockSpec((tq, DM), lambda n, a, b: (a[n], 0)),
                  pl.BlockSpec((NG, tq, LANE), lambda n, a, b: (0, a[n], 0))],
        out_specs=[pl.BlockSpec((tq, DM), lambda n, a, b: (a[n], 0)),
                   pl.BlockSpec((NG, tq), lambda n, a, b: (0, a[n])),
                   pl.BlockSpec((NG, tq), lambda n, a, b: (0, a[n]))],
        scratch_shapes=[pltpu.VMEM((NG, tq, LANE), F32), pltpu.VMEM((NG, tq, LANE), F32), pltpu.VMEM((tq, DM), F32)])
    return pl.pallas_call(
        body, name="attn_bwd_dq", grid_spec=gs,
        out_shape=[jax.ShapeDtypeStruct((t, DM), BF16), jax.ShapeDtypeStruct((NG, t), F32), jax.ShapeDtypeStruct((NG, t), F32)],
        compiler_params=_cparams(("arbitrary",)))(oi, ij, qkv, qkv, qkv, ckrow, o, do, lse)


def _attn_bwd_dkv(qkv, ckcol, do, lserow, deltarow, *, tq=512):
    t = qkv.shape[0]
    tq = min(tq, t)
    tk = tq
    nq = t // tq
    oj, ii = _causal_pairs(nq, False)

    def body(oj_ref, ii_ref, q_ref, k_ref, v_ref, ck_ref, do_ref, lse_ref, delta_ref, dk_ref, dv_ref, dckr_ref, dka_ref,
             dva_ref, dck_ref):
        n = pl.program_id(0)
        j, i = oj_ref[n], ii_ref[n]

        @pl.when(i == j)
        def _():
            dka_ref[...] = jnp.zeros_like(dka_ref)
            dva_ref[...] = jnp.zeros_like(dva_ref)
            dck_ref[...] = jnp.zeros_like(dck_ref)

        def step(masked):
            if masked:
                keep = lax.broadcasted_iota(I32, (tk, tq), 0) <= lax.broadcasted_iota(I32, (tk, tq), 1)
            lo = lax.broadcasted_iota(I32, (tk, LANE), 1) < HD
            for hp in range(NG // 2):
                cs = slice(hp * LANE, (hp + 1) * LANE)
                qp, kp, vp, dop = q_ref[:, cs], k_ref[:, cs], v_ref[:, cs], do_ref[:, cs].astype(BF16)
                dvs, dks = [], []
                for hh in range(2):
                    h = 2 * hp + hh
                    sel = lo if hh == 0 else jnp.logical_not(lo)
                    st = _dot(jnp.where(sel, kp, jnp.zeros_like(kp)), qp, NT) - _rep(ck_ref[h], tq)
                    if masked:
                        st = jnp.where(keep, st, NEG)
                    pt = jnp.exp(st - lse_ref[h:h + 1, :])
                    dvs.append(_dot(pt.astype(BF16), dop))
                    dpt = _dot(jnp.where(sel, vp, jnp.zeros_like(vp)), dop, NT)
                    dst = pt * (dpt - delta_ref[h:h + 1, :])
                    dks.append(_dot(dst.astype(BF16), qp))
                    dck_ref[h] -= jnp.sum(dst, axis=1, keepdims=True)
                dva_ref[:, cs] += jnp.where(lo, dvs[0], dvs[1])
                dka_ref[:, cs] += jnp.where(lo, dks[0], dks[1])

        @pl.when(i == j)
        def _():
            step(True)

        @pl.when(i > j)
        def _():
            step(False)

        @pl.when(i == nq - 1)
        def _():
            dk_ref[...] = dka_ref[...].astype(BF16)
            dv_ref[...] = dva_ref[...].astype(BF16)
            for h in range(NG):
                dckr_ref[h:h + 1, :] = _to_row(dck_ref[h])

    gs = pltpu.PrefetchScalarGridSpec(
        num_scalar_prefetch=2, grid=(int(oj.shape[0]),),
        in_specs=[pl.BlockSpec((tq, DM), lambda n, a, b: (b[n], 0)),
                  pl.BlockSpec((tk, DM), lambda n, a, b: (a[n], 1)),
                  pl.BlockSpec((tk, DM), lambda n, a, b: (a[n], 2)),
                  pl.BlockSpec((NG, tk, LANE), lambda n, a, b: (0, a[n], 0)),
                  pl.BlockSpec((tq, DM), lambda n, a, b: (b[n], 0)),
                  pl.BlockSpec((NG, tq), lambda n, a, b: (0, b[n])),
                  pl.BlockSpec((NG, tq), lambda n, a, b: (0, b[n]))],
        out_specs=[pl.BlockSpec((tk, DM), lambda n, a, b: (a[n], 0)),
                   pl.BlockSpec((tk, DM), lambda n, a, b: (a[n], 0)),
                   pl.BlockSpec((NG, tk), lambda n, a, b: (0, a[n]))],
        scratch_shapes=[pltpu.VMEM((tk, DM), F32), pltpu.VMEM((tk, DM), F32), pltpu.VMEM((NG, tk, LANE), F32)])
    return pl.pallas_call(
        body, name="attn_bwd_dkv", grid_spec=gs,
        out_shape=[jax.ShapeDtypeStruct((t, DM), BF16), jax.ShapeDtypeStruct((t, DM), BF16), jax.ShapeDtypeStruct((NG, t), F32)],
        compiler_params=_cparams(("arbitrary",)))(oj, ii, qkv, qkv, qkv, ckcol, do, lserow, deltarow)


def _merge_fwd(ga, zb, att, proj, wa, wb, wc, *, tm=256):
    t = ga.shape[0]
    tm = min(tm, t)
    wspec = pl.BlockSpec((DM, D), lambda i: (0, 0))

    def body(ga_ref, zb_ref, att_ref, gate_ref, wa_ref, wb_ref, wc_ref, m_ref):
        acc = jnp.zeros((tm, D), F32)
        for b, (x_ref, w_ref) in enumerate(((ga_ref, wa_ref), (zb_ref, wb_ref), (att_ref, wc_ref))):
            acc = acc + _sigmoid(gate_ref[:, b * D:(b + 1) * D]) * _dot(x_ref[...], w_ref[...])
        m_ref[...] = acc.astype(BF16)

    return pl.pallas_call(
        body, name="merge_fwd", grid=(t // tm,),
        in_specs=[_rows(tm, DM), _rows(tm, DM), _rows(tm, DM), _rows(tm, 3 * D, 0), wspec, wspec, wspec],
        out_specs=_rows(tm, D), out_shape=jax.ShapeDtypeStruct((t, D), BF16),
        compiler_params=_cparams(("parallel",)))(ga, zb, att, proj, wa, wb, wc)


def _merge_bwd(dm, ga, zb, att, proj, wa, wb, wc, *, tm=256):
    t = ga.shape[0]
    tm = min(tm, t)
    wspec = pl.BlockSpec((DM, D), lambda i: (0, 0))

    def body(dm_ref, ga_ref, zb_ref, att_ref, gate_ref, wa_ref, wb_ref, wc_ref, dgate_ref, dga_ref, dzb_ref, datt_ref,
             dwa_ref, dwb_ref, dwc_ref):
        @pl.when(pl.program_id(0) == 0)
        def _():
            dwa_ref[...] = jnp.zeros_like(dwa_ref)
            dwb_ref[...] = jnp.zeros_like(dwb_ref)
            dwc_ref[...] = jnp.zeros_like(dwc_ref)

        dmv = dm_ref[...]
        branches = ((ga_ref, wa_ref, dga_ref, dwa_ref), (zb_ref, wb_ref, dzb_ref, dwb_ref),
                    (att_ref, wc_ref, datt_ref, dwc_ref))
        for b, (x_ref, w_ref, dx_ref, dw_ref) in enumerate(branches):
            xv, wv = x_ref[...], w_ref[...]
            y = _dot(xv, wv)
            g = _sigmoid(gate_ref[:, b * D:(b + 1) * D])
            dgate_ref[:, b * D:(b + 1) * D] = (dmv * y * g * (1.0 - g)).astype(BF16)
            dy = (dmv * g).astype(BF16)
            dx_ref[...] = _dot(dy, wv, NT)
            dw_ref[...] += _dot(xv, dy, TN)

    return pl.pallas_call(
        body, name="merge_bwd", grid=(t // tm,),
        in_specs=[_rows(tm, D), _rows(tm, DM), _rows(tm, DM), _rows(tm, DM), _rows(tm, 3 * D, 0), wspec, wspec, wspec],
        out_specs=[_rows(tm, 3 * D), _rows(tm, DM), _rows(tm, DM), _rows(tm, DM), wspec, wspec, wspec],
        out_shape=[jax.ShapeDtypeStruct((t, 3 * D), BF16)] + [jax.ShapeDtypeStruct((t, DM), F32)] * 3
        + [jax.ShapeDtypeStruct((DM, D), F32)] * 3,
        compiler_params=_cparams(("arbitrary",)))(dm, ga, zb, att, proj, wa, wb, wc)


def _heads_layout(cum):
    t = cum.shape[0]
    ckrow = cum[:, :NG].T
    return ckrow, jnp.broadcast_to(ckrow[:, :, None], (NG, t, LANE))


def _layer_fwd(x, mod, w, late=None):
    sh1, sc1, gt1, sh2, sc2, gt2 = (mod[k:k + 1] for k in range(NMOD))
    h1 = _norm_mod(x, w["mix_pre_g"], sc1, sh1, name="norm_mix")
    proj = _matmul(h1, w["w_in_p"], name="mm_proj")
    ga = _gmlp_fwd(proj, w["gmlp_ln_g"], w["gmlp_ln_b"], w["wsm"], w["bsx"])
    zc, zb = _conv_fwd(proj, w["conv_w"], w["conv_b"], w["conv_ln_g"], w["conv_ln_b"])
    cum = _fox_cum(proj, w["bfp"])
    ckrow, ckcol = _heads_layout(cum)
    qkv = _qkv_prep(proj)
    att, lse, lser = _attn_fwd(qkv, ckrow)
    if late is not None:
        w = {**w, **late(att)}
    merged = _merge_fwd(ga, zb, att, proj, w["w_a_out"], w["w_b_out"], w["w_c_out"])
    y1 = _matmul(merged, w["w_out"], name="mm_out")
    x2 = _resid(x, y1, gt1, w["mix_post_g"], name="resid_mix")
    h2 = _norm_mod(x2, w["mlp_pre_g"], sc2, sh2, name="norm_mlp")
    a, hid = _matmul(h2, w["mlp_w1"], name="mm_w1", b_slabs=True, extra_out=(BF16,),
                     epilogue=lambda acc: (acc, jnp.square(jnp.maximum(acc, 0.0))))
    y2 = _matmul(hid, w["mlp_w2"], name="mm_w2", tk=TK_DEEP)
    x3 = _resid(x2, y2, gt2, w["mlp_post_g"], name="resid_mlp")
    saved = dict(x=x, h1=h1, proj=proj, ga=ga, zc=zc, zb=zb, qkv=qkv, ckrow=ckrow, ckcol=ckcol, att=att, lse=lse, lser=lser, merged=merged,
                 y1=y1, x2=x2, h2=h2, a=a, hid=hid, y2=y2)
    return x3, saved, w


def _layer_bwd(dx3, mod, w, s):
    sh1, sc1, gt1, sh2, sc2, gt2 = (mod[k:k + 1] for k in range(NMOD))
    g = {}
    dy2, dgt2, g["mlp_post_g"] = _resid_bwd(dx3, s["y2"], gt2, w["mlp_post_g"], name="resid_mlp_bwd")
    da = _matmul(dy2, w["mlp_w2"], tb=True, name="mm_dhid", out_dtype=BF16, extra=(s["a"],),
                 epilogue=lambda acc, a: (acc * (2.0 * jnp.maximum(a, 0.0)),))
    g["mlp_w2"] = _matmul(s["hid"], dy2, ta=True, name="mm_dw2", tk=TK_DEEP)
    g["mlp_w1"] = _matmul(s["h2"], da, ta=True, name="mm_dw1", out_slabs=N_CHIPS, tk=TK_DEEP)
    dh2 = _matmul(da, w["mlp_w1"], tb=True, name="mm_dh2", b_slabs=True)
    dx2, g["mlp_pre_g"], dsc2, dsh2 = _norm_bwd(dh2, dx3, s["x2"], w["mlp_pre_g"], sc2, name="norm_mlp_bwd")
    dy1, dgt1, g["mix_post_g"] = _resid_bwd(dx2, s["y1"], gt1, w["mix_post_g"], name="resid_mix_bwd")
    dmerged = _matmul(dy1, w["w_out"], tb=True, name="mm_dmerged")
    g["w_out"] = _matmul(s["merged"], dy1, ta=True, name="mm_dwout", tk=TK_DEEP)
    dgate, dga, dzb, datt, g["w_a_out"], g["w_b_out"], g["w_c_out"] = _merge_bwd(
        dmerged, s["ga"], s["zb"], s["att"], s["proj"], w["w_a_out"], w["w_b_out"], w["w_c_out"])
    duv, g["gmlp_ws"], dbs, g["gmlp_ln_g"], g["gmlp_ln_b"] = _gmlp_bwd(
        dga, s["proj"], w["gmlp_ln_g"], w["gmlp_ln_b"], w["wsm"], w["wsmt"], w["bsx"])
    g["gmlp_bs"] = dbs[:, :NG].T
    dzc, g["conv_ln_g"], g["conv_ln_b"] = _conv_bwd_ln(dzb, s["zc"], w["conv_ln_g"], w["conv_ln_b"])
    dglu, dcw, g["conv_b"] = _conv_bwd(dzc, s["proj"], w["conv_w"])
    g["conv_w"] = dcw[:KW]
    dq, delta, dcq = _attn_bwd_dq(s["qkv"], s["ckrow"], s["att"], datt, s["lse"])
    dk, dv, dck = _attn_bwd_dkv(s["qkv"], s["ckcol"], datt, s["lser"], delta)
    dcum = jnp.pad((dcq + dck).T, ((0, 0), (0, LANE - NG)))
    df, dbf = _fox_cum_bwd(dcum, s["proj"], w["bfp"])
    g["fox_bf"] = dbf[0, :NG]
    dproj = jnp.concatenate([dgate, duv, dglu, dq, dk, dv, df], axis=1)
    g["w_in_p"] = _matmul(s["h1"], dproj, ta=True, name="mm_dwin", tk=TK_DEEP)
    dh1 = _matmul(dproj, w["w_in_p"], tb=True, name="mm_dh1", tk=D_INP // 2)
    dx, g["mix_pre_g"], dsc1, dsh1 = _norm_bwd(dh1, dx2, s["x"], w["mix_pre_g"], sc1, name="norm_mix_bwd")
    dmod = jnp.concatenate([dsh1, dsc1, dgt1, dsh2, dsc2, dgt2], axis=0)
    return dx, g, dmod


def _position():
    return lax.axis_index("x"), lax.axis_index("y"), lax.axis_index("c")


def _all_gather8(v):
    m_per, n = v.shape

    def body(x_ref, out_ref, send_sems, recv_sems, local_sem):
        x, y, c = _position()
        me, sibling = (x, y, c), (x, y, 1 - c)
        chips = [(1 - x, y), (x, 1 - y), (1 - x, 1 - y)]

        def rows(px, py, pc):
            return out_ref.at[pl.ds((4 * px + 2 * py + pc) * m_per, m_per), :]

        def copy(k, block, to, src=None):
            return pltpu.make_async_remote_copy(
                src_ref=rows(*block) if src is None else src, dst_ref=rows(*block), send_sem=send_sems.at[k],
                recv_sem=recv_sems.at[k], device_id=to, device_id_type=MESH)

        mine = pltpu.make_async_copy(x_ref, rows(*me), local_sem)
        mine.start()
        first = [copy(0, me, sibling, src=x_ref)]
        first += [copy(1 + j, me, (*chip, c), src=x_ref) for j, chip in enumerate(chips)]
        for cp in first:
            cp.start()
        passed = [copy(4 + j, (*chip, c), sibling) for j, chip in enumerate(chips)]
        for j, chip in enumerate(chips):
            copy(1 + j, (*chip, c), me).wait_recv()
            passed[j].start()
        copy(0, sibling, me).wait_recv()
        for j, chip in enumerate(chips):
            copy(4 + j, (*chip, 1 - c), me).wait_recv()
        for cp in first + passed:
            cp.wait_send()
        mine.wait()

    out = pl.pallas_call(
        body, name="all_gather8", out_shape=jax.ShapeDtypeStruct((N_DEV * m_per, n), v.dtype),
        in_specs=[pl.BlockSpec(memory_space=pltpu.VMEM)], out_specs=pl.BlockSpec(memory_space=pltpu.VMEM),
        scratch_shapes=[pltpu.SemaphoreType.DMA((7,)), pltpu.SemaphoreType.DMA((7,)), pltpu.SemaphoreType.DMA],
        compiler_params=pltpu.CompilerParams(vmem_limit_bytes=VMEM_LIMIT),
    )(v)
    return out.reshape(N_DEV, m_per, n)


def _half(c, rows):
    return pl.ds(c * (rows // 2), rows // 2)


HBM = pl.BlockSpec(memory_space=pltpu.HBM)
SEM = pl.BlockSpec(memory_space=pltpu.SEMAPHORE)
EFFECT = pltpu.SideEffectType.DATAFLOW_SIDE_EFFECTING


def _ici_copies(kind, src_refs, land_refs, send_sems, recv_sems):
    x, y, c = _position()
    chips = [(1 - x, y), (x, 1 - y), (1 - x, 1 - y)]
    sends, recvs = [], []
    for wi, (src, land) in enumerate(zip(src_refs, land_refs)):
        for k, (px, py) in enumerate(chips):
            if kind == "gather":
                rows = src.shape[0]
                s_win = src.at[_half(c, rows)]
                there, here = land.at[2 * x + y, _half(c, rows)], land.at[2 * px + py, _half(c, rows)]
            else:
                s_win = src.at[2 * px + py]
                there = here = land.at[k]
            for dst, out in ((there, sends), (here, recvs)):
                out.append(pltpu.make_async_remote_copy(
                    src_ref=s_win, dst_ref=dst, send_sem=send_sems.at[wi * 3 + k], recv_sem=recv_sems.at[wi * 3 + k],
                    device_id=(px, py, c), device_id_type=MESH))
    return sends, recvs


def _ici_start(kind, srcs, land_shapes, after, *, name):
    nw = len(srcs)

    def body(*refs):
        src_refs, land_refs = refs[:nw], refs[nw:2 * nw]
        send_sems, recv_sems = refs[2 * nw + 1:2 * nw + 3]
        token = refs[-1]
        sends, _ = _ici_copies(kind, src_refs, land_refs, send_sems, recv_sems)
        for cp in sends:
            cp.start()
        token[...] = jnp.zeros_like(token)

    lands = [pltpu.with_memory_space_constraint(lax.empty(shp, s.dtype), pltpu.HBM) for shp, s in zip(land_shapes, srcs)]
    outs = pl.pallas_call(
        body, name=name,
        out_shape=(pltpu.SemaphoreType.DMA((3 * nw,)), pltpu.SemaphoreType.DMA((3 * nw,)),
                   *[pltpu.HBM(s.shape, s.dtype) for s in srcs], *[pltpu.HBM(shp, s.dtype) for shp, s in zip(land_shapes, srcs)],
                   jax.ShapeDtypeStruct((8, LANE), F32)),
        in_specs=[HBM] * (2 * nw) + [ANY], out_specs=(SEM, SEM, *[HBM] * (2 * nw), pl.BlockSpec(memory_space=pltpu.VMEM)),
        input_output_aliases={i: 2 + i for i in range(2 * nw)},
        compiler_params=pltpu.CompilerParams(has_side_effects=EFFECT),
    )(*[pltpu.with_memory_space_constraint(s, pltpu.HBM) for s in srcs], *lands, after)
    return outs[0], outs[1], outs[2:2 + nw], outs[2 + nw:2 + 2 * nw], outs[-1]


def _ici_wait(kind, send_sems, recv_sems, srcs, lands, after, *, name):
    nw = len(srcs)

    def body(*refs):
        src_refs, land_refs = refs[:nw], refs[nw:2 * nw]
        s_sems, r_sems = refs[2 * nw:2 * nw + 2]
        sends, recvs = _ici_copies(kind, src_refs, land_refs, s_sems, r_sems)
        for cp in sends:
            cp.wait_send()
        for cp in recvs:
            cp.wait_recv()

    outs = pl.pallas_call(
        body, name=name,
        out_shape=(*[pltpu.HBM(s.shape, s.dtype) for s in srcs], *[pltpu.HBM(a.shape, a.dtype) for a in lands]),
        in_specs=[HBM] * (2 * nw) + [SEM, SEM, ANY], out_specs=tuple([HBM] * (2 * nw)),
        input_output_aliases={i: i for i in range(2 * nw)},
        compiler_params=pltpu.CompilerParams(has_side_effects=EFFECT),
    )(*srcs, *lands, send_sems, recv_sems, after)
    return outs[nw:]


def _ag_d2d(lands, *, name):
    nw = len(lands)

    def body(*refs):
        out_refs = refs[nw:2 * nw]
        send_sems, recv_sems = refs[2 * nw:]
        x, y, c = _position()
        chips = [(1 - x, y), (x, 1 - y), (1 - x, 1 - y)]
        copies = []
        for wi in range(nw):
            rows = out_refs[wi].shape[1]
            for k, (px, py) in enumerate(chips):
                win = out_refs[wi].at[2 * px + py, _half(c, rows)]
                copies.append(pltpu.make_async_remote_copy(
                    src_ref=win, dst_ref=win, send_sem=send_sems.at[wi * 3 + k], recv_sem=recv_sems.at[wi * 3 + k],
                    device_id=(x, y, 1 - c), device_id_type=MESH))
        for cp in copies:
            cp.start()
        for wi in range(nw):
            rows = out_refs[wi].shape[1]
            for k, (px, py) in enumerate(chips):
                win = out_refs[wi].at[2 * px + py, _half(1 - c, rows)]
                pltpu.make_async_remote_copy(
                    src_ref=win, dst_ref=win, send_sem=send_sems.at[wi * 3 + k], recv_sem=recv_sems.at[wi * 3 + k],
                    device_id=(x, y, 1 - c), device_id_type=MESH).wait_recv()
        for cp in copies:
            cp.wait_send()

    return pl.pallas_call(
        body, name=name, out_shape=[jax.ShapeDtypeStruct(a.shape, a.dtype) for a in lands],
        in_specs=[ANY] * nw, out_specs=[ANY] * nw, input_output_aliases={i: i for i in range(nw)},
        scratch_shapes=[pltpu.SemaphoreType.DMA((3 * nw,)), pltpu.SemaphoreType.DMA((3 * nw,))],
    )(*lands)


def _rs_swap(gs, *, name):
    nw = len(gs)

    def body(*refs):
        g_refs, out_refs = refs[:nw], refs[nw:2 * nw]
        send_sems, recv_sems = refs[2 * nw:]
        x, y, c = _position()
        copies = []
        for wi in range(nw):
            rows = g_refs[wi].shape[1]
            for j in range(N_CHIPS):
                copies.append(pltpu.make_async_remote_copy(
                    src_ref=g_refs[wi].at[j, _half(1 - c, rows)], dst_ref=out_refs[wi].at[j],
                    send_sem=send_sems.at[wi * N_CHIPS + j], recv_sem=recv_sems.at[wi * N_CHIPS + j],
                    device_id=(x, y, 1 - c), device_id_type=MESH))
        for cp in copies:
            cp.start()
        for cp in copies:
            cp.wait()

    return pl.pallas_call(
        body, name=name,
        out_shape=[jax.ShapeDtypeStruct((N_CHIPS, g.shape[1] // 2, g.shape[2]), g.dtype) for g in gs],
        in_specs=[ANY] * nw, out_specs=[ANY] * nw,
        scratch_shapes=[pltpu.SemaphoreType.DMA((N_CHIPS * nw,)), pltpu.SemaphoreType.DMA((N_CHIPS * nw,))],
    )(*gs)


def _swap_reduced(reds):
    nw = len(reds)

    def body(*refs):
        r_refs, out_refs = refs[:nw], refs[nw:2 * nw]
        send_sems, recv_sems = refs[2 * nw:]
        x, y, c = _position()
        copies = [pltpu.make_async_remote_copy(
            src_ref=r_refs[wi], dst_ref=out_refs[wi], send_sem=send_sems.at[wi], recv_sem=recv_sems.at[wi],
            device_id=(x, y, 1 - c), device_id_type=MESH) for wi in range(nw)]
        for cp in copies:
            cp.start()
        for cp in copies:
            cp.wait()

    return pl.pallas_call(
        body, name="rs_swap_reduced", out_shape=[jax.ShapeDtypeStruct(r.shape, r.dtype) for r in reds],
        in_specs=[ANY] * nw, out_specs=[ANY] * nw,
        scratch_shapes=[pltpu.SemaphoreType.DMA((nw,)), pltpu.SemaphoreType.DMA((nw,))],
    )(*reds)


def _row_tile(rows, cols):
    tr = rows
    while tr * cols * 4 > (2 << 20) and tr % 32 == 0:
        tr //= 2
    return tr


def _add_own_half(g, recv, c, *, name):
    nj, h, w = recv.shape
    tr = _row_tile(h, w)
    nb = h // tr

    def body(c_ref, g_ref, r_ref, o_ref, ob_ref):
        sm = g_ref[0] + r_ref[0]
        o_ref[0] = sm
        ob_ref[0] = sm.astype(BF16)

    spec = pl.BlockSpec((1, tr, w), lambda j, i, cc: (j, i, 0))
    gs = pltpu.PrefetchScalarGridSpec(
        num_scalar_prefetch=1, grid=(nj, nb),
        in_specs=[pl.BlockSpec((1, tr, w), lambda j, i, cc: (j, cc[0] * nb + i, 0)), spec],
        out_specs=[spec, spec])
    return pl.pallas_call(body, name=name, grid_spec=gs,
                          out_shape=[jax.ShapeDtypeStruct((nj, h, w), F32), jax.ShapeDtypeStruct((nj, h, w), BF16)],
                          compiler_params=_cparams(("parallel", "parallel")))(jnp.reshape(c, (1,)).astype(I32), g, recv)


def _add_own_chip(sp, recv, j, *, name):
    _, r, w = sp.shape
    tr = _row_tile(r, w)

    def body(j_ref, s_ref, r_ref, o_ref):
        o_ref[...] = ((s_ref[0] + r_ref[0].astype(F32)) + r_ref[1].astype(F32)) + r_ref[2].astype(F32)

    gs = pltpu.PrefetchScalarGridSpec(
        num_scalar_prefetch=1, grid=(r // tr,),
        in_specs=[pl.BlockSpec((1, tr, w), lambda i, jj: (jj[0], i, 0)), pl.BlockSpec((3, tr, w), lambda i, jj: (0, i, 0))],
        out_specs=pl.BlockSpec((tr, w), lambda i, jj: (i, 0)))
    return pl.pallas_call(body, name=name, grid_spec=gs, out_shape=jax.ShapeDtypeStruct((r, w), F32),
                          compiler_params=_cparams(("parallel",)))(jnp.reshape(j, (1,)).astype(I32), sp, recv)


def _sum8(v):
    _, m, n = v.shape

    def body(v_ref, o_ref):
        acc = v_ref[0]
        for k in range(1, N_DEV):
            acc = acc + v_ref[k]
        o_ref[...] = acc

    return pl.pallas_call(body, name="sum8", grid=(1,), in_specs=[pl.BlockSpec((N_DEV, m, n), lambda i: (0, 0, 0))],
                          out_specs=pl.BlockSpec((m, n), lambda i: (0, 0)), out_shape=jax.ShapeDtypeStruct((m, n), F32),
                          compiler_params=_cparams(("arbitrary",)))(v)


def _ada_mod(c_all, ada_w, ada_b_loc, *, tn=512):
    nl, _, ncol = ada_w.shape

    def body(c_ref, w_ref, b_ref, o_ref):
        cv = c_ref[...]
        ca = (cv * _sigmoid(cv)).astype(BF16)
        o_ref[0] = _dot(ca, w_ref[0].astype(BF16)) + b_ref[0]

    return pl.pallas_call(
        body, name="ada_mod", grid=(nl, ncol // tn),
        in_specs=[pl.BlockSpec((N_DEV, D), lambda l, j: (0, 0)), pl.BlockSpec((1, D, tn), lambda l, j: (l, 0, j)),
                  pl.BlockSpec((1, 1, tn), lambda l, j: (l, 0, j))],
        out_specs=pl.BlockSpec((1, N_DEV, tn), lambda l, j: (l, 0, j)),
        out_shape=jax.ShapeDtypeStruct((nl, N_DEV, ncol), F32),
        compiler_params=_cparams(("parallel", "parallel")))(c_all, ada_w, ada_b_loc)


def _ada_grad(c_pad, dmod_pad, *, tn=512):
    nl, nb, ncol = dmod_pad.shape

    def body(c_ref, d_ref, o_ref):
        cv = c_ref[...]
        ca = (cv * _sigmoid(cv)).astype(BF16)
        o_ref[0] = _dot(ca, d_ref[0].astype(BF16), TN)

    return pl.pallas_call(
        body, name="ada_grad", grid=(nl, ncol // tn),
        in_specs=[pl.BlockSpec((nb, D), lambda l, j: (0, 0)), pl.BlockSpec((1, nb, tn), lambda l, j: (l, 0, j))],
        out_specs=pl.BlockSpec((1, D, tn), lambda l, j: (l, 0, j)),
        out_shape=jax.ShapeDtypeStruct((nl, D, ncol), F32),
        compiler_params=_cparams(("parallel", "parallel")))(c_pad, dmod_pad)


def _adamw(w, g, m, v, *, name):
    shape = w.shape
    if w.ndim == 3:
        lead, rows, cols = shape
    else:
        lead, (rows, cols) = 1, shape
    w3, g3, m3, v3 = (a.reshape(lead, rows, cols) for a in (w, g, m, v))
    tr = rows
    if rows * cols * 4 > (2 << 20):
        tr = next(cand for cand in (256, 128, 64, 8) if rows % cand == 0)
    c1 = 1.0 - ADAM_B1 ** ADAM_STEP
    c2 = 1.0 - ADAM_B2 ** ADAM_STEP

    def body(w_ref, g_ref, m_ref, v_ref, d_ref, nm_ref, nv_ref):
        gv = g_ref[...]
        nm = ADAM_B1 * m_ref[...] + (1.0 - ADAM_B1) * gv
        nv = ADAM_B2 * v_ref[...] + (1.0 - ADAM_B2) * (gv * gv)
        nm_ref[...] = nm
        nv_ref[...] = nv
        d_ref[...] = -ADAM_LR * ((nm / c1) / (jnp.sqrt(nv / c2) + ADAM_EPS) + ADAM_WD * w_ref[...])

    spec = pl.BlockSpec((1, tr, cols), lambda l, i: (l, i, 0))
    outs = pl.pallas_call(
        body, name=name, grid=(lead, rows // tr), in_specs=[spec] * 4, out_specs=[spec] * 3,
        out_shape=[jax.ShapeDtypeStruct((lead, rows, cols), F32)] * 3,
        compiler_params=_cparams(("parallel", "parallel")))(w3, g3, m3, v3)
    return tuple(o.reshape(shape) for o in outs)


def _adamw_layer(w, g_own, g_sib, m, v, c, layer, prev, after, *, name):
    _, rows, cols = w.shape
    tr = _row_tile(rows // 2, cols)
    nbh = rows // 2 // tr
    c1 = 1.0 - ADAM_B1 ** ADAM_STEP
    c2 = 1.0 - ADAM_B2 ** ADAM_STEP
    n_prev = 0 if prev is None else 4

    def body(c_ref, w_ref, o_ref, s_ref, m_ref, v_ref, *rest):
        g_ref, d_ref, nm_ref, nv_ref = rest[n_prev + 1:]
        gv = jnp.where(pl.program_id(0) // nbh == c_ref[0], o_ref[...], s_ref[...])
        nm = ADAM_B1 * m_ref[0] + (1.0 - ADAM_B1) * gv
        nv = ADAM_B2 * v_ref[0] + (1.0 - ADAM_B2) * (gv * gv)
        g_ref[0] = gv
        nm_ref[0] = nm
        nv_ref[0] = nv
        d_ref[0] = -ADAM_LR * ((nm / c1) / (jnp.sqrt(nv / c2) + ADAM_EPS) + ADAM_WD * w_ref[0])

    def source(own):
        return pl.BlockSpec((tr, cols), lambda i, cc: (jnp.where((i // nbh == cc[0]) == own, i % nbh, 0), 0))

    spec = pl.BlockSpec((1, tr, cols), lambda i, cc: (layer, i, 0))
    gs = pltpu.PrefetchScalarGridSpec(
        num_scalar_prefetch=1, grid=(2 * nbh,),
        in_specs=[spec, source(True), source(False), spec, spec] + [ANY] * (n_prev + 1), out_specs=[spec] * 4)
    return pl.pallas_call(
        body, name=name, grid_spec=gs, out_shape=[jax.ShapeDtypeStruct(w.shape, F32)] * 4,
        input_output_aliases={6 + k: k for k in range(n_prev)},
        compiler_params=_cparams(("parallel",)))(
            jnp.reshape(c, (1,)).astype(I32), w, g_own, g_sib, m, v, *(prev or ()), after)


SMALL = (("mix_pre_g", (2, D)), ("mix_post_g", (2, D)), ("mlp_pre_g", (2, D)), ("mlp_post_g", (2, D)),
         ("gmlp_ln_g", (2, DM)), ("gmlp_ln_b", (2, DM)), ("gmlp_ws", (2, NG, CH, CH)), ("gmlp_bs", (2, NG, CH)),
         ("conv_b", (2, DM)), ("conv_ln_g", (2, DM)), ("conv_ln_b", (2, DM)), ("fox_bf", (2, NG)))


def _pack_rows(arrays):
    parts = []
    for a in arrays:
        last = a.shape[-1]
        r = a.astype(F32).reshape(-1, LANE) if last % LANE == 0 else jnp.pad(a.astype(F32).reshape(-1, last), ((0, 0), (0, LANE - last)))
        pad = (-r.shape[0]) % 8
        parts.append(jnp.pad(r, ((0, pad), (0, 0))) if pad else r)
    return jnp.concatenate(parts, axis=0)


def _unpack_rows(buf, shapes):
    out, off = [], 0
    for shp in shapes:
        size = 1
        for d in shp:
            size *= d
        last = shp[-1]
        rows = size // (LANE if last % LANE == 0 else last)
        seg = buf[off:off + rows]
        out.append(seg.reshape(shp) if last % LANE == 0 else seg[:, :last].reshape(shp))
        off += rows + (-rows) % 8
    return out


def _assemble_w_in(w_in_full):
    uv_glu_qkv = w_in_full[:, :3584]
    f = w_in_full[:, 3584:3592]
    gate = w_in_full[:, 3592:]
    fpad = jnp.zeros((D, D_INP - C_F - NG), w_in_full.dtype)
    return jnp.concatenate([gate, uv_glu_qkv, f, fpad], axis=1)


def _disassemble_w_in(g_p):
    return jnp.concatenate([g_p[:, C_UV:C_F], g_p[:, C_F:C_F + NG], g_p[:, :C_UV]], axis=1)


def kernel(x, c, ada_w, ada_b, mix_pre_g, mix_post_g, mlp_pre_g, mlp_post_g, w_in, gmlp_ln_g, gmlp_ln_b, gmlp_ws, gmlp_bs, w_a_out, conv_w, conv_b, conv_ln_g, conv_ln_b, w_b_out, fox_bf, w_c_out, w_out, mlp_w1, mlp_w2, loss_target, m_ada_w, m_ada_b, m_mix_pre_g, m_mix_post_g, m_mlp_pre_g, m_mlp_post_g, m_w_in, m_gmlp_ln_g, m_gmlp_ln_b, m_gmlp_ws, m_gmlp_bs, m_w_a_out, m_conv_w, m_conv_b, m_conv_ln_g, m_conv_ln_b, m_w_b_out, m_fox_bf, m_w_c_out, m_w_out, m_mlp_w1, m_mlp_w2, v_ada_w, v_ada_b, v_mix_pre_g, v_mix_post_g, v_mlp_pre_g, v_mlp_post_g, v_w_in, v_gmlp_ln_g, v_gmlp_ln_b, v_gmlp_ws, v_gmlp_bs, v_w_a_out, v_conv_w, v_conv_b, v_conv_ln_g, v_conv_ln_b, v_w_b_out, v_fox_bf, v_w_c_out, v_w_out, v_mlp_w1, v_mlp_w2):
    weights = dict(ada_w=ada_w, ada_b=ada_b, mix_pre_g=mix_pre_g, mix_post_g=mix_post_g, mlp_pre_g=mlp_pre_g,
                   mlp_post_g=mlp_post_g, w_in=w_in, gmlp_ln_g=gmlp_ln_g, gmlp_ln_b=gmlp_ln_b, gmlp_ws=gmlp_ws,
                   gmlp_bs=gmlp_bs, w_a_out=w_a_out, conv_w=conv_w, conv_b=conv_b, conv_ln_g=conv_ln_g,
                   conv_ln_b=conv_ln_b, w_b_out=w_b_out, fox_bf=fox_bf, w_c_out=w_c_out, w_out=w_out, mlp_w1=mlp_w1,
                   mlp_w2=mlp_w2)
    mom_m = dict(ada_w=m_ada_w, ada_b=m_ada_b, mix_pre_g=m_mix_pre_g, mix_post_g=m_mix_post_g, mlp_pre_g=m_mlp_pre_g,
                 mlp_post_g=m_mlp_post_g, w_in=m_w_in, gmlp_ln_g=m_gmlp_ln_g, gmlp_ln_b=m_gmlp_ln_b, gmlp_ws=m_gmlp_ws,
                 gmlp_bs=m_gmlp_bs, w_a_out=m_w_a_out, conv_w=m_conv_w, conv_b=m_conv_b, conv_ln_g=m_conv_ln_g,
                 conv_ln_b=m_conv_ln_b, w_b_out=m_w_b_out, fox_bf=m_fox_bf, w_c_out=m_w_c_out, w_out=m_w_out,
                 mlp_w1=m_mlp_w1, mlp_w2=m_mlp_w2)
    mom_v = dict(ada_w=v_ada_w, ada_b=v_ada_b, mix_pre_g=v_mix_pre_g, mix_post_g=v_mix_post_g, mlp_pre_g=v_mlp_pre_g,
                 mlp_post_g=v_mlp_post_g, w_in=v_w_in, gmlp_ln_g=v_gmlp_ln_g, gmlp_ln_b=v_gmlp_ln_b, gmlp_ws=v_gmlp_ws,
                 gmlp_bs=v_gmlp_bs, w_a_out=v_w_a_out, conv_w=v_conv_w, conv_b=v_conv_b, conv_ln_g=v_conv_ln_g,
                 conv_ln_b=v_conv_ln_b, w_b_out=v_w_b_out, fox_bf=v_fox_bf, w_c_out=v_w_c_out, w_out=v_w_out,
                 mlp_w1=v_mlp_w1, mlp_w2=v_mlp_w2)
    order = list(weights)
    px, py, pc = _position()
    chip = 2 * px + py
    dev = 2 * chip + pc
    depth = ada_w.shape[0]
    t = x.shape[1]
    xl = x.reshape(t, D)
    tgt = loss_target.reshape(t, D)

    big_names = [b[0] for b in BIG]
    shards = [[weights[n][l].astype(BF16) for n in big_names] for l in range(depth)]
    land_shapes = [(N_CHIPS,) + sh.shape for sh in shards[0]]

    small_in = _pack_rows([c, conv_w])
    gathered = _all_gather8(small_in)
    c_all = gathered[:, :D // LANE, :].reshape(N_DEV, D)
    cw_rows = depth * KW * LANE // LANE
    conv_w_full = jnp.concatenate(
        [gathered[2 * j, D // LANE:D // LANE + cw_rows, :].reshape(depth, KW, LANE) for j in range(N_CHIPS)], axis=2)

    ncol = ada_w.shape[2]
    ada_b_loc = lax.dynamic_slice_in_dim(ada_b, chip * ncol, ncol, axis=1).reshape(depth, 1, ncol)
    mod_sh = _ada_mod(c_all, ada_w, ada_b_loc)
    mod_g = _all_gather8(mod_sh.reshape(-1, LANE)).reshape(N_DEV, depth, N_DEV, ncol)
    mod_all = jnp.concatenate([mod_g[2 * j] for j in range(N_CHIPS)], axis=2)
    mod_mine = lax.dynamic_index_in_dim(mod_all, dev, axis=1, keepdims=False)

    mods = [mod_mine[l].reshape(NMOD, D) for l in range(depth)]

    def gathered_weights(l, idx, lands):
        lands = _ag_d2d(lands, name="ag_d2d")
        w = {}
        for i, g in zip(idx, lands):
            n, r, cdim, ax = BIG[i]
            g = lax.dynamic_update_slice(g, shards[l][i][None], (chip, 0, 0))
            if n == "w_in":
                w["w_in_p"] = _assemble_w_in(g.transpose(1, 0, 2).reshape(r, N_CHIPS * cdim))
            elif n == "mlp_w1":
                w[n] = g
            elif ax == 1:
                w[n] = g.transpose(1, 0, 2).reshape(r, N_CHIPS * cdim)
            else:
                w[n] = g.reshape(N_CHIPS * r, cdim)
        return w

    def local_weights(l):
        w = {}
        for n in ("mix_pre_g", "mix_post_g", "mlp_pre_g", "mlp_post_g", "gmlp_ln_g", "gmlp_ln_b", "conv_b", "conv_ln_g",
                  "conv_ln_b"):
            w[n] = weights[n][l:l + 1]
        tril = jnp.tril(jnp.ones((CH, CH), F32))
        wsm = gmlp_ws[l] * tril
        w["wsm"] = wsm.astype(BF16)
        w["wsmt"] = jnp.swapaxes(wsm, 1, 2).astype(BF16)
        w["bsx"] = jnp.repeat(gmlp_bs[l].T, HD, axis=1)
        w["conv_w"] = conv_w_full[l]
        w["bfp"] = jnp.pad(fox_bf[l], (0, LANE - NG)).reshape(1, LANE)
        return w

    def slabs(gfull, n, r, cdim, ax):
        if n == "mlp_w1":
            return gfull
        if ax == 1:
            return gfull.reshape(gfull.shape[0], N_CHIPS, cdim).transpose(1, 0, 2)
        return gfull.reshape(N_CHIPS, r, cdim)

    def chip_sums(g):
        g["w_in"] = _disassemble_w_in(g.pop("w_in_p"))
        gs = [slabs(g[n], n, r, cdim, ax) for n, r, cdim, ax in BIG]
        from_sibling = _rs_swap(gs, name="rs_swap")
        return [_add_own_half(a, rv, pc, name="rs_add_half_" + n) for a, rv, n in zip(gs, from_sibling, big_names)]

    def reduce_rest(sums, from_chips):
        red = [_add_own_chip(sf, rv, chip, name="rs_add_chip_" + n) for (sf, _), rv, n in zip(sums, from_chips, big_names)]
        return red, _swap_reduced(red)

    assert depth == 2
    ga, gb = [0], list(range(1, len(BIG)))

    def pick(seq, idx):
        return [seq[i] for i in idx]

    ag0a = _ici_start("gather", pick(shards[0], ga), pick(land_shapes, ga), mod_mine, name="ag0a_start")
    lands0a = _ici_wait("gather", ag0a[0], ag0a[1], ag0a[2], ag0a[3], ag0a[4], name="ag0a_wait")
    ag0b = _ici_start("gather", pick(shards[0], gb), pick(land_shapes, gb), lands0a[0], name="ag0b_start")
    ag1 = _ici_start("gather", shards[1], land_shapes, ag0b[4], name="ag1_start")
    mod0 = mods[0] + ag1[4][0, 0]
    layers, saved = [None] * depth, [None] * depth

    def late0(att):
        return gathered_weights(0, gb, _ici_wait("gather", ag0b[0], ag0b[1], ag0b[2], ag0b[3], att, name="ag0b_wait"))

    xs, saved[0], layers[0] = _layer_fwd(xl, mod0, {**local_weights(0), **gathered_weights(0, ga, lands0a)}, late0)
    lands1 = _ici_wait("gather", ag1[0], ag1[1], ag1[2], ag1[3], xs, name="ag1_wait")
    xs, saved[1], layers[1] = _layer_fwd(xs, mods[1], {**local_weights(1), **gathered_weights(1, ga + gb, lands1)})
    loss_local, dx = _loss_and_grad(xs, tgt)
    loss = lax.psum(loss_local, ("x", "y", "c"))
    grads, dmods = [None] * depth, [None] * depth
    dx, grads[1], dmods[1] = _layer_bwd(dx, mods[1], layers[1], saved[1])
    sums1 = chip_sums(grads[1])
    sbf1 = [sb for _, sb in sums1]
    slot_shapes = [(3,) + sb.shape[1:] for sb in sbf1]
    rs1 = _ici_start("scatter", sbf1, slot_shapes, dx, name="rs1_start")
    dx, grads[0], dmods[0] = _layer_bwd(dx, mod0 + rs1[4][0, 0], layers[0], saved[0])
    grad_x = dx.reshape(x.shape)
    red1 = reduce_rest(sums1, _ici_wait("scatter", rs1[0], rs1[1], rs1[2], rs1[3], dx, name="rs1_wait"))
    sums0 = chip_sums(grads[0])
    g_out = {}

    small_names = [n for n, _ in SMALL]
    small_list = [jnp.stack(dmods)] + [jnp.stack([grads[l][n] for l in range(depth)]) for n in small_names]
    small_list.append(jnp.stack([grads[l]["conv_w"] for l in range(depth)]))
    small_shapes = [(depth, NMOD * D)] + [shp for _, shp in SMALL] + [(depth, KW, DM)]
    small_all = _all_gather8(_pack_rows(small_list))
    rs0 = _ici_start("scatter", [sb for _, sb in sums0], slot_shapes, small_all, name="rs0_start")
    tok0 = rs0[4][0, 0]
    c_all = c_all + tok0
    small_sum = _unpack_rows(_sum8(small_all) + tok0, small_shapes)
    g_out["ada_b"] = small_sum[0]
    for n, gs in zip(small_names, small_sum[1:-1]):
        g_out[n] = gs
    g_out["conv_w"] = lax.dynamic_slice_in_dim(small_sum[-1], chip * LANE, LANE, axis=2)
    dmod_all = small_all[:, :depth * NMOD * D // LANE, :].reshape(N_DEV, depth, NMOD * D)
    dmod_loc = lax.dynamic_slice_in_dim(dmod_all, chip * ncol, ncol, axis=2).transpose(1, 0, 2)
    g_out["ada_w"] = _ada_grad(jnp.pad(c_all, ((0, 8), (0, 0))), jnp.pad(dmod_loc, ((0, 0), (0, 8), (0, 0))))

    delta, new_m, new_v = {}, {}, {}
    delta["ada_w"], new_m["ada_w"], new_v["ada_w"] = _adamw(ada_w, g_out["ada_w"], m_ada_w, v_ada_w, name="adamw_ada_w")
    ws_rows = (depth * NG * CH, CH)
    ws_out = _adamw(*[d["gmlp_ws"].reshape(ws_rows) for d in (weights, g_out, mom_m, mom_v)], name="adamw_gmlp_ws")
    delta["gmlp_ws"], new_m["gmlp_ws"], new_v["gmlp_ws"] = (a.reshape(gmlp_ws.shape) for a in ws_out)
    small_params = ["ada_b"] + [n for n in small_names if n != "gmlp_ws"] + ["conv_w"]
    packs = [_pack_rows([d[n] for n in small_params]) for d in (weights, g_out, mom_m, mom_v)]
    outs = _adamw(*packs, name="adamw_small")
    shapes = [weights[n].shape for n in small_params]
    for dst, buf in zip((delta, new_m, new_v), outs):
        for n, a in zip(small_params, _unpack_rows(buf, shapes)):
            dst[n] = a
    half = {n: _adamw_layer(weights[n], red1[0][wi], red1[1][wi], mom_m[n], mom_v[n], pc, 1, None, rs0[4],
                            name="adamw1_" + n) for wi, n in enumerate(big_names)}
    done = jnp.stack([delta["ada_w"][0, 0, 0], outs[0][0, 0]] + [half[n][1][1, 0, 0] for n in big_names])
    red0 = reduce_rest(sums0, _ici_wait("scatter", rs0[0], rs0[1], rs0[2], rs0[3], done, name="rs0_wait"))
    for wi, n in enumerate(big_names):
        g_out[n], delta[n], new_m[n], new_v[n] = _adamw_layer(
            weights[n], red0[0][wi], red0[1][wi], mom_m[n], mom_v[n], pc, 0, half[n], rs0[4], name="adamw0_" + n)

    return (loss, grad_x, *[g_out[n] for n in order], *[delta[n] for n in order], *[new_m[n] for n in order],
            *[new_v[n] for n in order])
```

```python
import functools

import jax
import jax.numpy as jnp
from jax import lax
from jax.experimental import pallas as pl
from jax.experimental.pallas import tpu as pltpu

F32 = jnp.float32
BF16 = jnp.bfloat16
I32 = jnp.int32
MESH = pl.DeviceIdType.MESH
ANY = pl.BlockSpec(memory_space=pl.ANY)

D = 1024
DM = 512
NG = 8
CH = 128
KW = 31
HALO = 32
DFF = 4096
NMOD = 6
EPS = 1e-6
LANE = 128
N_CHIPS = 4
N_DEV = 8
C_GATE, C_UV, C_GLU, C_Q, C_K, C_V, C_F, D_INP = 0, 3072, 4096, 5120, 5632, 6144, 6656, 7168
D_IN = 6664
VMEM_LIMIT = 56 * 1024 * 1024
TK_DEEP = 4096

ADAM_LR, ADAM_B1, ADAM_B2, ADAM_EPS, ADAM_WD, ADAM_STEP = 0.001, 0.9, 0.999, 1e-08, 0.01, 10

BIG = (("w_in", 1024, 1666, 1), ("w_a_out", 512, 256, 1), ("w_b_out", 512, 256, 1), ("w_c_out", 512, 256, 1),
       ("w_out", 256, 1024, 0), ("mlp_w1", 1024, 1024, 1), ("mlp_w2", 1024, 1024, 0))


def _cparams(sem):
    return pltpu.CompilerParams(dimension_semantics=sem, vmem_limit_bytes=VMEM_LIMIT)


def _sigmoid(x):
    return jax.nn.sigmoid(x)


_GELU_K = 0.7978845608028654
_GELU_A = 0.044715


def _gelu(x):
    t = jnp.tanh(_GELU_K * (x + _GELU_A * x * x * x))
    return 0.5 * x * (1.0 + t)


def _gelu_grad(x):
    t = jnp.tanh(_GELU_K * (x + _GELU_A * x * x * x))
    return 0.5 * (1.0 + t) + 0.5 * x * (1.0 - t * t) * _GELU_K * (1.0 + 3.0 * _GELU_A * x * x)


def _mean(x):
    return jnp.mean(x, axis=-1, keepdims=True)


def _colsum(x):
    return jnp.sum(x, axis=0, keepdims=True)


def _dot(a, b, dims=((1,), (0,))):
    return lax.dot_general(a, b, (dims, ((), ())), preferred_element_type=F32)


NN = ((1,), (0,))
NT = ((1,), (1,))
TN = ((0,), (0,))


def _matmul(a, b, *, name, ta=False, tb=False, out_dtype=F32, tm=1024, tn=1024, tk=1024, epilogue=None, extra=(),
            extra_out=(), b_slabs=False, out_slabs=0):
    m, k = (a.shape[1], a.shape[0]) if ta else a.shape
    if b_slabs:
        ns, brows, bw = b.shape
        n = brows if tb else ns * bw
        assert (ns * bw if tb else brows) == k, (name, b.shape, k)
        tn, tk = (tn, bw) if tb else (bw, tk)
    else:
        n = b.shape[0] if tb else b.shape[1]
    tm, tn, tk = min(tm, m), min(tn, n), min(tk, k)
    assert m % tm == 0 and n % tn == 0 and k % tk == 0, (name, m, n, k, tm, tn, tk)
    assert not out_slabs or (n // out_slabs == tn and epilogue is None), name
    nk = k // tk
    dims = ((0 if ta else 1,), (1 if tb else 0,))
    n_extra = len(extra)
    out_dtypes = (out_dtype,) + tuple(extra_out)

    def body(a_ref, b_ref, *rest):
        extra_refs = rest[:n_extra]
        out_refs = rest[n_extra:n_extra + len(out_dtypes)]
        kk = pl.program_id(2)
        part = _dot(a_ref[...].astype(BF16), b_ref[...].astype(BF16), dims)

        def finish(acc):
            outs = (acc,) if epilogue is None else epilogue(acc, *[r[...] for r in extra_refs])
            for o_ref, o in zip(out_refs, outs):
                o_ref[...] = o.astype(o_ref.dtype)

        if nk == 1:
            finish(part)
        else:
            acc_ref = rest[-1]

            @pl.when(kk == 0)
            def _():
                acc_ref[...] = part

            @pl.when(jnp.logical_and(kk > 0, kk < nk - 1))
            def _():
                acc_ref[...] += part

            @pl.when(kk == nk - 1)
            def _():
                finish(acc_ref[...] + part)

    a_spec = pl.BlockSpec((tk, tm), lambda i, j, kk: (kk, i)) if ta else pl.BlockSpec((tm, tk), lambda i, j, kk: (i, kk))
    if b_slabs and tb:
        b_spec = pl.BlockSpec((None, tn, tk), lambda i, j, kk: (kk, j, 0))
    elif b_slabs:
        b_spec = pl.BlockSpec((None, tk, tn), lambda i, j, kk: (j, kk, 0))
    else:
        b_spec = pl.BlockSpec((tn, tk), lambda i, j, kk: (j, kk)) if tb else pl.BlockSpec((tk, tn), lambda i, j, kk: (kk, j))
    if out_slabs:
        o_spec = pl.BlockSpec((None, tm, tn), lambda i, j, kk: (j, i, 0))
        o_shape = (out_slabs, m, tn)
    else:
        o_spec = pl.BlockSpec((tm, tn), lambda i, j, kk: (i, j))
        o_shape = (m, n)
    outs = pl.pallas_call(
        body, name=name, grid=(m // tm, n // tn, nk),
        in_specs=[a_spec, b_spec] + [o_spec] * n_extra,
        out_specs=[o_spec] * len(out_dtypes),
        out_shape=[jax.ShapeDtypeStruct(o_shape, dt) for dt in out_dtypes],
        scratch_shapes=[pltpu.VMEM((tm, tn), F32)] if nk > 1 else [],
        compiler_params=_cparams(("parallel", "parallel", "arbitrary")),
    )(a, b, *extra)
    return outs[0] if len(outs) == 1 else outs


def _rows(tm, n, col=0):
    return pl.BlockSpec((tm, n), lambda i: (i, col))


def _vec(n):
    return pl.BlockSpec((1, n), lambda i: (0, 0))


def _norm_mod(x, g, sc, sh, *, name, tm=256):
    t = x.shape[0]
    tm = min(tm, t)

    def body(x_ref, g_ref, sc_ref, sh_ref, h_ref):
        xv = x_ref[...]
        inv = lax.rsqrt(_mean(xv * xv) + EPS)
        h_ref[...] = ((xv * inv * g_ref[...]) * (1.0 + sc_ref[...]) + sh_ref[...]).astype(BF16)

    return pl.pallas_call(
        body, name=name, grid=(t // tm,), in_specs=[_rows(tm, D), _vec(D), _vec(D), _vec(D)],
        out_specs=_rows(tm, D), out_shape=jax.ShapeDtypeStruct((t, D), BF16),
        compiler_params=_cparams(("parallel",)))(x, g, sc, sh)


def _resid(x, y, gt, gp, *, name, tm=256):
    t = x.shape[0]
    tm = min(tm, t)

    def body(x_ref, y_ref, gt_ref, gp_ref, o_ref):
        yv = y_ref[...]
        inv = lax.rsqrt(_mean(yv * yv) + EPS)
        o_ref[...] = x_ref[...] + gt_ref[...] * (yv * inv * gp_ref[...])

    return pl.pallas_call(
        body, name=name, grid=(t // tm,), in_specs=[_rows(tm, D), _rows(tm, D), _vec(D), _vec(D)],
        out_specs=_rows(tm, D), out_shape=jax.ShapeDtypeStruct((t, D), F32),
        compiler_params=_cparams(("parallel",)))(x, y, gt, gp)


def _resid_bwd(dx, y, gt, gp, *, name, tm=256):
    t = dx.shape[0]
    tm = min(tm, t)

    def body(dx_ref, y_ref, gt_ref, gp_ref, dy_ref, dgt_ref, dgp_ref):
        @pl.when(pl.program_id(0) == 0)
        def _():
            dgt_ref[...] = jnp.zeros_like(dgt_ref)
            dgp_ref[...] = jnp.zeros_like(dgp_ref)

        dxv, yv, gp_v = dx_ref[...], y_ref[...], gp_ref[...]
        inv = lax.rsqrt(_mean(yv * yv) + EPS)
        yh = yv * inv
        dgt_ref[...] += _colsum(dxv * (yh * gp_v))
        dr = dxv * gt_ref[...]
        dgp_ref[...] += _colsum(dr * yh)
        dyn = dr * gp_v
        dy_ref[...] = (inv * (dyn - yh * _mean(dyn * yh))).astype(BF16)

    return pl.pallas_call(
        body, name=name, grid=(t // tm,), in_specs=[_rows(tm, D), _rows(tm, D), _vec(D), _vec(D)],
        out_specs=[_rows(tm, D), _vec(D), _vec(D)],
        out_shape=[jax.ShapeDtypeStruct((t, D), BF16), jax.ShapeDtypeStruct((1, D), F32),
                   jax.ShapeDtypeStruct((1, D), F32)],
        compiler_params=_cparams(("arbitrary",)))(dx, y, gt, gp)


def _norm_bwd(dh, dx_res, x, g, sc, *, name, tm=256):
    t = dh.shape[0]
    tm = min(tm, t)

    def body(dh_ref, dxr_ref, x_ref, g_ref, sc_ref, dx_ref, dg_ref, dsc_ref, dsh_ref):
        @pl.when(pl.program_id(0) == 0)
        def _():
            dg_ref[...] = jnp.zeros_like(dg_ref)
            dsc_ref[...] = jnp.zeros_like(dsc_ref)
            dsh_ref[...] = jnp.zeros_like(dsh_ref)

        dhv, xv, gv = dh_ref[...], x_ref[...], g_ref[...]
        inv = lax.rsqrt(_mean(xv * xv) + EPS)
        xh = xv * inv
        dsh_ref[...] += _colsum(dhv)
        dsc_ref[...] += _colsum(dhv * (xh * gv))
        dn = dhv * (1.0 + sc_ref[...])
        dg_ref[...] += _colsum(dn * xh)
        dxh = dn * gv
        dx_ref[...] = inv * (dxh - xh * _mean(dxh * xh)) + dxr_ref[...]

    vec_out = jax.ShapeDtypeStruct((1, D), F32)
    return pl.pallas_call(
        body, name=name, grid=(t // tm,), in_specs=[_rows(tm, D), _rows(tm, D), _rows(tm, D), _vec(D), _vec(D)],
        out_specs=[_rows(tm, D), _vec(D), _vec(D), _vec(D)],
        out_shape=[jax.ShapeDtypeStruct((t, D), F32), vec_out, vec_out, vec_out],
        compiler_params=_cparams(("arbitrary",)))(dh, dx_res, x, g, sc)


def _loss_and_grad(x, target, *, tm=256):
    t = x.shape[0]
    tm = min(tm, t)

    def body(x_ref, t_ref, loss_ref, dx_ref):
        @pl.when(pl.program_id(0) == 0)
        def _():
            loss_ref[...] = jnp.zeros_like(loss_ref)

        e = x_ref[...] - t_ref[...]
        dx_ref[...] = e * (1.0 / D)
        s = jnp.sum(jnp.sum(e * e, axis=1, keepdims=True), axis=0, keepdims=True) * (0.5 / D)
        loss_ref[...] += jnp.broadcast_to(s, loss_ref.shape)

    loss, dx = pl.pallas_call(
        body, name="loss", grid=(t // tm,), in_specs=[_rows(tm, D), _rows(tm, D)],
        out_specs=[pl.BlockSpec((8, LANE), lambda i: (0, 0)), _rows(tm, D)],
        out_shape=[jax.ShapeDtypeStruct((8, LANE), F32), jax.ShapeDtypeStruct((t, D), F32)],
        compiler_params=_cparams(("arbitrary",)))(x, target)
    return loss[0, 0], dx


def _gmlp_core(uv, lng, lnb, ws_ref, bsx):
    tm = uv.shape[0]
    gu = _gelu(uv[:, :DM])
    gv = _gelu(uv[:, DM:])
    mu = _mean(gv)
    vc = gv - mu
    rstd = lax.rsqrt(_mean(vc * vc) + EPS)
    vh = vc * rstd
    vln = vh * lng + lnb
    lane = lax.broadcasted_iota(I32, (CH, LANE), 1)
    sv_rows = []
    for nchunk in range(tm // CH):
        vb = vln[nchunk * CH:(nchunk + 1) * CH].astype(BF16)
        cols = []
        for cb in range(DM // LANE):
            vcb = vb[:, cb * LANE:(cb + 1) * LANE]
            lo = _dot(ws_ref[2 * cb], vcb)
            hi = _dot(ws_ref[2 * cb + 1], vcb)
            cols.append(jnp.where(lane < 64, lo, hi))
        sv_rows.append(jnp.concatenate(cols, axis=1) + bsx)
    sv = jnp.concatenate(sv_rows, axis=0) if len(sv_rows) > 1 else sv_rows[0]
    return gu, vh, rstd, vln, sv


def _gmlp_fwd(proj, lng, lnb, wsm, bsx, *, tm=256):
    t = proj.shape[0]
    tm = min(tm, t)

    def body(uv_ref, lng_ref, lnb_ref, ws_ref, bs_ref, ga_ref):
        gu, _, _, _, sv = _gmlp_core(uv_ref[...], lng_ref[...], lnb_ref[...], ws_ref, bs_ref[...])
        ga_ref[...] = (gu * sv).astype(BF16)

    return pl.pallas_call(
        body, name="gmlp_fwd", grid=(t // tm,),
        in_specs=[_rows(tm, 2 * DM, C_UV // (2 * DM)), _vec(DM), _vec(DM),
                  pl.BlockSpec((NG, CH, CH), lambda i: (0, 0, 0)), pl.BlockSpec((CH, DM), lambda i: (0, 0))],
        out_specs=_rows(tm, DM), out_shape=jax.ShapeDtypeStruct((t, DM), BF16),
        compiler_params=_cparams(("parallel",)))(proj, lng, lnb, wsm, bsx)


def _gmlp_bwd(dga, proj, lng, lnb, wsm, wsmt, bsx, *, tm=256):
    t = proj.shape[0]
    tm = min(tm, t)

    def body(dga_ref, uv_ref, lng_ref, lnb_ref, ws_ref, wst_ref, bs_ref, duv_ref, dws_ref, dbs_ref, dlng_ref, dlnb_ref,
             dbsx_ref):
        i = pl.program_id(0)

        @pl.when(i == 0)
        def _():
            dws_ref[...] = jnp.zeros_like(dws_ref)
            dbsx_ref[...] = jnp.zeros_like(dbsx_ref)
            dlng_ref[...] = jnp.zeros_like(dlng_ref)
            dlnb_ref[...] = jnp.zeros_like(dlnb_ref)

        uv = uv_ref[...]
        lng_v = lng_ref[...]
        gu, vh, rstd, vln, sv = _gmlp_core(uv, lng_v, lnb_ref[...], ws_ref, bs_ref[...])
        dga_v = dga_ref[...]
        dgu = dga_v * sv
        dsv = dga_v * gu
        lane = lax.broadcasted_iota(I32, (CH, LANE), 1)
        tril = lax.broadcasted_iota(I32, (CH, CH), 0) >= lax.broadcasted_iota(I32, (CH, CH), 1)
        dvln_rows = []
        for nchunk in range(tm // CH):
            rows = slice(nchunk * CH, (nchunk + 1) * CH)
            dbsx_ref[...] += dsv[rows]
            vb = vln[rows].astype(BF16)
            cols = []
            for cb in range(DM // LANE):
                cs = slice(cb * LANE, (cb + 1) * LANE)
                dsvb = dsv[rows, cs]
                vcb = vb[:, cs]
                dlo = jnp.where(lane < 64, dsvb, 0.0).astype(BF16)
                dhi = jnp.where(lane < 64, 0.0, dsvb).astype(BF16)
                dws_ref[2 * cb] += jnp.where(tril, _dot(dlo, vcb, NT), 0.0)
                dws_ref[2 * cb + 1] += jnp.where(tril, _dot(dhi, vcb, NT), 0.0)
                dsb = dsvb.astype(BF16)
                cols.append(jnp.where(lane < 64, _dot(wst_ref[2 * cb], dsb), _dot(wst_ref[2 * cb + 1], dsb)))
            dvln_rows.append(jnp.concatenate(cols, axis=1))
        dvln = jnp.concatenate(dvln_rows, axis=0) if len(dvln_rows) > 1 else dvln_rows[0]
        dlnb_ref[...] += _colsum(dvln)
        dlng_ref[...] += _colsum(dvln * vh)
        dvh = dvln * lng_v
        dgv = rstd * (dvh - _mean(dvh) - vh * _mean(dvh * vh))
        duv_ref[:, :DM] = (dgu * _gelu_grad(uv[:, :DM])).astype(BF16)
        duv_ref[:, DM:] = (dgv * _gelu_grad(uv[:, DM:])).astype(BF16)

        @pl.when(i == pl.num_programs(0) - 1)
        def _():
            ind = (lax.broadcasted_iota(I32, (DM, LANE), 0) // 64 == lax.broadcasted_iota(I32, (DM, LANE), 1)).astype(F32)
            dbs_ref[...] = jnp.dot(dbsx_ref[...], ind, preferred_element_type=F32, precision=lax.Precision.HIGHEST)

    vec_out = jax.ShapeDtypeStruct((1, DM), F32)
    outs = pl.pallas_call(
        body, name="gmlp_bwd", grid=(t // tm,),
        in_specs=[_rows(tm, DM), _rows(tm, 2 * DM, C_UV // (2 * DM)), _vec(DM), _vec(DM),
                  pl.BlockSpec((NG, CH, CH), lambda i: (0, 0, 0)), pl.BlockSpec((NG, CH, CH), lambda i: (0, 0, 0)),
                  pl.BlockSpec((CH, DM), lambda i: (0, 0))],
        out_specs=[_rows(tm, 2 * DM), pl.BlockSpec((NG, CH, CH), lambda i: (0, 0, 0)),
                   pl.BlockSpec((CH, LANE), lambda i: (0, 0)), _vec(DM), _vec(DM)],
        out_shape=[jax.ShapeDtypeStruct((t, 2 * DM), BF16), jax.ShapeDtypeStruct((NG, CH, CH), F32),
                   jax.ShapeDtypeStruct((CH, LANE), F32), vec_out, vec_out],
        scratch_shapes=[pltpu.VMEM((CH, DM), F32)],
        compiler_params=_cparams(("arbitrary",)))(dga, proj, lng, lnb, wsm, wsmt, bsx)
    return outs


def _glu_into(zs_ref, glu_ref, halo_ref, first):
    hal = halo_ref[...]
    z0h = hal[:, :DM] * _sigmoid(hal[:, DM:])
    zs_ref[0:HALO, :] = jnp.where(first, 0.0, z0h)
    g = glu_ref[...]
    zs_ref[HALO:, :] = g[:, :DM] * _sigmoid(g[:, DM:])


SUB = 8


def _shifted_copies(dst_ref, src_ref):
    n = src_ref.shape[0]
    for s in range(SUB):
        dst_ref[s, 0:n - s, :] = src_ref[s:n, :]


def _window(shifted_ref, off, rows, cs):
    s = off % SUB
    return shifted_ref[s, off - s:off - s + rows, cs]


def _conv_fwd(proj, cw, cb, lng, lnb, *, tm=256, rb=64):
    t = proj.shape[0]
    tm = min(tm, t)
    hb = tm // HALO
    gcol = C_GLU // (2 * DM)

    def body(glu_ref, halo_ref, cw_ref, cb_ref, lng_ref, lnb_ref, zc_ref, zb_ref, zs_ref, zsh_ref):
        i = pl.program_id(0)
        _glu_into(zs_ref, glu_ref, halo_ref, i == 0)
        _shifted_copies(zsh_ref, zs_ref)
        for cbk in range(DM // LANE):
            cs = slice(cbk * LANE, (cbk + 1) * LANE)
            for r in range(tm // rb):
                acc = jnp.broadcast_to(cb_ref[:, cs], (rb, LANE))
                for k in range(KW):
                    acc = acc + cw_ref[k:k + 1, cs] * _window(zsh_ref, r * rb + HALO - (KW - 1) + k, rb, cs)
                zc_ref[r * rb:(r + 1) * rb, cs] = acc
        zc = zc_ref[...]
        mu = _mean(zc)
        zcc = zc - mu
        zh = zcc * lax.rsqrt(_mean(zcc * zcc) + EPS)
        a = zh * lng_ref[...] + lnb_ref[...]
        zb_ref[...] = (a * _sigmoid(a)).astype(BF16)

    return pl.pallas_call(
        body, name="conv_fwd", grid=(t // tm,),
        in_specs=[_rows(tm, 2 * DM, gcol),
                  pl.BlockSpec((HALO, 2 * DM), lambda i: (jnp.maximum(i * hb - 1, 0), gcol)),
                  pl.BlockSpec((KW, DM), lambda i: (0, 0)), _vec(DM), _vec(DM), _vec(DM)],
        out_specs=[_rows(tm, DM), _rows(tm, DM)],
        out_shape=[jax.ShapeDtypeStruct((t, DM), F32), jax.ShapeDtypeStruct((t, DM), BF16)],
        scratch_shapes=[pltpu.VMEM((HALO + tm, DM), F32), pltpu.VMEM((SUB, HALO + tm, DM), F32)],
        compiler_params=_cparams(("parallel",)))(proj, proj, cw, cb, lng, lnb)


def _conv_bwd_ln(dzb, zc, lng, lnb, *, tm=256):
    t = zc.shape[0]
    tm = min(tm, t)

    def body(dzb_ref, zc_ref, lng_ref, lnb_ref, dzc_ref, dlng_ref, dlnb_ref):
        @pl.when(pl.program_id(0) == 0)
        def _():
            dlng_ref[...] = jnp.zeros_like(dlng_ref)
            dlnb_ref[...] = jnp.zeros_like(dlnb_ref)

        zc = zc_ref[...]
        lng_v = lng_ref[...]
        mu = _mean(zc)
        zcc = zc - mu
        rstd = lax.rsqrt(_mean(zcc * zcc) + EPS)
        zh = zcc * rstd
        a = zh * lng_v + lnb_ref[...]
        s = _sigmoid(a)
        da = dzb_ref[...] * (s * (1.0 + a * (1.0 - s)))
        dlnb_ref[...] += _colsum(da)
        dlng_ref[...] += _colsum(da * zh)
        dzh = da * lng_v
        dzc_ref[...] = rstd * (dzh - _mean(dzh) - zh * _mean(dzh * zh))

    vec_out = jax.ShapeDtypeStruct((1, DM), F32)
    return pl.pallas_call(
        body, name="conv_bwd_ln", grid=(t // tm,), in_specs=[_rows(tm, DM), _rows(tm, DM), _vec(DM), _vec(DM)],
        out_specs=[_rows(tm, DM), _vec(DM), _vec(DM)],
        out_shape=[jax.ShapeDtypeStruct((t, DM), F32), vec_out, vec_out],
        compiler_params=_cparams(("arbitrary",)))(dzb, zc, lng, lnb)


def _conv_bwd(dzc, proj, cw, *, tm=256, rb=64):
    t = proj.shape[0]
    tm = min(tm, t)
    hb = tm // HALO
    nblk = t // tm
    gcol = C_GLU // (2 * DM)

    def body(dzc_ref, dnext_ref, glu_ref, halo_ref, cw_ref, dglu_ref, dcw_ref, dcb_ref, zs_ref, ds_ref, zsh_ref, dsh_ref):
        i = pl.program_id(0)

        @pl.when(i == 0)
        def _():
            dcw_ref[...] = jnp.zeros_like(dcw_ref)
            dcb_ref[...] = jnp.zeros_like(dcb_ref)

        _glu_into(zs_ref, glu_ref, halo_ref, i == 0)
        _shifted_copies(zsh_ref, zs_ref)
        dzc = dzc_ref[...]
        ds_ref[0:tm, :] = dzc
        ds_ref[tm:, :] = jnp.where(i == nblk - 1, 0.0, dnext_ref[...])
        _shifted_copies(dsh_ref, ds_ref)
        dcb_ref[...] += _colsum(dzc)
        for k in range(KW):
            dcw_ref[k:k + 1, :] += _colsum(dzc * _window(zsh_ref, HALO - (KW - 1) + k, tm, slice(None)))
        g = glu_ref[...]
        val, sg = g[:, :DM], _sigmoid(g[:, DM:])
        for cbk in range(DM // LANE):
            cs = slice(cbk * LANE, (cbk + 1) * LANE)
            for r in range(tm // rb):
                acc = jnp.zeros((rb, LANE), F32)
                for k in range(KW):
                    acc = acc + cw_ref[k:k + 1, cs] * _window(dsh_ref, r * rb + (KW - 1) - k, rb, cs)
                rs = slice(r * rb, (r + 1) * rb)
                dglu_ref[rs, cs] = (acc * sg[rs, cs]).astype(BF16)
                v, s = val[rs, cs], sg[rs, cs]
                dglu_ref[rs, DM + cbk * LANE:DM + (cbk + 1) * LANE] = (acc * v * s * (1.0 - s)).astype(BF16)

    return pl.pallas_call(
        body, name="conv_bwd", grid=(nblk,),
        in_specs=[_rows(tm, DM),
                  pl.BlockSpec((HALO, DM), lambda i: (jnp.minimum((i + 1) * hb, nblk * hb - 1), 0)),
                  _rows(tm, 2 * DM, gcol),
                  pl.BlockSpec((HALO, 2 * DM), lambda i: (jnp.maximum(i * hb - 1, 0), gcol)),
                  pl.BlockSpec((KW, DM), lambda i: (0, 0))],
        out_specs=[_rows(tm, 2 * DM), pl.BlockSpec((HALO, DM), lambda i: (0, 0)), _vec(DM)],
        out_shape=[jax.ShapeDtypeStruct((t, 2 * DM), BF16), jax.ShapeDtypeStruct((HALO, DM), F32),
                   jax.ShapeDtypeStruct((1, DM), F32)],
        scratch_shapes=[pltpu.VMEM((HALO + tm, DM), F32), pltpu.VMEM((tm + HALO, DM), F32),
                        pltpu.VMEM((SUB, HALO + tm, DM), F32), pltpu.VMEM((SUB, tm + HALO, DM), F32)],
        compiler_params=_cparams(("arbitrary",)))(dzc, dzc, proj, proj, cw)


CUM_ROWS = 512


def _log_sigmoid(x):
    return jnp.minimum(x, 0.0) - jnp.log1p(jnp.exp(-jnp.abs(x)))


def _fox_cum(proj, bfp):
    t = proj.shape[0]
    fcol = C_F // LANE
    cr = min(CUM_ROWS, t)

    def body(f_ref, bf_ref, cum_ref, carry_ref):
        @pl.when(pl.program_id(0) == 0)
        def _():
            carry_ref[...] = jnp.zeros_like(carry_ref)

        lf = _log_sigmoid(f_ref[...] + bf_ref[...])
        tri = (lax.broadcasted_iota(I32, (cr, cr), 0) >= lax.broadcasted_iota(I32, (cr, cr), 1)).astype(F32)
        cum = jnp.dot(tri, lf, preferred_element_type=F32, precision=lax.Precision.HIGHEST) + carry_ref[0:1, :]
        cum_ref[...] = cum
        carry_ref[...] = jnp.broadcast_to(cum[cr - 1:cr, :], carry_ref.shape)

    return pl.pallas_call(
        body, name="fox_cum", grid=(t // cr,), in_specs=[_rows(cr, LANE, fcol), _vec(LANE)],
        out_specs=_rows(cr, LANE), out_shape=jax.ShapeDtypeStruct((t, LANE), F32),
        scratch_shapes=[pltpu.VMEM((8, LANE), F32)],
        compiler_params=_cparams(("arbitrary",)))(proj, bfp)


def _fox_cum_bwd(dcum, proj, bfp):
    t = proj.shape[0]
    cr = min(CUM_ROWS, t)
    nb = t // cr
    fcol = C_F // LANE
    fw = D_INP - C_F

    def body(dc_ref, f_ref, bf_ref, df_ref, dbf_ref, carry_ref):
        @pl.when(pl.program_id(0) == 0)
        def _():
            carry_ref[...] = jnp.zeros_like(carry_ref)
            dbf_ref[...] = jnp.zeros_like(dbf_ref)

        triu = (lax.broadcasted_iota(I32, (cr, cr), 0) <= lax.broadcasted_iota(I32, (cr, cr), 1)).astype(F32)
        dlf = jnp.dot(triu, dc_ref[...], preferred_element_type=F32, precision=lax.Precision.HIGHEST) + carry_ref[0:1, :]
        carry_ref[...] = jnp.broadcast_to(dlf[0:1, :], carry_ref.shape)
        z = f_ref[...] + bf_ref[...]
        lane = lax.broadcasted_iota(I32, (cr, LANE), 1)
        df = jnp.where(lane < NG, dlf * _sigmoid(-z), 0.0)
        dbf_ref[...] += _colsum(df)
        df_ref[:, 0:LANE] = df.astype(BF16)
        df_ref[:, LANE:] = jnp.zeros((cr, fw - LANE), BF16)

    return pl.pallas_call(
        body, name="fox_cum_bwd", grid=(nb,),
        in_specs=[pl.BlockSpec((cr, LANE), lambda i: (nb - 1 - i, 0)),
                  pl.BlockSpec((cr, LANE), lambda i: (nb - 1 - i, fcol)), _vec(LANE)],
        out_specs=[pl.BlockSpec((cr, fw), lambda i: (nb - 1 - i, 0)), _vec(LANE)],
        out_shape=[jax.ShapeDtypeStruct((t, fw), BF16), jax.ShapeDtypeStruct((1, LANE), F32)],
        scratch_shapes=[pltpu.VMEM((8, LANE), F32)],
        compiler_params=_cparams(("arbitrary",)))(dcum, proj, bfp)


HD = 64
ATT_SCALE = 0.125
NEG = -1e30


def _qkv_prep(proj, *, tm=512):
    t = proj.shape[0]
    tm = min(tm, t)

    def body(q_ref, k_ref, v_ref, o_ref):
        o_ref[:, 0:DM] = (q_ref[...] * ATT_SCALE).astype(BF16)
        o_ref[:, DM:2 * DM] = k_ref[...].astype(BF16)
        o_ref[:, 2 * DM:] = v_ref[...].astype(BF16)

    return pl.pallas_call(
        body, name="qkv_prep", grid=(t // tm,),
        in_specs=[_rows(tm, DM, C_Q // DM), _rows(tm, DM, C_K // DM), _rows(tm, DM, C_V // DM)],
        out_specs=_rows(tm, 3 * DM), out_shape=jax.ShapeDtypeStruct((t, 3 * DM), BF16),
        compiler_params=_cparams(("parallel",)))(proj, proj, proj)


def _causal_pairs(nq, outer_is_query):
    if outer_is_query:
        pairs = [(i, j) for i in range(nq) for j in range(i + 1)]
    else:
        pairs = [(j, i) for j in range(nq) for i in range(j, nq)]
    return (jnp.asarray([p[0] for p in pairs], I32), jnp.asarray([p[1] for p in pairs], I32))


def _to_row(col):
    return jnp.transpose(col)[0:1, :]


def _rep(x, tk):
    return x if tk == LANE else jnp.tile(x, (1, tk // LANE))


def _attn_fwd(qkv, ckrow, *, tq=512):
    t = qkv.shape[0]
    tq = min(tq, t)
    tk = tq
    nq = t // tq
    oi, ij = _causal_pairs(nq, True)

    def body(oi_ref, ij_ref, q_ref, k_ref, v_ref, ck_ref, o_ref, lse_ref, lser_ref, m_ref, l_ref, acc_ref):
        n = pl.program_id(0)
        i, j = oi_ref[n], ij_ref[n]

        @pl.when(j == 0)
        def _():
            m_ref[...] = jnp.full_like(m_ref, NEG)
            l_ref[...] = jnp.zeros_like(l_ref)
            acc_ref[...] = jnp.zeros_like(acc_ref)

        def step(masked):
            if masked:
                keep = lax.broadcasted_iota(I32, (tq, tk), 1) <= lax.broadcasted_iota(I32, (tq, tk), 0)
            lo = lax.broadcasted_iota(I32, (tq, LANE), 1) < HD
            for hp in range(NG // 2):
                cs = slice(hp * LANE, (hp + 1) * LANE)
                qp, kp, vp = q_ref[:, cs], k_ref[:, cs], v_ref[:, cs]
                alphas, pvs = [], []
                for hh in range(2):
                    h = 2 * hp + hh
                    qm = jnp.where(lo if hh == 0 else jnp.logical_not(lo), qp, jnp.zeros_like(qp))
                    s = _dot(qm, kp, NT) - ck_ref[h:h + 1, :]
                    if masked:
                        s = jnp.where(keep, s, NEG)
                    m_prev = m_ref[h]
                    m_new = jnp.maximum(m_prev, jnp.max(s, axis=1, keepdims=True))
                    alpha = jnp.exp(m_prev - m_new)
                    p = jnp.exp(s - _rep(m_new, tk))
                    l_ref[h] = alpha * l_ref[h] + jnp.sum(p, axis=1, keepdims=True)
                    m_ref[h] = m_new
                    alphas.append(alpha)
                    pvs.append(_dot(p.astype(BF16), vp))
                acc_ref[:, cs] = jnp.where(lo, alphas[0], alphas[1]) * acc_ref[:, cs] + jnp.where(lo, pvs[0], pvs[1])

        @pl.when(j < i)
        def _():
            step(False)

        @pl.when(j == i)
        def _():
            step(True)
            lo = lax.broadcasted_iota(I32, (tq, LANE), 1) < HD
            for hp in range(NG // 2):
                cs = slice(hp * LANE, (hp + 1) * LANE)
                o_ref[:, cs] = (acc_ref[:, cs] / jnp.where(lo, l_ref[2 * hp], l_ref[2 * hp + 1])).astype(BF16)
            for h in range(NG):
                lse = m_ref[h] + jnp.log(l_ref[h])
                lse_ref[h] = lse
                lser_ref[h:h + 1, :] = _to_row(lse)

    gs = pltpu.PrefetchScalarGridSpec(
        num_scalar_prefetch=2, grid=(int(oi.shape[0]),),
        in_specs=[pl.BlockSpec((tq, DM), lambda n, a, b: (a[n], 0)),
                  pl.BlockSpec((tk, DM), lambda n, a, b: (b[n], 1)),
                  pl.BlockSpec((tk, DM), lambda n, a, b: (b[n], 2)),
                  pl.BlockSpec((NG, tk), lambda n, a, b: (0, b[n]))],
        out_specs=[pl.BlockSpec((tq, DM), lambda n, a, b: (a[n], 0)),
                   pl.BlockSpec((NG, tq, LANE), lambda n, a, b: (0, a[n], 0)),
                   pl.BlockSpec((NG, tq), lambda n, a, b: (0, a[n]))],
        scratch_shapes=[pltpu.VMEM((NG, tq, LANE), F32), pltpu.VMEM((NG, tq, LANE), F32), pltpu.VMEM((tq, DM), F32)])
    return pl.pallas_call(
        body, name="attn_fwd", grid_spec=gs,
        out_shape=[jax.ShapeDtypeStruct((t, DM), BF16), jax.ShapeDtypeStruct((NG, t, LANE), F32),
                   jax.ShapeDtypeStruct((NG, t), F32)],
        compiler_params=_cparams(("arbitrary",)))(oi, ij, qkv, qkv, qkv, ckrow)


def _attn_bwd_dq(qkv, ckrow, o, do, lse, *, tq=512):
    t = qkv.shape[0]
    tq = min(tq, t)
    tk = tq
    nq = t // tq
    oi, ij = _causal_pairs(nq, True)

    def body(oi_ref, ij_ref, q_ref, k_ref, v_ref, ck_ref, o_ref, do_ref, lse_ref, dq_ref, deltar_ref, dcqr_ref, delta_ref,
             dcq_ref, acc_ref):
        n = pl.program_id(0)
        i, j = oi_ref[n], ij_ref[n]

        @pl.when(j == 0)
        def _():
            acc_ref[...] = jnp.zeros_like(acc_ref)
            dcq_ref[...] = jnp.zeros_like(dcq_ref)
            lo = lax.broadcasted_iota(I32, (tq, LANE), 1) < HD
            for hp in range(NG // 2):
                cs = slice(hp * LANE, (hp + 1) * LANE)
                prod = do_ref[:, cs] * o_ref[:, cs].astype(F32)
                for hh in range(2):
                    d = jnp.sum(jnp.where(lo if hh == 0 else jnp.logical_not(lo), prod, 0.0), axis=1, keepdims=True)
                    dcol = jnp.broadcast_to(d, (tq, LANE))
                    delta_ref[2 * hp + hh] = dcol
                    deltar_ref[2 * hp + hh:2 * hp + hh + 1, :] = _to_row(dcol)

        def step(masked):
            if masked:
                keep = lax.broadcasted_iota(I32, (tq, tk), 1) <= lax.broadcasted_iota(I32, (tq, tk), 0)
            lo = lax.broadcasted_iota(I32, (tq, LANE), 1) < HD
            for hp in range(NG // 2):
                cs = slice(hp * LANE, (hp + 1) * LANE)
                qp, kp, vp, dop = q_ref[:, cs], k_ref[:, cs], v_ref[:, cs], do_ref[:, cs].astype(BF16)
                parts = []
                for hh in range(2):
                    h = 2 * hp + hh
                    sel = lo if hh == 0 else jnp.logical_not(lo)
                    s = _dot(jnp.where(sel, qp, jnp.zeros_like(qp)), kp, NT) - ck_ref[h:h + 1, :]
                    if masked:
                        s = jnp.where(keep, s, NEG)
                    p = jnp.exp(s - _rep(lse_ref[h], tk))
                    dp = _dot(jnp.where(sel, dop, jnp.zeros_like(dop)), vp, NT)
                    ds = p * (dp - _rep(delta_ref[h], tk))
                    dcq_ref[h] += jnp.sum(ds, axis=1, keepdims=True)
                    parts.append(_dot(ds.astype(BF16), kp))
                acc_ref[:, cs] += jnp.where(lo, parts[0], parts[1])

        @pl.when(j < i)
        def _():
            step(False)

        @pl.when(j == i)
        def _():
            step(True)
            dq_ref[...] = (acc_ref[...] * ATT_SCALE).astype(BF16)
            for h in range(NG):
                dcqr_ref[h:h + 1, :] = _to_row(dcq_ref[h])

    gs = pltpu.PrefetchScalarGridSpec(
        num_scalar_prefetch=2, grid=(int(oi.shape[0]),),
        in_specs=[pl.BlockSpec((tq, DM), lambda n, a, b: (a[n], 0)),
                  pl.BlockSpec((tk, DM), lambda n, a, b: (b[n], 1)),
                  pl.BlockSpec((tk, DM), lambda n, a, b: (b[n], 2)),
                  pl.BlockSpec((NG, tk), lambda n, a, b: (0, b[n])),
                  pl.BlockSpec((tq, DM), lambda n, a, b: (a[n], 0)),
                  pl.BlockSpec((tq, DM), lambda n, a, b: (a[n], 0)),
                  pl.BlockSpec((NG, tq, LANE), lambda n, a, b: (0, a[n], 0))],
        out_specs=[pl.BlockSpec((tq, DM), lambda n, a, b: (a[n], 0)),
                   pl.BlockSpec((NG, tq), lambda n, a, b: (0, a[n])),
                   pl.BlockSpec((NG, tq), lambda n, a, b: (0, a[n]))],
        scratch_shapes=[pltpu.VMEM((NG, tq, LANE), F32), pltpu.VMEM((NG, tq, LANE), F32), pltpu.VMEM((tq, DM), F32)])
    return pl.pallas_call(
        body, name="attn_bwd_dq", grid_spec=gs,
        out_shape=[jax.ShapeDtypeStruct((t, DM), BF16), jax.ShapeDtypeStruct((NG, t), F32), jax.ShapeDtypeStruct((NG, t), F32)],
        compiler_params=_cparams(("arbitrary",)))(oi, ij, qkv, qkv, qkv, ckrow, o, do, lse)


def _attn_bwd_dkv(qkv, ckcol, do, lserow, deltarow, *, tq=512):
    t = qkv.shape[0]
    tq = min(tq, t)
    tk = tq
    nq = t // tq
    oj, ii = _causal_pairs(nq, False)

    def body(oj_ref, ii_ref, q_ref, k_ref, v_ref, ck_ref, do_ref, lse_ref, delta_ref, dk_ref, dv_ref, dckr_ref, dka_ref,
             dva_ref, dck_ref):
        n = pl.program_id(0)
        j, i = oj_ref[n], ii_ref[n]

        @pl.when(i == j)
        def _():
            dka_ref[...] = jnp.zeros_like(dka_ref)
            dva_ref[...] = jnp.zeros_like(dva_ref)
            dck_ref[...] = jnp.zeros_like(dck_ref)

        def step(masked):
            if masked:
                keep = lax.broadcasted_iota(I32, (tk, tq), 0) <= lax.broadcasted_iota(I32, (tk, tq), 1)
            lo = lax.broadcasted_iota(I32, (tk, LANE), 1) < HD
            for hp in range(NG // 2):
                cs = slice(hp * LANE, (hp + 1) * LANE)
                qp, kp, vp, dop = q_ref[:, cs], k_ref[:, cs], v_ref[:, cs], do_ref[:, cs].astype(BF16)
                dvs, dks = [], []
                for hh in range(2):
                    h = 2 * hp + hh
                    sel = lo if hh == 0 else jnp.logical_not(lo)
                    st = _dot(jnp.where(sel, kp, jnp.zeros_like(kp)), qp, NT) - _rep(ck_ref[h], tq)
                    if masked:
                        st = jnp.where(keep, st, NEG)
                    pt = jnp.exp(st - lse_ref[h:h + 1, :])
                    dvs.append(_dot(pt.astype(BF16), dop))
                    dpt = _dot(jnp.where(sel, vp, jnp.zeros_like(vp)), dop, NT)
                    dst = pt * (dpt - delta_ref[h:h + 1, :])
                    dks.append(_dot(dst.astype(BF16), qp))
                    dck_ref[h] -= jnp.sum(dst, axis=1, keepdims=True)
                dva_ref[:, cs] += jnp.where(lo, dvs[0], dvs[1])
                dka_ref[:, cs] += jnp.where(lo, dks[0], dks[1])

        @pl.when(i == j)
        def _():
            step(True)

        @pl.when(i > j)
        def _():
            step(False)

        @pl.when(i == nq - 1)
        def _():
            dk_ref[...] = dka_ref[...].astype(BF16)
            dv_ref[...] = dva_ref[...].astype(BF16)
            for h in range(NG):
                dckr_ref[h:h + 1, :] = _to_row(dck_ref[h])

    gs = pltpu.PrefetchScalarGridSpec(
        num_scalar_prefetch=2, grid=(int(oj.shape[0]),),
        in_specs=[pl.BlockSpec((tq, DM), lambda n, a, b: (b[n], 0)),
                  pl.BlockSpec((tk, DM), lambda n, a, b: (a[n], 1)),
                  pl.BlockSpec((tk, DM), lambda n, a, b: (a[n], 2)),
                  pl.BlockSpec((NG, tk, LANE), lambda n, a, b: (0, a[n], 0)),
                  pl.BlockSpec((tq, DM), lambda n, a, b: (b[n], 0)),
                  pl.BlockSpec((NG, tq), lambda n, a, b: (0, b[n])),
                  pl.BlockSpec((NG, tq), lambda n, a, b: (0, b[n]))],
        out_specs=[pl.BlockSpec((tk, DM), lambda n, a, b: (a[n], 0)),
                   pl.BlockSpec((tk, DM), lambda n, a, b: (a[n], 0)),
                   pl.BlockSpec((NG, tk), lambda n, a, b: (0, a[n]))],
        scratch_shapes=[pltpu.VMEM((tk, DM), F32), pltpu.VMEM((tk, DM), F32), pltpu.VMEM((NG, tk, LANE), F32)])
    return pl.pallas_call(
        body, name="attn_bwd_dkv", grid_spec=gs,
        out_shape=[jax.ShapeDtypeStruct((t, DM), BF16), jax.ShapeDtypeStruct((t, DM), BF16), jax.ShapeDtypeStruct((NG, t), F32)],
        compiler_params=_cparams(("arbitrary",)))(oj, ii, qkv, qkv, qkv, ckcol, do, lserow, deltarow)


def _merge_fwd(ga, zb, att, proj, wa, wb, wc, *, tm=256):
    t = ga.shape[0]
    tm = min(tm, t)
    wspec = pl.BlockSpec((DM, D), lambda i: (0, 0))

    def body(ga_ref, zb_ref, att_ref, gate_ref, wa_ref, wb_ref, wc_ref, m_ref):
        acc = jnp.zeros((tm, D), F32)
        for b, (x_ref, w_ref) in enumerate(((ga_ref, wa_ref), (zb_ref, wb_ref), (att_ref, wc_ref))):
            acc = acc + _sigmoid(gate_ref[:, b * D:(b + 1) * D]) * _dot(x_ref[...], w_ref[...])
        m_ref[...] = acc.astype(BF16)

    return pl.pallas_call(
        body, name="merge_fwd", grid=(t // tm,),
        in_specs=[_rows(tm, DM), _rows(tm, DM), _rows(tm, DM), _rows(tm, 3 * D, 0), wspec, wspec, wspec],
        out_specs=_rows(tm, D), out_shape=jax.ShapeDtypeStruct((t, D), BF16),
        compiler_params=_cparams(("parallel",)))(ga, zb, att, proj, wa, wb, wc)


def _merge_bwd(dm, ga, zb, att, proj, wa, wb, wc, *, tm=256):
    t = ga.shape[0]
    tm = min(tm, t)
    wspec = pl.BlockSpec((DM, D), lambda i: (0, 0))

    def body(dm_ref, ga_ref, zb_ref, att_ref, gate_ref, wa_ref, wb_ref, wc_ref, dgate_ref, dga_ref, dzb_ref, datt_ref,
             dwa_ref, dwb_ref, dwc_ref):
        @pl.when(pl.program_id(0) == 0)
        def _():
            dwa_ref[...] = jnp.zeros_like(dwa_ref)
            dwb_ref[...] = jnp.zeros_like(dwb_ref)
            dwc_ref[...] = jnp.zeros_like(dwc_ref)

        dmv = dm_ref[...]
        branches = ((ga_ref, wa_ref, dga_ref, dwa_ref), (zb_ref, wb_ref, dzb_ref, dwb_ref),
                    (att_ref, wc_ref, datt_ref, dwc_ref))
        for b, (x_ref, w_ref, dx_ref, dw_ref) in enumerate(branches):
            xv, wv = x_ref[...], w_ref[...]
            y = _dot(xv, wv)
            g = _sigmoid(gate_ref[:, b * D:(b + 1) * D])
            dgate_ref[:, b * D:(b + 1) * D] = (dmv * y * g * (1.0 - g)).astype(BF16)
            dy = (dmv * g).astype(BF16)
            dx_ref[...] = _dot(dy, wv, NT)
            dw_ref[...] += _dot(xv, dy, TN)

    return pl.pallas_call(
        body, name="merge_bwd", grid=(t // tm,),
        in_specs=[_rows(tm, D), _rows(tm, DM), _rows(tm, DM), _rows(tm, DM), _rows(tm, 3 * D, 0), wspec, wspec, wspec],
        out_specs=[_rows(tm, 3 * D), _rows(tm, DM), _rows(tm, DM), _rows(tm, DM), wspec, wspec, wspec],
        out_shape=[jax.ShapeDtypeStruct((t, 3 * D), BF16)] + [jax.ShapeDtypeStruct((t, DM), F32)] * 3
        + [jax.ShapeDtypeStruct((DM, D), F32)] * 3,
        compiler_params=_cparams(("arbitrary",)))(dm, ga, zb, att, proj, wa, wb, wc)


def _heads_layout(cum):
    t = cum.shape[0]
    ckrow = cum[:, :NG].T
    return ckrow, jnp.broadcast_to(ckrow[:, :, None], (NG, t, LANE))


def _layer_fwd(x, mod, w, late=None):
    sh1, sc1, gt1, sh2, sc2, gt2 = (mod[k:k + 1] for k in range(NMOD))
    h1 = _norm_mod(x, w["mix_pre_g"], sc1, sh1, name="norm_mix")
    proj = _matmul(h1, w["w_in_p"], name="mm_proj")
    ga = _gmlp_fwd(proj, w["gmlp_ln_g"], w["gmlp_ln_b"], w["wsm"], w["bsx"])
    zc, zb = _conv_fwd(proj, w["conv_w"], w["conv_b"], w["conv_ln_g"], w["conv_ln_b"])
    cum = _fox_cum(proj, w["bfp"])
    ckrow, ckcol = _heads_layout(cum)
    qkv = _qkv_prep(proj)
    att, lse, lser = _attn_fwd(qkv, ckrow)
    if late is not None:
        w = {**w, **late(att)}
    merged = _merge_fwd(ga, zb, att, proj, w["w_a_out"], w["w_b_out"], w["w_c_out"])
    y1 = _matmul(merged, w["w_out"], name="mm_out")
    x2 = _resid(x, y1, gt1, w["mix_post_g"], name="resid_mix")
    h2 = _norm_mod(x2, w["mlp_pre_g"], sc2, sh2, name="norm_mlp")
    a, hid = _matmul(h2, w["mlp_w1"], name="mm_w1", b_slabs=True, extra_out=(BF16,),
                     epilogue=lambda acc: (acc, jnp.square(jnp.maximum(acc, 0.0))))
    y2 = _matmul(hid, w["mlp_w2"], name="mm_w2", tk=TK_DEEP)
    x3 = _resid(x2, y2, gt2, w["mlp_post_g"], name="resid_mlp")
    saved = dict(x=x, h1=h1, proj=proj, ga=ga, zc=zc, zb=zb, qkv=qkv, ckrow=ckrow, ckcol=ckcol, att=att, lse=lse, lser=lser, merged=merged,
                 y1=y1, x2=x2, h2=h2, a=a, hid=hid, y2=y2)
    return x3, saved, w


def _layer_bwd(dx3, mod, w, s, mid=None):
    sh1, sc1, gt1, sh2, sc2, gt2 = (mod[k:k + 1] for k in range(NMOD))
    g = {}
    dy2, dgt2, g["mlp_post_g"] = _resid_bwd(dx3, s["y2"], gt2, w["mlp_post_g"], name="resid_mlp_bwd")
    da = _matmul(dy2, w["mlp_w2"], tb=True, name="mm_dhid", out_dtype=BF16, extra=(s["a"],),
                 epilogue=lambda acc, a: (acc * (2.0 * jnp.maximum(a, 0.0)),))
    g["mlp_w2"] = _matmul(s["hid"], dy2, ta=True, name="mm_dw2", tk=TK_DEEP)
    g["mlp_w1"] = _matmul(s["h2"], da, ta=True, name="mm_dw1", out_slabs=N_CHIPS, tk=TK_DEEP)
    dh2 = _matmul(da, w["mlp_w1"], tb=True, name="mm_dh2", b_slabs=True)
    dx2, g["mlp_pre_g"], dsc2, dsh2 = _norm_bwd(dh2, dx3, s["x2"], w["mlp_pre_g"], sc2, name="norm_mlp_bwd")
    if mid is not None:
        gt1 = gt1 + mid(dx2)
    dy1, dgt1, g["mix_post_g"] = _resid_bwd(dx2, s["y1"], gt1, w["mix_post_g"], name="resid_mix_bwd")
    dmerged = _matmul(dy1, w["w_out"], tb=True, name="mm_dmerged")
    g["w_out"] = _matmul(s["merged"], dy1, ta=True, name="mm_dwout", tk=TK_DEEP)
    dgate, dga, dzb, datt, g["w_a_out"], g["w_b_out"], g["w_c_out"] = _merge_bwd(
        dmerged, s["ga"], s["zb"], s["att"], s["proj"], w["w_a_out"], w["w_b_out"], w["w_c_out"])
    duv, g["gmlp_ws"], dbs, g["gmlp_ln_g"], g["gmlp_ln_b"] = _gmlp_bwd(
        dga, s["proj"], w["gmlp_ln_g"], w["gmlp_ln_b"], w["wsm"], w["wsmt"], w["bsx"])
    g["gmlp_bs"] = dbs[:, :NG].T
    dzc, g["conv_ln_g"], g["conv_ln_b"] = _conv_bwd_ln(dzb, s["zc"], w["conv_ln_g"], w["conv_ln_b"])
    dglu, dcw, g["conv_b"] = _conv_bwd(dzc, s["proj"], w["conv_w"])
    g["conv_w"] = dcw[:KW]
    dq, delta, dcq = _attn_bwd_dq(s["qkv"], s["ckrow"], s["att"], datt, s["lse"])
    dk, dv, dck = _attn_bwd_dkv(s["qkv"], s["ckcol"], datt, s["lser"], delta)
    dcum = jnp.pad((dcq + dck).T, ((0, 0), (0, LANE - NG)))
    df, dbf = _fox_cum_bwd(dcum, s["proj"], w["bfp"])
    g["fox_bf"] = dbf[0, :NG]
    dproj = jnp.concatenate([dgate, duv, dglu, dq, dk, dv, df], axis=1)
    g["w_in_p"] = _matmul(s["h1"], dproj, ta=True, name="mm_dwin", tk=TK_DEEP)
    dh1 = _matmul(dproj, w["w_in_p"], tb=True, name="mm_dh1", tk=D_INP // 2)
    dx, g["mix_pre_g"], dsc1, dsh1 = _norm_bwd(dh1, dx2, s["x"], w["mix_pre_g"], sc1, name="norm_mix_bwd")
    dmod = jnp.concatenate([dsh1, dsc1, dgt1, dsh2, dsc2, dgt2], axis=0)
    return dx, g, dmod


def _position():
    return lax.axis_index("x"), lax.axis_index("y"), lax.axis_index("c")


def _all_gather8(v):
    m_per, n = v.shape

    def body(x_ref, out_ref, send_sems, recv_sems, local_sem):
        x, y, c = _position()
        me, sibling = (x, y, c), (x, y, 1 - c)
        chips = [(1 - x, y), (x, 1 - y), (1 - x, 1 - y)]

        def rows(px, py, pc):
            return out_ref.at[pl.ds((4 * px + 2 * py + pc) * m_per, m_per), :]

        def copy(k, block, to, src=None):
            return pltpu.make_async_remote_copy(
                src_ref=rows(*block) if src is None else src, dst_ref=rows(*block), send_sem=send_sems.at[k],
                recv_sem=recv_sems.at[k], device_id=to, device_id_type=MESH)

        mine = pltpu.make_async_copy(x_ref, rows(*me), local_sem)
        mine.start()
        first = [copy(0, me, sibling, src=x_ref)]
        first += [copy(1 + j, me, (*chip, c), src=x_ref) for j, chip in enumerate(chips)]
        for cp in first:
            cp.start()
        passed = [copy(4 + j, (*chip, c), sibling) for j, chip in enumerate(chips)]
        for j, chip in enumerate(chips):
            copy(1 + j, (*chip, c), me).wait_recv()
            passed[j].start()
        copy(0, sibling, me).wait_recv()
        for j, chip in enumerate(chips):
            copy(4 + j, (*chip, 1 - c), me).wait_recv()
        for cp in first + passed:
            cp.wait_send()
        mine.wait()

    out = pl.pallas_call(
        body, name="all_gather8", out_shape=jax.ShapeDtypeStruct((N_DEV * m_per, n), v.dtype),
        in_specs=[pl.BlockSpec(memory_space=pltpu.VMEM)], out_specs=pl.BlockSpec(memory_space=pltpu.VMEM),
        scratch_shapes=[pltpu.SemaphoreType.DMA((7,)), pltpu.SemaphoreType.DMA((7,)), pltpu.SemaphoreType.DMA],
        compiler_params=pltpu.CompilerParams(vmem_limit_bytes=VMEM_LIMIT),
    )(v)
    return out.reshape(N_DEV, m_per, n)


def _half(c, rows):
    return pl.ds(c * (rows // 2), rows // 2)


HBM = pl.BlockSpec(memory_space=pltpu.HBM)
SEM = pl.BlockSpec(memory_space=pltpu.SEMAPHORE)
EFFECT = pltpu.SideEffectType.DATAFLOW_SIDE_EFFECTING


_COPIES_PER_ARRAY = {"gather": 3, "scatter": 3, "swap": N_CHIPS}


def _ici_copies(kind, src_refs, land_refs, send_sems, recv_sems, arriving):
    x, y, c = _position()
    chips = [(1 - x, y), (x, 1 - y), (1 - x, 1 - y)]
    nper = _COPIES_PER_ARRAY[kind]
    copies = []
    for wi, (src, land) in enumerate(zip(src_refs, land_refs)):
        for k in range(nper):
            if kind == "swap":
                peer = (x, y, 1 - c)
                s_win, dst = src.at[k, _half(1 - c, src.shape[1])], land.at[k]
            else:
                px, py = chips[k]
                peer = (px, py, c)
                if kind == "gather":
                    rows = src.shape[0]
                    s_win = src.at[_half(c, rows)]
                    dst = land.at[2 * px + py if arriving else 2 * x + y, _half(c, rows)]
                else:
                    s_win, dst = src.at[2 * px + py], land.at[k]
            copies.append(pltpu.make_async_remote_copy(
                src_ref=s_win, dst_ref=dst, send_sem=send_sems.at[wi * nper + k], recv_sem=recv_sems.at[wi * nper + k],
                device_id=peer, device_id_type=MESH))
    return copies


def _ici_start(kind, srcs, land_shapes, after, *, name):
    nw = len(srcs)

    def body(*refs):
        src_refs, land_refs = refs[:nw], refs[nw:2 * nw]
        send_sems, recv_sems = refs[2 * nw + 1:2 * nw + 3]
        token = refs[-1]
        for cp in _ici_copies(kind, src_refs, land_refs, send_sems, recv_sems, False):
            cp.start()
        token[...] = jnp.zeros_like(token)

    lands = [pltpu.with_memory_space_constraint(lax.empty(shp, s.dtype), pltpu.HBM) for shp, s in zip(land_shapes, srcs)]
    outs = pl.pallas_call(
        body, name=name,
        out_shape=(pltpu.SemaphoreType.DMA((_COPIES_PER_ARRAY[kind] * nw,)), pltpu.SemaphoreType.DMA((_COPIES_PER_ARRAY[kind] * nw,)),
                   *[pltpu.HBM(s.shape, s.dtype) for s in srcs], *[pltpu.HBM(shp, s.dtype) for shp, s in zip(land_shapes, srcs)],
                   jax.ShapeDtypeStruct((8, LANE), F32)),
        in_specs=[HBM] * (2 * nw) + [ANY], out_specs=(SEM, SEM, *[HBM] * (2 * nw), pl.BlockSpec(memory_space=pltpu.VMEM)),
        input_output_aliases={i: 2 + i for i in range(2 * nw)},
        compiler_params=pltpu.CompilerParams(has_side_effects=EFFECT),
    )(*[pltpu.with_memory_space_constraint(s, pltpu.HBM) for s in srcs], *lands, after)
    return outs[0], outs[1], outs[2:2 + nw], outs[2 + nw:2 + 2 * nw], outs[-1]


def _ici_wait(kind, send_sems, recv_sems, srcs, lands, after, *, name):
    nw = len(srcs)

    def body(*refs):
        src_refs, land_refs = refs[:nw], refs[nw:2 * nw]
        s_sems, r_sems = refs[2 * nw:2 * nw + 2]
        for cp in _ici_copies(kind, src_refs, land_refs, s_sems, r_sems, False):
            cp.wait_send()
        for cp in _ici_copies(kind, src_refs, land_refs, s_sems, r_sems, True):
            cp.wait_recv()

    outs = pl.pallas_call(
        body, name=name,
        out_shape=(*[pltpu.HBM(s.shape, s.dtype) for s in srcs], *[pltpu.HBM(a.shape, a.dtype) for a in lands]),
        in_specs=[HBM] * (2 * nw) + [SEM, SEM, ANY], out_specs=tuple([HBM] * (2 * nw)),
        input_output_aliases={i: i for i in range(2 * nw)},
        compiler_params=pltpu.CompilerParams(has_side_effects=EFFECT),
    )(*srcs, *lands, send_sems, recv_sems, after)
    return outs[nw:]


def _ag_d2d(lands, *, name):
    nw = len(lands)

    def body(*refs):
        out_refs = refs[nw:2 * nw]
        send_sems, recv_sems = refs[2 * nw:]
        x, y, c = _position()
        chips = [(1 - x, y), (x, 1 - y), (1 - x, 1 - y)]
        copies = []
        for wi in range(nw):
            rows = out_refs[wi].shape[1]
            for k, (px, py) in enumerate(chips):
                win = out_refs[wi].at[2 * px + py, _half(c, rows)]
                copies.append(pltpu.make_async_remote_copy(
                    src_ref=win, dst_ref=win, send_sem=send_sems.at[wi * 3 + k], recv_sem=recv_sems.at[wi * 3 + k],
                    device_id=(x, y, 1 - c), device_id_type=MESH))
        for cp in copies:
            cp.start()
        for wi in range(nw):
            rows = out_refs[wi].shape[1]
            for k, (px, py) in enumerate(chips):
                win = out_refs[wi].at[2 * px + py, _half(1 - c, rows)]
                pltpu.make_async_remote_copy(
                    src_ref=win, dst_ref=win, send_sem=send_sems.at[wi * 3 + k], recv_sem=recv_sems.at[wi * 3 + k],
                    device_id=(x, y, 1 - c), device_id_type=MESH).wait_recv()
        for cp in copies:
            cp.wait_send()

    return pl.pallas_call(
        body, name=name, out_shape=[jax.ShapeDtypeStruct(a.shape, a.dtype) for a in lands],
        in_specs=[ANY] * nw, out_specs=[ANY] * nw, input_output_aliases={i: i for i in range(nw)},
        scratch_shapes=[pltpu.SemaphoreType.DMA((3 * nw,)), pltpu.SemaphoreType.DMA((3 * nw,))],
    )(*lands)


def _swap_reduced(reds):
    nw = len(reds)

    def body(*refs):
        r_refs, out_refs = refs[:nw], refs[nw:2 * nw]
        send_sems, recv_sems = refs[2 * nw:]
        x, y, c = _position()
        copies = [pltpu.make_async_remote_copy(
            src_ref=r_refs[wi], dst_ref=out_refs[wi], send_sem=send_sems.at[wi], recv_sem=recv_sems.at[wi],
            device_id=(x, y, 1 - c), device_id_type=MESH) for wi in range(nw)]
        for cp in copies:
            cp.start()
        for cp in copies:
            cp.wait()

    return pl.pallas_call(
        body, name="rs_swap_reduced", out_shape=[jax.ShapeDtypeStruct(r.shape, r.dtype) for r in reds],
        in_specs=[ANY] * nw, out_specs=[ANY] * nw,
        scratch_shapes=[pltpu.SemaphoreType.DMA((nw,)), pltpu.SemaphoreType.DMA((nw,))],
    )(*reds)


def _row_tile(rows, cols):
    tr = rows
    while tr * cols * 4 > (2 << 20) and tr % 32 == 0:
        tr //= 2
    return tr


def _add_own_half(g, recv, c, *, name):
    nj, h, w = recv.shape
    tr = _row_tile(h, w)
    nb = h // tr

    def body(c_ref, g_ref, r_ref, o_ref, ob_ref):
        sm = g_ref[0] + r_ref[0]
        o_ref[0] = sm
        ob_ref[0] = sm.astype(BF16)

    spec = pl.BlockSpec((1, tr, w), lambda j, i, cc: (j, i, 0))
    gs = pltpu.PrefetchScalarGridSpec(
        num_scalar_prefetch=1, grid=(nj, nb),
        in_specs=[pl.BlockSpec((1, tr, w), lambda j, i, cc: (j, cc[0] * nb + i, 0)), spec],
        out_specs=[spec, spec])
    return pl.pallas_call(body, name=name, grid_spec=gs,
                          out_shape=[jax.ShapeDtypeStruct((nj, h, w), F32), jax.ShapeDtypeStruct((nj, h, w), BF16)],
                          compiler_params=_cparams(("parallel", "parallel")))(jnp.reshape(c, (1,)).astype(I32), g, recv)


def _add_own_chip(sp, recv, j, *, name):
    _, r, w = sp.shape
    tr = _row_tile(r, w)

    def body(j_ref, s_ref, r_ref, o_ref):
        o_ref[...] = ((s_ref[0] + r_ref[0].astype(F32)) + r_ref[1].astype(F32)) + r_ref[2].astype(F32)

    gs = pltpu.PrefetchScalarGridSpec(
        num_scalar_prefetch=1, grid=(r // tr,),
        in_specs=[pl.BlockSpec((1, tr, w), lambda i, jj: (jj[0], i, 0)), pl.BlockSpec((3, tr, w), lambda i, jj: (0, i, 0))],
        out_specs=pl.BlockSpec((tr, w), lambda i, jj: (i, 0)))
    return pl.pallas_call(body, name=name, grid_spec=gs, out_shape=jax.ShapeDtypeStruct((r, w), F32),
                          compiler_params=_cparams(("parallel",)))(jnp.reshape(j, (1,)).astype(I32), sp, recv)


def _sum8(v):
    _, m, n = v.shape

    def body(v_ref, o_ref):
        acc = v_ref[0]
        for k in range(1, N_DEV):
            acc = acc + v_ref[k]
        o_ref[...] = acc

    return pl.pallas_call(body, name="sum8", grid=(1,), in_specs=[pl.BlockSpec((N_DEV, m, n), lambda i: (0, 0, 0))],
                          out_specs=pl.BlockSpec((m, n), lambda i: (0, 0)), out_shape=jax.ShapeDtypeStruct((m, n), F32),
                          compiler_params=_cparams(("arbitrary",)))(v)


def _ada_mod(c_all, ada_w, ada_b_loc, *, tn=512):
    nl, _, ncol = ada_w.shape

    def body(c_ref, w_ref, b_ref, o_ref):
        cv = c_ref[...]
        ca = (cv * _sigmoid(cv)).astype(BF16)
        o_ref[0] = _dot(ca, w_ref[0].astype(BF16)) + b_ref[0]

    return pl.pallas_call(
        body, name="ada_mod", grid=(nl, ncol // tn),
        in_specs=[pl.BlockSpec((N_DEV, D), lambda l, j: (0, 0)), pl.BlockSpec((1, D, tn), lambda l, j: (l, 0, j)),
                  pl.BlockSpec((1, 1, tn), lambda l, j: (l, 0, j))],
        out_specs=pl.BlockSpec((1, N_DEV, tn), lambda l, j: (l, 0, j)),
        out_shape=jax.ShapeDtypeStruct((nl, N_DEV, ncol), F32),
        compiler_params=_cparams(("parallel", "parallel")))(c_all, ada_w, ada_b_loc)


def _ada_grad(c_pad, dmod_pad, *, tn=512):
    nl, nb, ncol = dmod_pad.shape

    def body(c_ref, d_ref, o_ref):
        cv = c_ref[...]
        ca = (cv * _sigmoid(cv)).astype(BF16)
        o_ref[0] = _dot(ca, d_ref[0].astype(BF16), TN)

    return pl.pallas_call(
        body, name="ada_grad", grid=(nl, ncol // tn),
        in_specs=[pl.BlockSpec((nb, D), lambda l, j: (0, 0)), pl.BlockSpec((1, nb, tn), lambda l, j: (l, 0, j))],
        out_specs=pl.BlockSpec((1, D, tn), lambda l, j: (l, 0, j)),
        out_shape=jax.ShapeDtypeStruct((nl, D, ncol), F32),
        compiler_params=_cparams(("parallel", "parallel")))(c_pad, dmod_pad)


def _adamw(w, g, m, v, *, name):
    shape = w.shape
    if w.ndim == 3:
        lead, rows, cols = shape
    else:
        lead, (rows, cols) = 1, shape
    w3, g3, m3, v3 = (a.reshape(lead, rows, cols) for a in (w, g, m, v))
    tr = rows
    if rows * cols * 4 > (2 << 20):
        tr = next(cand for cand in (256, 128, 64, 8) if rows % cand == 0)
    c1 = 1.0 - ADAM_B1 ** ADAM_STEP
    c2 = 1.0 - ADAM_B2 ** ADAM_STEP

    def body(w_ref, g_ref, m_ref, v_ref, d_ref, nm_ref, nv_ref):
        gv = g_ref[...]
        nm = ADAM_B1 * m_ref[...] + (1.0 - ADAM_B1) * gv
        nv = ADAM_B2 * v_ref[...] + (1.0 - ADAM_B2) * (gv * gv)
        nm_ref[...] = nm
        nv_ref[...] = nv
        d_ref[...] = -ADAM_LR * ((nm / c1) / (jnp.sqrt(nv / c2) + ADAM_EPS) + ADAM_WD * w_ref[...])

    spec = pl.BlockSpec((1, tr, cols), lambda l, i: (l, i, 0))
    outs = pl.pallas_call(
        body, name=name, grid=(lead, rows // tr), in_specs=[spec] * 4, out_specs=[spec] * 3,
        out_shape=[jax.ShapeDtypeStruct((lead, rows, cols), F32)] * 3,
        compiler_params=_cparams(("parallel", "parallel")))(w3, g3, m3, v3)
    return tuple(o.reshape(shape) for o in outs)


def _adamw_layer(w, g_own, g_sib, m, v, c, layer, prev, after, *, name):
    _, rows, cols = w.shape
    tr = _row_tile(rows // 2, cols)
    nbh = rows // 2 // tr
    c1 = 1.0 - ADAM_B1 ** ADAM_STEP
    c2 = 1.0 - ADAM_B2 ** ADAM_STEP
    n_prev = 0 if prev is None else 4

    def body(c_ref, w_ref, o_ref, s_ref, m_ref, v_ref, *rest):
        g_ref, d_ref, nm_ref, nv_ref = rest[n_prev + 1:]
        gv = jnp.where(pl.program_id(0) // nbh == c_ref[0], o_ref[...], s_ref[...])
        nm = ADAM_B1 * m_ref[0] + (1.0 - ADAM_B1) * gv
        nv = ADAM_B2 * v_ref[0] + (1.0 - ADAM_B2) * (gv * gv)
        g_ref[0] = gv
        nm_ref[0] = nm
        nv_ref[0] = nv
        d_ref[0] = -ADAM_LR * ((nm / c1) / (jnp.sqrt(nv / c2) + ADAM_EPS) + ADAM_WD * w_ref[0])

    def source(own):
        return pl.BlockSpec((tr, cols), lambda i, cc: (jnp.where((i // nbh == cc[0]) == own, i % nbh, 0), 0))

    spec = pl.BlockSpec((1, tr, cols), lambda i, cc: (layer, i, 0))
    gs = pltpu.PrefetchScalarGridSpec(
        num_scalar_prefetch=1, grid=(2 * nbh,),
        in_specs=[spec, source(True), source(False), spec, spec] + [ANY] * (n_prev + 1), out_specs=[spec] * 4)
    return pl.pallas_call(
        body, name=name, grid_spec=gs, out_shape=[jax.ShapeDtypeStruct(w.shape, F32)] * 4,
        input_output_aliases={6 + k: k for k in range(n_prev)},
        compiler_params=_cparams(("parallel",)))(
            jnp.reshape(c, (1,)).astype(I32), w, g_own, g_sib, m, v, *(prev or ()), after)


SMALL = (("mix_pre_g", (2, D)), ("mix_post_g", (2, D)), ("mlp_pre_g", (2, D)), ("mlp_post_g", (2, D)),
         ("gmlp_ln_g", (2, DM)), ("gmlp_ln_b", (2, DM)), ("gmlp_ws", (2, NG, CH, CH)), ("gmlp_bs", (2, NG, CH)),
         ("conv_b", (2, DM)), ("conv_ln_g", (2, DM)), ("conv_ln_b", (2, DM)), ("fox_bf", (2, NG)))


def _pack_rows(arrays):
    parts = []
    for a in arrays:
        last = a.shape[-1]
        r = a.astype(F32).reshape(-1, LANE) if last % LANE == 0 else jnp.pad(a.astype(F32).reshape(-1, last), ((0, 0), (0, LANE - last)))
        pad = (-r.shape[0]) % 8
        parts.append(jnp.pad(r, ((0, pad), (0, 0))) if pad else r)
    return jnp.concatenate(parts, axis=0)


def _unpack_rows(buf, shapes):
    out, off = [], 0
    for shp in shapes:
        size = 1
        for d in shp:
            size *= d
        last = shp[-1]
        rows = size // (LANE if last % LANE == 0 else last)
        seg = buf[off:off + rows]
        out.append(seg.reshape(shp) if last % LANE == 0 else seg[:, :last].reshape(shp))
        off += rows + (-rows) % 8
    return out


def _assemble_w_in(w_in_full):
    uv_glu_qkv = w_in_full[:, :3584]
    f = w_in_full[:, 3584:3592]
    gate = w_in_full[:, 3592:]
    fpad = jnp.zeros((D, D_INP - C_F - NG), w_in_full.dtype)
    return jnp.concatenate([gate, uv_glu_qkv, f, fpad], axis=1)


def _disassemble_w_in(g_p):
    return jnp.concatenate([g_p[:, C_UV:C_F], g_p[:, C_F:C_F + NG], g_p[:, :C_UV]], axis=1)


def kernel(x, c, ada_w, ada_b, mix_pre_g, mix_post_g, mlp_pre_g, mlp_post_g, w_in, gmlp_ln_g, gmlp_ln_b, gmlp_ws, gmlp_bs, w_a_out, conv_w, conv_b, conv_ln_g, conv_ln_b, w_b_out, fox_bf, w_c_out, w_out, mlp_w1, mlp_w2, loss_target, m_ada_w, m_ada_b, m_mix_pre_g, m_mix_post_g, m_mlp_pre_g, m_mlp_post_g, m_w_in, m_gmlp_ln_g, m_gmlp_ln_b, m_gmlp_ws, m_gmlp_bs, m_w_a_out, m_conv_w, m_conv_b, m_conv_ln_g, m_conv_ln_b, m_w_b_out, m_fox_bf, m_w_c_out, m_w_out, m_mlp_w1, m_mlp_w2, v_ada_w, v_ada_b, v_mix_pre_g, v_mix_post_g, v_mlp_pre_g, v_mlp_post_g, v_w_in, v_gmlp_ln_g, v_gmlp_ln_b, v_gmlp_ws, v_gmlp_bs, v_w_a_out, v_conv_w, v_conv_b, v_conv_ln_g, v_conv_ln_b, v_w_b_out, v_fox_bf, v_w_c_out, v_w_out, v_mlp_w1, v_mlp_w2):
    weights = dict(ada_w=ada_w, ada_b=ada_b, mix_pre_g=mix_pre_g, mix_post_g=mix_post_g, mlp_pre_g=mlp_pre_g,
                   mlp_post_g=mlp_post_g, w_in=w_in, gmlp_ln_g=gmlp_ln_g, gmlp_ln_b=gmlp_ln_b, gmlp_ws=gmlp_ws,
                   gmlp_bs=gmlp_bs, w_a_out=w_a_out, conv_w=conv_w, conv_b=conv_b, conv_ln_g=conv_ln_g,
                   conv_ln_b=conv_ln_b, w_b_out=w_b_out, fox_bf=fox_bf, w_c_out=w_c_out, w_out=w_out, mlp_w1=mlp_w1,
                   mlp_w2=mlp_w2)
    mom_m = dict(ada_w=m_ada_w, ada_b=m_ada_b, mix_pre_g=m_mix_pre_g, mix_post_g=m_mix_post_g, mlp_pre_g=m_mlp_pre_g,
                 mlp_post_g=m_mlp_post_g, w_in=m_w_in, gmlp_ln_g=m_gmlp_ln_g, gmlp_ln_b=m_gmlp_ln_b, gmlp_ws=m_gmlp_ws,
                 gmlp_bs=m_gmlp_bs, w_a_out=m_w_a_out, conv_w=m_conv_w, conv_b=m_conv_b, conv_ln_g=m_conv_ln_g,
                 conv_ln_b=m_conv_ln_b, w_b_out=m_w_b_out, fox_bf=m_fox_bf, w_c_out=m_w_c_out, w_out=m_w_out,
                 mlp_w1=m_mlp_w1, mlp_w2=m_mlp_w2)
    mom_v = dict(ada_w=v_ada_w, ada_b=v_ada_b, mix_pre_g=v_mix_pre_g, mix_post_g=v_mix_post_g, mlp_pre_g=v_mlp_pre_g,
                 mlp_post_g=v_mlp_post_g, w_in=v_w_in, gmlp_ln_g=v_gmlp_ln_g, gmlp_ln_b=v_gmlp_ln_b, gmlp_ws=v_gmlp_ws,
                 gmlp_bs=v_gmlp_bs, w_a_out=v_w_a_out, conv_w=v_conv_w, conv_b=v_conv_b, conv_ln_g=v_conv_ln_g,
                 conv_ln_b=v_conv_ln_b, w_b_out=v_w_b_out, fox_bf=v_fox_bf, w_c_out=v_w_c_out, w_out=v_w_out,
                 mlp_w1=v_mlp_w1, mlp_w2=v_mlp_w2)
    order = list(weights)
    px, py, pc = _position()
    chip = 2 * px + py
    dev = 2 * chip + pc
    depth = ada_w.shape[0]
    t = x.shape[1]
    xl = x.reshape(t, D)
    tgt = loss_target.reshape(t, D)

    big_names = [b[0] for b in BIG]
    shards = [[weights[n][l].astype(BF16) for n in big_names] for l in range(depth)]
    land_shapes = [(N_CHIPS,) + sh.shape for sh in shards[0]]

    small_in = _pack_rows([c, conv_w])
    gathered = _all_gather8(small_in)
    c_all = gathered[:, :D // LANE, :].reshape(N_DEV, D)
    cw_rows = depth * KW * LANE // LANE
    conv_w_full = jnp.concatenate(
        [gathered[2 * j, D // LANE:D // LANE + cw_rows, :].reshape(depth, KW, LANE) for j in range(N_CHIPS)], axis=2)

    ncol = ada_w.shape[2]
    ada_b_loc = lax.dynamic_slice_in_dim(ada_b, chip * ncol, ncol, axis=1).reshape(depth, 1, ncol)
    mod_sh = _ada_mod(c_all, ada_w, ada_b_loc)
    mod_g = _all_gather8(mod_sh.reshape(-1, LANE)).reshape(N_DEV, depth, N_DEV, ncol)
    mod_all = jnp.concatenate([mod_g[2 * j] for j in range(N_CHIPS)], axis=2)
    mod_mine = lax.dynamic_index_in_dim(mod_all, dev, axis=1, keepdims=False)

    mods = [mod_mine[l].reshape(NMOD, D) for l in range(depth)]

    def gathered_weights(l, idx, lands):
        lands = _ag_d2d(lands, name="ag_d2d")
        w = {}
        for i, g in zip(idx, lands):
            n, r, cdim, ax = BIG[i]
            g = lax.dynamic_update_slice(g, shards[l][i][None], (chip, 0, 0))
            if n == "w_in":
                w["w_in_p"] = _assemble_w_in(g.transpose(1, 0, 2).reshape(r, N_CHIPS * cdim))
            elif n == "mlp_w1":
                w[n] = g
            elif ax == 1:
                w[n] = g.transpose(1, 0, 2).reshape(r, N_CHIPS * cdim)
            else:
                w[n] = g.reshape(N_CHIPS * r, cdim)
        return w

    def local_weights(l):
        w = {}
        for n in ("mix_pre_g", "mix_post_g", "mlp_pre_g", "mlp_post_g", "gmlp_ln_g", "gmlp_ln_b", "conv_b", "conv_ln_g",
                  "conv_ln_b"):
            w[n] = weights[n][l:l + 1]
        tril = jnp.tril(jnp.ones((CH, CH), F32))
        wsm = gmlp_ws[l] * tril
        w["wsm"] = wsm.astype(BF16)
        w["wsmt"] = jnp.swapaxes(wsm, 1, 2).astype(BF16)
        w["bsx"] = jnp.repeat(gmlp_bs[l].T, HD, axis=1)
        w["conv_w"] = conv_w_full[l]
        w["bfp"] = jnp.pad(fox_bf[l], (0, LANE - NG)).reshape(1, LANE)
        return w

    def slabs(gfull, n, r, cdim, ax):
        if n == "mlp_w1":
            return gfull
        if ax == 1:
            return gfull.reshape(gfull.shape[0], N_CHIPS, cdim).transpose(1, 0, 2)
        return gfull.reshape(N_CHIPS, r, cdim)

    def swap_start(g, after, name):
        g["w_in"] = _disassemble_w_in(g.pop("w_in_p"))
        gs = [slabs(g[n], n, r, cdim, ax) for n, r, cdim, ax in BIG]
        return _ici_start("swap", gs, [(N_CHIPS, a.shape[1] // 2, a.shape[2]) for a in gs], after, name=name)

    def chip_sums(sw, after, name):
        from_sibling = _ici_wait("swap", sw[0], sw[1], sw[2], sw[3], after, name=name)
        return [_add_own_half(a, rv, pc, name="rs_add_half_" + n) for a, rv, n in zip(sw[2], from_sibling, big_names)]

    def reduce_rest(sums, from_chips):
        red = [_add_own_chip(sf, rv, chip, name="rs_add_chip_" + n) for (sf, _), rv, n in zip(sums, from_chips, big_names)]
        return red, _swap_reduced(red)

    assert depth == 2
    ga, gb = [0], list(range(1, len(BIG)))

    def pick(seq, idx):
        return [seq[i] for i in idx]

    ag0a = _ici_start("gather", pick(shards[0], ga), pick(land_shapes, ga), mod_mine, name="ag0a_start")
    lands0a = _ici_wait("gather", ag0a[0], ag0a[1], ag0a[2], ag0a[3], ag0a[4], name="ag0a_wait")
    ag0b = _ici_start("gather", pick(shards[0], gb), pick(land_shapes, gb), lands0a[0], name="ag0b_start")
    ag1 = _ici_start("gather", shards[1], land_shapes, ag0b[4], name="ag1_start")
    mod0 = mods[0] + ag1[4][0, 0]
    layers, saved = [None] * depth, [None] * depth

    def late0(att):
        return gathered_weights(0, gb, _ici_wait("gather", ag0b[0], ag0b[1], ag0b[2], ag0b[3], att, name="ag0b_wait"))

    xs, saved[0], layers[0] = _layer_fwd(xl, mod0, {**local_weights(0), **gathered_weights(0, ga, lands0a)}, late0)
    lands1 = _ici_wait("gather", ag1[0], ag1[1], ag1[2], ag1[3], xs, name="ag1_wait")
    xs, saved[1], layers[1] = _layer_fwd(xs, mods[1], {**local_weights(1), **gathered_weights(1, ga + gb, lands1)})
    loss_local, dx = _loss_and_grad(xs, tgt)
    loss = lax.psum(loss_local, ("x", "y", "c"))
    grads, dmods = [None] * depth, [None] * depth
    dx, grads[1], dmods[1] = _layer_bwd(dx, mods[1], layers[1], saved[1])
    sw1 = swap_start(grads[1], dx, "sw1_start")
    rs_l1 = {}

    def mid0(dx2):
        rs_l1["sums"] = chip_sums(sw1, dx2, "sw1_wait")
        sbf1 = [sb for _, sb in rs_l1["sums"]]
        rs_l1["rs"] = _ici_start("scatter", sbf1, [(3,) + sb.shape[1:] for sb in sbf1], dx2, name="rs1_start")
        return rs_l1["rs"][4][0:1, 0:1]

    dx, grads[0], dmods[0] = _layer_bwd(dx, mod0 + sw1[4][0, 0], layers[0], saved[0], mid0)
    grad_x = dx.reshape(x.shape)
    sums1, rs1 = rs_l1["sums"], rs_l1["rs"]
    slot_shapes = [(3,) + sb.shape[1:] for _, sb in sums1]
    red1 = reduce_rest(sums1, _ici_wait("scatter", rs1[0], rs1[1], rs1[2], rs1[3], dx, name="rs1_wait"))
    sw0 = swap_start(grads[0], red1[1][0], "sw0_start")
    g_out = {}

    small_names = [n for n, _ in SMALL]
    small_list = [jnp.stack(dmods)] + [jnp.stack([grads[l][n] for l in range(depth)]) for n in small_names]
    small_list.append(jnp.stack([grads[l]["conv_w"] for l in range(depth)]))
    small_shapes = [(depth, NMOD * D)] + [shp for _, shp in SMALL] + [(depth, KW, DM)]
    small_all = _all_gather8(_pack_rows(small_list) + sw0[4][0, 0])
    sums0 = chip_sums(sw0, small_all, "sw0_wait")
    rs0 = _ici_start("scatter", [sb for _, sb in sums0], slot_shapes, small_all, name="rs0_start")
    tok0 = rs0[4][0, 0]
    c_all = c_all + tok0
    small_sum = _unpack_rows(_sum8(small_all) + tok0, small_shapes)
    g_out["ada_b"] = small_sum[0]
    for n, gs in zip(small_names, small_sum[1:-1]):
        g_out[n] = gs
    g_out["conv_w"] = lax.dynamic_slice_in_dim(small_sum[-1], chip * LANE, LANE, axis=2)
    dmod_all = small_all[:, :depth * NMOD * D // LANE, :].reshape(N_DEV, depth, NMOD * D)
    dmod_loc = lax.dynamic_slice_in_dim(dmod_all, chip * ncol, ncol, axis=2).transpose(1, 0, 2)
    g_out["ada_w"] = _ada_grad(jnp.pad(c_all, ((0, 8), (0, 0))), jnp.pad(dmod_loc, ((0, 0), (0, 8), (0, 0))))

    delta, new_m, new_v = {}, {}, {}
    delta["ada_w"], new_m["ada_w"], new_v["ada_w"] = _adamw(ada_w, g_out["ada_w"], m_ada_w, v_ada_w, name="adamw_ada_w")
    ws_rows = (depth * NG * CH, CH)
    ws_out = _adamw(*[d["gmlp_ws"].reshape(ws_rows) for d in (weights, g_out, mom_m, mom_v)], name="adamw_gmlp_ws")
    delta["gmlp_ws"], new_m["gmlp_ws"], new_v["gmlp_ws"] = (a.reshape(gmlp_ws.shape) for a in ws_out)
    small_params = ["ada_b"] + [n for n in small_names if n != "gmlp_ws"] + ["conv_w"]
    packs = [_pack_rows([d[n] for n in small_params]) for d in (weights, g_out, mom_m, mom_v)]
    outs = _adamw(*packs, name="adamw_small")
    shapes = [weights[n].shape for n in small_params]
    for dst, buf in zip((delta, new_m, new_v), outs):
        for n, a in zip(small_params, _unpack_rows(buf, shapes)):
            dst[n] = a
    half = {n: _adamw_layer(weights[n], red1[0][wi], red1[1][wi], mom_m[n], mom_v[n], pc, 1, None, rs0[4],
                            name="adamw1_" + n) for wi, n in enumerate(big_names)}
    done = jnp.stack([delta["ada_w"][0, 0, 0], outs[0][0, 0]] + [half[n][1][1, 0, 0] for n in big_names])
    red0 = reduce_rest(sums0, _ici_wait("scatter", rs0[0], rs0[1], rs0[2], rs0[3], done, name="rs0_wait"))
    for wi, n in enumerate(big_names):
        g_out[n], delta[n], new_m[n], new_v[n] = _adamw_layer(
            weights[n], red0[0][wi], red0[1][wi], mom_m[n], mom_v[n], pc, 0, half[n], rs0[4], name="adamw0_" + n)

    return (loss, grad_x, *[g_out[n] for n in order], *[delta[n] for n in order], *[new_m[n] for n in order],
            *[new_v[n] for n in order])
```

```python
import functools

import jax
import jax.numpy as jnp
from jax import lax
from jax.experimental import pallas as pl
from jax.experimental.pallas import tpu as pltpu

F32 = jnp.float32
BF16 = jnp.bfloat16
I32 = jnp.int32
MESH = pl.DeviceIdType.MESH
ANY = pl.BlockSpec(memory_space=pl.ANY)

D = 1024
DM = 512
NG = 8
CH = 128
KW = 31
HALO = 32
DFF = 4096
NMOD = 6
EPS = 1e-6
LANE = 128
N_CHIPS = 4
N_DEV = 8
C_GATE, C_UV, C_GLU, C_Q, C_K, C_V, C_F, D_INP = 0, 3072, 4096, 5120, 5632, 6144, 6656, 7168
D_IN = 6664
VMEM_LIMIT = 56 * 1024 * 1024
TK_DEEP = 4096

ADAM_LR, ADAM_B1, ADAM_B2, ADAM_EPS, ADAM_WD, ADAM_STEP = 0.001, 0.9, 0.999, 1e-08, 0.01, 10

BIG = (("w_in", 1024, 1666, 1), ("w_a_out", 512, 256, 1), ("w_b_out", 512, 256, 1), ("w_c_out", 512, 256, 1),
       ("w_out", 256, 1024, 0), ("mlp_w1", 1024, 1024, 1), ("mlp_w2", 1024, 1024, 0))


def _cparams(sem):
    return pltpu.CompilerParams(dimension_semantics=sem, vmem_limit_bytes=VMEM_LIMIT)


def _sigmoid(x):
    return jax.nn.sigmoid(x)


_GELU_K = 0.7978845608028654
_GELU_A = 0.044715


def _gelu(x):
    t = jnp.tanh(_GELU_K * (x + _GELU_A * x * x * x))
    return 0.5 * x * (1.0 + t)


def _gelu_grad(x):
    t = jnp.tanh(_GELU_K * (x + _GELU_A * x * x * x))
    return 0.5 * (1.0 + t) + 0.5 * x * (1.0 - t * t) * _GELU_K * (1.0 + 3.0 * _GELU_A * x * x)


def _mean(x):
    return jnp.mean(x, axis=-1, keepdims=True)


def _colsum(x):
    return jnp.sum(x, axis=0, keepdims=True)


def _dot(a, b, dims=((1,), (0,))):
    return lax.dot_general(a, b, (dims, ((), ())), preferred_element_type=F32)


NN = ((1,), (0,))
NT = ((1,), (1,))
TN = ((0,), (0,))


def _matmul(a, b, *, name, ta=False, tb=False, out_dtype=F32, tm=1024, tn=1024, tk=1024, epilogue=None, extra=(),
            extra_out=(), b_slabs=False, out_slabs=0):
    m, k = (a.shape[1], a.shape[0]) if ta else a.shape
    if b_slabs:
        ns, brows, bw = b.shape
        n = brows if tb else ns * bw
        assert (ns * bw if tb else brows) == k, (name, b.shape, k)
        tn, tk = (tn, bw) if tb else (bw, tk)
    else:
        n = b.shape[0] if tb else b.shape[1]
    tm, tn, tk = min(tm, m), min(tn, n), min(tk, k)
    assert m % tm == 0 and n % tn == 0 and k % tk == 0, (name, m, n, k, tm, tn, tk)
    assert not out_slabs or (n // out_slabs == tn and epilogue is None), name
    nk = k // tk
    dims = ((0 if ta else 1,), (1 if tb else 0,))
    n_extra = len(extra)
    out_dtypes = (out_dtype,) + tuple(extra_out)

    def body(a_ref, b_ref, *rest):
        extra_refs = rest[:n_extra]
        out_refs = rest[n_extra:n_extra + len(out_dtypes)]
        kk = pl.program_id(2)
        part = _dot(a_ref[...].astype(BF16), b_ref[...].astype(BF16), dims)

        def finish(acc):
            outs = (acc,) if epilogue is None else epilogue(acc, *[r[...] for r in extra_refs])
            for o_ref, o in zip(out_refs, outs):
                o_ref[...] = o.astype(o_ref.dtype)

        if nk == 1:
            finish(part)
        else:
            acc_ref = rest[-1]

            @pl.when(kk == 0)
            def _():
                acc_ref[...] = part

            @pl.when(jnp.logical_and(kk > 0, kk < nk - 1))
            def _():
                acc_ref[...] += part

            @pl.when(kk == nk - 1)
            def _():
                finish(acc_ref[...] + part)

    a_spec = pl.BlockSpec((tk, tm), lambda i, j, kk: (kk, i)) if ta else pl.BlockSpec((tm, tk), lambda i, j, kk: (i, kk))
    if b_slabs and tb:
        b_spec = pl.BlockSpec((None, tn, tk), lambda i, j, kk: (kk, j, 0))
    elif b_slabs:
        b_spec = pl.BlockSpec((None, tk, tn), lambda i, j, kk: (j, kk, 0))
    else:
        b_spec = pl.BlockSpec((tn, tk), lambda i, j, kk: (j, kk)) if tb else pl.BlockSpec((tk, tn), lambda i, j, kk: (kk, j))
    if out_slabs:
        o_spec = pl.BlockSpec((None, tm, tn), lambda i, j, kk: (j, i, 0))
        o_shape = (out_slabs, m, tn)
    else:
        o_spec = pl.BlockSpec((tm, tn), lambda i, j, kk: (i, j))
        o_shape = (m, n)
    outs = pl.pallas_call(
        body, name=name, grid=(m // tm, n // tn, nk),
        in_specs=[a_spec, b_spec] + [o_spec] * n_extra,
        out_specs=[o_spec] * len(out_dtypes),
        out_shape=[jax.ShapeDtypeStruct(o_shape, dt) for dt in out_dtypes],
        scratch_shapes=[pltpu.VMEM((tm, tn), F32)] if nk > 1 else [],
        compiler_params=_cparams(("parallel", "parallel", "arbitrary")),
    )(a, b, *extra)
    return outs[0] if len(outs) == 1 else outs


def _rows(tm, n, col=0):
    return pl.BlockSpec((tm, n), lambda i: (i, col))


def _vec(n):
    return pl.BlockSpec((1, n), lambda i: (0, 0))


def _norm_mod(x, g, sc, sh, *, name, tm=256):
    t = x.shape[0]
    tm = min(tm, t)

    def body(x_ref, g_ref, sc_ref, sh_ref, h_ref):
        xv = x_ref[...]
        inv = lax.rsqrt(_mean(xv * xv) + EPS)
        h_ref[...] = ((xv * inv * g_ref[...]) * (1.0 + sc_ref[...]) + sh_ref[...]).astype(BF16)

    return pl.pallas_call(
        body, name=name, grid=(t // tm,), in_specs=[_rows(tm, D), _vec(D), _vec(D), _vec(D)],
        out_specs=_rows(tm, D), out_shape=jax.ShapeDtypeStruct((t, D), BF16),
        compiler_params=_cparams(("parallel",)))(x, g, sc, sh)


def _resid(x, y, gt, gp, *, name, tm=256):
    t = x.shape[0]
    tm = min(tm, t)

    def body(x_ref, y_ref, gt_ref, gp_ref, o_ref):
        yv = y_ref[...]
        inv = lax.rsqrt(_mean(yv * yv) + EPS)
        o_ref[...] = x_ref[...] + gt_ref[...] * (yv * inv * gp_ref[...])

    return pl.pallas_call(
        body, name=name, grid=(t // tm,), in_specs=[_rows(tm, D), _rows(tm, D), _vec(D), _vec(D)],
        out_specs=_rows(tm, D), out_shape=jax.ShapeDtypeStruct((t, D), F32),
        compiler_params=_cparams(("parallel",)))(x, y, gt, gp)


def _resid_bwd(dx, y, gt, gp, *, name, tm=256):
    t = dx.shape[0]
    tm = min(tm, t)

    def body(dx_ref, y_ref, gt_ref, gp_ref, dy_ref, dgt_ref, dgp_ref):
        @pl.when(pl.program_id(0) == 0)
        def _():
            dgt_ref[...] = jnp.zeros_like(dgt_ref)
            dgp_ref[...] = jnp.zeros_like(dgp_ref)

        dxv, yv, gp_v = dx_ref[...], y_ref[...], gp_ref[...]
        inv = lax.rsqrt(_mean(yv * yv) + EPS)
        yh = yv * inv
        dgt_ref[...] += _colsum(dxv * (yh * gp_v))
        dr = dxv * gt_ref[...]
        dgp_ref[...] += _colsum(dr * yh)
        dyn = dr * gp_v
        dy_ref[...] = (inv * (dyn - yh * _mean(dyn * yh))).astype(BF16)

    return pl.pallas_call(
        body, name=name, grid=(t // tm,), in_specs=[_rows(tm, D), _rows(tm, D), _vec(D), _vec(D)],
        out_specs=[_rows(tm, D), _vec(D), _vec(D)],
        out_shape=[jax.ShapeDtypeStruct((t, D), BF16), jax.ShapeDtypeStruct((1, D), F32),
                   jax.ShapeDtypeStruct((1, D), F32)],
        compiler_params=_cparams(("arbitrary",)))(dx, y, gt, gp)


def _norm_bwd(dh, dx_res, x, g, sc, *, name, tm=256):
    t = dh.shape[0]
    tm = min(tm, t)

    def body(dh_ref, dxr_ref, x_ref, g_ref, sc_ref, dx_ref, dg_ref, dsc_ref, dsh_ref):
        @pl.when(pl.program_id(0) == 0)
        def _():
            dg_ref[...] = jnp.zeros_like(dg_ref)
            dsc_ref[...] = jnp.zeros_like(dsc_ref)
            dsh_ref[...] = jnp.zeros_like(dsh_ref)

        dhv, xv, gv = dh_ref[...], x_ref[...], g_ref[...]
        inv = lax.rsqrt(_mean(xv * xv) + EPS)
        xh = xv * inv
        dsh_ref[...] += _colsum(dhv)
        dsc_ref[...] += _colsum(dhv * (xh * gv))
        dn = dhv * (1.0 + sc_ref[...])
        dg_ref[...] += _colsum(dn * xh)
        dxh = dn * gv
        dx_ref[...] = inv * (dxh - xh * _mean(dxh * xh)) + dxr_ref[...]

    vec_out = jax.ShapeDtypeStruct((1, D), F32)
    return pl.pallas_call(
        body, name=name, grid=(t // tm,), in_specs=[_rows(tm, D), _rows(tm, D), _rows(tm, D), _vec(D), _vec(D)],
        out_specs=[_rows(tm, D), _vec(D), _vec(D), _vec(D)],
        out_shape=[jax.ShapeDtypeStruct((t, D), F32), vec_out, vec_out, vec_out],
        compiler_params=_cparams(("arbitrary",)))(dh, dx_res, x, g, sc)


def _loss_and_grad(x, target, *, tm=256):
    t = x.shape[0]
    tm = min(tm, t)

    def body(x_ref, t_ref, loss_ref, dx_ref):
        @pl.when(pl.program_id(0) == 0)
        def _():
            loss_ref[...] = jnp.zeros_like(loss_ref)

        e = x_ref[...] - t_ref[...]
        dx_ref[...] = e * (1.0 / D)
        s = jnp.sum(jnp.sum(e * e, axis=1, keepdims=True), axis=0, keepdims=True) * (0.5 / D)
        loss_ref[...] += jnp.broadcast_to(s, loss_ref.shape)

    loss, dx = pl.pallas_call(
        body, name="loss", grid=(t // tm,), in_specs=[_rows(tm, D), _rows(tm, D)],
        out_specs=[pl.BlockSpec((8, LANE), lambda i: (0, 0)), _rows(tm, D)],
        out_shape=[jax.ShapeDtypeStruct((8, LANE), F32), jax.ShapeDtypeStruct((t, D), F32)],
        compiler_params=_cparams(("arbitrary",)))(x, target)
    return loss[0, 0], dx


def _gmlp_core(uv, lng, lnb, ws_ref, bsx):
    tm = uv.shape[0]
    gu = _gelu(uv[:, :DM])
    gv = _gelu(uv[:, DM:])
    mu = _mean(gv)
    vc = gv - mu
    rstd = lax.rsqrt(_mean(vc * vc) + EPS)
    vh = vc * rstd
    vln = vh * lng + lnb
    lane = lax.broadcasted_iota(I32, (CH, LANE), 1)
    sv_rows = []
    for nchunk in range(tm // CH):
        vb = vln[nchunk * CH:(nchunk + 1) * CH].astype(BF16)
        cols = []
        for cb in range(DM // LANE):
            vcb = vb[:, cb * LANE:(cb + 1) * LANE]
            lo = _dot(ws_ref[2 * cb], vcb)
            hi = _dot(ws_ref[2 * cb + 1], vcb)
            cols.append(jnp.where(lane < 64, lo, hi))
        sv_rows.append(jnp.concatenate(cols, axis=1) + bsx)
    sv = jnp.concatenate(sv_rows, axis=0) if len(sv_rows) > 1 else sv_rows[0]
    return gu, vh, rstd, vln, sv


def _gmlp_fwd(proj, lng, lnb, wsm, bsx, *, tm=256):
    t = proj.shape[0]
    tm = min(tm, t)

    def body(uv_ref, lng_ref, lnb_ref, ws_ref, bs_ref, ga_ref):
        gu, _, _, _, sv = _gmlp_core(uv_ref[...], lng_ref[...], lnb_ref[...], ws_ref, bs_ref[...])
        ga_ref[...] = (gu * sv).astype(BF16)

    return pl.pallas_call(
        body, name="gmlp_fwd", grid=(t // tm,),
        in_specs=[_rows(tm, 2 * DM, C_UV // (2 * DM)), _vec(DM), _vec(DM),
                  pl.BlockSpec((NG, CH, CH), lambda i: (0, 0, 0)), pl.BlockSpec((CH, DM), lambda i: (0, 0))],
        out_specs=_rows(tm, DM), out_shape=jax.ShapeDtypeStruct((t, DM), BF16),
        compiler_params=_cparams(("parallel",)))(proj, lng, lnb, wsm, bsx)


def _gmlp_bwd(dga, proj, lng, lnb, wsm, wsmt, bsx, *, tm=256):
    t = proj.shape[0]
    tm = min(tm, t)

    def body(dga_ref, uv_ref, lng_ref, lnb_ref, ws_ref, wst_ref, bs_ref, duv_ref, dws_ref, dbs_ref, dlng_ref, dlnb_ref,
             dbsx_ref):
        i = pl.program_id(0)

        @pl.when(i == 0)
        def _():
            dws_ref[...] = jnp.zeros_like(dws_ref)
            dbsx_ref[...] = jnp.zeros_like(dbsx_ref)
            dlng_ref[...] = jnp.zeros_like(dlng_ref)
            dlnb_ref[...] = jnp.zeros_like(dlnb_ref)

        uv = uv_ref[...]
        lng_v = lng_ref[...]
        gu, vh, rstd, vln, sv = _gmlp_core(uv, lng_v, lnb_ref[...], ws_ref, bs_ref[...])
        dga_v = dga_ref[...]
        dgu = dga_v * sv
        dsv = dga_v * gu
        lane = lax.broadcasted_iota(I32, (CH, LANE), 1)
        tril = lax.broadcasted_iota(I32, (CH, CH), 0) >= lax.broadcasted_iota(I32, (CH, CH), 1)
        dvln_rows = []
        for nchunk in range(tm // CH):
            rows = slice(nchunk * CH, (nchunk + 1) * CH)
            dbsx_ref[...] += dsv[rows]
            vb = vln[rows].astype(BF16)
            cols = []
            for cb in range(DM // LANE):
                cs = slice(cb * LANE, (cb + 1) * LANE)
                dsvb = dsv[rows, cs]
                vcb = vb[:, cs]
                dlo = jnp.where(lane < 64, dsvb, 0.0).astype(BF16)
                dhi = jnp.where(lane < 64, 0.0, dsvb).astype(BF16)
                dws_ref[2 * cb] += jnp.where(tril, _dot(dlo, vcb, NT), 0.0)
                dws_ref[2 * cb + 1] += jnp.where(tril, _dot(dhi, vcb, NT), 0.0)
                dsb = dsvb.astype(BF16)
                cols.append(jnp.where(lane < 64, _dot(wst_ref[2 * cb], dsb), _dot(wst_ref[2 * cb + 1], dsb)))
            dvln_rows.append(jnp.concatenate(cols, axis=1))
        dvln = jnp.concatenate(dvln_rows, axis=0) if len(dvln_rows) > 1 else dvln_rows[0]
        dlnb_ref[...] += _colsum(dvln)
        dlng_ref[...] += _colsum(dvln * vh)
        dvh = dvln * lng_v
        dgv = rstd * (dvh - _mean(dvh) - vh * _mean(dvh * vh))
        duv_ref[:, :DM] = (dgu * _gelu_grad(uv[:, :DM])).astype(BF16)
        duv_ref[:, DM:] = (dgv * _gelu_grad(uv[:, DM:])).astype(BF16)

        @pl.when(i == pl.num_programs(0) - 1)
        def _():
            ind = (lax.broadcasted_iota(I32, (DM, LANE), 0) // 64 == lax.broadcasted_iota(I32, (DM, LANE), 1)).astype(F32)
            dbs_ref[...] = jnp.dot(dbsx_ref[...], ind, preferred_element_type=F32, precision=lax.Precision.HIGHEST)

    vec_out = jax.ShapeDtypeStruct((1, DM), F32)
    outs = pl.pallas_call(
        body, name="gmlp_bwd", grid=(t // tm,),
        in_specs=[_rows(tm, DM), _rows(tm, 2 * DM, C_UV // (2 * DM)), _vec(DM), _vec(DM),
                  pl.BlockSpec((NG, CH, CH), lambda i: (0, 0, 0)), pl.BlockSpec((NG, CH, CH), lambda i: (0, 0, 0)),
                  pl.BlockSpec((CH, DM), lambda i: (0, 0))],
        out_specs=[_rows(tm, 2 * DM), pl.BlockSpec((NG, CH, CH), lambda i: (0, 0, 0)),
                   pl.BlockSpec((CH, LANE), lambda i: (0, 0)), _vec(DM), _vec(DM)],
        out_shape=[jax.ShapeDtypeStruct((t, 2 * DM), BF16), jax.ShapeDtypeStruct((NG, CH, CH), F32),
                   jax.ShapeDtypeStruct((CH, LANE), F32), vec_out, vec_out],
        scratch_shapes=[pltpu.VMEM((CH, DM), F32)],
        compiler_params=_cparams(("arbitrary",)))(dga, proj, lng, lnb, wsm, wsmt, bsx)
    return outs


def _glu_into(zs_ref, glu_ref, halo_ref, first):
    hal = halo_ref[...]
    z0h = hal[:, :DM] * _sigmoid(hal[:, DM:])
    zs_ref[0:HALO, :] = jnp.where(first, 0.0, z0h)
    g = glu_ref[...]
    zs_ref[HALO:, :] = g[:, :DM] * _sigmoid(g[:, DM:])


SUB = 8


def _shifted_copies(dst_ref, src_ref):
    n = src_ref.shape[0]
    for s in range(SUB):
        dst_ref[s, 0:n - s, :] = src_ref[s:n, :]


def _window(shifted_ref, off, rows, cs):
    s = off % SUB
    return shifted_ref[s, off - s:off - s + rows, cs]


def _conv_fwd(proj, cw, cb, lng, lnb, *, tm=256, rb=64):
    t = proj.shape[0]
    tm = min(tm, t)
    hb = tm // HALO
    gcol = C_GLU // (2 * DM)

    def body(glu_ref, halo_ref, cw_ref, cb_ref, lng_ref, lnb_ref, zc_ref, zb_ref, zs_ref, zsh_ref):
        i = pl.program_id(0)
        _glu_into(zs_ref, glu_ref, halo_ref, i == 0)
        _shifted_copies(zsh_ref, zs_ref)
        for cbk in range(DM // LANE):
            cs = slice(cbk * LANE, (cbk + 1) * LANE)
            for r in range(tm // rb):
                acc = jnp.broadcast_to(cb_ref[:, cs], (rb, LANE))
                for k in range(KW):
                    acc = acc + cw_ref[k:k + 1, cs] * _window(zsh_ref, r * rb + HALO - (KW - 1) + k, rb, cs)
                zc_ref[r * rb:(r + 1) * rb, cs] = acc
        zc = zc_ref[...]
        mu = _mean(zc)
        zcc = zc - mu
        zh = zcc * lax.rsqrt(_mean(zcc * zcc) + EPS)
        a = zh * lng_ref[...] + lnb_ref[...]
        zb_ref[...] = (a * _sigmoid(a)).astype(BF16)

    return pl.pallas_call(
        body, name="conv_fwd", grid=(t // tm,),
        in_specs=[_rows(tm, 2 * DM, gcol),
                  pl.BlockSpec((HALO, 2 * DM), lambda i: (jnp.maximum(i * hb - 1, 0), gcol)),
                  pl.BlockSpec((KW, DM), lambda i: (0, 0)), _vec(DM), _vec(DM), _vec(DM)],
        out_specs=[_rows(tm, DM), _rows(tm, DM)],
        out_shape=[jax.ShapeDtypeStruct((t, DM), F32), jax.ShapeDtypeStruct((t, DM), BF16)],
        scratch_shapes=[pltpu.VMEM((HALO + tm, DM), F32), pltpu.VMEM((SUB, HALO + tm, DM), F32)],
        compiler_params=_cparams(("parallel",)))(proj, proj, cw, cb, lng, lnb)


def _conv_bwd_ln(dzb, zc, lng, lnb, *, tm=256):
    t = zc.shape[0]
    tm = min(tm, t)

    def body(dzb_ref, zc_ref, lng_ref, lnb_ref, dzc_ref, dlng_ref, dlnb_ref):
        @pl.when(pl.program_id(0) == 0)
        def _():
            dlng_ref[...] = jnp.zeros_like(dlng_ref)
            dlnb_ref[...] = jnp.zeros_like(dlnb_ref)

        zc = zc_ref[...]
        lng_v = lng_ref[...]
        mu = _mean(zc)
        zcc = zc - mu
        rstd = lax.rsqrt(_mean(zcc * zcc) + EPS)
        zh = zcc * rstd
        a = zh * lng_v + lnb_ref[...]
        s = _sigmoid(a)
        da = dzb_ref[...] * (s * (1.0 + a * (1.0 - s)))
        dlnb_ref[...] += _colsum(da)
        dlng_ref[...] += _colsum(da * zh)
        dzh = da * lng_v
        dzc_ref[...] = rstd * (dzh - _mean(dzh) - zh * _mean(dzh * zh))

    vec_out = jax.ShapeDtypeStruct((1, DM), F32)
    return pl.pallas_call(
        body, name="conv_bwd_ln", grid=(t // tm,), in_specs=[_rows(tm, DM), _rows(tm, DM), _vec(DM), _vec(DM)],
        out_specs=[_rows(tm, DM), _vec(DM), _vec(DM)],
        out_shape=[jax.ShapeDtypeStruct((t, DM), F32), vec_out, vec_out],
        compiler_params=_cparams(("arbitrary",)))(dzb, zc, lng, lnb)


def _conv_bwd(dzc, proj, cw, *, tm=256, rb=64):
    t = proj.shape[0]
    tm = min(tm, t)
    hb = tm // HALO
    nblk = t // tm
    gcol = C_GLU // (2 * DM)

    def body(dzc_ref, dnext_ref, glu_ref, halo_ref, cw_ref, dglu_ref, dcw_ref, dcb_ref, zs_ref, ds_ref, zsh_ref, dsh_ref):
        i = pl.program_id(0)

        @pl.when(i == 0)
        def _():
            dcw_ref[...] = jnp.zeros_like(dcw_ref)
            dcb_ref[...] = jnp.zeros_like(dcb_ref)

        _glu_into(zs_ref, glu_ref, halo_ref, i == 0)
        _shifted_copies(zsh_ref, zs_ref)
        dzc = dzc_ref[...]
        ds_ref[0:tm, :] = dzc
        ds_ref[tm:, :] = jnp.where(i == nblk - 1, 0.0, dnext_ref[...])
        _shifted_copies(dsh_ref, ds_ref)
        dcb_ref[...] += _colsum(dzc)
        for k in range(KW):
            dcw_ref[k:k + 1, :] += _colsum(dzc * _window(zsh_ref, HALO - (KW - 1) + k, tm, slice(None)))
        g = glu_ref[...]
        val, sg = g[:, :DM], _sigmoid(g[:, DM:])
        for cbk in range(DM // LANE):
            cs = slice(cbk * LANE, (cbk + 1) * LANE)
            for r in range(tm // rb):
                acc = jnp.zeros((rb, LANE), F32)
                for k in range(KW):
                    acc = acc + cw_ref[k:k + 1, cs] * _window(dsh_ref, r * rb + (KW - 1) - k, rb, cs)
                rs = slice(r * rb, (r + 1) * rb)
                dglu_ref[rs, cs] = (acc * sg[rs, cs]).astype(BF16)
                v, s = val[rs, cs], sg[rs, cs]
                dglu_ref[rs, DM + cbk * LANE:DM + (cbk + 1) * LANE] = (acc * v * s * (1.0 - s)).astype(BF16)

    return pl.pallas_call(
        body, name="conv_bwd", grid=(nblk,),
        in_specs=[_rows(tm, DM),
                  pl.BlockSpec((HALO, DM), lambda i: (jnp.minimum((i + 1) * hb, nblk * hb - 1), 0)),
                  _rows(tm, 2 * DM, gcol),
                  pl.BlockSpec((HALO, 2 * DM), lambda i: (jnp.maximum(i * hb - 1, 0), gcol)),
                  pl.BlockSpec((KW, DM), lambda i: (0, 0))],
        out_specs=[_rows(tm, 2 * DM), pl.BlockSpec((HALO, DM), lambda i: (0, 0)), _vec(DM)],
        out_shape=[jax.ShapeDtypeStruct((t, 2 * DM), BF16), jax.ShapeDtypeStruct((HALO, DM), F32),
                   jax.ShapeDtypeStruct((1, DM), F32)],
        scratch_shapes=[pltpu.VMEM((HALO + tm, DM), F32), pltpu.VMEM((tm + HALO, DM), F32),
                        pltpu.VMEM((SUB, HALO + tm, DM), F32), pltpu.VMEM((SUB, tm + HALO, DM), F32)],
        compiler_params=_cparams(("arbitrary",)))(dzc, dzc, proj, proj, cw)


CUM_ROWS = 512


def _log_sigmoid(x):
    return jnp.minimum(x, 0.0) - jnp.log1p(jnp.exp(-jnp.abs(x)))


def _fox_cum(proj, bfp):
    t = proj.shape[0]
    fcol = C_F // LANE
    cr = min(CUM_ROWS, t)

    def body(f_ref, bf_ref, cum_ref, carry_ref):
        @pl.when(pl.program_id(0) == 0)
        def _():
            carry_ref[...] = jnp.zeros_like(carry_ref)

        lf = _log_sigmoid(f_ref[...] + bf_ref[...])
        tri = (lax.broadcasted_iota(I32, (cr, cr), 0) >= lax.broadcasted_iota(I32, (cr, cr), 1)).astype(F32)
        cum = jnp.dot(tri, lf, preferred_element_type=F32, precision=lax.Precision.HIGHEST) + carry_ref[0:1, :]
        cum_ref[...] = cum
        carry_ref[...] = jnp.broadcast_to(cum[cr - 1:cr, :], carry_ref.shape)

    return pl.pallas_call(
        body, name="fox_cum", grid=(t // cr,), in_specs=[_rows(cr, LANE, fcol), _vec(LANE)],
        out_specs=_rows(cr, LANE), out_shape=jax.ShapeDtypeStruct((t, LANE), F32),
        scratch_shapes=[pltpu.VMEM((8, LANE), F32)],
        compiler_params=_cparams(("arbitrary",)))(proj, bfp)


def _fox_cum_bwd(dcum, proj, bfp):
    t = proj.shape[0]
    cr = min(CUM_ROWS, t)
    nb = t // cr
    fcol = C_F // LANE
    fw = D_INP - C_F

    def body(dc_ref, f_ref, bf_ref, df_ref, dbf_ref, carry_ref):
        @pl.when(pl.program_id(0) == 0)
        def _():
            carry_ref[...] = jnp.zeros_like(carry_ref)
            dbf_ref[...] = jnp.zeros_like(dbf_ref)

        triu = (lax.broadcasted_iota(I32, (cr, cr), 0) <= lax.broadcasted_iota(I32, (cr, cr), 1)).astype(F32)
        dlf = jnp.dot(triu, dc_ref[...], preferred_element_type=F32, precision=lax.Precision.HIGHEST) + carry_ref[0:1, :]
        carry_ref[...] = jnp.broadcast_to(dlf[0:1, :], carry_ref.shape)
        z = f_ref[...] + bf_ref[...]
        lane = lax.broadcasted_iota(I32, (cr, LANE), 1)
        df = jnp.where(lane < NG, dlf * _sigmoid(-z), 0.0)
        dbf_ref[...] += _colsum(df)
        df_ref[:, 0:LANE] = df.astype(BF16)
        df_ref[:, LANE:] = jnp.zeros((cr, fw - LANE), BF16)

    return pl.pallas_call(
        body, name="fox_cum_bwd", grid=(nb,),
        in_specs=[pl.BlockSpec((cr, LANE), lambda i: (nb - 1 - i, 0)),
                  pl.BlockSpec((cr, LANE), lambda i: (nb - 1 - i, fcol)), _vec(LANE)],
        out_specs=[pl.BlockSpec((cr, fw), lambda i: (nb - 1 - i, 0)), _vec(LANE)],
        out_shape=[jax.ShapeDtypeStruct((t, fw), BF16), jax.ShapeDtypeStruct((1, LANE), F32)],
        scratch_shapes=[pltpu.VMEM((8, LANE), F32)],
        compiler_params=_cparams(("arbitrary",)))(dcum, proj, bfp)


HD = 64
ATT_SCALE = 0.125
NEG = -1e30


def _qkv_prep(proj, *, tm=512):
    t = proj.shape[0]
    tm = min(tm, t)

    def body(q_ref, k_ref, v_ref, o_ref):
        o_ref[:, 0:DM] = (q_ref[...] * ATT_SCALE).astype(BF16)
        o_ref[:, DM:2 * DM] = k_ref[...].astype(BF16)
        o_ref[:, 2 * DM:] = v_ref[...].astype(BF16)

    return pl.pallas_call(
        body, name="qkv_prep", grid=(t // tm,),
        in_specs=[_rows(tm, DM, C_Q // DM), _rows(tm, DM, C_K // DM), _rows(tm, DM, C_V // DM)],
        out_specs=_rows(tm, 3 * DM), out_shape=jax.ShapeDtypeStruct((t, 3 * DM), BF16),
        compiler_params=_cparams(("parallel",)))(proj, proj, proj)


def _causal_pairs(nq, outer_is_query):
    if outer_is_query:
        pairs = [(i, j) for i in range(nq) for j in range(i + 1)]
    else:
        pairs = [(j, i) for j in range(nq) for i in range(j, nq)]
    return (jnp.asarray([p[0] for p in pairs], I32), jnp.asarray([p[1] for p in pairs], I32))


def _to_row(col):
    return jnp.transpose(col)[0:1, :]


def _rep(x, tk):
    return x if tk == LANE else jnp.tile(x, (1, tk // LANE))


def _attn_fwd(qkv, ckrow, *, tq=512):
    t = qkv.shape[0]
    tq = min(tq, t)
    tk = tq
    nq = t // tq
    oi, ij = _causal_pairs(nq, True)

    def body(oi_ref, ij_ref, q_ref, k_ref, v_ref, ck_ref, o_ref, lse_ref, lser_ref, m_ref, l_ref, acc_ref):
        n = pl.program_id(0)
        i, j = oi_ref[n], ij_ref[n]

        @pl.when(j == 0)
        def _():
            m_ref[...] = jnp.full_like(m_ref, NEG)
            l_ref[...] = jnp.zeros_like(l_ref)
            acc_ref[...] = jnp.zeros_like(acc_ref)

        def step(masked):
            if masked:
                keep = lax.broadcasted_iota(I32, (tq, tk), 1) <= lax.broadcasted_iota(I32, (tq, tk), 0)
            lo = lax.broadcasted_iota(I32, (tq, LANE), 1) < HD
            for hp in range(NG // 2):
                cs = slice(hp * LANE, (hp + 1) * LANE)
                qp, kp, vp = q_ref[:, cs], k_ref[:, cs], v_ref[:, cs]
                alphas, pvs = [], []
                for hh in range(2):
                    h = 2 * hp + hh
                    qm = jnp.where(lo if hh == 0 else jnp.logical_not(lo), qp, jnp.zeros_like(qp))
                    s = _dot(qm, kp, NT) - ck_ref[h:h + 1, :]
                    if masked:
                        s = jnp.where(keep, s, NEG)
                    m_prev = m_ref[h]
                    m_new = jnp.maximum(m_prev, jnp.max(s, axis=1, keepdims=True))
                    alpha = jnp.exp(m_prev - m_new)
                    p = jnp.exp(s - _rep(m_new, tk))
                    l_ref[h] = alpha * l_ref[h] + jnp.sum(p, axis=1, keepdims=True)
                    m_ref[h] = m_new
                    alphas.append(alpha)
                    pvs.append(_dot(p.astype(BF16), vp))
                acc_ref[:, cs] = jnp.where(lo, alphas[0], alphas[1]) * acc_ref[:, cs] + jnp.where(lo, pvs[0], pvs[1])

        @pl.when(j < i)
        def _():
            step(False)

        @pl.when(j == i)
        def _():
            step(True)
            lo = lax.broadcasted_iota(I32, (tq, LANE), 1) < HD
            for hp in range(NG // 2):
                cs = slice(hp * LANE, (hp + 1) * LANE)
                o_ref[:, cs] = (acc_ref[:, cs] / jnp.where(lo, l_ref[2 * hp], l_ref[2 * hp + 1])).astype(BF16)
            for h in range(NG):
                lse = m_ref[h] + jnp.log(l_ref[h])
                lse_ref[h] = lse
                lser_ref[h:h + 1, :] = _to_row(lse)

    gs = pltpu.PrefetchScalarGridSpec(
        num_scalar_prefetch=2, grid=(int(oi.shape[0]),),
        in_specs=[pl.BlockSpec((tq, DM), lambda n, a, b: (a[n], 0)),
                  pl.BlockSpec((tk, DM), lambda n, a, b: (b[n], 1)),
                  pl.BlockSpec((tk, DM), lambda n, a, b: (b[n], 2)),
                  pl.BlockSpec((NG, tk), lambda n, a, b: (0, b[n]))],
        out_specs=[pl.BlockSpec((tq, DM), lambda n, a, b: (a[n], 0)),
                   pl.BlockSpec((NG, tq, LANE), lambda n, a, b: (0, a[n], 0)),
                   pl.BlockSpec((NG, tq), lambda n, a, b: (0, a[n]))],
        scratch_shapes=[pltpu.VMEM((NG, tq, LANE), F32), pltpu.VMEM((NG, tq, LANE), F32), pltpu.VMEM((tq, DM), F32)])
    return pl.pallas_call(
        body, name="attn_fwd", grid_spec=gs,
        out_shape=[jax.ShapeDtypeStruct((t, DM), BF16), jax.ShapeDtypeStruct((NG, t, LANE), F32),
                   jax.ShapeDtypeStruct((NG, t), F32)],
        compiler_params=_cparams(("arbitrary",)))(oi, ij, qkv, qkv, qkv, ckrow)


def _attn_bwd_dq(qkv, ckrow, o, do, lse, *, tq=512):
    t = qkv.shape[0]
    tq = min(tq, t)
    tk = tq
    nq = t // tq
    oi, ij = _causal_pairs(nq, True)

    def body(oi_ref, ij_ref, q_ref, k_ref, v_ref, ck_ref, o_ref, do_ref, lse_ref, dq_ref, deltar_ref, dcqr_ref, delta_ref,
             dcq_ref, acc_ref):
        n = pl.program_id(0)
        i, j = oi_ref[n], ij_ref[n]

        @pl.when(j == 0)
        def _():
            acc_ref[...] = jnp.zeros_like(acc_ref)
            dcq_ref[...] = jnp.zeros_like(dcq_ref)
            lo = lax.broadcasted_iota(I32, (tq, LANE), 1) < HD
            for hp in range(NG // 2):
                cs = slice(hp * LANE, (hp + 1) * LANE)
                prod = do_ref[:, cs] * o_ref[:, cs].astype(F32)
                for hh in range(2):
                    d = jnp.sum(jnp.where(lo if hh == 0 else jnp.logical_not(lo), prod, 0.0), axis=1, keepdims=True)
                    dcol = jnp.broadcast_to(d, (tq, LANE))
                    delta_ref[2 * hp + hh] = dcol
                    deltar_ref[2 * hp + hh:2 * hp + hh + 1, :] = _to_row(dcol)

        def step(masked):
            if masked:
                keep = lax.broadcasted_iota(I32, (tq, tk), 1) <= lax.broadcasted_iota(I32, (tq, tk), 0)
            lo = lax.broadcasted_iota(I32, (tq, LANE), 1) < HD
            for hp in range(NG // 2):
                cs = slice(hp * LANE, (hp + 1) * LANE)
                qp, kp, vp, dop = q_ref[:, cs], k_ref[:, cs], v_ref[:, cs], do_ref[:, cs].astype(BF16)
                parts = []
                for hh in range(2):
                    h = 2 * hp + hh
                    sel = lo if hh == 0 else jnp.logical_not(lo)
                    s = _dot(jnp.where(sel, qp, jnp.zeros_like(qp)), kp, NT) - ck_ref[h:h + 1, :]
                    if masked:
                        s = jnp.where(keep, s, NEG)
                    p = jnp.exp(s - _rep(lse_ref[h], tk))
                    dp = _dot(jnp.where(sel, dop, jnp.zeros_like(dop)), vp, NT)
                    ds = p * (dp - _rep(delta_ref[h], tk))
                    dcq_ref[h] += jnp.sum(ds, axis=1, keepdims=True)
                    parts.append(_dot(ds.astype(BF16), kp))
                acc_ref[:, cs] += jnp.where(lo, parts[0], parts[1])

        @pl.when(j < i)
        def _():
            step(False)

        @pl.when(j == i)
        def _():
            step(True)
            dq_ref[...] = (acc_ref[...] * ATT_SCALE).astype(BF16)
            for h in range(NG):
                dcqr_ref[h:h + 1, :] = _to_row(dcq_ref[h])

    gs = pltpu.PrefetchScalarGridSpec(
        num_scalar_prefetch=2, grid=(int(oi.shape[0]),),
        in_specs=[pl.BlockSpec((tq, DM), lambda n, a, b: (a[n], 0)),
                  pl.BlockSpec((tk, DM), lambda n, a, b: (b[n], 1)),
                  pl.BlockSpec((tk, DM), lambda n, a, b: (b[n], 2)),
                  pl.BlockSpec((NG, tk), lambda n, a, b: (0, b[n])),
                  pl.BlockSpec((tq, DM), lambda n, a, b: (a[n], 0)),
                  pl.BlockSpec((tq, DM), lambda n, a, b: (a[n], 0)),
                  pl.BlockSpec((NG, tq, LANE), lambda n, a, b: (0, a[n], 0))],
        out_specs=[pl.BlockSpec((tq, DM), lambda n, a, b: (a[n], 0)),
                   pl.BlockSpec((NG, tq), lambda n, a, b: (0, a[n])),
                   pl.BlockSpec((NG, tq), lambda n, a, b: (0, a[n]))],
        scratch_shapes=[pltpu.VMEM((NG, tq, LANE), F32), pltpu.VMEM((NG, tq, LANE), F32), pltpu.VMEM((tq, DM), F32)])
    return pl.pallas_call(
        body, name="attn_bwd_dq", grid_spec=gs,
        out_shape=[jax.ShapeDtypeStruct((t, DM), BF16), jax.ShapeDtypeStruct((NG, t), F32), jax.ShapeDtypeStruct((NG, t), F32)],
        compiler_params=_cparams(("arbitrary",)))(oi, ij, qkv, qkv, qkv, ckrow, o, do, lse)


def _attn_bwd_dkv(qkv, ckcol, do, lserow, deltarow, *, tq=512):
    t = qkv.shape[0]
    tq = min(tq, t)
    tk = tq
    nq = t // tq
    oj, ii = _causal_pairs(nq, False)

    def body(oj_ref, ii_ref, q_ref, k_ref, v_ref, ck_ref, do_ref, lse_ref, delta_ref, dk_ref, dv_ref, dckr_ref, dka_ref,
             dva_ref, dck_ref):
        n = pl.program_id(0)
        j, i = oj_ref[n], ii_ref[n]

        @pl.when(i == j)
        def _():
            dka_ref[...] = jnp.zeros_like(dka_ref)
            dva_ref[...] = jnp.zeros_like(dva_ref)
            dck_ref[...] = jnp.zeros_like(dck_ref)

        def step(masked):
            if masked:
                keep = lax.broadcasted_iota(I32, (tk, tq), 0) <= lax.broadcasted_iota(I32, (tk, tq), 1)
            lo = lax.broadcasted_iota(I32, (tk, LANE), 1) < HD
            for hp in range(NG // 2):
                cs = slice(hp * LANE, (hp + 1) * LANE)
                qp, kp, vp, dop = q_ref[:, cs], k_ref[:, cs], v_ref[:, cs], do_ref[:, cs].astype(BF16)
                dvs, dks = [], []
                for hh in range(2):
                    h = 2 * hp + hh
                    sel = lo if hh == 0 else jnp.logical_not(lo)
                    st = _dot(jnp.where(sel, kp, jnp.zeros_like(kp)), qp, NT) - _rep(ck_ref[h], tq)
                    if masked:
                        st = jnp.where(keep, st, NEG)
                    pt = jnp.exp(st - lse_ref[h:h + 1, :])
                    dvs.append(_dot(pt.astype(BF16), dop))
                    dpt = _dot(jnp.where(sel, vp, jnp.zeros_like(vp)), dop, NT)
                    dst = pt * (dpt - delta_ref[h:h + 1, :])
                    dks.append(_dot(dst.astype(BF16), qp))
                    dck_ref[h] -= jnp.sum(dst, axis=1, keepdims=True)
                dva_ref[:, cs] += jnp.where(lo, dvs[0], dvs[1])
                dka_ref[:, cs] += jnp.where(lo, dks[0], dks[1])

        @pl.when(i == j)
        def _():
            step(True)

        @pl.when(i > j)
        def _():
            step(False)

        @pl.when(i == nq - 1)
        def _():
            dk_ref[...] = dka_ref[...].astype(BF16)
            dv_ref[...] = dva_ref[...].astype(BF16)
            for h in range(NG):
                dckr_ref[h:h + 1, :] = _to_row(dck_ref[h])

    gs = pltpu.PrefetchScalarGridSpec(
        num_scalar_prefetch=2, grid=(int(oj.shape[0]),),
        in_specs=[pl.BlockSpec((tq, DM), lambda n, a, b: (b[n], 0)),
                  pl.BlockSpec((tk, DM), lambda n, a, b: (a[n], 1)),
                  pl.BlockSpec((tk, DM), lambda n, a, b: (a[n], 2)),
                  pl.BlockSpec((NG, tk, LANE), lambda n, a, b: (0, a[n], 0)),
                  pl.BlockSpec((tq, DM), lambda n, a, b: (b[n], 0)),
                  pl.BlockSpec((NG, tq), lambda n, a, b: (0, b[n])),
                  pl.BlockSpec((NG, tq), lambda n, a, b: (0, b[n]))],
        out_specs=[pl.BlockSpec((tk, DM), lambda n, a, b: (a[n], 0)),
                   pl.BlockSpec((tk, DM), lambda n, a, b: (a[n], 0)),
                   pl.BlockSpec((NG, tk), lambda n, a, b: (0, a[n]))],
        scratch_shapes=[pltpu.VMEM((tk, DM), F32), pltpu.VMEM((tk, DM), F32), pltpu.VMEM((NG, tk, LANE), F32)])
    return pl.pallas_call(
        body, name="attn_bwd_dkv", grid_spec=gs,
        out_shape=[jax.ShapeDtypeStruct((t, DM), BF16), jax.ShapeDtypeStruct((t, DM), BF16), jax.ShapeDtypeStruct((NG, t), F32)],
        compiler_params=_cparams(("arbitrary",)))(oj, ii, qkv, qkv, qkv, ckcol, do, lserow, deltarow)


def _merge_fwd(ga, zb, att, proj, wa, wb, wc, *, tm=256):
    t = ga.shape[0]
    tm = min(tm, t)
    wspec = pl.BlockSpec((DM, D), lambda i: (0, 0))

    def body(ga_ref, zb_ref, att_ref, gate_ref, wa_ref, wb_ref, wc_ref, m_ref):
        acc = jnp.zeros((tm, D), F32)
        for b, (x_ref, w_ref) in enumerate(((ga_ref, wa_ref), (zb_ref, wb_ref), (att_ref, wc_ref))):
            acc = acc + _sigmoid(gate_ref[:, b * D:(b + 1) * D]) * _dot(x_ref[...], w_ref[...])
        m_ref[...] = acc.astype(BF16)

    return pl.pallas_call(
        body, name="merge_fwd", grid=(t // tm,),
        in_specs=[_rows(tm, DM), _rows(tm, DM), _rows(tm, DM), _rows(tm, 3 * D, 0), wspec, wspec, wspec],
        out_specs=_rows(tm, D), out_shape=jax.ShapeDtypeStruct((t, D), BF16),
        compiler_params=_cparams(("parallel",)))(ga, zb, att, proj, wa, wb, wc)


def _merge_bwd(dm, ga, zb, att, proj, wa, wb, wc, *, tm=256):
    t = ga.shape[0]
    tm = min(tm, t)
    wspec = pl.BlockSpec((DM, D), lambda i: (0, 0))

    def body(dm_ref, ga_ref, zb_ref, att_ref, gate_ref, wa_ref, wb_ref, wc_ref, dgate_ref, dga_ref, dzb_ref, datt_ref,
             dwa_ref, dwb_ref, dwc_ref):
        @pl.when(pl.program_id(0) == 0)
        def _():
            dwa_ref[...] = jnp.zeros_like(dwa_ref)
            dwb_ref[...] = jnp.zeros_like(dwb_ref)
            dwc_ref[...] = jnp.zeros_like(dwc_ref)

        dmv = dm_ref[...]
        branches = ((ga_ref, wa_ref, dga_ref, dwa_ref), (zb_ref, wb_ref, dzb_ref, dwb_ref),
                    (att_ref, wc_ref, datt_ref, dwc_ref))
        for b, (x_ref, w_ref, dx_ref, dw_ref) in enumerate(branches):
            xv, wv = x_ref[...], w_ref[...]
            y = _dot(xv, wv)
            g = _sigmoid(gate_ref[:, b * D:(b + 1) * D])
            dgate_ref[:, b * D:(b + 1) * D] = (dmv * y * g * (1.0 - g)).astype(BF16)
            dy = (dmv * g).astype(BF16)
            dx_ref[...] = _dot(dy, wv, NT)
            dw_ref[...] += _dot(xv, dy, TN)

    return pl.pallas_call(
        body, name="merge_bwd", grid=(t // tm,),
        in_specs=[_rows(tm, D), _rows(tm, DM), _rows(tm, DM), _rows(tm, DM), _rows(tm, 3 * D, 0), wspec, wspec, wspec],
        out_specs=[_rows(tm, 3 * D), _rows(tm, DM), _rows(tm, DM), _rows(tm, DM), wspec, wspec, wspec],
        out_shape=[jax.ShapeDtypeStruct((t, 3 * D), BF16)] + [jax.ShapeDtypeStruct((t, DM), F32)] * 3
        + [jax.ShapeDtypeStruct((DM, D), F32)] * 3,
        compiler_params=_cparams(("arbitrary",)))(dm, ga, zb, att, proj, wa, wb, wc)


def _heads_layout(cum):
    t = cum.shape[0]
    ckrow = cum[:, :NG].T
    return ckrow, jnp.broadcast_to(ckrow[:, :, None], (NG, t, LANE))


def _layer_fwd(x, mod, w, late=None):
    sh1, sc1, gt1, sh2, sc2, gt2 = (mod[k:k + 1] for k in range(NMOD))
    h1 = _norm_mod(x, w["mix_pre_g"], sc1, sh1, name="norm_mix")
    proj = _matmul(h1, w["w_in_p"], name="mm_proj")
    ga = _gmlp_fwd(proj, w["gmlp_ln_g"], w["gmlp_ln_b"], w["wsm"], w["bsx"])
    zc, zb = _conv_fwd(proj, w["conv_w"], w["conv_b"], w["conv_ln_g"], w["conv_ln_b"])
    cum = _fox_cum(proj, w["bfp"])
    ckrow, ckcol = _heads_layout(cum)
    qkv = _qkv_prep(proj)
    att, lse, lser = _attn_fwd(qkv, ckrow)
    if late is not None:
        w = {**w, **late(att)}
    merged = _merge_fwd(ga, zb, att, proj, w["w_a_out"], w["w_b_out"], w["w_c_out"])
    y1 = _matmul(merged, w["w_out"], name="mm_out")
    x2 = _resid(x, y1, gt1, w["mix_post_g"], name="resid_mix")
    h2 = _norm_mod(x2, w["mlp_pre_g"], sc2, sh2, name="norm_mlp")
    a, hid = _matmul(h2, w["mlp_w1"], name="mm_w1", b_slabs=True, extra_out=(BF16,),
                     epilogue=lambda acc: (acc, jnp.square(jnp.maximum(acc, 0.0))))
    y2 = _matmul(hid, w["mlp_w2"], name="mm_w2", tk=TK_DEEP)
    x3 = _resid(x2, y2, gt2, w["mlp_post_g"], name="resid_mlp")
    saved = dict(x=x, h1=h1, proj=proj, ga=ga, zc=zc, zb=zb, qkv=qkv, ckrow=ckrow, ckcol=ckcol, att=att, lse=lse, lser=lser, merged=merged,
                 y1=y1, x2=x2, h2=h2, a=a, hid=hid, y2=y2)
    return x3, saved, w


def _layer_bwd(dx3, mod, w, s, mid=None):
    sh1, sc1, gt1, sh2, sc2, gt2 = (mod[k:k + 1] for k in range(NMOD))
    g = {}
    dy2, dgt2, g["mlp_post_g"] = _resid_bwd(dx3, s["y2"], gt2, w["mlp_post_g"], name="resid_mlp_bwd")
    da = _matmul(dy2, w["mlp_w2"], tb=True, name="mm_dhid", out_dtype=BF16, extra=(s["a"],),
                 epilogue=lambda acc, a: (acc * (2.0 * jnp.maximum(a, 0.0)),))
    g["mlp_w2"] = _matmul(s["hid"], dy2, ta=True, name="mm_dw2", tk=TK_DEEP)
    g["mlp_w1"] = _matmul(s["h2"], da, ta=True, name="mm_dw1", out_slabs=N_CHIPS, tk=TK_DEEP)
    dh2 = _matmul(da, w["mlp_w1"], tb=True, name="mm_dh2", b_slabs=True)
    dx2, g["mlp_pre_g"], dsc2, dsh2 = _norm_bwd(dh2, dx3, s["x2"], w["mlp_pre_g"], sc2, name="norm_mlp_bwd")
    if mid is not None:
        gt1 = gt1 + mid(dx2)
    dy1, dgt1, g["mix_post_g"] = _resid_bwd(dx2, s["y1"], gt1, w["mix_post_g"], name="resid_mix_bwd")
    dmerged = _matmul(dy1, w["w_out"], tb=True, name="mm_dmerged")
    g["w_out"] = _matmul(s["merged"], dy1, ta=True, name="mm_dwout", tk=TK_DEEP)
    dgate, dga, dzb, datt, g["w_a_out"], g["w_b_out"], g["w_c_out"] = _merge_bwd(
        dmerged, s["ga"], s["zb"], s["att"], s["proj"], w["w_a_out"], w["w_b_out"], w["w_c_out"])
    duv, g["gmlp_ws"], dbs, g["gmlp_ln_g"], g["gmlp_ln_b"] = _gmlp_bwd(
        dga, s["proj"], w["gmlp_ln_g"], w["gmlp_ln_b"], w["wsm"], w["wsmt"], w["bsx"])
    g["gmlp_bs"] = dbs[:, :NG].T
    dzc, g["conv_ln_g"], g["conv_ln_b"] = _conv_bwd_ln(dzb, s["zc"], w["conv_ln_g"], w["conv_ln_b"])
    dglu, dcw, g["conv_b"] = _conv_bwd(dzc, s["proj"], w["conv_w"])
    g["conv_w"] = dcw[:KW]
    dq, delta, dcq = _attn_bwd_dq(s["qkv"], s["ckrow"], s["att"], datt, s["lse"])
    dk, dv, dck = _attn_bwd_dkv(s["qkv"], s["ckcol"], datt, s["lser"], delta)
    dcum = jnp.pad((dcq + dck).T, ((0, 0), (0, LANE - NG)))
    df, dbf = _fox_cum_bwd(dcum, s["proj"], w["bfp"])
    g["fox_bf"] = dbf[0, :NG]
    dproj = jnp.concatenate([dgate, duv, dglu, dq, dk, dv, df], axis=1)
    g["w_in_p"] = _matmul(s["h1"], dproj, ta=True, name="mm_dwin", tk=TK_DEEP)
    dh1 = _matmul(dproj, w["w_in_p"], tb=True, name="mm_dh1", tk=D_INP // 2)
    dx, g["mix_pre_g"], dsc1, dsh1 = _norm_bwd(dh1, dx2, s["x"], w["mix_pre_g"], sc1, name="norm_mix_bwd")
    dmod = jnp.concatenate([dsh1, dsc1, dgt1, dsh2, dsc2, dgt2], axis=0)
    return dx, g, dmod


def _position():
    return lax.axis_index("x"), lax.axis_index("y"), lax.axis_index("c")


def _all_gather8(v):
    m_per, n = v.shape

    def body(x_ref, out_ref, send_sems, recv_sems, local_sem):
        x, y, c = _position()
        me, sibling = (x, y, c), (x, y, 1 - c)
        chips = [(1 - x, y), (x, 1 - y), (1 - x, 1 - y)]

        def rows(px, py, pc):
            return out_ref.at[pl.ds((4 * px + 2 * py + pc) * m_per, m_per), :]

        def copy(k, block, to, src=None):
            return pltpu.make_async_remote_copy(
                src_ref=rows(*block) if src is None else src, dst_ref=rows(*block), send_sem=send_sems.at[k],
                recv_sem=recv_sems.at[k], device_id=to, device_id_type=MESH)

        mine = pltpu.make_async_copy(x_ref, rows(*me), local_sem)
        mine.start()
        first = [copy(0, me, sibling, src=x_ref)]
        first += [copy(1 + j, me, (*chip, c), src=x_ref) for j, chip in enumerate(chips)]
        for cp in first:
            cp.start()
        passed = [copy(4 + j, (*chip, c), sibling) for j, chip in enumerate(chips)]
        for j, chip in enumerate(chips):
            copy(1 + j, (*chip, c), me).wait_recv()
            passed[j].start()
        copy(0, sibling, me).wait_recv()
        for j, chip in enumerate(chips):
            copy(4 + j, (*chip, 1 - c), me).wait_recv()
        for cp in first + passed:
            cp.wait_send()
        mine.wait()

    out = pl.pallas_call(
        body, name="all_gather8", out_shape=jax.ShapeDtypeStruct((N_DEV * m_per, n), v.dtype),
        in_specs=[pl.BlockSpec(memory_space=pltpu.VMEM)], out_specs=pl.BlockSpec(memory_space=pltpu.VMEM),
        scratch_shapes=[pltpu.SemaphoreType.DMA((7,)), pltpu.SemaphoreType.DMA((7,)), pltpu.SemaphoreType.DMA],
        compiler_params=pltpu.CompilerParams(vmem_limit_bytes=VMEM_LIMIT),
    )(v)
    return out.reshape(N_DEV, m_per, n)


def _half(c, rows):
    return pl.ds(c * (rows // 2), rows // 2)


HBM = pl.BlockSpec(memory_space=pltpu.HBM)
SEM = pl.BlockSpec(memory_space=pltpu.SEMAPHORE)
EFFECT = pltpu.SideEffectType.DATAFLOW_SIDE_EFFECTING


_COPIES_PER_ARRAY = {"gather": 3, "scatter": 3, "swap": N_CHIPS}


def _ici_copies(kind, src_refs, land_refs, send_sems, recv_sems, arriving):
    x, y, c = _position()
    chips = [(1 - x, y), (x, 1 - y), (1 - x, 1 - y)]
    nper = _COPIES_PER_ARRAY[kind]
    copies = []
    for wi, (src, land) in enumerate(zip(src_refs, land_refs)):
        for k in range(nper):
            if kind == "swap":
                peer = (x, y, 1 - c)
                s_win, dst = src.at[k, _half(1 - c, src.shape[1])], land.at[k]
            else:
                px, py = chips[k]
                peer = (px, py, c)
                if kind == "gather":
                    rows = src.shape[0]
                    s_win = src.at[_half(c, rows)]
                    dst = land.at[2 * px + py if arriving else 2 * x + y, _half(c, rows)]
                else:
                    s_win, dst = src.at[2 * px + py], land.at[k]
            copies.append(pltpu.make_async_remote_copy(
                src_ref=s_win, dst_ref=dst, send_sem=send_sems.at[wi * nper + k], recv_sem=recv_sems.at[wi * nper + k],
                device_id=peer, device_id_type=MESH))
    return copies


def _ici_start(kind, srcs, land_shapes, after, *, name):
    nw = len(srcs)

    def body(*refs):
        src_refs, land_refs = refs[:nw], refs[nw:2 * nw]
        send_sems, recv_sems = refs[2 * nw + 1:2 * nw + 3]
        token = refs[-1]
        for cp in _ici_copies(kind, src_refs, land_refs, send_sems, recv_sems, False):
            cp.start()
        token[...] = jnp.zeros_like(token)

    lands = [pltpu.with_memory_space_constraint(lax.empty(shp, s.dtype), pltpu.HBM) for shp, s in zip(land_shapes, srcs)]
    outs = pl.pallas_call(
        body, name=name,
        out_shape=(pltpu.SemaphoreType.DMA((_COPIES_PER_ARRAY[kind] * nw,)), pltpu.SemaphoreType.DMA((_COPIES_PER_ARRAY[kind] * nw,)),
                   *[pltpu.HBM(s.shape, s.dtype) for s in srcs], *[pltpu.HBM(shp, s.dtype) for shp, s in zip(land_shapes, srcs)],
                   jax.ShapeDtypeStruct((8, LANE), F32)),
        in_specs=[HBM] * (2 * nw) + [ANY], out_specs=(SEM, SEM, *[HBM] * (2 * nw), pl.BlockSpec(memory_space=pltpu.VMEM)),
        input_output_aliases={i: 2 + i for i in range(2 * nw)},
        compiler_params=pltpu.CompilerParams(has_side_effects=EFFECT),
    )(*[pltpu.with_memory_space_constraint(s, pltpu.HBM) for s in srcs], *lands, after)
    return outs[0], outs[1], outs[2:2 + nw], outs[2 + nw:2 + 2 * nw], outs[-1]


def _ici_wait(kind, send_sems, recv_sems, srcs, lands, after, *, name):
    nw = len(srcs)

    def body(*refs):
        src_refs, land_refs = refs[:nw], refs[nw:2 * nw]
        s_sems, r_sems = refs[2 * nw:2 * nw + 2]
        for cp in _ici_copies(kind, src_refs, land_refs, s_sems, r_sems, False):
            cp.wait_send()
        for cp in _ici_copies(kind, src_refs, land_refs, s_sems, r_sems, True):
            cp.wait_recv()

    outs = pl.pallas_call(
        body, name=name,
        out_shape=(*[pltpu.HBM(s.shape, s.dtype) for s in srcs], *[pltpu.HBM(a.shape, a.dtype) for a in lands]),
        in_specs=[HBM] * (2 * nw) + [SEM, SEM, ANY], out_specs=tuple([HBM] * (2 * nw)),
        input_output_aliases={i: i for i in range(2 * nw)},
        compiler_params=pltpu.CompilerParams(has_side_effects=EFFECT),
    )(*srcs, *lands, send_sems, recv_sems, after)
    return outs[:nw], outs[nw:]


def _ag_d2d(lands, *, name):
    nw = len(lands)

    def body(*refs):
        out_refs = refs[nw:2 * nw]
        send_sems, recv_sems = refs[2 * nw:]
        x, y, c = _position()
        chips = [(1 - x, y), (x, 1 - y), (1 - x, 1 - y)]
        copies = []
        for wi in range(nw):
            rows = out_refs[wi].shape[1]
            for k, (px, py) in enumerate(chips):
                win = out_refs[wi].at[2 * px + py, _half(c, rows)]
                copies.append(pltpu.make_async_remote_copy(
                    src_ref=win, dst_ref=win, send_sem=send_sems.at[wi * 3 + k], recv_sem=recv_sems.at[wi * 3 + k],
                    device_id=(x, y, 1 - c), device_id_type=MESH))
        for cp in copies:
            cp.start()
        for wi in range(nw):
            rows = out_refs[wi].shape[1]
            for k, (px, py) in enumerate(chips):
                win = out_refs[wi].at[2 * px + py, _half(1 - c, rows)]
                pltpu.make_async_remote_copy(
                    src_ref=win, dst_ref=win, send_sem=send_sems.at[wi * 3 + k], recv_sem=recv_sems.at[wi * 3 + k],
                    device_id=(x, y, 1 - c), device_id_type=MESH).wait_recv()
        for cp in copies:
            cp.wait_send()

    return pl.pallas_call(
        body, name=name, out_shape=[jax.ShapeDtypeStruct(a.shape, a.dtype) for a in lands],
        in_specs=[ANY] * nw, out_specs=[ANY] * nw, input_output_aliases={i: i for i in range(nw)},
        scratch_shapes=[pltpu.SemaphoreType.DMA((3 * nw,)), pltpu.SemaphoreType.DMA((3 * nw,))],
    )(*lands)


def _swap_reduced(reds):
    nw = len(reds)

    def body(*refs):
        r_refs, out_refs = refs[:nw], refs[nw:2 * nw]
        send_sems, recv_sems = refs[2 * nw:]
        x, y, c = _position()
        copies = [pltpu.make_async_remote_copy(
            src_ref=r_refs[wi], dst_ref=out_refs[wi], send_sem=send_sems.at[wi], recv_sem=recv_sems.at[wi],
            device_id=(x, y, 1 - c), device_id_type=MESH) for wi in range(nw)]
        for cp in copies:
            cp.start()
        for cp in copies:
            cp.wait()

    return pl.pallas_call(
        body, name="rs_swap_reduced", out_shape=[jax.ShapeDtypeStruct(r.shape, r.dtype) for r in reds],
        in_specs=[ANY] * nw, out_specs=[ANY] * nw,
        scratch_shapes=[pltpu.SemaphoreType.DMA((nw,)), pltpu.SemaphoreType.DMA((nw,))],
    )(*reds)


def _row_tile(rows, cols):
    tr = rows
    while tr * cols * 4 > (2 << 20) and tr % 32 == 0:
        tr //= 2
    return tr


def _add_own_half(g, recv, c, *, name):
    nj, h, w = recv.shape
    tr = _row_tile(h, w)
    nb = h // tr

    def body(c_ref, g_ref, r_ref, o_ref, ob_ref):
        sm = g_ref[0] + r_ref[0]
        o_ref[0] = sm
        ob_ref[0] = sm.astype(BF16)

    spec = pl.BlockSpec((1, tr, w), lambda j, i, cc: (j, i, 0))
    gs = pltpu.PrefetchScalarGridSpec(
        num_scalar_prefetch=1, grid=(nj, nb),
        in_specs=[pl.BlockSpec((1, tr, w), lambda j, i, cc: (j, cc[0] * nb + i, 0)), spec],
        out_specs=[spec, spec])
    return pl.pallas_call(body, name=name, grid_spec=gs,
                          out_shape=[jax.ShapeDtypeStruct((nj, h, w), F32), jax.ShapeDtypeStruct((nj, h, w), BF16)],
                          compiler_params=_cparams(("parallel", "parallel")))(jnp.reshape(c, (1,)).astype(I32), g, recv)


def _add_own_chip(sp, recv, j, *, name):
    _, r, w = sp.shape
    tr = _row_tile(r, w)

    def body(j_ref, s_ref, r_ref, o_ref):
        o_ref[...] = ((s_ref[0] + r_ref[0].astype(F32)) + r_ref[1].astype(F32)) + r_ref[2].astype(F32)

    gs = pltpu.PrefetchScalarGridSpec(
        num_scalar_prefetch=1, grid=(r // tr,),
        in_specs=[pl.BlockSpec((1, tr, w), lambda i, jj: (jj[0], i, 0)), pl.BlockSpec((3, tr, w), lambda i, jj: (0, i, 0))],
        out_specs=pl.BlockSpec((tr, w), lambda i, jj: (i, 0)))
    return pl.pallas_call(body, name=name, grid_spec=gs, out_shape=jax.ShapeDtypeStruct((r, w), F32),
                          compiler_params=_cparams(("parallel",)))(jnp.reshape(j, (1,)).astype(I32), sp, recv)


def _sum8(v):
    _, m, n = v.shape

    def body(v_ref, o_ref):
        acc = v_ref[0]
        for k in range(1, N_DEV):
            acc = acc + v_ref[k]
        o_ref[...] = acc

    return pl.pallas_call(body, name="sum8", grid=(1,), in_specs=[pl.BlockSpec((N_DEV, m, n), lambda i: (0, 0, 0))],
                          out_specs=pl.BlockSpec((m, n), lambda i: (0, 0)), out_shape=jax.ShapeDtypeStruct((m, n), F32),
                          compiler_params=_cparams(("arbitrary",)))(v)


def _ada_mod(c_all, ada_w, ada_b_loc, *, tn=512):
    nl, _, ncol = ada_w.shape

    def body(c_ref, w_ref, b_ref, o_ref):
        cv = c_ref[...]
        ca = (cv * _sigmoid(cv)).astype(BF16)
        o_ref[0] = _dot(ca, w_ref[0].astype(BF16)) + b_ref[0]

    return pl.pallas_call(
        body, name="ada_mod", grid=(nl, ncol // tn),
        in_specs=[pl.BlockSpec((N_DEV, D), lambda l, j: (0, 0)), pl.BlockSpec((1, D, tn), lambda l, j: (l, 0, j)),
                  pl.BlockSpec((1, 1, tn), lambda l, j: (l, 0, j))],
        out_specs=pl.BlockSpec((1, N_DEV, tn), lambda l, j: (l, 0, j)),
        out_shape=jax.ShapeDtypeStruct((nl, N_DEV, ncol), F32),
        compiler_params=_cparams(("parallel", "parallel")))(c_all, ada_w, ada_b_loc)


def _ada_grad(c_pad, dmod_pad, *, tn=512):
    nl, nb, ncol = dmod_pad.shape

    def body(c_ref, d_ref, o_ref):
        cv = c_ref[...]
        ca = (cv * _sigmoid(cv)).astype(BF16)
        o_ref[0] = _dot(ca, d_ref[0].astype(BF16), TN)

    return pl.pallas_call(
        body, name="ada_grad", grid=(nl, ncol // tn),
        in_specs=[pl.BlockSpec((nb, D), lambda l, j: (0, 0)), pl.BlockSpec((1, nb, tn), lambda l, j: (l, 0, j))],
        out_specs=pl.BlockSpec((1, D, tn), lambda l, j: (l, 0, j)),
        out_shape=jax.ShapeDtypeStruct((nl, D, ncol), F32),
        compiler_params=_cparams(("parallel", "parallel")))(c_pad, dmod_pad)


def _adamw(w, g, m, v, *, name):
    shape = w.shape
    if w.ndim == 3:
        lead, rows, cols = shape
    else:
        lead, (rows, cols) = 1, shape
    w3, g3, m3, v3 = (a.reshape(lead, rows, cols) for a in (w, g, m, v))
    tr = rows
    if rows * cols * 4 > (2 << 20):
        tr = next(cand for cand in (256, 128, 64, 8) if rows % cand == 0)
    c1 = 1.0 - ADAM_B1 ** ADAM_STEP
    c2 = 1.0 - ADAM_B2 ** ADAM_STEP

    def body(w_ref, g_ref, m_ref, v_ref, d_ref, nm_ref, nv_ref):
        gv = g_ref[...]
        nm = ADAM_B1 * m_ref[...] + (1.0 - ADAM_B1) * gv
        nv = ADAM_B2 * v_ref[...] + (1.0 - ADAM_B2) * (gv * gv)
        nm_ref[...] = nm
        nv_ref[...] = nv
        d_ref[...] = -ADAM_LR * ((nm / c1) / (jnp.sqrt(nv / c2) + ADAM_EPS) + ADAM_WD * w_ref[...])

    spec = pl.BlockSpec((1, tr, cols), lambda l, i: (l, i, 0))
    outs = pl.pallas_call(
        body, name=name, grid=(lead, rows // tr), in_specs=[spec] * 4, out_specs=[spec] * 3,
        out_shape=[jax.ShapeDtypeStruct((lead, rows, cols), F32)] * 3,
        compiler_params=_cparams(("parallel", "parallel")))(w3, g3, m3, v3)
    return tuple(o.reshape(shape) for o in outs)


def _adamw_layer(w, g_own, g_sib, m, v, c, layer, prev, after, *, name):
    _, rows, cols = w.shape
    tr = _row_tile(rows // 2, cols)
    nbh = rows // 2 // tr
    c1 = 1.0 - ADAM_B1 ** ADAM_STEP
    c2 = 1.0 - ADAM_B2 ** ADAM_STEP
    n_prev = 0 if prev is None else 4

    def body(c_ref, w_ref, o_ref, s_ref, m_ref, v_ref, *rest):
        g_ref, d_ref, nm_ref, nv_ref = rest[n_prev + 1:]
        gv = jnp.where(pl.program_id(0) // nbh == c_ref[0], o_ref[...], s_ref[...])
        nm = ADAM_B1 * m_ref[0] + (1.0 - ADAM_B1) * gv
        nv = ADAM_B2 * v_ref[0] + (1.0 - ADAM_B2) * (gv * gv)
        g_ref[0] = gv
        nm_ref[0] = nm
        nv_ref[0] = nv
        d_ref[0] = -ADAM_LR * ((nm / c1) / (jnp.sqrt(nv / c2) + ADAM_EPS) + ADAM_WD * w_ref[0])

    def source(own):
        return pl.BlockSpec((tr, cols), lambda i, cc: (jnp.where((i // nbh == cc[0]) == own, i % nbh, 0), 0))

    spec = pl.BlockSpec((1, tr, cols), lambda i, cc: (layer, i, 0))
    gs = pltpu.PrefetchScalarGridSpec(
        num_scalar_prefetch=1, grid=(2 * nbh,),
        in_specs=[spec, source(True), source(False), spec, spec] + [ANY] * (n_prev + 1), out_specs=[spec] * 4)
    return pl.pallas_call(
        body, name=name, grid_spec=gs, out_shape=[jax.ShapeDtypeStruct(w.shape, F32)] * 4,
        input_output_aliases={6 + k: k for k in range(n_prev)},
        compiler_params=_cparams(("parallel",)))(
            jnp.reshape(c, (1,)).astype(I32), w, g_own, g_sib, m, v, *(prev or ()), after)


SMALL = (("mix_pre_g", (2, D)), ("mix_post_g", (2, D)), ("mlp_pre_g", (2, D)), ("mlp_post_g", (2, D)),
         ("gmlp_ln_g", (2, DM)), ("gmlp_ln_b", (2, DM)), ("gmlp_ws", (2, NG, CH, CH)), ("gmlp_bs", (2, NG, CH)),
         ("conv_b", (2, DM)), ("conv_ln_g", (2, DM)), ("conv_ln_b", (2, DM)), ("fox_bf", (2, NG)))


def _pack_rows(arrays):
    parts = []
    for a in arrays:
        last = a.shape[-1]
        r = a.astype(F32).reshape(-1, LANE) if last % LANE == 0 else jnp.pad(a.astype(F32).reshape(-1, last), ((0, 0), (0, LANE - last)))
        pad = (-r.shape[0]) % 8
        parts.append(jnp.pad(r, ((0, pad), (0, 0))) if pad else r)
    return jnp.concatenate(parts, axis=0)


def _unpack_rows(buf, shapes):
    out, off = [], 0
    for shp in shapes:
        size = 1
        for d in shp:
            size *= d
        last = shp[-1]
        rows = size // (LANE if last % LANE == 0 else last)
        seg = buf[off:off + rows]
        out.append(seg.reshape(shp) if last % LANE == 0 else seg[:, :last].reshape(shp))
        off += rows + (-rows) % 8
    return out


def _assemble_w_in(w_in_full):
    uv_glu_qkv = w_in_full[:, :3584]
    f = w_in_full[:, 3584:3592]
    gate = w_in_full[:, 3592:]
    fpad = jnp.zeros((D, D_INP - C_F - NG), w_in_full.dtype)
    return jnp.concatenate([gate, uv_glu_qkv, f, fpad], axis=1)


def _disassemble_w_in(g_p):
    return jnp.concatenate([g_p[:, C_UV:C_F], g_p[:, C_F:C_F + NG], g_p[:, :C_UV]], axis=1)


def kernel(x, c, ada_w, ada_b, mix_pre_g, mix_post_g, mlp_pre_g, mlp_post_g, w_in, gmlp_ln_g, gmlp_ln_b, gmlp_ws, gmlp_bs, w_a_out, conv_w, conv_b, conv_ln_g, conv_ln_b, w_b_out, fox_bf, w_c_out, w_out, mlp_w1, mlp_w2, loss_target, m_ada_w, m_ada_b, m_mix_pre_g, m_mix_post_g, m_mlp_pre_g, m_mlp_post_g, m_w_in, m_gmlp_ln_g, m_gmlp_ln_b, m_gmlp_ws, m_gmlp_bs, m_w_a_out, m_conv_w, m_conv_b, m_conv_ln_g, m_conv_ln_b, m_w_b_out, m_fox_bf, m_w_c_out, m_w_out, m_mlp_w1, m_mlp_w2, v_ada_w, v_ada_b, v_mix_pre_g, v_mix_post_g, v_mlp_pre_g, v_mlp_post_g, v_w_in, v_gmlp_ln_g, v_gmlp_ln_b, v_gmlp_ws, v_gmlp_bs, v_w_a_out, v_conv_w, v_conv_b, v_conv_ln_g, v_conv_ln_b, v_w_b_out, v_fox_bf, v_w_c_out, v_w_out, v_mlp_w1, v_mlp_w2):
    weights = dict(ada_w=ada_w, ada_b=ada_b, mix_pre_g=mix_pre_g, mix_post_g=mix_post_g, mlp_pre_g=mlp_pre_g,
                   mlp_post_g=mlp_post_g, w_in=w_in, gmlp_ln_g=gmlp_ln_g, gmlp_ln_b=gmlp_ln_b, gmlp_ws=gmlp_ws,
                   gmlp_bs=gmlp_bs, w_a_out=w_a_out, conv_w=conv_w, conv_b=conv_b, conv_ln_g=conv_ln_g,
                   conv_ln_b=conv_ln_b, w_b_out=w_b_out, fox_bf=fox_bf, w_c_out=w_c_out, w_out=w_out, mlp_w1=mlp_w1,
                   mlp_w2=mlp_w2)
    mom_m = dict(ada_w=m_ada_w, ada_b=m_ada_b, mix_pre_g=m_mix_pre_g, mix_post_g=m_mix_post_g, mlp_pre_g=m_mlp_pre_g,
                 mlp_post_g=m_mlp_post_g, w_in=m_w_in, gmlp_ln_g=m_gmlp_ln_g, gmlp_ln_b=m_gmlp_ln_b, gmlp_ws=m_gmlp_ws,
                 gmlp_bs=m_gmlp_bs, w_a_out=m_w_a_out, conv_w=m_conv_w, conv_b=m_conv_b, conv_ln_g=m_conv_ln_g,
                 conv_ln_b=m_conv_ln_b, w_b_out=m_w_b_out, fox_bf=m_fox_bf, w_c_out=m_w_c_out, w_out=m_w_out,
                 mlp_w1=m_mlp_w1, mlp_w2=m_mlp_w2)
    mom_v = dict(ada_w=v_ada_w, ada_b=v_ada_b, mix_pre_g=v_mix_pre_g, mix_post_g=v_mix_post_g, mlp_pre_g=v_mlp_pre_g,
                 mlp_post_g=v_mlp_post_g, w_in=v_w_in, gmlp_ln_g=v_gmlp_ln_g, gmlp_ln_b=v_gmlp_ln_b, gmlp_ws=v_gmlp_ws,
                 gmlp_bs=v_gmlp_bs, w_a_out=v_w_a_out, conv_w=v_conv_w, conv_b=v_conv_b, conv_ln_g=v_conv_ln_g,
                 conv_ln_b=v_conv_ln_b, w_b_out=v_w_b_out, fox_bf=v_fox_bf, w_c_out=v_w_c_out, w_out=v_w_out,
                 mlp_w1=v_mlp_w1, mlp_w2=v_mlp_w2)
    order = list(weights)
    px, py, pc = _position()
    chip = 2 * px + py
    dev = 2 * chip + pc
    depth = ada_w.shape[0]
    t = x.shape[1]
    xl = x.reshape(t, D)
    tgt = loss_target.reshape(t, D)

    big_names = [b[0] for b in BIG]
    shards = [[weights[n][l].astype(BF16) for n in big_names] for l in range(depth)]
    land_shapes = [(N_CHIPS,) + sh.shape for sh in shards[0]]

    small_in = _pack_rows([c, conv_w])
    gathered = _all_gather8(small_in)
    c_all = gathered[:, :D // LANE, :].reshape(N_DEV, D)
    cw_rows = depth * KW * LANE // LANE
    conv_w_full = jnp.concatenate(
        [gathered[2 * j, D // LANE:D // LANE + cw_rows, :].reshape(depth, KW, LANE) for j in range(N_CHIPS)], axis=2)

    ncol = ada_w.shape[2]
    ada_b_loc = lax.dynamic_slice_in_dim(ada_b, chip * ncol, ncol, axis=1).reshape(depth, 1, ncol)
    mod_sh = _ada_mod(c_all, ada_w, ada_b_loc)
    mod_g = _all_gather8(mod_sh.reshape(-1, LANE)).reshape(N_DEV, depth, N_DEV, ncol)
    mod_all = jnp.concatenate([mod_g[2 * j] for j in range(N_CHIPS)], axis=2)
    mod_mine = lax.dynamic_index_in_dim(mod_all, dev, axis=1, keepdims=False)

    mods = [mod_mine[l].reshape(NMOD, D) for l in range(depth)]

    def gathered_weights(idx, waited):
        own, lands = waited
        lands = _ag_d2d(lands, name="ag_d2d")
        w = {}
        for i, sh, g in zip(idx, own, lands):
            n, r, cdim, ax = BIG[i]
            g = lax.dynamic_update_slice(g, sh[None], (chip, 0, 0))
            if n == "w_in":
                w["w_in_p"] = _assemble_w_in(g.transpose(1, 0, 2).reshape(r, N_CHIPS * cdim))
            elif n == "mlp_w1":
                w[n] = g
            elif ax == 1:
                w[n] = g.transpose(1, 0, 2).reshape(r, N_CHIPS * cdim)
            else:
                w[n] = g.reshape(N_CHIPS * r, cdim)
        return w

    def local_weights(l):
        w = {}
        for n in ("mix_pre_g", "mix_post_g", "mlp_pre_g", "mlp_post_g", "gmlp_ln_g", "gmlp_ln_b", "conv_b", "conv_ln_g",
                  "conv_ln_b"):
            w[n] = weights[n][l:l + 1]
        tril = jnp.tril(jnp.ones((CH, CH), F32))
        wsm = gmlp_ws[l] * tril
        w["wsm"] = wsm.astype(BF16)
        w["wsmt"] = jnp.swapaxes(wsm, 1, 2).astype(BF16)
        w["bsx"] = jnp.repeat(gmlp_bs[l].T, HD, axis=1)
        w["conv_w"] = conv_w_full[l]
        w["bfp"] = jnp.pad(fox_bf[l], (0, LANE - NG)).reshape(1, LANE)
        return w

    def slabs(gfull, n, r, cdim, ax):
        if n == "mlp_w1":
            return gfull
        if ax == 1:
            return gfull.reshape(gfull.shape[0], N_CHIPS, cdim).transpose(1, 0, 2)
        return gfull.reshape(N_CHIPS, r, cdim)

    def swap_start(g, after, name):
        g["w_in"] = _disassemble_w_in(g.pop("w_in_p"))
        gs = [slabs(g[n], n, r, cdim, ax) for n, r, cdim, ax in BIG]
        return _ici_start("swap", gs, [(N_CHIPS, a.shape[1] // 2, a.shape[2]) for a in gs], after, name=name)

    def chip_sums(sw, after, name):
        gs, from_sibling = _ici_wait("swap", sw[0], sw[1], sw[2], sw[3], after, name=name)
        return [_add_own_half(a, rv, pc, name="rs_add_half_" + n) for a, rv, n in zip(gs, from_sibling, big_names)]

    def reduce_rest(sums, from_chips):
        red = [_add_own_chip(sf, rv, chip, name="rs_add_chip_" + n) for (sf, _), rv, n in zip(sums, from_chips, big_names)]
        return red, _swap_reduced(red)

    assert depth == 2
    ga, gb = [0], list(range(1, len(BIG)))

    def pick(seq, idx):
        return [seq[i] for i in idx]

    ag0a = _ici_start("gather", pick(shards[0], ga), pick(land_shapes, ga), mod_mine, name="ag0a_start")
    lands0a = _ici_wait("gather", ag0a[0], ag0a[1], ag0a[2], ag0a[3], ag0a[4], name="ag0a_wait")
    ag0b = _ici_start("gather", pick(shards[0], gb), pick(land_shapes, gb), lands0a[1][0], name="ag0b_start")
    ag1 = _ici_start("gather", shards[1], land_shapes, ag0b[4], name="ag1_start")
    mod0 = mods[0] + ag1[4][0, 0]
    layers, saved = [None] * depth, [None] * depth

    def late0(att):
        return gathered_weights(gb, _ici_wait("gather", ag0b[0], ag0b[1], ag0b[2], ag0b[3], att, name="ag0b_wait"))

    xs, saved[0], layers[0] = _layer_fwd(xl, mod0, {**local_weights(0), **gathered_weights(ga, lands0a)}, late0)
    lands1 = _ici_wait("gather", ag1[0], ag1[1], ag1[2], ag1[3], xs, name="ag1_wait")
    xs, saved[1], layers[1] = _layer_fwd(xs, mods[1], {**local_weights(1), **gathered_weights(ga + gb, lands1)})
    loss_local, dx = _loss_and_grad(xs, tgt)
    loss = lax.psum(loss_local, ("x", "y", "c"))
    grads, dmods = [None] * depth, [None] * depth
    dx, grads[1], dmods[1] = _layer_bwd(dx, mods[1], layers[1], saved[1])
    sw1 = swap_start(grads[1], dx, "sw1_start")
    rs_l1 = {}

    def mid0(dx2):
        rs_l1["sums"] = chip_sums(sw1, dx2, "sw1_wait")
        sbf1 = [sb for _, sb in rs_l1["sums"]]
        rs_l1["rs"] = _ici_start("scatter", sbf1, [(3,) + sb.shape[1:] for sb in sbf1], dx2, name="rs1_start")
        return rs_l1["rs"][4][0:1, 0:1]

    dx, grads[0], dmods[0] = _layer_bwd(dx, mod0 + sw1[4][0, 0], layers[0], saved[0], mid0)
    grad_x = dx.reshape(x.shape)
    sums1, rs1 = rs_l1["sums"], rs_l1["rs"]
    slot_shapes = [(3,) + sb.shape[1:] for _, sb in sums1]
    red1 = reduce_rest(sums1, _ici_wait("scatter", rs1[0], rs1[1], rs1[2], rs1[3], dx, name="rs1_wait")[1])
    sw0 = swap_start(grads[0], red1[1][0], "sw0_start")
    g_out = {}

    small_names = [n for n, _ in SMALL]
    small_list = [jnp.stack(dmods)] + [jnp.stack([grads[l][n] for l in range(depth)]) for n in small_names]
    small_list.append(jnp.stack([grads[l]["conv_w"] for l in range(depth)]))
    small_shapes = [(depth, NMOD * D)] + [shp for _, shp in SMALL] + [(depth, KW, DM)]
    small_all = _all_gather8(_pack_rows(small_list) + sw0[4][0, 0])
    sums0 = chip_sums(sw0, small_all, "sw0_wait")
    rs0 = _ici_start("scatter", [sb for _, sb in sums0], slot_shapes, small_all, name="rs0_start")
    tok0 = rs0[4][0, 0]
    c_all = c_all + tok0
    small_sum = _unpack_rows(_sum8(small_all) + tok0, small_shapes)
    g_out["ada_b"] = small_sum[0]
    for n, gs in zip(small_names, small_sum[1:-1]):
        g_out[n] = gs
    g_out["conv_w"] = lax.dynamic_slice_in_dim(small_sum[-1], chip * LANE, LANE, axis=2)
    dmod_all = small_all[:, :depth * NMOD * D // LANE, :].reshape(N_DEV, depth, NMOD * D)
    dmod_loc = lax.dynamic_slice_in_dim(dmod_all, chip * ncol, ncol, axis=2).transpose(1, 0, 2)
    g_out["ada_w"] = _ada_grad(jnp.pad(c_all, ((0, 8), (0, 0))), jnp.pad(dmod_loc, ((0, 0), (0, 8), (0, 0))))

    delta, new_m, new_v = {}, {}, {}
    delta["ada_w"], new_m["ada_w"], new_v["ada_w"] = _adamw(ada_w, g_out["ada_w"], m_ada_w, v_ada_w, name="adamw_ada_w")
    ws_rows = (depth * NG * CH, CH)
    ws_out = _adamw(*[d["gmlp_ws"].reshape(ws_rows) for d in (weights, g_out, mom_m, mom_v)], name="adamw_gmlp_ws")
    delta["gmlp_ws"], new_m["gmlp_ws"], new_v["gmlp_ws"] = (a.reshape(gmlp_ws.shape) for a in ws_out)
    small_params = ["ada_b"] + [n for n in small_names if n != "gmlp_ws"] + ["conv_w"]
    packs = [_pack_rows([d[n] for n in small_params]) for d in (weights, g_out, mom_m, mom_v)]
    outs = _adamw(*packs, name="adamw_small")
    shapes = [weights[n].shape for n in small_params]
    for dst, buf in zip((delta, new_m, new_v), outs):
        for n, a in zip(small_params, _unpack_rows(buf, shapes)):
            dst[n] = a
    half = {n: _adamw_layer(weights[n], red1[0][wi], red1[1][wi], mom_m[n], mom_v[n], pc, 1, None, rs0[4],
                            name="adamw1_" + n) for wi, n in enumerate(big_names)}
    done = jnp.stack([delta["ada_w"][0, 0, 0], outs[0][0, 0]] + [half[n][1][1, 0, 0] for n in big_names])
    red0 = reduce_rest(sums0, _ici_wait("scatter", rs0[0], rs0[1], rs0[2], rs0[3], done, name="rs0_wait")[1])
    for wi, n in enumerate(big_names):
        g_out[n], delta[n], new_m[n], new_v[n] = _adamw_layer(
            weights[n], red0[0][wi], red0[1][wi], mom_m[n], mom_v[n], pc, 0, half[n], rs0[4], name="adamw0_" + n)

    return (loss, grad_x, *[g_out[n] for n in order], *[delta[n] for n in order], *[new_m[n] for n in order],
            *[new_v[n] for n in order])
```

```python
import functools

import jax
import jax.numpy as jnp
from jax import lax
from jax.experimental import pallas as pl
from jax.experimental.pallas import tpu as pltpu

F32 = jnp.float32
BF16 = jnp.bfloat16
I32 = jnp.int32
MESH = pl.DeviceIdType.MESH
ANY = pl.BlockSpec(memory_space=pl.ANY)

D = 1024
DM = 512
NG = 8
CH = 128
KW = 31
HALO = 32
DFF = 4096
NMOD = 6
EPS = 1e-6
LANE = 128
N_CHIPS = 4
N_DEV = 8
C_GATE, C_UV, C_GLU, C_Q, C_K, C_V, C_F, D_INP = 0, 3072, 4096, 5120, 5632, 6144, 6656, 7168
D_IN = 6664
VMEM_LIMIT = 56 * 1024 * 1024
TK_DEEP = 4096

ADAM_LR, ADAM_B1, ADAM_B2, ADAM_EPS, ADAM_WD, ADAM_STEP = 0.001, 0.9, 0.999, 1e-08, 0.01, 10

BIG = (("w_in", 1024, 1666, 1), ("w_a_out", 512, 256, 1), ("w_b_out", 512, 256, 1), ("w_c_out", 512, 256, 1),
       ("w_out", 256, 1024, 0), ("mlp_w1", 1024, 1024, 1), ("mlp_w2", 1024, 1024, 0))


def _cparams(sem):
    return pltpu.CompilerParams(dimension_semantics=sem, vmem_limit_bytes=VMEM_LIMIT)


def _sigmoid(x):
    return jax.nn.sigmoid(x)


_GELU_K = 0.7978845608028654
_GELU_A = 0.044715


def _gelu(x):
    t = jnp.tanh(_GELU_K * (x + _GELU_A * x * x * x))
    return 0.5 * x * (1.0 + t)


def _gelu_grad(x):
    t = jnp.tanh(_GELU_K * (x + _GELU_A * x * x * x))
    return 0.5 * (1.0 + t) + 0.5 * x * (1.0 - t * t) * _GELU_K * (1.0 + 3.0 * _GELU_A * x * x)


def _mean(x):
    return jnp.mean(x, axis=-1, keepdims=True)


def _colsum(x):
    return jnp.sum(x, axis=0, keepdims=True)


def _dot(a, b, dims=((1,), (0,))):
    return lax.dot_general(a, b, (dims, ((), ())), preferred_element_type=F32)


NN = ((1,), (0,))
NT = ((1,), (1,))
TN = ((0,), (0,))


def _matmul(a, b, *, name, ta=False, tb=False, out_dtype=F32, tm=1024, tn=1024, tk=1024, epilogue=None, extra=(),
            extra_out=(), b_slabs=False, out_slabs=0):
    m, k = (a.shape[1], a.shape[0]) if ta else a.shape
    if b_slabs:
        ns, brows, bw = b.shape
        n = brows if tb else ns * bw
        assert (ns * bw if tb else brows) == k, (name, b.shape, k)
        tn, tk = (tn, bw) if tb else (bw, tk)
    else:
        n = b.shape[0] if tb else b.shape[1]
    tm, tn, tk = min(tm, m), min(tn, n), min(tk, k)
    assert m % tm == 0 and n % tn == 0 and k % tk == 0, (name, m, n, k, tm, tn, tk)
    assert not out_slabs or (n // out_slabs == tn and epilogue is None), name
    nk = k // tk
    dims = ((0 if ta else 1,), (1 if tb else 0,))
    n_extra = len(extra)
    out_dtypes = (out_dtype,) + tuple(extra_out)

    def body(a_ref, b_ref, *rest):
        extra_refs = rest[:n_extra]
        out_refs = rest[n_extra:n_extra + len(out_dtypes)]
        kk = pl.program_id(2)
        part = _dot(a_ref[...].astype(BF16), b_ref[...].astype(BF16), dims)

        def finish(acc):
            outs = (acc,) if epilogue is None else epilogue(acc, *[r[...] for r in extra_refs])
            for o_ref, o in zip(out_refs, outs):
                o_ref[...] = o.astype(o_ref.dtype)

        if nk == 1:
            finish(part)
        else:
            acc_ref = rest[-1]

            @pl.when(kk == 0)
            def _():
                acc_ref[...] = part

            @pl.when(jnp.logical_and(kk > 0, kk < nk - 1))
            def _():
                acc_ref[...] += part

            @pl.when(kk == nk - 1)
            def _():
                finish(acc_ref[...] + part)

    a_spec = pl.BlockSpec((tk, tm), lambda i, j, kk: (kk, i)) if ta else pl.BlockSpec((tm, tk), lambda i, j, kk: (i, kk))
    if b_slabs and tb:
        b_spec = pl.BlockSpec((None, tn, tk), lambda i, j, kk: (kk, j, 0))
    elif b_slabs:
        b_spec = pl.BlockSpec((None, tk, tn), lambda i, j, kk: (j, kk, 0))
    else:
        b_spec = pl.BlockSpec((tn, tk), lambda i, j, kk: (j, kk)) if tb else pl.BlockSpec((tk, tn), lambda i, j, kk: (kk, j))
    if out_slabs:
        o_spec = pl.BlockSpec((None, tm, tn), lambda i, j, kk: (j, i, 0))
        o_shape = (out_slabs, m, tn)
    else:
        o_spec = pl.BlockSpec((tm, tn), lambda i, j, kk: (i, j))
        o_shape = (m, n)
    outs = pl.pallas_call(
        body, name=name, grid=(m // tm, n // tn, nk),
        in_specs=[a_spec, b_spec] + [o_spec] * n_extra,
        out_specs=[o_spec] * len(out_dtypes),
        out_shape=[jax.ShapeDtypeStruct(o_shape, dt) for dt in out_dtypes],
        scratch_shapes=[pltpu.VMEM((tm, tn), F32)] if nk > 1 else [],
        compiler_params=_cparams(("parallel", "parallel", "arbitrary")),
    )(a, b, *extra)
    return outs[0] if len(outs) == 1 else outs


def _rows(tm, n, col=0):
    return pl.BlockSpec((tm, n), lambda i: (i, col))


def _vec(n):
    return pl.BlockSpec((1, n), lambda i: (0, 0))


def _norm_mod(x, g, sc, sh, *, name, tm=256):
    t = x.shape[0]
    tm = min(tm, t)

    def body(x_ref, g_ref, sc_ref, sh_ref, h_ref):
        xv = x_ref[...]
        inv = lax.rsqrt(_mean(xv * xv) + EPS)
        h_ref[...] = ((xv * inv * g_ref[...]) * (1.0 + sc_ref[...]) + sh_ref[...]).astype(BF16)

    return pl.pallas_call(
        body, name=name, grid=(t // tm,), in_specs=[_rows(tm, D), _vec(D), _vec(D), _vec(D)],
        out_specs=_rows(tm, D), out_shape=jax.ShapeDtypeStruct((t, D), BF16),
        compiler_params=_cparams(("parallel",)))(x, g, sc, sh)


def _resid(x, y, gt, gp, *, name, tm=256):
    t = x.shape[0]
    tm = min(tm, t)

    def body(x_ref, y_ref, gt_ref, gp_ref, o_ref):
        yv = y_ref[...]
        inv = lax.rsqrt(_mean(yv * yv) + EPS)
        o_ref[...] = x_ref[...] + gt_ref[...] * (yv * inv * gp_ref[...])

    return pl.pallas_call(
        body, name=name, grid=(t // tm,), in_specs=[_rows(tm, D), _rows(tm, D), _vec(D), _vec(D)],
        out_specs=_rows(tm, D), out_shape=jax.ShapeDtypeStruct((t, D), F32),
        compiler_params=_cparams(("parallel",)))(x, y, gt, gp)


def _resid_bwd(dx, y, gt, gp, *, name, tm=256):
    t = dx.shape[0]
    tm = min(tm, t)

    def body(dx_ref, y_ref, gt_ref, gp_ref, dy_ref, dgt_ref, dgp_ref):
        @pl.when(pl.program_id(0) == 0)
        def _():
            dgt_ref[...] = jnp.zeros_like(dgt_ref)
            dgp_ref[...] = jnp.zeros_like(dgp_ref)

        dxv, yv, gp_v = dx_ref[...], y_ref[...], gp_ref[...]
        inv = lax.rsqrt(_mean(yv * yv) + EPS)
        yh = yv * inv
        dgt_ref[...] += _colsum(dxv * (yh * gp_v))
        dr = dxv * gt_ref[...]
        dgp_ref[...] += _colsum(dr * yh)
        dyn = dr * gp_v
        dy_ref[...] = (inv * (dyn - yh * _mean(dyn * yh))).astype(BF16)

    return pl.pallas_call(
        body, name=name, grid=(t // tm,), in_specs=[_rows(tm, D), _rows(tm, D), _vec(D), _vec(D)],
        out_specs=[_rows(tm, D), _vec(D), _vec(D)],
        out_shape=[jax.ShapeDtypeStruct((t, D), BF16), jax.ShapeDtypeStruct((1, D), F32),
                   jax.ShapeDtypeStruct((1, D), F32)],
        compiler_params=_cparams(("arbitrary",)))(dx, y, gt, gp)


def _norm_bwd(dh, dx_res, x, g, sc, *, name, tm=256):
    t = dh.shape[0]
    tm = min(tm, t)

    def body(dh_ref, dxr_ref, x_ref, g_ref, sc_ref, dx_ref, dg_ref, dsc_ref, dsh_ref):
        @pl.when(pl.program_id(0) == 0)
        def _():
            dg_ref[...] = jnp.zeros_like(dg_ref)
            dsc_ref[...] = jnp.zeros_like(dsc_ref)
            dsh_ref[...] = jnp.zeros_like(dsh_ref)

        dhv, xv, gv = dh_ref[...], x_ref[...], g_ref[...]
        inv = lax.rsqrt(_mean(xv * xv) + EPS)
        xh = xv * inv
        dsh_ref[...] += _colsum(dhv)
        dsc_ref[...] += _colsum(dhv * (xh * gv))
        dn = dhv * (1.0 + sc_ref[...])
        dg_ref[...] += _colsum(dn * xh)
        dxh = dn * gv
        dx_ref[...] = inv * (dxh - xh * _mean(dxh * xh)) + dxr_ref[...]

    vec_out = jax.ShapeDtypeStruct((1, D), F32)
    return pl.pallas_call(
        body, name=name, grid=(t // tm,), in_specs=[_rows(tm, D), _rows(tm, D), _rows(tm, D), _vec(D), _vec(D)],
        out_specs=[_rows(tm, D), _vec(D), _vec(D), _vec(D)],
        out_shape=[jax.ShapeDtypeStruct((t, D), F32), vec_out, vec_out, vec_out],
        compiler_params=_cparams(("arbitrary",)))(dh, dx_res, x, g, sc)


def _loss_and_grad(x, target, *, tm=256):
    t = x.shape[0]
    tm = min(tm, t)

    def body(x_ref, t_ref, loss_ref, dx_ref):
        @pl.when(pl.program_id(0) == 0)
        def _():
            loss_ref[...] = jnp.zeros_like(loss_ref)

        e = x_ref[...] - t_ref[...]
        dx_ref[...] = e * (1.0 / D)
        s = jnp.sum(jnp.sum(e * e, axis=1, keepdims=True), axis=0, keepdims=True) * (0.5 / D)
        loss_ref[...] += jnp.broadcast_to(s, loss_ref.shape)

    loss, dx = pl.pallas_call(
        body, name="loss", grid=(t // tm,), in_specs=[_rows(tm, D), _rows(tm, D)],
        out_specs=[pl.BlockSpec((8, LANE), lambda i: (0, 0)), _rows(tm, D)],
        out_shape=[jax.ShapeDtypeStruct((8, LANE), F32), jax.ShapeDtypeStruct((t, D), F32)],
        compiler_params=_cparams(("arbitrary",)))(x, target)
    return loss[0, 0], dx


def _gmlp_core(uv, lng, lnb, ws_ref, bsx):
    tm = uv.shape[0]
    gu = _gelu(uv[:, :DM])
    gv = _gelu(uv[:, DM:])
    mu = _mean(gv)
    vc = gv - mu
    rstd = lax.rsqrt(_mean(vc * vc) + EPS)
    vh = vc * rstd
    vln = vh * lng + lnb
    lane = lax.broadcasted_iota(I32, (CH, LANE), 1)
    sv_rows = []
    for nchunk in range(tm // CH):
        vb = vln[nchunk * CH:(nchunk + 1) * CH].astype(BF16)
        cols = []
        for cb in range(DM // LANE):
            vcb = vb[:, cb * LANE:(cb + 1) * LANE]
            lo = _dot(ws_ref[2 * cb], vcb)
            hi = _dot(ws_ref[2 * cb + 1], vcb)
            cols.append(jnp.where(lane < 64, lo, hi))
        sv_rows.append(jnp.concatenate(cols, axis=1) + bsx)
    sv = jnp.concatenate(sv_rows, axis=0) if len(sv_rows) > 1 else sv_rows[0]
    return gu, vh, rstd, vln, sv


def _gmlp_fwd(proj, lng, lnb, wsm, bsx, *, tm=256):
    t = proj.shape[0]
    tm = min(tm, t)

    def body(uv_ref, lng_ref, lnb_ref, ws_ref, bs_ref, ga_ref):
        gu, _, _, _, sv = _gmlp_core(uv_ref[...], lng_ref[...], lnb_ref[...], ws_ref, bs_ref[...])
        ga_ref[...] = (gu * sv).astype(BF16)

    return pl.pallas_call(
        body, name="gmlp_fwd", grid=(t // tm,),
        in_specs=[_rows(tm, 2 * DM, C_UV // (2 * DM)), _vec(DM), _vec(DM),
                  pl.BlockSpec((NG, CH, CH), lambda i: (0, 0, 0)), pl.BlockSpec((CH, DM), lambda i: (0, 0))],
        out_specs=_rows(tm, DM), out_shape=jax.ShapeDtypeStruct((t, DM), BF16),
        compiler_params=_cparams(("parallel",)))(proj, lng, lnb, wsm, bsx)


def _gmlp_bwd(dga, proj, lng, lnb, wsm, wsmt, bsx, *, tm=256):
    t = proj.shape[0]
    tm = min(tm, t)

    def body(dga_ref, uv_ref, lng_ref, lnb_ref, ws_ref, wst_ref, bs_ref, duv_ref, dws_ref, dbs_ref, dlng_ref, dlnb_ref,
             dbsx_ref):
        i = pl.program_id(0)

        @pl.when(i == 0)
        def _():
            dws_ref[...] = jnp.zeros_like(dws_ref)
            dbsx_ref[...] = jnp.zeros_like(dbsx_ref)
            dlng_ref[...] = jnp.zeros_like(dlng_ref)
            dlnb_ref[...] = jnp.zeros_like(dlnb_ref)

        uv = uv_ref[...]
        lng_v = lng_ref[...]
        gu, vh, rstd, vln, sv = _gmlp_core(uv, lng_v, lnb_ref[...], ws_ref, bs_ref[...])
        dga_v = dga_ref[...]
        dgu = dga_v * sv
        dsv = dga_v * gu
        lane = lax.broadcasted_iota(I32, (CH, LANE), 1)
        tril = lax.broadcasted_iota(I32, (CH, CH), 0) >= lax.broadcasted_iota(I32, (CH, CH), 1)
        dvln_rows = []
        for nchunk in range(tm // CH):
            rows = slice(nchunk * CH, (nchunk + 1) * CH)
            dbsx_ref[...] += dsv[rows]
            vb = vln[rows].astype(BF16)
            cols = []
            for cb in range(DM // LANE):
                cs = slice(cb * LANE, (cb + 1) * LANE)
                dsvb = dsv[rows, cs]
                vcb = vb[:, cs]
                dlo = jnp.where(lane < 64, dsvb, 0.0).astype(BF16)
                dhi = jnp.where(lane < 64, 0.0, dsvb).astype(BF16)
                dws_ref[2 * cb] += jnp.where(tril, _dot(dlo, vcb, NT), 0.0)
                dws_ref[2 * cb + 1] += jnp.where(tril, _dot(dhi, vcb, NT), 0.0)
                dsb = dsvb.astype(BF16)
                cols.append(jnp.where(lane < 64, _dot(wst_ref[2 * cb], dsb), _dot(wst_ref[2 * cb + 1], dsb)))
            dvln_rows.append(jnp.concatenate(cols, axis=1))
        dvln = jnp.concatenate(dvln_rows, axis=0) if len(dvln_rows) > 1 else dvln_rows[0]
        dlnb_ref[...] += _colsum(dvln)
        dlng_ref[...] += _colsum(dvln * vh)
        dvh = dvln * lng_v
        dgv = rstd * (dvh - _mean(dvh) - vh * _mean(dvh * vh))
        duv_ref[:, :DM] = (dgu * _gelu_grad(uv[:, :DM])).astype(BF16)
        duv_ref[:, DM:] = (dgv * _gelu_grad(uv[:, DM:])).astype(BF16)

        @pl.when(i == pl.num_programs(0) - 1)
        def _():
            ind = (lax.broadcasted_iota(I32, (DM, LANE), 0) // 64 == lax.broadcasted_iota(I32, (DM, LANE), 1)).astype(F32)
            dbs_ref[...] = jnp.dot(dbsx_ref[...], ind, preferred_element_type=F32, precision=lax.Precision.HIGHEST)

    vec_out = jax.ShapeDtypeStruct((1, DM), F32)
    outs = pl.pallas_call(
        body, name="gmlp_bwd", grid=(t // tm,),
        in_specs=[_rows(tm, DM), _rows(tm, 2 * DM, C_UV // (2 * DM)), _vec(DM), _vec(DM),
                  pl.BlockSpec((NG, CH, CH), lambda i: (0, 0, 0)), pl.BlockSpec((NG, CH, CH), lambda i: (0, 0, 0)),
                  pl.BlockSpec((CH, DM), lambda i: (0, 0))],
        out_specs=[_rows(tm, 2 * DM), pl.BlockSpec((NG, CH, CH), lambda i: (0, 0, 0)),
                   pl.BlockSpec((CH, LANE), lambda i: (0, 0)), _vec(DM), _vec(DM)],
        out_shape=[jax.ShapeDtypeStruct((t, 2 * DM), BF16), jax.ShapeDtypeStruct((NG, CH, CH), F32),
                   jax.ShapeDtypeStruct((CH, LANE), F32), vec_out, vec_out],
        scratch_shapes=[pltpu.VMEM((CH, DM), F32)],
        compiler_params=_cparams(("arbitrary",)))(dga, proj, lng, lnb, wsm, wsmt, bsx)
    return outs


def _glu_into(zs_ref, glu_ref, halo_ref, first):
    hal = halo_ref[...]
    z0h = hal[:, :DM] * _sigmoid(hal[:, DM:])
    zs_ref[0:HALO, :] = jnp.where(first, 0.0, z0h)
    g = glu_ref[...]
    zs_ref[HALO:, :] = g[:, :DM] * _sigmoid(g[:, DM:])


SUB = 8


def _shifted_copies(dst_ref, src_ref):
    n = src_ref.shape[0]
    for s in range(SUB):
        dst_ref[s, 0:n - s, :] = src_ref[s:n, :]


def _window(shifted_ref, off, rows, cs):
    s = off % SUB
    return shifted_ref[s, off - s:off - s + rows, cs]


def _conv_fwd(proj, cw, cb, lng, lnb, *, tm=256, rb=64):
    t = proj.shape[0]
    tm = min(tm, t)
    hb = tm // HALO
    gcol = C_GLU // (2 * DM)

    def body(glu_ref, halo_ref, cw_ref, cb_ref, lng_ref, lnb_ref, zc_ref, zb_ref, zs_ref, zsh_ref):
        i = pl.program_id(0)
        _glu_into(zs_ref, glu_ref, halo_ref, i == 0)
        _shifted_copies(zsh_ref, zs_ref)
        for cbk in range(DM // LANE):
            cs = slice(cbk * LANE, (cbk + 1) * LANE)
            for r in range(tm // rb):
                acc = jnp.broadcast_to(cb_ref[:, cs], (rb, LANE))
                for k in range(KW):
                    acc = acc + cw_ref[k:k + 1, cs] * _window(zsh_ref, r * rb + HALO - (KW - 1) + k, rb, cs)
                zc_ref[r * rb:(r + 1) * rb, cs] = acc
        zc = zc_ref[...]
        mu = _mean(zc)
        zcc = zc - mu
        zh = zcc * lax.rsqrt(_mean(zcc * zcc) + EPS)
        a = zh * lng_ref[...] + lnb_ref[...]
        zb_ref[...] = (a * _sigmoid(a)).astype(BF16)

    return pl.pallas_call(
        body, name="conv_fwd", grid=(t // tm,),
        in_specs=[_rows(tm, 2 * DM, gcol),
                  pl.BlockSpec((HALO, 2 * DM), lambda i: (jnp.maximum(i * hb - 1, 0), gcol)),
                  pl.BlockSpec((KW, DM), lambda i: (0, 0)), _vec(DM), _vec(DM), _vec(DM)],
        out_specs=[_rows(tm, DM), _rows(tm, DM)],
        out_shape=[jax.ShapeDtypeStruct((t, DM), F32), jax.ShapeDtypeStruct((t, DM), BF16)],
        scratch_shapes=[pltpu.VMEM((HALO + tm, DM), F32), pltpu.VMEM((SUB, HALO + tm, DM), F32)],
        compiler_params=_cparams(("parallel",)))(proj, proj, cw, cb, lng, lnb)


def _conv_bwd_ln(dzb, zc, lng, lnb, *, tm=256):
    t = zc.shape[0]
    tm = min(tm, t)

    def body(dzb_ref, zc_ref, lng_ref, lnb_ref, dzc_ref, dlng_ref, dlnb_ref):
        @pl.when(pl.program_id(0) == 0)
        def _():
            dlng_ref[...] = jnp.zeros_like(dlng_ref)
            dlnb_ref[...] = jnp.zeros_like(dlnb_ref)

        zc = zc_ref[...]
        lng_v = lng_ref[...]
        mu = _mean(zc)
        zcc = zc - mu
        rstd = lax.rsqrt(_mean(zcc * zcc) + EPS)
        zh = zcc * rstd
        a = zh * lng_v + lnb_ref[...]
        s = _sigmoid(a)
        da = dzb_ref[...] * (s * (1.0 + a * (1.0 - s)))
        dlnb_ref[...] += _colsum(da)
        dlng_ref[...] += _colsum(da * zh)
        dzh = da * lng_v
        dzc_ref[...] = rstd * (dzh - _mean(dzh) - zh * _mean(dzh * zh))

    vec_out = jax.ShapeDtypeStruct((1, DM), F32)
    return pl.pallas_call(
        body, name="conv_bwd_ln", grid=(t // tm,), in_specs=[_rows(tm, DM), _rows(tm, DM), _vec(DM), _vec(DM)],
        out_specs=[_rows(tm, DM), _vec(DM), _vec(DM)],
        out_shape=[jax.ShapeDtypeStruct((t, DM), F32), vec_out, vec_out],
        compiler_params=_cparams(("arbitrary",)))(dzb, zc, lng, lnb)


def _conv_bwd(dzc, proj, cw, *, tm=256, rb=64):
    t = proj.shape[0]
    tm = min(tm, t)
    hb = tm // HALO
    nblk = t // tm
    gcol = C_GLU // (2 * DM)

    def body(dzc_ref, dnext_ref, glu_ref, halo_ref, cw_ref, dglu_ref, dcw_ref, dcb_ref, zs_ref, ds_ref, zsh_ref, dsh_ref):
        i = pl.program_id(0)

        @pl.when(i == 0)
        def _():
            dcw_ref[...] = jnp.zeros_like(dcw_ref)
            dcb_ref[...] = jnp.zeros_like(dcb_ref)

        _glu_into(zs_ref, glu_ref, halo_ref, i == 0)
        _shifted_copies(zsh_ref, zs_ref)
        dzc = dzc_ref[...]
        ds_ref[0:tm, :] = dzc
        ds_ref[tm:, :] = jnp.where(i == nblk - 1, 0.0, dnext_ref[...])
        _shifted_copies(dsh_ref, ds_ref)
        dcb_ref[...] += _colsum(dzc)
        for k in range(KW):
            dcw_ref[k:k + 1, :] += _colsum(dzc * _window(zsh_ref, HALO - (KW - 1) + k, tm, slice(None)))
        g = glu_ref[...]
        val, sg = g[:, :DM], _sigmoid(g[:, DM:])
        for cbk in range(DM // LANE):
            cs = slice(cbk * LANE, (cbk + 1) * LANE)
            for r in range(tm // rb):
                acc = jnp.zeros((rb, LANE), F32)
                for k in range(KW):
                    acc = acc + cw_ref[k:k + 1, cs] * _window(dsh_ref, r * rb + (KW - 1) - k, rb, cs)
                rs = slice(r * rb, (r + 1) * rb)
                dglu_ref[rs, cs] = (acc * sg[rs, cs]).astype(BF16)
                v, s = val[rs, cs], sg[rs, cs]
                dglu_ref[rs, DM + cbk * LANE:DM + (cbk + 1) * LANE] = (acc * v * s * (1.0 - s)).astype(BF16)

    return pl.pallas_call(
        body, name="conv_bwd", grid=(nblk,),
        in_specs=[_rows(tm, DM),
                  pl.BlockSpec((HALO, DM), lambda i: (jnp.minimum((i + 1) * hb, nblk * hb - 1), 0)),
                  _rows(tm, 2 * DM, gcol),
                  pl.BlockSpec((HALO, 2 * DM), lambda i: (jnp.maximum(i * hb - 1, 0), gcol)),
                  pl.BlockSpec((KW, DM), lambda i: (0, 0))],
        out_specs=[_rows(tm, 2 * DM), pl.BlockSpec((HALO, DM), lambda i: (0, 0)), _vec(DM)],
        out_shape=[jax.ShapeDtypeStruct((t, 2 * DM), BF16), jax.ShapeDtypeStruct((HALO, DM), F32),
                   jax.ShapeDtypeStruct((1, DM), F32)],
        scratch_shapes=[pltpu.VMEM((HALO + tm, DM), F32), pltpu.VMEM((tm + HALO, DM), F32),
                        pltpu.VMEM((SUB, HALO + tm, DM), F32), pltpu.VMEM((SUB, tm + HALO, DM), F32)],
        compiler_params=_cparams(("arbitrary",)))(dzc, dzc, proj, proj, cw)


CUM_ROWS = 512


def _log_sigmoid(x):
    return jnp.minimum(x, 0.0) - jnp.log1p(jnp.exp(-jnp.abs(x)))


def _fox_cum(proj, bfp):
    t = proj.shape[0]
    fcol = C_F // LANE
    cr = min(CUM_ROWS, t)

    def body(f_ref, bf_ref, cum_ref, carry_ref):
        @pl.when(pl.program_id(0) == 0)
        def _():
            carry_ref[...] = jnp.zeros_like(carry_ref)

        lf = _log_sigmoid(f_ref[...] + bf_ref[...])
        tri = (lax.broadcasted_iota(I32, (cr, cr), 0) >= lax.broadcasted_iota(I32, (cr, cr), 1)).astype(F32)
        cum = jnp.dot(tri, lf, preferred_element_type=F32, precision=lax.Precision.HIGHEST) + carry_ref[0:1, :]
        cum_ref[...] = cum
        carry_ref[...] = jnp.broadcast_to(cum[cr - 1:cr, :], carry_ref.shape)

    return pl.pallas_call(
        body, name="fox_cum", grid=(t // cr,), in_specs=[_rows(cr, LANE, fcol), _vec(LANE)],
        out_specs=_rows(cr, LANE), out_shape=jax.ShapeDtypeStruct((t, LANE), F32),
        scratch_shapes=[pltpu.VMEM((8, LANE), F32)],
        compiler_params=_cparams(("arbitrary",)))(proj, bfp)


def _fox_cum_bwd(dcum, proj, bfp):
    t = proj.shape[0]
    cr = min(CUM_ROWS, t)
    nb = t // cr
    fcol = C_F // LANE
    fw = D_INP - C_F

    def body(dc_ref, f_ref, bf_ref, df_ref, dbf_ref, carry_ref):
        @pl.when(pl.program_id(0) == 0)
        def _():
            carry_ref[...] = jnp.zeros_like(carry_ref)
            dbf_ref[...] = jnp.zeros_like(dbf_ref)

        triu = (lax.broadcasted_iota(I32, (cr, cr), 0) <= lax.broadcasted_iota(I32, (cr, cr), 1)).astype(F32)
        dlf = jnp.dot(triu, dc_ref[...], preferred_element_type=F32, precision=lax.Precision.HIGHEST) + carry_ref[0:1, :]
        carry_ref[...] = jnp.broadcast_to(dlf[0:1, :], carry_ref.shape)
        z = f_ref[...] + bf_ref[...]
        lane = lax.broadcasted_iota(I32, (cr, LANE), 1)
        df = jnp.where(lane < NG, dlf * _sigmoid(-z), 0.0)
        dbf_ref[...] += _colsum(df)
        df_ref[:, 0:LANE] = df.astype(BF16)
        df_ref[:, LANE:] = jnp.zeros((cr, fw - LANE), BF16)

    return pl.pallas_call(
        body, name="fox_cum_bwd", grid=(nb,),
        in_specs=[pl.BlockSpec((cr, LANE), lambda i: (nb - 1 - i, 0)),
                  pl.BlockSpec((cr, LANE), lambda i: (nb - 1 - i, fcol)), _vec(LANE)],
        out_specs=[pl.BlockSpec((cr, fw), lambda i: (nb - 1 - i, 0)), _vec(LANE)],
        out_shape=[jax.ShapeDtypeStruct((t, fw), BF16), jax.ShapeDtypeStruct((1, LANE), F32)],
        scratch_shapes=[pltpu.VMEM((8, LANE), F32)],
        compiler_params=_cparams(("arbitrary",)))(dcum, proj, bfp)


HD = 64
ATT_SCALE = 0.125
NEG = -1e30


def _qkv_prep(proj, *, tm=512):
    t = proj.shape[0]
    tm = min(tm, t)

    def body(q_ref, k_ref, v_ref, o_ref):
        o_ref[:, 0:DM] = (q_ref[...] * ATT_SCALE).astype(BF16)
        o_ref[:, DM:2 * DM] = k_ref[...].astype(BF16)
        o_ref[:, 2 * DM:] = v_ref[...].astype(BF16)

    return pl.pallas_call(
        body, name="qkv_prep", grid=(t // tm,),
        in_specs=[_rows(tm, DM, C_Q // DM), _rows(tm, DM, C_K // DM), _rows(tm, DM, C_V // DM)],
        out_specs=_rows(tm, 3 * DM), out_shape=jax.ShapeDtypeStruct((t, 3 * DM), BF16),
        compiler_params=_cparams(("parallel",)))(proj, proj, proj)


def _causal_pairs(nq, outer_is_query):
    if outer_is_query:
        pairs = [(i, j) for i in range(nq) for j in range(i + 1)]
    else:
        pairs = [(j, i) for j in range(nq) for i in range(j, nq)]
    return (jnp.asarray([p[0] for p in pairs], I32), jnp.asarray([p[1] for p in pairs], I32))


def _to_row(col):
    return jnp.transpose(col)[0:1, :]


def _rep(x, tk):
    return x if tk == LANE else jnp.tile(x, (1, tk // LANE))


def _attn_fwd(qkv, ckrow, *, tq=512):
    t = qkv.shape[0]
    tq = min(tq, t)
    tk = tq
    nq = t // tq
    oi, ij = _causal_pairs(nq, True)

    def body(oi_ref, ij_ref, q_ref, k_ref, v_ref, ck_ref, o_ref, lser_ref, m_ref, l_ref, acc_ref):
        n = pl.program_id(0)
        i, j = oi_ref[n], ij_ref[n]

        @pl.when(j == 0)
        def _():
            m_ref[...] = jnp.full_like(m_ref, NEG)
            l_ref[...] = jnp.zeros_like(l_ref)
            acc_ref[...] = jnp.zeros_like(acc_ref)

        def step(masked):
            if masked:
                keep = lax.broadcasted_iota(I32, (tq, tk), 1) <= lax.broadcasted_iota(I32, (tq, tk), 0)
            lo = lax.broadcasted_iota(I32, (tq, LANE), 1) < HD
            for hp in range(NG // 2):
                cs = slice(hp * LANE, (hp + 1) * LANE)
                qp, kp, vp = q_ref[:, cs], k_ref[:, cs], v_ref[:, cs]
                alphas, pvs = [], []
                for hh in range(2):
                    h = 2 * hp + hh
                    qm = jnp.where(lo if hh == 0 else jnp.logical_not(lo), qp, jnp.zeros_like(qp))
                    s = _dot(qm, kp, NT) - ck_ref[h:h + 1, :]
                    if masked:
                        s = jnp.where(keep, s, NEG)
                    m_prev = m_ref[h]
                    m_new = jnp.maximum(m_prev, jnp.max(s, axis=1, keepdims=True))
                    alpha = jnp.exp(m_prev - m_new)
                    p = jnp.exp(s - _rep(m_new, tk))
                    l_ref[h] = alpha * l_ref[h] + jnp.sum(p, axis=1, keepdims=True)
                    m_ref[h] = m_new
                    alphas.append(alpha)
                    pvs.append(_dot(p.astype(BF16), vp))
                acc_ref[:, cs] = jnp.where(lo, alphas[0], alphas[1]) * acc_ref[:, cs] + jnp.where(lo, pvs[0], pvs[1])

        @pl.when(j < i)
        def _():
            step(False)

        @pl.when(j == i)
        def _():
            step(True)
            lo = lax.broadcasted_iota(I32, (tq, LANE), 1) < HD
            for hp in range(NG // 2):
                cs = slice(hp * LANE, (hp + 1) * LANE)
                o_ref[:, cs] = (acc_ref[:, cs] / jnp.where(lo, l_ref[2 * hp], l_ref[2 * hp + 1])).astype(BF16)
            for h in range(NG):
                lser_ref[h:h + 1, :] = _to_row(m_ref[h] + jnp.log(l_ref[h]))

    gs = pltpu.PrefetchScalarGridSpec(
        num_scalar_prefetch=2, grid=(int(oi.shape[0]),),
        in_specs=[pl.BlockSpec((tq, DM), lambda n, a, b: (a[n], 0)),
                  pl.BlockSpec((tk, DM), lambda n, a, b: (b[n], 1)),
                  pl.BlockSpec((tk, DM), lambda n, a, b: (b[n], 2)),
                  pl.BlockSpec((NG, tk), lambda n, a, b: (0, b[n]))],
        out_specs=[pl.BlockSpec((tq, DM), lambda n, a, b: (a[n], 0)),
                   pl.BlockSpec((NG, tq), lambda n, a, b: (0, a[n]))],
        scratch_shapes=[pltpu.VMEM((NG, tq, LANE), F32), pltpu.VMEM((NG, tq, LANE), F32), pltpu.VMEM((tq, DM), F32)])
    return pl.pallas_call(
        body, name="attn_fwd", grid_spec=gs,
        out_shape=[jax.ShapeDtypeStruct((t, DM), BF16), jax.ShapeDtypeStruct((NG, t), F32)],
        compiler_params=_cparams(("arbitrary",)))(oi, ij, qkv, qkv, qkv, ckrow)


def _attn_delta(o, do, *, tq=512):
    t = o.shape[0]
    tq = min(tq, t)

    def body(o_ref, do_ref, d_ref):
        lo = lax.broadcasted_iota(I32, (tq, LANE), 1) < HD
        for hp in range(NG // 2):
            cs = slice(hp * LANE, (hp + 1) * LANE)
            prod = do_ref[:, cs] * o_ref[:, cs].astype(F32)
            for hh in range(2):
                d = jnp.sum(jnp.where(lo if hh == 0 else jnp.logical_not(lo), prod, 0.0), axis=1, keepdims=True)
                d_ref[2 * hp + hh:2 * hp + hh + 1, :] = _to_row(jnp.broadcast_to(d, (tq, LANE)))

    return pl.pallas_call(
        body, name="attn_delta", grid=(t // tq,),
        in_specs=[pl.BlockSpec((tq, DM), lambda i: (i, 0)), pl.BlockSpec((tq, DM), lambda i: (i, 0))],
        out_specs=pl.BlockSpec((NG, tq), lambda i: (0, i)), out_shape=jax.ShapeDtypeStruct((NG, t), F32),
        compiler_params=_cparams(("parallel",)))(o, do)


def _attn_bwd(qkv, ckcol, do, lserow, deltarow, *, tq=512):
    t = qkv.shape[0]
    tq = min(tq, t)
    tk = tq
    nq = t // tq
    oj, ii = _causal_pairs(nq, False)
    npairs = int(oj.shape[0])

    def body(oj_ref, ii_ref, q_ref, k_ref, v_ref, ck_ref, do_ref, lse_ref, delta_ref, dq_ref, dk_ref, dv_ref, dcq_ref,
             dckr_ref, dqa_ref, dka_ref, dva_ref, dck_ref):
        n = pl.program_id(0)
        j, i = oj_ref[n], ii_ref[n]

        @pl.when(n == 0)
        def _():
            dqa_ref[...] = jnp.zeros_like(dqa_ref)
            dcq_ref[...] = jnp.zeros_like(dcq_ref)

        @pl.when(i == j)
        def _():
            dka_ref[...] = jnp.zeros_like(dka_ref)
            dva_ref[...] = jnp.zeros_like(dva_ref)
            dck_ref[...] = jnp.zeros_like(dck_ref)

        def step(masked):
            if masked:
                keep = lax.broadcasted_iota(I32, (tk, tq), 0) <= lax.broadcasted_iota(I32, (tk, tq), 1)
            lo = lax.broadcasted_iota(I32, (tk, LANE), 1) < HD
            for hp in range(NG // 2):
                cs = slice(hp * LANE, (hp + 1) * LANE)
                qp, kp, vp, dop = q_ref[:, cs], k_ref[:, cs], v_ref[:, cs], do_ref[:, cs].astype(BF16)
                dvs, dks, dqs = [], [], []
                for hh in range(2):
                    h = 2 * hp + hh
                    sel = lo if hh == 0 else jnp.logical_not(lo)
                    st = _dot(jnp.where(sel, kp, jnp.zeros_like(kp)), qp, NT) - _rep(ck_ref[h], tq)
                    if masked:
                        st = jnp.where(keep, st, NEG)
                    pt = jnp.exp(st - lse_ref[h:h + 1, :])
                    dvs.append(_dot(pt.astype(BF16), dop))
                    dpt = _dot(jnp.where(sel, vp, jnp.zeros_like(vp)), dop, NT)
                    dst = pt * (dpt - delta_ref[h:h + 1, :])
                    dsb = dst.astype(BF16)
                    dks.append(_dot(dsb, qp))
                    dqs.append(_dot(dsb, kp, TN))
                    dck_ref[h] -= jnp.sum(dst, axis=1, keepdims=True)
                    dcq_ref[i, h:h + 1, :] += jnp.sum(dst, axis=0, keepdims=True)
                dva_ref[:, cs] += jnp.where(lo, dvs[0], dvs[1])
                dka_ref[:, cs] += jnp.where(lo, dks[0], dks[1])
                dqa_ref[i, :, cs] += jnp.where(lo, dqs[0], dqs[1])

        @pl.when(i == j)
        def _():
            step(True)

        @pl.when(i > j)
        def _():
            step(False)

        @pl.when(i == nq - 1)
        def _():
            dk_ref[...] = dka_ref[...].astype(BF16)
            dv_ref[...] = dva_ref[...].astype(BF16)
            for h in range(NG):
                dckr_ref[h:h + 1, :] = _to_row(dck_ref[h])

        @pl.when(n == npairs - 1)
        def _():
            dq_ref[...] = (dqa_ref[...] * ATT_SCALE).astype(BF16)

    gs = pltpu.PrefetchScalarGridSpec(
        num_scalar_prefetch=2, grid=(npairs,),
        in_specs=[pl.BlockSpec((tq, DM), lambda n, a, b: (b[n], 0)),
                  pl.BlockSpec((tk, DM), lambda n, a, b: (a[n], 1)),
                  pl.BlockSpec((tk, DM), lambda n, a, b: (a[n], 2)),
                  pl.BlockSpec((NG, tk, LANE), lambda n, a, b: (0, a[n], 0)),
                  pl.BlockSpec((tq, DM), lambda n, a, b: (b[n], 0)),
                  pl.BlockSpec((NG, tq), lambda n, a, b: (0, b[n])),
                  pl.BlockSpec((NG, tq), lambda n, a, b: (0, b[n]))],
        out_specs=[pl.BlockSpec((nq, tq, DM), lambda n, a, b: (0, 0, 0)),
                   pl.BlockSpec((tk, DM), lambda n, a, b: (a[n], 0)),
                   pl.BlockSpec((tk, DM), lambda n, a, b: (a[n], 0)),
                   pl.BlockSpec((nq, NG, tq), lambda n, a, b: (0, 0, 0)),
                   pl.BlockSpec((NG, tk), lambda n, a, b: (0, a[n]))],
        scratch_shapes=[pltpu.VMEM((nq, tq, DM), F32), pltpu.VMEM((tk, DM), F32), pltpu.VMEM((tk, DM), F32),
                        pltpu.VMEM((NG, tk, LANE), F32)])
    return pl.pallas_call(
        body, name="attn_bwd", grid_spec=gs,
        out_shape=[jax.ShapeDtypeStruct((nq, tq, DM), BF16), jax.ShapeDtypeStruct((t, DM), BF16),
                   jax.ShapeDtypeStruct((t, DM), BF16), jax.ShapeDtypeStruct((nq, NG, tq), F32),
                   jax.ShapeDtypeStruct((NG, t), F32)],
        compiler_params=_cparams(("arbitrary",)))(oj, ii, qkv, qkv, qkv, ckcol, do, lserow, deltarow)


def _merge_fwd(ga, zb, att, proj, wa, wb, wc, *, tm=256):
    t = ga.shape[0]
    tm = min(tm, t)
    wspec = pl.BlockSpec((DM, D), lambda i: (0, 0))

    def body(ga_ref, zb_ref, att_ref, gate_ref, wa_ref, wb_ref, wc_ref, m_ref):
        acc = jnp.zeros((tm, D), F32)
        for b, (x_ref, w_ref) in enumerate(((ga_ref, wa_ref), (zb_ref, wb_ref), (att_ref, wc_ref))):
            acc = acc + _sigmoid(gate_ref[:, b * D:(b + 1) * D]) * _dot(x_ref[...], w_ref[...])
        m_ref[...] = acc.astype(BF16)

    return pl.pallas_call(
        body, name="merge_fwd", grid=(t // tm,),
        in_specs=[_rows(tm, DM), _rows(tm, DM), _rows(tm, DM), _rows(tm, 3 * D, 0), wspec, wspec, wspec],
        out_specs=_rows(tm, D), out_shape=jax.ShapeDtypeStruct((t, D), BF16),
        compiler_params=_cparams(("parallel",)))(ga, zb, att, proj, wa, wb, wc)


def _merge_bwd(dm, ga, zb, att, proj, wa, wb, wc, *, tm=256):
    t = ga.shape[0]
    tm = min(tm, t)
    wspec = pl.BlockSpec((DM, D), lambda i: (0, 0))

    def body(dm_ref, ga_ref, zb_ref, att_ref, gate_ref, wa_ref, wb_ref, wc_ref, dgate_ref, dga_ref, dzb_ref, datt_ref,
             dwa_ref, dwb_ref, dwc_ref):
        @pl.when(pl.program_id(0) == 0)
        def _():
            dwa_ref[...] = jnp.zeros_like(dwa_ref)
            dwb_ref[...] = jnp.zeros_like(dwb_ref)
            dwc_ref[...] = jnp.zeros_like(dwc_ref)

        dmv = dm_ref[...]
        branches = ((ga_ref, wa_ref, dga_ref, dwa_ref), (zb_ref, wb_ref, dzb_ref, dwb_ref),
                    (att_ref, wc_ref, datt_ref, dwc_ref))
        for b, (x_ref, w_ref, dx_ref, dw_ref) in enumerate(branches):
            xv, wv = x_ref[...], w_ref[...]
            y = _dot(xv, wv)
            g = _sigmoid(gate_ref[:, b * D:(b + 1) * D])
            dgate_ref[:, b * D:(b + 1) * D] = (dmv * y * g * (1.0 - g)).astype(BF16)
            dy = (dmv * g).astype(BF16)
            dx_ref[...] = _dot(dy, wv, NT)
            dw_ref[...] += _dot(xv, dy, TN)

    return pl.pallas_call(
        body, name="merge_bwd", grid=(t // tm,),
        in_specs=[_rows(tm, D), _rows(tm, DM), _rows(tm, DM), _rows(tm, DM), _rows(tm, 3 * D, 0), wspec, wspec, wspec],
        out_specs=[_rows(tm, 3 * D), _rows(tm, DM), _rows(tm, DM), _rows(tm, DM), wspec, wspec, wspec],
        out_shape=[jax.ShapeDtypeStruct((t, 3 * D), BF16)] + [jax.ShapeDtypeStruct((t, DM), F32)] * 3
        + [jax.ShapeDtypeStruct((DM, D), F32)] * 3,
        compiler_params=_cparams(("arbitrary",)))(dm, ga, zb, att, proj, wa, wb, wc)


def _heads_layout(cum):
    t = cum.shape[0]
    ckrow = cum[:, :NG].T
    return ckrow, jnp.broadcast_to(ckrow[:, :, None], (NG, t, LANE))


def _layer_fwd(x, mod, w, late=None):
    sh1, sc1, gt1, sh2, sc2, gt2 = (mod[k:k + 1] for k in range(NMOD))
    h1 = _norm_mod(x, w["mix_pre_g"], sc1, sh1, name="norm_mix")
    proj = _matmul(h1, w["w_in_p"], name="mm_proj")
    ga = _gmlp_fwd(proj, w["gmlp_ln_g"], w["gmlp_ln_b"], w["wsm"], w["bsx"])
    zc, zb = _conv_fwd(proj, w["conv_w"], w["conv_b"], w["conv_ln_g"], w["conv_ln_b"])
    cum = _fox_cum(proj, w["bfp"])
    ckrow, ckcol = _heads_layout(cum)
    qkv = _qkv_prep(proj)
    att, lser = _attn_fwd(qkv, ckrow)
    if late is not None:
        w = {**w, **late(att)}
    merged = _merge_fwd(ga, zb, att, proj, w["w_a_out"], w["w_b_out"], w["w_c_out"])
    y1 = _matmul(merged, w["w_out"], name="mm_out")
    x2 = _resid(x, y1, gt1, w["mix_post_g"], name="resid_mix")
    h2 = _norm_mod(x2, w["mlp_pre_g"], sc2, sh2, name="norm_mlp")
    a, hid = _matmul(h2, w["mlp_w1"], name="mm_w1", b_slabs=True, extra_out=(BF16,),
                     epilogue=lambda acc: (acc, jnp.square(jnp.maximum(acc, 0.0))))
    y2 = _matmul(hid, w["mlp_w2"], name="mm_w2", tk=TK_DEEP)
    x3 = _resid(x2, y2, gt2, w["mlp_post_g"], name="resid_mlp")
    saved = dict(x=x, h1=h1, proj=proj, ga=ga, zc=zc, zb=zb, qkv=qkv, ckrow=ckrow, ckcol=ckcol, att=att, lser=lser, merged=merged,
                 y1=y1, x2=x2, h2=h2, a=a, hid=hid, y2=y2)
    return x3, saved, w


def _layer_bwd(dx3, mod, w, s, mid=None):
    sh1, sc1, gt1, sh2, sc2, gt2 = (mod[k:k + 1] for k in range(NMOD))
    g = {}
    dy2, dgt2, g["mlp_post_g"] = _resid_bwd(dx3, s["y2"], gt2, w["mlp_post_g"], name="resid_mlp_bwd")
    da = _matmul(dy2, w["mlp_w2"], tb=True, name="mm_dhid", out_dtype=BF16, extra=(s["a"],),
                 epilogue=lambda acc, a: (acc * (2.0 * jnp.maximum(a, 0.0)),))
    g["mlp_w2"] = _matmul(s["hid"], dy2, ta=True, name="mm_dw2", tk=TK_DEEP)
    g["mlp_w1"] = _matmul(s["h2"], da, ta=True, name="mm_dw1", out_slabs=N_CHIPS, tk=TK_DEEP)
    dh2 = _matmul(da, w["mlp_w1"], tb=True, name="mm_dh2", b_slabs=True)
    dx2, g["mlp_pre_g"], dsc2, dsh2 = _norm_bwd(dh2, dx3, s["x2"], w["mlp_pre_g"], sc2, name="norm_mlp_bwd")
    if mid is not None:
        gt1 = gt1 + mid(dx2)
    dy1, dgt1, g["mix_post_g"] = _resid_bwd(dx2, s["y1"], gt1, w["mix_post_g"], name="resid_mix_bwd")
    dmerged = _matmul(dy1, w["w_out"], tb=True, name="mm_dmerged")
    g["w_out"] = _matmul(s["merged"], dy1, ta=True, name="mm_dwout", tk=TK_DEEP)
    dgate, dga, dzb, datt, g["w_a_out"], g["w_b_out"], g["w_c_out"] = _merge_bwd(
        dmerged, s["ga"], s["zb"], s["att"], s["proj"], w["w_a_out"], w["w_b_out"], w["w_c_out"])
    duv, g["gmlp_ws"], dbs, g["gmlp_ln_g"], g["gmlp_ln_b"] = _gmlp_bwd(
        dga, s["proj"], w["gmlp_ln_g"], w["gmlp_ln_b"], w["wsm"], w["wsmt"], w["bsx"])
    g["gmlp_bs"] = dbs[:, :NG].T
    dzc, g["conv_ln_g"], g["conv_ln_b"] = _conv_bwd_ln(dzb, s["zc"], w["conv_ln_g"], w["conv_ln_b"])
    dglu, dcw, g["conv_b"] = _conv_bwd(dzc, s["proj"], w["conv_w"])
    g["conv_w"] = dcw[:KW]
    t = dx3.shape[0]
    dq, dk, dv, dcq, dck = _attn_bwd(s["qkv"], s["ckcol"], datt, s["lser"], _attn_delta(s["att"], datt))
    dq = dq.reshape(t, DM)
    dcum = jnp.pad((dcq.transpose(1, 0, 2).reshape(NG, t) + dck).T, ((0, 0), (0, LANE - NG)))
    df, dbf = _fox_cum_bwd(dcum, s["proj"], w["bfp"])
    g["fox_bf"] = dbf[0, :NG]
    dproj = jnp.concatenate([dgate, duv, dglu, dq, dk, dv, df], axis=1)
    g["w_in_p"] = _matmul(s["h1"], dproj, ta=True, name="mm_dwin", tk=TK_DEEP)
    dh1 = _matmul(dproj, w["w_in_p"], tb=True, name="mm_dh1", tk=D_INP // 2)
    dx, g["mix_pre_g"], dsc1, dsh1 = _norm_bwd(dh1, dx2, s["x"], w["mix_pre_g"], sc1, name="norm_mix_bwd")
    dmod = jnp.concatenate([dsh1, dsc1, dgt1, dsh2, dsc2, dgt2], axis=0)
    return dx, g, dmod


def _position():
    return lax.axis_index("x"), lax.axis_index("y"), lax.axis_index("c")


def _all_gather8(v):
    m_per, n = v.shape

    def body(x_ref, out_ref, send_sems, recv_sems, local_sem):
        x, y, c = _position()
        me, sibling = (x, y, c), (x, y, 1 - c)
        chips = [(1 - x, y), (x, 1 - y), (1 - x, 1 - y)]

        def rows(px, py, pc):
            return out_ref.at[pl.ds((4 * px + 2 * py + pc) * m_per, m_per), :]

        def copy(k, block, to, src=None):
            return pltpu.make_async_remote_copy(
                src_ref=rows(*block) if src is None else src, dst_ref=rows(*block), send_sem=send_sems.at[k],
                recv_sem=recv_sems.at[k], device_id=to, device_id_type=MESH)

        mine = pltpu.make_async_copy(x_ref, rows(*me), local_sem)
        mine.start()
        first = [copy(0, me, sibling, src=x_ref)]
        first += [copy(1 + j, me, (*chip, c), src=x_ref) for j, chip in enumerate(chips)]
        for cp in first:
            cp.start()
        passed = [copy(4 + j, (*chip, c), sibling) for j, chip in enumerate(chips)]
        for j, chip in enumerate(chips):
            copy(1 + j, (*chip, c), me).wait_recv()
            passed[j].start()
        copy(0, sibling, me).wait_recv()
        for j, chip in enumerate(chips):
            copy(4 + j, (*chip, 1 - c), me).wait_recv()
        for cp in first + passed:
            cp.wait_send()
        mine.wait()

    out = pl.pallas_call(
        body, name="all_gather8", out_shape=jax.ShapeDtypeStruct((N_DEV * m_per, n), v.dtype),
        in_specs=[pl.BlockSpec(memory_space=pltpu.VMEM)], out_specs=pl.BlockSpec(memory_space=pltpu.VMEM),
        scratch_shapes=[pltpu.SemaphoreType.DMA((7,)), pltpu.SemaphoreType.DMA((7,)), pltpu.SemaphoreType.DMA],
        compiler_params=pltpu.CompilerParams(vmem_limit_bytes=VMEM_LIMIT),
    )(v)
    return out.reshape(N_DEV, m_per, n)


def _half(c, rows):
    return pl.ds(c * (rows // 2), rows // 2)


HBM = pl.BlockSpec(memory_space=pltpu.HBM)
SEM = pl.BlockSpec(memory_space=pltpu.SEMAPHORE)
EFFECT = pltpu.SideEffectType.DATAFLOW_SIDE_EFFECTING


_COPIES_PER_ARRAY = {"gather": 3, "scatter": 3, "swap": N_CHIPS}


def _ici_copies(kind, src_refs, land_refs, send_sems, recv_sems, arriving):
    x, y, c = _position()
    chips = [(1 - x, y), (x, 1 - y), (1 - x, 1 - y)]
    nper = _COPIES_PER_ARRAY[kind]
    copies = []
    for wi, (src, land) in enumerate(zip(src_refs, land_refs)):
        for k in range(nper):
            if kind == "swap":
                peer = (x, y, 1 - c)
                s_win, dst = src.at[k, _half(1 - c, src.shape[1])], land.at[k]
            else:
                px, py = chips[k]
                peer = (px, py, c)
                if kind == "gather":
                    rows = src.shape[0]
                    s_win = src.at[_half(c, rows)]
                    dst = land.at[2 * px + py if arriving else 2 * x + y, _half(c, rows)]
                else:
                    s_win, dst = src.at[2 * px + py], land.at[k]
            copies.append(pltpu.make_async_remote_copy(
                src_ref=s_win, dst_ref=dst, send_sem=send_sems.at[wi * nper + k], recv_sem=recv_sems.at[wi * nper + k],
                device_id=peer, device_id_type=MESH))
    return copies


def _ici_start(kind, srcs, land_shapes, after, *, name):
    nw = len(srcs)

    def body(*refs):
        src_refs, land_refs = refs[:nw], refs[nw:2 * nw]
        send_sems, recv_sems = refs[2 * nw + 1:2 * nw + 3]
        token = refs[-1]
        for cp in _ici_copies(kind, src_refs, land_refs, send_sems, recv_sems, False):
            cp.start()
        token[...] = jnp.zeros_like(token)

    lands = [pltpu.with_memory_space_constraint(lax.empty(shp, s.dtype), pltpu.HBM) for shp, s in zip(land_shapes, srcs)]
    outs = pl.pallas_call(
        body, name=name,
        out_shape=(pltpu.SemaphoreType.DMA((_COPIES_PER_ARRAY[kind] * nw,)), pltpu.SemaphoreType.DMA((_COPIES_PER_ARRAY[kind] * nw,)),
                   *[pltpu.HBM(s.shape, s.dtype) for s in srcs], *[pltpu.HBM(shp, s.dtype) for shp, s in zip(land_shapes, srcs)],
                   jax.ShapeDtypeStruct((8, LANE), F32)),
        in_specs=[HBM] * (2 * nw) + [ANY], out_specs=(SEM, SEM, *[HBM] * (2 * nw), pl.BlockSpec(memory_space=pltpu.VMEM)),
        input_output_aliases={i: 2 + i for i in range(2 * nw)},
        compiler_params=pltpu.CompilerParams(has_side_effects=EFFECT),
    )(*[pltpu.with_memory_space_constraint(s, pltpu.HBM) for s in srcs], *lands, after)
    return outs[0], outs[1], outs[2:2 + nw], outs[2 + nw:2 + 2 * nw], outs[-1]


def _ici_wait(kind, send_sems, recv_sems, srcs, lands, after, *, name):
    nw = len(srcs)

    def body(*refs):
        src_refs, land_refs = refs[:nw], refs[nw:2 * nw]
        s_sems, r_sems = refs[2 * nw:2 * nw + 2]
        for cp in _ici_copies(kind, src_refs, land_refs, s_sems, r_sems, False):
            cp.wait_send()
        for cp in _ici_copies(kind, src_refs, land_refs, s_sems, r_sems, True):
            cp.wait_recv()

    outs = pl.pallas_call(
        body, name=name,
        out_shape=(*[pltpu.HBM(s.shape, s.dtype) for s in srcs], *[pltpu.HBM(a.shape, a.dtype) for a in lands]),
        in_specs=[HBM] * (2 * nw) + [SEM, SEM, ANY], out_specs=tuple([HBM] * (2 * nw)),
        input_output_aliases={i: i for i in range(2 * nw)},
        compiler_params=pltpu.CompilerParams(has_side_effects=EFFECT),
    )(*srcs, *lands, send_sems, recv_sems, after)
    return outs[:nw], outs[nw:]


def _ag_d2d(lands, *, name):
    nw = len(lands)

    def body(*refs):
        out_refs = refs[nw:2 * nw]
        send_sems, recv_sems = refs[2 * nw:]
        x, y, c = _position()
        chips = [(1 - x, y), (x, 1 - y), (1 - x, 1 - y)]
        copies = []
        for wi in range(nw):
            rows = out_refs[wi].shape[1]
            for k, (px, py) in enumerate(chips):
                win = out_refs[wi].at[2 * px + py, _half(c, rows)]
                copies.append(pltpu.make_async_remote_copy(
                    src_ref=win, dst_ref=win, send_sem=send_sems.at[wi * 3 + k], recv_sem=recv_sems.at[wi * 3 + k],
                    device_id=(x, y, 1 - c), device_id_type=MESH))
        for cp in copies:
            cp.start()
        for wi in range(nw):
            rows = out_refs[wi].shape[1]
            for k, (px, py) in enumerate(chips):
                win = out_refs[wi].at[2 * px + py, _half(1 - c, rows)]
                pltpu.make_async_remote_copy(
                    src_ref=win, dst_ref=win, send_sem=send_sems.at[wi * 3 + k], recv_sem=recv_sems.at[wi * 3 + k],
                    device_id=(x, y, 1 - c), device_id_type=MESH).wait_recv()
        for cp in copies:
            cp.wait_send()

    return pl.pallas_call(
        body, name=name, out_shape=[jax.ShapeDtypeStruct(a.shape, a.dtype) for a in lands],
        in_specs=[ANY] * nw, out_specs=[ANY] * nw, input_output_aliases={i: i for i in range(nw)},
        scratch_shapes=[pltpu.SemaphoreType.DMA((3 * nw,)), pltpu.SemaphoreType.DMA((3 * nw,))],
    )(*lands)


def _swap_reduced(reds):
    nw = len(reds)

    def body(*refs):
        r_refs, out_refs = refs[:nw], refs[nw:2 * nw]
        send_sems, recv_sems = refs[2 * nw:]
        x, y, c = _position()
        copies = [pltpu.make_async_remote_copy(
            src_ref=r_refs[wi], dst_ref=out_refs[wi], send_sem=send_sems.at[wi], recv_sem=recv_sems.at[wi],
            device_id=(x, y, 1 - c), device_id_type=MESH) for wi in range(nw)]
        for cp in copies:
            cp.start()
        for cp in copies:
            cp.wait()

    return pl.pallas_call(
        body, name="rs_swap_reduced", out_shape=[jax.ShapeDtypeStruct(r.shape, r.dtype) for r in reds],
        in_specs=[ANY] * nw, out_specs=[ANY] * nw,
        scratch_shapes=[pltpu.SemaphoreType.DMA((nw,)), pltpu.SemaphoreType.DMA((nw,))],
    )(*reds)


def _row_tile(rows, cols):
    tr = rows
    while tr * cols * 4 > (2 << 20) and tr % 32 == 0:
        tr //= 2
    return tr


def _add_own_half(g, recv, c, *, name):
    nj, h, w = recv.shape
    tr = _row_tile(h, w)
    nb = h // tr

    def body(c_ref, g_ref, r_ref, o_ref, ob_ref):
        sm = g_ref[0] + r_ref[0]
        o_ref[0] = sm
        ob_ref[0] = sm.astype(BF16)

    spec = pl.BlockSpec((1, tr, w), lambda j, i, cc: (j, i, 0))
    gs = pltpu.PrefetchScalarGridSpec(
        num_scalar_prefetch=1, grid=(nj, nb),
        in_specs=[pl.BlockSpec((1, tr, w), lambda j, i, cc: (j, cc[0] * nb + i, 0)), spec],
        out_specs=[spec, spec])
    return pl.pallas_call(body, name=name, grid_spec=gs,
                          out_shape=[jax.ShapeDtypeStruct((nj, h, w), F32), jax.ShapeDtypeStruct((nj, h, w), BF16)],
                          compiler_params=_cparams(("parallel", "parallel")))(jnp.reshape(c, (1,)).astype(I32), g, recv)


def _add_own_chip(sp, recv, j, *, name):
    _, r, w = sp.shape
    tr = _row_tile(r, w)

    def body(j_ref, s_ref, r_ref, o_ref):
        o_ref[...] = ((s_ref[0] + r_ref[0].astype(F32)) + r_ref[1].astype(F32)) + r_ref[2].astype(F32)

    gs = pltpu.PrefetchScalarGridSpec(
        num_scalar_prefetch=1, grid=(r // tr,),
        in_specs=[pl.BlockSpec((1, tr, w), lambda i, jj: (jj[0], i, 0)), pl.BlockSpec((3, tr, w), lambda i, jj: (0, i, 0))],
        out_specs=pl.BlockSpec((tr, w), lambda i, jj: (i, 0)))
    return pl.pallas_call(body, name=name, grid_spec=gs, out_shape=jax.ShapeDtypeStruct((r, w), F32),
                          compiler_params=_cparams(("parallel",)))(jnp.reshape(j, (1,)).astype(I32), sp, recv)


def _sum8(v):
    _, m, n = v.shape

    def body(v_ref, o_ref):
        acc = v_ref[0]
        for k in range(1, N_DEV):
            acc = acc + v_ref[k]
        o_ref[...] = acc

    return pl.pallas_call(body, name="sum8", grid=(1,), in_specs=[pl.BlockSpec((N_DEV, m, n), lambda i: (0, 0, 0))],
                          out_specs=pl.BlockSpec((m, n), lambda i: (0, 0)), out_shape=jax.ShapeDtypeStruct((m, n), F32),
                          compiler_params=_cparams(("arbitrary",)))(v)


def _ada_mod(c_all, ada_w, ada_b_loc, *, tn=512):
    nl, _, ncol = ada_w.shape

    def body(c_ref, w_ref, b_ref, o_ref):
        cv = c_ref[...]
        ca = (cv * _sigmoid(cv)).astype(BF16)
        o_ref[0] = _dot(ca, w_ref[0].astype(BF16)) + b_ref[0]

    return pl.pallas_call(
        body, name="ada_mod", grid=(nl, ncol // tn),
        in_specs=[pl.BlockSpec((N_DEV, D), lambda l, j: (0, 0)), pl.BlockSpec((1, D, tn), lambda l, j: (l, 0, j)),
                  pl.BlockSpec((1, 1, tn), lambda l, j: (l, 0, j))],
        out_specs=pl.BlockSpec((1, N_DEV, tn), lambda l, j: (l, 0, j)),
        out_shape=jax.ShapeDtypeStruct((nl, N_DEV, ncol), F32),
        compiler_params=_cparams(("parallel", "parallel")))(c_all, ada_w, ada_b_loc)


def _ada_grad(c_pad, dmod_pad, *, tn=512):
    nl, nb, ncol = dmod_pad.shape

    def body(c_ref, d_ref, o_ref):
        cv = c_ref[...]
        ca = (cv * _sigmoid(cv)).astype(BF16)
        o_ref[0] = _dot(ca, d_ref[0].astype(BF16), TN)

    return pl.pallas_call(
        body, name="ada_grad", grid=(nl, ncol // tn),
        in_specs=[pl.BlockSpec((nb, D), lambda l, j: (0, 0)), pl.BlockSpec((1, nb, tn), lambda l, j: (l, 0, j))],
        out_specs=pl.BlockSpec((1, D, tn), lambda l, j: (l, 0, j)),
        out_shape=jax.ShapeDtypeStruct((nl, D, ncol), F32),
        compiler_params=_cparams(("parallel", "parallel")))(c_pad, dmod_pad)


def _adamw(w, g, m, v, *, name):
    shape = w.shape
    if w.ndim == 3:
        lead, rows, cols = shape
    else:
        lead, (rows, cols) = 1, shape
    w3, g3, m3, v3 = (a.reshape(lead, rows, cols) for a in (w, g, m, v))
    tr = rows
    if rows * cols * 4 > (2 << 20):
        tr = next(cand for cand in (256, 128, 64, 8) if rows % cand == 0)
    c1 = 1.0 - ADAM_B1 ** ADAM_STEP
    c2 = 1.0 - ADAM_B2 ** ADAM_STEP

    def body(w_ref, g_ref, m_ref, v_ref, d_ref, nm_ref, nv_ref):
        gv = g_ref[...]
        nm = ADAM_B1 * m_ref[...] + (1.0 - ADAM_B1) * gv
        nv = ADAM_B2 * v_ref[...] + (1.0 - ADAM_B2) * (gv * gv)
        nm_ref[...] = nm
        nv_ref[...] = nv
        d_ref[...] = -ADAM_LR * ((nm / c1) / (jnp.sqrt(nv / c2) + ADAM_EPS) + ADAM_WD * w_ref[...])

    spec = pl.BlockSpec((1, tr, cols), lambda l, i: (l, i, 0))
    outs = pl.pallas_call(
        body, name=name, grid=(lead, rows // tr), in_specs=[spec] * 4, out_specs=[spec] * 3,
        out_shape=[jax.ShapeDtypeStruct((lead, rows, cols), F32)] * 3,
        compiler_params=_cparams(("parallel", "parallel")))(w3, g3, m3, v3)
    return tuple(o.reshape(shape) for o in outs)


def _adamw_layer(w, g_own, g_sib, m, v, c, layer, prev, after, *, name):
    _, rows, cols = w.shape
    tr = _row_tile(rows // 2, cols)
    nbh = rows // 2 // tr
    c1 = 1.0 - ADAM_B1 ** ADAM_STEP
    c2 = 1.0 - ADAM_B2 ** ADAM_STEP
    n_prev = 0 if prev is None else 4

    def body(c_ref, w_ref, o_ref, s_ref, m_ref, v_ref, *rest):
        g_ref, d_ref, nm_ref, nv_ref = rest[n_prev + 1:]
        gv = jnp.where(pl.program_id(0) // nbh == c_ref[0], o_ref[...], s_ref[...])
        nm = ADAM_B1 * m_ref[0] + (1.0 - ADAM_B1) * gv
        nv = ADAM_B2 * v_ref[0] + (1.0 - ADAM_B2) * (gv * gv)
        g_ref[0] = gv
        nm_ref[0] = nm
        nv_ref[0] = nv
        d_ref[0] = -ADAM_LR * ((nm / c1) / (jnp.sqrt(nv / c2) + ADAM_EPS) + ADAM_WD * w_ref[0])

    def source(own):
        return pl.BlockSpec((tr, cols), lambda i, cc: (jnp.where((i // nbh == cc[0]) == own, i % nbh, 0), 0))

    spec = pl.BlockSpec((1, tr, cols), lambda i, cc: (layer, i, 0))
    gs = pltpu.PrefetchScalarGridSpec(
        num_scalar_prefetch=1, grid=(2 * nbh,),
        in_specs=[spec, source(True), source(False), spec, spec] + [ANY] * (n_prev + 1), out_specs=[spec] * 4)
    return pl.pallas_call(
        body, name=name, grid_spec=gs, out_shape=[jax.ShapeDtypeStruct(w.shape, F32)] * 4,
        input_output_aliases={6 + k: k for k in range(n_prev)},
        compiler_params=_cparams(("parallel",)))(
            jnp.reshape(c, (1,)).astype(I32), w, g_own, g_sib, m, v, *(prev or ()), after)


SMALL = (("mix_pre_g", (2, D)), ("mix_post_g", (2, D)), ("mlp_pre_g", (2, D)), ("mlp_post_g", (2, D)),
         ("gmlp_ln_g", (2, DM)), ("gmlp_ln_b", (2, DM)), ("gmlp_ws", (2, NG, CH, CH)), ("gmlp_bs", (2, NG, CH)),
         ("conv_b", (2, DM)), ("conv_ln_g", (2, DM)), ("conv_ln_b", (2, DM)), ("fox_bf", (2, NG)))


def _pack_rows(arrays):
    parts = []
    for a in arrays:
        last = a.shape[-1]
        r = a.astype(F32).reshape(-1, LANE) if last % LANE == 0 else jnp.pad(a.astype(F32).reshape(-1, last), ((0, 0), (0, LANE - last)))
        pad = (-r.shape[0]) % 8
        parts.append(jnp.pad(r, ((0, pad), (0, 0))) if pad else r)
    return jnp.concatenate(parts, axis=0)


def _unpack_rows(buf, shapes):
    out, off = [], 0
    for shp in shapes:
        size = 1
        for d in shp:
            size *= d
        last = shp[-1]
        rows = size // (LANE if last % LANE == 0 else last)
        seg = buf[off:off + rows]
        out.append(seg.reshape(shp) if last % LANE == 0 else seg[:, :last].reshape(shp))
        off += rows + (-rows) % 8
    return out


def _assemble_w_in(w_in_full):
    uv_glu_qkv = w_in_full[:, :3584]
    f = w_in_full[:, 3584:3592]
    gate = w_in_full[:, 3592:]
    fpad = jnp.zeros((D, D_INP - C_F - NG), w_in_full.dtype)
    return jnp.concatenate([gate, uv_glu_qkv, f, fpad], axis=1)


def _disassemble_w_in(g_p):
    return jnp.concatenate([g_p[:, C_UV:C_F], g_p[:, C_F:C_F + NG], g_p[:, :C_UV]], axis=1)


def kernel(x, c, ada_w, ada_b, mix_pre_g, mix_post_g, mlp_pre_g, mlp_post_g, w_in, gmlp_ln_g, gmlp_ln_b, gmlp_ws, gmlp_bs, w_a_out, conv_w, conv_b, conv_ln_g, conv_ln_b, w_b_out, fox_bf, w_c_out, w_out, mlp_w1, mlp_w2, loss_target, m_ada_w, m_ada_b, m_mix_pre_g, m_mix_post_g, m_mlp_pre_g, m_mlp_post_g, m_w_in, m_gmlp_ln_g, m_gmlp_ln_b, m_gmlp_ws, m_gmlp_bs, m_w_a_out, m_conv_w, m_conv_b, m_conv_ln_g, m_conv_ln_b, m_w_b_out, m_fox_bf, m_w_c_out, m_w_out, m_mlp_w1, m_mlp_w2, v_ada_w, v_ada_b, v_mix_pre_g, v_mix_post_g, v_mlp_pre_g, v_mlp_post_g, v_w_in, v_gmlp_ln_g, v_gmlp_ln_b, v_gmlp_ws, v_gmlp_bs, v_w_a_out, v_conv_w, v_conv_b, v_conv_ln_g, v_conv_ln_b, v_w_b_out, v_fox_bf, v_w_c_out, v_w_out, v_mlp_w1, v_mlp_w2):
    weights = dict(ada_w=ada_w, ada_b=ada_b, mix_pre_g=mix_pre_g, mix_post_g=mix_post_g, mlp_pre_g=mlp_pre_g,
                   mlp_post_g=mlp_post_g, w_in=w_in, gmlp_ln_g=gmlp_ln_g, gmlp_ln_b=gmlp_ln_b, gmlp_ws=gmlp_ws,
                   gmlp_bs=gmlp_bs, w_a_out=w_a_out, conv_w=conv_w, conv_b=conv_b, conv_ln_g=conv_ln_g,
                   conv_ln_b=conv_ln_b, w_b_out=w_b_out, fox_bf=fox_bf, w_c_out=w_c_out, w_out=w_out, mlp_w1=mlp_w1,
                   mlp_w2=mlp_w2)
    mom_m = dict(ada_w=m_ada_w, ada_b=m_ada_b, mix_pre_g=m_mix_pre_g, mix_post_g=m_mix_post_g, mlp_pre_g=m_mlp_pre_g,
                 mlp_post_g=m_mlp_post_g, w_in=m_w_in, gmlp_ln_g=m_gmlp_ln_g, gmlp_ln_b=m_gmlp_ln_b, gmlp_ws=m_gmlp_ws,
                 gmlp_bs=m_gmlp_bs, w_a_out=m_w_a_out, conv_w=m_conv_w, conv_b=m_conv_b, conv_ln_g=m_conv_ln_g,
                 conv_ln_b=m_conv_ln_b, w_b_out=m_w_b_out, fox_bf=m_fox_bf, w_c_out=m_w_c_out, w_out=m_w_out,
                 mlp_w1=m_mlp_w1, mlp_w2=m_mlp_w2)
    mom_v = dict(ada_w=v_ada_w, ada_b=v_ada_b, mix_pre_g=v_mix_pre_g, mix_post_g=v_mix_post_g, mlp_pre_g=v_mlp_pre_g,
                 mlp_post_g=v_mlp_post_g, w_in=v_w_in, gmlp_ln_g=v_gmlp_ln_g, gmlp_ln_b=v_gmlp_ln_b, gmlp_ws=v_gmlp_ws,
                 gmlp_bs=v_gmlp_bs, w_a_out=v_w_a_out, conv_w=v_conv_w, conv_b=v_conv_b, conv_ln_g=v_conv_ln_g,
                 conv_ln_b=v_conv_ln_b, w_b_out=v_w_b_out, fox_bf=v_fox_bf, w_c_out=v_w_c_out, w_out=v_w_out,
                 mlp_w1=v_mlp_w1, mlp_w2=v_mlp_w2)
    order = list(weights)
    px, py, pc = _position()
    chip = 2 * px + py
    dev = 2 * chip + pc
    depth = ada_w.shape[0]
    t = x.shape[1]
    xl = x.reshape(t, D)
    tgt = loss_target.reshape(t, D)

    big_names = [b[0] for b in BIG]
    shards = [[weights[n][l].astype(BF16) for n in big_names] for l in range(depth)]
    land_shapes = [(N_CHIPS,) + sh.shape for sh in shards[0]]

    small_in = _pack_rows([c, conv_w])
    gathered = _all_gather8(small_in)
    c_all = gathered[:, :D // LANE, :].reshape(N_DEV, D)
    cw_rows = depth * KW * LANE // LANE
    conv_w_full = jnp.concatenate(
        [gathered[2 * j, D // LANE:D // LANE + cw_rows, :].reshape(depth, KW, LANE) for j in range(N_CHIPS)], axis=2)

    ncol = ada_w.shape[2]
    ada_b_loc = lax.dynamic_slice_in_dim(ada_b, chip * ncol, ncol, axis=1).reshape(depth, 1, ncol)
    mod_sh = _ada_mod(c_all, ada_w, ada_b_loc)
    mod_g = _all_gather8(mod_sh.reshape(-1, LANE)).reshape(N_DEV, depth, N_DEV, ncol)
    mod_all = jnp.concatenate([mod_g[2 * j] for j in range(N_CHIPS)], axis=2)
    mod_mine = lax.dynamic_index_in_dim(mod_all, dev, axis=1, keepdims=False)

    mods = [mod_mine[l].reshape(NMOD, D) for l in range(depth)]

    def gathered_weights(idx, waited):
        own, lands = waited
        lands = _ag_d2d(lands, name="ag_d2d")
        w = {}
        for i, sh, g in zip(idx, own, lands):
            n, r, cdim, ax = BIG[i]
            g = lax.dynamic_update_slice(g, sh[None], (chip, 0, 0))
            if n == "w_in":
                w["w_in_p"] = _assemble_w_in(g.transpose(1, 0, 2).reshape(r, N_CHIPS * cdim))
            elif n == "mlp_w1":
                w[n] = g
            elif ax == 1:
                w[n] = g.transpose(1, 0, 2).reshape(r, N_CHIPS * cdim)
            else:
                w[n] = g.reshape(N_CHIPS * r, cdim)
        return w

    def local_weights(l):
        w = {}
        for n in ("mix_pre_g", "mix_post_g", "mlp_pre_g", "mlp_post_g", "gmlp_ln_g", "gmlp_ln_b", "conv_b", "conv_ln_g",
                  "conv_ln_b"):
            w[n] = weights[n][l:l + 1]
        tril = jnp.tril(jnp.ones((CH, CH), F32))
        wsm = gmlp_ws[l] * tril
        w["wsm"] = wsm.astype(BF16)
        w["wsmt"] = jnp.swapaxes(wsm, 1, 2).astype(BF16)
        w["bsx"] = jnp.repeat(gmlp_bs[l].T, HD, axis=1)
        w["conv_w"] = conv_w_full[l]
        w["bfp"] = jnp.pad(fox_bf[l], (0, LANE - NG)).reshape(1, LANE)
        return w

    def slabs(gfull, n, r, cdim, ax):
        if n == "mlp_w1":
            return gfull
        if ax == 1:
            return gfull.reshape(gfull.shape[0], N_CHIPS, cdim).transpose(1, 0, 2)
        return gfull.reshape(N_CHIPS, r, cdim)

    def swap_start(g, after, name):
        g["w_in"] = _disassemble_w_in(g.pop("w_in_p"))
        gs = [slabs(g[n], n, r, cdim, ax) for n, r, cdim, ax in BIG]
        return _ici_start("swap", gs, [(N_CHIPS, a.shape[1] // 2, a.shape[2]) for a in gs], after, name=name)

    def chip_sums(sw, after, name):
        gs, from_sibling = _ici_wait("swap", sw[0], sw[1], sw[2], sw[3], after, name=name)
        return [_add_own_half(a, rv, pc, name="rs_add_half_" + n) for a, rv, n in zip(gs, from_sibling, big_names)]

    def reduce_rest(sums, from_chips):
        red = [_add_own_chip(sf, rv, chip, name="rs_add_chip_" + n) for (sf, _), rv, n in zip(sums, from_chips, big_names)]
        return red, _swap_reduced(red)

    assert depth == 2
    ga, gb = [0], list(range(1, len(BIG)))

    def pick(seq, idx):
        return [seq[i] for i in idx]

    ag0a = _ici_start("gather", pick(shards[0], ga), pick(land_shapes, ga), mod_mine, name="ag0a_start")
    lands0a = _ici_wait("gather", ag0a[0], ag0a[1], ag0a[2], ag0a[3], ag0a[4], name="ag0a_wait")
    ag0b = _ici_start("gather", pick(shards[0], gb), pick(land_shapes, gb), lands0a[1][0], name="ag0b_start")
    ag1 = _ici_start("gather", shards[1], land_shapes, ag0b[4], name="ag1_start")
    mod0 = mods[0] + ag1[4][0, 0]
    layers, saved = [None] * depth, [None] * depth

    def late0(att):
        return gathered_weights(gb, _ici_wait("gather", ag0b[0], ag0b[1], ag0b[2], ag0b[3], att, name="ag0b_wait"))

    xs, saved[0], layers[0] = _layer_fwd(xl, mod0, {**local_weights(0), **gathered_weights(ga, lands0a)}, late0)
    lands1 = _ici_wait("gather", ag1[0], ag1[1], ag1[2], ag1[3], xs, name="ag1_wait")
    xs, saved[1], layers[1] = _layer_fwd(xs, mods[1], {**local_weights(1), **gathered_weights(ga + gb, lands1)})
    loss_local, dx = _loss_and_grad(xs, tgt)
    loss = lax.psum(loss_local, ("x", "y", "c"))
    grads, dmods = [None] * depth, [None] * depth
    dx, grads[1], dmods[1] = _layer_bwd(dx, mods[1], layers[1], saved[1])
    sw1 = swap_start(grads[1], dx, "sw1_start")
    rs_l1 = {}

    def mid0(dx2):
        rs_l1["sums"] = chip_sums(sw1, dx2, "sw1_wait")
        sbf1 = [sb for _, sb in rs_l1["sums"]]
        rs_l1["rs"] = _ici_start("scatter", sbf1, [(3,) + sb.shape[1:] for sb in sbf1], dx2, name="rs1_start")
        return rs_l1["rs"][4][0:1, 0:1]

    dx, grads[0], dmods[0] = _layer_bwd(dx, mod0 + sw1[4][0, 0], layers[0], saved[0], mid0)
    grad_x = dx.reshape(x.shape)
    sums1, rs1 = rs_l1["sums"], rs_l1["rs"]
    slot_shapes = [(3,) + sb.shape[1:] for _, sb in sums1]
    red1 = reduce_rest(sums1, _ici_wait("scatter", rs1[0], rs1[1], rs1[2], rs1[3], dx, name="rs1_wait")[1])
    sw0 = swap_start(grads[0], red1[1][0], "sw0_start")
    g_out = {}

    small_names = [n for n, _ in SMALL]
    small_list = [jnp.stack(dmods)] + [jnp.stack([grads[l][n] for l in range(depth)]) for n in small_names]
    small_list.append(jnp.stack([grads[l]["conv_w"] for l in range(depth)]))
    small_shapes = [(depth, NMOD * D)] + [shp for _, shp in SMALL] + [(depth, KW, DM)]
    small_all = _all_gather8(_pack_rows(small_list) + sw0[4][0, 0])
    sums0 = chip_sums(sw0, small_all, "sw0_wait")
    rs0 = _ici_start("scatter", [sb for _, sb in sums0], slot_shapes, small_all, name="rs0_start")
    tok0 = rs0[4][0, 0]
    c_all = c_all + tok0
    small_sum = _unpack_rows(_sum8(small_all) + tok0, small_shapes)
    g_out["ada_b"] = small_sum[0]
    for n, gs in zip(small_names, small_sum[1:-1]):
        g_out[n] = gs
    g_out["conv_w"] = lax.dynamic_slice_in_dim(small_sum[-1], chip * LANE, LANE, axis=2)
    dmod_all = small_all[:, :depth * NMOD * D // LANE, :].reshape(N_DEV, depth, NMOD * D)
    dmod_loc = lax.dynamic_slice_in_dim(dmod_all, chip * ncol, ncol, axis=2).transpose(1, 0, 2)
    g_out["ada_w"] = _ada_grad(jnp.pad(c_all, ((0, 8), (0, 0))), jnp.pad(dmod_loc, ((0, 0), (0, 8), (0, 0))))

    delta, new_m, new_v = {}, {}, {}
    delta["ada_w"], new_m["ada_w"], new_v["ada_w"] = _adamw(ada_w, g_out["ada_w"], m_ada_w, v_ada_w, name="adamw_ada_w")
    ws_rows = (depth * NG * CH, CH)
    ws_out = _adamw(*[d["gmlp_ws"].reshape(ws_rows) for d in (weights, g_out, mom_m, mom_v)], name="adamw_gmlp_ws")
    delta["gmlp_ws"], new_m["gmlp_ws"], new_v["gmlp_ws"] = (a.reshape(gmlp_ws.shape) for a in ws_out)
    small_params = ["ada_b"] + [n for n in small_names if n != "gmlp_ws"] + ["conv_w"]
    packs = [_pack_rows([d[n] for n in small_params]) for d in (weights, g_out, mom_m, mom_v)]
    outs = _adamw(*packs, name="adamw_small")
    shapes = [weights[n].shape for n in small_params]
    for dst, buf in zip((delta, new_m, new_v), outs):
        for n, a in zip(small_params, _unpack_rows(buf, shapes)):
            dst[n] = a
    half = {n: _adamw_layer(weights[n], red1[0][wi], red1[1][wi], mom_m[n], mom_v[n], pc, 1, None, rs0[4],
                            name="adamw1_" + n) for wi, n in enumerate(big_names)}
    done = jnp.stack([delta["ada_w"][0, 0, 0], outs[0][0, 0]] + [half[n][1][1, 0, 0] for n in big_names])
    red0 = reduce_rest(sums0, _ici_wait("scatter", rs0[0], rs0[1], rs0[2], rs0[3], done, name="rs0_wait")[1])
    for wi, n in enumerate(big_names):
        g_out[n], delta[n], new_m[n], new_v[n] = _adamw_layer(
            weights[n], red0[0][wi], red0[1][wi], mom_m[n], mom_v[n], pc, 0, half[n], rs0[4], name="adamw0_" + n)

    return (loss, grad_x, *[g_out[n] for n in order], *[delta[n] for n in order], *[new_m[n] for n in order],
            *[new_v[n] for n in order])
```

```python
import functools

import jax
import jax.numpy as jnp
from jax import lax
from jax.experimental import pallas as pl
from jax.experimental.pallas import tpu as pltpu

F32 = jnp.float32
BF16 = jnp.bfloat16
I32 = jnp.int32
MESH = pl.DeviceIdType.MESH
ANY = pl.BlockSpec(memory_space=pl.ANY)

D = 1024
DM = 512
NG = 8
CH = 128
KW = 31
HALO = 32
DFF = 4096
NMOD = 6
EPS = 1e-6
LANE = 128
N_CHIPS = 4
N_DEV = 8
C_GATE, C_UV, C_GLU, C_Q, C_K, C_V, C_F, D_INP = 0, 3072, 4096, 5120, 5632, 6144, 6656, 7168
D_IN = 6664
VMEM_LIMIT = 56 * 1024 * 1024
ROW_TILE = 512
TK_DEEP = 4096

ADAM_LR, ADAM_B1, ADAM_B2, ADAM_EPS, ADAM_WD, ADAM_STEP = 0.001, 0.9, 0.999, 1e-08, 0.01, 10

BIG = (("w_in", 1024, 1666, 1), ("w_a_out", 512, 256, 1), ("w_b_out", 512, 256, 1), ("w_c_out", 512, 256, 1),
       ("w_out", 256, 1024, 0), ("mlp_w1", 1024, 1024, 1), ("mlp_w2", 1024, 1024, 0))


def _cparams(sem):
    return pltpu.CompilerParams(dimension_semantics=sem, vmem_limit_bytes=VMEM_LIMIT)


def _sigmoid(x):
    return jax.nn.sigmoid(x)


_GELU_K = 0.7978845608028654
_GELU_A = 0.044715


def _gelu(x):
    t = jnp.tanh(_GELU_K * (x + _GELU_A * x * x * x))
    return 0.5 * x * (1.0 + t)


def _gelu_grad(x):
    t = jnp.tanh(_GELU_K * (x + _GELU_A * x * x * x))
    return 0.5 * (1.0 + t) + 0.5 * x * (1.0 - t * t) * _GELU_K * (1.0 + 3.0 * _GELU_A * x * x)


def _mean(x):
    return jnp.mean(x, axis=-1, keepdims=True)


def _colsum(x):
    return jnp.sum(x, axis=0, keepdims=True)


def _dot(a, b, dims=((1,), (0,))):
    return lax.dot_general(a, b, (dims, ((), ())), preferred_element_type=F32)


NN = ((1,), (0,))
NT = ((1,), (1,))
TN = ((0,), (0,))


def _matmul(a, b, *, name, ta=False, tb=False, out_dtype=F32, tm=1024, tn=1024, tk=1024, epilogue=None, extra=(),
            extra_out=(), b_slabs=False, out_slabs=0):
    m, k = (a.shape[1], a.shape[0]) if ta else a.shape
    if b_slabs:
        ns, brows, bw = b.shape
        n = brows if tb else ns * bw
        assert (ns * bw if tb else brows) == k, (name, b.shape, k)
        all_slabs = tb and tk >= k
        tn, tk = (tn, k if all_slabs else bw) if tb else (bw, tk)
    else:
        n = b.shape[0] if tb else b.shape[1]
        all_slabs = False
    tm, tn, tk = min(tm, m), min(tn, n), min(tk, k)
    assert m % tm == 0 and n % tn == 0 and k % tk == 0, (name, m, n, k, tm, tn, tk)
    assert not out_slabs or (n // out_slabs == tn and epilogue is None), name
    nk = k // tk
    dims = ((0 if ta else 1,), (1 if tb else 0,))
    n_extra = len(extra)
    out_dtypes = (out_dtype,) + tuple(extra_out)

    def body(a_ref, b_ref, *rest):
        extra_refs = rest[:n_extra]
        out_refs = rest[n_extra:n_extra + len(out_dtypes)]
        kk = pl.program_id(2)
        if all_slabs:
            part = sum(_dot(a_ref[:, s * bw:(s + 1) * bw].astype(BF16), b_ref[s].astype(BF16), dims) for s in range(ns))
        else:
            part = _dot(a_ref[...].astype(BF16), b_ref[...].astype(BF16), dims)

        def finish(acc):
            outs = (acc,) if epilogue is None else epilogue(acc, *[r[...] for r in extra_refs])
            for o_ref, o in zip(out_refs, outs):
                o_ref[...] = o.astype(o_ref.dtype)

        if nk == 1:
            finish(part)
        else:
            acc_ref = rest[-1]

            @pl.when(kk == 0)
            def _():
                acc_ref[...] = part

            @pl.when(jnp.logical_and(kk > 0, kk < nk - 1))
            def _():
                acc_ref[...] += part

            @pl.when(kk == nk - 1)
            def _():
                finish(acc_ref[...] + part)

    a_spec = pl.BlockSpec((tk, tm), lambda i, j, kk: (kk, i)) if ta else pl.BlockSpec((tm, tk), lambda i, j, kk: (i, kk))
    if all_slabs:
        b_spec = pl.BlockSpec((ns, tn, bw), lambda i, j, kk: (0, j, 0))
    elif b_slabs and tb:
        b_spec = pl.BlockSpec((None, tn, tk), lambda i, j, kk: (kk, j, 0))
    elif b_slabs:
        b_spec = pl.BlockSpec((None, tk, tn), lambda i, j, kk: (j, kk, 0))
    else:
        b_spec = pl.BlockSpec((tn, tk), lambda i, j, kk: (j, kk)) if tb else pl.BlockSpec((tk, tn), lambda i, j, kk: (kk, j))
    if out_slabs:
        o_spec = pl.BlockSpec((None, tm, tn), lambda i, j, kk: (j, i, 0))
        o_shape = (out_slabs, m, tn)
    else:
        o_spec = pl.BlockSpec((tm, tn), lambda i, j, kk: (i, j))
        o_shape = (m, n)
    outs = pl.pallas_call(
        body, name=name, grid=(m // tm, n // tn, nk),
        in_specs=[a_spec, b_spec] + [o_spec] * n_extra,
        out_specs=[o_spec] * len(out_dtypes),
        out_shape=[jax.ShapeDtypeStruct(o_shape, dt) for dt in out_dtypes],
        scratch_shapes=[pltpu.VMEM((tm, tn), F32)] if nk > 1 else [],
        compiler_params=_cparams(("parallel", "parallel", "arbitrary")),
    )(a, b, *extra)
    return outs[0] if len(outs) == 1 else outs


def _rows(tm, n, col=0):
    return pl.BlockSpec((tm, n), lambda i: (i, col))


def _vec(n):
    return pl.BlockSpec((1, n), lambda i: (0, 0))


def _norm_mod(x, g, sc, sh, *, name, tm=ROW_TILE):
    t = x.shape[0]
    tm = min(tm, t)

    def body(x_ref, g_ref, sc_ref, sh_ref, h_ref):
        xv = x_ref[...]
        inv = lax.rsqrt(_mean(xv * xv) + EPS)
        h_ref[...] = ((xv * inv * g_ref[...]) * (1.0 + sc_ref[...]) + sh_ref[...]).astype(BF16)

    return pl.pallas_call(
        body, name=name, grid=(t // tm,), in_specs=[_rows(tm, D), _vec(D), _vec(D), _vec(D)],
        out_specs=_rows(tm, D), out_shape=jax.ShapeDtypeStruct((t, D), BF16),
        compiler_params=_cparams(("parallel",)))(x, g, sc, sh)


def _resid(x, y, gt, gp, *, name, tm=ROW_TILE):
    t = x.shape[0]
    tm = min(tm, t)

    def body(x_ref, y_ref, gt_ref, gp_ref, o_ref):
        yv = y_ref[...]
        inv = lax.rsqrt(_mean(yv * yv) + EPS)
        o_ref[...] = x_ref[...] + gt_ref[...] * (yv * inv * gp_ref[...])

    return pl.pallas_call(
        body, name=name, grid=(t // tm,), in_specs=[_rows(tm, D), _rows(tm, D), _vec(D), _vec(D)],
        out_specs=_rows(tm, D), out_shape=jax.ShapeDtypeStruct((t, D), F32),
        compiler_params=_cparams(("parallel",)))(x, y, gt, gp)


def _resid_bwd(dx, y, gt, gp, *, name, tm=ROW_TILE):
    t = dx.shape[0]
    tm = min(tm, t)

    def body(dx_ref, y_ref, gt_ref, gp_ref, dy_ref, dgt_ref, dgp_ref):
        @pl.when(pl.program_id(0) == 0)
        def _():
            dgt_ref[...] = jnp.zeros_like(dgt_ref)
            dgp_ref[...] = jnp.zeros_like(dgp_ref)

        dxv, yv, gp_v = dx_ref[...], y_ref[...], gp_ref[...]
        inv = lax.rsqrt(_mean(yv * yv) + EPS)
        yh = yv * inv
        dgt_ref[...] += _colsum(dxv * (yh * gp_v))
        dr = dxv * gt_ref[...]
        dgp_ref[...] += _colsum(dr * yh)
        dyn = dr * gp_v
        dy_ref[...] = (inv * (dyn - yh * _mean(dyn * yh))).astype(BF16)

    return pl.pallas_call(
        body, name=name, grid=(t // tm,), in_specs=[_rows(tm, D), _rows(tm, D), _vec(D), _vec(D)],
        out_specs=[_rows(tm, D), _vec(D), _vec(D)],
        out_shape=[jax.ShapeDtypeStruct((t, D), BF16), jax.ShapeDtypeStruct((1, D), F32),
                   jax.ShapeDtypeStruct((1, D), F32)],
        compiler_params=_cparams(("arbitrary",)))(dx, y, gt, gp)


def _norm_bwd(dh, dx_res, x, g, sc, *, name, tm=ROW_TILE):
    t = dh.shape[0]
    tm = min(tm, t)

    def body(dh_ref, dxr_ref, x_ref, g_ref, sc_ref, dx_ref, dg_ref, dsc_ref, dsh_ref):
        @pl.when(pl.program_id(0) == 0)
        def _():
            dg_ref[...] = jnp.zeros_like(dg_ref)
            dsc_ref[...] = jnp.zeros_like(dsc_ref)
            dsh_ref[...] = jnp.zeros_like(dsh_ref)

        dhv, xv, gv = dh_ref[...], x_ref[...], g_ref[...]
        inv = lax.rsqrt(_mean(xv * xv) + EPS)
        xh = xv * inv
        dsh_ref[...] += _colsum(dhv)
        dsc_ref[...] += _colsum(dhv * (xh * gv))
        dn = dhv * (1.0 + sc_ref[...])
        dg_ref[...] += _colsum(dn * xh)
        dxh = dn * gv
        dx_ref[...] = inv * (dxh - xh * _mean(dxh * xh)) + dxr_ref[...]

    vec_out = jax.ShapeDtypeStruct((1, D), F32)
    return pl.pallas_call(
        body, name=name, grid=(t // tm,), in_specs=[_rows(tm, D), _rows(tm, D), _rows(tm, D), _vec(D), _vec(D)],
        out_specs=[_rows(tm, D), _vec(D), _vec(D), _vec(D)],
        out_shape=[jax.ShapeDtypeStruct((t, D), F32), vec_out, vec_out, vec_out],
        compiler_params=_cparams(("arbitrary",)))(dh, dx_res, x, g, sc)


def _loss_and_grad(x, target, *, tm=ROW_TILE):
    t = x.shape[0]
    tm = min(tm, t)

    def body(x_ref, t_ref, loss_ref, dx_ref):
        @pl.when(pl.program_id(0) == 0)
        def _():
            loss_ref[...] = jnp.zeros_like(loss_ref)

        e = x_ref[...] - t_ref[...]
        dx_ref[...] = e * (1.0 / D)
        s = jnp.sum(jnp.sum(e * e, axis=1, keepdims=True), axis=0, keepdims=True) * (0.5 / D)
        loss_ref[...] += jnp.broadcast_to(s, loss_ref.shape)

    loss, dx = pl.pallas_call(
        body, name="loss", grid=(t // tm,), in_specs=[_rows(tm, D), _rows(tm, D)],
        out_specs=[pl.BlockSpec((8, LANE), lambda i: (0, 0)), _rows(tm, D)],
        out_shape=[jax.ShapeDtypeStruct((8, LANE), F32), jax.ShapeDtypeStruct((t, D), F32)],
        compiler_params=_cparams(("arbitrary",)))(x, target)
    return loss[0, 0], dx


def _gmlp_core(uv, lng, lnb, ws_ref, bsx):
    tm = uv.shape[0]
    gu = _gelu(uv[:, :DM])
    gv = _gelu(uv[:, DM:])
    mu = _mean(gv)
    vc = gv - mu
    rstd = lax.rsqrt(_mean(vc * vc) + EPS)
    vh = vc * rstd
    vln = vh * lng + lnb
    lane = lax.broadcasted_iota(I32, (CH, LANE), 1)
    sv_rows = []
    for nchunk in range(tm // CH):
        vb = vln[nchunk * CH:(nchunk + 1) * CH].astype(BF16)
        cols = []
        for cb in range(DM // LANE):
            vcb = vb[:, cb * LANE:(cb + 1) * LANE]
            lo = _dot(ws_ref[2 * cb], vcb)
            hi = _dot(ws_ref[2 * cb + 1], vcb)
            cols.append(jnp.where(lane < 64, lo, hi))
        sv_rows.append(jnp.concatenate(cols, axis=1) + bsx)
    sv = jnp.concatenate(sv_rows, axis=0) if len(sv_rows) > 1 else sv_rows[0]
    return gu, vh, rstd, vln, sv


def _gmlp_fwd(proj, lng, lnb, wsm, bsx, *, tm=256):
    t = proj.shape[0]
    tm = min(tm, t)

    def body(uv_ref, lng_ref, lnb_ref, ws_ref, bs_ref, ga_ref):
        gu, _, _, _, sv = _gmlp_core(uv_ref[...], lng_ref[...], lnb_ref[...], ws_ref, bs_ref[...])
        ga_ref[...] = (gu * sv).astype(BF16)

    return pl.pallas_call(
        body, name="gmlp_fwd", grid=(t // tm,),
        in_specs=[_rows(tm, 2 * DM, C_UV // (2 * DM)), _vec(DM), _vec(DM),
                  pl.BlockSpec((NG, CH, CH), lambda i: (0, 0, 0)), pl.BlockSpec((CH, DM), lambda i: (0, 0))],
        out_specs=_rows(tm, DM), out_shape=jax.ShapeDtypeStruct((t, DM), BF16),
        compiler_params=_cparams(("parallel",)))(proj, lng, lnb, wsm, bsx)


def _gmlp_bwd(dga, proj, lng, lnb, wsm, wsmt, bsx, *, tm=256):
    t = proj.shape[0]
    tm = min(tm, t)

    def body(dga_ref, uv_ref, lng_ref, lnb_ref, ws_ref, wst_ref, bs_ref, duv_ref, dws_ref, dbs_ref, dlng_ref, dlnb_ref,
             dbsx_ref):
        i = pl.program_id(0)

        @pl.when(i == 0)
        def _():
            dws_ref[...] = jnp.zeros_like(dws_ref)
            dbsx_ref[...] = jnp.zeros_like(dbsx_ref)
            dlng_ref[...] = jnp.zeros_like(dlng_ref)
            dlnb_ref[...] = jnp.zeros_like(dlnb_ref)

        uv = uv_ref[...]
        lng_v = lng_ref[...]
        gu, vh, rstd, vln, sv = _gmlp_core(uv, lng_v, lnb_ref[...], ws_ref, bs_ref[...])
        dga_v = dga_ref[...]
        dgu = dga_v * sv
        dsv = dga_v * gu
        lane = lax.broadcasted_iota(I32, (CH, LANE), 1)
        tril = lax.broadcasted_iota(I32, (CH, CH), 0) >= lax.broadcasted_iota(I32, (CH, CH), 1)
        dvln_rows = []
        for nchunk in range(tm // CH):
            rows = slice(nchunk * CH, (nchunk + 1) * CH)
            dbsx_ref[...] += dsv[rows]
            vb = vln[rows].astype(BF16)
            cols = []
            for cb in range(DM // LANE):
                cs = slice(cb * LANE, (cb + 1) * LANE)
                dsvb = dsv[rows, cs]
                vcb = vb[:, cs]
                dlo = jnp.where(lane < 64, dsvb, 0.0).astype(BF16)
                dhi = jnp.where(lane < 64, 0.0, dsvb).astype(BF16)
                dws_ref[2 * cb] += jnp.where(tril, _dot(dlo, vcb, NT), 0.0)
                dws_ref[2 * cb + 1] += jnp.where(tril, _dot(dhi, vcb, NT), 0.0)
                dsb = dsvb.astype(BF16)
                cols.append(jnp.where(lane < 64, _dot(wst_ref[2 * cb], dsb), _dot(wst_ref[2 * cb + 1], dsb)))
            dvln_rows.append(jnp.concatenate(cols, axis=1))
        dvln = jnp.concatenate(dvln_rows, axis=0) if len(dvln_rows) > 1 else dvln_rows[0]
        dlnb_ref[...] += _colsum(dvln)
        dlng_ref[...] += _colsum(dvln * vh)
        dvh = dvln * lng_v
        dgv = rstd * (dvh - _mean(dvh) - vh * _mean(dvh * vh))
        duv_ref[:, :DM] = (dgu * _gelu_grad(uv[:, :DM])).astype(BF16)
        duv_ref[:, DM:] = (dgv * _gelu_grad(uv[:, DM:])).astype(BF16)

        @pl.when(i == pl.num_programs(0) - 1)
        def _():
            ind = (lax.broadcasted_iota(I32, (DM, LANE), 0) // 64 == lax.broadcasted_iota(I32, (DM, LANE), 1)).astype(F32)
            dbs_ref[...] = jnp.dot(dbsx_ref[...], ind, preferred_element_type=F32, precision=lax.Precision.HIGHEST)

    vec_out = jax.ShapeDtypeStruct((1, DM), F32)
    outs = pl.pallas_call(
        body, name="gmlp_bwd", grid=(t // tm,),
        in_specs=[_rows(tm, DM), _rows(tm, 2 * DM, C_UV // (2 * DM)), _vec(DM), _vec(DM),
                  pl.BlockSpec((NG, CH, CH), lambda i: (0, 0, 0)), pl.BlockSpec((NG, CH, CH), lambda i: (0, 0, 0)),
                  pl.BlockSpec((CH, DM), lambda i: (0, 0))],
        out_specs=[_rows(tm, 2 * DM), pl.BlockSpec((NG, CH, CH), lambda i: (0, 0, 0)),
                   pl.BlockSpec((CH, LANE), lambda i: (0, 0)), _vec(DM), _vec(DM)],
        out_shape=[jax.ShapeDtypeStruct((t, 2 * DM), BF16), jax.ShapeDtypeStruct((NG, CH, CH), F32),
                   jax.ShapeDtypeStruct((CH, LANE), F32), vec_out, vec_out],
        scratch_shapes=[pltpu.VMEM((CH, DM), F32)],
        compiler_params=_cparams(("arbitrary",)))(dga, proj, lng, lnb, wsm, wsmt, bsx)
    return outs


def _glu_into(zs_ref, glu_ref, halo_ref, first):
    hal = halo_ref[...]
    z0h = hal[:, :DM] * _sigmoid(hal[:, DM:])
    zs_ref[0:HALO, :] = jnp.where(first, 0.0, z0h)
    g = glu_ref[...]
    zs_ref[HALO:, :] = g[:, :DM] * _sigmoid(g[:, DM:])


SUB = 8


def _shifted_copies(dst_ref, src_ref):
    n = src_ref.shape[0]
    for s in range(SUB):
        dst_ref[s, 0:n - s, :] = src_ref[s:n, :]


def _window(shifted_ref, off, rows, cs):
    s = off % SUB
    return shifted_ref[s, off - s:off - s + rows, cs]


def _conv_fwd(proj, cw, cb, lng, lnb, *, tm=256, rb=64):
    t = proj.shape[0]
    tm = min(tm, t)
    hb = tm // HALO
    gcol = C_GLU // (2 * DM)

    def body(glu_ref, halo_ref, cw_ref, cb_ref, lng_ref, lnb_ref, zc_ref, zb_ref, zs_ref, zsh_ref):
        i = pl.program_id(0)
        _glu_into(zs_ref, glu_ref, halo_ref, i == 0)
        _shifted_copies(zsh_ref, zs_ref)
        for cbk in range(DM // LANE):
            cs = slice(cbk * LANE, (cbk + 1) * LANE)
            for r in range(tm // rb):
                acc = jnp.broadcast_to(cb_ref[:, cs], (rb, LANE))
                for k in range(KW):
                    acc = acc + cw_ref[k:k + 1, cs] * _window(zsh_ref, r * rb + HALO - (KW - 1) + k, rb, cs)
                zc_ref[r * rb:(r + 1) * rb, cs] = acc
        zc = zc_ref[...]
        mu = _mean(zc)
        zcc = zc - mu
        zh = zcc * lax.rsqrt(_mean(zcc * zcc) + EPS)
        a = zh * lng_ref[...] + lnb_ref[...]
        zb_ref[...] = (a * _sigmoid(a)).astype(BF16)

    return pl.pallas_call(
        body, name="conv_fwd", grid=(t // tm,),
        in_specs=[_rows(tm, 2 * DM, gcol),
                  pl.BlockSpec((HALO, 2 * DM), lambda i: (jnp.maximum(i * hb - 1, 0), gcol)),
                  pl.BlockSpec((KW, DM), lambda i: (0, 0)), _vec(DM), _vec(DM), _vec(DM)],
        out_specs=[_rows(tm, DM), _rows(tm, DM)],
        out_shape=[jax.ShapeDtypeStruct((t, DM), F32), jax.ShapeDtypeStruct((t, DM), BF16)],
        scratch_shapes=[pltpu.VMEM((HALO + tm, DM), F32), pltpu.VMEM((SUB, HALO + tm, DM), F32)],
        compiler_params=_cparams(("parallel",)))(proj, proj, cw, cb, lng, lnb)


def _conv_bwd_ln(dzb, zc, lng, lnb, *, tm=ROW_TILE):
    t = zc.shape[0]
    tm = min(tm, t)

    def body(dzb_ref, zc_ref, lng_ref, lnb_ref, dzc_ref, dlng_ref, dlnb_ref):
        @pl.when(pl.program_id(0) == 0)
        def _():
            dlng_ref[...] = jnp.zeros_like(dlng_ref)
            dlnb_ref[...] = jnp.zeros_like(dlnb_ref)

        zc = zc_ref[...]
        lng_v = lng_ref[...]
        mu = _mean(zc)
        zcc = zc - mu
        rstd = lax.rsqrt(_mean(zcc * zcc) + EPS)
        zh = zcc * rstd
        a = zh * lng_v + lnb_ref[...]
        s = _sigmoid(a)
        da = dzb_ref[...] * (s * (1.0 + a * (1.0 - s)))
        dlnb_ref[...] += _colsum(da)
        dlng_ref[...] += _colsum(da * zh)
        dzh = da * lng_v
        dzc_ref[...] = rstd * (dzh - _mean(dzh) - zh * _mean(dzh * zh))

    vec_out = jax.ShapeDtypeStruct((1, DM), F32)
    return pl.pallas_call(
        body, name="conv_bwd_ln", grid=(t // tm,), in_specs=[_rows(tm, DM), _rows(tm, DM), _vec(DM), _vec(DM)],
        out_specs=[_rows(tm, DM), _vec(DM), _vec(DM)],
        out_shape=[jax.ShapeDtypeStruct((t, DM), F32), vec_out, vec_out],
        compiler_params=_cparams(("arbitrary",)))(dzb, zc, lng, lnb)


def _conv_bwd(dzc, proj, cw, *, tm=256, rb=64):
    t = proj.shape[0]
    tm = min(tm, t)
    hb = tm // HALO
    nblk = t // tm
    gcol = C_GLU // (2 * DM)

    def body(dzc_ref, dnext_ref, glu_ref, halo_ref, cw_ref, dglu_ref, dcw_ref, dcb_ref, zs_ref, ds_ref, zsh_ref, dsh_ref):
        i = pl.program_id(0)

        @pl.when(i == 0)
        def _():
            dcw_ref[...] = jnp.zeros_like(dcw_ref)
            dcb_ref[...] = jnp.zeros_like(dcb_ref)

        _glu_into(zs_ref, glu_ref, halo_ref, i == 0)
        _shifted_copies(zsh_ref, zs_ref)
        dzc = dzc_ref[...]
        ds_ref[0:tm, :] = dzc
        ds_ref[tm:, :] = jnp.where(i == nblk - 1, 0.0, dnext_ref[...])
        _shifted_copies(dsh_ref, ds_ref)
        dcb_ref[...] += _colsum(dzc)
        for k in range(KW):
            dcw_ref[k:k + 1, :] += _colsum(dzc * _window(zsh_ref, HALO - (KW - 1) + k, tm, slice(None)))
        g = glu_ref[...]
        val, sg = g[:, :DM], _sigmoid(g[:, DM:])
        for cbk in range(DM // LANE):
            cs = slice(cbk * LANE, (cbk + 1) * LANE)
            for r in range(tm // rb):
                acc = jnp.zeros((rb, LANE), F32)
                for k in range(KW):
                    acc = acc + cw_ref[k:k + 1, cs] * _window(dsh_ref, r * rb + (KW - 1) - k, rb, cs)
                rs = slice(r * rb, (r + 1) * rb)
                dglu_ref[rs, cs] = (acc * sg[rs, cs]).astype(BF16)
                v, s = val[rs, cs], sg[rs, cs]
                dglu_ref[rs, DM + cbk * LANE:DM + (cbk + 1) * LANE] = (acc * v * s * (1.0 - s)).astype(BF16)

    return pl.pallas_call(
        body, name="conv_bwd", grid=(nblk,),
        in_specs=[_rows(tm, DM),
                  pl.BlockSpec((HALO, DM), lambda i: (jnp.minimum((i + 1) * hb, nblk * hb - 1), 0)),
                  _rows(tm, 2 * DM, gcol),
                  pl.BlockSpec((HALO, 2 * DM), lambda i: (jnp.maximum(i * hb - 1, 0), gcol)),
                  pl.BlockSpec((KW, DM), lambda i: (0, 0))],
        out_specs=[_rows(tm, 2 * DM), pl.BlockSpec((HALO, DM), lambda i: (0, 0)), _vec(DM)],
        out_shape=[jax.ShapeDtypeStruct((t, 2 * DM), BF16), jax.ShapeDtypeStruct((HALO, DM), F32),
                   jax.ShapeDtypeStruct((1, DM), F32)],
        scratch_shapes=[pltpu.VMEM((HALO + tm, DM), F32), pltpu.VMEM((tm + HALO, DM), F32),
                        pltpu.VMEM((SUB, HALO + tm, DM), F32), pltpu.VMEM((SUB, tm + HALO, DM), F32)],
        compiler_params=_cparams(("arbitrary",)))(dzc, dzc, proj, proj, cw)


CUM_ROWS = 512


def _log_sigmoid(x):
    return jnp.minimum(x, 0.0) - jnp.log1p(jnp.exp(-jnp.abs(x)))


def _fox_cum(proj, bfp):
    t = proj.shape[0]
    fcol = C_F // LANE
    cr = min(CUM_ROWS, t)

    def body(f_ref, bf_ref, cum_ref, carry_ref):
        @pl.when(pl.program_id(0) == 0)
        def _():
            carry_ref[...] = jnp.zeros_like(carry_ref)

        lf = _log_sigmoid(f_ref[...] + bf_ref[...])
        tri = (lax.broadcasted_iota(I32, (cr, cr), 0) >= lax.broadcasted_iota(I32, (cr, cr), 1)).astype(F32)
        cum = jnp.dot(tri, lf, preferred_element_type=F32, precision=lax.Precision.HIGHEST) + carry_ref[0:1, :]
        cum_ref[...] = cum
        carry_ref[...] = jnp.broadcast_to(cum[cr - 1:cr, :], carry_ref.shape)

    return pl.pallas_call(
        body, name="fox_cum", grid=(t // cr,), in_specs=[_rows(cr, LANE, fcol), _vec(LANE)],
        out_specs=_rows(cr, LANE), out_shape=jax.ShapeDtypeStruct((t, LANE), F32),
        scratch_shapes=[pltpu.VMEM((8, LANE), F32)],
        compiler_params=_cparams(("arbitrary",)))(proj, bfp)


def _fox_cum_bwd(dcum, proj, bfp):
    t = proj.shape[0]
    cr = min(CUM_ROWS, t)
    nb = t // cr
    fcol = C_F // LANE
    fw = D_INP - C_F

    def body(dc_ref, f_ref, bf_ref, df_ref, dbf_ref, carry_ref):
        @pl.when(pl.program_id(0) == 0)
        def _():
            carry_ref[...] = jnp.zeros_like(carry_ref)
            dbf_ref[...] = jnp.zeros_like(dbf_ref)

        triu = (lax.broadcasted_iota(I32, (cr, cr), 0) <= lax.broadcasted_iota(I32, (cr, cr), 1)).astype(F32)
        dlf = jnp.dot(triu, dc_ref[...], preferred_element_type=F32, precision=lax.Precision.HIGHEST) + carry_ref[0:1, :]
        carry_ref[...] = jnp.broadcast_to(dlf[0:1, :], carry_ref.shape)
        z = f_ref[...] + bf_ref[...]
        lane = lax.broadcasted_iota(I32, (cr, LANE), 1)
        df = jnp.where(lane < NG, dlf * _sigmoid(-z), 0.0)
        dbf_ref[...] += _colsum(df)
        df_ref[:, 0:LANE] = df.astype(BF16)
        df_ref[:, LANE:] = jnp.zeros((cr, fw - LANE), BF16)

    return pl.pallas_call(
        body, name="fox_cum_bwd", grid=(nb,),
        in_specs=[pl.BlockSpec((cr, LANE), lambda i: (nb - 1 - i, 0)),
                  pl.BlockSpec((cr, LANE), lambda i: (nb - 1 - i, fcol)), _vec(LANE)],
        out_specs=[pl.BlockSpec((cr, fw), lambda i: (nb - 1 - i, 0)), _vec(LANE)],
        out_shape=[jax.ShapeDtypeStruct((t, fw), BF16), jax.ShapeDtypeStruct((1, LANE), F32)],
        scratch_shapes=[pltpu.VMEM((8, LANE), F32)],
        compiler_params=_cparams(("arbitrary",)))(dcum, proj, bfp)


HD = 64
ATT_SCALE = 0.125
NEG = -1e30


def _qkv_prep(proj, *, tm=512):
    t = proj.shape[0]
    tm = min(tm, t)

    def body(q_ref, k_ref, v_ref, o_ref):
        o_ref[:, 0:DM] = (q_ref[...] * ATT_SCALE).astype(BF16)
        o_ref[:, DM:2 * DM] = k_ref[...].astype(BF16)
        o_ref[:, 2 * DM:] = v_ref[...].astype(BF16)

    return pl.pallas_call(
        body, name="qkv_prep", grid=(t // tm,),
        in_specs=[_rows(tm, DM, C_Q // DM), _rows(tm, DM, C_K // DM), _rows(tm, DM, C_V // DM)],
        out_specs=_rows(tm, 3 * DM), out_shape=jax.ShapeDtypeStruct((t, 3 * DM), BF16),
        compiler_params=_cparams(("parallel",)))(proj, proj, proj)


def _causal_pairs(nq, outer_is_query):
    if outer_is_query:
        pairs = [(i, j) for i in range(nq) for j in range(i + 1)]
    else:
        pairs = [(j, i) for j in range(nq) for i in range(j, nq)]
    return (jnp.asarray([p[0] for p in pairs], I32), jnp.asarray([p[1] for p in pairs], I32))


def _to_row(col):
    return jnp.transpose(col)[0:1, :]


def _rep(x, tk):
    return x if tk == LANE else jnp.tile(x, (1, tk // LANE))


def _attn_fwd(qkv, ckrow, *, tq=512):
    t = qkv.shape[0]
    tq = min(tq, t)
    tk = tq
    nq = t // tq
    oi, ij = _causal_pairs(nq, True)

    def body(oi_ref, ij_ref, q_ref, k_ref, v_ref, ck_ref, o_ref, lser_ref, m_ref, l_ref, acc_ref):
        n = pl.program_id(0)
        i, j = oi_ref[n], ij_ref[n]

        @pl.when(j == 0)
        def _():
            m_ref[...] = jnp.full_like(m_ref, NEG)
            l_ref[...] = jnp.zeros_like(l_ref)
            acc_ref[...] = jnp.zeros_like(acc_ref)

        def step(masked):
            if masked:
                keep = lax.broadcasted_iota(I32, (tq, tk), 1) <= lax.broadcasted_iota(I32, (tq, tk), 0)
            lo = lax.broadcasted_iota(I32, (tq, LANE), 1) < HD
            for hp in range(NG // 2):
                cs = slice(hp * LANE, (hp + 1) * LANE)
                qp, kp, vp = q_ref[:, cs], k_ref[:, cs], v_ref[:, cs]
                alphas, pvs = [], []
                for hh in range(2):
                    h = 2 * hp + hh
                    qm = jnp.where(lo if hh == 0 else jnp.logical_not(lo), qp, jnp.zeros_like(qp))
                    s = _dot(qm, kp, NT) - ck_ref[h:h + 1, :]
                    if masked:
                        s = jnp.where(keep, s, NEG)
                    m_prev = m_ref[h]
                    m_new = jnp.maximum(m_prev, jnp.max(s, axis=1, keepdims=True))
                    alpha = jnp.exp(m_prev - m_new)
                    p = jnp.exp(s - _rep(m_new, tk))
                    l_ref[h] = alpha * l_ref[h] + jnp.sum(p, axis=1, keepdims=True)
                    m_ref[h] = m_new
                    alphas.append(alpha)
                    pvs.append(_dot(p.astype(BF16), vp))
                acc_ref[:, cs] = jnp.where(lo, alphas[0], alphas[1]) * acc_ref[:, cs] + jnp.where(lo, pvs[0], pvs[1])

        @pl.when(j < i)
        def _():
            step(False)

        @pl.when(j == i)
        def _():
            step(True)
            lo = lax.broadcasted_iota(I32, (tq, LANE), 1) < HD
            for hp in range(NG // 2):
                cs = slice(hp * LANE, (hp + 1) * LANE)
                o_ref[:, cs] = (acc_ref[:, cs] / jnp.where(lo, l_ref[2 * hp], l_ref[2 * hp + 1])).astype(BF16)
            for h in range(NG):
                lser_ref[h:h + 1, :] = _to_row(m_ref[h] + jnp.log(l_ref[h]))

    gs = pltpu.PrefetchScalarGridSpec(
        num_scalar_prefetch=2, grid=(int(oi.shape[0]),),
        in_specs=[pl.BlockSpec((tq, DM), lambda n, a, b: (a[n], 0)),
                  pl.BlockSpec((tk, DM), lambda n, a, b: (b[n], 1)),
                  pl.BlockSpec((tk, DM), lambda n, a, b: (b[n], 2)),
                  pl.BlockSpec((NG, tk), lambda n, a, b: (0, b[n]))],
        out_specs=[pl.BlockSpec((tq, DM), lambda n, a, b: (a[n], 0)),
                   pl.BlockSpec((NG, tq), lambda n, a, b: (0, a[n]))],
        scratch_shapes=[pltpu.VMEM((NG, tq, LANE), F32), pltpu.VMEM((NG, tq, LANE), F32), pltpu.VMEM((tq, DM), F32)])
    return pl.pallas_call(
        body, name="attn_fwd", grid_spec=gs,
        out_shape=[jax.ShapeDtypeStruct((t, DM), BF16), jax.ShapeDtypeStruct((NG, t), F32)],
        compiler_params=_cparams(("arbitrary",)))(oi, ij, qkv, qkv, qkv, ckrow)


def _attn_delta(o, do, *, tq=512):
    t = o.shape[0]
    tq = min(tq, t)

    def body(o_ref, do_ref, d_ref):
        lo = lax.broadcasted_iota(I32, (tq, LANE), 1) < HD
        for hp in range(NG // 2):
            cs = slice(hp * LANE, (hp + 1) * LANE)
            prod = do_ref[:, cs] * o_ref[:, cs].astype(F32)
            for hh in range(2):
                d = jnp.sum(jnp.where(lo if hh == 0 else jnp.logical_not(lo), prod, 0.0), axis=1, keepdims=True)
                d_ref[2 * hp + hh:2 * hp + hh + 1, :] = _to_row(jnp.broadcast_to(d, (tq, LANE)))

    return pl.pallas_call(
        body, name="attn_delta", grid=(t // tq,),
        in_specs=[pl.BlockSpec((tq, DM), lambda i: (i, 0)), pl.BlockSpec((tq, DM), lambda i: (i, 0))],
        out_specs=pl.BlockSpec((NG, tq), lambda i: (0, i)), out_shape=jax.ShapeDtypeStruct((NG, t), F32),
        compiler_params=_cparams(("parallel",)))(o, do)


def _attn_bwd(qkv, ckcol, do, lserow, deltarow, *, tq=512):
    t = qkv.shape[0]
    tq = min(tq, t)
    tk = tq
    nq = t // tq
    oj, ii = _causal_pairs(nq, False)
    npairs = int(oj.shape[0])

    def body(oj_ref, ii_ref, q_ref, k_ref, v_ref, ck_ref, do_ref, lse_ref, delta_ref, dq_ref, dk_ref, dv_ref, dcq_ref,
             dckr_ref, dqa_ref, dka_ref, dva_ref, dck_ref):
        n = pl.program_id(0)
        j, i = oj_ref[n], ii_ref[n]

        @pl.when(n == 0)
        def _():
            dqa_ref[...] = jnp.zeros_like(dqa_ref)
            dcq_ref[...] = jnp.zeros_like(dcq_ref)

        @pl.when(i == j)
        def _():
            dka_ref[...] = jnp.zeros_like(dka_ref)
            dva_ref[...] = jnp.zeros_like(dva_ref)
            dck_ref[...] = jnp.zeros_like(dck_ref)

        def step(masked):
            if masked:
                keep = lax.broadcasted_iota(I32, (tk, tq), 0) <= lax.broadcasted_iota(I32, (tk, tq), 1)
            lo = lax.broadcasted_iota(I32, (tk, LANE), 1) < HD
            for hp in range(NG // 2):
                cs = slice(hp * LANE, (hp + 1) * LANE)
                qp, kp, vp, dop = q_ref[:, cs], k_ref[:, cs], v_ref[:, cs], do_ref[:, cs].astype(BF16)
                dvs, dks, dqs = [], [], []
                for hh in range(2):
                    h = 2 * hp + hh
                    sel = lo if hh == 0 else jnp.logical_not(lo)
                    st = _dot(jnp.where(sel, kp, jnp.zeros_like(kp)), qp, NT) - _rep(ck_ref[h], tq)
                    if masked:
                        st = jnp.where(keep, st, NEG)
                    pt = jnp.exp(st - lse_ref[h:h + 1, :])
                    dvs.append(_dot(pt.astype(BF16), dop))
                    dpt = _dot(jnp.where(sel, vp, jnp.zeros_like(vp)), dop, NT)
                    dst = pt * (dpt - delta_ref[h:h + 1, :])
                    dsb = dst.astype(BF16)
                    dks.append(_dot(dsb, qp))
                    dqs.append(_dot(dsb, kp, TN))
                    dck_ref[h] -= jnp.sum(dst, axis=1, keepdims=True)
                    dcq_ref[i, h:h + 1, :] += jnp.sum(dst, axis=0, keepdims=True)
                dva_ref[:, cs] += jnp.where(lo, dvs[0], dvs[1])
                dka_ref[:, cs] += jnp.where(lo, dks[0], dks[1])
                dqa_ref[i, :, cs] += jnp.where(lo, dqs[0], dqs[1])

        @pl.when(i == j)
        def _():
            step(True)

        @pl.when(i > j)
        def _():
            step(False)

        @pl.when(i == nq - 1)
        def _():
            dk_ref[...] = dka_ref[...].astype(BF16)
            dv_ref[...] = dva_ref[...].astype(BF16)
            for h in range(NG):
                dckr_ref[h:h + 1, :] = _to_row(dck_ref[h])

        @pl.when(n == npairs - 1)
        def _():
            dq_ref[...] = (dqa_ref[...] * ATT_SCALE).astype(BF16)

    gs = pltpu.PrefetchScalarGridSpec(
        num_scalar_prefetch=2, grid=(npairs,),
        in_specs=[pl.BlockSpec((tq, DM), lambda n, a, b: (b[n], 0)),
                  pl.BlockSpec((tk, DM), lambda n, a, b: (a[n], 1)),
                  pl.BlockSpec((tk, DM), lambda n, a, b: (a[n], 2)),
                  pl.BlockSpec((NG, tk, LANE), lambda n, a, b: (0, a[n], 0)),
                  pl.BlockSpec((tq, DM), lambda n, a, b: (b[n], 0)),
                  pl.BlockSpec((NG, tq), lambda n, a, b: (0, b[n])),
                  pl.BlockSpec((NG, tq), lambda n, a, b: (0, b[n]))],
        out_specs=[pl.BlockSpec((nq, tq, DM), lambda n, a, b: (0, 0, 0)),
                   pl.BlockSpec((tk, DM), lambda n, a, b: (a[n], 0)),
                   pl.BlockSpec((tk, DM), lambda n, a, b: (a[n], 0)),
                   pl.BlockSpec((nq, NG, tq), lambda n, a, b: (0, 0, 0)),
                   pl.BlockSpec((NG, tk), lambda n, a, b: (0, a[n]))],
        scratch_shapes=[pltpu.VMEM((nq, tq, DM), F32), pltpu.VMEM((tk, DM), F32), pltpu.VMEM((tk, DM), F32),
                        pltpu.VMEM((NG, tk, LANE), F32)])
    return pl.pallas_call(
        body, name="attn_bwd", grid_spec=gs,
        out_shape=[jax.ShapeDtypeStruct((nq, tq, DM), BF16), jax.ShapeDtypeStruct((t, DM), BF16),
                   jax.ShapeDtypeStruct((t, DM), BF16), jax.ShapeDtypeStruct((nq, NG, tq), F32),
                   jax.ShapeDtypeStruct((NG, t), F32)],
        compiler_params=_cparams(("arbitrary",)))(oj, ii, qkv, qkv, qkv, ckcol, do, lserow, deltarow)


def _merge_fwd(ga, zb, att, proj, wa, wb, wc, *, tm=256):
    t = ga.shape[0]
    tm = min(tm, t)
    wspec = pl.BlockSpec((DM, D), lambda i: (0, 0))

    def body(ga_ref, zb_ref, att_ref, gate_ref, wa_ref, wb_ref, wc_ref, m_ref):
        acc = jnp.zeros((tm, D), F32)
        for b, (x_ref, w_ref) in enumerate(((ga_ref, wa_ref), (zb_ref, wb_ref), (att_ref, wc_ref))):
            acc = acc + _sigmoid(gate_ref[:, b * D:(b + 1) * D]) * _dot(x_ref[...], w_ref[...])
        m_ref[...] = acc.astype(BF16)

    return pl.pallas_call(
        body, name="merge_fwd", grid=(t // tm,),
        in_specs=[_rows(tm, DM), _rows(tm, DM), _rows(tm, DM), _rows(tm, 3 * D, 0), wspec, wspec, wspec],
        out_specs=_rows(tm, D), out_shape=jax.ShapeDtypeStruct((t, D), BF16),
        compiler_params=_cparams(("parallel",)))(ga, zb, att, proj, wa, wb, wc)


def _merge_bwd(dm, ga, zb, att, proj, wa, wb, wc, *, tm=256):
    t = ga.shape[0]
    tm = min(tm, t)
    wspec = pl.BlockSpec((DM, D), lambda i: (0, 0))

    def body(dm_ref, ga_ref, zb_ref, att_ref, gate_ref, wa_ref, wb_ref, wc_ref, dgate_ref, dga_ref, dzb_ref, datt_ref,
             dwa_ref, dwb_ref, dwc_ref):
        @pl.when(pl.program_id(0) == 0)
        def _():
            dwa_ref[...] = jnp.zeros_like(dwa_ref)
            dwb_ref[...] = jnp.zeros_like(dwb_ref)
            dwc_ref[...] = jnp.zeros_like(dwc_ref)

        dmv = dm_ref[...]
        branches = ((ga_ref, wa_ref, dga_ref, dwa_ref), (zb_ref, wb_ref, dzb_ref, dwb_ref),
                    (att_ref, wc_ref, datt_ref, dwc_ref))
        for b, (x_ref, w_ref, dx_ref, dw_ref) in enumerate(branches):
            xv, wv = x_ref[...], w_ref[...]
            y = _dot(xv, wv)
            g = _sigmoid(gate_ref[:, b * D:(b + 1) * D])
            dgate_ref[:, b * D:(b + 1) * D] = (dmv * y * g * (1.0 - g)).astype(BF16)
            dy = (dmv * g).astype(BF16)
            dx_ref[...] = _dot(dy, wv, NT)
            dw_ref[...] += _dot(xv, dy, TN)

    return pl.pallas_call(
        body, name="merge_bwd", grid=(t // tm,),
        in_specs=[_rows(tm, D), _rows(tm, DM), _rows(tm, DM), _rows(tm, DM), _rows(tm, 3 * D, 0), wspec, wspec, wspec],
        out_specs=[_rows(tm, 3 * D), _rows(tm, DM), _rows(tm, DM), _rows(tm, DM), wspec, wspec, wspec],
        out_shape=[jax.ShapeDtypeStruct((t, 3 * D), BF16)] + [jax.ShapeDtypeStruct((t, DM), F32)] * 3
        + [jax.ShapeDtypeStruct((DM, D), F32)] * 3,
        compiler_params=_cparams(("arbitrary",)))(dm, ga, zb, att, proj, wa, wb, wc)


def _heads_layout(cum):
    t = cum.shape[0]
    ckrow = cum[:, :NG].T
    return ckrow, jnp.broadcast_to(ckrow[:, :, None], (NG, t, LANE))


def _layer_fwd(x, mod, w, late=None):
    sh1, sc1, gt1, sh2, sc2, gt2 = (mod[k:k + 1] for k in range(NMOD))
    h1 = _norm_mod(x, w["mix_pre_g"], sc1, sh1, name="norm_mix")
    proj = _matmul(h1, w["w_in_p"], name="mm_proj")
    ga = _gmlp_fwd(proj, w["gmlp_ln_g"], w["gmlp_ln_b"], w["wsm"], w["bsx"])
    zc, zb = _conv_fwd(proj, w["conv_w"], w["conv_b"], w["conv_ln_g"], w["conv_ln_b"])
    cum = _fox_cum(proj, w["bfp"])
    ckrow, ckcol = _heads_layout(cum)
    qkv = _qkv_prep(proj)
    att, lser = _attn_fwd(qkv, ckrow)
    if late is not None:
        w = {**w, **late(att)}
    merged = _merge_fwd(ga, zb, att, proj, w["w_a_out"], w["w_b_out"], w["w_c_out"])
    y1 = _matmul(merged, w["w_out"], name="mm_out")
    x2 = _resid(x, y1, gt1, w["mix_post_g"], name="resid_mix")
    h2 = _norm_mod(x2, w["mlp_pre_g"], sc2, sh2, name="norm_mlp")
    a, hid = _matmul(h2, w["mlp_w1"], name="mm_w1", b_slabs=True, extra_out=(BF16,),
                     epilogue=lambda acc: (acc, jnp.square(jnp.maximum(acc, 0.0))))
    y2 = _matmul(hid, w["mlp_w2"], name="mm_w2", tk=TK_DEEP)
    x3 = _resid(x2, y2, gt2, w["mlp_post_g"], name="resid_mlp")
    saved = dict(x=x, h1=h1, proj=proj, ga=ga, zc=zc, zb=zb, qkv=qkv, ckrow=ckrow, ckcol=ckcol, att=att, lser=lser, merged=merged,
                 y1=y1, x2=x2, h2=h2, a=a, hid=hid, y2=y2)
    return x3, saved, w


def _layer_bwd(dx3, mod, w, s, mid=None):
    sh1, sc1, gt1, sh2, sc2, gt2 = (mod[k:k + 1] for k in range(NMOD))
    g = {}
    dy2, dgt2, g["mlp_post_g"] = _resid_bwd(dx3, s["y2"], gt2, w["mlp_post_g"], name="resid_mlp_bwd")
    da = _matmul(dy2, w["mlp_w2"], tb=True, name="mm_dhid", out_dtype=BF16, extra=(s["a"],),
                 epilogue=lambda acc, a: (acc * (2.0 * jnp.maximum(a, 0.0)),))
    g["mlp_w2"] = _matmul(s["hid"], dy2, ta=True, name="mm_dw2", tk=TK_DEEP)
    g["mlp_w1"] = _matmul(s["h2"], da, ta=True, name="mm_dw1", out_slabs=N_CHIPS, tk=TK_DEEP)
    dh2 = _matmul(da, w["mlp_w1"], tb=True, name="mm_dh2", b_slabs=True, tk=TK_DEEP)
    dx2, g["mlp_pre_g"], dsc2, dsh2 = _norm_bwd(dh2, dx3, s["x2"], w["mlp_pre_g"], sc2, name="norm_mlp_bwd")
    if mid is not None:
        gt1 = gt1 + mid(dx2)
    dy1, dgt1, g["mix_post_g"] = _resid_bwd(dx2, s["y1"], gt1, w["mix_post_g"], name="resid_mix_bwd")
    dmerged = _matmul(dy1, w["w_out"], tb=True, name="mm_dmerged")
    g["w_out"] = _matmul(s["merged"], dy1, ta=True, name="mm_dwout", tk=TK_DEEP)
    dgate, dga, dzb, datt, g["w_a_out"], g["w_b_out"], g["w_c_out"] = _merge_bwd(
        dmerged, s["ga"], s["zb"], s["att"], s["proj"], w["w_a_out"], w["w_b_out"], w["w_c_out"])
    duv, g["gmlp_ws"], dbs, g["gmlp_ln_g"], g["gmlp_ln_b"] = _gmlp_bwd(
        dga, s["proj"], w["gmlp_ln_g"], w["gmlp_ln_b"], w["wsm"], w["wsmt"], w["bsx"])
    g["gmlp_bs"] = dbs[:, :NG].T
    dzc, g["conv_ln_g"], g["conv_ln_b"] = _conv_bwd_ln(dzb, s["zc"], w["conv_ln_g"], w["conv_ln_b"])
    dglu, dcw, g["conv_b"] = _conv_bwd(dzc, s["proj"], w["conv_w"])
    g["conv_w"] = dcw[:KW]
    t = dx3.shape[0]
    dq, dk, dv, dcq, dck = _attn_bwd(s["qkv"], s["ckcol"], datt, s["lser"], _attn_delta(s["att"], datt))
    dq = dq.reshape(t, DM)
    dcum = jnp.pad((dcq.transpose(1, 0, 2).reshape(NG, t) + dck).T, ((0, 0), (0, LANE - NG)))
    df, dbf = _fox_cum_bwd(dcum, s["proj"], w["bfp"])
    g["fox_bf"] = dbf[0, :NG]
    dproj = jnp.concatenate([dgate, duv, dglu, dq, dk, dv, df], axis=1)
    g["w_in_p"] = _matmul(s["h1"], dproj, ta=True, name="mm_dwin", tk=TK_DEEP)
    dh1 = _matmul(dproj, w["w_in_p"], tb=True, name="mm_dh1", tk=D_INP // 2)
    dx, g["mix_pre_g"], dsc1, dsh1 = _norm_bwd(dh1, dx2, s["x"], w["mix_pre_g"], sc1, name="norm_mix_bwd")
    dmod = jnp.concatenate([dsh1, dsc1, dgt1, dsh2, dsc2, dgt2], axis=0)
    return dx, g, dmod


def _position():
    return lax.axis_index("x"), lax.axis_index("y"), lax.axis_index("c")


def _all_gather8(v):
    m_per, n = v.shape

    def body(x_ref, out_ref, send_sems, recv_sems, local_sem):
        x, y, c = _position()
        me, sibling = (x, y, c), (x, y, 1 - c)
        chips = [(1 - x, y), (x, 1 - y), (1 - x, 1 - y)]

        def rows(px, py, pc):
            return out_ref.at[pl.ds((4 * px + 2 * py + pc) * m_per, m_per), :]

        def copy(k, block, to, src=None):
            return pltpu.make_async_remote_copy(
                src_ref=rows(*block) if src is None else src, dst_ref=rows(*block), send_sem=send_sems.at[k],
                recv_sem=recv_sems.at[k], device_id=to, device_id_type=MESH)

        mine = pltpu.make_async_copy(x_ref, rows(*me), local_sem)
        mine.start()
        first = [copy(0, me, sibling, src=x_ref)]
        first += [copy(1 + j, me, (*chip, c), src=x_ref) for j, chip in enumerate(chips)]
        for cp in first:
            cp.start()
        passed = [copy(4 + j, (*chip, c), sibling) for j, chip in enumerate(chips)]
        for j, chip in enumerate(chips):
            copy(1 + j, (*chip, c), me).wait_recv()
            passed[j].start()
        copy(0, sibling, me).wait_recv()
        for j, chip in enumerate(chips):
            copy(4 + j, (*chip, 1 - c), me).wait_recv()
        for cp in first + passed:
            cp.wait_send()
        mine.wait()

    out = pl.pallas_call(
        body, name="all_gather8", out_shape=jax.ShapeDtypeStruct((N_DEV * m_per, n), v.dtype),
        in_specs=[pl.BlockSpec(memory_space=pltpu.VMEM)], out_specs=pl.BlockSpec(memory_space=pltpu.VMEM),
        scratch_shapes=[pltpu.SemaphoreType.DMA((7,)), pltpu.SemaphoreType.DMA((7,)), pltpu.SemaphoreType.DMA],
        compiler_params=pltpu.CompilerParams(vmem_limit_bytes=VMEM_LIMIT),
    )(v)
    return out.reshape(N_DEV, m_per, n)


def _half(c, rows):
    return pl.ds(c * (rows // 2), rows // 2)


HBM = pl.BlockSpec(memory_space=pltpu.HBM)
SEM = pl.BlockSpec(memory_space=pltpu.SEMAPHORE)
EFFECT = pltpu.SideEffectType.DATAFLOW_SIDE_EFFECTING


_COPIES_PER_ARRAY = {"gather": 3, "scatter": 3, "swap": N_CHIPS}


def _ici_copies(kind, src_refs, land_refs, send_sems, recv_sems, arriving):
    x, y, c = _position()
    chips = [(1 - x, y), (x, 1 - y), (1 - x, 1 - y)]
    nper = _COPIES_PER_ARRAY[kind]
    copies = []
    for wi, (src, land) in enumerate(zip(src_refs, land_refs)):
        for k in range(nper):
            if kind == "swap":
                peer = (x, y, 1 - c)
                s_win, dst = src.at[k, _half(1 - c, src.shape[1])], land.at[k]
            else:
                px, py = chips[k]
                peer = (px, py, c)
                if kind == "gather":
                    rows = src.shape[0]
                    s_win = src.at[_half(c, rows)]
                    dst = land.at[2 * px + py if arriving else 2 * x + y, _half(c, rows)]
                else:
                    s_win, dst = src.at[2 * px + py], land.at[k]
            copies.append(pltpu.make_async_remote_copy(
                src_ref=s_win, dst_ref=dst, send_sem=send_sems.at[wi * nper + k], recv_sem=recv_sems.at[wi * nper + k],
                device_id=peer, device_id_type=MESH))
    return copies


def _ici_start(kind, srcs, land_shapes, after, *, name):
    nw = len(srcs)

    def body(*refs):
        src_refs, land_refs = refs[:nw], refs[nw:2 * nw]
        send_sems, recv_sems = refs[2 * nw + 1:2 * nw + 3]
        token = refs[-1]
        for cp in _ici_copies(kind, src_refs, land_refs, send_sems, recv_sems, False):
            cp.start()
        token[...] = jnp.zeros_like(token)

    lands = [pltpu.with_memory_space_constraint(lax.empty(shp, s.dtype), pltpu.HBM) for shp, s in zip(land_shapes, srcs)]
    outs = pl.pallas_call(
        body, name=name,
        out_shape=(pltpu.SemaphoreType.DMA((_COPIES_PER_ARRAY[kind] * nw,)), pltpu.SemaphoreType.DMA((_COPIES_PER_ARRAY[kind] * nw,)),
                   *[pltpu.HBM(s.shape, s.dtype) for s in srcs], *[pltpu.HBM(shp, s.dtype) for shp, s in zip(land_shapes, srcs)],
                   jax.ShapeDtypeStruct((8, LANE), F32)),
        in_specs=[HBM] * (2 * nw) + [ANY], out_specs=(SEM, SEM, *[HBM] * (2 * nw), pl.BlockSpec(memory_space=pltpu.VMEM)),
        input_output_aliases={i: 2 + i for i in range(2 * nw)},
        compiler_params=pltpu.CompilerParams(has_side_effects=EFFECT),
    )(*[pltpu.with_memory_space_constraint(s, pltpu.HBM) for s in srcs], *lands, after)
    return outs[0], outs[1], outs[2:2 + nw], outs[2 + nw:2 + 2 * nw], outs[-1]


def _ici_wait(kind, send_sems, recv_sems, srcs, lands, after, *, name):
    nw = len(srcs)

    def body(*refs):
        src_refs, land_refs = refs[:nw], refs[nw:2 * nw]
        s_sems, r_sems = refs[2 * nw:2 * nw + 2]
        for cp in _ici_copies(kind, src_refs, land_refs, s_sems, r_sems, False):
            cp.wait_send()
        for cp in _ici_copies(kind, src_refs, land_refs, s_sems, r_sems, True):
            cp.wait_recv()

    outs = pl.pallas_call(
        body, name=name,
        out_shape=(*[pltpu.HBM(s.shape, s.dtype) for s in srcs], *[pltpu.HBM(a.shape, a.dtype) for a in lands]),
        in_specs=[HBM] * (2 * nw) + [SEM, SEM, ANY], out_specs=tuple([HBM] * (2 * nw)),
        input_output_aliases={i: i for i in range(2 * nw)},
        compiler_params=pltpu.CompilerParams(has_side_effects=EFFECT),
    )(*srcs, *lands, send_sems, recv_sems, after)
    return outs[:nw], outs[nw:]


def _ag_d2d(lands, *, name):
    nw = len(lands)

    def body(*refs):
        out_refs = refs[nw:2 * nw]
        send_sems, recv_sems = refs[2 * nw:]
        x, y, c = _position()
        chips = [(1 - x, y), (x, 1 - y), (1 - x, 1 - y)]
        copies = []
        for wi in range(nw):
            rows = out_refs[wi].shape[1]
            for k, (px, py) in enumerate(chips):
                win = out_refs[wi].at[2 * px + py, _half(c, rows)]
                copies.append(pltpu.make_async_remote_copy(
                    src_ref=win, dst_ref=win, send_sem=send_sems.at[wi * 3 + k], recv_sem=recv_sems.at[wi * 3 + k],
                    device_id=(x, y, 1 - c), device_id_type=MESH))
        for cp in copies:
            cp.start()
        for wi in range(nw):
            rows = out_refs[wi].shape[1]
            for k, (px, py) in enumerate(chips):
                win = out_refs[wi].at[2 * px + py, _half(1 - c, rows)]
                pltpu.make_async_remote_copy(
                    src_ref=win, dst_ref=win, send_sem=send_sems.at[wi * 3 + k], recv_sem=recv_sems.at[wi * 3 + k],
                    device_id=(x, y, 1 - c), device_id_type=MESH).wait_recv()
        for cp in copies:
            cp.wait_send()

    return pl.pallas_call(
        body, name=name, out_shape=[jax.ShapeDtypeStruct(a.shape, a.dtype) for a in lands],
        in_specs=[ANY] * nw, out_specs=[ANY] * nw, input_output_aliases={i: i for i in range(nw)},
        scratch_shapes=[pltpu.SemaphoreType.DMA((3 * nw,)), pltpu.SemaphoreType.DMA((3 * nw,))],
    )(*lands)


def _swap_reduced(reds):
    nw = len(reds)

    def body(*refs):
        r_refs, out_refs = refs[:nw], refs[nw:2 * nw]
        send_sems, recv_sems = refs[2 * nw:]
        x, y, c = _position()
        copies = [pltpu.make_async_remote_copy(
            src_ref=r_refs[wi], dst_ref=out_refs[wi], send_sem=send_sems.at[wi], recv_sem=recv_sems.at[wi],
            device_id=(x, y, 1 - c), device_id_type=MESH) for wi in range(nw)]
        for cp in copies:
            cp.start()
        for cp in copies:
            cp.wait()

    return pl.pallas_call(
        body, name="rs_swap_reduced", out_shape=[jax.ShapeDtypeStruct(r.shape, r.dtype) for r in reds],
        in_specs=[ANY] * nw, out_specs=[ANY] * nw,
        scratch_shapes=[pltpu.SemaphoreType.DMA((nw,)), pltpu.SemaphoreType.DMA((nw,))],
    )(*reds)


def _row_tile(rows, cols):
    tr = rows
    while tr * cols * 4 > (2 << 20) and tr % 32 == 0:
        tr //= 2
    return tr


def _add_own_half(g, recv, c, *, name):
    nj, h, w = recv.shape
    tr = _row_tile(h, w)
    nb = h // tr

    def body(c_ref, g_ref, r_ref, o_ref, ob_ref):
        sm = g_ref[0] + r_ref[0]
        o_ref[0] = sm
        ob_ref[0] = sm.astype(BF16)

    spec = pl.BlockSpec((1, tr, w), lambda j, i, cc: (j, i, 0))
    gs = pltpu.PrefetchScalarGridSpec(
        num_scalar_prefetch=1, grid=(nj, nb),
        in_specs=[pl.BlockSpec((1, tr, w), lambda j, i, cc: (j, cc[0] * nb + i, 0)), spec],
        out_specs=[spec, spec])
    return pl.pallas_call(body, name=name, grid_spec=gs,
                          out_shape=[jax.ShapeDtypeStruct((nj, h, w), F32), jax.ShapeDtypeStruct((nj, h, w), BF16)],
                          compiler_params=_cparams(("parallel", "parallel")))(jnp.reshape(c, (1,)).astype(I32), g, recv)


def _add_own_chip(sp, recv, j, *, name):
    _, r, w = sp.shape
    tr = _row_tile(r, w)

    def body(j_ref, s_ref, r_ref, o_ref):
        o_ref[...] = ((s_ref[0] + r_ref[0].astype(F32)) + r_ref[1].astype(F32)) + r_ref[2].astype(F32)

    gs = pltpu.PrefetchScalarGridSpec(
        num_scalar_prefetch=1, grid=(r // tr,),
        in_specs=[pl.BlockSpec((1, tr, w), lambda i, jj: (jj[0], i, 0)), pl.BlockSpec((3, tr, w), lambda i, jj: (0, i, 0))],
        out_specs=pl.BlockSpec((tr, w), lambda i, jj: (i, 0)))
    return pl.pallas_call(body, name=name, grid_spec=gs, out_shape=jax.ShapeDtypeStruct((r, w), F32),
                          compiler_params=_cparams(("parallel",)))(jnp.reshape(j, (1,)).astype(I32), sp, recv)


def _sum8(v):
    _, m, n = v.shape

    def body(v_ref, o_ref):
        acc = v_ref[0]
        for k in range(1, N_DEV):
            acc = acc + v_ref[k]
        o_ref[...] = acc

    return pl.pallas_call(body, name="sum8", grid=(1,), in_specs=[pl.BlockSpec((N_DEV, m, n), lambda i: (0, 0, 0))],
                          out_specs=pl.BlockSpec((m, n), lambda i: (0, 0)), out_shape=jax.ShapeDtypeStruct((m, n), F32),
                          compiler_params=_cparams(("arbitrary",)))(v)


def _ada_mod(c_all, ada_w, ada_b_loc, *, tn=512):
    nl, _, ncol = ada_w.shape

    def body(c_ref, w_ref, b_ref, o_ref):
        cv = c_ref[...]
        ca = (cv * _sigmoid(cv)).astype(BF16)
        o_ref[0] = _dot(ca, w_ref[0].astype(BF16)) + b_ref[0]

    return pl.pallas_call(
        body, name="ada_mod", grid=(nl, ncol // tn),
        in_specs=[pl.BlockSpec((N_DEV, D), lambda l, j: (0, 0)), pl.BlockSpec((1, D, tn), lambda l, j: (l, 0, j)),
                  pl.BlockSpec((1, 1, tn), lambda l, j: (l, 0, j))],
        out_specs=pl.BlockSpec((1, N_DEV, tn), lambda l, j: (l, 0, j)),
        out_shape=jax.ShapeDtypeStruct((nl, N_DEV, ncol), F32),
        compiler_params=_cparams(("parallel", "parallel")))(c_all, ada_w, ada_b_loc)


def _ada_grad(c_pad, dmod_pad, *, tn=512):
    nl, nb, ncol = dmod_pad.shape

    def body(c_ref, d_ref, o_ref):
        cv = c_ref[...]
        ca = (cv * _sigmoid(cv)).astype(BF16)
        o_ref[0] = _dot(ca, d_ref[0].astype(BF16), TN)

    return pl.pallas_call(
        body, name="ada_grad", grid=(nl, ncol // tn),
        in_specs=[pl.BlockSpec((nb, D), lambda l, j: (0, 0)), pl.BlockSpec((1, nb, tn), lambda l, j: (l, 0, j))],
        out_specs=pl.BlockSpec((1, D, tn), lambda l, j: (l, 0, j)),
        out_shape=jax.ShapeDtypeStruct((nl, D, ncol), F32),
        compiler_params=_cparams(("parallel", "parallel")))(c_pad, dmod_pad)


def _adamw(w, g, m, v, *, name):
    shape = w.shape
    if w.ndim == 3:
        lead, rows, cols = shape
    else:
        lead, (rows, cols) = 1, shape
    w3, g3, m3, v3 = (a.reshape(lead, rows, cols) for a in (w, g, m, v))
    tr = rows
    if rows * cols * 4 > (2 << 20):
        tr = next(cand for cand in (256, 128, 64, 8) if rows % cand == 0)
    c1 = 1.0 - ADAM_B1 ** ADAM_STEP
    c2 = 1.0 - ADAM_B2 ** ADAM_STEP

    def body(w_ref, g_ref, m_ref, v_ref, d_ref, nm_ref, nv_ref):
        gv = g_ref[...]
        nm = ADAM_B1 * m_ref[...] + (1.0 - ADAM_B1) * gv
        nv = ADAM_B2 * v_ref[...] + (1.0 - ADAM_B2) * (gv * gv)
        nm_ref[...] = nm
        nv_ref[...] = nv
        d_ref[...] = -ADAM_LR * ((nm / c1) / (jnp.sqrt(nv / c2) + ADAM_EPS) + ADAM_WD * w_ref[...])

    spec = pl.BlockSpec((1, tr, cols), lambda l, i: (l, i, 0))
    outs = pl.pallas_call(
        body, name=name, grid=(lead, rows // tr), in_specs=[spec] * 4, out_specs=[spec] * 3,
        out_shape=[jax.ShapeDtypeStruct((lead, rows, cols), F32)] * 3,
        compiler_params=_cparams(("parallel", "parallel")))(w3, g3, m3, v3)
    return tuple(o.reshape(shape) for o in outs)


def _adamw_layer(w, g_own, g_sib, m, v, c, layer, prev, after, *, name):
    _, rows, cols = w.shape
    tr = _row_tile(rows // 2, cols)
    nbh = rows // 2 // tr
    c1 = 1.0 - ADAM_B1 ** ADAM_STEP
    c2 = 1.0 - ADAM_B2 ** ADAM_STEP
    n_prev = 0 if prev is None else 4

    def body(c_ref, w_ref, o_ref, s_ref, m_ref, v_ref, *rest):
        g_ref, d_ref, nm_ref, nv_ref = rest[n_prev + 1:]
        gv = jnp.where(pl.program_id(0) // nbh == c_ref[0], o_ref[...], s_ref[...])
        nm = ADAM_B1 * m_ref[0] + (1.0 - ADAM_B1) * gv
        nv = ADAM_B2 * v_ref[0] + (1.0 - ADAM_B2) * (gv * gv)
        g_ref[0] = gv
        nm_ref[0] = nm
        nv_ref[0] = nv
        d_ref[0] = -ADAM_LR * ((nm / c1) / (jnp.sqrt(nv / c2) + ADAM_EPS) + ADAM_WD * w_ref[0])

    def source(own):
        return pl.BlockSpec((tr, cols), lambda i, cc: (jnp.where((i // nbh == cc[0]) == own, i % nbh, 0), 0))

    spec = pl.BlockSpec((1, tr, cols), lambda i, cc: (layer, i, 0))
    gs = pltpu.PrefetchScalarGridSpec(
        num_scalar_prefetch=1, grid=(2 * nbh,),
        in_specs=[spec, source(True), source(False), spec, spec] + [ANY] * (n_prev + 1), out_specs=[spec] * 4)
    return pl.pallas_call(
        body, name=name, grid_spec=gs, out_shape=[jax.ShapeDtypeStruct(w.shape, F32)] * 4,
        input_output_aliases={6 + k: k for k in range(n_prev)},
        compiler_params=_cparams(("parallel",)))(
            jnp.reshape(c, (1,)).astype(I32), w, g_own, g_sib, m, v, *(prev or ()), after)


SMALL = (("mix_pre_g", (2, D)), ("mix_post_g", (2, D)), ("mlp_pre_g", (2, D)), ("mlp_post_g", (2, D)),
         ("gmlp_ln_g", (2, DM)), ("gmlp_ln_b", (2, DM)), ("gmlp_ws", (2, NG, CH, CH)), ("gmlp_bs", (2, NG, CH)),
         ("conv_b", (2, DM)), ("conv_ln_g", (2, DM)), ("conv_ln_b", (2, DM)), ("fox_bf", (2, NG)))


def _pack_rows(arrays):
    parts = []
    for a in arrays:
        last = a.shape[-1]
        r = a.astype(F32).reshape(-1, LANE) if last % LANE == 0 else jnp.pad(a.astype(F32).reshape(-1, last), ((0, 0), (0, LANE - last)))
        pad = (-r.shape[0]) % 8
        parts.append(jnp.pad(r, ((0, pad), (0, 0))) if pad else r)
    return jnp.concatenate(parts, axis=0)


def _unpack_rows(buf, shapes):
    out, off = [], 0
    for shp in shapes:
        size = 1
        for d in shp:
            size *= d
        last = shp[-1]
        rows = size // (LANE if last % LANE == 0 else last)
        seg = buf[off:off + rows]
        out.append(seg.reshape(shp) if last % LANE == 0 else seg[:, :last].reshape(shp))
        off += rows + (-rows) % 8
    return out


def _assemble_w_in(w_in_full):
    uv_glu_qkv = w_in_full[:, :3584]
    f = w_in_full[:, 3584:3592]
    gate = w_in_full[:, 3592:]
    fpad = jnp.zeros((D, D_INP - C_F - NG), w_in_full.dtype)
    return jnp.concatenate([gate, uv_glu_qkv, f, fpad], axis=1)


def _disassemble_w_in(g_p):
    return jnp.concatenate([g_p[:, C_UV:C_F], g_p[:, C_F:C_F + NG], g_p[:, :C_UV]], axis=1)


def kernel(x, c, ada_w, ada_b, mix_pre_g, mix_post_g, mlp_pre_g, mlp_post_g, w_in, gmlp_ln_g, gmlp_ln_b, gmlp_ws, gmlp_bs, w_a_out, conv_w, conv_b, conv_ln_g, conv_ln_b, w_b_out, fox_bf, w_c_out, w_out, mlp_w1, mlp_w2, loss_target, m_ada_w, m_ada_b, m_mix_pre_g, m_mix_post_g, m_mlp_pre_g, m_mlp_post_g, m_w_in, m_gmlp_ln_g, m_gmlp_ln_b, m_gmlp_ws, m_gmlp_bs, m_w_a_out, m_conv_w, m_conv_b, m_conv_ln_g, m_conv_ln_b, m_w_b_out, m_fox_bf, m_w_c_out, m_w_out, m_mlp_w1, m_mlp_w2, v_ada_w, v_ada_b, v_mix_pre_g, v_mix_post_g, v_mlp_pre_g, v_mlp_post_g, v_w_in, v_gmlp_ln_g, v_gmlp_ln_b, v_gmlp_ws, v_gmlp_bs, v_w_a_out, v_conv_w, v_conv_b, v_conv_ln_g, v_conv_ln_b, v_w_b_out, v_fox_bf, v_w_c_out, v_w_out, v_mlp_w1, v_mlp_w2):
    weights = dict(ada_w=ada_w, ada_b=ada_b, mix_pre_g=mix_pre_g, mix_post_g=mix_post_g, mlp_pre_g=mlp_pre_g,
                   mlp_post_g=mlp_post_g, w_in=w_in, gmlp_ln_g=gmlp_ln_g, gmlp_ln_b=gmlp_ln_b, gmlp_ws=gmlp_ws,
                   gmlp_bs=gmlp_bs, w_a_out=w_a_out, conv_w=conv_w, conv_b=conv_b, conv_ln_g=conv_ln_g,
                   conv_ln_b=conv_ln_b, w_b_out=w_b_out, fox_bf=fox_bf, w_c_out=w_c_out, w_out=w_out, mlp_w1=mlp_w1,
                   mlp_w2=mlp_w2)
    mom_m = dict(ada_w=m_ada_w, ada_b=m_ada_b, mix_pre_g=m_mix_pre_g, mix_post_g=m_mix_post_g, mlp_pre_g=m_mlp_pre_g,
                 mlp_post_g=m_mlp_post_g, w_in=m_w_in, gmlp_ln_g=m_gmlp_ln_g, gmlp_ln_b=m_gmlp_ln_b, gmlp_ws=m_gmlp_ws,
                 gmlp_bs=m_gmlp_bs, w_a_out=m_w_a_out, conv_w=m_conv_w, conv_b=m_conv_b, conv_ln_g=m_conv_ln_g,
                 conv_ln_b=m_conv_ln_b, w_b_out=m_w_b_out, fox_bf=m_fox_bf, w_c_out=m_w_c_out, w_out=m_w_out,
                 mlp_w1=m_mlp_w1, mlp_w2=m_mlp_w2)
    mom_v = dict(ada_w=v_ada_w, ada_b=v_ada_b, mix_pre_g=v_mix_pre_g, mix_post_g=v_mix_post_g, mlp_pre_g=v_mlp_pre_g,
                 mlp_post_g=v_mlp_post_g, w_in=v_w_in, gmlp_ln_g=v_gmlp_ln_g, gmlp_ln_b=v_gmlp_ln_b, gmlp_ws=v_gmlp_ws,
                 gmlp_bs=v_gmlp_bs, w_a_out=v_w_a_out, conv_w=v_conv_w, conv_b=v_conv_b, conv_ln_g=v_conv_ln_g,
                 conv_ln_b=v_conv_ln_b, w_b_out=v_w_b_out, fox_bf=v_fox_bf, w_c_out=v_w_c_out, w_out=v_w_out,
                 mlp_w1=v_mlp_w1, mlp_w2=v_mlp_w2)
    order = list(weights)
    px, py, pc = _position()
    chip = 2 * px + py
    dev = 2 * chip + pc
    depth = ada_w.shape[0]
    t = x.shape[1]
    xl = x.reshape(t, D)
    tgt = loss_target.reshape(t, D)

    big_names = [b[0] for b in BIG]
    shards = [[weights[n][l].astype(BF16) for n in big_names] for l in range(depth)]
    land_shapes = [(N_CHIPS,) + sh.shape for sh in shards[0]]

    small_in = _pack_rows([c, conv_w])
    gathered = _all_gather8(small_in)
    c_all = gathered[:, :D // LANE, :].reshape(N_DEV, D)
    cw_rows = depth * KW * LANE // LANE
    conv_w_full = jnp.concatenate(
        [gathered[2 * j, D // LANE:D // LANE + cw_rows, :].reshape(depth, KW, LANE) for j in range(N_CHIPS)], axis=2)

    ncol = ada_w.shape[2]
    ada_b_loc = lax.dynamic_slice_in_dim(ada_b, chip * ncol, ncol, axis=1).reshape(depth, 1, ncol)
    mod_sh = _ada_mod(c_all, ada_w, ada_b_loc)
    mod_g = _all_gather8(mod_sh.reshape(-1, LANE)).reshape(N_DEV, depth, N_DEV, ncol)
    mod_all = jnp.concatenate([mod_g[2 * j] for j in range(N_CHIPS)], axis=2)
    mod_mine = lax.dynamic_index_in_dim(mod_all, dev, axis=1, keepdims=False)

    mods = [mod_mine[l].reshape(NMOD, D) for l in range(depth)]

    def gathered_weights(idx, waited):
        own, lands = waited
        lands = _ag_d2d(lands, name="ag_d2d")
        w = {}
        for i, sh, g in zip(idx, own, lands):
            n, r, cdim, ax = BIG[i]
            g = lax.dynamic_update_slice(g, sh[None], (chip, 0, 0))
            if n == "w_in":
                w["w_in_p"] = _assemble_w_in(g.transpose(1, 0, 2).reshape(r, N_CHIPS * cdim))
            elif n == "mlp_w1":
                w[n] = g
            elif ax == 1:
                w[n] = g.transpose(1, 0, 2).reshape(r, N_CHIPS * cdim)
            else:
                w[n] = g.reshape(N_CHIPS * r, cdim)
        return w

    def local_weights(l):
        w = {}
        for n in ("mix_pre_g", "mix_post_g", "mlp_pre_g", "mlp_post_g", "gmlp_ln_g", "gmlp_ln_b", "conv_b", "conv_ln_g",
                  "conv_ln_b"):
            w[n] = weights[n][l:l + 1]
        tril = jnp.tril(jnp.ones((CH, CH), F32))
        wsm = gmlp_ws[l] * tril
        w["wsm"] = wsm.astype(BF16)
        w["wsmt"] = jnp.swapaxes(wsm, 1, 2).astype(BF16)
        w["bsx"] = jnp.repeat(gmlp_bs[l].T, HD, axis=1)
        w["conv_w"] = conv_w_full[l]
        w["bfp"] = jnp.pad(fox_bf[l], (0, LANE - NG)).reshape(1, LANE)
        return w

    def slabs(gfull, n, r, cdim, ax):
        if n == "mlp_w1":
            return gfull
        if ax == 1:
            return gfull.reshape(gfull.shape[0], N_CHIPS, cdim).transpose(1, 0, 2)
        return gfull.reshape(N_CHIPS, r, cdim)

    def swap_start(g, after, name):
        g["w_in"] = _disassemble_w_in(g.pop("w_in_p"))
        gs = [slabs(g[n], n, r, cdim, ax) for n, r, cdim, ax in BIG]
        return _ici_start("swap", gs, [(N_CHIPS, a.shape[1] // 2, a.shape[2]) for a in gs], after, name=name)

    def chip_sums(sw, after, name):
        gs, from_sibling = _ici_wait("swap", sw[0], sw[1], sw[2], sw[3], after, name=name)
        return [_add_own_half(a, rv, pc, name="rs_add_half_" + n) for a, rv, n in zip(gs, from_sibling, big_names)]

    def reduce_rest(sums, from_chips):
        red = [_add_own_chip(sf, rv, chip, name="rs_add_chip_" + n) for (sf, _), rv, n in zip(sums, from_chips, big_names)]
        return red, _swap_reduced(red)

    assert depth == 2
    ga, gb = [0], list(range(1, len(BIG)))

    def pick(seq, idx):
        return [seq[i] for i in idx]

    ag0a = _ici_start("gather", pick(shards[0], ga), pick(land_shapes, ga), mod_mine, name="ag0a_start")
    lands0a = _ici_wait("gather", ag0a[0], ag0a[1], ag0a[2], ag0a[3], ag0a[4], name="ag0a_wait")
    ag0b = _ici_start("gather", pick(shards[0], gb), pick(land_shapes, gb), lands0a[1][0], name="ag0b_start")
    ag1 = _ici_start("gather", shards[1], land_shapes, ag0b[4], name="ag1_start")
    mod0 = mods[0] + ag1[4][0, 0]
    layers, saved = [None] * depth, [None] * depth

    def late0(att):
        return gathered_weights(gb, _ici_wait("gather", ag0b[0], ag0b[1], ag0b[2], ag0b[3], att, name="ag0b_wait"))

    xs, saved[0], layers[0] = _layer_fwd(xl, mod0, {**local_weights(0), **gathered_weights(ga, lands0a)}, late0)
    lands1 = _ici_wait("gather", ag1[0], ag1[1], ag1[2], ag1[3], xs, name="ag1_wait")
    xs, saved[1], layers[1] = _layer_fwd(xs, mods[1], {**local_weights(1), **gathered_weights(ga + gb, lands1)})
    loss_local, dx = _loss_and_grad(xs, tgt)
    loss = lax.psum(loss_local, ("x", "y", "c"))
    grads, dmods = [None] * depth, [None] * depth
    dx, grads[1], dmods[1] = _layer_bwd(dx, mods[1], layers[1], saved[1])
    sw1 = swap_start(grads[1], dx, "sw1_start")
    rs_l1 = {}

    def mid0(dx2):
        rs_l1["sums"] = chip_sums(sw1, dx2, "sw1_wait")
        sbf1 = [sb for _, sb in rs_l1["sums"]]
        rs_l1["rs"] = _ici_start("scatter", sbf1, [(3,) + sb.shape[1:] for sb in sbf1], dx2, name="rs1_start")
        return rs_l1["rs"][4][0:1, 0:1]

    dx, grads[0], dmods[0] = _layer_bwd(dx, mod0 + sw1[4][0, 0], layers[0], saved[0], mid0)
    grad_x = dx.reshape(x.shape)
    sums1, rs1 = rs_l1["sums"], rs_l1["rs"]
    slot_shapes = [(3,) + sb.shape[1:] for _, sb in sums1]
    sw0 = swap_start(grads[0], dx, "sw0_start")
    g_out = {}

    small_names = [n for n, _ in SMALL]
    small_list = [jnp.stack(dmods)] + [jnp.stack([grads[l][n] for l in range(depth)]) for n in small_names]
    small_list.append(jnp.stack([grads[l]["conv_w"] for l in range(depth)]))
    small_shapes = [(depth, NMOD * D)] + [shp for _, shp in SMALL] + [(depth, KW, DM)]
    small_all = _all_gather8(_pack_rows(small_list) + sw0[4][0, 0])
    sums0 = chip_sums(sw0, small_all, "sw0_wait")
    rs0 = _ici_start("scatter", [sb for _, sb in sums0], slot_shapes, small_all, name="rs0_start")
    tok0 = rs0[4][0, 0]
    red1 = reduce_rest(sums1, _ici_wait("scatter", rs1[0], rs1[1], rs1[2], rs1[3], rs0[4], name="rs1_wait")[1])
    c_all = c_all + tok0
    small_sum = _unpack_rows(_sum8(small_all) + tok0, small_shapes)
    g_out["ada_b"] = small_sum[0]
    for n, gs in zip(small_names, small_sum[1:-1]):
        g_out[n] = gs
    g_out["conv_w"] = lax.dynamic_slice_in_dim(small_sum[-1], chip * LANE, LANE, axis=2)
    dmod_all = small_all[:, :depth * NMOD * D // LANE, :].reshape(N_DEV, depth, NMOD * D)
    dmod_loc = lax.dynamic_slice_in_dim(dmod_all, chip * ncol, ncol, axis=2).transpose(1, 0, 2)
    g_out["ada_w"] = _ada_grad(jnp.pad(c_all, ((0, 8), (0, 0))), jnp.pad(dmod_loc, ((0, 0), (0, 8), (0, 0))))

    delta, new_m, new_v = {}, {}, {}
    delta["ada_w"], new_m["ada_w"], new_v["ada_w"] = _adamw(ada_w, g_out["ada_w"], m_ada_w, v_ada_w, name="adamw_ada_w")
    ws_rows = (depth * NG * CH, CH)
    ws_out = _adamw(*[d["gmlp_ws"].reshape(ws_rows) for d in (weights, g_out, mom_m, mom_v)], name="adamw_gmlp_ws")
    delta["gmlp_ws"], new_m["gmlp_ws"], new_v["gmlp_ws"] = (a.reshape(gmlp_ws.shape) for a in ws_out)
    small_params = ["ada_b"] + [n for n in small_names if n != "gmlp_ws"] + ["conv_w"]
    packs = [_pack_rows([d[n] for n in small_params]) for d in (weights, g_out, mom_m, mom_v)]
    outs = _adamw(*packs, name="adamw_small")
    shapes = [weights[n].shape for n in small_params]
    for dst, buf in zip((delta, new_m, new_v), outs):
        for n, a in zip(small_params, _unpack_rows(buf, shapes)):
            dst[n] = a
    half = {n: _adamw_layer(weights[n], red1[0][wi], red1[1][wi], mom_m[n], mom_v[n], pc, 1, None, rs0[4],
                            name="adamw1_" + n) for wi, n in enumerate(big_names)}
    done = jnp.stack([delta["ada_w"][0, 0, 0], outs[0][0, 0]] + [half[n][1][1, 0, 0] for n in big_names])
    red0 = reduce_rest(sums0, _ici_wait("scatter", rs0[0], rs0[1], rs0[2], rs0[3], done, name="rs0_wait")[1])
    for wi, n in enumerate(big_names):
        g_out[n], delta[n], new_m[n], new_v[n] = _adamw_layer(
            weights[n], red0[0][wi], red0[1][wi], mom_m[n], mom_v[n], pc, 0, half[n], rs0[4], name="adamw0_" + n)

    return (loss, grad_x, *[g_out[n] for n in order], *[delta[n] for n in order], *[new_m[n] for n in order],
            *[new_v[n] for n in order])
```

```python
import functools

import jax
import jax.numpy as jnp
from jax import lax
from jax.experimental import pallas as pl
from jax.experimental.pallas import tpu as pltpu

F32 = jnp.float32
BF16 = jnp.bfloat16
I32 = jnp.int32
MESH = pl.DeviceIdType.MESH
ANY = pl.BlockSpec(memory_space=pl.ANY)

D = 1024
DM = 512
NG = 8
CH = 128
KW = 31
HALO = 32
DFF = 4096
NMOD = 6
EPS = 1e-6
LANE = 128
N_CHIPS = 4
N_DEV = 8
C_GATE, C_UV, C_GLU, C_Q, C_K, C_V, C_F, D_INP = 0, 3072, 4096, 5120, 5632, 6144, 6656, 7168
D_IN = 6664
VMEM_LIMIT = 56 * 1024 * 1024
ROW_TILE = 1024
TK_DEEP = 4096

ADAM_LR, ADAM_B1, ADAM_B2, ADAM_EPS, ADAM_WD, ADAM_STEP = 0.001, 0.9, 0.999, 1e-08, 0.01, 10

BIG = (("w_in", 1024, 1666, 1), ("w_a_out", 512, 256, 1), ("w_b_out", 512, 256, 1), ("w_c_out", 512, 256, 1),
       ("w_out", 256, 1024, 0), ("mlp_w1", 1024, 1024, 1), ("mlp_w2", 1024, 1024, 0))


def _cparams(sem):
    return pltpu.CompilerParams(dimension_semantics=sem, vmem_limit_bytes=VMEM_LIMIT)


def _sigmoid(x):
    return jax.nn.sigmoid(x)


_GELU_K = 0.7978845608028654
_GELU_A = 0.044715


def _gelu(x):
    t = jnp.tanh(_GELU_K * (x + _GELU_A * x * x * x))
    return 0.5 * x * (1.0 + t)


def _gelu_grad(x):
    t = jnp.tanh(_GELU_K * (x + _GELU_A * x * x * x))
    return 0.5 * (1.0 + t) + 0.5 * x * (1.0 - t * t) * _GELU_K * (1.0 + 3.0 * _GELU_A * x * x)


def _mean(x):
    return jnp.mean(x, axis=-1, keepdims=True)


def _colsum(x):
    return jnp.sum(x, axis=0, keepdims=True)


def _dot(a, b, dims=((1,), (0,))):
    return lax.dot_general(a, b, (dims, ((), ())), preferred_element_type=F32)


NN = ((1,), (0,))
NT = ((1,), (1,))
TN = ((0,), (0,))


def _matmul(a, b, *, name, ta=False, tb=False, out_dtype=F32, tm=1024, tn=1024, tk=1024, epilogue=None, extra=(),
            extra_out=(), b_slabs=False, out_slabs=0):
    m, k = (a.shape[1], a.shape[0]) if ta else a.shape
    if b_slabs:
        ns, brows, bw = b.shape
        n = brows if tb else ns * bw
        assert (ns * bw if tb else brows) == k, (name, b.shape, k)
        all_slabs = tb and tk >= k
        tn, tk = (tn, k if all_slabs else bw) if tb else (bw, tk)
    else:
        n = b.shape[0] if tb else b.shape[1]
        all_slabs = False
    tm, tn, tk = min(tm, m), min(tn, n), min(tk, k)
    assert m % tm == 0 and n % tn == 0 and k % tk == 0, (name, m, n, k, tm, tn, tk)
    assert not out_slabs or (n // out_slabs == tn and epilogue is None), name
    nk = k // tk
    dims = ((0 if ta else 1,), (1 if tb else 0,))
    n_extra = len(extra)
    out_dtypes = (out_dtype,) + tuple(extra_out)

    def body(a_ref, b_ref, *rest):
        extra_refs = rest[:n_extra]
        out_refs = rest[n_extra:n_extra + len(out_dtypes)]
        kk = pl.program_id(2)
        if all_slabs:
            part = sum(_dot(a_ref[:, s * bw:(s + 1) * bw].astype(BF16), b_ref[s].astype(BF16), dims) for s in range(ns))
        else:
            part = _dot(a_ref[...].astype(BF16), b_ref[...].astype(BF16), dims)

        def finish(acc):
            outs = (acc,) if epilogue is None else epilogue(acc, *[r[...] for r in extra_refs])
            for o_ref, o in zip(out_refs, outs):
                o_ref[...] = o.astype(o_ref.dtype)

        if nk == 1:
            finish(part)
        else:
            acc_ref = rest[-1]

            @pl.when(kk == 0)
            def _():
                acc_ref[...] = part

            @pl.when(jnp.logical_and(kk > 0, kk < nk - 1))
            def _():
                acc_ref[...] += part

            @pl.when(kk == nk - 1)
            def _():
                finish(acc_ref[...] + part)

    a_spec = pl.BlockSpec((tk, tm), lambda i, j, kk: (kk, i)) if ta else pl.BlockSpec((tm, tk), lambda i, j, kk: (i, kk))
    if all_slabs:
        b_spec = pl.BlockSpec((ns, tn, bw), lambda i, j, kk: (0, j, 0))
    elif b_slabs and tb:
        b_spec = pl.BlockSpec((None, tn, tk), lambda i, j, kk: (kk, j, 0))
    elif b_slabs:
        b_spec = pl.BlockSpec((None, tk, tn), lambda i, j, kk: (j, kk, 0))
    else:
        b_spec = pl.BlockSpec((tn, tk), lambda i, j, kk: (j, kk)) if tb else pl.BlockSpec((tk, tn), lambda i, j, kk: (kk, j))
    if out_slabs:
        o_spec = pl.BlockSpec((None, tm, tn), lambda i, j, kk: (j, i, 0))
        o_shape = (out_slabs, m, tn)
    else:
        o_spec = pl.BlockSpec((tm, tn), lambda i, j, kk: (i, j))
        o_shape = (m, n)
    outs = pl.pallas_call(
        body, name=name, grid=(m // tm, n // tn, nk),
        in_specs=[a_spec, b_spec] + [o_spec] * n_extra,
        out_specs=[o_spec] * len(out_dtypes),
        out_shape=[jax.ShapeDtypeStruct(o_shape, dt) for dt in out_dtypes],
        scratch_shapes=[pltpu.VMEM((tm, tn), F32)] if nk > 1 else [],
        compiler_params=_cparams(("parallel", "parallel", "arbitrary")),
    )(a, b, *extra)
    return outs[0] if len(outs) == 1 else outs


def _rows(tm, n, col=0):
    return pl.BlockSpec((tm, n), lambda i: (i, col))


def _vec(n):
    return pl.BlockSpec((1, n), lambda i: (0, 0))


def _norm_mod(x, g, sc, sh, *, name, tm=ROW_TILE):
    t = x.shape[0]
    tm = min(tm, t)

    def body(x_ref, g_ref, sc_ref, sh_ref, h_ref):
        xv = x_ref[...]
        inv = lax.rsqrt(_mean(xv * xv) + EPS)
        h_ref[...] = ((xv * inv * g_ref[...]) * (1.0 + sc_ref[...]) + sh_ref[...]).astype(BF16)

    return pl.pallas_call(
        body, name=name, grid=(t // tm,), in_specs=[_rows(tm, D), _vec(D), _vec(D), _vec(D)],
        out_specs=_rows(tm, D), out_shape=jax.ShapeDtypeStruct((t, D), BF16),
        compiler_params=_cparams(("parallel",)))(x, g, sc, sh)


def _resid(x, y, gt, gp, *, name, tm=ROW_TILE):
    t = x.shape[0]
    tm = min(tm, t)

    def body(x_ref, y_ref, gt_ref, gp_ref, o_ref):
        yv = y_ref[...]
        inv = lax.rsqrt(_mean(yv * yv) + EPS)
        o_ref[...] = x_ref[...] + gt_ref[...] * (yv * inv * gp_ref[...])

    return pl.pallas_call(
        body, name=name, grid=(t // tm,), in_specs=[_rows(tm, D), _rows(tm, D), _vec(D), _vec(D)],
        out_specs=_rows(tm, D), out_shape=jax.ShapeDtypeStruct((t, D), F32),
        compiler_params=_cparams(("parallel",)))(x, y, gt, gp)


def _resid_bwd(dx, y, gt, gp, *, name, tm=ROW_TILE):
    t = dx.shape[0]
    tm = min(tm, t)

    def body(dx_ref, y_ref, gt_ref, gp_ref, dy_ref, dgt_ref, dgp_ref):
        @pl.when(pl.program_id(0) == 0)
        def _():
            dgt_ref[...] = jnp.zeros_like(dgt_ref)
            dgp_ref[...] = jnp.zeros_like(dgp_ref)

        dxv, yv, gp_v = dx_ref[...], y_ref[...], gp_ref[...]
        inv = lax.rsqrt(_mean(yv * yv) + EPS)
        yh = yv * inv
        dgt_ref[...] += _colsum(dxv * (yh * gp_v))
        dr = dxv * gt_ref[...]
        dgp_ref[...] += _colsum(dr * yh)
        dyn = dr * gp_v
        dy_ref[...] = (inv * (dyn - yh * _mean(dyn * yh))).astype(BF16)

    return pl.pallas_call(
        body, name=name, grid=(t // tm,), in_specs=[_rows(tm, D), _rows(tm, D), _vec(D), _vec(D)],
        out_specs=[_rows(tm, D), _vec(D), _vec(D)],
        out_shape=[jax.ShapeDtypeStruct((t, D), BF16), jax.ShapeDtypeStruct((1, D), F32),
                   jax.ShapeDtypeStruct((1, D), F32)],
        compiler_params=_cparams(("arbitrary",)))(dx, y, gt, gp)


def _norm_bwd(dh, dx_res, x, g, sc, *, name, tm=ROW_TILE):
    t = dh.shape[0]
    tm = min(tm, t)

    def body(dh_ref, dxr_ref, x_ref, g_ref, sc_ref, dx_ref, dg_ref, dsc_ref, dsh_ref):
        @pl.when(pl.program_id(0) == 0)
        def _():
            dg_ref[...] = jnp.zeros_like(dg_ref)
            dsc_ref[...] = jnp.zeros_like(dsc_ref)
            dsh_ref[...] = jnp.zeros_like(dsh_ref)

        dhv, xv, gv = dh_ref[...], x_ref[...], g_ref[...]
        inv = lax.rsqrt(_mean(xv * xv) + EPS)
        xh = xv * inv
        dsh_ref[...] += _colsum(dhv)
        dsc_ref[...] += _colsum(dhv * (xh * gv))
        dn = dhv * (1.0 + sc_ref[...])
        dg_ref[...] += _colsum(dn * xh)
        dxh = dn * gv
        dx_ref[...] = inv * (dxh - xh * _mean(dxh * xh)) + dxr_ref[...]

    vec_out = jax.ShapeDtypeStruct((1, D), F32)
    return pl.pallas_call(
        body, name=name, grid=(t // tm,), in_specs=[_rows(tm, D), _rows(tm, D), _rows(tm, D), _vec(D), _vec(D)],
        out_specs=[_rows(tm, D), _vec(D), _vec(D), _vec(D)],
        out_shape=[jax.ShapeDtypeStruct((t, D), F32), vec_out, vec_out, vec_out],
        compiler_params=_cparams(("arbitrary",)))(dh, dx_res, x, g, sc)


def _loss_and_grad(x, target, *, tm=ROW_TILE):
    t = x.shape[0]
    tm = min(tm, t)

    def body(x_ref, t_ref, loss_ref, dx_ref):
        @pl.when(pl.program_id(0) == 0)
        def _():
            loss_ref[...] = jnp.zeros_like(loss_ref)

        e = x_ref[...] - t_ref[...]
        dx_ref[...] = e * (1.0 / D)
        s = jnp.sum(jnp.sum(e * e, axis=1, keepdims=True), axis=0, keepdims=True) * (0.5 / D)
        loss_ref[...] += jnp.broadcast_to(s, loss_ref.shape)

    loss, dx = pl.pallas_call(
        body, name="loss", grid=(t // tm,), in_specs=[_rows(tm, D), _rows(tm, D)],
        out_specs=[pl.BlockSpec((8, LANE), lambda i: (0, 0)), _rows(tm, D)],
        out_shape=[jax.ShapeDtypeStruct((8, LANE), F32), jax.ShapeDtypeStruct((t, D), F32)],
        compiler_params=_cparams(("arbitrary",)))(x, target)
    return loss[0, 0], dx


def _gmlp_core(uv, lng, lnb, ws_ref, bsx):
    tm = uv.shape[0]
    gu = _gelu(uv[:, :DM])
    gv = _gelu(uv[:, DM:])
    mu = _mean(gv)
    vc = gv - mu
    rstd = lax.rsqrt(_mean(vc * vc) + EPS)
    vh = vc * rstd
    vln = vh * lng + lnb
    lane = lax.broadcasted_iota(I32, (CH, LANE), 1)
    sv_rows = []
    for nchunk in range(tm // CH):
        vb = vln[nchunk * CH:(nchunk + 1) * CH].astype(BF16)
        cols = []
        for cb in range(DM // LANE):
            vcb = vb[:, cb * LANE:(cb + 1) * LANE]
            lo = _dot(ws_ref[2 * cb], vcb)
            hi = _dot(ws_ref[2 * cb + 1], vcb)
            cols.append(jnp.where(lane < 64, lo, hi))
        sv_rows.append(jnp.concatenate(cols, axis=1) + bsx)
    sv = jnp.concatenate(sv_rows, axis=0) if len(sv_rows) > 1 else sv_rows[0]
    return gu, vh, rstd, vln, sv


def _gmlp_fwd(proj, lng, lnb, wsm, bsx, *, tm=256):
    t = proj.shape[0]
    tm = min(tm, t)

    def body(uv_ref, lng_ref, lnb_ref, ws_ref, bs_ref, ga_ref):
        gu, _, _, _, sv = _gmlp_core(uv_ref[...], lng_ref[...], lnb_ref[...], ws_ref, bs_ref[...])
        ga_ref[...] = (gu * sv).astype(BF16)

    return pl.pallas_call(
        body, name="gmlp_fwd", grid=(t // tm,),
        in_specs=[_rows(tm, 2 * DM, C_UV // (2 * DM)), _vec(DM), _vec(DM),
                  pl.BlockSpec((NG, CH, CH), lambda i: (0, 0, 0)), pl.BlockSpec((CH, DM), lambda i: (0, 0))],
        out_specs=_rows(tm, DM), out_shape=jax.ShapeDtypeStruct((t, DM), BF16),
        compiler_params=_cparams(("parallel",)))(proj, lng, lnb, wsm, bsx)


def _gmlp_bwd(dga, proj, lng, lnb, wsm, wsmt, bsx, *, tm=256):
    t = proj.shape[0]
    tm = min(tm, t)

    def body(dga_ref, uv_ref, lng_ref, lnb_ref, ws_ref, wst_ref, bs_ref, duv_ref, dws_ref, dbs_ref, dlng_ref, dlnb_ref,
             dbsx_ref):
        i = pl.program_id(0)

        @pl.when(i == 0)
        def _():
            dws_ref[...] = jnp.zeros_like(dws_ref)
            dbsx_ref[...] = jnp.zeros_like(dbsx_ref)
            dlng_ref[...] = jnp.zeros_like(dlng_ref)
            dlnb_ref[...] = jnp.zeros_like(dlnb_ref)

        uv = uv_ref[...]
        lng_v = lng_ref[...]
        gu, vh, rstd, vln, sv = _gmlp_core(uv, lng_v, lnb_ref[...], ws_ref, bs_ref[...])
        dga_v = dga_ref[...]
        dgu = dga_v * sv
        dsv = dga_v * gu
        lane = lax.broadcasted_iota(I32, (CH, LANE), 1)
        tril = lax.broadcasted_iota(I32, (CH, CH), 0) >= lax.broadcasted_iota(I32, (CH, CH), 1)
        dvln_rows = []
        for nchunk in range(tm // CH):
            rows = slice(nchunk * CH, (nchunk + 1) * CH)
            dbsx_ref[...] += dsv[rows]
            vb = vln[rows].astype(BF16)
            cols = []
            for cb in range(DM // LANE):
                cs = slice(cb * LANE, (cb + 1) * LANE)
                dsvb = dsv[rows, cs]
                vcb = vb[:, cs]
                dlo = jnp.where(lane < 64, dsvb, 0.0).astype(BF16)
                dhi = jnp.where(lane < 64, 0.0, dsvb).astype(BF16)
                dws_ref[2 * cb] += jnp.where(tril, _dot(dlo, vcb, NT), 0.0)
                dws_ref[2 * cb + 1] += jnp.where(tril, _dot(dhi, vcb, NT), 0.0)
                dsb = dsvb.astype(BF16)
                cols.append(jnp.where(lane < 64, _dot(wst_ref[2 * cb], dsb), _dot(wst_ref[2 * cb + 1], dsb)))
            dvln_rows.append(jnp.concatenate(cols, axis=1))
        dvln = jnp.concatenate(dvln_rows, axis=0) if len(dvln_rows) > 1 else dvln_rows[0]
        dlnb_ref[...] += _colsum(dvln)
        dlng_ref[...] += _colsum(dvln * vh)
        dvh = dvln * lng_v
        dgv = rstd * (dvh - _mean(dvh) - vh * _mean(dvh * vh))
        duv_ref[:, :DM] = (dgu * _gelu_grad(uv[:, :DM])).astype(BF16)
        duv_ref[:, DM:] = (dgv * _gelu_grad(uv[:, DM:])).astype(BF16)

        @pl.when(i == pl.num_programs(0) - 1)
        def _():
            ind = (lax.broadcasted_iota(I32, (DM, LANE), 0) // 64 == lax.broadcasted_iota(I32, (DM, LANE), 1)).astype(F32)
            dbs_ref[...] = jnp.dot(dbsx_ref[...], ind, preferred_element_type=F32, precision=lax.Precision.HIGHEST)

    vec_out = jax.ShapeDtypeStruct((1, DM), F32)
    outs = pl.pallas_call(
        body, name="gmlp_bwd", grid=(t // tm,),
        in_specs=[_rows(tm, DM), _rows(tm, 2 * DM, C_UV // (2 * DM)), _vec(DM), _vec(DM),
                  pl.BlockSpec((NG, CH, CH), lambda i: (0, 0, 0)), pl.BlockSpec((NG, CH, CH), lambda i: (0, 0, 0)),
                  pl.BlockSpec((CH, DM), lambda i: (0, 0))],
        out_specs=[_rows(tm, 2 * DM), pl.BlockSpec((NG, CH, CH), lambda i: (0, 0, 0)),
                   pl.BlockSpec((CH, LANE), lambda i: (0, 0)), _vec(DM), _vec(DM)],
        out_shape=[jax.ShapeDtypeStruct((t, 2 * DM), BF16), jax.ShapeDtypeStruct((NG, CH, CH), F32),
                   jax.ShapeDtypeStruct((CH, LANE), F32), vec_out, vec_out],
        scratch_shapes=[pltpu.VMEM((CH, DM), F32)],
        compiler_params=_cparams(("arbitrary",)))(dga, proj, lng, lnb, wsm, wsmt, bsx)
    return outs


def _glu_into(zs_ref, glu_ref, halo_ref, first):
    hal = halo_ref[...]
    z0h = hal[:, :DM] * _sigmoid(hal[:, DM:])
    zs_ref[0:HALO, :] = jnp.where(first, 0.0, z0h)
    g = glu_ref[...]
    zs_ref[HALO:, :] = g[:, :DM] * _sigmoid(g[:, DM:])


SUB = 8


def _shifted_copies(dst_ref, src_ref):
    n = src_ref.shape[0]
    for s in range(SUB):
        dst_ref[s, 0:n - s, :] = src_ref[s:n, :]


def _window(shifted_ref, off, rows, cs):
    s = off % SUB
    return shifted_ref[s, off - s:off - s + rows, cs]


def _conv_fwd(proj, cw, cb, lng, lnb, *, tm=256, rb=64):
    t = proj.shape[0]
    tm = min(tm, t)
    hb = tm // HALO
    gcol = C_GLU // (2 * DM)

    def body(glu_ref, halo_ref, cw_ref, cb_ref, lng_ref, lnb_ref, zc_ref, zb_ref, zs_ref, zsh_ref):
        i = pl.program_id(0)
        _glu_into(zs_ref, glu_ref, halo_ref, i == 0)
        _shifted_copies(zsh_ref, zs_ref)
        for cbk in range(DM // LANE):
            cs = slice(cbk * LANE, (cbk + 1) * LANE)
            for r in range(tm // rb):
                acc = jnp.broadcast_to(cb_ref[:, cs], (rb, LANE))
                for k in range(KW):
                    acc = acc + cw_ref[k:k + 1, cs] * _window(zsh_ref, r * rb + HALO - (KW - 1) + k, rb, cs)
                zc_ref[r * rb:(r + 1) * rb, cs] = acc
        zc = zc_ref[...]
        mu = _mean(zc)
        zcc = zc - mu
        zh = zcc * lax.rsqrt(_mean(zcc * zcc) + EPS)
        a = zh * lng_ref[...] + lnb_ref[...]
        zb_ref[...] = (a * _sigmoid(a)).astype(BF16)

    return pl.pallas_call(
        body, name="conv_fwd", grid=(t // tm,),
        in_specs=[_rows(tm, 2 * DM, gcol),
                  pl.BlockSpec((HALO, 2 * DM), lambda i: (jnp.maximum(i * hb - 1, 0), gcol)),
                  pl.BlockSpec((KW, DM), lambda i: (0, 0)), _vec(DM), _vec(DM), _vec(DM)],
        out_specs=[_rows(tm, DM), _rows(tm, DM)],
        out_shape=[jax.ShapeDtypeStruct((t, DM), F32), jax.ShapeDtypeStruct((t, DM), BF16)],
        scratch_shapes=[pltpu.VMEM((HALO + tm, DM), F32), pltpu.VMEM((SUB, HALO + tm, DM), F32)],
        compiler_params=_cparams(("parallel",)))(proj, proj, cw, cb, lng, lnb)


def _conv_bwd_ln(dzb, zc, lng, lnb, *, tm=ROW_TILE):
    t = zc.shape[0]
    tm = min(tm, t)

    def body(dzb_ref, zc_ref, lng_ref, lnb_ref, dzc_ref, dlng_ref, dlnb_ref):
        @pl.when(pl.program_id(0) == 0)
        def _():
            dlng_ref[...] = jnp.zeros_like(dlng_ref)
            dlnb_ref[...] = jnp.zeros_like(dlnb_ref)

        zc = zc_ref[...]
        lng_v = lng_ref[...]
        mu = _mean(zc)
        zcc = zc - mu
        rstd = lax.rsqrt(_mean(zcc * zcc) + EPS)
        zh = zcc * rstd
        a = zh * lng_v + lnb_ref[...]
        s = _sigmoid(a)
        da = dzb_ref[...] * (s * (1.0 + a * (1.0 - s)))
        dlnb_ref[...] += _colsum(da)
        dlng_ref[...] += _colsum(da * zh)
        dzh = da * lng_v
        dzc_ref[...] = rstd * (dzh - _mean(dzh) - zh * _mean(dzh * zh))

    vec_out = jax.ShapeDtypeStruct((1, DM), F32)
    return pl.pallas_call(
        body, name="conv_bwd_ln", grid=(t // tm,), in_specs=[_rows(tm, DM), _rows(tm, DM), _vec(DM), _vec(DM)],
        out_specs=[_rows(tm, DM), _vec(DM), _vec(DM)],
        out_shape=[jax.ShapeDtypeStruct((t, DM), F32), vec_out, vec_out],
        compiler_params=_cparams(("arbitrary",)))(dzb, zc, lng, lnb)


def _conv_bwd(dzc, proj, cw, *, tm=256, rb=64):
    t = proj.shape[0]
    tm = min(tm, t)
    hb = tm // HALO
    nblk = t // tm
    gcol = C_GLU // (2 * DM)

    def body(dzc_ref, dnext_ref, glu_ref, halo_ref, cw_ref, dglu_ref, dcw_ref, dcb_ref, zs_ref, ds_ref, zsh_ref, dsh_ref):
        i = pl.program_id(0)

        @pl.when(i == 0)
        def _():
            dcw_ref[...] = jnp.zeros_like(dcw_ref)
            dcb_ref[...] = jnp.zeros_like(dcb_ref)

        _glu_into(zs_ref, glu_ref, halo_ref, i == 0)
        _shifted_copies(zsh_ref, zs_ref)
        dzc = dzc_ref[...]
        ds_ref[0:tm, :] = dzc
        ds_ref[tm:, :] = jnp.where(i == nblk - 1, 0.0, dnext_ref[...])
        _shifted_copies(dsh_ref, ds_ref)
        dcb_ref[...] += _colsum(dzc)
        for k in range(KW):
            dcw_ref[k:k + 1, :] += _colsum(dzc * _window(zsh_ref, HALO - (KW - 1) + k, tm, slice(None)))
        g = glu_ref[...]
        val, sg = g[:, :DM], _sigmoid(g[:, DM:])
        for cbk in range(DM // LANE):
            cs = slice(cbk * LANE, (cbk + 1) * LANE)
            for r in range(tm // rb):
                acc = jnp.zeros((rb, LANE), F32)
                for k in range(KW):
                    acc = acc + cw_ref[k:k + 1, cs] * _window(dsh_ref, r * rb + (KW - 1) - k, rb, cs)
                rs = slice(r * rb, (r + 1) * rb)
                dglu_ref[rs, cs] = (acc * sg[rs, cs]).astype(BF16)
                v, s = val[rs, cs], sg[rs, cs]
                dglu_ref[rs, DM + cbk * LANE:DM + (cbk + 1) * LANE] = (acc * v * s * (1.0 - s)).astype(BF16)

    return pl.pallas_call(
        body, name="conv_bwd", grid=(nblk,),
        in_specs=[_rows(tm, DM),
                  pl.BlockSpec((HALO, DM), lambda i: (jnp.minimum((i + 1) * hb, nblk * hb - 1), 0)),
                  _rows(tm, 2 * DM, gcol),
                  pl.BlockSpec((HALO, 2 * DM), lambda i: (jnp.maximum(i * hb - 1, 0), gcol)),
                  pl.BlockSpec((KW, DM), lambda i: (0, 0))],
        out_specs=[_rows(tm, 2 * DM), pl.BlockSpec((HALO, DM), lambda i: (0, 0)), _vec(DM)],
        out_shape=[jax.ShapeDtypeStruct((t, 2 * DM), BF16), jax.ShapeDtypeStruct((HALO, DM), F32),
                   jax.ShapeDtypeStruct((1, DM), F32)],
        scratch_shapes=[pltpu.VMEM((HALO + tm, DM), F32), pltpu.VMEM((tm + HALO, DM), F32),
                        pltpu.VMEM((SUB, HALO + tm, DM), F32), pltpu.VMEM((SUB, tm + HALO, DM), F32)],
        compiler_params=_cparams(("arbitrary",)))(dzc, dzc, proj, proj, cw)


CUM_ROWS = 512


def _log_sigmoid(x):
    return jnp.minimum(x, 0.0) - jnp.log1p(jnp.exp(-jnp.abs(x)))


def _fox_cum(proj, bfp):
    t = proj.shape[0]
    fcol = C_F // LANE
    cr = min(CUM_ROWS, t)

    def body(f_ref, bf_ref, cum_ref, carry_ref):
        @pl.when(pl.program_id(0) == 0)
        def _():
            carry_ref[...] = jnp.zeros_like(carry_ref)

        lf = _log_sigmoid(f_ref[...] + bf_ref[...])
        tri = (lax.broadcasted_iota(I32, (cr, cr), 0) >= lax.broadcasted_iota(I32, (cr, cr), 1)).astype(F32)
        cum = jnp.dot(tri, lf, preferred_element_type=F32, precision=lax.Precision.HIGHEST) + carry_ref[0:1, :]
        cum_ref[...] = cum
        carry_ref[...] = jnp.broadcast_to(cum[cr - 1:cr, :], carry_ref.shape)

    return pl.pallas_call(
        body, name="fox_cum", grid=(t // cr,), in_specs=[_rows(cr, LANE, fcol), _vec(LANE)],
        out_specs=_rows(cr, LANE), out_shape=jax.ShapeDtypeStruct((t, LANE), F32),
        scratch_shapes=[pltpu.VMEM((8, LANE), F32)],
        compiler_params=_cparams(("arbitrary",)))(proj, bfp)


def _fox_cum_bwd(dcum, proj, bfp):
    t = proj.shape[0]
    cr = min(CUM_ROWS, t)
    nb = t // cr
    fcol = C_F // LANE
    fw = D_INP - C_F

    def body(dc_ref, f_ref, bf_ref, df_ref, dbf_ref, carry_ref):
        @pl.when(pl.program_id(0) == 0)
        def _():
            carry_ref[...] = jnp.zeros_like(carry_ref)
            dbf_ref[...] = jnp.zeros_like(dbf_ref)

        triu = (lax.broadcasted_iota(I32, (cr, cr), 0) <= lax.broadcasted_iota(I32, (cr, cr), 1)).astype(F32)
        dlf = jnp.dot(triu, dc_ref[...], preferred_element_type=F32, precision=lax.Precision.HIGHEST) + carry_ref[0:1, :]
        carry_ref[...] = jnp.broadcast_to(dlf[0:1, :], carry_ref.shape)
        z = f_ref[...] + bf_ref[...]
        lane = lax.broadcasted_iota(I32, (cr, LANE), 1)
        df = jnp.where(lane < NG, dlf * _sigmoid(-z), 0.0)
        dbf_ref[...] += _colsum(df)
        df_ref[:, 0:LANE] = df.astype(BF16)
        df_ref[:, LANE:] = jnp.zeros((cr, fw - LANE), BF16)

    return pl.pallas_call(
        body, name="fox_cum_bwd", grid=(nb,),
        in_specs=[pl.BlockSpec((cr, LANE), lambda i: (nb - 1 - i, 0)),
                  pl.BlockSpec((cr, LANE), lambda i: (nb - 1 - i, fcol)), _vec(LANE)],
        out_specs=[pl.BlockSpec((cr, fw), lambda i: (nb - 1 - i, 0)), _vec(LANE)],
        out_shape=[jax.ShapeDtypeStruct((t, fw), BF16), jax.ShapeDtypeStruct((1, LANE), F32)],
        scratch_shapes=[pltpu.VMEM((8, LANE), F32)],
        compiler_params=_cparams(("arbitrary",)))(dcum, proj, bfp)


HD = 64
ATT_SCALE = 0.125
NEG = -1e30


def _qkv_prep(proj, *, tm=512):
    t = proj.shape[0]
    tm = min(tm, t)

    def body(q_ref, k_ref, v_ref, o_ref):
        o_ref[:, 0:DM] = (q_ref[...] * ATT_SCALE).astype(BF16)
        o_ref[:, DM:2 * DM] = k_ref[...].astype(BF16)
        o_ref[:, 2 * DM:] = v_ref[...].astype(BF16)

    return pl.pallas_call(
        body, name="qkv_prep", grid=(t // tm,),
        in_specs=[_rows(tm, DM, C_Q // DM), _rows(tm, DM, C_K // DM), _rows(tm, DM, C_V // DM)],
        out_specs=_rows(tm, 3 * DM), out_shape=jax.ShapeDtypeStruct((t, 3 * DM), BF16),
        compiler_params=_cparams(("parallel",)))(proj, proj, proj)


def _causal_pairs(nq, outer_is_query):
    if outer_is_query:
        pairs = [(i, j) for i in range(nq) for j in range(i + 1)]
    else:
        pairs = [(j, i) for j in range(nq) for i in range(j, nq)]
    return (jnp.asarray([p[0] for p in pairs], I32), jnp.asarray([p[1] for p in pairs], I32))


def _to_row(col):
    return jnp.transpose(col)[0:1, :]


def _rep(x, tk):
    return x if tk == LANE else jnp.tile(x, (1, tk // LANE))


def _attn_fwd(qkv, ckrow, *, tq=512):
    t = qkv.shape[0]
    tq = min(tq, t)
    tk = tq
    nq = t // tq
    oi, ij = _causal_pairs(nq, True)

    def body(oi_ref, ij_ref, q_ref, k_ref, v_ref, ck_ref, o_ref, lser_ref, m_ref, l_ref, acc_ref):
        n = pl.program_id(0)
        i, j = oi_ref[n], ij_ref[n]

        @pl.when(j == 0)
        def _():
            m_ref[...] = jnp.full_like(m_ref, NEG)
            l_ref[...] = jnp.zeros_like(l_ref)
            acc_ref[...] = jnp.zeros_like(acc_ref)

        def step(masked):
            if masked:
                keep = lax.broadcasted_iota(I32, (tq, tk), 1) <= lax.broadcasted_iota(I32, (tq, tk), 0)
            lo = lax.broadcasted_iota(I32, (tq, LANE), 1) < HD
            for hp in range(NG // 2):
                cs = slice(hp * LANE, (hp + 1) * LANE)
                qp, kp, vp = q_ref[:, cs], k_ref[:, cs], v_ref[:, cs]
                alphas, pvs = [], []
                for hh in range(2):
                    h = 2 * hp + hh
                    qm = jnp.where(lo if hh == 0 else jnp.logical_not(lo), qp, jnp.zeros_like(qp))
                    s = _dot(qm, kp, NT) - ck_ref[h:h + 1, :]
                    if masked:
                        s = jnp.where(keep, s, NEG)
                    m_prev = m_ref[h]
                    m_new = jnp.maximum(m_prev, jnp.max(s, axis=1, keepdims=True))
                    alpha = jnp.exp(m_prev - m_new)
                    p = jnp.exp(s - _rep(m_new, tk))
                    l_ref[h] = alpha * l_ref[h] + jnp.sum(p, axis=1, keepdims=True)
                    m_ref[h] = m_new
                    alphas.append(alpha)
                    pvs.append(_dot(p.astype(BF16), vp))
                acc_ref[:, cs] = jnp.where(lo, alphas[0], alphas[1]) * acc_ref[:, cs] + jnp.where(lo, pvs[0], pvs[1])

        @pl.when(j < i)
        def _():
            step(False)

        @pl.when(j == i)
        def _():
            step(True)
            lo = lax.broadcasted_iota(I32, (tq, LANE), 1) < HD
            for hp in range(NG // 2):
                cs = slice(hp * LANE, (hp + 1) * LANE)
                o_ref[:, cs] = (acc_ref[:, cs] / jnp.where(lo, l_ref[2 * hp], l_ref[2 * hp + 1])).astype(BF16)
            for h in range(NG):
                lser_ref[h:h + 1, :] = _to_row(m_ref[h] + jnp.log(l_ref[h]))

    gs = pltpu.PrefetchScalarGridSpec(
        num_scalar_prefetch=2, grid=(int(oi.shape[0]),),
        in_specs=[pl.BlockSpec((tq, DM), lambda n, a, b: (a[n], 0)),
                  pl.BlockSpec((tk, DM), lambda n, a, b: (b[n], 1)),
                  pl.BlockSpec((tk, DM), lambda n, a, b: (b[n], 2)),
                  pl.BlockSpec((NG, tk), lambda n, a, b: (0, b[n]))],
        out_specs=[pl.BlockSpec((tq, DM), lambda n, a, b: (a[n], 0)),
                   pl.BlockSpec((NG, tq), lambda n, a, b: (0, a[n]))],
        scratch_shapes=[pltpu.VMEM((NG, tq, LANE), F32), pltpu.VMEM((NG, tq, LANE), F32), pltpu.VMEM((tq, DM), F32)])
    return pl.pallas_call(
        body, name="attn_fwd", grid_spec=gs,
        out_shape=[jax.ShapeDtypeStruct((t, DM), BF16), jax.ShapeDtypeStruct((NG, t), F32)],
        compiler_params=_cparams(("arbitrary",)))(oi, ij, qkv, qkv, qkv, ckrow)


def _attn_delta(o, do, *, tq=512):
    t = o.shape[0]
    tq = min(tq, t)

    def body(o_ref, do_ref, d_ref):
        lo = lax.broadcasted_iota(I32, (tq, LANE), 1) < HD
        for hp in range(NG // 2):
            cs = slice(hp * LANE, (hp + 1) * LANE)
            prod = do_ref[:, cs] * o_ref[:, cs].astype(F32)
            for hh in range(2):
                d = jnp.sum(jnp.where(lo if hh == 0 else jnp.logical_not(lo), prod, 0.0), axis=1, keepdims=True)
                d_ref[2 * hp + hh:2 * hp + hh + 1, :] = _to_row(jnp.broadcast_to(d, (tq, LANE)))

    return pl.pallas_call(
        body, name="attn_delta", grid=(t // tq,),
        in_specs=[pl.BlockSpec((tq, DM), lambda i: (i, 0)), pl.BlockSpec((tq, DM), lambda i: (i, 0))],
        out_specs=pl.BlockSpec((NG, tq), lambda i: (0, i)), out_shape=jax.ShapeDtypeStruct((NG, t), F32),
        compiler_params=_cparams(("parallel",)))(o, do)


def _attn_bwd(qkv, ckcol, do, lserow, deltarow, *, tq=512):
    t = qkv.shape[0]
    tq = min(tq, t)
    tk = tq
    nq = t // tq
    oj, ii = _causal_pairs(nq, False)
    npairs = int(oj.shape[0])

    def body(oj_ref, ii_ref, q_ref, k_ref, v_ref, ck_ref, do_ref, lse_ref, delta_ref, dq_ref, dk_ref, dv_ref, dcq_ref,
             dckr_ref, dqa_ref, dka_ref, dva_ref, dck_ref):
        n = pl.program_id(0)
        j, i = oj_ref[n], ii_ref[n]

        @pl.when(n == 0)
        def _():
            dqa_ref[...] = jnp.zeros_like(dqa_ref)
            dcq_ref[...] = jnp.zeros_like(dcq_ref)

        @pl.when(i == j)
        def _():
            dka_ref[...] = jnp.zeros_like(dka_ref)
            dva_ref[...] = jnp.zeros_like(dva_ref)
            dck_ref[...] = jnp.zeros_like(dck_ref)

        def step(masked):
            if masked:
                keep = lax.broadcasted_iota(I32, (tk, tq), 0) <= lax.broadcasted_iota(I32, (tk, tq), 1)
            lo = lax.broadcasted_iota(I32, (tk, LANE), 1) < HD
            for hp in range(NG // 2):
                cs = slice(hp * LANE, (hp + 1) * LANE)
                qp, kp, vp, dop = q_ref[:, cs], k_ref[:, cs], v_ref[:, cs], do_ref[:, cs].astype(BF16)
                dvs, dks, dqs = [], [], []
                for hh in range(2):
                    h = 2 * hp + hh
                    sel = lo if hh == 0 else jnp.logical_not(lo)
                    st = _dot(jnp.where(sel, kp, jnp.zeros_like(kp)), qp, NT) - _rep(ck_ref[h], tq)
                    if masked:
                        st = jnp.where(keep, st, NEG)
                    pt = jnp.exp(st - lse_ref[h:h + 1, :])
                    dvs.append(_dot(pt.astype(BF16), dop))
                    dpt = _dot(jnp.where(sel, vp, jnp.zeros_like(vp)), dop, NT)
                    dst = pt * (dpt - delta_ref[h:h + 1, :])
                    dsb = dst.astype(BF16)
                    dks.append(_dot(dsb, qp))
                    dqs.append(_dot(dsb, kp, TN))
                    dck_ref[h] -= jnp.sum(dst, axis=1, keepdims=True)
                    dcq_ref[i, h:h + 1, :] += jnp.sum(dst, axis=0, keepdims=True)
                dva_ref[:, cs] += jnp.where(lo, dvs[0], dvs[1])
                dka_ref[:, cs] += jnp.where(lo, dks[0], dks[1])
                dqa_ref[i, :, cs] += jnp.where(lo, dqs[0], dqs[1])

        @pl.when(i == j)
        def _():
            step(True)

        @pl.when(i > j)
        def _():
            step(False)

        @pl.when(i == nq - 1)
        def _():
            dk_ref[...] = dka_ref[...].astype(BF16)
            dv_ref[...] = dva_ref[...].astype(BF16)
            for h in range(NG):
                dckr_ref[h:h + 1, :] = _to_row(dck_ref[h])

        @pl.when(n == npairs - 1)
        def _():
            dq_ref[...] = (dqa_ref[...] * ATT_SCALE).astype(BF16)

    gs = pltpu.PrefetchScalarGridSpec(
        num_scalar_prefetch=2, grid=(npairs,),
        in_specs=[pl.BlockSpec((tq, DM), lambda n, a, b: (b[n], 0)),
                  pl.BlockSpec((tk, DM), lambda n, a, b: (a[n], 1)),
                  pl.BlockSpec((tk, DM), lambda n, a, b: (a[n], 2)),
                  pl.BlockSpec((NG, tk, LANE), lambda n, a, b: (0, a[n], 0)),
                  pl.BlockSpec((tq, DM), lambda n, a, b: (b[n], 0)),
                  pl.BlockSpec((NG, tq), lambda n, a, b: (0, b[n])),
                  pl.BlockSpec((NG, tq), lambda n, a, b: (0, b[n]))],
        out_specs=[pl.BlockSpec((nq, tq, DM), lambda n, a, b: (0, 0, 0)),
                   pl.BlockSpec((tk, DM), lambda n, a, b: (a[n], 0)),
                   pl.BlockSpec((tk, DM), lambda n, a, b: (a[n], 0)),
                   pl.BlockSpec((nq, NG, tq), lambda n, a, b: (0, 0, 0)),
                   pl.BlockSpec((NG, tk), lambda n, a, b: (0, a[n]))],
        scratch_shapes=[pltpu.VMEM((nq, tq, DM), F32), pltpu.VMEM((tk, DM), F32), pltpu.VMEM((tk, DM), F32),
                        pltpu.VMEM((NG, tk, LANE), F32)])
    return pl.pallas_call(
        body, name="attn_bwd", grid_spec=gs,
        out_shape=[jax.ShapeDtypeStruct((nq, tq, DM), BF16), jax.ShapeDtypeStruct((t, DM), BF16),
                   jax.ShapeDtypeStruct((t, DM), BF16), jax.ShapeDtypeStruct((nq, NG, tq), F32),
                   jax.ShapeDtypeStruct((NG, t), F32)],
        compiler_params=_cparams(("arbitrary",)))(oj, ii, qkv, qkv, qkv, ckcol, do, lserow, deltarow)


def _merge_fwd(ga, zb, att, proj, wa, wb, wc, *, tm=256):
    t = ga.shape[0]
    tm = min(tm, t)
    wspec = pl.BlockSpec((DM, D), lambda i: (0, 0))

    def body(ga_ref, zb_ref, att_ref, gate_ref, wa_ref, wb_ref, wc_ref, m_ref):
        acc = jnp.zeros((tm, D), F32)
        for b, (x_ref, w_ref) in enumerate(((ga_ref, wa_ref), (zb_ref, wb_ref), (att_ref, wc_ref))):
            acc = acc + _sigmoid(gate_ref[:, b * D:(b + 1) * D]) * _dot(x_ref[...], w_ref[...])
        m_ref[...] = acc.astype(BF16)

    return pl.pallas_call(
        body, name="merge_fwd", grid=(t // tm,),
        in_specs=[_rows(tm, DM), _rows(tm, DM), _rows(tm, DM), _rows(tm, 3 * D, 0), wspec, wspec, wspec],
        out_specs=_rows(tm, D), out_shape=jax.ShapeDtypeStruct((t, D), BF16),
        compiler_params=_cparams(("parallel",)))(ga, zb, att, proj, wa, wb, wc)


def _merge_bwd(dm, ga, zb, att, proj, wa, wb, wc, *, tm=256):
    t = ga.shape[0]
    tm = min(tm, t)
    wspec = pl.BlockSpec((DM, D), lambda i: (0, 0))

    def body(dm_ref, ga_ref, zb_ref, att_ref, gate_ref, wa_ref, wb_ref, wc_ref, dgate_ref, dga_ref, dzb_ref, datt_ref,
             dwa_ref, dwb_ref, dwc_ref):
        @pl.when(pl.program_id(0) == 0)
        def _():
            dwa_ref[...] = jnp.zeros_like(dwa_ref)
            dwb_ref[...] = jnp.zeros_like(dwb_ref)
            dwc_ref[...] = jnp.zeros_like(dwc_ref)

        dmv = dm_ref[...]
        branches = ((ga_ref, wa_ref, dga_ref, dwa_ref), (zb_ref, wb_ref, dzb_ref, dwb_ref),
                    (att_ref, wc_ref, datt_ref, dwc_ref))
        for b, (x_ref, w_ref, dx_ref, dw_ref) in enumerate(branches):
            xv, wv = x_ref[...], w_ref[...]
            y = _dot(xv, wv)
            g = _sigmoid(gate_ref[:, b * D:(b + 1) * D])
            dgate_ref[:, b * D:(b + 1) * D] = (dmv * y * g * (1.0 - g)).astype(BF16)
            dy = (dmv * g).astype(BF16)
            dx_ref[...] = _dot(dy, wv, NT)
            dw_ref[...] += _dot(xv, dy, TN)

    return pl.pallas_call(
        body, name="merge_bwd", grid=(t // tm,),
        in_specs=[_rows(tm, D), _rows(tm, DM), _rows(tm, DM), _rows(tm, DM), _rows(tm, 3 * D, 0), wspec, wspec, wspec],
        out_specs=[_rows(tm, 3 * D), _rows(tm, DM), _rows(tm, DM), _rows(tm, DM), wspec, wspec, wspec],
        out_shape=[jax.ShapeDtypeStruct((t, 3 * D), BF16)] + [jax.ShapeDtypeStruct((t, DM), F32)] * 3
        + [jax.ShapeDtypeStruct((DM, D), F32)] * 3,
        compiler_params=_cparams(("arbitrary",)))(dm, ga, zb, att, proj, wa, wb, wc)


def _heads_layout(cum):
    t = cum.shape[0]
    ckrow = cum[:, :NG].T
    return ckrow, jnp.broadcast_to(ckrow[:, :, None], (NG, t, LANE))


def _layer_fwd(x, mod, w, late=None):
    sh1, sc1, gt1, sh2, sc2, gt2 = (mod[k:k + 1] for k in range(NMOD))
    h1 = _norm_mod(x, w["mix_pre_g"], sc1, sh1, name="norm_mix")
    proj = _matmul(h1, w["w_in_p"], name="mm_proj")
    ga = _gmlp_fwd(proj, w["gmlp_ln_g"], w["gmlp_ln_b"], w["wsm"], w["bsx"])
    zc, zb = _conv_fwd(proj, w["conv_w"], w["conv_b"], w["conv_ln_g"], w["conv_ln_b"])
    cum = _fox_cum(proj, w["bfp"])
    ckrow, ckcol = _heads_layout(cum)
    qkv = _qkv_prep(proj)
    att, lser = _attn_fwd(qkv, ckrow)
    if late is not None:
        w = {**w, **late(att)}
    merged = _merge_fwd(ga, zb, att, proj, w["w_a_out"], w["w_b_out"], w["w_c_out"])
    y1 = _matmul(merged, w["w_out"], name="mm_out")
    x2 = _resid(x, y1, gt1, w["mix_post_g"], name="resid_mix")
    h2 = _norm_mod(x2, w["mlp_pre_g"], sc2, sh2, name="norm_mlp")
    hid = _matmul(h2, w["mlp_w1"], name="mm_w1", b_slabs=True, out_dtype=BF16,
                  epilogue=lambda acc: (jnp.square(jnp.maximum(acc, 0.0)),))
    y2 = _matmul(hid, w["mlp_w2"], name="mm_w2", tk=TK_DEEP)
    x3 = _resid(x2, y2, gt2, w["mlp_post_g"], name="resid_mlp")
    saved = dict(x=x, h1=h1, proj=proj, ga=ga, zc=zc, zb=zb, qkv=qkv, ckrow=ckrow, ckcol=ckcol, att=att, lser=lser, merged=merged,
                 y1=y1, x2=x2, h2=h2, hid=hid, y2=y2)
    return x3, saved, w


def _layer_bwd(dx3, mod, w, s, mid=None):
    sh1, sc1, gt1, sh2, sc2, gt2 = (mod[k:k + 1] for k in range(NMOD))
    g = {}
    dy2, dgt2, g["mlp_post_g"] = _resid_bwd(dx3, s["y2"], gt2, w["mlp_post_g"], name="resid_mlp_bwd")
    da = _matmul(dy2, w["mlp_w2"], tb=True, name="mm_dhid", out_dtype=BF16, extra=(s["hid"],),
                 epilogue=lambda acc, hid: (acc * (2.0 * jnp.sqrt(hid.astype(F32))),))
    g["mlp_w2"] = _matmul(s["hid"], dy2, ta=True, name="mm_dw2", tk=TK_DEEP)
    g["mlp_w1"] = _matmul(s["h2"], da, ta=True, name="mm_dw1", out_slabs=N_CHIPS, tk=TK_DEEP)
    dh2 = _matmul(da, w["mlp_w1"], tb=True, name="mm_dh2", b_slabs=True, tk=TK_DEEP)
    dx2, g["mlp_pre_g"], dsc2, dsh2 = _norm_bwd(dh2, dx3, s["x2"], w["mlp_pre_g"], sc2, name="norm_mlp_bwd")
    if mid is not None:
        gt1 = gt1 + mid(dx2)
    dy1, dgt1, g["mix_post_g"] = _resid_bwd(dx2, s["y1"], gt1, w["mix_post_g"], name="resid_mix_bwd")
    dmerged = _matmul(dy1, w["w_out"], tb=True, name="mm_dmerged")
    g["w_out"] = _matmul(s["merged"], dy1, ta=True, name="mm_dwout", tk=TK_DEEP)
    dgate, dga, dzb, datt, g["w_a_out"], g["w_b_out"], g["w_c_out"] = _merge_bwd(
        dmerged, s["ga"], s["zb"], s["att"], s["proj"], w["w_a_out"], w["w_b_out"], w["w_c_out"])
    duv, g["gmlp_ws"], dbs, g["gmlp_ln_g"], g["gmlp_ln_b"] = _gmlp_bwd(
        dga, s["proj"], w["gmlp_ln_g"], w["gmlp_ln_b"], w["wsm"], w["wsmt"], w["bsx"])
    g["gmlp_bs"] = dbs[:, :NG].T
    dzc, g["conv_ln_g"], g["conv_ln_b"] = _conv_bwd_ln(dzb, s["zc"], w["conv_ln_g"], w["conv_ln_b"])
    dglu, dcw, g["conv_b"] = _conv_bwd(dzc, s["proj"], w["conv_w"])
    g["conv_w"] = dcw[:KW]
    t = dx3.shape[0]
    dq, dk, dv, dcq, dck = _attn_bwd(s["qkv"], s["ckcol"], datt, s["lser"], _attn_delta(s["att"], datt))
    dq = dq.reshape(t, DM)
    dcum = jnp.pad((dcq.transpose(1, 0, 2).reshape(NG, t) + dck).T, ((0, 0), (0, LANE - NG)))
    df, dbf = _fox_cum_bwd(dcum, s["proj"], w["bfp"])
    g["fox_bf"] = dbf[0, :NG]
    dproj = jnp.concatenate([dgate, duv, dglu, dq, dk, dv, df], axis=1)
    g["w_in_p"] = _matmul(s["h1"], dproj, ta=True, name="mm_dwin", tk=TK_DEEP)
    dh1 = _matmul(dproj, w["w_in_p"], tb=True, name="mm_dh1", tk=D_INP // 2)
    dx, g["mix_pre_g"], dsc1, dsh1 = _norm_bwd(dh1, dx2, s["x"], w["mix_pre_g"], sc1, name="norm_mix_bwd")
    dmod = jnp.concatenate([dsh1, dsc1, dgt1, dsh2, dsc2, dgt2], axis=0)
    return dx, g, dmod


def _position():
    return lax.axis_index("x"), lax.axis_index("y"), lax.axis_index("c")


def _all_gather8(v):
    m_per, n = v.shape

    def body(x_ref, out_ref, send_sems, recv_sems, local_sem):
        x, y, c = _position()
        me, sibling = (x, y, c), (x, y, 1 - c)
        chips = [(1 - x, y), (x, 1 - y), (1 - x, 1 - y)]

        def rows(px, py, pc):
            return out_ref.at[pl.ds((4 * px + 2 * py + pc) * m_per, m_per), :]

        def copy(k, block, to, src=None):
            return pltpu.make_async_remote_copy(
                src_ref=rows(*block) if src is None else src, dst_ref=rows(*block), send_sem=send_sems.at[k],
                recv_sem=recv_sems.at[k], device_id=to, device_id_type=MESH)

        mine = pltpu.make_async_copy(x_ref, rows(*me), local_sem)
        mine.start()
        first = [copy(0, me, sibling, src=x_ref)]
        first += [copy(1 + j, me, (*chip, c), src=x_ref) for j, chip in enumerate(chips)]
        for cp in first:
            cp.start()
        passed = [copy(4 + j, (*chip, c), sibling) for j, chip in enumerate(chips)]
        for j, chip in enumerate(chips):
            copy(1 + j, (*chip, c), me).wait_recv()
            passed[j].start()
        copy(0, sibling, me).wait_recv()
        for j, chip in enumerate(chips):
            copy(4 + j, (*chip, 1 - c), me).wait_recv()
        for cp in first + passed:
            cp.wait_send()
        mine.wait()

    out = pl.pallas_call(
        body, name="all_gather8", out_shape=jax.ShapeDtypeStruct((N_DEV * m_per, n), v.dtype),
        in_specs=[pl.BlockSpec(memory_space=pltpu.VMEM)], out_specs=pl.BlockSpec(memory_space=pltpu.VMEM),
        scratch_shapes=[pltpu.SemaphoreType.DMA((7,)), pltpu.SemaphoreType.DMA((7,)), pltpu.SemaphoreType.DMA],
        compiler_params=pltpu.CompilerParams(vmem_limit_bytes=VMEM_LIMIT),
    )(v)
    return out.reshape(N_DEV, m_per, n)


def _half(c, rows):
    return pl.ds(c * (rows // 2), rows // 2)


HBM = pl.BlockSpec(memory_space=pltpu.HBM)
SEM = pl.BlockSpec(memory_space=pltpu.SEMAPHORE)
EFFECT = pltpu.SideEffectType.DATAFLOW_SIDE_EFFECTING


_COPIES_PER_ARRAY = {"gather": 3, "scatter": 3, "swap": N_CHIPS}


def _ici_copies(kind, src_refs, land_refs, send_sems, recv_sems, arriving):
    x, y, c = _position()
    chips = [(1 - x, y), (x, 1 - y), (1 - x, 1 - y)]
    nper = _COPIES_PER_ARRAY[kind]
    copies = []
    for wi, (src, land) in enumerate(zip(src_refs, land_refs)):
        for k in range(nper):
            if kind == "swap":
                peer = (x, y, 1 - c)
                s_win, dst = src.at[k, _half(1 - c, src.shape[1])], land.at[k]
            else:
                px, py = chips[k]
                peer = (px, py, c)
                if kind == "gather":
                    rows = src.shape[0]
                    s_win = src.at[_half(c, rows)]
                    dst = land.at[2 * px + py if arriving else 2 * x + y, _half(c, rows)]
                else:
                    s_win, dst = src.at[2 * px + py], land.at[k]
            copies.append(pltpu.make_async_remote_copy(
                src_ref=s_win, dst_ref=dst, send_sem=send_sems.at[wi * nper + k], recv_sem=recv_sems.at[wi * nper + k],
                device_id=peer, device_id_type=MESH))
    return copies


def _ici_start(kind, srcs, land_shapes, after, *, name):
    nw = len(srcs)

    def body(*refs):
        src_refs, land_refs = refs[:nw], refs[nw:2 * nw]
        send_sems, recv_sems = refs[2 * nw + 1:2 * nw + 3]
        token = refs[-1]
        for cp in _ici_copies(kind, src_refs, land_refs, send_sems, recv_sems, False):
            cp.start()
        token[...] = jnp.zeros_like(token)

    lands = [pltpu.with_memory_space_constraint(lax.empty(shp, s.dtype), pltpu.HBM) for shp, s in zip(land_shapes, srcs)]
    outs = pl.pallas_call(
        body, name=name,
        out_shape=(pltpu.SemaphoreType.DMA((_COPIES_PER_ARRAY[kind] * nw,)), pltpu.SemaphoreType.DMA((_COPIES_PER_ARRAY[kind] * nw,)),
                   *[pltpu.HBM(s.shape, s.dtype) for s in srcs], *[pltpu.HBM(shp, s.dtype) for shp, s in zip(land_shapes, srcs)],
                   jax.ShapeDtypeStruct((8, LANE), F32)),
        in_specs=[HBM] * (2 * nw) + [ANY], out_specs=(SEM, SEM, *[HBM] * (2 * nw), pl.BlockSpec(memory_space=pltpu.VMEM)),
        input_output_aliases={i: 2 + i for i in range(2 * nw)},
        compiler_params=pltpu.CompilerParams(has_side_effects=EFFECT),
    )(*[pltpu.with_memory_space_constraint(s, pltpu.HBM) for s in srcs], *lands, after)
    return outs[0], outs[1], outs[2:2 + nw], outs[2 + nw:2 + 2 * nw], outs[-1]


def _ici_wait(kind, send_sems, recv_sems, srcs, lands, after, *, name):
    nw = len(srcs)

    def body(*refs):
        src_refs, land_refs = refs[:nw], refs[nw:2 * nw]
        s_sems, r_sems = refs[2 * nw:2 * nw + 2]
        for cp in _ici_copies(kind, src_refs, land_refs, s_sems, r_sems, False):
            cp.wait_send()
        for cp in _ici_copies(kind, src_refs, land_refs, s_sems, r_sems, True):
            cp.wait_recv()

    outs = pl.pallas_call(
        body, name=name,
        out_shape=(*[pltpu.HBM(s.shape, s.dtype) for s in srcs], *[pltpu.HBM(a.shape, a.dtype) for a in lands]),
        in_specs=[HBM] * (2 * nw) + [SEM, SEM, ANY], out_specs=tuple([HBM] * (2 * nw)),
        input_output_aliases={i: i for i in range(2 * nw)},
        compiler_params=pltpu.CompilerParams(has_side_effects=EFFECT),
    )(*srcs, *lands, send_sems, recv_sems, after)
    return outs[:nw], outs[nw:]


def _ag_d2d(lands, *, name):
    nw = len(lands)

    def body(*refs):
        out_refs = refs[nw:2 * nw]
        send_sems, recv_sems = refs[2 * nw:]
        x, y, c = _position()
        chips = [(1 - x, y), (x, 1 - y), (1 - x, 1 - y)]
        copies = []
        for wi in range(nw):
            rows = out_refs[wi].shape[1]
            for k, (px, py) in enumerate(chips):
                win = out_refs[wi].at[2 * px + py, _half(c, rows)]
                copies.append(pltpu.make_async_remote_copy(
                    src_ref=win, dst_ref=win, send_sem=send_sems.at[wi * 3 + k], recv_sem=recv_sems.at[wi * 3 + k],
                    device_id=(x, y, 1 - c), device_id_type=MESH))
        for cp in copies:
            cp.start()
        for wi in range(nw):
            rows = out_refs[wi].shape[1]
            for k, (px, py) in enumerate(chips):
                win = out_refs[wi].at[2 * px + py, _half(1 - c, rows)]
                pltpu.make_async_remote_copy(
                    src_ref=win, dst_ref=win, send_sem=send_sems.at[wi * 3 + k], recv_sem=recv_sems.at[wi * 3 + k],
                    device_id=(x, y, 1 - c), device_id_type=MESH).wait_recv()
        for cp in copies:
            cp.wait_send()

    return pl.pallas_call(
        body, name=name, out_shape=[jax.ShapeDtypeStruct(a.shape, a.dtype) for a in lands],
        in_specs=[ANY] * nw, out_specs=[ANY] * nw, input_output_aliases={i: i for i in range(nw)},
        scratch_shapes=[pltpu.SemaphoreType.DMA((3 * nw,)), pltpu.SemaphoreType.DMA((3 * nw,))],
    )(*lands)


def _swap_reduced(reds):
    nw = len(reds)

    def body(*refs):
        r_refs, out_refs = refs[:nw], refs[nw:2 * nw]
        send_sems, recv_sems = refs[2 * nw:]
        x, y, c = _position()
        copies = [pltpu.make_async_remote_copy(
            src_ref=r_refs[wi], dst_ref=out_refs[wi], send_sem=send_sems.at[wi], recv_sem=recv_sems.at[wi],
            device_id=(x, y, 1 - c), device_id_type=MESH) for wi in range(nw)]
        for cp in copies:
            cp.start()
        for cp in copies:
            cp.wait()

    return pl.pallas_call(
        body, name="rs_swap_reduced", out_shape=[jax.ShapeDtypeStruct(r.shape, r.dtype) for r in reds],
        in_specs=[ANY] * nw, out_specs=[ANY] * nw,
        scratch_shapes=[pltpu.SemaphoreType.DMA((nw,)), pltpu.SemaphoreType.DMA((nw,))],
    )(*reds)


def _row_tile(rows, cols):
    tr = rows
    while tr * cols * 4 > (2 << 20) and tr % 32 == 0:
        tr //= 2
    return tr


def _add_own_half(g, recv, c, *, name):
    nj, h, w = recv.shape
    tr = _row_tile(h, w)
    nb = h // tr

    def body(c_ref, g_ref, r_ref, o_ref, ob_ref):
        sm = g_ref[0] + r_ref[0]
        o_ref[0] = sm
        ob_ref[0] = sm.astype(BF16)

    spec = pl.BlockSpec((1, tr, w), lambda j, i, cc: (j, i, 0))
    gs = pltpu.PrefetchScalarGridSpec(
        num_scalar_prefetch=1, grid=(nj, nb),
        in_specs=[pl.BlockSpec((1, tr, w), lambda j, i, cc: (j, cc[0] * nb + i, 0)), spec],
        out_specs=[spec, spec])
    return pl.pallas_call(body, name=name, grid_spec=gs,
                          out_shape=[jax.ShapeDtypeStruct((nj, h, w), F32), jax.ShapeDtypeStruct((nj, h, w), BF16)],
                          compiler_params=_cparams(("parallel", "parallel")))(jnp.reshape(c, (1,)).astype(I32), g, recv)


def _add_own_chip(sp, recv, j, *, name):
    _, r, w = sp.shape
    tr = _row_tile(r, w)

    def body(j_ref, s_ref, r_ref, o_ref):
        o_ref[...] = ((s_ref[0] + r_ref[0].astype(F32)) + r_ref[1].astype(F32)) + r_ref[2].astype(F32)

    gs = pltpu.PrefetchScalarGridSpec(
        num_scalar_prefetch=1, grid=(r // tr,),
        in_specs=[pl.BlockSpec((1, tr, w), lambda i, jj: (jj[0], i, 0)), pl.BlockSpec((3, tr, w), lambda i, jj: (0, i, 0))],
        out_specs=pl.BlockSpec((tr, w), lambda i, jj: (i, 0)))
    return pl.pallas_call(body, name=name, grid_spec=gs, out_shape=jax.ShapeDtypeStruct((r, w), F32),
                          compiler_params=_cparams(("parallel",)))(jnp.reshape(j, (1,)).astype(I32), sp, recv)


def _sum8(v):
    _, m, n = v.shape

    def body(v_ref, o_ref):
        acc = v_ref[0]
        for k in range(1, N_DEV):
            acc = acc + v_ref[k]
        o_ref[...] = acc

    return pl.pallas_call(body, name="sum8", grid=(1,), in_specs=[pl.BlockSpec((N_DEV, m, n), lambda i: (0, 0, 0))],
                          out_specs=pl.BlockSpec((m, n), lambda i: (0, 0)), out_shape=jax.ShapeDtypeStruct((m, n), F32),
                          compiler_params=_cparams(("arbitrary",)))(v)


def _ada_mod(c_all, ada_w, ada_b_loc, *, tn=512):
    nl, _, ncol = ada_w.shape

    def body(c_ref, w_ref, b_ref, o_ref):
        cv = c_ref[...]
        ca = (cv * _sigmoid(cv)).astype(BF16)
        o_ref[0] = _dot(ca, w_ref[0].astype(BF16)) + b_ref[0]

    return pl.pallas_call(
        body, name="ada_mod", grid=(nl, ncol // tn),
        in_specs=[pl.BlockSpec((N_DEV, D), lambda l, j: (0, 0)), pl.BlockSpec((1, D, tn), lambda l, j: (l, 0, j)),
                  pl.BlockSpec((1, 1, tn), lambda l, j: (l, 0, j))],
        out_specs=pl.BlockSpec((1, N_DEV, tn), lambda l, j: (l, 0, j)),
        out_shape=jax.ShapeDtypeStruct((nl, N_DEV, ncol), F32),
        compiler_params=_cparams(("parallel", "parallel")))(c_all, ada_w, ada_b_loc)


def _ada_grad(c_pad, dmod_pad, *, tn=512):
    nl, nb, ncol = dmod_pad.shape

    def body(c_ref, d_ref, o_ref):
        cv = c_ref[...]
        ca = (cv * _sigmoid(cv)).astype(BF16)
        o_ref[0] = _dot(ca, d_ref[0].astype(BF16), TN)

    return pl.pallas_call(
        body, name="ada_grad", grid=(nl, ncol // tn),
        in_specs=[pl.BlockSpec((nb, D), lambda l, j: (0, 0)), pl.BlockSpec((1, nb, tn), lambda l, j: (l, 0, j))],
        out_specs=pl.BlockSpec((1, D, tn), lambda l, j: (l, 0, j)),
        out_shape=jax.ShapeDtypeStruct((nl, D, ncol), F32),
        compiler_params=_cparams(("parallel", "parallel")))(c_pad, dmod_pad)


def _adamw(w, g, m, v, *, name):
    shape = w.shape
    if w.ndim == 3:
        lead, rows, cols = shape
    else:
        lead, (rows, cols) = 1, shape
    w3, g3, m3, v3 = (a.reshape(lead, rows, cols) for a in (w, g, m, v))
    tr = rows
    if rows * cols * 4 > (2 << 20):
        tr = next(cand for cand in (256, 128, 64, 8) if rows % cand == 0)
    c1 = 1.0 - ADAM_B1 ** ADAM_STEP
    c2 = 1.0 - ADAM_B2 ** ADAM_STEP

    def body(w_ref, g_ref, m_ref, v_ref, d_ref, nm_ref, nv_ref):
        gv = g_ref[...]
        nm = ADAM_B1 * m_ref[...] + (1.0 - ADAM_B1) * gv
        nv = ADAM_B2 * v_ref[...] + (1.0 - ADAM_B2) * (gv * gv)
        nm_ref[...] = nm
        nv_ref[...] = nv
        d_ref[...] = -ADAM_LR * ((nm / c1) / (jnp.sqrt(nv / c2) + ADAM_EPS) + ADAM_WD * w_ref[...])

    spec = pl.BlockSpec((1, tr, cols), lambda l, i: (l, i, 0))
    outs = pl.pallas_call(
        body, name=name, grid=(lead, rows // tr), in_specs=[spec] * 4, out_specs=[spec] * 3,
        out_shape=[jax.ShapeDtypeStruct((lead, rows, cols), F32)] * 3,
        compiler_params=_cparams(("parallel", "parallel")))(w3, g3, m3, v3)
    return tuple(o.reshape(shape) for o in outs)


def _adamw_layer(w, g_own, g_sib, m, v, c, layer, prev, after, *, name):
    _, rows, cols = w.shape
    tr = _row_tile(rows // 2, cols)
    nbh = rows // 2 // tr
    c1 = 1.0 - ADAM_B1 ** ADAM_STEP
    c2 = 1.0 - ADAM_B2 ** ADAM_STEP
    n_prev = 0 if prev is None else 4

    def body(c_ref, w_ref, o_ref, s_ref, m_ref, v_ref, *rest):
        g_ref, d_ref, nm_ref, nv_ref = rest[n_prev + 1:]
        gv = jnp.where(pl.program_id(0) // nbh == c_ref[0], o_ref[...], s_ref[...])
        nm = ADAM_B1 * m_ref[0] + (1.0 - ADAM_B1) * gv
        nv = ADAM_B2 * v_ref[0] + (1.0 - ADAM_B2) * (gv * gv)
        g_ref[0] = gv
        nm_ref[0] = nm
        nv_ref[0] = nv
        d_ref[0] = -ADAM_LR * ((nm / c1) / (jnp.sqrt(nv / c2) + ADAM_EPS) + ADAM_WD * w_ref[0])

    def source(own):
        return pl.BlockSpec((tr, cols), lambda i, cc: (jnp.where((i // nbh == cc[0]) == own, i % nbh, 0), 0))

    spec = pl.BlockSpec((1, tr, cols), lambda i, cc: (layer, i, 0))
    gs = pltpu.PrefetchScalarGridSpec(
        num_scalar_prefetch=1, grid=(2 * nbh,),
        in_specs=[spec, source(True), source(False), spec, spec] + [ANY] * (n_prev + 1), out_specs=[spec] * 4)
    return pl.pallas_call(
        body, name=name, grid_spec=gs, out_shape=[jax.ShapeDtypeStruct(w.shape, F32)] * 4,
        input_output_aliases={6 + k: k for k in range(n_prev)},
        compiler_params=_cparams(("parallel",)))(
            jnp.reshape(c, (1,)).astype(I32), w, g_own, g_sib, m, v, *(prev or ()), after)


SMALL = (("mix_pre_g", (2, D)), ("mix_post_g", (2, D)), ("mlp_pre_g", (2, D)), ("mlp_post_g", (2, D)),
         ("gmlp_ln_g", (2, DM)), ("gmlp_ln_b", (2, DM)), ("gmlp_ws", (2, NG, CH, CH)), ("gmlp_bs", (2, NG, CH)),
         ("conv_b", (2, DM)), ("conv_ln_g", (2, DM)), ("conv_ln_b", (2, DM)), ("fox_bf", (2, NG)))


def _pack_rows(arrays):
    parts = []
    for a in arrays:
        last = a.shape[-1]
        r = a.astype(F32).reshape(-1, LANE) if last % LANE == 0 else jnp.pad(a.astype(F32).reshape(-1, last), ((0, 0), (0, LANE - last)))
        pad = (-r.shape[0]) % 8
        parts.append(jnp.pad(r, ((0, pad), (0, 0))) if pad else r)
    return jnp.concatenate(parts, axis=0)


def _unpack_rows(buf, shapes):
    out, off = [], 0
    for shp in shapes:
        size = 1
        for d in shp:
            size *= d
        last = shp[-1]
        rows = size // (LANE if last % LANE == 0 else last)
        seg = buf[off:off + rows]
        out.append(seg.reshape(shp) if last % LANE == 0 else seg[:, :last].reshape(shp))
        off += rows + (-rows) % 8
    return out


def _assemble_w_in(w_in_full):
    uv_glu_qkv = w_in_full[:, :3584]
    f = w_in_full[:, 3584:3592]
    gate = w_in_full[:, 3592:]
    fpad = jnp.zeros((D, D_INP - C_F - NG), w_in_full.dtype)
    return jnp.concatenate([gate, uv_glu_qkv, f, fpad], axis=1)


def _disassemble_w_in(g_p):
    return jnp.concatenate([g_p[:, C_UV:C_F], g_p[:, C_F:C_F + NG], g_p[:, :C_UV]], axis=1)


def kernel(x, c, ada_w, ada_b, mix_pre_g, mix_post_g, mlp_pre_g, mlp_post_g, w_in, gmlp_ln_g, gmlp_ln_b, gmlp_ws, gmlp_bs, w_a_out, conv_w, conv_b, conv_ln_g, conv_ln_b, w_b_out, fox_bf, w_c_out, w_out, mlp_w1, mlp_w2, loss_target, m_ada_w, m_ada_b, m_mix_pre_g, m_mix_post_g, m_mlp_pre_g, m_mlp_post_g, m_w_in, m_gmlp_ln_g, m_gmlp_ln_b, m_gmlp_ws, m_gmlp_bs, m_w_a_out, m_conv_w, m_conv_b, m_conv_ln_g, m_conv_ln_b, m_w_b_out, m_fox_bf, m_w_c_out, m_w_out, m_mlp_w1, m_mlp_w2, v_ada_w, v_ada_b, v_mix_pre_g, v_mix_post_g, v_mlp_pre_g, v_mlp_post_g, v_w_in, v_gmlp_ln_g, v_gmlp_ln_b, v_gmlp_ws, v_gmlp_bs, v_w_a_out, v_conv_w, v_conv_b, v_conv_ln_g, v_conv_ln_b, v_w_b_out, v_fox_bf, v_w_c_out, v_w_out, v_mlp_w1, v_mlp_w2):
    weights = dict(ada_w=ada_w, ada_b=ada_b, mix_pre_g=mix_pre_g, mix_post_g=mix_post_g, mlp_pre_g=mlp_pre_g,
                   mlp_post_g=mlp_post_g, w_in=w_in, gmlp_ln_g=gmlp_ln_g, gmlp_ln_b=gmlp_ln_b, gmlp_ws=gmlp_ws,
                   gmlp_bs=gmlp_bs, w_a_out=w_a_out, conv_w=conv_w, conv_b=conv_b, conv_ln_g=conv_ln_g,
                   conv_ln_b=conv_ln_b, w_b_out=w_b_out, fox_bf=fox_bf, w_c_out=w_c_out, w_out=w_out, mlp_w1=mlp_w1,
                   mlp_w2=mlp_w2)
    mom_m = dict(ada_w=m_ada_w, ada_b=m_ada_b, mix_pre_g=m_mix_pre_g, mix_post_g=m_mix_post_g, mlp_pre_g=m_mlp_pre_g,
                 mlp_post_g=m_mlp_post_g, w_in=m_w_in, gmlp_ln_g=m_gmlp_ln_g, gmlp_ln_b=m_gmlp_ln_b, gmlp_ws=m_gmlp_ws,
                 gmlp_bs=m_gmlp_bs, w_a_out=m_w_a_out, conv_w=m_conv_w, conv_b=m_conv_b, conv_ln_g=m_conv_ln_g,
                 conv_ln_b=m_conv_ln_b, w_b_out=m_w_b_out, fox_bf=m_fox_bf, w_c_out=m_w_c_out, w_out=m_w_out,
                 mlp_w1=m_mlp_w1, mlp_w2=m_mlp_w2)
    mom_v = dict(ada_w=v_ada_w, ada_b=v_ada_b, mix_pre_g=v_mix_pre_g, mix_post_g=v_mix_post_g, mlp_pre_g=v_mlp_pre_g,
                 mlp_post_g=v_mlp_post_g, w_in=v_w_in, gmlp_ln_g=v_gmlp_ln_g, gmlp_ln_b=v_gmlp_ln_b, gmlp_ws=v_gmlp_ws,
                 gmlp_bs=v_gmlp_bs, w_a_out=v_w_a_out, conv_w=v_conv_w, conv_b=v_conv_b, conv_ln_g=v_conv_ln_g,
                 conv_ln_b=v_conv_ln_b, w_b_out=v_w_b_out, fox_bf=v_fox_bf, w_c_out=v_w_c_out, w_out=v_w_out,
                 mlp_w1=v_mlp_w1, mlp_w2=v_mlp_w2)
    order = list(weights)
    px, py, pc = _position()
    chip = 2 * px + py
    dev = 2 * chip + pc
    depth = ada_w.shape[0]
    t = x.shape[1]
    xl = x.reshape(t, D)
    tgt = loss_target.reshape(t, D)

    big_names = [b[0] for b in BIG]
    shards = [[weights[n][l].astype(BF16) for n in big_names] for l in range(depth)]
    land_shapes = [(N_CHIPS,) + sh.shape for sh in shards[0]]

    small_in = _pack_rows([c, conv_w])
    gathered = _all_gather8(small_in)
    c_all = gathered[:, :D // LANE, :].reshape(N_DEV, D)
    cw_rows = depth * KW * LANE // LANE
    conv_w_full = jnp.concatenate(
        [gathered[2 * j, D // LANE:D // LANE + cw_rows, :].reshape(depth, KW, LANE) for j in range(N_CHIPS)], axis=2)

    ncol = ada_w.shape[2]
    ada_b_loc = lax.dynamic_slice_in_dim(ada_b, chip * ncol, ncol, axis=1).reshape(depth, 1, ncol)
    mod_sh = _ada_mod(c_all, ada_w, ada_b_loc)
    mod_g = _all_gather8(mod_sh.reshape(-1, LANE)).reshape(N_DEV, depth, N_DEV, ncol)
    mod_all = jnp.concatenate([mod_g[2 * j] for j in range(N_CHIPS)], axis=2)
    mod_mine = lax.dynamic_index_in_dim(mod_all, dev, axis=1, keepdims=False)

    mods = [mod_mine[l].reshape(NMOD, D) for l in range(depth)]

    def gathered_weights(idx, waited):
        own, lands = waited
        lands = _ag_d2d(lands, name="ag_d2d")
        w = {}
        for i, sh, g in zip(idx, own, lands):
            n, r, cdim, ax = BIG[i]
            g = lax.dynamic_update_slice(g, sh[None], (chip, 0, 0))
            if n == "w_in":
                w["w_in_p"] = _assemble_w_in(g.transpose(1, 0, 2).reshape(r, N_CHIPS * cdim))
            elif n == "mlp_w1":
                w[n] = g
            elif ax == 1:
                w[n] = g.transpose(1, 0, 2).reshape(r, N_CHIPS * cdim)
            else:
                w[n] = g.reshape(N_CHIPS * r, cdim)
        return w

    def local_weights(l):
        w = {}
        for n in ("mix_pre_g", "mix_post_g", "mlp_pre_g", "mlp_post_g", "gmlp_ln_g", "gmlp_ln_b", "conv_b", "conv_ln_g",
                  "conv_ln_b"):
            w[n] = weights[n][l:l + 1]
        tril = jnp.tril(jnp.ones((CH, CH), F32))
        wsm = gmlp_ws[l] * tril
        w["wsm"] = wsm.astype(BF16)
        w["wsmt"] = jnp.swapaxes(wsm, 1, 2).astype(BF16)
        w["bsx"] = jnp.repeat(gmlp_bs[l].T, HD, axis=1)
        w["conv_w"] = conv_w_full[l]
        w["bfp"] = jnp.pad(fox_bf[l], (0, LANE - NG)).reshape(1, LANE)
        return w

    def slabs(gfull, n, r, cdim, ax):
        if n == "mlp_w1":
            return gfull
        if ax == 1:
            return gfull.reshape(gfull.shape[0], N_CHIPS, cdim).transpose(1, 0, 2)
        return gfull.reshape(N_CHIPS, r, cdim)

    def swap_start(g, after, name):
        g["w_in"] = _disassemble_w_in(g.pop("w_in_p"))
        gs = [slabs(g[n], n, r, cdim, ax) for n, r, cdim, ax in BIG]
        return _ici_start("swap", gs, [(N_CHIPS, a.shape[1] // 2, a.shape[2]) for a in gs], after, name=name)

    def chip_sums(sw, after, name):
        gs, from_sibling = _ici_wait("swap", sw[0], sw[1], sw[2], sw[3], after, name=name)
        return [_add_own_half(a, rv, pc, name="rs_add_half_" + n) for a, rv, n in zip(gs, from_sibling, big_names)]

    def reduce_rest(sums, from_chips):
        red = [_add_own_chip(sf, rv, chip, name="rs_add_chip_" + n) for (sf, _), rv, n in zip(sums, from_chips, big_names)]
        return red, _swap_reduced(red)

    assert depth == 2
    ga, gb = [0], list(range(1, len(BIG)))

    def pick(seq, idx):
        return [seq[i] for i in idx]

    ag0a = _ici_start("gather", pick(shards[0], ga), pick(land_shapes, ga), mod_mine, name="ag0a_start")
    lands0a = _ici_wait("gather", ag0a[0], ag0a[1], ag0a[2], ag0a[3], ag0a[4], name="ag0a_wait")
    ag0b = _ici_start("gather", pick(shards[0], gb), pick(land_shapes, gb), lands0a[1][0], name="ag0b_start")
    ag1 = _ici_start("gather", shards[1], land_shapes, ag0b[4], name="ag1_start")
    mod0 = mods[0] + ag1[4][0, 0]
    layers, saved = [None] * depth, [None] * depth

    def late0(att):
        return gathered_weights(gb, _ici_wait("gather", ag0b[0], ag0b[1], ag0b[2], ag0b[3], att, name="ag0b_wait"))

    xs, saved[0], layers[0] = _layer_fwd(xl, mod0, {**local_weights(0), **gathered_weights(ga, lands0a)}, late0)
    lands1 = _ici_wait("gather", ag1[0], ag1[1], ag1[2], ag1[3], xs, name="ag1_wait")
    xs, saved[1], layers[1] = _layer_fwd(xs, mods[1], {**local_weights(1), **gathered_weights(ga + gb, lands1)})
    loss_local, dx = _loss_and_grad(xs, tgt)
    loss = lax.psum(loss_local, ("x", "y", "c"))
    grads, dmods = [None] * depth, [None] * depth
    dx, grads[1], dmods[1] = _layer_bwd(dx, mods[1], layers[1], saved[1])
    sw1 = swap_start(grads[1], dx, "sw1_start")
    rs_l1 = {}

    def mid0(dx2):
        rs_l1["sums"] = chip_sums(sw1, dx2, "sw1_wait")
        sbf1 = [sb for _, sb in rs_l1["sums"]]
        rs_l1["rs"] = _ici_start("scatter", sbf1, [(3,) + sb.shape[1:] for sb in sbf1], dx2, name="rs1_start")
        return rs_l1["rs"][4][0:1, 0:1]

    dx, grads[0], dmods[0] = _layer_bwd(dx, mod0 + sw1[4][0, 0], layers[0], saved[0], mid0)
    grad_x = dx.reshape(x.shape)
    sums1, rs1 = rs_l1["sums"], rs_l1["rs"]
    slot_shapes = [(3,) + sb.shape[1:] for _, sb in sums1]
    sw0 = swap_start(grads[0], dx, "sw0_start")
    g_out = {}

    small_names = [n for n, _ in SMALL]
    small_list = [jnp.stack(dmods)] + [jnp.stack([grads[l][n] for l in range(depth)]) for n in small_names]
    small_list.append(jnp.stack([grads[l]["conv_w"] for l in range(depth)]))
    small_shapes = [(depth, NMOD * D)] + [shp for _, shp in SMALL] + [(depth, KW, DM)]
    small_all = _all_gather8(_pack_rows(small_list) + sw0[4][0, 0])
    sums0 = chip_sums(sw0, small_all, "sw0_wait")
    rs0 = _ici_start("scatter", [sb for _, sb in sums0], slot_shapes, small_all, name="rs0_start")
    tok0 = rs0[4][0, 0]
    red1 = reduce_rest(sums1, _ici_wait("scatter", rs1[0], rs1[1], rs1[2], rs1[3], rs0[4], name="rs1_wait")[1])
    c_all = c_all + tok0
    small_sum = _unpack_rows(_sum8(small_all) + tok0, small_shapes)
    g_out["ada_b"] = small_sum[0]
    for n, gs in zip(small_names, small_sum[1:-1]):
        g_out[n] = gs
    g_out["conv_w"] = lax.dynamic_slice_in_dim(small_sum[-1], chip * LANE, LANE, axis=2)
    dmod_all = small_all[:, :depth * NMOD * D // LANE, :].reshape(N_DEV, depth, NMOD * D)
    dmod_loc = lax.dynamic_slice_in_dim(dmod_all, chip * ncol, ncol, axis=2).transpose(1, 0, 2)
    g_out["ada_w"] = _ada_grad(jnp.pad(c_all, ((0, 8), (0, 0))), jnp.pad(dmod_loc, ((0, 0), (0, 8), (0, 0))))

    delta, new_m, new_v = {}, {}, {}
    delta["ada_w"], new_m["ada_w"], new_v["ada_w"] = _adamw(ada_w, g_out["ada_w"], m_ada_w, v_ada_w, name="adamw_ada_w")
    ws_rows = (depth * NG * CH, CH)
    ws_out = _adamw(*[d["gmlp_ws"].reshape(ws_rows) for d in (weights, g_out, mom_m, mom_v)], name="adamw_gmlp_ws")
    delta["gmlp_ws"], new_m["gmlp_ws"], new_v["gmlp_ws"] = (a.reshape(gmlp_ws.shape) for a in ws_out)
    small_params = ["ada_b"] + [n for n in small_names if n != "gmlp_ws"] + ["conv_w"]
    packs = [_pack_rows([d[n] for n in small_params]) for d in (weights, g_out, mom_m, mom_v)]
    outs = _adamw(*packs, name="adamw_small")
    shapes = [weights[n].shape for n in small_params]
    for dst, buf in zip((delta, new_m, new_v), outs):
        for n, a in zip(small_params, _unpack_rows(buf, shapes)):
            dst[n] = a
    half = {n: _adamw_layer(weights[n], red1[0][wi], red1[1][wi], mom_m[n], mom_v[n], pc, 1, None, rs0[4],
                            name="adamw1_" + n) for wi, n in enumerate(big_names)}
    done = jnp.stack([delta["ada_w"][0, 0, 0], outs[0][0, 0]] + [half[n][1][1, 0, 0] for n in big_names])
    red0 = reduce_rest(sums0, _ici_wait("scatter", rs0[0], rs0[1], rs0[2], rs0[3], done, name="rs0_wait")[1])
    for wi, n in enumerate(big_names):
        g_out[n], delta[n], new_m[n], new_v[n] = _adamw_layer(
            weights[n], red0[0][wi], red0[1][wi], mom_m[n], mom_v[n], pc, 0, half[n], rs0[4], name="adamw0_" + n)

    return (loss, grad_x, *[g_out[n] for n in order], *[delta[n] for n in order], *[new_m[n] for n in order],
            *[new_v[n] for n in order])
```

```python
import functools

import jax
import jax.numpy as jnp
from jax import lax
from jax.experimental import pallas as pl
from jax.experimental.pallas import tpu as pltpu

F32 = jnp.float32
BF16 = jnp.bfloat16
I32 = jnp.int32
MESH = pl.DeviceIdType.MESH
ANY = pl.BlockSpec(memory_space=pl.ANY)

D = 1024
DM = 512
NG = 8
CH = 128
KW = 31
HALO = 32
DFF = 4096
NMOD = 6
EPS = 1e-6
LANE = 128
N_CHIPS = 4
N_DEV = 8
C_GATE, C_UV, C_GLU, C_Q, C_K, C_V, C_F, D_INP = 0, 3072, 4096, 5120, 5632, 6144, 6656, 7168
D_IN = 6664
VMEM_LIMIT = 56 * 1024 * 1024
ROW_TILE = 1024
TK_DEEP = 4096

ADAM_LR, ADAM_B1, ADAM_B2, ADAM_EPS, ADAM_WD, ADAM_STEP = 0.001, 0.9, 0.999, 1e-08, 0.01, 10

BIG = (("w_in", 1024, 1666, 1), ("w_a_out", 512, 256, 1), ("w_b_out", 512, 256, 1), ("w_c_out", 512, 256, 1),
       ("w_out", 256, 1024, 0), ("mlp_w1", 1024, 1024, 1), ("mlp_w2", 1024, 1024, 0))


def _cparams(sem):
    return pltpu.CompilerParams(dimension_semantics=sem, vmem_limit_bytes=VMEM_LIMIT)


def _sigmoid(x):
    return jax.nn.sigmoid(x)


_GELU_K = 0.7978845608028654
_GELU_A = 0.044715


def _gelu(x):
    t = jnp.tanh(_GELU_K * (x + _GELU_A * x * x * x))
    return 0.5 * x * (1.0 + t)


def _gelu_grad(x):
    t = jnp.tanh(_GELU_K * (x + _GELU_A * x * x * x))
    return 0.5 * (1.0 + t) + 0.5 * x * (1.0 - t * t) * _GELU_K * (1.0 + 3.0 * _GELU_A * x * x)


def _mean(x):
    return jnp.mean(x, axis=-1, keepdims=True)


def _colsum(x):
    return jnp.sum(x, axis=0, keepdims=True)


def _dot(a, b, dims=((1,), (0,))):
    return lax.dot_general(a, b, (dims, ((), ())), preferred_element_type=F32)


NN = ((1,), (0,))
NT = ((1,), (1,))
TN = ((0,), (0,))


def _matmul(a, b, *, name, ta=False, tb=False, out_dtype=F32, tm=1024, tn=1024, tk=1024, epilogue=None, extra=(),
            extra_out=(), b_slabs=False, out_slabs=0):
    m, k = (a.shape[1], a.shape[0]) if ta else a.shape
    if b_slabs:
        ns, brows, bw = b.shape
        n = brows if tb else ns * bw
        assert (ns * bw if tb else brows) == k, (name, b.shape, k)
        all_slabs = tb and tk >= k
        tn, tk = (tn, k if all_slabs else bw) if tb else (bw, tk)
    else:
        n = b.shape[0] if tb else b.shape[1]
        all_slabs = False
    tm, tn, tk = min(tm, m), min(tn, n), min(tk, k)
    assert m % tm == 0 and n % tn == 0 and k % tk == 0, (name, m, n, k, tm, tn, tk)
    assert not out_slabs or (n // out_slabs == tn and epilogue is None), name
    nk = k // tk
    dims = ((0 if ta else 1,), (1 if tb else 0,))
    n_extra = len(extra)
    out_dtypes = (out_dtype,) + tuple(extra_out)

    def body(a_ref, b_ref, *rest):
        extra_refs = rest[:n_extra]
        out_refs = rest[n_extra:n_extra + len(out_dtypes)]
        kk = pl.program_id(2)
        if all_slabs:
            part = sum(_dot(a_ref[:, s * bw:(s + 1) * bw].astype(BF16), b_ref[s].astype(BF16), dims) for s in range(ns))
        else:
            part = _dot(a_ref[...].astype(BF16), b_ref[...].astype(BF16), dims)

        def finish(acc):
            outs = (acc,) if epilogue is None else epilogue(acc, *[r[...] for r in extra_refs])
            for o_ref, o in zip(out_refs, outs):
                o_ref[...] = o.astype(o_ref.dtype)

        if nk == 1:
            finish(part)
        else:
            acc_ref = rest[-1]

            @pl.when(kk == 0)
            def _():
                acc_ref[...] = part

            @pl.when(jnp.logical_and(kk > 0, kk < nk - 1))
            def _():
                acc_ref[...] += part

            @pl.when(kk == nk - 1)
            def _():
                finish(acc_ref[...] + part)

    a_spec = pl.BlockSpec((tk, tm), lambda i, j, kk: (kk, i)) if ta else pl.BlockSpec((tm, tk), lambda i, j, kk: (i, kk))
    if all_slabs:
        b_spec = pl.BlockSpec((ns, tn, bw), lambda i, j, kk: (0, j, 0))
    elif b_slabs and tb:
        b_spec = pl.BlockSpec((None, tn, tk), lambda i, j, kk: (kk, j, 0))
    elif b_slabs:
        b_spec = pl.BlockSpec((None, tk, tn), lambda i, j, kk: (j, kk, 0))
    else:
        b_spec = pl.BlockSpec((tn, tk), lambda i, j, kk: (j, kk)) if tb else pl.BlockSpec((tk, tn), lambda i, j, kk: (kk, j))
    if out_slabs:
        o_spec = pl.BlockSpec((None, tm, tn), lambda i, j, kk: (j, i, 0))
        o_shape = (out_slabs, m, tn)
    else:
        o_spec = pl.BlockSpec((tm, tn), lambda i, j, kk: (i, j))
        o_shape = (m, n)
    outs = pl.pallas_call(
        body, name=name, grid=(m // tm, n // tn, nk),
        in_specs=[a_spec, b_spec] + [o_spec] * n_extra,
        out_specs=[o_spec] * len(out_dtypes),
        out_shape=[jax.ShapeDtypeStruct(o_shape, dt) for dt in out_dtypes],
        scratch_shapes=[pltpu.VMEM((tm, tn), F32)] if nk > 1 else [],
        compiler_params=_cparams(("parallel", "parallel", "arbitrary")),
    )(a, b, *extra)
    return outs[0] if len(outs) == 1 else outs


def _rows(tm, n, col=0):
    return pl.BlockSpec((tm, n), lambda i: (i, col))


def _vec(n):
    return pl.BlockSpec((1, n), lambda i: (0, 0))


def _norm_mod(x, g, sc, sh, *, name, tm=ROW_TILE):
    t = x.shape[0]
    tm = min(tm, t)

    def body(x_ref, g_ref, sc_ref, sh_ref, h_ref):
        xv = x_ref[...]
        inv = lax.rsqrt(_mean(xv * xv) + EPS)
        h_ref[...] = ((xv * inv * g_ref[...]) * (1.0 + sc_ref[...]) + sh_ref[...]).astype(BF16)

    return pl.pallas_call(
        body, name=name, grid=(t // tm,), in_specs=[_rows(tm, D), _vec(D), _vec(D), _vec(D)],
        out_specs=_rows(tm, D), out_shape=jax.ShapeDtypeStruct((t, D), BF16),
        compiler_params=_cparams(("parallel",)))(x, g, sc, sh)


def _resid(x, y, gt, gp, *, name, tm=ROW_TILE):
    t = x.shape[0]
    tm = min(tm, t)

    def body(x_ref, y_ref, gt_ref, gp_ref, o_ref):
        yv = y_ref[...]
        inv = lax.rsqrt(_mean(yv * yv) + EPS)
        o_ref[...] = x_ref[...] + gt_ref[...] * (yv * inv * gp_ref[...])

    return pl.pallas_call(
        body, name=name, grid=(t // tm,), in_specs=[_rows(tm, D), _rows(tm, D), _vec(D), _vec(D)],
        out_specs=_rows(tm, D), out_shape=jax.ShapeDtypeStruct((t, D), F32),
        compiler_params=_cparams(("parallel",)))(x, y, gt, gp)


def _resid_bwd(dx, y, gt, gp, *, name, tm=ROW_TILE):
    t = dx.shape[0]
    tm = min(tm, t)

    def body(dx_ref, y_ref, gt_ref, gp_ref, dy_ref, dgt_ref, dgp_ref):
        @pl.when(pl.program_id(0) == 0)
        def _():
            dgt_ref[...] = jnp.zeros_like(dgt_ref)
            dgp_ref[...] = jnp.zeros_like(dgp_ref)

        dxv, yv, gp_v = dx_ref[...], y_ref[...], gp_ref[...]
        inv = lax.rsqrt(_mean(yv * yv) + EPS)
        yh = yv * inv
        dgt_ref[...] += _colsum(dxv * (yh * gp_v))
        dr = dxv * gt_ref[...]
        dgp_ref[...] += _colsum(dr * yh)
        dyn = dr * gp_v
        dy_ref[...] = (inv * (dyn - yh * _mean(dyn * yh))).astype(BF16)

    return pl.pallas_call(
        body, name=name, grid=(t // tm,), in_specs=[_rows(tm, D), _rows(tm, D), _vec(D), _vec(D)],
        out_specs=[_rows(tm, D), _vec(D), _vec(D)],
        out_shape=[jax.ShapeDtypeStruct((t, D), BF16), jax.ShapeDtypeStruct((1, D), F32),
                   jax.ShapeDtypeStruct((1, D), F32)],
        compiler_params=_cparams(("arbitrary",)))(dx, y, gt, gp)


def _norm_bwd(dh, dx_res, x, g, sc, *, name, tm=ROW_TILE):
    t = dh.shape[0]
    tm = min(tm, t)

    def body(dh_ref, dxr_ref, x_ref, g_ref, sc_ref, dx_ref, dg_ref, dsc_ref, dsh_ref):
        @pl.when(pl.program_id(0) == 0)
        def _():
            dg_ref[...] = jnp.zeros_like(dg_ref)
            dsc_ref[...] = jnp.zeros_like(dsc_ref)
            dsh_ref[...] = jnp.zeros_like(dsh_ref)

        dhv, xv, gv = dh_ref[...], x_ref[...], g_ref[...]
        inv = lax.rsqrt(_mean(xv * xv) + EPS)
        xh = xv * inv
        dsh_ref[...] += _colsum(dhv)
        dsc_ref[...] += _colsum(dhv * (xh * gv))
        dn = dhv * (1.0 + sc_ref[...])
        dg_ref[...] += _colsum(dn * xh)
        dxh = dn * gv
        dx_ref[...] = inv * (dxh - xh * _mean(dxh * xh)) + dxr_ref[...]

    vec_out = jax.ShapeDtypeStruct((1, D), F32)
    return pl.pallas_call(
        body, name=name, grid=(t // tm,), in_specs=[_rows(tm, D), _rows(tm, D), _rows(tm, D), _vec(D), _vec(D)],
        out_specs=[_rows(tm, D), _vec(D), _vec(D), _vec(D)],
        out_shape=[jax.ShapeDtypeStruct((t, D), F32), vec_out, vec_out, vec_out],
        compiler_params=_cparams(("arbitrary",)))(dh, dx_res, x, g, sc)


def _loss_and_grad(x, target, *, tm=ROW_TILE):
    t = x.shape[0]
    tm = min(tm, t)

    def body(x_ref, t_ref, loss_ref, dx_ref):
        @pl.when(pl.program_id(0) == 0)
        def _():
            loss_ref[...] = jnp.zeros_like(loss_ref)

        e = x_ref[...] - t_ref[...]
        dx_ref[...] = e * (1.0 / D)
        s = jnp.sum(jnp.sum(e * e, axis=1, keepdims=True), axis=0, keepdims=True) * (0.5 / D)
        loss_ref[...] += jnp.broadcast_to(s, loss_ref.shape)

    loss, dx = pl.pallas_call(
        body, name="loss", grid=(t // tm,), in_specs=[_rows(tm, D), _rows(tm, D)],
        out_specs=[pl.BlockSpec((8, LANE), lambda i: (0, 0)), _rows(tm, D)],
        out_shape=[jax.ShapeDtypeStruct((8, LANE), F32), jax.ShapeDtypeStruct((t, D), F32)],
        compiler_params=_cparams(("arbitrary",)))(x, target)
    return loss[0, 0], dx


def _gmlp_core(uv, lng, lnb, ws_ref, bsx):
    tm = uv.shape[0]
    gu = _gelu(uv[:, :DM])
    gv = _gelu(uv[:, DM:])
    mu = _mean(gv)
    vc = gv - mu
    rstd = lax.rsqrt(_mean(vc * vc) + EPS)
    vh = vc * rstd
    vln = vh * lng + lnb
    lane = lax.broadcasted_iota(I32, (CH, LANE), 1)
    sv_rows = []
    for nchunk in range(tm // CH):
        vb = vln[nchunk * CH:(nchunk + 1) * CH].astype(BF16)
        cols = []
        for cb in range(DM // LANE):
            vcb = vb[:, cb * LANE:(cb + 1) * LANE]
            lo = _dot(ws_ref[2 * cb], vcb)
            hi = _dot(ws_ref[2 * cb + 1], vcb)
            cols.append(jnp.where(lane < 64, lo, hi))
        sv_rows.append(jnp.concatenate(cols, axis=1) + bsx)
    sv = jnp.concatenate(sv_rows, axis=0) if len(sv_rows) > 1 else sv_rows[0]
    return gu, vh, rstd, vln, sv


def _gmlp_fwd(proj, lng, lnb, wsm, bsx, *, tm=256):
    t = proj.shape[0]
    tm = min(tm, t)

    def body(uv_ref, lng_ref, lnb_ref, ws_ref, bs_ref, ga_ref):
        gu, _, _, _, sv = _gmlp_core(uv_ref[...], lng_ref[...], lnb_ref[...], ws_ref, bs_ref[...])
        ga_ref[...] = (gu * sv).astype(BF16)

    return pl.pallas_call(
        body, name="gmlp_fwd", grid=(t // tm,),
        in_specs=[_rows(tm, 2 * DM, C_UV // (2 * DM)), _vec(DM), _vec(DM),
                  pl.BlockSpec((NG, CH, CH), lambda i: (0, 0, 0)), pl.BlockSpec((CH, DM), lambda i: (0, 0))],
        out_specs=_rows(tm, DM), out_shape=jax.ShapeDtypeStruct((t, DM), BF16),
        compiler_params=_cparams(("parallel",)))(proj, lng, lnb, wsm, bsx)


def _gmlp_bwd(dga, proj, lng, lnb, wsm, wsmt, bsx, *, tm=256):
    t = proj.shape[0]
    tm = min(tm, t)

    def body(dga_ref, uv_ref, lng_ref, lnb_ref, ws_ref, wst_ref, bs_ref, duv_ref, dws_ref, dbs_ref, dlng_ref, dlnb_ref,
             dbsx_ref):
        i = pl.program_id(0)

        @pl.when(i == 0)
        def _():
            dws_ref[...] = jnp.zeros_like(dws_ref)
            dbsx_ref[...] = jnp.zeros_like(dbsx_ref)
            dlng_ref[...] = jnp.zeros_like(dlng_ref)
            dlnb_ref[...] = jnp.zeros_like(dlnb_ref)

        uv = uv_ref[...]
        lng_v = lng_ref[...]
        gu, vh, rstd, vln, sv = _gmlp_core(uv, lng_v, lnb_ref[...], ws_ref, bs_ref[...])
        dga_v = dga_ref[...]
        dgu = dga_v * sv
        dsv = dga_v * gu
        lane = lax.broadcasted_iota(I32, (CH, LANE), 1)
        tril = lax.broadcasted_iota(I32, (CH, CH), 0) >= lax.broadcasted_iota(I32, (CH, CH), 1)
        dvln_rows = []
        for nchunk in range(tm // CH):
            rows = slice(nchunk * CH, (nchunk + 1) * CH)
            dbsx_ref[...] += dsv[rows]
            vb = vln[rows].astype(BF16)
            cols = []
            for cb in range(DM // LANE):
                cs = slice(cb * LANE, (cb + 1) * LANE)
                dsvb = dsv[rows, cs]
                vcb = vb[:, cs]
                dlo = jnp.where(lane < 64, dsvb, 0.0).astype(BF16)
                dhi = jnp.where(lane < 64, 0.0, dsvb).astype(BF16)
                dws_ref[2 * cb] += jnp.where(tril, _dot(dlo, vcb, NT), 0.0)
                dws_ref[2 * cb + 1] += jnp.where(tril, _dot(dhi, vcb, NT), 0.0)
                dsb = dsvb.astype(BF16)
                cols.append(jnp.where(lane < 64, _dot(wst_ref[2 * cb], dsb), _dot(wst_ref[2 * cb + 1], dsb)))
            dvln_rows.append(jnp.concatenate(cols, axis=1))
        dvln = jnp.concatenate(dvln_rows, axis=0) if len(dvln_rows) > 1 else dvln_rows[0]
        dlnb_ref[...] += _colsum(dvln)
        dlng_ref[...] += _colsum(dvln * vh)
        dvh = dvln * lng_v
        dgv = rstd * (dvh - _mean(dvh) - vh * _mean(dvh * vh))
        duv_ref[:, :DM] = (dgu * _gelu_grad(uv[:, :DM])).astype(BF16)
        duv_ref[:, DM:] = (dgv * _gelu_grad(uv[:, DM:])).astype(BF16)

        @pl.when(i == pl.num_programs(0) - 1)
        def _():
            ind = (lax.broadcasted_iota(I32, (DM, LANE), 0) // 64 == lax.broadcasted_iota(I32, (DM, LANE), 1)).astype(F32)
            dbs_ref[...] = jnp.dot(dbsx_ref[...], ind, preferred_element_type=F32, precision=lax.Precision.HIGHEST)

    vec_out = jax.ShapeDtypeStruct((1, DM), F32)
    outs = pl.pallas_call(
        body, name="gmlp_bwd", grid=(t // tm,),
        in_specs=[_rows(tm, DM), _rows(tm, 2 * DM, C_UV // (2 * DM)), _vec(DM), _vec(DM),
                  pl.BlockSpec((NG, CH, CH), lambda i: (0, 0, 0)), pl.BlockSpec((NG, CH, CH), lambda i: (0, 0, 0)),
                  pl.BlockSpec((CH, DM), lambda i: (0, 0))],
        out_specs=[_rows(tm, 2 * DM), pl.BlockSpec((NG, CH, CH), lambda i: (0, 0, 0)),
                   pl.BlockSpec((CH, LANE), lambda i: (0, 0)), _vec(DM), _vec(DM)],
        out_shape=[jax.ShapeDtypeStruct((t, 2 * DM), BF16), jax.ShapeDtypeStruct((NG, CH, CH), F32),
                   jax.ShapeDtypeStruct((CH, LANE), F32), vec_out, vec_out],
        scratch_shapes=[pltpu.VMEM((CH, DM), F32)],
        compiler_params=_cparams(("arbitrary",)))(dga, proj, lng, lnb, wsm, wsmt, bsx)
    return outs


def _glu_into(zs_ref, glu_ref, halo_ref, first):
    hal = halo_ref[...]
    z0h = hal[:, :DM] * _sigmoid(hal[:, DM:])
    zs_ref[0:HALO, :] = jnp.where(first, 0.0, z0h)
    g = glu_ref[...]
    zs_ref[HALO:, :] = g[:, :DM] * _sigmoid(g[:, DM:])


SUB = 8


def _shifted_copies(dst_ref, src_ref):
    n = src_ref.shape[0]
    for s in range(SUB):
        dst_ref[s, 0:n - s, :] = src_ref[s:n, :]


def _window(shifted_ref, off, rows, cs):
    s = off % SUB
    return shifted_ref[s, off - s:off - s + rows, cs]


def _conv_fwd(proj, cw, cb, lng, lnb, *, tm=256, rb=64):
    t = proj.shape[0]
    tm = min(tm, t)
    hb = tm // HALO
    gcol = C_GLU // (2 * DM)

    def body(glu_ref, halo_ref, cw_ref, cb_ref, lng_ref, lnb_ref, zc_ref, zb_ref, zs_ref, zsh_ref):
        i = pl.program_id(0)
        _glu_into(zs_ref, glu_ref, halo_ref, i == 0)
        _shifted_copies(zsh_ref, zs_ref)
        for cbk in range(DM // LANE):
            cs = slice(cbk * LANE, (cbk + 1) * LANE)
            for r in range(tm // rb):
                acc = jnp.broadcast_to(cb_ref[:, cs], (rb, LANE))
                for k in range(KW):
                    acc = acc + cw_ref[k:k + 1, cs] * _window(zsh_ref, r * rb + HALO - (KW - 1) + k, rb, cs)
                zc_ref[r * rb:(r + 1) * rb, cs] = acc
        zc = zc_ref[...]
        mu = _mean(zc)
        zcc = zc - mu
        zh = zcc * lax.rsqrt(_mean(zcc * zcc) + EPS)
        a = zh * lng_ref[...] + lnb_ref[...]
        zb_ref[...] = (a * _sigmoid(a)).astype(BF16)

    return pl.pallas_call(
        body, name="conv_fwd", grid=(t // tm,),
        in_specs=[_rows(tm, 2 * DM, gcol),
                  pl.BlockSpec((HALO, 2 * DM), lambda i: (jnp.maximum(i * hb - 1, 0), gcol)),
                  pl.BlockSpec((KW, DM), lambda i: (0, 0)), _vec(DM), _vec(DM), _vec(DM)],
        out_specs=[_rows(tm, DM), _rows(tm, DM)],
        out_shape=[jax.ShapeDtypeStruct((t, DM), F32), jax.ShapeDtypeStruct((t, DM), BF16)],
        scratch_shapes=[pltpu.VMEM((HALO + tm, DM), F32), pltpu.VMEM((SUB, HALO + tm, DM), F32)],
        compiler_params=_cparams(("parallel",)))(proj, proj, cw, cb, lng, lnb)


def _conv_bwd_ln(dzb, zc, lng, lnb, *, tm=ROW_TILE):
    t = zc.shape[0]
    tm = min(tm, t)

    def body(dzb_ref, zc_ref, lng_ref, lnb_ref, dzc_ref, dlng_ref, dlnb_ref):
        @pl.when(pl.program_id(0) == 0)
        def _():
            dlng_ref[...] = jnp.zeros_like(dlng_ref)
            dlnb_ref[...] = jnp.zeros_like(dlnb_ref)

        zc = zc_ref[...]
        lng_v = lng_ref[...]
        mu = _mean(zc)
        zcc = zc - mu
        rstd = lax.rsqrt(_mean(zcc * zcc) + EPS)
        zh = zcc * rstd
        a = zh * lng_v + lnb_ref[...]
        s = _sigmoid(a)
        da = dzb_ref[...] * (s * (1.0 + a * (1.0 - s)))
        dlnb_ref[...] += _colsum(da)
        dlng_ref[...] += _colsum(da * zh)
        dzh = da * lng_v
        dzc_ref[...] = rstd * (dzh - _mean(dzh) - zh * _mean(dzh * zh))

    vec_out = jax.ShapeDtypeStruct((1, DM), F32)
    return pl.pallas_call(
        body, name="conv_bwd_ln", grid=(t // tm,), in_specs=[_rows(tm, DM), _rows(tm, DM), _vec(DM), _vec(DM)],
        out_specs=[_rows(tm, DM), _vec(DM), _vec(DM)],
        out_shape=[jax.ShapeDtypeStruct((t, DM), F32), vec_out, vec_out],
        compiler_params=_cparams(("arbitrary",)))(dzb, zc, lng, lnb)


def _conv_bwd(dzc, proj, cw, *, tm=256, rb=64):
    t = proj.shape[0]
    tm = min(tm, t)
    hb = tm // HALO
    nblk = t // tm
    gcol = C_GLU // (2 * DM)

    def body(dzc_ref, dnext_ref, glu_ref, halo_ref, cw_ref, dglu_ref, dcw_ref, dcb_ref, zs_ref, ds_ref, zsh_ref, dsh_ref):
        i = pl.program_id(0)

        @pl.when(i == 0)
        def _():
            dcw_ref[...] = jnp.zeros_like(dcw_ref)
            dcb_ref[...] = jnp.zeros_like(dcb_ref)

        _glu_into(zs_ref, glu_ref, halo_ref, i == 0)
        _shifted_copies(zsh_ref, zs_ref)
        dzc = dzc_ref[...]
        ds_ref[0:tm, :] = dzc
        ds_ref[tm:, :] = jnp.where(i == nblk - 1, 0.0, dnext_ref[...])
        _shifted_copies(dsh_ref, ds_ref)
        dcb_ref[...] += _colsum(dzc)
        for k in range(KW):
            dcw_ref[k:k + 1, :] += _colsum(dzc * _window(zsh_ref, HALO - (KW - 1) + k, tm, slice(None)))
        g = glu_ref[...]
        val, sg = g[:, :DM], _sigmoid(g[:, DM:])
        for cbk in range(DM // LANE):
            cs = slice(cbk * LANE, (cbk + 1) * LANE)
            for r in range(tm // rb):
                acc = jnp.zeros((rb, LANE), F32)
                for k in range(KW):
                    acc = acc + cw_ref[k:k + 1, cs] * _window(dsh_ref, r * rb + (KW - 1) - k, rb, cs)
                rs = slice(r * rb, (r + 1) * rb)
                dglu_ref[rs, cs] = (acc * sg[rs, cs]).astype(BF16)
                v, s = val[rs, cs], sg[rs, cs]
                dglu_ref[rs, DM + cbk * LANE:DM + (cbk + 1) * LANE] = (acc * v * s * (1.0 - s)).astype(BF16)

    return pl.pallas_call(
        body, name="conv_bwd", grid=(nblk,),
        in_specs=[_rows(tm, DM),
                  pl.BlockSpec((HALO, DM), lambda i: (jnp.minimum((i + 1) * hb, nblk * hb - 1), 0)),
                  _rows(tm, 2 * DM, gcol),
                  pl.BlockSpec((HALO, 2 * DM), lambda i: (jnp.maximum(i * hb - 1, 0), gcol)),
                  pl.BlockSpec((KW, DM), lambda i: (0, 0))],
        out_specs=[_rows(tm, 2 * DM), pl.BlockSpec((HALO, DM), lambda i: (0, 0)), _vec(DM)],
        out_shape=[jax.ShapeDtypeStruct((t, 2 * DM), BF16), jax.ShapeDtypeStruct((HALO, DM), F32),
                   jax.ShapeDtypeStruct((1, DM), F32)],
        scratch_shapes=[pltpu.VMEM((HALO + tm, DM), F32), pltpu.VMEM((tm + HALO, DM), F32),
                        pltpu.VMEM((SUB, HALO + tm, DM), F32), pltpu.VMEM((SUB, tm + HALO, DM), F32)],
        compiler_params=_cparams(("arbitrary",)))(dzc, dzc, proj, proj, cw)


CUM_ROWS = 512


def _log_sigmoid(x):
    return jnp.minimum(x, 0.0) - jnp.log1p(jnp.exp(-jnp.abs(x)))


def _fox_cum(proj, bfp):
    t = proj.shape[0]
    fcol = C_F // LANE
    cr = min(CUM_ROWS, t)

    def body(f_ref, bf_ref, cum_ref, carry_ref):
        @pl.when(pl.program_id(0) == 0)
        def _():
            carry_ref[...] = jnp.zeros_like(carry_ref)

        lf = _log_sigmoid(f_ref[...] + bf_ref[...])
        tri = (lax.broadcasted_iota(I32, (cr, cr), 0) >= lax.broadcasted_iota(I32, (cr, cr), 1)).astype(F32)
        cum = jnp.dot(tri, lf, preferred_element_type=F32, precision=lax.Precision.HIGHEST) + carry_ref[0:1, :]
        cum_ref[...] = cum
        carry_ref[...] = jnp.broadcast_to(cum[cr - 1:cr, :], carry_ref.shape)

    return pl.pallas_call(
        body, name="fox_cum", grid=(t // cr,), in_specs=[_rows(cr, LANE, fcol), _vec(LANE)],
        out_specs=_rows(cr, LANE), out_shape=jax.ShapeDtypeStruct((t, LANE), F32),
        scratch_shapes=[pltpu.VMEM((8, LANE), F32)],
        compiler_params=_cparams(("arbitrary",)))(proj, bfp)


def _fox_cum_bwd(dcum, proj, bfp):
    t = proj.shape[0]
    cr = min(CUM_ROWS, t)
    nb = t // cr
    fcol = C_F // LANE
    fw = D_INP - C_F

    def body(dc_ref, f_ref, bf_ref, df_ref, dbf_ref, carry_ref):
        @pl.when(pl.program_id(0) == 0)
        def _():
            carry_ref[...] = jnp.zeros_like(carry_ref)
            dbf_ref[...] = jnp.zeros_like(dbf_ref)

        triu = (lax.broadcasted_iota(I32, (cr, cr), 0) <= lax.broadcasted_iota(I32, (cr, cr), 1)).astype(F32)
        dlf = jnp.dot(triu, dc_ref[...], preferred_element_type=F32, precision=lax.Precision.HIGHEST) + carry_ref[0:1, :]
        carry_ref[...] = jnp.broadcast_to(dlf[0:1, :], carry_ref.shape)
        z = f_ref[...] + bf_ref[...]
        lane = lax.broadcasted_iota(I32, (cr, LANE), 1)
        df = jnp.where(lane < NG, dlf * _sigmoid(-z), 0.0)
        dbf_ref[...] += _colsum(df)
        df_ref[:, 0:LANE] = df.astype(BF16)
        df_ref[:, LANE:] = jnp.zeros((cr, fw - LANE), BF16)

    return pl.pallas_call(
        body, name="fox_cum_bwd", grid=(nb,),
        in_specs=[pl.BlockSpec((cr, LANE), lambda i: (nb - 1 - i, 0)),
                  pl.BlockSpec((cr, LANE), lambda i: (nb - 1 - i, fcol)), _vec(LANE)],
        out_specs=[pl.BlockSpec((cr, fw), lambda i: (nb - 1 - i, 0)), _vec(LANE)],
        out_shape=[jax.ShapeDtypeStruct((t, fw), BF16), jax.ShapeDtypeStruct((1, LANE), F32)],
        scratch_shapes=[pltpu.VMEM((8, LANE), F32)],
        compiler_params=_cparams(("arbitrary",)))(dcum, proj, bfp)


HD = 64
ATT_SCALE = 0.125
NEG = -1e30


def _qkv_prep(proj, *, tm=512):
    t = proj.shape[0]
    tm = min(tm, t)

    def body(q_ref, k_ref, v_ref, o_ref):
        o_ref[:, 0:DM] = (q_ref[...] * ATT_SCALE).astype(BF16)
        o_ref[:, DM:2 * DM] = k_ref[...].astype(BF16)
        o_ref[:, 2 * DM:] = v_ref[...].astype(BF16)

    return pl.pallas_call(
        body, name="qkv_prep", grid=(t // tm,),
        in_specs=[_rows(tm, DM, C_Q // DM), _rows(tm, DM, C_K // DM), _rows(tm, DM, C_V // DM)],
        out_specs=_rows(tm, 3 * DM), out_shape=jax.ShapeDtypeStruct((t, 3 * DM), BF16),
        compiler_params=_cparams(("parallel",)))(proj, proj, proj)


def _causal_pairs(nq, outer_is_query):
    if outer_is_query:
        pairs = [(i, j) for i in range(nq) for j in range(i + 1)]
    else:
        pairs = [(j, i) for j in range(nq) for i in range(j, nq)]
    return (jnp.asarray([p[0] for p in pairs], I32), jnp.asarray([p[1] for p in pairs], I32))


def _to_row(col):
    return jnp.transpose(col)[0:1, :]


def _rep(x, tk):
    return x if tk == LANE else jnp.tile(x, (1, tk // LANE))


def _attn_fwd(qkv, ckrow, *, tq=512):
    t = qkv.shape[0]
    tq = min(tq, t)
    tk = tq
    nq = t // tq
    oi, ij = _causal_pairs(nq, True)

    def body(oi_ref, ij_ref, q_ref, k_ref, v_ref, ck_ref, o_ref, lser_ref, m_ref, l_ref, acc_ref):
        n = pl.program_id(0)
        i, j = oi_ref[n], ij_ref[n]

        @pl.when(j == 0)
        def _():
            m_ref[...] = jnp.full_like(m_ref, NEG)
            l_ref[...] = jnp.zeros_like(l_ref)
            acc_ref[...] = jnp.zeros_like(acc_ref)

        def step(masked):
            if masked:
                keep = lax.broadcasted_iota(I32, (tq, tk), 1) <= lax.broadcasted_iota(I32, (tq, tk), 0)
            lo = lax.broadcasted_iota(I32, (tq, LANE), 1) < HD
            for hp in range(NG // 2):
                cs = slice(hp * LANE, (hp + 1) * LANE)
                qp, kp, vp = q_ref[:, cs], k_ref[:, cs], v_ref[:, cs]
                alphas, pvs = [], []
                for hh in range(2):
                    h = 2 * hp + hh
                    qm = jnp.where(lo if hh == 0 else jnp.logical_not(lo), qp, jnp.zeros_like(qp))
                    s = _dot(qm, kp, NT) - ck_ref[h:h + 1, :]
                    if masked:
                        s = jnp.where(keep, s, NEG)
                    m_prev = m_ref[h]
                    m_new = jnp.maximum(m_prev, jnp.max(s, axis=1, keepdims=True))
                    alpha = jnp.exp(m_prev - m_new)
                    p = jnp.exp(s - _rep(m_new, tk))
                    l_ref[h] = alpha * l_ref[h] + jnp.sum(p, axis=1, keepdims=True)
                    m_ref[h] = m_new
                    alphas.append(alpha)
                    pvs.append(_dot(p.astype(BF16), vp))
                acc_ref[:, cs] = jnp.where(lo, alphas[0], alphas[1]) * acc_ref[:, cs] + jnp.where(lo, pvs[0], pvs[1])

        @pl.when(j < i)
        def _():
            step(False)

        @pl.when(j == i)
        def _():
            step(True)
            lo = lax.broadcasted_iota(I32, (tq, LANE), 1) < HD
            for hp in range(NG // 2):
                cs = slice(hp * LANE, (hp + 1) * LANE)
                o_ref[:, cs] = (acc_ref[:, cs] / jnp.where(lo, l_ref[2 * hp], l_ref[2 * hp + 1])).astype(BF16)
            for h in range(NG):
                lser_ref[h:h + 1, :] = _to_row(m_ref[h] + jnp.log(l_ref[h]))

    gs = pltpu.PrefetchScalarGridSpec(
        num_scalar_prefetch=2, grid=(int(oi.shape[0]),),
        in_specs=[pl.BlockSpec((tq, DM), lambda n, a, b: (a[n], 0)),
                  pl.BlockSpec((tk, DM), lambda n, a, b: (b[n], 1)),
                  pl.BlockSpec((tk, DM), lambda n, a, b: (b[n], 2)),
                  pl.BlockSpec((NG, tk), lambda n, a, b: (0, b[n]))],
        out_specs=[pl.BlockSpec((tq, DM), lambda n, a, b: (a[n], 0)),
                   pl.BlockSpec((NG, tq), lambda n, a, b: (0, a[n]))],
        scratch_shapes=[pltpu.VMEM((NG, tq, LANE), F32), pltpu.VMEM((NG, tq, LANE), F32), pltpu.VMEM((tq, DM), F32)])
    return pl.pallas_call(
        body, name="attn_fwd", grid_spec=gs,
        out_shape=[jax.ShapeDtypeStruct((t, DM), BF16), jax.ShapeDtypeStruct((NG, t), F32)],
        compiler_params=_cparams(("arbitrary",)))(oi, ij, qkv, qkv, qkv, ckrow)


def _attn_delta(o, do, *, tq=512):
    t = o.shape[0]
    tq = min(tq, t)

    def body(o_ref, do_ref, d_ref):
        lo = lax.broadcasted_iota(I32, (tq, LANE), 1) < HD
        for hp in range(NG // 2):
            cs = slice(hp * LANE, (hp + 1) * LANE)
            prod = do_ref[:, cs] * o_ref[:, cs].astype(F32)
            for hh in range(2):
                d = jnp.sum(jnp.where(lo if hh == 0 else jnp.logical_not(lo), prod, 0.0), axis=1, keepdims=True)
                d_ref[2 * hp + hh:2 * hp + hh + 1, :] = _to_row(jnp.broadcast_to(d, (tq, LANE)))

    return pl.pallas_call(
        body, name="attn_delta", grid=(t // tq,),
        in_specs=[pl.BlockSpec((tq, DM), lambda i: (i, 0)), pl.BlockSpec((tq, DM), lambda i: (i, 0))],
        out_specs=pl.BlockSpec((NG, tq), lambda i: (0, i)), out_shape=jax.ShapeDtypeStruct((NG, t), F32),
        compiler_params=_cparams(("parallel",)))(o, do)


def _attn_bwd(qkv, ckcol, do, lserow, deltarow, *, tq=512):
    t = qkv.shape[0]
    tq = min(tq, t)
    tk = tq
    nq = t // tq
    oj, ii = _causal_pairs(nq, False)
    npairs = int(oj.shape[0])

    def body(oj_ref, ii_ref, q_ref, k_ref, v_ref, ck_ref, do_ref, lse_ref, delta_ref, dq_ref, dk_ref, dv_ref, dcq_ref,
             dckr_ref, dqa_ref, dka_ref, dva_ref, dck_ref):
        n = pl.program_id(0)
        j, i = oj_ref[n], ii_ref[n]

        @pl.when(n == 0)
        def _():
            dqa_ref[...] = jnp.zeros_like(dqa_ref)
            dcq_ref[...] = jnp.zeros_like(dcq_ref)

        @pl.when(i == j)
        def _():
            dka_ref[...] = jnp.zeros_like(dka_ref)
            dva_ref[...] = jnp.zeros_like(dva_ref)
            dck_ref[...] = jnp.zeros_like(dck_ref)

        def step(masked):
            if masked:
                keep = lax.broadcasted_iota(I32, (tk, tq), 0) <= lax.broadcasted_iota(I32, (tk, tq), 1)
            lo = lax.broadcasted_iota(I32, (tk, LANE), 1) < HD
            for hp in range(NG // 2):
                cs = slice(hp * LANE, (hp + 1) * LANE)
                qp, kp, vp, dop = q_ref[:, cs], k_ref[:, cs], v_ref[:, cs], do_ref[:, cs].astype(BF16)
                dvs, dks, dqs = [], [], []
                for hh in range(2):
                    h = 2 * hp + hh
                    sel = lo if hh == 0 else jnp.logical_not(lo)
                    st = _dot(jnp.where(sel, kp, jnp.zeros_like(kp)), qp, NT) - _rep(ck_ref[h], tq)
                    if masked:
                        st = jnp.where(keep, st, NEG)
                    pt = jnp.exp(st - lse_ref[h:h + 1, :])
                    dvs.append(_dot(pt.astype(BF16), dop))
                    dpt = _dot(jnp.where(sel, vp, jnp.zeros_like(vp)), dop, NT)
                    dst = pt * (dpt - delta_ref[h:h + 1, :])
                    dsb = dst.astype(BF16)
                    dks.append(_dot(dsb, qp))
                    dqs.append(_dot(dsb, kp, TN))
                    dck_ref[h] -= jnp.sum(dst, axis=1, keepdims=True)
                    dcq_ref[i, h:h + 1, :] += jnp.sum(dst, axis=0, keepdims=True)
                dva_ref[:, cs] += jnp.where(lo, dvs[0], dvs[1])
                dka_ref[:, cs] += jnp.where(lo, dks[0], dks[1])
                dqa_ref[i, :, cs] += jnp.where(lo, dqs[0], dqs[1])

        @pl.when(i == j)
        def _():
            step(True)

        @pl.when(i > j)
        def _():
            step(False)

        @pl.when(i == nq - 1)
        def _():
            dk_ref[...] = dka_ref[...].astype(BF16)
            dv_ref[...] = dva_ref[...].astype(BF16)
            for h in range(NG):
                dckr_ref[h:h + 1, :] = _to_row(dck_ref[h])

        @pl.when(n == npairs - 1)
        def _():
            dq_ref[...] = (dqa_ref[...] * ATT_SCALE).astype(BF16)

    gs = pltpu.PrefetchScalarGridSpec(
        num_scalar_prefetch=2, grid=(npairs,),
        in_specs=[pl.BlockSpec((tq, DM), lambda n, a, b: (b[n], 0)),
                  pl.BlockSpec((tk, DM), lambda n, a, b: (a[n], 1)),
                  pl.BlockSpec((tk, DM), lambda n, a, b: (a[n], 2)),
                  pl.BlockSpec((NG, tk, LANE), lambda n, a, b: (0, a[n], 0)),
                  pl.BlockSpec((tq, DM), lambda n, a, b: (b[n], 0)),
                  pl.BlockSpec((NG, tq), lambda n, a, b: (0, b[n])),
                  pl.BlockSpec((NG, tq), lambda n, a, b: (0, b[n]))],
        out_specs=[pl.BlockSpec((nq, tq, DM), lambda n, a, b: (0, 0, 0)),
                   pl.BlockSpec((tk, DM), lambda n, a, b: (a[n], 0)),
                   pl.BlockSpec((tk, DM), lambda n, a, b: (a[n], 0)),
                   pl.BlockSpec((nq, NG, tq), lambda n, a, b: (0, 0, 0)),
                   pl.BlockSpec((NG, tk), lambda n, a, b: (0, a[n]))],
        scratch_shapes=[pltpu.VMEM((nq, tq, DM), F32), pltpu.VMEM((tk, DM), F32), pltpu.VMEM((tk, DM), F32),
                        pltpu.VMEM((NG, tk, LANE), F32)])
    return pl.pallas_call(
        body, name="attn_bwd", grid_spec=gs,
        out_shape=[jax.ShapeDtypeStruct((nq, tq, DM), BF16), jax.ShapeDtypeStruct((t, DM), BF16),
                   jax.ShapeDtypeStruct((t, DM), BF16), jax.ShapeDtypeStruct((nq, NG, tq), F32),
                   jax.ShapeDtypeStruct((NG, t), F32)],
        compiler_params=_cparams(("arbitrary",)))(oj, ii, qkv, qkv, qkv, ckcol, do, lserow, deltarow)


def _merge_fwd(ga, zb, att, proj, wa, wb, wc, *, tm=256):
    t = ga.shape[0]
    tm = min(tm, t)
    wspec = pl.BlockSpec((DM, D), lambda i: (0, 0))

    def body(ga_ref, zb_ref, att_ref, gate_ref, wa_ref, wb_ref, wc_ref, m_ref):
        acc = jnp.zeros((tm, D), F32)
        for b, (x_ref, w_ref) in enumerate(((ga_ref, wa_ref), (zb_ref, wb_ref), (att_ref, wc_ref))):
            acc = acc + _sigmoid(gate_ref[:, b * D:(b + 1) * D]) * _dot(x_ref[...], w_ref[...])
        m_ref[...] = acc.astype(BF16)

    return pl.pallas_call(
        body, name="merge_fwd", grid=(t // tm,),
        in_specs=[_rows(tm, DM), _rows(tm, DM), _rows(tm, DM), _rows(tm, 3 * D, 0), wspec, wspec, wspec],
        out_specs=_rows(tm, D), out_shape=jax.ShapeDtypeStruct((t, D), BF16),
        compiler_params=_cparams(("parallel",)))(ga, zb, att, proj, wa, wb, wc)


def _merge_bwd(dm, ga, zb, att, proj, wa, wb, wc, *, tm=256):
    t = ga.shape[0]
    tm = min(tm, t)
    wspec = pl.BlockSpec((DM, D), lambda i: (0, 0))

    def body(dm_ref, ga_ref, zb_ref, att_ref, gate_ref, wa_ref, wb_ref, wc_ref, dgate_ref, dga_ref, dzb_ref, datt_ref,
             dwa_ref, dwb_ref, dwc_ref):
        @pl.when(pl.program_id(0) == 0)
        def _():
            dwa_ref[...] = jnp.zeros_like(dwa_ref)
            dwb_ref[...] = jnp.zeros_like(dwb_ref)
            dwc_ref[...] = jnp.zeros_like(dwc_ref)

        dmv = dm_ref[...]
        branches = ((ga_ref, wa_ref, dga_ref, dwa_ref), (zb_ref, wb_ref, dzb_ref, dwb_ref),
                    (att_ref, wc_ref, datt_ref, dwc_ref))
        for b, (x_ref, w_ref, dx_ref, dw_ref) in enumerate(branches):
            xv, wv = x_ref[...], w_ref[...]
            y = _dot(xv, wv)
            g = _sigmoid(gate_ref[:, b * D:(b + 1) * D])
            dgate_ref[:, b * D:(b + 1) * D] = (dmv * y * g * (1.0 - g)).astype(BF16)
            dy = (dmv * g).astype(BF16)
            dx_ref[...] = _dot(dy, wv, NT)
            dw_ref[...] += _dot(xv, dy, TN)

    return pl.pallas_call(
        body, name="merge_bwd", grid=(t // tm,),
        in_specs=[_rows(tm, D), _rows(tm, DM), _rows(tm, DM), _rows(tm, DM), _rows(tm, 3 * D, 0), wspec, wspec, wspec],
        out_specs=[_rows(tm, 3 * D), _rows(tm, DM), _rows(tm, DM), _rows(tm, DM), wspec, wspec, wspec],
        out_shape=[jax.ShapeDtypeStruct((t, 3 * D), BF16)] + [jax.ShapeDtypeStruct((t, DM), F32)] * 3
        + [jax.ShapeDtypeStruct((DM, D), F32)] * 3,
        compiler_params=_cparams(("arbitrary",)))(dm, ga, zb, att, proj, wa, wb, wc)


def _heads_layout(cum):
    t = cum.shape[0]
    ckrow = cum[:, :NG].T
    return ckrow, jnp.broadcast_to(ckrow[:, :, None], (NG, t, LANE))


def _layer_fwd(x, mod, w, late=None):
    sh1, sc1, gt1, sh2, sc2, gt2 = (mod[k:k + 1] for k in range(NMOD))
    h1 = _norm_mod(x, w["mix_pre_g"], sc1, sh1, name="norm_mix")
    proj = _matmul(h1, w["w_in_p"], name="mm_proj")
    ga = _gmlp_fwd(proj, w["gmlp_ln_g"], w["gmlp_ln_b"], w["wsm"], w["bsx"])
    zc, zb = _conv_fwd(proj, w["conv_w"], w["conv_b"], w["conv_ln_g"], w["conv_ln_b"])
    cum = _fox_cum(proj, w["bfp"])
    ckrow, ckcol = _heads_layout(cum)
    qkv = _qkv_prep(proj)
    att, lser = _attn_fwd(qkv, ckrow)
    if late is not None:
        w = {**w, **late(att)}
    merged = _merge_fwd(ga, zb, att, proj, w["w_a_out"], w["w_b_out"], w["w_c_out"])
    y1 = _matmul(merged, w["w_out"], name="mm_out")
    x2 = _resid(x, y1, gt1, w["mix_post_g"], name="resid_mix")
    h2 = _norm_mod(x2, w["mlp_pre_g"], sc2, sh2, name="norm_mlp")
    hid = _matmul(h2, w["mlp_w1"], name="mm_w1", b_slabs=True, out_dtype=BF16,
                  epilogue=lambda acc: (jnp.square(jnp.maximum(acc, 0.0)),))
    y2 = _matmul(hid, w["mlp_w2"], name="mm_w2", tk=TK_DEEP)
    x3 = _resid(x2, y2, gt2, w["mlp_post_g"], name="resid_mlp")
    saved = dict(x=x, h1=h1, proj=proj, ga=ga, zc=zc, zb=zb, qkv=qkv, ckrow=ckrow, ckcol=ckcol, att=att, lser=lser, merged=merged,
                 y1=y1, x2=x2, h2=h2, hid=hid, y2=y2)
    return x3, saved, w


def _layer_bwd(dx3, mod, w, s, mid=None):
    sh1, sc1, gt1, sh2, sc2, gt2 = (mod[k:k + 1] for k in range(NMOD))
    g = {}
    dy2, dgt2, g["mlp_post_g"] = _resid_bwd(dx3, s["y2"], gt2, w["mlp_post_g"], name="resid_mlp_bwd")
    da = _matmul(dy2, w["mlp_w2"], tb=True, name="mm_dhid", out_dtype=BF16, extra=(s["hid"],),
                 epilogue=lambda acc, hid: (acc * (2.0 * jnp.sqrt(hid.astype(F32))),))
    g["mlp_w2"] = _matmul(s["hid"], dy2, ta=True, name="mm_dw2", tk=TK_DEEP)
    g["mlp_w1"] = _matmul(s["h2"], da, ta=True, name="mm_dw1", out_slabs=N_CHIPS, tk=TK_DEEP)
    dh2 = _matmul(da, w["mlp_w1"], tb=True, name="mm_dh2", b_slabs=True, tk=TK_DEEP)
    dx2, g["mlp_pre_g"], dsc2, dsh2 = _norm_bwd(dh2, dx3, s["x2"], w["mlp_pre_g"], sc2, name="norm_mlp_bwd")
    if mid is not None:
        gt1 = gt1 + mid(dx2)
    dy1, dgt1, g["mix_post_g"] = _resid_bwd(dx2, s["y1"], gt1, w["mix_post_g"], name="resid_mix_bwd")
    dmerged = _matmul(dy1, w["w_out"], tb=True, name="mm_dmerged")
    g["w_out"] = _matmul(s["merged"], dy1, ta=True, name="mm_dwout", tk=TK_DEEP)
    dgate, dga, dzb, datt, g["w_a_out"], g["w_b_out"], g["w_c_out"] = _merge_bwd(
        dmerged, s["ga"], s["zb"], s["att"], s["proj"], w["w_a_out"], w["w_b_out"], w["w_c_out"])
    duv, g["gmlp_ws"], dbs, g["gmlp_ln_g"], g["gmlp_ln_b"] = _gmlp_bwd(
        dga, s["proj"], w["gmlp_ln_g"], w["gmlp_ln_b"], w["wsm"], w["wsmt"], w["bsx"])
    g["gmlp_bs"] = dbs[:, :NG].T
    dzc, g["conv_ln_g"], g["conv_ln_b"] = _conv_bwd_ln(dzb, s["zc"], w["conv_ln_g"], w["conv_ln_b"])
    dglu, dcw, g["conv_b"] = _conv_bwd(dzc, s["proj"], w["conv_w"])
    g["conv_w"] = dcw[:KW]
    t = dx3.shape[0]
    dq, dk, dv, dcq, dck = _attn_bwd(s["qkv"], s["ckcol"], datt, s["lser"], _attn_delta(s["att"], datt))
    dq = dq.reshape(t, DM)
    dcum = jnp.pad((dcq.transpose(1, 0, 2).reshape(NG, t) + dck).T, ((0, 0), (0, LANE - NG)))
    df, dbf = _fox_cum_bwd(dcum, s["proj"], w["bfp"])
    g["fox_bf"] = dbf[0, :NG]
    dproj = jnp.concatenate([dgate, duv, dglu, dq, dk, dv, df], axis=1)
    g["w_in_p"] = _matmul(s["h1"], dproj, ta=True, name="mm_dwin", tk=TK_DEEP)
    dh1 = _matmul(dproj, w["w_in_p"], tb=True, name="mm_dh1", tk=D_INP // 2)
    dx, g["mix_pre_g"], dsc1, dsh1 = _norm_bwd(dh1, dx2, s["x"], w["mix_pre_g"], sc1, name="norm_mix_bwd")
    dmod = jnp.concatenate([dsh1, dsc1, dgt1, dsh2, dsc2, dgt2], axis=0)
    return dx, g, dmod


def _position():
    return lax.axis_index("x"), lax.axis_index("y"), lax.axis_index("c")


def _all_gather8(v):
    m_per, n = v.shape

    def body(x_ref, out_ref, send_sems, recv_sems, local_sem):
        x, y, c = _position()
        me, sibling = (x, y, c), (x, y, 1 - c)
        chips = [(1 - x, y), (x, 1 - y), (1 - x, 1 - y)]

        def rows(px, py, pc):
            return out_ref.at[pl.ds((4 * px + 2 * py + pc) * m_per, m_per), :]

        def copy(k, block, to, src=None):
            return pltpu.make_async_remote_copy(
                src_ref=rows(*block) if src is None else src, dst_ref=rows(*block), send_sem=send_sems.at[k],
                recv_sem=recv_sems.at[k], device_id=to, device_id_type=MESH)

        mine = pltpu.make_async_copy(x_ref, rows(*me), local_sem)
        mine.start()
        first = [copy(0, me, sibling, src=x_ref)]
        first += [copy(1 + j, me, (*chip, c), src=x_ref) for j, chip in enumerate(chips)]
        for cp in first:
            cp.start()
        passed = [copy(4 + j, (*chip, c), sibling) for j, chip in enumerate(chips)]
        for j, chip in enumerate(chips):
            copy(1 + j, (*chip, c), me).wait_recv()
            passed[j].start()
        copy(0, sibling, me).wait_recv()
        for j, chip in enumerate(chips):
            copy(4 + j, (*chip, 1 - c), me).wait_recv()
        for cp in first + passed:
            cp.wait_send()
        mine.wait()

    out = pl.pallas_call(
        body, name="all_gather8", out_shape=jax.ShapeDtypeStruct((N_DEV * m_per, n), v.dtype),
        in_specs=[pl.BlockSpec(memory_space=pltpu.VMEM)], out_specs=pl.BlockSpec(memory_space=pltpu.VMEM),
        scratch_shapes=[pltpu.SemaphoreType.DMA((7,)), pltpu.SemaphoreType.DMA((7,)), pltpu.SemaphoreType.DMA],
        compiler_params=pltpu.CompilerParams(vmem_limit_bytes=VMEM_LIMIT),
    )(v)
    return out.reshape(N_DEV, m_per, n)


def _half(c, rows):
    return pl.ds(c * (rows // 2), rows // 2)


HBM = pl.BlockSpec(memory_space=pltpu.HBM)
SEM = pl.BlockSpec(memory_space=pltpu.SEMAPHORE)
EFFECT = pltpu.SideEffectType.DATAFLOW_SIDE_EFFECTING


_COPIES_PER_ARRAY = {"gather": 3, "scatter": 3, "swap": N_CHIPS}


def _ici_copies(kind, src_refs, land_refs, send_sems, recv_sems, arriving):
    x, y, c = _position()
    chips = [(1 - x, y), (x, 1 - y), (1 - x, 1 - y)]
    nper = _COPIES_PER_ARRAY[kind]
    copies = []
    for wi, (src, land) in enumerate(zip(src_refs, land_refs)):
        for k in range(nper):
            if kind == "swap":
                peer = (x, y, 1 - c)
                s_win, dst = src.at[k, _half(1 - c, src.shape[1])], land.at[k]
            else:
                px, py = chips[k]
                peer = (px, py, c)
                if kind == "gather":
                    rows = src.shape[0]
                    s_win = src.at[_half(c, rows)]
                    dst = land.at[2 * px + py if arriving else 2 * x + y, _half(c, rows)]
                else:
                    s_win, dst = src.at[2 * px + py], land.at[k]
            copies.append(pltpu.make_async_remote_copy(
                src_ref=s_win, dst_ref=dst, send_sem=send_sems.at[wi * nper + k], recv_sem=recv_sems.at[wi * nper + k],
                device_id=peer, device_id_type=MESH))
    return copies


def _ici_start(kind, srcs, land_shapes, after, *, name):
    nw = len(srcs)

    def body(*refs):
        src_refs, land_refs = refs[:nw], refs[nw:2 * nw]
        send_sems, recv_sems = refs[2 * nw + 1:2 * nw + 3]
        token = refs[-1]
        for cp in _ici_copies(kind, src_refs, land_refs, send_sems, recv_sems, False):
            cp.start()
        token[...] = jnp.zeros_like(token)

    lands = [pltpu.with_memory_space_constraint(lax.empty(shp, s.dtype), pltpu.HBM) for shp, s in zip(land_shapes, srcs)]
    outs = pl.pallas_call(
        body, name=name,
        out_shape=(pltpu.SemaphoreType.DMA((_COPIES_PER_ARRAY[kind] * nw,)), pltpu.SemaphoreType.DMA((_COPIES_PER_ARRAY[kind] * nw,)),
                   *[pltpu.HBM(s.shape, s.dtype) for s in srcs], *[pltpu.HBM(shp, s.dtype) for shp, s in zip(land_shapes, srcs)],
                   jax.ShapeDtypeStruct((8, LANE), F32)),
        in_specs=[HBM] * (2 * nw) + [ANY], out_specs=(SEM, SEM, *[HBM] * (2 * nw), pl.BlockSpec(memory_space=pltpu.VMEM)),
        input_output_aliases={i: 2 + i for i in range(2 * nw)},
        compiler_params=pltpu.CompilerParams(has_side_effects=EFFECT),
    )(*[pltpu.with_memory_space_constraint(s, pltpu.HBM) for s in srcs], *lands, after)
    return outs[0], outs[1], outs[2:2 + nw], outs[2 + nw:2 + 2 * nw], outs[-1]


def _ici_wait(kind, send_sems, recv_sems, srcs, lands, after, *, name):
    nw = len(srcs)

    def body(*refs):
        src_refs, land_refs = refs[:nw], refs[nw:2 * nw]
        s_sems, r_sems = refs[2 * nw:2 * nw + 2]
        for cp in _ici_copies(kind, src_refs, land_refs, s_sems, r_sems, False):
            cp.wait_send()
        for cp in _ici_copies(kind, src_refs, land_refs, s_sems, r_sems, True):
            cp.wait_recv()

    outs = pl.pallas_call(
        body, name=name,
        out_shape=(*[pltpu.HBM(s.shape, s.dtype) for s in srcs], *[pltpu.HBM(a.shape, a.dtype) for a in lands]),
        in_specs=[HBM] * (2 * nw) + [SEM, SEM, ANY], out_specs=tuple([HBM] * (2 * nw)),
        input_output_aliases={i: i for i in range(2 * nw)},
        compiler_params=pltpu.CompilerParams(has_side_effects=EFFECT),
    )(*srcs, *lands, send_sems, recv_sems, after)
    return outs[:nw], outs[nw:]


def _ag_d2d(lands, *, name):
    nw = len(lands)

    def body(*refs):
        out_refs = refs[nw:2 * nw]
        send_sems, recv_sems = refs[2 * nw:]
        x, y, c = _position()
        chips = [(1 - x, y), (x, 1 - y), (1 - x, 1 - y)]
        copies = []
        for wi in range(nw):
            rows = out_refs[wi].shape[1]
            for k, (px, py) in enumerate(chips):
                win = out_refs[wi].at[2 * px + py, _half(c, rows)]
                copies.append(pltpu.make_async_remote_copy(
                    src_ref=win, dst_ref=win, send_sem=send_sems.at[wi * 3 + k], recv_sem=recv_sems.at[wi * 3 + k],
                    device_id=(x, y, 1 - c), device_id_type=MESH))
        for cp in copies:
            cp.start()
        for wi in range(nw):
            rows = out_refs[wi].shape[1]
            for k, (px, py) in enumerate(chips):
                win = out_refs[wi].at[2 * px + py, _half(1 - c, rows)]
                pltpu.make_async_remote_copy(
                    src_ref=win, dst_ref=win, send_sem=send_sems.at[wi * 3 + k], recv_sem=recv_sems.at[wi * 3 + k],
                    device_id=(x, y, 1 - c), device_id_type=MESH).wait_recv()
        for cp in copies:
            cp.wait_send()

    return pl.pallas_call(
        body, name=name, out_shape=[jax.ShapeDtypeStruct(a.shape, a.dtype) for a in lands],
        in_specs=[ANY] * nw, out_specs=[ANY] * nw, input_output_aliases={i: i for i in range(nw)},
        scratch_shapes=[pltpu.SemaphoreType.DMA((3 * nw,)), pltpu.SemaphoreType.DMA((3 * nw,))],
    )(*lands)


def _swap_reduced(reds):
    nw = len(reds)

    def body(*refs):
        r_refs, out_refs = refs[:nw], refs[nw:2 * nw]
        send_sems, recv_sems = refs[2 * nw:]
        x, y, c = _position()
        copies = [pltpu.make_async_remote_copy(
            src_ref=r_refs[wi], dst_ref=out_refs[wi], send_sem=send_sems.at[wi], recv_sem=recv_sems.at[wi],
            device_id=(x, y, 1 - c), device_id_type=MESH) for wi in range(nw)]
        for cp in copies:
            cp.start()
        for cp in copies:
            cp.wait()

    return pl.pallas_call(
        body, name="rs_swap_reduced", out_shape=[jax.ShapeDtypeStruct(r.shape, r.dtype) for r in reds],
        in_specs=[ANY] * nw, out_specs=[ANY] * nw,
        scratch_shapes=[pltpu.SemaphoreType.DMA((nw,)), pltpu.SemaphoreType.DMA((nw,))],
    )(*reds)


def _row_tile(rows, cols):
    tr = rows
    while tr * cols * 4 > (2 << 20) and tr % 32 == 0:
        tr //= 2
    return tr


def _add_own_half(g, recv, c, *, name):
    nj, h, w = recv.shape
    tr = _row_tile(h, w)
    nb = h // tr

    def body(c_ref, g_ref, r_ref, o_ref, ob_ref):
        sm = g_ref[0] + r_ref[0]
        o_ref[0] = sm
        ob_ref[0] = sm.astype(BF16)

    spec = pl.BlockSpec((1, tr, w), lambda j, i, cc: (j, i, 0))
    gs = pltpu.PrefetchScalarGridSpec(
        num_scalar_prefetch=1, grid=(nj, nb),
        in_specs=[pl.BlockSpec((1, tr, w), lambda j, i, cc: (j, cc[0] * nb + i, 0)), spec],
        out_specs=[spec, spec])
    return pl.pallas_call(body, name=name, grid_spec=gs,
                          out_shape=[jax.ShapeDtypeStruct((nj, h, w), F32), jax.ShapeDtypeStruct((nj, h, w), BF16)],
                          compiler_params=_cparams(("parallel", "parallel")))(jnp.reshape(c, (1,)).astype(I32), g, recv)


def _add_own_chip(sp, recv, j, *, name):
    _, r, w = sp.shape
    tr = _row_tile(r, w)

    def body(j_ref, s_ref, r_ref, o_ref):
        o_ref[...] = ((s_ref[0] + r_ref[0].astype(F32)) + r_ref[1].astype(F32)) + r_ref[2].astype(F32)

    gs = pltpu.PrefetchScalarGridSpec(
        num_scalar_prefetch=1, grid=(r // tr,),
        in_specs=[pl.BlockSpec((1, tr, w), lambda i, jj: (jj[0], i, 0)), pl.BlockSpec((3, tr, w), lambda i, jj: (0, i, 0))],
        out_specs=pl.BlockSpec((tr, w), lambda i, jj: (i, 0)))
    return pl.pallas_call(body, name=name, grid_spec=gs, out_shape=jax.ShapeDtypeStruct((r, w), F32),
                          compiler_params=_cparams(("parallel",)))(jnp.reshape(j, (1,)).astype(I32), sp, recv)


def _sum8(v):
    _, m, n = v.shape

    def body(v_ref, o_ref):
        acc = v_ref[0]
        for k in range(1, N_DEV):
            acc = acc + v_ref[k]
        o_ref[...] = acc

    return pl.pallas_call(body, name="sum8", grid=(1,), in_specs=[pl.BlockSpec((N_DEV, m, n), lambda i: (0, 0, 0))],
                          out_specs=pl.BlockSpec((m, n), lambda i: (0, 0)), out_shape=jax.ShapeDtypeStruct((m, n), F32),
                          compiler_params=_cparams(("arbitrary",)))(v)


def _ada_mod(c_all, ada_w, ada_b_loc, *, tn=512):
    nl, _, ncol = ada_w.shape

    def body(c_ref, w_ref, b_ref, o_ref):
        cv = c_ref[...]
        ca = (cv * _sigmoid(cv)).astype(BF16)
        o_ref[0] = _dot(ca, w_ref[0].astype(BF16)) + b_ref[0]

    return pl.pallas_call(
        body, name="ada_mod", grid=(nl, ncol // tn),
        in_specs=[pl.BlockSpec((N_DEV, D), lambda l, j: (0, 0)), pl.BlockSpec((1, D, tn), lambda l, j: (l, 0, j)),
                  pl.BlockSpec((1, 1, tn), lambda l, j: (l, 0, j))],
        out_specs=pl.BlockSpec((1, N_DEV, tn), lambda l, j: (l, 0, j)),
        out_shape=jax.ShapeDtypeStruct((nl, N_DEV, ncol), F32),
        compiler_params=_cparams(("parallel", "parallel")))(c_all, ada_w, ada_b_loc)


def _ada_grad(c_pad, dmod_pad, *, tn=512):
    nl, nb, ncol = dmod_pad.shape

    def body(c_ref, d_ref, o_ref):
        cv = c_ref[...]
        ca = (cv * _sigmoid(cv)).astype(BF16)
        o_ref[0] = _dot(ca, d_ref[0].astype(BF16), TN)

    return pl.pallas_call(
        body, name="ada_grad", grid=(nl, ncol // tn),
        in_specs=[pl.BlockSpec((nb, D), lambda l, j: (0, 0)), pl.BlockSpec((1, nb, tn), lambda l, j: (l, 0, j))],
        out_specs=pl.BlockSpec((1, D, tn), lambda l, j: (l, 0, j)),
        out_shape=jax.ShapeDtypeStruct((nl, D, ncol), F32),
        compiler_params=_cparams(("parallel", "parallel")))(c_pad, dmod_pad)


def _adamw(w, g, m, v, *, name):
    shape = w.shape
    if w.ndim == 3:
        lead, rows, cols = shape
    else:
        lead, (rows, cols) = 1, shape
    w3, g3, m3, v3 = (a.reshape(lead, rows, cols) for a in (w, g, m, v))
    tr = rows
    if rows * cols * 4 > (2 << 20):
        tr = next(cand for cand in (256, 128, 64, 8) if rows % cand == 0)
    c1 = 1.0 - ADAM_B1 ** ADAM_STEP
    c2 = 1.0 - ADAM_B2 ** ADAM_STEP

    def body(w_ref, g_ref, m_ref, v_ref, d_ref, nm_ref, nv_ref):
        gv = g_ref[...]
        nm = ADAM_B1 * m_ref[...] + (1.0 - ADAM_B1) * gv
        nv = ADAM_B2 * v_ref[...] + (1.0 - ADAM_B2) * (gv * gv)
        nm_ref[...] = nm
        nv_ref[...] = nv
        d_ref[...] = -ADAM_LR * ((nm / c1) / (jnp.sqrt(nv / c2) + ADAM_EPS) + ADAM_WD * w_ref[...])

    spec = pl.BlockSpec((1, tr, cols), lambda l, i: (l, i, 0))
    outs = pl.pallas_call(
        body, name=name, grid=(lead, rows // tr), in_specs=[spec] * 4, out_specs=[spec] * 3,
        out_shape=[jax.ShapeDtypeStruct((lead, rows, cols), F32)] * 3,
        compiler_params=_cparams(("parallel", "parallel")))(w3, g3, m3, v3)
    return tuple(o.reshape(shape) for o in outs)


def _adamw_layer(w, g_own, g_sib, m, v, c, layer, prev, after, *, name):
    _, rows, cols = w.shape
    tr = _row_tile(rows // 2, cols)
    nbh = rows // 2 // tr
    c1 = 1.0 - ADAM_B1 ** ADAM_STEP
    c2 = 1.0 - ADAM_B2 ** ADAM_STEP
    n_prev = 0 if prev is None else 4

    def body(c_ref, w_ref, o_ref, s_ref, m_ref, v_ref, *rest):
        g_ref, d_ref, nm_ref, nv_ref = rest[n_prev + 1:]
        gv = jnp.where(pl.program_id(0) // nbh == c_ref[0], o_ref[...], s_ref[...])
        nm = ADAM_B1 * m_ref[0] + (1.0 - ADAM_B1) * gv
        nv = ADAM_B2 * v_ref[0] + (1.0 - ADAM_B2) * (gv * gv)
        g_ref[0] = gv
        nm_ref[0] = nm
        nv_ref[0] = nv
        d_ref[0] = -ADAM_LR * ((nm / c1) / (jnp.sqrt(nv / c2) + ADAM_EPS) + ADAM_WD * w_ref[0])

    def source(own):
        return pl.BlockSpec((tr, cols), lambda i, cc: (jnp.where((i // nbh == cc[0]) == own, i % nbh, 0), 0))

    spec = pl.BlockSpec((1, tr, cols), lambda i, cc: (layer, i, 0))
    gs = pltpu.PrefetchScalarGridSpec(
        num_scalar_prefetch=1, grid=(2 * nbh,),
        in_specs=[spec, source(True), source(False), spec, spec] + [ANY] * (n_prev + 1), out_specs=[spec] * 4)
    return pl.pallas_call(
        body, name=name, grid_spec=gs, out_shape=[jax.ShapeDtypeStruct(w.shape, F32)] * 4,
        input_output_aliases={6 + k: k for k in range(n_prev)},
        compiler_params=_cparams(("parallel",)))(
            jnp.reshape(c, (1,)).astype(I32), w, g_own, g_sib, m, v, *(prev or ()), after)


SMALL = (("mix_pre_g", (2, D)), ("mix_post_g", (2, D)), ("mlp_pre_g", (2, D)), ("mlp_post_g", (2, D)),
         ("gmlp_ln_g", (2, DM)), ("gmlp_ln_b", (2, DM)), ("gmlp_ws", (2, NG, CH, CH)), ("gmlp_bs", (2, NG, CH)),
         ("conv_b", (2, DM)), ("conv_ln_g", (2, DM)), ("conv_ln_b", (2, DM)), ("fox_bf", (2, NG)))


def _pack_rows(arrays):
    parts = []
    for a in arrays:
        last = a.shape[-1]
        r = a.astype(F32).reshape(-1, LANE) if last % LANE == 0 else jnp.pad(a.astype(F32).reshape(-1, last), ((0, 0), (0, LANE - last)))
        pad = (-r.shape[0]) % 8
        parts.append(jnp.pad(r, ((0, pad), (0, 0))) if pad else r)
    return jnp.concatenate(parts, axis=0)


def _unpack_rows(buf, shapes):
    out, off = [], 0
    for shp in shapes:
        size = 1
        for d in shp:
            size *= d
        last = shp[-1]
        rows = size // (LANE if last % LANE == 0 else last)
        seg = buf[off:off + rows]
        out.append(seg.reshape(shp) if last % LANE == 0 else seg[:, :last].reshape(shp))
        off += rows + (-rows) % 8
    return out


def _assemble_w_in(w_in_full):
    uv_glu_qkv = w_in_full[:, :3584]
    f = w_in_full[:, 3584:3592]
    gate = w_in_full[:, 3592:]
    fpad = jnp.zeros((D, D_INP - C_F - NG), w_in_full.dtype)
    return jnp.concatenate([gate, uv_glu_qkv, f, fpad], axis=1)


def _disassemble_w_in(g_p):
    return jnp.concatenate([g_p[:, C_UV:C_F], g_p[:, C_F:C_F + NG], g_p[:, :C_UV]], axis=1)


def kernel(x, c, ada_w, ada_b, mix_pre_g, mix_post_g, mlp_pre_g, mlp_post_g, w_in, gmlp_ln_g, gmlp_ln_b, gmlp_ws, gmlp_bs, w_a_out, conv_w, conv_b, conv_ln_g, conv_ln_b, w_b_out, fox_bf, w_c_out, w_out, mlp_w1, mlp_w2, loss_target, m_ada_w, m_ada_b, m_mix_pre_g, m_mix_post_g, m_mlp_pre_g, m_mlp_post_g, m_w_in, m_gmlp_ln_g, m_gmlp_ln_b, m_gmlp_ws, m_gmlp_bs, m_w_a_out, m_conv_w, m_conv_b, m_conv_ln_g, m_conv_ln_b, m_w_b_out, m_fox_bf, m_w_c_out, m_w_out, m_mlp_w1, m_mlp_w2, v_ada_w, v_ada_b, v_mix_pre_g, v_mix_post_g, v_mlp_pre_g, v_mlp_post_g, v_w_in, v_gmlp_ln_g, v_gmlp_ln_b, v_gmlp_ws, v_gmlp_bs, v_w_a_out, v_conv_w, v_conv_b, v_conv_ln_g, v_conv_ln_b, v_w_b_out, v_fox_bf, v_w_c_out, v_w_out, v_mlp_w1, v_mlp_w2):
    weights = dict(ada_w=ada_w, ada_b=ada_b, mix_pre_g=mix_pre_g, mix_post_g=mix_post_g, mlp_pre_g=mlp_pre_g,
                   mlp_post_g=mlp_post_g, w_in=w_in, gmlp_ln_g=gmlp_ln_g, gmlp_ln_b=gmlp_ln_b, gmlp_ws=gmlp_ws,
                   gmlp_bs=gmlp_bs, w_a_out=w_a_out, conv_w=conv_w, conv_b=conv_b, conv_ln_g=conv_ln_g,
                   conv_ln_b=conv_ln_b, w_b_out=w_b_out, fox_bf=fox_bf, w_c_out=w_c_out, w_out=w_out, mlp_w1=mlp_w1,
                   mlp_w2=mlp_w2)
    mom_m = dict(ada_w=m_ada_w, ada_b=m_ada_b, mix_pre_g=m_mix_pre_g, mix_post_g=m_mix_post_g, mlp_pre_g=m_mlp_pre_g,
                 mlp_post_g=m_mlp_post_g, w_in=m_w_in, gmlp_ln_g=m_gmlp_ln_g, gmlp_ln_b=m_gmlp_ln_b, gmlp_ws=m_gmlp_ws,
                 gmlp_bs=m_gmlp_bs, w_a_out=m_w_a_out, conv_w=m_conv_w, conv_b=m_conv_b, conv_ln_g=m_conv_ln_g,
                 conv_ln_b=m_conv_ln_b, w_b_out=m_w_b_out, fox_bf=m_fox_bf, w_c_out=m_w_c_out, w_out=m_w_out,
                 mlp_w1=m_mlp_w1, mlp_w2=m_mlp_w2)
    mom_v = dict(ada_w=v_ada_w, ada_b=v_ada_b, mix_pre_g=v_mix_pre_g, mix_post_g=v_mix_post_g, mlp_pre_g=v_mlp_pre_g,
                 mlp_post_g=v_mlp_post_g, w_in=v_w_in, gmlp_ln_g=v_gmlp_ln_g, gmlp_ln_b=v_gmlp_ln_b, gmlp_ws=v_gmlp_ws,
                 gmlp_bs=v_gmlp_bs, w_a_out=v_w_a_out, conv_w=v_conv_w, conv_b=v_conv_b, conv_ln_g=v_conv_ln_g,
                 conv_ln_b=v_conv_ln_b, w_b_out=v_w_b_out, fox_bf=v_fox_bf, w_c_out=v_w_c_out, w_out=v_w_out,
                 mlp_w1=v_mlp_w1, mlp_w2=v_mlp_w2)
    order = list(weights)
    px, py, pc = _position()
    chip = 2 * px + py
    dev = 2 * chip + pc
    depth = ada_w.shape[0]
    t = x.shape[1]
    xl = x.reshape(t, D)
    tgt = loss_target.reshape(t, D)

    big_names = [b[0] for b in BIG]
    shards = [[weights[n][l].astype(BF16) for n in big_names] for l in range(depth)]
    land_shapes = [(N_CHIPS,) + sh.shape for sh in shards[0]]
    ag0a = _ici_start("gather", shards[0][:1], land_shapes[:1], c, name="ag0a_start")

    small_in = _pack_rows([c, conv_w]) + ag0a[4][0, 0]
    gathered = _all_gather8(small_in)
    c_all = gathered[:, :D // LANE, :].reshape(N_DEV, D)
    cw_rows = depth * KW * LANE // LANE
    conv_w_full = jnp.concatenate(
        [gathered[2 * j, D // LANE:D // LANE + cw_rows, :].reshape(depth, KW, LANE) for j in range(N_CHIPS)], axis=2)

    ncol = ada_w.shape[2]
    ada_b_loc = lax.dynamic_slice_in_dim(ada_b, chip * ncol, ncol, axis=1).reshape(depth, 1, ncol)
    mod_sh = _ada_mod(c_all, ada_w, ada_b_loc)
    mod_g = _all_gather8(mod_sh.reshape(-1, LANE)).reshape(N_DEV, depth, N_DEV, ncol)
    mod_all = jnp.concatenate([mod_g[2 * j] for j in range(N_CHIPS)], axis=2)
    mod_mine = lax.dynamic_index_in_dim(mod_all, dev, axis=1, keepdims=False)

    mods = [mod_mine[l].reshape(NMOD, D) for l in range(depth)]

    def gathered_weights(idx, waited):
        own, lands = waited
        lands = _ag_d2d(lands, name="ag_d2d")
        w = {}
        for i, sh, g in zip(idx, own, lands):
            n, r, cdim, ax = BIG[i]
            g = lax.dynamic_update_slice(g, sh[None], (chip, 0, 0))
            if n == "w_in":
                w["w_in_p"] = _assemble_w_in(g.transpose(1, 0, 2).reshape(r, N_CHIPS * cdim))
            elif n == "mlp_w1":
                w[n] = g
            elif ax == 1:
                w[n] = g.transpose(1, 0, 2).reshape(r, N_CHIPS * cdim)
            else:
                w[n] = g.reshape(N_CHIPS * r, cdim)
        return w

    def local_weights(l):
        w = {}
        for n in ("mix_pre_g", "mix_post_g", "mlp_pre_g", "mlp_post_g", "gmlp_ln_g", "gmlp_ln_b", "conv_b", "conv_ln_g",
                  "conv_ln_b"):
            w[n] = weights[n][l:l + 1]
        tril = jnp.tril(jnp.ones((CH, CH), F32))
        wsm = gmlp_ws[l] * tril
        w["wsm"] = wsm.astype(BF16)
        w["wsmt"] = jnp.swapaxes(wsm, 1, 2).astype(BF16)
        w["bsx"] = jnp.repeat(gmlp_bs[l].T, HD, axis=1)
        w["conv_w"] = conv_w_full[l]
        w["bfp"] = jnp.pad(fox_bf[l], (0, LANE - NG)).reshape(1, LANE)
        return w

    def slabs(gfull, n, r, cdim, ax):
        if n == "mlp_w1":
            return gfull
        if ax == 1:
            return gfull.reshape(gfull.shape[0], N_CHIPS, cdim).transpose(1, 0, 2)
        return gfull.reshape(N_CHIPS, r, cdim)

    def swap_start(g, after, name):
        g["w_in"] = _disassemble_w_in(g.pop("w_in_p"))
        gs = [slabs(g[n], n, r, cdim, ax) for n, r, cdim, ax in BIG]
        return _ici_start("swap", gs, [(N_CHIPS, a.shape[1] // 2, a.shape[2]) for a in gs], after, name=name)

    def chip_sums(sw, after, name):
        gs, from_sibling = _ici_wait("swap", sw[0], sw[1], sw[2], sw[3], after, name=name)
        return [_add_own_half(a, rv, pc, name="rs_add_half_" + n) for a, rv, n in zip(gs, from_sibling, big_names)]

    def reduce_rest(sums, from_chips):
        red = [_add_own_chip(sf, rv, chip, name="rs_add_chip_" + n) for (sf, _), rv, n in zip(sums, from_chips, big_names)]
        return red, _swap_reduced(red)

    assert depth == 2
    ga, gb = [0], list(range(1, len(BIG)))

    def pick(seq, idx):
        return [seq[i] for i in idx]

    lands0a = _ici_wait("gather", ag0a[0], ag0a[1], ag0a[2], ag0a[3], mod_mine, name="ag0a_wait")
    ag0b = _ici_start("gather", pick(shards[0], gb), pick(land_shapes, gb), lands0a[1][0], name="ag0b_start")
    ag1 = _ici_start("gather", shards[1], land_shapes, ag0b[4], name="ag1_start")
    mod0 = mods[0] + ag1[4][0, 0]
    layers, saved = [None] * depth, [None] * depth

    def late0(att):
        return gathered_weights(gb, _ici_wait("gather", ag0b[0], ag0b[1], ag0b[2], ag0b[3], att, name="ag0b_wait"))

    xs, saved[0], layers[0] = _layer_fwd(xl, mod0, {**local_weights(0), **gathered_weights(ga, lands0a)}, late0)
    lands1 = _ici_wait("gather", ag1[0], ag1[1], ag1[2], ag1[3], xs, name="ag1_wait")
    xs, saved[1], layers[1] = _layer_fwd(xs, mods[1], {**local_weights(1), **gathered_weights(ga + gb, lands1)})
    loss_local, dx = _loss_and_grad(xs, tgt)
    loss = lax.psum(loss_local, ("x", "y", "c"))
    grads, dmods = [None] * depth, [None] * depth
    dx, grads[1], dmods[1] = _layer_bwd(dx, mods[1], layers[1], saved[1])
    sw1 = swap_start(grads[1], dx, "sw1_start")
    rs_l1 = {}

    def mid0(dx2):
        rs_l1["sums"] = chip_sums(sw1, dx2, "sw1_wait")
        sbf1 = [sb for _, sb in rs_l1["sums"]]
        rs_l1["rs"] = _ici_start("scatter", sbf1, [(3,) + sb.shape[1:] for sb in sbf1], dx2, name="rs1_start")
        return rs_l1["rs"][4][0:1, 0:1]

    dx, grads[0], dmods[0] = _layer_bwd(dx, mod0 + sw1[4][0, 0], layers[0], saved[0], mid0)
    grad_x = dx.reshape(x.shape)
    sums1, rs1 = rs_l1["sums"], rs_l1["rs"]
    slot_shapes = [(3,) + sb.shape[1:] for _, sb in sums1]
    sw0 = swap_start(grads[0], dx, "sw0_start")
    g_out = {}

    small_names = [n for n, _ in SMALL]
    small_list = [jnp.stack(dmods)] + [jnp.stack([grads[l][n] for l in range(depth)]) for n in small_names]
    small_list.append(jnp.stack([grads[l]["conv_w"] for l in range(depth)]))
    small_shapes = [(depth, NMOD * D)] + [shp for _, shp in SMALL] + [(depth, KW, DM)]
    small_all = _all_gather8(_pack_rows(small_list) + sw0[4][0, 0])
    sums0 = chip_sums(sw0, small_all, "sw0_wait")
    rs0 = _ici_start("scatter", [sb for _, sb in sums0], slot_shapes, small_all, name="rs0_start")
    tok0 = rs0[4][0, 0]
    red1 = reduce_rest(sums1, _ici_wait("scatter", rs1[0], rs1[1], rs1[2], rs1[3], rs0[4], name="rs1_wait")[1])
    c_all = c_all + tok0
    small_sum = _unpack_rows(_sum8(small_all) + tok0, small_shapes)
    g_out["ada_b"] = small_sum[0]
    for n, gs in zip(small_names, small_sum[1:-1]):
        g_out[n] = gs
    g_out["conv_w"] = lax.dynamic_slice_in_dim(small_sum[-1], chip * LANE, LANE, axis=2)
    dmod_all = small_all[:, :depth * NMOD * D // LANE, :].reshape(N_DEV, depth, NMOD * D)
    dmod_loc = lax.dynamic_slice_in_dim(dmod_all, chip * ncol, ncol, axis=2).transpose(1, 0, 2)
    g_out["ada_w"] = _ada_grad(jnp.pad(c_all, ((0, 8), (0, 0))), jnp.pad(dmod_loc, ((0, 0), (0, 8), (0, 0))))

    delta, new_m, new_v = {}, {}, {}
    delta["ada_w"], new_m["ada_w"], new_v["ada_w"] = _adamw(ada_w, g_out["ada_w"], m_ada_w, v_ada_w, name="adamw_ada_w")
    ws_rows = (depth * NG * CH, CH)
    ws_out = _adamw(*[d["gmlp_ws"].reshape(ws_rows) for d in (weights, g_out, mom_m, mom_v)], name="adamw_gmlp_ws")
    delta["gmlp_ws"], new_m["gmlp_ws"], new_v["gmlp_ws"] = (a.reshape(gmlp_ws.shape) for a in ws_out)
    small_params = ["ada_b"] + [n for n in small_names if n != "gmlp_ws"] + ["conv_w"]
    packs = [_pack_rows([d[n] for n in small_params]) for d in (weights, g_out, mom_m, mom_v)]
    outs = _adamw(*packs, name="adamw_small")
    shapes = [weights[n].shape for n in small_params]
    for dst, buf in zip((delta, new_m, new_v), outs):
        for n, a in zip(small_params, _unpack_rows(buf, shapes)):
            dst[n] = a
    half = {n: _adamw_layer(weights[n], red1[0][wi], red1[1][wi], mom_m[n], mom_v[n], pc, 1, None, rs0[4],
                            name="adamw1_" + n) for wi, n in enumerate(big_names)}
    done = jnp.stack([delta["ada_w"][0, 0, 0], outs[0][0, 0]] + [half[n][1][1, 0, 0] for n in big_names])
    red0 = reduce_rest(sums0, _ici_wait("scatter", rs0[0], rs0[1], rs0[2], rs0[3], done, name="rs0_wait")[1])
    for wi, n in enumerate(big_names):
        g_out[n], delta[n], new_m[n], new_v[n] = _adamw_layer(
            weights[n], red0[0][wi], red0[1][wi], mom_m[n], mom_v[n], pc, 0, half[n], rs0[4], name="adamw0_" + n)

    return (loss, grad_x, *[g_out[n] for n in order], *[delta[n] for n in order], *[new_m[n] for n in order],
            *[new_v[n] for n in order])
```

```python
import jax
import jax.numpy as jnp
from jax import lax
from jax.experimental import pallas as pl
from jax.experimental.pallas import tpu as pltpu

F32 = jnp.float32
BF16 = jnp.bfloat16
I32 = jnp.int32
MESH = pl.DeviceIdType.MESH
ANY = pl.BlockSpec(memory_space=pl.ANY)

D = 1024
DM = 512
NG = 8
GD = DM // NG
CH = 128
KW = 31
HALO = 32
NMOD = 6
EPS = 1e-6
LANE = 128
N_CHIPS = 4
N_DEV = 8
C_GATE, C_UV, C_GLU, C_Q, C_K, C_V, C_F, D_INP = 0, 3072, 4096, 5120, 5632, 6144, 6656, 7168
VMEM_LIMIT = 56 * 1024 * 1024
ROW_TILE = 1024
TK_DEEP = 4096

ADAM_LR, ADAM_B1, ADAM_B2, ADAM_EPS, ADAM_WD, ADAM_STEP = 0.001, 0.9, 0.999, 1e-08, 0.01, 10

BIG = (("w_in", 1024, 1666, 1), ("w_a_out", 512, 256, 1), ("w_b_out", 512, 256, 1), ("w_c_out", 512, 256, 1),
       ("w_out", 256, 1024, 0), ("mlp_w1", 1024, 1024, 1), ("mlp_w2", 1024, 1024, 0))


def _cparams(sem):
    return pltpu.CompilerParams(dimension_semantics=sem, vmem_limit_bytes=VMEM_LIMIT)


def _sigmoid(x):
    return jax.nn.sigmoid(x)


_GELU_K = 0.7978845608028654
_GELU_A = 0.044715


def _gelu(x):
    t = jnp.tanh(_GELU_K * (x + _GELU_A * x * x * x))
    return 0.5 * x * (1.0 + t)


def _gelu_grad(x):
    t = jnp.tanh(_GELU_K * (x + _GELU_A * x * x * x))
    return 0.5 * (1.0 + t) + 0.5 * x * (1.0 - t * t) * _GELU_K * (1.0 + 3.0 * _GELU_A * x * x)


def _mean(x):
    return jnp.mean(x, axis=-1, keepdims=True)


def _colsum(x):
    return jnp.sum(x, axis=0, keepdims=True)


def _dot(a, b, dims=((1,), (0,))):
    return lax.dot_general(a, b, (dims, ((), ())), preferred_element_type=F32)


NN = ((1,), (0,))
NT = ((1,), (1,))
TN = ((0,), (0,))


def _matmul(a, b, *, name, ta=False, tb=False, out_dtype=F32, tm=1024, tn=1024, tk=1024, epilogue=None, extra=(),
            extra_out=(), b_slabs=False, out_slabs=0):
    m, k = (a.shape[1], a.shape[0]) if ta else a.shape
    if b_slabs:
        ns, brows, bw = b.shape
        n = brows if tb else ns * bw
        assert (ns * bw if tb else brows) == k, (name, b.shape, k)
        all_slabs = tb and tk >= k
        tn, tk = (tn, k if all_slabs else bw) if tb else (bw, tk)
    else:
        n = b.shape[0] if tb else b.shape[1]
        all_slabs = False
    tm, tn, tk = min(tm, m), min(tn, n), min(tk, k)
    assert m % tm == 0 and n % tn == 0 and k % tk == 0, (name, m, n, k, tm, tn, tk)
    assert not out_slabs or (n // out_slabs == tn and epilogue is None), name
    nk = k // tk
    dims = ((0 if ta else 1,), (1 if tb else 0,))
    n_extra = len(extra)
    out_dtypes = (out_dtype,) + tuple(extra_out)

    def body(a_ref, b_ref, *rest):
        extra_refs = rest[:n_extra]
        out_refs = rest[n_extra:n_extra + len(out_dtypes)]
        kk = pl.program_id(2)
        if all_slabs:
            part = sum(_dot(a_ref[:, s * bw:(s + 1) * bw].astype(BF16), b_ref[s].astype(BF16), dims) for s in range(ns))
        else:
            part = _dot(a_ref[...].astype(BF16), b_ref[...].astype(BF16), dims)

        def finish(acc):
            outs = (acc,) if epilogue is None else epilogue(acc, *[r[...] for r in extra_refs])
            for o_ref, o in zip(out_refs, outs):
                o_ref[...] = o.astype(o_ref.dtype)

        if nk == 1:
            finish(part)
        else:
            acc_ref = rest[-1]

            @pl.when(kk == 0)
            def _():
                acc_ref[...] = part

            @pl.when(jnp.logical_and(kk > 0, kk < nk - 1))
            def _():
                acc_ref[...] += part

            @pl.when(kk == nk - 1)
            def _():
                finish(acc_ref[...] + part)

    a_spec = pl.BlockSpec((tk, tm), lambda i, j, kk: (kk, i)) if ta else pl.BlockSpec((tm, tk), lambda i, j, kk: (i, kk))
    if all_slabs:
        b_spec = pl.BlockSpec((ns, tn, bw), lambda i, j, kk: (0, j, 0))
    elif b_slabs and tb:
        b_spec = pl.BlockSpec((None, tn, tk), lambda i, j, kk: (kk, j, 0))
    elif b_slabs:
        b_spec = pl.BlockSpec((None, tk, tn), lambda i, j, kk: (j, kk, 0))
    else:
        b_spec = pl.BlockSpec((tn, tk), lambda i, j, kk: (j, kk)) if tb else pl.BlockSpec((tk, tn), lambda i, j, kk: (kk, j))
    if out_slabs:
        o_spec = pl.BlockSpec((None, tm, tn), lambda i, j, kk: (j, i, 0))
        o_shape = (out_slabs, m, tn)
    else:
        o_spec = pl.BlockSpec((tm, tn), lambda i, j, kk: (i, j))
        o_shape = (m, n)
    outs = pl.pallas_call(
        body, name=name, grid=(m // tm, n // tn, nk),
        in_specs=[a_spec, b_spec] + [o_spec] * n_extra,
        out_specs=[o_spec] * len(out_dtypes),
        out_shape=[jax.ShapeDtypeStruct(o_shape, dt) for dt in out_dtypes],
        scratch_shapes=[pltpu.VMEM((tm, tn), F32)] if nk > 1 else [],
        compiler_params=_cparams(("parallel", "parallel", "arbitrary")),
    )(a, b, *extra)
    return outs[0] if len(outs) == 1 else outs


def _rows(tm, n, col=0):
    return pl.BlockSpec((tm, n), lambda i: (i, col))


def _vec(n):
    return pl.BlockSpec((1, n), lambda i: (0, 0))


def _norm_mod(x, g, sc, sh, *, name, tm=ROW_TILE):
    t = x.shape[0]
    tm = min(tm, t)

    def body(x_ref, g_ref, sc_ref, sh_ref, h_ref):
        xv = x_ref[...]
        inv = lax.rsqrt(_mean(xv * xv) + EPS)
        h_ref[...] = ((xv * inv * g_ref[...]) * (1.0 + sc_ref[...]) + sh_ref[...]).astype(BF16)

    return pl.pallas_call(
        body, name=name, grid=(t // tm,), in_specs=[_rows(tm, D), _vec(D), _vec(D), _vec(D)],
        out_specs=_rows(tm, D), out_shape=jax.ShapeDtypeStruct((t, D), BF16),
        compiler_params=_cparams(("parallel",)))(x, g, sc, sh)


def _resid(x, y, gt, gp, *, name, tm=ROW_TILE):
    t = x.shape[0]
    tm = min(tm, t)

    def body(x_ref, y_ref, gt_ref, gp_ref, o_ref):
        yv = y_ref[...]
        inv = lax.rsqrt(_mean(yv * yv) + EPS)
        o_ref[...] = x_ref[...] + gt_ref[...] * (yv * inv * gp_ref[...])

    return pl.pallas_call(
        body, name=name, grid=(t // tm,), in_specs=[_rows(tm, D), _rows(tm, D), _vec(D), _vec(D)],
        out_specs=_rows(tm, D), out_shape=jax.ShapeDtypeStruct((t, D), F32),
        compiler_params=_cparams(("parallel",)))(x, y, gt, gp)


def _resid_norm(x, y, gt, gp, g, sc, sh, *, name, tm=ROW_TILE):
    t = x.shape[0]
    tm = min(tm, t)

    def body(x_ref, y_ref, gt_ref, gp_ref, g_ref, sc_ref, sh_ref, o_ref, h_ref):
        yv = y_ref[...]
        inv = lax.rsqrt(_mean(yv * yv) + EPS)
        xv = x_ref[...] + gt_ref[...] * (yv * inv * gp_ref[...])
        o_ref[...] = xv
        inv2 = lax.rsqrt(_mean(xv * xv) + EPS)
        h_ref[...] = ((xv * inv2 * g_ref[...]) * (1.0 + sc_ref[...]) + sh_ref[...]).astype(BF16)

    return pl.pallas_call(
        body, name=name, grid=(t // tm,), in_specs=[_rows(tm, D), _rows(tm, D)] + [_vec(D)] * 5,
        out_specs=[_rows(tm, D), _rows(tm, D)],
        out_shape=[jax.ShapeDtypeStruct((t, D), F32), jax.ShapeDtypeStruct((t, D), BF16)],
        compiler_params=_cparams(("parallel",)))(x, y, gt, gp, g, sc, sh)


def _resid_bwd(dx, y, gt, gp, *, name, tm=ROW_TILE):
    t = dx.shape[0]
    tm = min(tm, t)

    def body(dx_ref, y_ref, gt_ref, gp_ref, dy_ref, dgt_ref, dgp_ref):
        @pl.when(pl.program_id(0) == 0)
        def _():
            dgt_ref[...] = jnp.zeros_like(dgt_ref)
            dgp_ref[...] = jnp.zeros_like(dgp_ref)

        dxv, yv, gp_v = dx_ref[...], y_ref[...], gp_ref[...]
        inv = lax.rsqrt(_mean(yv * yv) + EPS)
        yh = yv * inv
        dgt_ref[...] += _colsum(dxv * (yh * gp_v))
        dr = dxv * gt_ref[...]
        dgp_ref[...] += _colsum(dr * yh)
        dyn = dr * gp_v
        dy_ref[...] = (inv * (dyn - yh * _mean(dyn * yh))).astype(BF16)

    return pl.pallas_call(
        body, name=name, grid=(t // tm,), in_specs=[_rows(tm, D), _rows(tm, D), _vec(D), _vec(D)],
        out_specs=[_rows(tm, D), _vec(D), _vec(D)],
        out_shape=[jax.ShapeDtypeStruct((t, D), BF16), jax.ShapeDtypeStruct((1, D), F32),
                   jax.ShapeDtypeStruct((1, D), F32)],
        compiler_params=_cparams(("arbitrary",)))(dx, y, gt, gp)


def _norm_bwd(dh, dx_res, x, g, sc, *, name, tm=ROW_TILE):
    t = dh.shape[0]
    tm = min(tm, t)

    def body(dh_ref, dxr_ref, x_ref, g_ref, sc_ref, dx_ref, dg_ref, dsc_ref, dsh_ref):
        @pl.when(pl.program_id(0) == 0)
        def _():
            dg_ref[...] = jnp.zeros_like(dg_ref)
            dsc_ref[...] = jnp.zeros_like(dsc_ref)
            dsh_ref[...] = jnp.zeros_like(dsh_ref)

        dhv, xv, gv = dh_ref[...], x_ref[...], g_ref[...]
        inv = lax.rsqrt(_mean(xv * xv) + EPS)
        xh = xv * inv
        dsh_ref[...] += _colsum(dhv)
        dsc_ref[...] += _colsum(dhv * (xh * gv))
        dn = dhv * (1.0 + sc_ref[...])
        dg_ref[...] += _colsum(dn * xh)
        dxh = dn * gv
        dx_ref[...] = inv * (dxh - xh * _mean(dxh * xh)) + dxr_ref[...]

    vec_out = jax.ShapeDtypeStruct((1, D), F32)
    return pl.pallas_call(
        body, name=name, grid=(t // tm,), in_specs=[_rows(tm, D), _rows(tm, D), _rows(tm, D), _vec(D), _vec(D)],
        out_specs=[_rows(tm, D), _vec(D), _vec(D), _vec(D)],
        out_shape=[jax.ShapeDtypeStruct((t, D), F32), vec_out, vec_out, vec_out],
        compiler_params=_cparams(("arbitrary",)))(dh, dx_res, x, g, sc)


def _loss_and_grad(x, target, *, tm=ROW_TILE):
    t = x.shape[0]
    tm = min(tm, t)

    def body(x_ref, t_ref, loss_ref, dx_ref):
        @pl.when(pl.program_id(0) == 0)
        def _():
            loss_ref[...] = jnp.zeros_like(loss_ref)

        e = x_ref[...] - t_ref[...]
        dx_ref[...] = e * (1.0 / D)
        s = jnp.sum(jnp.sum(e * e, axis=1, keepdims=True), axis=0, keepdims=True) * (0.5 / D)
        loss_ref[...] += jnp.broadcast_to(s, loss_ref.shape)

    loss, dx = pl.pallas_call(
        body, name="loss", grid=(t // tm,), in_specs=[_rows(tm, D), _rows(tm, D)],
        out_specs=[pl.BlockSpec((8, LANE), lambda i: (0, 0)), _rows(tm, D)],
        out_shape=[jax.ShapeDtypeStruct((8, LANE), F32), jax.ShapeDtypeStruct((t, D), F32)],
        compiler_params=_cparams(("arbitrary",)))(x, target)
    return loss[0, 0], dx


def _gmlp_core(uv, lng, lnb, ws_ref, bsx):
    tm = uv.shape[0]
    gu = _gelu(uv[:, :DM])
    gv = _gelu(uv[:, DM:])
    mu = _mean(gv)
    vc = gv - mu
    rstd = lax.rsqrt(_mean(vc * vc) + EPS)
    vh = vc * rstd
    vln = vh * lng + lnb
    lane = lax.broadcasted_iota(I32, (CH, LANE), 1)
    sv_rows = []
    for nchunk in range(tm // CH):
        vb = vln[nchunk * CH:(nchunk + 1) * CH].astype(BF16)
        cols = []
        for cb in range(DM // LANE):
            vcb = vb[:, cb * LANE:(cb + 1) * LANE]
            lo = _dot(ws_ref[2 * cb], vcb)
            hi = _dot(ws_ref[2 * cb + 1], vcb)
            cols.append(jnp.where(lane < GD, lo, hi))
        sv_rows.append(jnp.concatenate(cols, axis=1) + bsx)
    sv = jnp.concatenate(sv_rows, axis=0) if len(sv_rows) > 1 else sv_rows[0]
    return gu, vh, rstd, vln, sv


def _gmlp_fwd(proj, lng, lnb, wsm, bsx, *, tm=256):
    t = proj.shape[0]
    tm = min(tm, t)

    def body(uv_ref, lng_ref, lnb_ref, ws_ref, bs_ref, ga_ref):
        gu, _, _, _, sv = _gmlp_core(uv_ref[...], lng_ref[...], lnb_ref[...], ws_ref, bs_ref[...])
        ga_ref[...] = (gu * sv).astype(BF16)

    return pl.pallas_call(
        body, name="gmlp_fwd", grid=(t // tm,),
        in_specs=[_rows(tm, 2 * DM, C_UV // (2 * DM)), _vec(DM), _vec(DM),
                  pl.BlockSpec((NG, CH, CH), lambda i: (0, 0, 0)), pl.BlockSpec((CH, DM), lambda i: (0, 0))],
        out_specs=_rows(tm, DM), out_shape=jax.ShapeDtypeStruct((t, DM), BF16),
        compiler_params=_cparams(("parallel",)))(proj, lng, lnb, wsm, bsx)


def _gmlp_bwd(dga, proj, lng, lnb, wsm, wsmt, bsx, *, tm=256):
    t = proj.shape[0]
    tm = min(tm, t)

    def body(dga_ref, uv_ref, lng_ref, lnb_ref, ws_ref, wst_ref, bs_ref, duv_ref, dws_ref, dbs_ref, dlng_ref, dlnb_ref,
             dbsx_ref):
        i = pl.program_id(0)

        @pl.when(i == 0)
        def _():
            dws_ref[...] = jnp.zeros_like(dws_ref)
            dbsx_ref[...] = jnp.zeros_like(dbsx_ref)
            dlng_ref[...] = jnp.zeros_like(dlng_ref)
            dlnb_ref[...] = jnp.zeros_like(dlnb_ref)

        uv = uv_ref[...]
        lng_v = lng_ref[...]
        gu, vh, rstd, vln, sv = _gmlp_core(uv, lng_v, lnb_ref[...], ws_ref, bs_ref[...])
        dga_v = dga_ref[...]
        dgu = dga_v * sv
        dsv = dga_v * gu
        lane = lax.broadcasted_iota(I32, (CH, LANE), 1)
        tril = lax.broadcasted_iota(I32, (CH, CH), 0) >= lax.broadcasted_iota(I32, (CH, CH), 1)
        dvln_rows = []
        for nchunk in range(tm // CH):
            rows = slice(nchunk * CH, (nchunk + 1) * CH)
            dbsx_ref[...] += dsv[rows]
            vb = vln[rows].astype(BF16)
            cols = []
            for cb in range(DM // LANE):
                cs = slice(cb * LANE, (cb + 1) * LANE)
                dsvb = dsv[rows, cs]
                vcb = vb[:, cs]
                dlo = jnp.where(lane < GD, dsvb, 0.0).astype(BF16)
                dhi = jnp.where(lane < GD, 0.0, dsvb).astype(BF16)
                dws_ref[2 * cb] += jnp.where(tril, _dot(dlo, vcb, NT), 0.0)
                dws_ref[2 * cb + 1] += jnp.where(tril, _dot(dhi, vcb, NT), 0.0)
                dsb = dsvb.astype(BF16)
                cols.append(jnp.where(lane < GD, _dot(wst_ref[2 * cb], dsb), _dot(wst_ref[2 * cb + 1], dsb)))
            dvln_rows.append(jnp.concatenate(cols, axis=1))
        dvln = jnp.concatenate(dvln_rows, axis=0) if len(dvln_rows) > 1 else dvln_rows[0]
        dlnb_ref[...] += _colsum(dvln)
        dlng_ref[...] += _colsum(dvln * vh)
        dvh = dvln * lng_v
        dgv = rstd * (dvh - _mean(dvh) - vh * _mean(dvh * vh))
        duv_ref[:, :DM] = (dgu * _gelu_grad(uv[:, :DM])).astype(BF16)
        duv_ref[:, DM:] = (dgv * _gelu_grad(uv[:, DM:])).astype(BF16)

        @pl.when(i == pl.num_programs(0) - 1)
        def _():
            ind = (lax.broadcasted_iota(I32, (DM, LANE), 0) // GD ==lax.broadcasted_iota(I32, (DM, LANE), 1)).astype(F32)
            dbs_ref[...] = jnp.dot(dbsx_ref[...], ind, preferred_element_type=F32, precision=lax.Precision.HIGHEST)

    vec_out = jax.ShapeDtypeStruct((1, DM), F32)
    outs = pl.pallas_call(
        body, name="gmlp_bwd", grid=(t // tm,),
        in_specs=[_rows(tm, DM), _rows(tm, 2 * DM, C_UV // (2 * DM)), _vec(DM), _vec(DM),
                  pl.BlockSpec((NG, CH, CH), lambda i: (0, 0, 0)), pl.BlockSpec((NG, CH, CH), lambda i: (0, 0, 0)),
                  pl.BlockSpec((CH, DM), lambda i: (0, 0))],
        out_specs=[_rows(tm, 2 * DM), pl.BlockSpec((NG, CH, CH), lambda i: (0, 0, 0)),
                   pl.BlockSpec((CH, LANE), lambda i: (0, 0)), _vec(DM), _vec(DM)],
        out_shape=[jax.ShapeDtypeStruct((t, 2 * DM), BF16), jax.ShapeDtypeStruct((NG, CH, CH), F32),
                   jax.ShapeDtypeStruct((CH, LANE), F32), vec_out, vec_out],
        scratch_shapes=[pltpu.VMEM((CH, DM), F32)],
        compiler_params=_cparams(("arbitrary",)))(dga, proj, lng, lnb, wsm, wsmt, bsx)
    return outs


def _glu_into(zs_ref, glu_ref, halo_ref, first):
    hal = halo_ref[...]
    z0h = hal[:, :DM] * _sigmoid(hal[:, DM:])
    zs_ref[0:HALO, :] = jnp.where(first, 0.0, z0h)
    g = glu_ref[...]
    zs_ref[HALO:, :] = g[:, :DM] * _sigmoid(g[:, DM:])


SUB = 8


def _shifted_copies(dst_ref, src_ref):
    n = src_ref.shape[0]
    for s in range(SUB):
        dst_ref[s, 0:n - s, :] = src_ref[s:n, :]


def _window(shifted_ref, off, rows, cs):
    s = off % SUB
    return shifted_ref[s, off - s:off - s + rows, cs]


def _conv_fwd(proj, cw, cb, lng, lnb, *, tm=256, rb=64):
    t = proj.shape[0]
    tm = min(tm, t)
    hb = tm // HALO
    gcol = C_GLU // (2 * DM)

    def body(glu_ref, halo_ref, cw_ref, cb_ref, lng_ref, lnb_ref, zc_ref, zb_ref, zs_ref, zsh_ref):
        i = pl.program_id(0)
        _glu_into(zs_ref, glu_ref, halo_ref, i == 0)
        _shifted_copies(zsh_ref, zs_ref)
        for cbk in range(DM // LANE):
            cs = slice(cbk * LANE, (cbk + 1) * LANE)
            for r in range(tm // rb):
                acc = jnp.broadcast_to(cb_ref[:, cs], (rb, LANE))
                for k in range(KW):
                    acc = acc + cw_ref[k:k + 1, cs] * _window(zsh_ref, r * rb + HALO - (KW - 1) + k, rb, cs)
                zc_ref[r * rb:(r + 1) * rb, cs] = acc
        zc = zc_ref[...]
        mu = _mean(zc)
        zcc = zc - mu
        zh = zcc * lax.rsqrt(_mean(zcc * zcc) + EPS)
        a = zh * lng_ref[...] + lnb_ref[...]
        zb_ref[...] = (a * _sigmoid(a)).astype(BF16)

    return pl.pallas_call(
        body, name="conv_fwd", grid=(t // tm,),
        in_specs=[_rows(tm, 2 * DM, gcol),
                  pl.BlockSpec((HALO, 2 * DM), lambda i: (jnp.maximum(i * hb - 1, 0), gcol)),
                  pl.BlockSpec((KW, DM), lambda i: (0, 0)), _vec(DM), _vec(DM), _vec(DM)],
        out_specs=[_rows(tm, DM), _rows(tm, DM)],
        out_shape=[jax.ShapeDtypeStruct((t, DM), F32), jax.ShapeDtypeStruct((t, DM), BF16)],
        scratch_shapes=[pltpu.VMEM((HALO + tm, DM), F32), pltpu.VMEM((SUB, HALO + tm, DM), F32)],
        compiler_params=_cparams(("parallel",)))(proj, proj, cw, cb, lng, lnb)


def _conv_bwd_ln(dzb, zc, lng, lnb, *, tm=ROW_TILE):
    t = zc.shape[0]
    tm = min(tm, t)

    def body(dzb_ref, zc_ref, lng_ref, lnb_ref, dzc_ref, dlng_ref, dlnb_ref):
        @pl.when(pl.program_id(0) == 0)
        def _():
            dlng_ref[...] = jnp.zeros_like(dlng_ref)
            dlnb_ref[...] = jnp.zeros_like(dlnb_ref)

        zc = zc_ref[...]
        lng_v = lng_ref[...]
        mu = _mean(zc)
        zcc = zc - mu
        rstd = lax.rsqrt(_mean(zcc * zcc) + EPS)
        zh = zcc * rstd
        a = zh * lng_v + lnb_ref[...]
        s = _sigmoid(a)
        da = dzb_ref[...] * (s * (1.0 + a * (1.0 - s)))
        dlnb_ref[...] += _colsum(da)
        dlng_ref[...] += _colsum(da * zh)
        dzh = da * lng_v
        dzc_ref[...] = rstd * (dzh - _mean(dzh) - zh * _mean(dzh * zh))

    vec_out = jax.ShapeDtypeStruct((1, DM), F32)
    return pl.pallas_call(
        body, name="conv_bwd_ln", grid=(t // tm,), in_specs=[_rows(tm, DM), _rows(tm, DM), _vec(DM), _vec(DM)],
        out_specs=[_rows(tm, DM), _vec(DM), _vec(DM)],
        out_shape=[jax.ShapeDtypeStruct((t, DM), F32), vec_out, vec_out],
        compiler_params=_cparams(("arbitrary",)))(dzb, zc, lng, lnb)


def _conv_bwd(dzc, proj, cw, *, tm=256, rb=64):
    t = proj.shape[0]
    tm = min(tm, t)
    hb = tm // HALO
    nblk = t // tm
    gcol = C_GLU // (2 * DM)

    def body(dzc_ref, dnext_ref, glu_ref, halo_ref, cw_ref, dglu_ref, dcw_ref, dcb_ref, zs_ref, ds_ref, zsh_ref, dsh_ref):
        i = pl.program_id(0)

        @pl.when(i == 0)
        def _():
            dcw_ref[...] = jnp.zeros_like(dcw_ref)
            dcb_ref[...] = jnp.zeros_like(dcb_ref)

        _glu_into(zs_ref, glu_ref, halo_ref, i == 0)
        _shifted_copies(zsh_ref, zs_ref)
        dzc = dzc_ref[...]
        ds_ref[0:tm, :] = dzc
        ds_ref[tm:, :] = jnp.where(i == nblk - 1, 0.0, dnext_ref[...])
        _shifted_copies(dsh_ref, ds_ref)
        dcb_ref[...] += _colsum(dzc)
        for k in range(KW):
            dcw_ref[k:k + 1, :] += _colsum(dzc * _window(zsh_ref, HALO - (KW - 1) + k, tm, slice(None)))
        g = glu_ref[...]
        val, sg = g[:, :DM], _sigmoid(g[:, DM:])
        for cbk in range(DM // LANE):
            cs = slice(cbk * LANE, (cbk + 1) * LANE)
            for r in range(tm // rb):
                acc = jnp.zeros((rb, LANE), F32)
                for k in range(KW):
                    acc = acc + cw_ref[k:k + 1, cs] * _window(dsh_ref, r * rb + (KW - 1) - k, rb, cs)
                rs = slice(r * rb, (r + 1) * rb)
                dglu_ref[rs, cs] = (acc * sg[rs, cs]).astype(BF16)
                v, s = val[rs, cs], sg[rs, cs]
                dglu_ref[rs, DM + cbk * LANE:DM + (cbk + 1) * LANE] = (acc * v * s * (1.0 - s)).astype(BF16)

    return pl.pallas_call(
        body, name="conv_bwd", grid=(nblk,),
        in_specs=[_rows(tm, DM),
                  pl.BlockSpec((HALO, DM), lambda i: (jnp.minimum((i + 1) * hb, nblk * hb - 1), 0)),
                  _rows(tm, 2 * DM, gcol),
                  pl.BlockSpec((HALO, 2 * DM), lambda i: (jnp.maximum(i * hb - 1, 0), gcol)),
                  pl.BlockSpec((KW, DM), lambda i: (0, 0))],
        out_specs=[_rows(tm, 2 * DM), pl.BlockSpec((HALO, DM), lambda i: (0, 0)), _vec(DM)],
        out_shape=[jax.ShapeDtypeStruct((t, 2 * DM), BF16), jax.ShapeDtypeStruct((HALO, DM), F32),
                   jax.ShapeDtypeStruct((1, DM), F32)],
        scratch_shapes=[pltpu.VMEM((HALO + tm, DM), F32), pltpu.VMEM((tm + HALO, DM), F32),
                        pltpu.VMEM((SUB, HALO + tm, DM), F32), pltpu.VMEM((SUB, tm + HALO, DM), F32)],
        compiler_params=_cparams(("arbitrary",)))(dzc, dzc, proj, proj, cw)


CUM_ROWS = 512


def _log_sigmoid(x):
    return jnp.minimum(x, 0.0) - jnp.log1p(jnp.exp(-jnp.abs(x)))


def _fox_cum(proj, bfp):
    t = proj.shape[0]
    fcol = C_F // LANE
    cr = min(CUM_ROWS, t)

    def body(f_ref, bf_ref, cum_ref, carry_ref):
        @pl.when(pl.program_id(0) == 0)
        def _():
            carry_ref[...] = jnp.zeros_like(carry_ref)

        lf = _log_sigmoid(f_ref[...] + bf_ref[...])
        tri = (lax.broadcasted_iota(I32, (cr, cr), 0) >= lax.broadcasted_iota(I32, (cr, cr), 1)).astype(F32)
        cum = jnp.dot(tri, lf, preferred_element_type=F32, precision=lax.Precision.HIGHEST) + carry_ref[0:1, :]
        cum_ref[...] = cum
        carry_ref[...] = jnp.broadcast_to(cum[cr - 1:cr, :], carry_ref.shape)

    return pl.pallas_call(
        body, name="fox_cum", grid=(t // cr,), in_specs=[_rows(cr, LANE, fcol), _vec(LANE)],
        out_specs=_rows(cr, LANE), out_shape=jax.ShapeDtypeStruct((t, LANE), F32),
        scratch_shapes=[pltpu.VMEM((8, LANE), F32)],
        compiler_params=_cparams(("arbitrary",)))(proj, bfp)


def _fox_cum_bwd(dcum, proj, bfp):
    t = proj.shape[0]
    cr = min(CUM_ROWS, t)
    nb = t // cr
    fcol = C_F // LANE
    fw = D_INP - C_F

    def body(dc_ref, f_ref, bf_ref, df_ref, dbf_ref, carry_ref):
        @pl.when(pl.program_id(0) == 0)
        def _():
            carry_ref[...] = jnp.zeros_like(carry_ref)
            dbf_ref[...] = jnp.zeros_like(dbf_ref)

        triu = (lax.broadcasted_iota(I32, (cr, cr), 0) <= lax.broadcasted_iota(I32, (cr, cr), 1)).astype(F32)
        dlf = jnp.dot(triu, dc_ref[...], preferred_element_type=F32, precision=lax.Precision.HIGHEST) + carry_ref[0:1, :]
        carry_ref[...] = jnp.broadcast_to(dlf[0:1, :], carry_ref.shape)
        z = f_ref[...] + bf_ref[...]
        lane = lax.broadcasted_iota(I32, (cr, LANE), 1)
        df = jnp.where(lane < NG, dlf * _sigmoid(-z), 0.0)
        dbf_ref[...] += _colsum(df)
        df_ref[:, 0:LANE] = df.astype(BF16)
        df_ref[:, LANE:] = jnp.zeros((cr, fw - LANE), BF16)

    return pl.pallas_call(
        body, name="fox_cum_bwd", grid=(nb,),
        in_specs=[pl.BlockSpec((cr, LANE), lambda i: (nb - 1 - i, 0)),
                  pl.BlockSpec((cr, LANE), lambda i: (nb - 1 - i, fcol)), _vec(LANE)],
        out_specs=[pl.BlockSpec((cr, fw), lambda i: (nb - 1 - i, 0)), _vec(LANE)],
        out_shape=[jax.ShapeDtypeStruct((t, fw), BF16), jax.ShapeDtypeStruct((1, LANE), F32)],
        scratch_shapes=[pltpu.VMEM((8, LANE), F32)],
        compiler_params=_cparams(("arbitrary",)))(dcum, proj, bfp)


HD = 64
ATT_SCALE = 0.125
NEG = -1e30


def _qkv_prep(proj, *, tm=512):
    t = proj.shape[0]
    tm = min(tm, t)

    def body(q_ref, k_ref, v_ref, o_ref):
        o_ref[:, 0:DM] = (q_ref[...] * ATT_SCALE).astype(BF16)
        o_ref[:, DM:2 * DM] = k_ref[...].astype(BF16)
        o_ref[:, 2 * DM:] = v_ref[...].astype(BF16)

    return pl.pallas_call(
        body, name="qkv_prep", grid=(t // tm,),
        in_specs=[_rows(tm, DM, C_Q // DM), _rows(tm, DM, C_K // DM), _rows(tm, DM, C_V // DM)],
        out_specs=_rows(tm, 3 * DM), out_shape=jax.ShapeDtypeStruct((t, 3 * DM), BF16),
        compiler_params=_cparams(("parallel",)))(proj, proj, proj)


def _causal_pairs(nq, outer_is_query):
    if outer_is_query:
        pairs = [(i, j) for i in range(nq) for j in range(i + 1)]
    else:
        pairs = [(j, i) for j in range(nq) for i in range(j, nq)]
    return (jnp.asarray([p[0] for p in pairs], I32), jnp.asarray([p[1] for p in pairs], I32))


def _to_row(col):
    return jnp.transpose(col)[0:1, :]


def _rep(x, tk):
    return x if tk == LANE else jnp.tile(x, (1, tk // LANE))


def _attn_fwd(qkv, ckrow, *, tq=512):
    t = qkv.shape[0]
    tq = min(tq, t)
    tk = tq
    nq = t // tq
    oi, ij = _causal_pairs(nq, True)

    def body(oi_ref, ij_ref, q_ref, k_ref, v_ref, ck_ref, o_ref, lser_ref, m_ref, l_ref, acc_ref):
        n = pl.program_id(0)
        i, j = oi_ref[n], ij_ref[n]

        @pl.when(j == 0)
        def _():
            m_ref[...] = jnp.full_like(m_ref, NEG)
            l_ref[...] = jnp.zeros_like(l_ref)
            acc_ref[...] = jnp.zeros_like(acc_ref)

        def step(masked):
            if masked:
                keep = lax.broadcasted_iota(I32, (tq, tk), 1) <= lax.broadcasted_iota(I32, (tq, tk), 0)
            lo = lax.broadcasted_iota(I32, (tq, LANE), 1) < HD
            for hp in range(NG // 2):
                cs = slice(hp * LANE, (hp + 1) * LANE)
                qp, kp, vp = q_ref[:, cs], k_ref[:, cs], v_ref[:, cs]
                alphas, pvs = [], []
                for hh in range(2):
                    h = 2 * hp + hh
                    qm = jnp.where(lo if hh == 0 else jnp.logical_not(lo), qp, jnp.zeros_like(qp))
                    s = _dot(qm, kp, NT) - ck_ref[h:h + 1, :]
                    if masked:
                        s = jnp.where(keep, s, NEG)
                    m_prev = m_ref[h]
                    m_new = jnp.maximum(m_prev, jnp.max(s, axis=1, keepdims=True))
                    alpha = jnp.exp(m_prev - m_new)
                    p = jnp.exp(s - _rep(m_new, tk))
                    l_ref[h] = alpha * l_ref[h] + jnp.sum(p, axis=1, keepdims=True)
                    m_ref[h] = m_new
                    alphas.append(alpha)
                    pvs.append(_dot(p.astype(BF16), vp))
                acc_ref[:, cs] = jnp.where(lo, alphas[0], alphas[1]) * acc_ref[:, cs] + jnp.where(lo, pvs[0], pvs[1])

        @pl.when(j < i)
        def _():
            step(False)

        @pl.when(j == i)
        def _():
            step(True)
            lo = lax.broadcasted_iota(I32, (tq, LANE), 1) < HD
            for hp in range(NG // 2):
                cs = slice(hp * LANE, (hp + 1) * LANE)
                o_ref[:, cs] = (acc_ref[:, cs] / jnp.where(lo, l_ref[2 * hp], l_ref[2 * hp + 1])).astype(BF16)
            for h in range(NG):
                lser_ref[h:h + 1, :] = _to_row(m_ref[h] + jnp.log(l_ref[h]))

    gs = pltpu.PrefetchScalarGridSpec(
        num_scalar_prefetch=2, grid=(int(oi.shape[0]),),
        in_specs=[pl.BlockSpec((tq, DM), lambda n, a, b: (a[n], 0)),
                  pl.BlockSpec((tk, DM), lambda n, a, b: (b[n], 1)),
                  pl.BlockSpec((tk, DM), lambda n, a, b: (b[n], 2)),
                  pl.BlockSpec((NG, tk), lambda n, a, b: (0, b[n]))],
        out_specs=[pl.BlockSpec((tq, DM), lambda n, a, b: (a[n], 0)),
                   pl.BlockSpec((NG, tq), lambda n, a, b: (0, a[n]))],
        scratch_shapes=[pltpu.VMEM((NG, tq, LANE), F32), pltpu.VMEM((NG, tq, LANE), F32), pltpu.VMEM((tq, DM), F32)])
    return pl.pallas_call(
        body, name="attn_fwd", grid_spec=gs,
        out_shape=[jax.ShapeDtypeStruct((t, DM), BF16), jax.ShapeDtypeStruct((NG, t), F32)],
        compiler_params=_cparams(("arbitrary",)))(oi, ij, qkv, qkv, qkv, ckrow)


def _attn_delta(o, do, *, tq=512):
    t = o.shape[0]
    tq = min(tq, t)

    def body(o_ref, do_ref, d_ref):
        lo = lax.broadcasted_iota(I32, (tq, LANE), 1) < HD
        for hp in range(NG // 2):
            cs = slice(hp * LANE, (hp + 1) * LANE)
            prod = do_ref[:, cs] * o_ref[:, cs].astype(F32)
            for hh in range(2):
                d = jnp.sum(jnp.where(lo if hh == 0 else jnp.logical_not(lo), prod, 0.0), axis=1, keepdims=True)
                d_ref[2 * hp + hh:2 * hp + hh + 1, :] = _to_row(jnp.broadcast_to(d, (tq, LANE)))

    return pl.pallas_call(
        body, name="attn_delta", grid=(t // tq,),
        in_specs=[pl.BlockSpec((tq, DM), lambda i: (i, 0)), pl.BlockSpec((tq, DM), lambda i: (i, 0))],
        out_specs=pl.BlockSpec((NG, tq), lambda i: (0, i)), out_shape=jax.ShapeDtypeStruct((NG, t), F32),
        compiler_params=_cparams(("parallel",)))(o, do)


def _attn_bwd(qkv, ckcol, do, lserow, deltarow, *, tq=512):
    t = qkv.shape[0]
    tq = min(tq, t)
    tk = tq
    nq = t // tq
    oj, ii = _causal_pairs(nq, False)
    npairs = int(oj.shape[0])

    def body(oj_ref, ii_ref, q_ref, k_ref, v_ref, ck_ref, do_ref, lse_ref, delta_ref, dq_ref, dk_ref, dv_ref, dcq_ref,
             dckr_ref, dqa_ref, dka_ref, dva_ref, dck_ref):
        n = pl.program_id(0)
        j, i = oj_ref[n], ii_ref[n]

        @pl.when(n == 0)
        def _():
            dqa_ref[...] = jnp.zeros_like(dqa_ref)
            dcq_ref[...] = jnp.zeros_like(dcq_ref)

        @pl.when(i == j)
        def _():
            dka_ref[...] = jnp.zeros_like(dka_ref)
            dva_ref[...] = jnp.zeros_like(dva_ref)
            dck_ref[...] = jnp.zeros_like(dck_ref)

        def step(masked):
            if masked:
                keep = lax.broadcasted_iota(I32, (tk, tq), 0) <= lax.broadcasted_iota(I32, (tk, tq), 1)
            lo = lax.broadcasted_iota(I32, (tk, LANE), 1) < HD
            for hp in range(NG // 2):
                cs = slice(hp * LANE, (hp + 1) * LANE)
                qp, kp, vp, dop = q_ref[:, cs], k_ref[:, cs], v_ref[:, cs], do_ref[:, cs].astype(BF16)
                dvs, dks, dqs = [], [], []
                for hh in range(2):
                    h = 2 * hp + hh
                    sel = lo if hh == 0 else jnp.logical_not(lo)
                    st = _dot(jnp.where(sel, kp, jnp.zeros_like(kp)), qp, NT) - _rep(ck_ref[h], tq)
                    if masked:
                        st = jnp.where(keep, st, NEG)
                    pt = jnp.exp(st - lse_ref[h:h + 1, :])
                    dvs.append(_dot(pt.astype(BF16), dop))
                    dpt = _dot(jnp.where(sel, vp, jnp.zeros_like(vp)), dop, NT)
                    dst = pt * (dpt - delta_ref[h:h + 1, :])
                    dsb = dst.astype(BF16)
                    dks.append(_dot(dsb, qp))
                    dqs.append(_dot(dsb, kp, TN))
                    dck_ref[h] -= jnp.sum(dst, axis=1, keepdims=True)
                    dcq_ref[i, h:h + 1, :] += jnp.sum(dst, axis=0, keepdims=True)
                dva_ref[:, cs] += jnp.where(lo, dvs[0], dvs[1])
                dka_ref[:, cs] += jnp.where(lo, dks[0], dks[1])
                dqa_ref[i, :, cs] += jnp.where(lo, dqs[0], dqs[1])

        @pl.when(i == j)
        def _():
            step(True)

        @pl.when(i > j)
        def _():
            step(False)

        @pl.when(i == nq - 1)
        def _():
            dk_ref[...] = dka_ref[...].astype(BF16)
            dv_ref[...] = dva_ref[...].astype(BF16)
            for h in range(NG):
                dckr_ref[h:h + 1, :] = _to_row(dck_ref[h])

        @pl.when(n == npairs - 1)
        def _():
            dq_ref[...] = (dqa_ref[...] * ATT_SCALE).astype(BF16)

    gs = pltpu.PrefetchScalarGridSpec(
        num_scalar_prefetch=2, grid=(npairs,),
        in_specs=[pl.BlockSpec((tq, DM), lambda n, a, b: (b[n], 0)),
                  pl.BlockSpec((tk, DM), lambda n, a, b: (a[n], 1)),
                  pl.BlockSpec((tk, DM), lambda n, a, b: (a[n], 2)),
                  pl.BlockSpec((NG, tk, LANE), lambda n, a, b: (0, a[n], 0)),
                  pl.BlockSpec((tq, DM), lambda n, a, b: (b[n], 0)),
                  pl.BlockSpec((NG, tq), lambda n, a, b: (0, b[n])),
                  pl.BlockSpec((NG, tq), lambda n, a, b: (0, b[n]))],
        out_specs=[pl.BlockSpec((nq, tq, DM), lambda n, a, b: (0, 0, 0)),
                   pl.BlockSpec((tk, DM), lambda n, a, b: (a[n], 0)),
                   pl.BlockSpec((tk, DM), lambda n, a, b: (a[n], 0)),
                   pl.BlockSpec((nq, NG, tq), lambda n, a, b: (0, 0, 0)),
                   pl.BlockSpec((NG, tk), lambda n, a, b: (0, a[n]))],
        scratch_shapes=[pltpu.VMEM((nq, tq, DM), F32), pltpu.VMEM((tk, DM), F32), pltpu.VMEM((tk, DM), F32),
                        pltpu.VMEM((NG, tk, LANE), F32)])
    return pl.pallas_call(
        body, name="attn_bwd", grid_spec=gs,
        out_shape=[jax.ShapeDtypeStruct((nq, tq, DM), BF16), jax.ShapeDtypeStruct((t, DM), BF16),
                   jax.ShapeDtypeStruct((t, DM), BF16), jax.ShapeDtypeStruct((nq, NG, tq), F32),
                   jax.ShapeDtypeStruct((NG, t), F32)],
        compiler_params=_cparams(("arbitrary",)))(oj, ii, qkv, qkv, qkv, ckcol, do, lserow, deltarow)


def _merge_fwd(ga, zb, att, proj, wa, wb, wc, *, tm=256):
    t = ga.shape[0]
    tm = min(tm, t)
    wspec = pl.BlockSpec((DM, D), lambda i: (0, 0))

    def body(ga_ref, zb_ref, att_ref, gate_ref, wa_ref, wb_ref, wc_ref, m_ref):
        acc = jnp.zeros((tm, D), F32)
        for b, (x_ref, w_ref) in enumerate(((ga_ref, wa_ref), (zb_ref, wb_ref), (att_ref, wc_ref))):
            acc = acc + _sigmoid(gate_ref[:, b * D:(b + 1) * D]) * _dot(x_ref[...], w_ref[...])
        m_ref[...] = acc.astype(BF16)

    return pl.pallas_call(
        body, name="merge_fwd", grid=(t // tm,),
        in_specs=[_rows(tm, DM), _rows(tm, DM), _rows(tm, DM), _rows(tm, 3 * D, 0), wspec, wspec, wspec],
        out_specs=_rows(tm, D), out_shape=jax.ShapeDtypeStruct((t, D), BF16),
        compiler_params=_cparams(("parallel",)))(ga, zb, att, proj, wa, wb, wc)


def _merge_bwd(dm, ga, zb, att, proj, wa, wb, wc, *, tm=256):
    t = ga.shape[0]
    tm = min(tm, t)
    wspec = pl.BlockSpec((DM, D), lambda i: (0, 0))

    def body(dm_ref, ga_ref, zb_ref, att_ref, gate_ref, wa_ref, wb_ref, wc_ref, dgate_ref, dga_ref, dzb_ref, datt_ref,
             dwa_ref, dwb_ref, dwc_ref):
        @pl.when(pl.program_id(0) == 0)
        def _():
            dwa_ref[...] = jnp.zeros_like(dwa_ref)
            dwb_ref[...] = jnp.zeros_like(dwb_ref)
            dwc_ref[...] = jnp.zeros_like(dwc_ref)

        dmv = dm_ref[...]
        branches = ((ga_ref, wa_ref, dga_ref, dwa_ref), (zb_ref, wb_ref, dzb_ref, dwb_ref),
                    (att_ref, wc_ref, datt_ref, dwc_ref))
        for b, (x_ref, w_ref, dx_ref, dw_ref) in enumerate(branches):
            xv, wv = x_ref[...], w_ref[...]
            y = _dot(xv, wv)
            g = _sigmoid(gate_ref[:, b * D:(b + 1) * D])
            dgate_ref[:, b * D:(b + 1) * D] = (dmv * y * g * (1.0 - g)).astype(BF16)
            dy = (dmv * g).astype(BF16)
            dx_ref[...] = _dot(dy, wv, NT)
            dw_ref[...] += _dot(xv, dy, TN)

    return pl.pallas_call(
        body, name="merge_bwd", grid=(t // tm,),
        in_specs=[_rows(tm, D), _rows(tm, DM), _rows(tm, DM), _rows(tm, DM), _rows(tm, 3 * D, 0), wspec, wspec, wspec],
        out_specs=[_rows(tm, 3 * D), _rows(tm, DM), _rows(tm, DM), _rows(tm, DM), wspec, wspec, wspec],
        out_shape=[jax.ShapeDtypeStruct((t, 3 * D), BF16)] + [jax.ShapeDtypeStruct((t, DM), F32)] * 3
        + [jax.ShapeDtypeStruct((DM, D), F32)] * 3,
        compiler_params=_cparams(("arbitrary",)))(dm, ga, zb, att, proj, wa, wb, wc)


def _heads_layout(cum):
    t = cum.shape[0]
    ckrow = cum[:, :NG].T
    return ckrow, jnp.broadcast_to(ckrow[:, :, None], (NG, t, LANE))


def _layer_fwd(x, mod, w, late=None):
    sh1, sc1, gt1, sh2, sc2, gt2 = (mod[k:k + 1] for k in range(NMOD))
    h1 = _norm_mod(x, w["mix_pre_g"], sc1, sh1, name="norm_mix")
    proj = _matmul(h1, w["w_in_p"], name="mm_proj")
    ga = _gmlp_fwd(proj, w["gmlp_ln_g"], w["gmlp_ln_b"], w["wsm"], w["bsx"])
    zc, zb = _conv_fwd(proj, w["conv_w"], w["conv_b"], w["conv_ln_g"], w["conv_ln_b"])
    cum = _fox_cum(proj, w["bfp"])
    ckrow, ckcol = _heads_layout(cum)
    qkv = _qkv_prep(proj)
    att, lser = _attn_fwd(qkv, ckrow)
    if late is not None:
        w = {**w, **late(att)}
    merged = _merge_fwd(ga, zb, att, proj, w["w_a_out"], w["w_b_out"], w["w_c_out"])
    y1 = _matmul(merged, w["w_out"], name="mm_out")
    x2, h2 = _resid_norm(x, y1, gt1, w["mix_post_g"], w["mlp_pre_g"], sc2, sh2, name="resid_mix_norm_mlp")
    hid = _matmul(h2, w["mlp_w1"], name="mm_w1", b_slabs=True, out_dtype=BF16,
                  epilogue=lambda acc: (jnp.square(jnp.maximum(acc, 0.0)),))
    y2 = _matmul(hid, w["mlp_w2"], name="mm_w2", tk=TK_DEEP)
    x3 = _resid(x2, y2, gt2, w["mlp_post_g"], name="resid_mlp")
    saved = dict(x=x, h1=h1, proj=proj, ga=ga, zc=zc, zb=zb, qkv=qkv, ckrow=ckrow, ckcol=ckcol, att=att, lser=lser, merged=merged,
                 y1=y1, x2=x2, h2=h2, hid=hid, y2=y2)
    return x3, saved, w


def _layer_bwd(dx3, mod, w, s, mid=None):
    sh1, sc1, gt1, sh2, sc2, gt2 = (mod[k:k + 1] for k in range(NMOD))
    g = {}
    dy2, dgt2, g["mlp_post_g"] = _resid_bwd(dx3, s["y2"], gt2, w["mlp_post_g"], name="resid_mlp_bwd")
    da = _matmul(dy2, w["mlp_w2"], tb=True, name="mm_dhid", out_dtype=BF16, extra=(s["hid"],),
                 epilogue=lambda acc, hid: (acc * (2.0 * jnp.sqrt(hid.astype(F32))),))
    g["mlp_w2"] = _matmul(s["hid"], dy2, ta=True, name="mm_dw2", tk=TK_DEEP)
    g["mlp_w1"] = _matmul(s["h2"], da, ta=True, name="mm_dw1", out_slabs=N_CHIPS, tk=TK_DEEP)
    dh2 = _matmul(da, w["mlp_w1"], tb=True, name="mm_dh2", b_slabs=True, tk=TK_DEEP)
    dx2, g["mlp_pre_g"], dsc2, dsh2 = _norm_bwd(dh2, dx3, s["x2"], w["mlp_pre_g"], sc2, name="norm_mlp_bwd")
    if mid is not None:
        gt1 = gt1 + mid(dx2)
    dy1, dgt1, g["mix_post_g"] = _resid_bwd(dx2, s["y1"], gt1, w["mix_post_g"], name="resid_mix_bwd")
    dmerged = _matmul(dy1, w["w_out"], tb=True, name="mm_dmerged")
    g["w_out"] = _matmul(s["merged"], dy1, ta=True, name="mm_dwout", tk=TK_DEEP)
    dgate, dga, dzb, datt, g["w_a_out"], g["w_b_out"], g["w_c_out"] = _merge_bwd(
        dmerged, s["ga"], s["zb"], s["att"], s["proj"], w["w_a_out"], w["w_b_out"], w["w_c_out"])
    duv, g["gmlp_ws"], dbs, g["gmlp_ln_g"], g["gmlp_ln_b"] = _gmlp_bwd(
        dga, s["proj"], w["gmlp_ln_g"], w["gmlp_ln_b"], w["wsm"], w["wsmt"], w["bsx"])
    g["gmlp_bs"] = dbs[:, :NG].T
    dzc, g["conv_ln_g"], g["conv_ln_b"] = _conv_bwd_ln(dzb, s["zc"], w["conv_ln_g"], w["conv_ln_b"])
    dglu, dcw, g["conv_b"] = _conv_bwd(dzc, s["proj"], w["conv_w"])
    g["conv_w"] = dcw[:KW]
    t = dx3.shape[0]
    dq, dk, dv, dcq, dck = _attn_bwd(s["qkv"], s["ckcol"], datt, s["lser"], _attn_delta(s["att"], datt))
    dq = dq.reshape(t, DM)
    dcum = jnp.pad((dcq.transpose(1, 0, 2).reshape(NG, t) + dck).T, ((0, 0), (0, LANE - NG)))
    df, dbf = _fox_cum_bwd(dcum, s["proj"], w["bfp"])
    g["fox_bf"] = dbf[0, :NG]
    dproj = jnp.concatenate([dgate, duv, dglu, dq, dk, dv, df], axis=1)
    g["w_in_p"] = _matmul(s["h1"], dproj, ta=True, name="mm_dwin", tk=TK_DEEP)
    dh1 = _matmul(dproj, w["w_in_p"], tb=True, name="mm_dh1", tk=D_INP // 2)
    dx, g["mix_pre_g"], dsc1, dsh1 = _norm_bwd(dh1, dx2, s["x"], w["mix_pre_g"], sc1, name="norm_mix_bwd")
    dmod = jnp.concatenate([dsh1, dsc1, dgt1, dsh2, dsc2, dgt2], axis=0)
    return dx, g, dmod


def _position():
    return lax.axis_index("x"), lax.axis_index("y"), lax.axis_index("c")


def _all_gather8(v):
    m_per, n = v.shape

    def body(x_ref, out_ref, send_sems, recv_sems, local_sem):
        x, y, c = _position()
        me, sibling = (x, y, c), (x, y, 1 - c)
        chips = [(1 - x, y), (x, 1 - y), (1 - x, 1 - y)]

        def rows(px, py, pc):
            return out_ref.at[pl.ds((4 * px + 2 * py + pc) * m_per, m_per), :]

        def copy(k, block, to, src=None):
            return pltpu.make_async_remote_copy(
                src_ref=rows(*block) if src is None else src, dst_ref=rows(*block), send_sem=send_sems.at[k],
                recv_sem=recv_sems.at[k], device_id=to, device_id_type=MESH)

        mine = pltpu.make_async_copy(x_ref, rows(*me), local_sem)
        mine.start()
        first = [copy(0, me, sibling, src=x_ref)]
        first += [copy(1 + j, me, (*chip, c), src=x_ref) for j, chip in enumerate(chips)]
        for cp in first:
            cp.start()
        passed = [copy(4 + j, (*chip, c), sibling) for j, chip in enumerate(chips)]
        for j, chip in enumerate(chips):
            copy(1 + j, (*chip, c), me).wait_recv()
            passed[j].start()
        copy(0, sibling, me).wait_recv()
        for j, chip in enumerate(chips):
            copy(4 + j, (*chip, 1 - c), me).wait_recv()
        for cp in first + passed:
            cp.wait_send()
        mine.wait()

    out = pl.pallas_call(
        body, name="all_gather8", out_shape=jax.ShapeDtypeStruct((N_DEV * m_per, n), v.dtype),
        in_specs=[pl.BlockSpec(memory_space=pltpu.VMEM)], out_specs=pl.BlockSpec(memory_space=pltpu.VMEM),
        scratch_shapes=[pltpu.SemaphoreType.DMA((7,)), pltpu.SemaphoreType.DMA((7,)), pltpu.SemaphoreType.DMA],
        compiler_params=pltpu.CompilerParams(vmem_limit_bytes=VMEM_LIMIT),
    )(v)
    return out.reshape(N_DEV, m_per, n)


def _half(c, rows):
    return pl.ds(c * (rows // 2), rows // 2)


HBM = pl.BlockSpec(memory_space=pltpu.HBM)
SEM = pl.BlockSpec(memory_space=pltpu.SEMAPHORE)
EFFECT = pltpu.SideEffectType.DATAFLOW_SIDE_EFFECTING


_COPIES_PER_ARRAY = {"gather": 3, "scatter": 3, "swap": N_CHIPS}


def _ici_copies(kind, src_refs, land_refs, send_sems, recv_sems, arriving):
    x, y, c = _position()
    chips = [(1 - x, y), (x, 1 - y), (1 - x, 1 - y)]
    nper = _COPIES_PER_ARRAY[kind]
    copies = []
    for wi, (src, land) in enumerate(zip(src_refs, land_refs)):
        for k in range(nper):
            if kind == "swap":
                peer = (x, y, 1 - c)
                s_win, dst = src.at[k, _half(1 - c, src.shape[1])], land.at[k]
            else:
                px, py = chips[k]
                peer = (px, py, c)
                if kind == "gather":
                    rows = src.shape[0]
                    s_win = src.at[_half(c, rows)]
                    dst = land.at[2 * px + py if arriving else 2 * x + y, _half(c, rows)]
                else:
                    s_win, dst = src.at[2 * px + py], land.at[k]
            copies.append(pltpu.make_async_remote_copy(
                src_ref=s_win, dst_ref=dst, send_sem=send_sems.at[wi * nper + k], recv_sem=recv_sems.at[wi * nper + k],
                device_id=peer, device_id_type=MESH))
    return copies


def _ici_start(kind, srcs, land_shapes, after, *, name):
    nw = len(srcs)

    def body(*refs):
        src_refs, land_refs = refs[:nw], refs[nw:2 * nw]
        send_sems, recv_sems = refs[2 * nw + 1:2 * nw + 3]
        token = refs[-1]
        for cp in _ici_copies(kind, src_refs, land_refs, send_sems, recv_sems, False):
            cp.start()
        token[...] = jnp.zeros_like(token)

    lands = [pltpu.with_memory_space_constraint(lax.empty(shp, s.dtype), pltpu.HBM) for shp, s in zip(land_shapes, srcs)]
    outs = pl.pallas_call(
        body, name=name,
        out_shape=(pltpu.SemaphoreType.DMA((_COPIES_PER_ARRAY[kind] * nw,)), pltpu.SemaphoreType.DMA((_COPIES_PER_ARRAY[kind] * nw,)),
                   *[pltpu.HBM(s.shape, s.dtype) for s in srcs], *[pltpu.HBM(shp, s.dtype) for shp, s in zip(land_shapes, srcs)],
                   jax.ShapeDtypeStruct((8, LANE), F32)),
        in_specs=[HBM] * (2 * nw) + [ANY], out_specs=(SEM, SEM, *[HBM] * (2 * nw), pl.BlockSpec(memory_space=pltpu.VMEM)),
        input_output_aliases={i: 2 + i for i in range(2 * nw)},
        compiler_params=pltpu.CompilerParams(has_side_effects=EFFECT),
    )(*[pltpu.with_memory_space_constraint(s, pltpu.HBM) for s in srcs], *lands, after)
    return outs[0], outs[1], outs[2:2 + nw], outs[2 + nw:2 + 2 * nw], outs[-1]


def _ici_wait(kind, send_sems, recv_sems, srcs, lands, after, *, name):
    nw = len(srcs)

    def body(*refs):
        src_refs, land_refs = refs[:nw], refs[nw:2 * nw]
        s_sems, r_sems = refs[2 * nw:2 * nw + 2]
        for cp in _ici_copies(kind, src_refs, land_refs, s_sems, r_sems, False):
            cp.wait_send()
        for cp in _ici_copies(kind, src_refs, land_refs, s_sems, r_sems, True):
            cp.wait_recv()

    outs = pl.pallas_call(
        body, name=name,
        out_shape=(*[pltpu.HBM(s.shape, s.dtype) for s in srcs], *[pltpu.HBM(a.shape, a.dtype) for a in lands]),
        in_specs=[HBM] * (2 * nw) + [SEM, SEM, ANY], out_specs=tuple([HBM] * (2 * nw)),
        input_output_aliases={i: i for i in range(2 * nw)},
        compiler_params=pltpu.CompilerParams(has_side_effects=EFFECT),
    )(*srcs, *lands, send_sems, recv_sems, after)
    return outs[:nw], outs[nw:]


def _ag_d2d(lands, *, name):
    nw = len(lands)

    def body(*refs):
        out_refs = refs[nw:2 * nw]
        send_sems, recv_sems = refs[2 * nw:]
        x, y, c = _position()
        chips = [(1 - x, y), (x, 1 - y), (1 - x, 1 - y)]
        copies = []
        for wi in range(nw):
            rows = out_refs[wi].shape[1]
            for k, (px, py) in enumerate(chips):
                win = out_refs[wi].at[2 * px + py, _half(c, rows)]
                copies.append(pltpu.make_async_remote_copy(
                    src_ref=win, dst_ref=win, send_sem=send_sems.at[wi * 3 + k], recv_sem=recv_sems.at[wi * 3 + k],
                    device_id=(x, y, 1 - c), device_id_type=MESH))
        for cp in copies:
            cp.start()
        for wi in range(nw):
            rows = out_refs[wi].shape[1]
            for k, (px, py) in enumerate(chips):
                win = out_refs[wi].at[2 * px + py, _half(1 - c, rows)]
                pltpu.make_async_remote_copy(
                    src_ref=win, dst_ref=win, send_sem=send_sems.at[wi * 3 + k], recv_sem=recv_sems.at[wi * 3 + k],
                    device_id=(x, y, 1 - c), device_id_type=MESH).wait_recv()
        for cp in copies:
            cp.wait_send()

    return pl.pallas_call(
        body, name=name, out_shape=[jax.ShapeDtypeStruct(a.shape, a.dtype) for a in lands],
        in_specs=[ANY] * nw, out_specs=[ANY] * nw, input_output_aliases={i: i for i in range(nw)},
        scratch_shapes=[pltpu.SemaphoreType.DMA((3 * nw,)), pltpu.SemaphoreType.DMA((3 * nw,))],
    )(*lands)


def _swap_reduced(reds):
    nw = len(reds)

    def body(*refs):
        r_refs, out_refs = refs[:nw], refs[nw:2 * nw]
        send_sems, recv_sems = refs[2 * nw:]
        x, y, c = _position()
        copies = [pltpu.make_async_remote_copy(
            src_ref=r_refs[wi], dst_ref=out_refs[wi], send_sem=send_sems.at[wi], recv_sem=recv_sems.at[wi],
            device_id=(x, y, 1 - c), device_id_type=MESH) for wi in range(nw)]
        for cp in copies:
            cp.start()
        for cp in copies:
            cp.wait()

    return pl.pallas_call(
        body, name="rs_swap_reduced", out_shape=[jax.ShapeDtypeStruct(r.shape, r.dtype) for r in reds],
        in_specs=[ANY] * nw, out_specs=[ANY] * nw,
        scratch_shapes=[pltpu.SemaphoreType.DMA((nw,)), pltpu.SemaphoreType.DMA((nw,))],
    )(*reds)


def _row_tile(rows, cols):
    tr = rows
    while tr * cols * 4 > (2 << 20) and tr % 32 == 0:
        tr //= 2
    return tr


def _add_own_half(g, recv, c, *, name):
    nj, h, w = recv.shape
    tr = _row_tile(h, w)
    nb = h // tr

    def body(c_ref, g_ref, r_ref, o_ref, ob_ref):
        sm = g_ref[0] + r_ref[0]
        o_ref[0] = sm
        ob_ref[0] = sm.astype(BF16)

    spec = pl.BlockSpec((1, tr, w), lambda j, i, cc: (j, i, 0))
    gs = pltpu.PrefetchScalarGridSpec(
        num_scalar_prefetch=1, grid=(nj, nb),
        in_specs=[pl.BlockSpec((1, tr, w), lambda j, i, cc: (j, cc[0] * nb + i, 0)), spec],
        out_specs=[spec, spec])
    return pl.pallas_call(body, name=name, grid_spec=gs,
                          out_shape=[jax.ShapeDtypeStruct((nj, h, w), F32), jax.ShapeDtypeStruct((nj, h, w), BF16)],
                          compiler_params=_cparams(("parallel", "parallel")))(jnp.reshape(c, (1,)).astype(I32), g, recv)


def _add_own_chip(sp, recv, j, *, name):
    _, r, w = sp.shape
    tr = _row_tile(r, w)

    def body(j_ref, s_ref, r_ref, o_ref):
        o_ref[...] = ((s_ref[0] + r_ref[0].astype(F32)) + r_ref[1].astype(F32)) + r_ref[2].astype(F32)

    gs = pltpu.PrefetchScalarGridSpec(
        num_scalar_prefetch=1, grid=(r // tr,),
        in_specs=[pl.BlockSpec((1, tr, w), lambda i, jj: (jj[0], i, 0)), pl.BlockSpec((3, tr, w), lambda i, jj: (0, i, 0))],
        out_specs=pl.BlockSpec((tr, w), lambda i, jj: (i, 0)))
    return pl.pallas_call(body, name=name, grid_spec=gs, out_shape=jax.ShapeDtypeStruct((r, w), F32),
                          compiler_params=_cparams(("parallel",)))(jnp.reshape(j, (1,)).astype(I32), sp, recv)


def _sum8(v):
    _, m, n = v.shape

    def body(v_ref, o_ref):
        acc = v_ref[0]
        for k in range(1, N_DEV):
            acc = acc + v_ref[k]
        o_ref[...] = acc

    return pl.pallas_call(body, name="sum8", grid=(1,), in_specs=[pl.BlockSpec((N_DEV, m, n), lambda i: (0, 0, 0))],
                          out_specs=pl.BlockSpec((m, n), lambda i: (0, 0)), out_shape=jax.ShapeDtypeStruct((m, n), F32),
                          compiler_params=_cparams(("arbitrary",)))(v)


def _ada_mod(c_all, ada_w, ada_b_loc, *, tn=512):
    nl, _, ncol = ada_w.shape

    def body(c_ref, w_ref, b_ref, o_ref):
        cv = c_ref[...]
        ca = (cv * _sigmoid(cv)).astype(BF16)
        o_ref[0] = _dot(ca, w_ref[0].astype(BF16)) + b_ref[0]

    return pl.pallas_call(
        body, name="ada_mod", grid=(nl, ncol // tn),
        in_specs=[pl.BlockSpec((N_DEV, D), lambda l, j: (0, 0)), pl.BlockSpec((1, D, tn), lambda l, j: (l, 0, j)),
                  pl.BlockSpec((1, 1, tn), lambda l, j: (l, 0, j))],
        out_specs=pl.BlockSpec((1, N_DEV, tn), lambda l, j: (l, 0, j)),
        out_shape=jax.ShapeDtypeStruct((nl, N_DEV, ncol), F32),
        compiler_params=_cparams(("parallel", "parallel")))(c_all, ada_w, ada_b_loc)


def _ada_grad(c_pad, dmod_pad, *, tn=512):
    nl, nb, ncol = dmod_pad.shape

    def body(c_ref, d_ref, o_ref):
        cv = c_ref[...]
        ca = (cv * _sigmoid(cv)).astype(BF16)
        o_ref[0] = _dot(ca, d_ref[0].astype(BF16), TN)

    return pl.pallas_call(
        body, name="ada_grad", grid=(nl, ncol // tn),
        in_specs=[pl.BlockSpec((nb, D), lambda l, j: (0, 0)), pl.BlockSpec((1, nb, tn), lambda l, j: (l, 0, j))],
        out_specs=pl.BlockSpec((1, D, tn), lambda l, j: (l, 0, j)),
        out_shape=jax.ShapeDtypeStruct((nl, D, ncol), F32),
        compiler_params=_cparams(("parallel", "parallel")))(c_pad, dmod_pad)


def _adamw(w, g, m, v, *, name):
    shape = w.shape
    if w.ndim == 3:
        lead, rows, cols = shape
    else:
        lead, (rows, cols) = 1, shape
    w3, g3, m3, v3 = (a.reshape(lead, rows, cols) for a in (w, g, m, v))
    tr = rows
    if rows * cols * 4 > (2 << 20):
        tr = next(cand for cand in (256, 128, 64, 8) if rows % cand == 0)
    c1 = 1.0 - ADAM_B1 ** ADAM_STEP
    c2 = 1.0 - ADAM_B2 ** ADAM_STEP

    def body(w_ref, g_ref, m_ref, v_ref, d_ref, nm_ref, nv_ref):
        gv = g_ref[...]
        nm = ADAM_B1 * m_ref[...] + (1.0 - ADAM_B1) * gv
        nv = ADAM_B2 * v_ref[...] + (1.0 - ADAM_B2) * (gv * gv)
        nm_ref[...] = nm
        nv_ref[...] = nv
        d_ref[...] = -ADAM_LR * ((nm / c1) / (jnp.sqrt(nv / c2) + ADAM_EPS) + ADAM_WD * w_ref[...])

    spec = pl.BlockSpec((1, tr, cols), lambda l, i: (l, i, 0))
    outs = pl.pallas_call(
        body, name=name, grid=(lead, rows // tr), in_specs=[spec] * 4, out_specs=[spec] * 3,
        out_shape=[jax.ShapeDtypeStruct((lead, rows, cols), F32)] * 3,
        compiler_params=_cparams(("parallel", "parallel")))(w3, g3, m3, v3)
    return tuple(o.reshape(shape) for o in outs)


def _adamw_layer(w, g_own, g_sib, m, v, c, layer, prev, after, *, name):
    _, rows, cols = w.shape
    tr = _row_tile(rows // 2, cols)
    nbh = rows // 2 // tr
    c1 = 1.0 - ADAM_B1 ** ADAM_STEP
    c2 = 1.0 - ADAM_B2 ** ADAM_STEP
    n_prev = 0 if prev is None else 4

    def body(c_ref, w_ref, o_ref, s_ref, m_ref, v_ref, *rest):
        g_ref, d_ref, nm_ref, nv_ref = rest[n_prev + 1:]
        gv = jnp.where(pl.program_id(0) // nbh == c_ref[0], o_ref[...], s_ref[...])
        nm = ADAM_B1 * m_ref[0] + (1.0 - ADAM_B1) * gv
        nv = ADAM_B2 * v_ref[0] + (1.0 - ADAM_B2) * (gv * gv)
        g_ref[0] = gv
        nm_ref[0] = nm
        nv_ref[0] = nv
        d_ref[0] = -ADAM_LR * ((nm / c1) / (jnp.sqrt(nv / c2) + ADAM_EPS) + ADAM_WD * w_ref[0])

    def source(own):
        return pl.BlockSpec((tr, cols), lambda i, cc: (jnp.where((i // nbh == cc[0]) == own, i % nbh, 0), 0))

    spec = pl.BlockSpec((1, tr, cols), lambda i, cc: (layer, i, 0))
    gs = pltpu.PrefetchScalarGridSpec(
        num_scalar_prefetch=1, grid=(2 * nbh,),
        in_specs=[spec, source(True), source(False), spec, spec] + [ANY] * (n_prev + 1), out_specs=[spec] * 4)
    return pl.pallas_call(
        body, name=name, grid_spec=gs, out_shape=[jax.ShapeDtypeStruct(w.shape, F32)] * 4,
        input_output_aliases={6 + k: k for k in range(n_prev)},
        compiler_params=_cparams(("parallel",)))(
            jnp.reshape(c, (1,)).astype(I32), w, g_own, g_sib, m, v, *(prev or ()), after)


SMALL = (("mix_pre_g", (2, D)), ("mix_post_g", (2, D)), ("mlp_pre_g", (2, D)), ("mlp_post_g", (2, D)),
         ("gmlp_ln_g", (2, DM)), ("gmlp_ln_b", (2, DM)), ("gmlp_ws", (2, NG, CH, CH)), ("gmlp_bs", (2, NG, CH)),
         ("conv_b", (2, DM)), ("conv_ln_g", (2, DM)), ("conv_ln_b", (2, DM)), ("fox_bf", (2, NG)))


def _pack_rows(arrays):
    parts = []
    for a in arrays:
        last = a.shape[-1]
        r = a.astype(F32).reshape(-1, LANE) if last % LANE == 0 else jnp.pad(a.astype(F32).reshape(-1, last), ((0, 0), (0, LANE - last)))
        pad = (-r.shape[0]) % 8
        parts.append(jnp.pad(r, ((0, pad), (0, 0))) if pad else r)
    return jnp.concatenate(parts, axis=0)


def _unpack_rows(buf, shapes):
    out, off = [], 0
    for shp in shapes:
        size = 1
        for d in shp:
            size *= d
        last = shp[-1]
        rows = size // (LANE if last % LANE == 0 else last)
        seg = buf[off:off + rows]
        out.append(seg.reshape(shp) if last % LANE == 0 else seg[:, :last].reshape(shp))
        off += rows + (-rows) % 8
    return out


def _assemble_w_in(w_in_full):
    n_early = C_F - C_UV
    uv_glu_qkv = w_in_full[:, :n_early]
    f = w_in_full[:, n_early:n_early + NG]
    gate = w_in_full[:, n_early + NG:]
    fpad = jnp.zeros((D, D_INP - C_F - NG), w_in_full.dtype)
    return jnp.concatenate([gate, uv_glu_qkv, f, fpad], axis=1)


def _disassemble_w_in(g_p):
    return jnp.concatenate([g_p[:, C_UV:C_F], g_p[:, C_F:C_F + NG], g_p[:, :C_UV]], axis=1)


def kernel(x, c, ada_w, ada_b, mix_pre_g, mix_post_g, mlp_pre_g, mlp_post_g, w_in, gmlp_ln_g, gmlp_ln_b, gmlp_ws, gmlp_bs, w_a_out, conv_w, conv_b, conv_ln_g, conv_ln_b, w_b_out, fox_bf, w_c_out, w_out, mlp_w1, mlp_w2, loss_target, m_ada_w, m_ada_b, m_mix_pre_g, m_mix_post_g, m_mlp_pre_g, m_mlp_post_g, m_w_in, m_gmlp_ln_g, m_gmlp_ln_b, m_gmlp_ws, m_gmlp_bs, m_w_a_out, m_conv_w, m_conv_b, m_conv_ln_g, m_conv_ln_b, m_w_b_out, m_fox_bf, m_w_c_out, m_w_out, m_mlp_w1, m_mlp_w2, v_ada_w, v_ada_b, v_mix_pre_g, v_mix_post_g, v_mlp_pre_g, v_mlp_post_g, v_w_in, v_gmlp_ln_g, v_gmlp_ln_b, v_gmlp_ws, v_gmlp_bs, v_w_a_out, v_conv_w, v_conv_b, v_conv_ln_g, v_conv_ln_b, v_w_b_out, v_fox_bf, v_w_c_out, v_w_out, v_mlp_w1, v_mlp_w2):
    weights = dict(ada_w=ada_w, ada_b=ada_b, mix_pre_g=mix_pre_g, mix_post_g=mix_post_g, mlp_pre_g=mlp_pre_g,
                   mlp_post_g=mlp_post_g, w_in=w_in, gmlp_ln_g=gmlp_ln_g, gmlp_ln_b=gmlp_ln_b, gmlp_ws=gmlp_ws,
                   gmlp_bs=gmlp_bs, w_a_out=w_a_out, conv_w=conv_w, conv_b=conv_b, conv_ln_g=conv_ln_g,
                   conv_ln_b=conv_ln_b, w_b_out=w_b_out, fox_bf=fox_bf, w_c_out=w_c_out, w_out=w_out, mlp_w1=mlp_w1,
                   mlp_w2=mlp_w2)
    mom_m = dict(ada_w=m_ada_w, ada_b=m_ada_b, mix_pre_g=m_mix_pre_g, mix_post_g=m_mix_post_g, mlp_pre_g=m_mlp_pre_g,
                 mlp_post_g=m_mlp_post_g, w_in=m_w_in, gmlp_ln_g=m_gmlp_ln_g, gmlp_ln_b=m_gmlp_ln_b, gmlp_ws=m_gmlp_ws,
                 gmlp_bs=m_gmlp_bs, w_a_out=m_w_a_out, conv_w=m_conv_w, conv_b=m_conv_b, conv_ln_g=m_conv_ln_g,
                 conv_ln_b=m_conv_ln_b, w_b_out=m_w_b_out, fox_bf=m_fox_bf, w_c_out=m_w_c_out, w_out=m_w_out,
                 mlp_w1=m_mlp_w1, mlp_w2=m_mlp_w2)
    mom_v = dict(ada_w=v_ada_w, ada_b=v_ada_b, mix_pre_g=v_mix_pre_g, mix_post_g=v_mix_post_g, mlp_pre_g=v_mlp_pre_g,
                 mlp_post_g=v_mlp_post_g, w_in=v_w_in, gmlp_ln_g=v_gmlp_ln_g, gmlp_ln_b=v_gmlp_ln_b, gmlp_ws=v_gmlp_ws,
                 gmlp_bs=v_gmlp_bs, w_a_out=v_w_a_out, conv_w=v_conv_w, conv_b=v_conv_b, conv_ln_g=v_conv_ln_g,
                 conv_ln_b=v_conv_ln_b, w_b_out=v_w_b_out, fox_bf=v_fox_bf, w_c_out=v_w_c_out, w_out=v_w_out,
                 mlp_w1=v_mlp_w1, mlp_w2=v_mlp_w2)
    order = list(weights)
    px, py, pc = _position()
    chip = 2 * px + py
    dev = 2 * chip + pc
    depth = ada_w.shape[0]
    t = x.shape[1]
    xl = x.reshape(t, D)
    tgt = loss_target.reshape(t, D)

    big_names = [b[0] for b in BIG]
    shards = [[weights[n][l].astype(BF16) for n in big_names] for l in range(depth)]
    land_shapes = [(N_CHIPS,) + sh.shape for sh in shards[0]]
    ag0a = _ici_start("gather", shards[0][:1], land_shapes[:1], c, name="ag0a_start")

    small_in = _pack_rows([c, conv_w]) + ag0a[4][0, 0]
    gathered = _all_gather8(small_in)
    c_all = gathered[:, :D // LANE, :].reshape(N_DEV, D)
    cw_rows = depth * KW * LANE // LANE
    conv_w_full = jnp.concatenate(
        [gathered[2 * j, D // LANE:D // LANE + cw_rows, :].reshape(depth, KW, LANE) for j in range(N_CHIPS)], axis=2)

    ncol = ada_w.shape[2]
    ada_b_loc = lax.dynamic_slice_in_dim(ada_b, chip * ncol, ncol, axis=1).reshape(depth, 1, ncol)
    mod_sh = _ada_mod(c_all, ada_w, ada_b_loc)
    mod_g = _all_gather8(mod_sh.reshape(-1, LANE)).reshape(N_DEV, depth, N_DEV, ncol)
    mod_all = jnp.concatenate([mod_g[2 * j] for j in range(N_CHIPS)], axis=2)
    mod_mine = lax.dynamic_index_in_dim(mod_all, dev, axis=1, keepdims=False)

    mods = [mod_mine[l].reshape(NMOD, D) for l in range(depth)]

    def gathered_weights(idx, waited):
        own, lands = waited
        lands = _ag_d2d(lands, name="ag_d2d")
        w = {}
        for i, sh, g in zip(idx, own, lands):
            n, r, cdim, ax = BIG[i]
            g = lax.dynamic_update_slice(g, sh[None], (chip, 0, 0))
            if n == "w_in":
                w["w_in_p"] = _assemble_w_in(g.transpose(1, 0, 2).reshape(r, N_CHIPS * cdim))
            elif n == "mlp_w1":
                w[n] = g
            elif ax == 1:
                w[n] = g.transpose(1, 0, 2).reshape(r, N_CHIPS * cdim)
            else:
                w[n] = g.reshape(N_CHIPS * r, cdim)
        return w

    def local_weights(l):
        w = {}
        for n in ("mix_pre_g", "mix_post_g", "mlp_pre_g", "mlp_post_g", "gmlp_ln_g", "gmlp_ln_b", "conv_b", "conv_ln_g",
                  "conv_ln_b"):
            w[n] = weights[n][l:l + 1]
        tril = jnp.tril(jnp.ones((CH, CH), F32))
        wsm = gmlp_ws[l] * tril
        w["wsm"] = wsm.astype(BF16)
        w["wsmt"] = jnp.swapaxes(wsm, 1, 2).astype(BF16)
        w["bsx"] = jnp.repeat(gmlp_bs[l].T, HD, axis=1)
        w["conv_w"] = conv_w_full[l]
        w["bfp"] = jnp.pad(fox_bf[l], (0, LANE - NG)).reshape(1, LANE)
        return w

    def slabs(gfull, n, r, cdim, ax):
        if n == "mlp_w1":
            return gfull
        if ax == 1:
            return gfull.reshape(gfull.shape[0], N_CHIPS, cdim).transpose(1, 0, 2)
        return gfull.reshape(N_CHIPS, r, cdim)

    def swap_start(g, after, name):
        g["w_in"] = _disassemble_w_in(g.pop("w_in_p"))
        gs = [slabs(g[n], n, r, cdim, ax) for n, r, cdim, ax in BIG]
        return _ici_start("swap", gs, [(N_CHIPS, a.shape[1] // 2, a.shape[2]) for a in gs], after, name=name)

    def chip_sums(sw, after, name):
        gs, from_sibling = _ici_wait("swap", sw[0], sw[1], sw[2], sw[3], after, name=name)
        return [_add_own_half(a, rv, pc, name="rs_add_half_" + n) for a, rv, n in zip(gs, from_sibling, big_names)]

    def reduce_rest(sums, from_chips):
        red = [_add_own_chip(sf, rv, chip, name="rs_add_chip_" + n) for (sf, _), rv, n in zip(sums, from_chips, big_names)]
        return red, _swap_reduced(red)

    assert depth == 2
    ga, gb = [0], list(range(1, len(BIG)))

    def pick(seq, idx):
        return [seq[i] for i in idx]

    lands0a = _ici_wait("gather", ag0a[0], ag0a[1], ag0a[2], ag0a[3], mod_mine, name="ag0a_wait")
    ag0b = _ici_start("gather", pick(shards[0], gb), pick(land_shapes, gb), lands0a[1][0], name="ag0b_start")
    ag1 = _ici_start("gather", shards[1], land_shapes, ag0b[4], name="ag1_start")
    mod0 = mods[0] + ag1[4][0, 0]
    layers, saved = [None] * depth, [None] * depth

    def late0(att):
        return gathered_weights(gb, _ici_wait("gather", ag0b[0], ag0b[1], ag0b[2], ag0b[3], att, name="ag0b_wait"))

    xs, saved[0], layers[0] = _layer_fwd(xl, mod0, {**local_weights(0), **gathered_weights(ga, lands0a)}, late0)
    lands1 = _ici_wait("gather", ag1[0], ag1[1], ag1[2], ag1[3], xs, name="ag1_wait")
    xs, saved[1], layers[1] = _layer_fwd(xs, mods[1], {**local_weights(1), **gathered_weights(ga + gb, lands1)})
    loss_local, dx = _loss_and_grad(xs, tgt)
    loss = lax.psum(loss_local, ("x", "y", "c"))
    grads, dmods = [None] * depth, [None] * depth
    dx, grads[1], dmods[1] = _layer_bwd(dx, mods[1], layers[1], saved[1])
    sw1 = swap_start(grads[1], dx, "sw1_start")
    rs_l1 = {}

    def mid0(dx2):
        rs_l1["sums"] = chip_sums(sw1, dx2, "sw1_wait")
        sbf1 = [sb for _, sb in rs_l1["sums"]]
        rs_l1["rs"] = _ici_start("scatter", sbf1, [(3,) + sb.shape[1:] for sb in sbf1], dx2, name="rs1_start")
        return rs_l1["rs"][4][0:1, 0:1]

    dx, grads[0], dmods[0] = _layer_bwd(dx, mod0 + sw1[4][0, 0], layers[0], saved[0], mid0)
    grad_x = dx.reshape(x.shape)
    sums1, rs1 = rs_l1["sums"], rs_l1["rs"]
    slot_shapes = [(3,) + sb.shape[1:] for _, sb in sums1]
    sw0 = swap_start(grads[0], dx, "sw0_start")
    g_out = {}

    small_names = [n for n, _ in SMALL]
    small_list = [jnp.stack(dmods)] + [jnp.stack([grads[l][n] for l in range(depth)]) for n in small_names]
    small_list.append(jnp.stack([grads[l]["conv_w"] for l in range(depth)]))
    small_shapes = [(depth, NMOD * D)] + [shp for _, shp in SMALL] + [(depth, KW, DM)]
    small_all = _all_gather8(_pack_rows(small_list) + sw0[4][0, 0])
    sums0 = chip_sums(sw0, small_all, "sw0_wait")
    rs0 = _ici_start("scatter", [sb for _, sb in sums0], slot_shapes, small_all, name="rs0_start")
    tok0 = rs0[4][0, 0]
    red1 = reduce_rest(sums1, _ici_wait("scatter", rs1[0], rs1[1], rs1[2], rs1[3], rs0[4], name="rs1_wait")[1])
    c_all = c_all + tok0
    small_sum = _unpack_rows(_sum8(small_all) + tok0, small_shapes)
    g_out["ada_b"] = small_sum[0]
    for n, gs in zip(small_names, small_sum[1:-1]):
        g_out[n] = gs
    g_out["conv_w"] = lax.dynamic_slice_in_dim(small_sum[-1], chip * LANE, LANE, axis=2)
    dmod_all = small_all[:, :depth * NMOD * D // LANE, :].reshape(N_DEV, depth, NMOD * D)
    dmod_loc = lax.dynamic_slice_in_dim(dmod_all, chip * ncol, ncol, axis=2).transpose(1, 0, 2)
    g_out["ada_w"] = _ada_grad(jnp.pad(c_all, ((0, 8), (0, 0))), jnp.pad(dmod_loc, ((0, 0), (0, 8), (0, 0))))

    delta, new_m, new_v = {}, {}, {}
    delta["ada_w"], new_m["ada_w"], new_v["ada_w"] = _adamw(ada_w, g_out["ada_w"], m_ada_w, v_ada_w, name="adamw_ada_w")
    ws_rows = (depth * NG * CH, CH)
    ws_out = _adamw(*[d["gmlp_ws"].reshape(ws_rows) for d in (weights, g_out, mom_m, mom_v)], name="adamw_gmlp_ws")
    delta["gmlp_ws"], new_m["gmlp_ws"], new_v["gmlp_ws"] = (a.reshape(gmlp_ws.shape) for a in ws_out)
    small_params = ["ada_b"] + [n for n in small_names if n != "gmlp_ws"] + ["conv_w"]
    packs = [_pack_rows([d[n] for n in small_params]) for d in (weights, g_out, mom_m, mom_v)]
    outs = _adamw(*packs, name="adamw_small")
    shapes = [weights[n].shape for n in small_params]
    for dst, buf in zip((delta, new_m, new_v), outs):
        for n, a in zip(small_params, _unpack_rows(buf, shapes)):
            dst[n] = a
    half = {n: _adamw_layer(weights[n], red1[0][wi], red1[1][wi], mom_m[n], mom_v[n], pc, 1, None, rs0[4],
                            name="adamw1_" + n) for wi, n in enumerate(big_names)}
    done = jnp.stack([delta["ada_w"][0, 0, 0], outs[0][0, 0]] + [half[n][1][1, 0, 0] for n in big_names])
    red0 = reduce_rest(sums0, _ici_wait("scatter", rs0[0], rs0[1], rs0[2], rs0[3], done, name="rs0_wait")[1])
    for wi, n in enumerate(big_names):
        g_out[n], delta[n], new_m[n], new_v[n] = _adamw_layer(
            weights[n], red0[0][wi], red0[1][wi], mom_m[n], mom_v[n], pc, 0, half[n], rs0[4], name="adamw0_" + n)

    return (loss, grad_x, *[g_out[n] for n in order], *[delta[n] for n in order], *[new_m[n] for n in order],
            *[new_v[n] for n in order])
```
